```python
import jax, jax.numpy as jnp
from jax import lax
import numpy as np

D_MODEL = 2048
BATCH = 8
SEQ = 4096
DEPTH = 2

HEAD_DIM = 64
ATTN_WIDTH = D_MODEL // 2
ATTN_HEADS = ATTN_WIDTH // HEAD_DIM
DILATED_BRANCHES = ((128, 1), (512, 4), (2048, 16))
ATTN_BLOCK = 128

SSD_WIDTH = D_MODEL // 2
SSD_HEAD_DIM = 64
SSD_HEADS = SSD_WIDTH // SSD_HEAD_DIM
SSD_GROUPS = 2
SSD_STATE = 128
SSD_CONV = 4
SSD_CHUNK = 128
CONV_CH = SSD_WIDTH + 2 * SSD_GROUPS * SSD_STATE

MIX_WIDTH = ATTN_WIDTH + SSD_WIDTH
IN_PROJ = 3 * ATTN_WIDTH + SSD_WIDTH + CONV_CH + SSD_HEADS
D_FF = 4 * D_MODEL
NORM_EPS = 1e-5

kernel_name = "hybrid_ssd_dilated_alibi_block"


def alibi_slopes(n_heads):
    return jnp.asarray(2.0 ** (-8.0 * (np.arange(n_heads) + 1) / n_heads), dtype=jnp.float32)


def rmsnorm(x, g):
    x32 = x.astype(jnp.float32)
    y = x32 * lax.rsqrt(jnp.mean(x32 * x32, axis=-1, keepdims=True) + NORM_EPS)
    return (y * g.astype(jnp.float32)).astype(x.dtype)


def dilated_branch(q, k, v, slopes, window, dilation):
    b, s, h, dh = q.shape
    L = s // dilation
    nb = -(-L // ATTN_BLOCK)
    Lp = nb * ATTN_BLOCK
    steps = window // dilation

    def to_blocks(t):
        t = t.astype(jnp.float32).reshape(b, L, dilation, h, dh).transpose(0, 2, 3, 1, 4)
        t = jnp.pad(t, ((0, 0), (0, 0), (0, 0), (0, Lp - L), (0, 0)))
        return t.reshape(b, dilation, h, nb, ATTN_BLOCK, dh)

    def with_prev(t):
        prev = jnp.pad(t, ((0, 0), (0, 0), (0, 0), (1, 0), (0, 0), (0, 0)))[:, :, :, :-1]
        return jnp.concatenate([prev, t], axis=4)

    qb, kb, vb = to_blocks(q), to_blocks(k), to_blocks(v)
    kc, vc = with_prev(kb), with_prev(vb)
    scores = jnp.einsum('brhnid,brhnjd->brhnij', qb, kc) * (dh ** -0.5)

    i = jnp.arange(ATTN_BLOCK)[:, None]
    j = jnp.arange(2 * ATTN_BLOCK)[None, :]
    delta = i - j + ATTN_BLOCK
    key_pos = jnp.arange(nb)[:, None, None] * ATTN_BLOCK - ATTN_BLOCK + j
    valid = (delta >= 0) & (delta <= steps) & (key_pos >= 0)
    bias = -slopes[:, None, None, None] * (delta * dilation).astype(jnp.float32)
    scores = jnp.where(valid, scores + bias, -jnp.inf)

    m = jnp.max(scores, axis=-1, keepdims=True)
    p = jnp.exp(scores - m)
    den = jnp.sum(p, axis=-1)
    out = jnp.einsum('brhnij,brhnjd->brhnid', p, vc) / den[..., None]
    lse = m[..., 0] + jnp.log(den)

    out = out.reshape(b, dilation, h, Lp, dh)[:, :, :, :L].transpose(0, 3, 1, 2, 4).reshape(b, s, h, dh)
    lse = lse.reshape(b, dilation, h, Lp)[:, :, :, :L].transpose(0, 3, 1, 2).reshape(b, s, h)
    return out, lse


def dilated_attention(q, k, v, slopes):
    outs, lses = [], []
    for window, dilation in DILATED_BRANCHES:
        o, l = dilated_branch(q, k, v, slopes, window, dilation)
        outs.append(o)
        lses.append(l)
    w = jax.nn.softmax(jnp.stack(lses, axis=0), axis=0)
    out = jnp.sum(w[..., None] * jnp.stack(outs, axis=0), axis=0)
    return out.astype(q.dtype)


def causal_depthwise_conv(u, w, bias):
    out = lax.conv_general_dilated(
        u, w[:, None, :], window_strides=(1,), padding=((SSD_CONV - 1, 0),),
        dimension_numbers=('NWC', 'WIO', 'NWC'), feature_group_count=u.shape[-1])
    return out + bias


def segsum_exp(a):
    cum = jnp.cumsum(a, axis=-1)
    diff = cum[..., :, None] - cum[..., None, :]
    T = a.shape[-1]
    mask = jnp.tril(jnp.ones((T, T), dtype=bool))
    return jnp.exp(jnp.where(mask, diff, -jnp.inf))


def ssd_scan(x, dt, a, b_in, c_in):
    bs, s, h, p = x.shape
    g, n, Q = SSD_GROUPS, SSD_STATE, SSD_CHUNK
    e = h // g
    nc = s // Q
    x = x.astype(jnp.float32)
    X = (x * dt[..., None]).reshape(bs, nc, Q, g, e, p)
    dA = (dt * a).reshape(bs, nc, Q, g, e).transpose(0, 3, 4, 1, 2)
    Bc = b_in.astype(jnp.float32).reshape(bs, nc, Q, g, n)
    Cc = c_in.astype(jnp.float32).reshape(bs, nc, Q, g, n)
    a_cum = jnp.cumsum(dA, axis=-1)

    Lmat = segsum_exp(dA)
    cb = jnp.einsum('bclgn,bcsgn->bgcls', Cc, Bc)
    y_diag = jnp.einsum('bgecls,bcsgep->bclgep', cb[:, :, None] * Lmat, X)

    decay_states = jnp.exp(a_cum[..., -1:] - a_cum)
    states = jnp.einsum('bcsgn,bgecs,bcsgep->bcgepn', Bc, decay_states, X)
    states = jnp.concatenate([jnp.zeros_like(states[:, :1]), states], axis=1)
    decay_chunk = segsum_exp(jnp.pad(a_cum[..., -1], ((0, 0), (0, 0), (0, 0), (1, 0))))
    states = jnp.einsum('bgezc,bcgepn->bzgepn', decay_chunk, states)[:, :-1]

    y_off = jnp.einsum('bclgn,bcgepn,bgecl->bclgep', Cc, states, jnp.exp(a_cum))
    return (y_diag + y_off).reshape(bs, s, h, p)


def hybrid_layer(x, ln1_g, w_in, conv_w, conv_b, dt_bias, a_log, d_skip,
                 attn_norm_g, ssd_norm_g, w_out, ln2_g, w_mlp_in, w_mlp_out, slopes):
    b, s, _ = x.shape
    h = rmsnorm(x, ln1_g)
    proj = h @ w_in
    cuts = [ATTN_WIDTH, 2 * ATTN_WIDTH, 3 * ATTN_WIDTH,
            3 * ATTN_WIDTH + SSD_WIDTH, 3 * ATTN_WIDTH + SSD_WIDTH + CONV_CH]
    q, k, v, z, xbc, dt_raw = jnp.split(proj, cuts, axis=-1)

    q = q.reshape(b, s, ATTN_HEADS, HEAD_DIM)
    k = k.reshape(b, s, ATTN_HEADS, HEAD_DIM)
    v = v.reshape(b, s, ATTN_HEADS, HEAD_DIM)
    attn = dilated_attention(q, k, v, slopes).reshape(b, s, ATTN_WIDTH)
    attn = rmsnorm(attn, attn_norm_g)

    xbc = jax.nn.silu(causal_depthwise_conv(xbc, conv_w, conv_b))
    xs, bm, cm = jnp.split(xbc, [SSD_WIDTH, SSD_WIDTH + SSD_GROUPS * SSD_STATE], axis=-1)
    xs = xs.reshape(b, s, SSD_HEADS, SSD_HEAD_DIM)
    dt = jax.nn.softplus(dt_raw.astype(jnp.float32) + dt_bias.astype(jnp.float32))
    a = -jnp.exp(a_log.astype(jnp.float32))
    y = ssd_scan(xs, dt, a, bm.reshape(b, s, SSD_GROUPS, SSD_STATE),
                 cm.reshape(b, s, SSD_GROUPS, SSD_STATE)).astype(x.dtype)
    y = y + d_skip[:, None] * xs
    y = y.reshape(b, s, SSD_WIDTH) * jax.nn.silu(z)
    y = rmsnorm(y.reshape(b, s, SSD_GROUPS, SSD_WIDTH // SSD_GROUPS),
                ssd_norm_g.reshape(SSD_GROUPS, SSD_WIDTH // SSD_GROUPS)).reshape(b, s, SSD_WIDTH)

    x = x + jnp.concatenate([attn, y], axis=-1) @ w_out

    h = rmsnorm(x, ln2_g)
    x = x + jnp.square(jax.nn.relu(h @ w_mlp_in)) @ w_mlp_out
    return x


def _fwd_setup_inputs(seed: int = 0) -> dict:
    key = jax.random.key(seed)
    ks = jax.random.split(key, 16)
    f32 = jnp.float32
    dt0 = jnp.exp(jax.random.uniform(ks[5], (DEPTH, SSD_HEADS), f32,
                                     minval=float(np.log(1e-3)), maxval=float(np.log(1e-1))))
    return {
        "x": jax.random.normal(ks[0], (BATCH, SEQ, D_MODEL), f32),
        "ln1_g": 1.0 + 0.02 * jax.random.normal(ks[1], (DEPTH, D_MODEL), f32),
        "w_in": jax.random.normal(ks[2], (DEPTH, D_MODEL, IN_PROJ), f32) * D_MODEL ** -0.5,
        "conv_w": jax.random.normal(ks[3], (DEPTH, SSD_CONV, CONV_CH), f32) * SSD_CONV ** -0.5,
        "conv_b": 0.02 * jax.random.normal(ks[4], (DEPTH, CONV_CH), f32),
        "dt_bias": dt0 + jnp.log(-jnp.expm1(-dt0)),
        "a_log": jnp.log(jax.random.uniform(ks[6], (DEPTH, SSD_HEADS), f32, minval=1.0, maxval=16.0)),
        "d_skip": 1.0 + 0.1 * jax.random.normal(ks[7], (DEPTH, SSD_HEADS), f32),
        "attn_norm_g": 1.0 + 0.02 * jax.random.normal(ks[8], (DEPTH, ATTN_WIDTH), f32),
        "ssd_norm_g": 1.0 + 0.02 * jax.random.normal(ks[9], (DEPTH, SSD_WIDTH), f32),
        "w_out": jax.random.normal(ks[10], (DEPTH, MIX_WIDTH, D_MODEL), f32) * MIX_WIDTH ** -0.5,
        "ln2_g": 1.0 + 0.02 * jax.random.normal(ks[11], (DEPTH, D_MODEL), f32),
        "w_mlp_in": jax.random.normal(ks[12], (DEPTH, D_MODEL, D_FF), f32) * D_MODEL ** -0.5,
        "w_mlp_out": jax.random.normal(ks[13], (DEPTH, D_FF, D_MODEL), f32) * D_FF ** -0.5,
        "final_norm_g": 1.0 + 0.02 * jax.random.normal(ks[14], (D_MODEL,), f32),
    }


def _fwd_reference(x, ln1_g, w_in, conv_w, conv_b, dt_bias, a_log, d_skip,
              attn_norm_g, ssd_norm_g, w_out, ln2_g, w_mlp_in, w_mlp_out, final_norm_g):
    slopes = alibi_slopes(ATTN_HEADS)
    for l in range(DEPTH):
        x = hybrid_layer(x, ln1_g[l], w_in[l], conv_w[l], conv_b[l], dt_bias[l], a_log[l],
                         d_skip[l], attn_norm_g[l], ssd_norm_g[l], w_out[l], ln2_g[l],
                         w_mlp_in[l], w_mlp_out[l], slopes)
    return rmsnorm(x, final_norm_g)


import jax as _jax
import jax.numpy as _jnp

TWIN_FORMAT = 'train_step'
FWD_PARAMS = ['x', 'ln1_g', 'w_in', 'conv_w', 'conv_b', 'dt_bias', 'a_log', 'd_skip', 'attn_norm_g', 'ssd_norm_g', 'w_out', 'ln2_g', 'w_mlp_in', 'w_mlp_out', 'final_norm_g']
TWIN_WEIGHTS = ['ln1_g', 'w_in', 'conv_w', 'conv_b', 'dt_bias', 'a_log', 'd_skip', 'attn_norm_g', 'ssd_norm_g', 'w_out', 'ln2_g', 'w_mlp_in', 'w_mlp_out', 'final_norm_g']
TWIN_DIFF_INPUT = 'x'
TWIN_INPUTS = ['x', 'ln1_g', 'w_in', 'conv_w', 'conv_b', 'dt_bias', 'a_log', 'd_skip', 'attn_norm_g', 'ssd_norm_g', 'w_out', 'ln2_g', 'w_mlp_in', 'w_mlp_out', 'final_norm_g', 'loss_target', 'm_ln1_g', 'm_w_in', 'm_conv_w', 'm_conv_b', 'm_dt_bias', 'm_a_log', 'm_d_skip', 'm_attn_norm_g', 'm_ssd_norm_g', 'm_w_out', 'm_ln2_g', 'm_w_mlp_in', 'm_w_mlp_out', 'm_final_norm_g', 'v_ln1_g', 'v_w_in', 'v_conv_w', 'v_conv_b', 'v_dt_bias', 'v_a_log', 'v_d_skip', 'v_attn_norm_g', 'v_ssd_norm_g', 'v_w_out', 'v_ln2_g', 'v_w_mlp_in', 'v_w_mlp_out', 'v_final_norm_g']
TWIN_OUTPUTS = ['loss', 'grad_x', 'grad_ln1_g', 'grad_w_in', 'grad_conv_w', 'grad_conv_b', 'grad_dt_bias', 'grad_a_log', 'grad_d_skip', 'grad_attn_norm_g', 'grad_ssd_norm_g', 'grad_w_out', 'grad_ln2_g', 'grad_w_mlp_in', 'grad_w_mlp_out', 'grad_final_norm_g', 'delta_ln1_g', 'delta_w_in', 'delta_conv_w', 'delta_conv_b', 'delta_dt_bias', 'delta_a_log', 'delta_d_skip', 'delta_attn_norm_g', 'delta_ssd_norm_g', 'delta_w_out', 'delta_ln2_g', 'delta_w_mlp_in', 'delta_w_mlp_out', 'delta_final_norm_g', 'new_m_ln1_g', 'new_m_w_in', 'new_m_conv_w', 'new_m_conv_b', 'new_m_dt_bias', 'new_m_a_log', 'new_m_d_skip', 'new_m_attn_norm_g', 'new_m_ssd_norm_g', 'new_m_w_out', 'new_m_ln2_g', 'new_m_w_mlp_in', 'new_m_w_mlp_out', 'new_m_final_norm_g', 'new_v_ln1_g', 'new_v_w_in', 'new_v_conv_w', 'new_v_conv_b', 'new_v_dt_bias', 'new_v_a_log', 'new_v_d_skip', 'new_v_attn_norm_g', 'new_v_ssd_norm_g', 'new_v_w_out', 'new_v_ln2_g', 'new_v_w_mlp_in', 'new_v_w_mlp_out', 'new_v_final_norm_g']
TWIN_LEAF_KINDS = {'loss': 'loss', 'grad_x': 'grad_x', 'grad_ln1_g': 'grad_w', 'grad_w_in': 'grad_w', 'grad_conv_w': 'grad_w', 'grad_conv_b': 'grad_w', 'grad_dt_bias': 'grad_w', 'grad_a_log': 'grad_w', 'grad_d_skip': 'grad_w', 'grad_attn_norm_g': 'grad_w', 'grad_ssd_norm_g': 'grad_w', 'grad_w_out': 'grad_w', 'grad_ln2_g': 'grad_w', 'grad_w_mlp_in': 'grad_w', 'grad_w_mlp_out': 'grad_w', 'grad_final_norm_g': 'grad_w', 'delta_ln1_g': 'delta_w', 'delta_w_in': 'delta_w', 'delta_conv_w': 'delta_w', 'delta_conv_b': 'delta_w', 'delta_dt_bias': 'delta_w', 'delta_a_log': 'delta_w', 'delta_d_skip': 'delta_w', 'delta_attn_norm_g': 'delta_w', 'delta_ssd_norm_g': 'delta_w', 'delta_w_out': 'delta_w', 'delta_ln2_g': 'delta_w', 'delta_w_mlp_in': 'delta_w', 'delta_w_mlp_out': 'delta_w', 'delta_final_norm_g': 'delta_w', 'new_m_ln1_g': 'new_m', 'new_m_w_in': 'new_m', 'new_m_conv_w': 'new_m', 'new_m_conv_b': 'new_m', 'new_m_dt_bias': 'new_m', 'new_m_a_log': 'new_m', 'new_m_d_skip': 'new_m', 'new_m_attn_norm_g': 'new_m', 'new_m_ssd_norm_g': 'new_m', 'new_m_w_out': 'new_m', 'new_m_ln2_g': 'new_m', 'new_m_w_mlp_in': 'new_m', 'new_m_w_mlp_out': 'new_m', 'new_m_final_norm_g': 'new_m', 'new_v_ln1_g': 'new_v', 'new_v_w_in': 'new_v', 'new_v_conv_w': 'new_v', 'new_v_conv_b': 'new_v', 'new_v_dt_bias': 'new_v', 'new_v_a_log': 'new_v', 'new_v_d_skip': 'new_v', 'new_v_attn_norm_g': 'new_v', 'new_v_ssd_norm_g': 'new_v', 'new_v_w_out': 'new_v', 'new_v_ln2_g': 'new_v', 'new_v_w_mlp_in': 'new_v', 'new_v_w_mlp_out': 'new_v', 'new_v_final_norm_g': 'new_v'}


def _forward(args):
    return _fwd_reference(*[args[k] for k in FWD_PARAMS])


def _output_shape():
    out = _jax.eval_shape(lambda: _forward(_fwd_setup_inputs(0)))
    return out.shape, out.dtype

N_MICROBATCH = 1
ADAM_LR = 0.001
ADAM_B1 = 0.9
ADAM_B2 = 0.999
ADAM_EPS = 1e-08
ADAM_WD = 0.01
ADAM_STEP = 10
PER_EXAMPLE_BATCH_AXIS = {'x': 0, 'loss_target': 0}
SHARED_INPUTS = []
_WEIGHT_DTYPES = {'ln1_g': _jnp.float32, 'w_in': _jnp.float32, 'conv_w': _jnp.float32, 'conv_b': _jnp.float32, 'dt_bias': _jnp.float32, 'a_log': _jnp.float32, 'd_skip': _jnp.float32, 'attn_norm_g': _jnp.float32, 'ssd_norm_g': _jnp.float32, 'w_out': _jnp.float32, 'ln2_g': _jnp.float32, 'w_mlp_in': _jnp.float32, 'w_mlp_out': _jnp.float32, 'final_norm_g': _jnp.float32}
MOMENT_SCALE = {'ln1_g': 8.994497e-02, 'w_in': 5.273862e-02, 'conv_w': 5.444267e-02, 'conv_b': 7.542824e-02, 'dt_bias': 1.584013e-01, 'a_log': 2.611256e-01, 'd_skip': 4.982295e-01, 'attn_norm_g': 6.366182e-02, 'ssd_norm_g': 6.985500e-02, 'w_out': 6.370920e-02, 'ln2_g': 6.328711e-02, 'w_mlp_in': 3.216092e-02, 'w_mlp_out': 6.639317e-02, 'final_norm_g': 1.627118e+01}


def _to_microbatches(a, axis):
    t = _jnp.moveaxis(a, axis, 0)
    t = t.reshape((N_MICROBATCH, t.shape[0] // N_MICROBATCH) + t.shape[1:])
    return _jnp.moveaxis(t, 1, axis + 1)


def setup_inputs(seed: int = 0) -> dict:
    inp = _fwd_setup_inputs(seed)
    key = _jax.random.fold_in(_jax.random.key(seed), 7919)
    shape, _ = _output_shape()
    out = dict(inp)
    out["loss_target"] = _jax.random.normal(_jax.random.fold_in(key, 0), shape, _jnp.float32)
    for i, name in enumerate(TWIN_WEIGHTS):
        w = inp[name].astype(_jnp.float32)
        if MOMENT_SCALE is None:
            s = _jnp.sqrt(_jnp.mean(_jnp.square(w)) + 1e-30)
        else:
            s = MOMENT_SCALE[name]
        km, kv = _jax.random.split(_jax.random.fold_in(key, i + 1))
        out[name] = w
        out["m_" + name] = s * _jax.random.normal(km, w.shape, _jnp.float32)
        out["v_" + name] = (s * s) * _jax.random.uniform(kv, w.shape, _jnp.float32, 0.5, 1.5)
    if N_MICROBATCH > 1:
        for name, axis in PER_EXAMPLE_BATCH_AXIS.items():
            out[name] = _to_microbatches(out[name], axis)
    return {'x': out['x'], 'ln1_g': out['ln1_g'], 'w_in': out['w_in'], 'conv_w': out['conv_w'], 'conv_b': out['conv_b'], 'dt_bias': out['dt_bias'], 'a_log': out['a_log'], 'd_skip': out['d_skip'], 'attn_norm_g': out['attn_norm_g'], 'ssd_norm_g': out['ssd_norm_g'], 'w_out': out['w_out'], 'ln2_g': out['ln2_g'], 'w_mlp_in': out['w_mlp_in'], 'w_mlp_out': out['w_mlp_out'], 'final_norm_g': out['final_norm_g'], 'loss_target': out['loss_target'], 'm_ln1_g': out['m_ln1_g'], 'm_w_in': out['m_w_in'], 'm_conv_w': out['m_conv_w'], 'm_conv_b': out['m_conv_b'], 'm_dt_bias': out['m_dt_bias'], 'm_a_log': out['m_a_log'], 'm_d_skip': out['m_d_skip'], 'm_attn_norm_g': out['m_attn_norm_g'], 'm_ssd_norm_g': out['m_ssd_norm_g'], 'm_w_out': out['m_w_out'], 'm_ln2_g': out['m_ln2_g'], 'm_w_mlp_in': out['m_w_mlp_in'], 'm_w_mlp_out': out['m_w_mlp_out'], 'm_final_norm_g': out['m_final_norm_g'], 'v_ln1_g': out['v_ln1_g'], 'v_w_in': out['v_w_in'], 'v_conv_w': out['v_conv_w'], 'v_conv_b': out['v_conv_b'], 'v_dt_bias': out['v_dt_bias'], 'v_a_log': out['v_a_log'], 'v_d_skip': out['v_d_skip'], 'v_attn_norm_g': out['v_attn_norm_g'], 'v_ssd_norm_g': out['v_ssd_norm_g'], 'v_w_out': out['v_w_out'], 'v_ln2_g': out['v_ln2_g'], 'v_w_mlp_in': out['v_w_mlp_in'], 'v_w_mlp_out': out['v_w_mlp_out'], 'v_final_norm_g': out['v_final_norm_g']}


def _loss(weights, diff, rest, loss_target):
    with _jax.named_scope("forward"):
        args = {**rest, TWIN_DIFF_INPUT: diff, **{k: w.astype(_WEIGHT_DTYPES[k]) for k, w in weights.items()}}
        y = _forward(args)
    with _jax.named_scope("loss_head"):
        err = _jnp.square(y.astype(_jnp.float32) - loss_target)
        return 0.5 * _jnp.sum(_jnp.mean(err, axis=-1)) if err.ndim else 0.5 * err


def _adamw(w, g, m, v):
    m = ADAM_B1 * m + (1.0 - ADAM_B1) * g
    v = ADAM_B2 * v + (1.0 - ADAM_B2) * _jnp.square(g)
    m_hat = m / (1.0 - ADAM_B1 ** ADAM_STEP)
    v_hat = v / (1.0 - ADAM_B2 ** ADAM_STEP)
    delta = -ADAM_LR * (m_hat / (_jnp.sqrt(v_hat) + ADAM_EPS) + ADAM_WD * w)
    return delta, m, v


def reference(x, ln1_g, w_in, conv_w, conv_b, dt_bias, a_log, d_skip, attn_norm_g, ssd_norm_g, w_out, ln2_g, w_mlp_in, w_mlp_out, final_norm_g, loss_target, m_ln1_g, m_w_in, m_conv_w, m_conv_b, m_dt_bias, m_a_log, m_d_skip, m_attn_norm_g, m_ssd_norm_g, m_w_out, m_ln2_g, m_w_mlp_in, m_w_mlp_out, m_final_norm_g, v_ln1_g, v_w_in, v_conv_w, v_conv_b, v_dt_bias, v_a_log, v_d_skip, v_attn_norm_g, v_ssd_norm_g, v_w_out, v_ln2_g, v_w_mlp_in, v_w_mlp_out, v_final_norm_g):
    given = dict(x=x, ln1_g=ln1_g, w_in=w_in, conv_w=conv_w, conv_b=conv_b, dt_bias=dt_bias, a_log=a_log, d_skip=d_skip, attn_norm_g=attn_norm_g, ssd_norm_g=ssd_norm_g, w_out=w_out, ln2_g=ln2_g, w_mlp_in=w_mlp_in, w_mlp_out=w_mlp_out, final_norm_g=final_norm_g, loss_target=loss_target, m_ln1_g=m_ln1_g, m_w_in=m_w_in, m_conv_w=m_conv_w, m_conv_b=m_conv_b, m_dt_bias=m_dt_bias, m_a_log=m_a_log, m_d_skip=m_d_skip, m_attn_norm_g=m_attn_norm_g, m_ssd_norm_g=m_ssd_norm_g, m_w_out=m_w_out, m_ln2_g=m_ln2_g, m_w_mlp_in=m_w_mlp_in, m_w_mlp_out=m_w_mlp_out, m_final_norm_g=m_final_norm_g, v_ln1_g=v_ln1_g, v_w_in=v_w_in, v_conv_w=v_conv_w, v_conv_b=v_conv_b, v_dt_bias=v_dt_bias, v_a_log=v_a_log, v_d_skip=v_d_skip, v_attn_norm_g=v_attn_norm_g, v_ssd_norm_g=v_ssd_norm_g, v_w_out=v_w_out, v_ln2_g=v_ln2_g, v_w_mlp_in=v_w_mlp_in, v_w_mlp_out=v_w_mlp_out, v_final_norm_g=v_final_norm_g)
    weights = {n: given[n] for n in TWIN_WEIGHTS}
    shared = {n: given[n] for n in SHARED_INPUTS}
    per_example = {n: given[n] for n in ['x']}
    grad_fn = _jax.value_and_grad(_loss, argnums=(0, 1))

    def one_microbatch(ex, loss_target):
        ex = dict(ex)
        diff = ex.pop(TWIN_DIFF_INPUT)
        return grad_fn(weights, diff, {**shared, **ex}, loss_target)

    if N_MICROBATCH == 1:
        loss, (grad_w, grad_x) = one_microbatch(per_example, given["loss_target"])
    else:
        def body(carry, xs):
            loss_sum, grad_sum = carry
            l_k, (gw_k, gx_k) = one_microbatch(xs[0], xs[1])
            with _jax.named_scope("update"):
                return (loss_sum + l_k, _jax.tree.map(_jnp.add, grad_sum, gw_k)), gx_k

        init = (_jnp.zeros((), _jnp.float32), _jax.tree.map(_jnp.zeros_like, weights))
        (loss, grad_w), grad_x = _jax.lax.scan(body, init, (per_example, given["loss_target"]))
    with _jax.named_scope("update"):
        delta_w, new_m, new_v = {}, {}, {}
        for n in TWIN_WEIGHTS:
            delta_w[n], new_m[n], new_v[n] = _adamw(weights[n], grad_w[n], given["m_" + n], given["v_" + n])
    return (loss, grad_x, *[grad_w[n] for n in TWIN_WEIGHTS], *[delta_w[n] for n in TWIN_WEIGHTS],
            *[new_m[n] for n in TWIN_WEIGHTS], *[new_v[n] for n in TWIN_WEIGHTS])
```

```python
import functools
import math

import numpy as np
import jax
import jax.numpy as jnp
from jax import lax
from jax.experimental import pallas as pl
from jax.experimental.pallas import tpu as pltpu

F32 = jnp.float32
BF16 = jnp.bfloat16

N_DEV = 8
LANES = 128
HEAD_DIM = 64
ATTN_BLOCK = 128
BRANCH_DILATIONS = (1, 4, 16)
SSD_GROUPS = 2
SSD_STATE = 128
SSD_CHUNK = 128
SSD_CONV = 4
NORM_EPS = 1e-5
ADAM_LR, ADAM_B1, ADAM_B2, ADAM_EPS, ADAM_WD, ADAM_STEP = 0.001, 0.9, 0.999, 1e-08, 0.01, 10
VMEM_LIMIT_BYTES = 56 * 1024 * 1024
MESH = pl.DeviceIdType.MESH
NEG_INF = float("-inf")


def _params(*sem):
    return pltpu.CompilerParams(dimension_semantics=tuple(sem), vmem_limit_bytes=VMEM_LIMIT_BYTES)


def _pick(n, target, mult):
    best = None
    for t in range(mult, min(n, target) + 1, mult):
        if n % t == 0:
            best = t
    assert best is not None, (n, target, mult)
    return best


def _dot(a, b, ca, cb):
    return lax.dot_general(a, b, (((ca,), (cb,)), ((), ())), preferred_element_type=F32)


def _split3(v):
    hi = v.astype(BF16)
    r = v - hi.astype(F32)
    mid = r.astype(BF16)
    lo = (r - mid.astype(F32)).astype(BF16)
    return hi, mid, lo


def _dot_exact(v, sel, ca, cb):
    hi, mid, lo = _split3(v)
    return _dot(hi, sel, ca, cb) + _dot(mid, sel, ca, cb) + _dot(lo, sel, ca, cb)


_HBM = pl.BlockSpec(memory_space=pltpu.HBM)


def _all_gather(xs, name):
    n = len(xs)

    def body(*refs):
        x_refs, o_refs = refs[:n], refs[n:2 * n]
        send_sems, recv_sems, local_sems = refs[2 * n:]
        x, y, c = lax.axis_index("x"), lax.axis_index("y"), lax.axis_index("c")
        me, sibling = (x, y, c), (x, y, 1 - c)
        chips = [(1 - x, y), (x, 1 - y), (1 - x, 1 - y)]

        def copy(t, k, block, to, src=None):
            bx, by, bc = block
            dst = o_refs[t].at[4 * bx + 2 * by + bc]
            return pltpu.make_async_remote_copy(
                src_ref=dst if src is None else src, dst_ref=dst,
                send_sem=send_sems.at[t, k], recv_sem=recv_sems.at[t, k],
                device_id=to, device_id_type=MESH)

        mine = [pltpu.make_async_copy(x_refs[t], o_refs[t].at[4 * x + 2 * y + c], local_sems.at[t])
                for t in range(n)]
        first, passed = [], []
        for t in range(n):
            mine[t].start()
            cps = [copy(t, 0, me, sibling, src=x_refs[t])]
            cps += [copy(t, 1 + j, me, (*chip, c), src=x_refs[t]) for j, chip in enumerate(chips)]
            for cp in cps:
                cp.start()
            first += cps
        for t in range(n):
            for j, chip in enumerate(chips):
                copy(t, 1 + j, (*chip, c), me).wait_recv()
                fwd = copy(t, 4 + j, (*chip, c), sibling)
                fwd.start()
                passed.append(fwd)
        for t in range(n):
            copy(t, 0, sibling, me).wait_recv()
            for j, chip in enumerate(chips):
                copy(t, 4 + j, (*chip, 1 - c), me).wait_recv()
        for cp in first + passed:
            cp.wait_send()
        for t in range(n):
            mine[t].wait()

    return pl.pallas_call(
        body, name=name,
        out_shape=[jax.ShapeDtypeStruct((N_DEV,) + a.shape, a.dtype) for a in xs],
        in_specs=[_HBM] * n, out_specs=[_HBM] * n,
        scratch_shapes=[pltpu.SemaphoreType.DMA((n, 7)), pltpu.SemaphoreType.DMA((n, 7)),
                        pltpu.SemaphoreType.DMA((n,))],
    )(*xs)


def _exchange_blocks(xs, name):
    n = len(xs)

    def body(*refs):
        x_refs, o_refs = refs[:n], refs[n:2 * n]
        send_sems, recv_sems, local_sems = refs[2 * n:]
        x, y, c = lax.axis_index("x"), lax.axis_index("y"), lax.axis_index("c")
        my_id = 4 * x + 2 * y + c

        def peer(k):
            px = 1 - x if (k >> 2) & 1 else x
            py = 1 - y if (k >> 1) & 1 else y
            pc = 1 - c if k & 1 else c
            return (px, py, pc)

        def copy(t, k):
            px, py, pc = peer(k)
            pid = 4 * px + 2 * py + pc
            return pltpu.make_async_remote_copy(
                src_ref=x_refs[t].at[pid], dst_ref=o_refs[t].at[my_id],
                send_sem=send_sems.at[t, k - 1], recv_sem=recv_sems.at[t, k - 1],
                device_id=(px, py, pc), device_id_type=MESH)

        def landed(t, k):
            px, py, pc = peer(k)
            pid = 4 * px + 2 * py + pc
            return pltpu.make_async_remote_copy(
                src_ref=x_refs[t].at[pid], dst_ref=o_refs[t].at[pid],
                send_sem=send_sems.at[t, k - 1], recv_sem=recv_sems.at[t, k - 1],
                device_id=(px, py, pc), device_id_type=MESH)

        mine = [pltpu.make_async_copy(x_refs[t].at[my_id], o_refs[t].at[my_id], local_sems.at[t])
                for t in range(n)]
        for t in range(n):
            mine[t].start()
            for k in range(1, N_DEV):
                copy(t, k).start()
        for t in range(n):
            for k in range(1, N_DEV):
                landed(t, k).wait_recv()
        for t in range(n):
            for k in range(1, N_DEV):
                copy(t, k).wait_send()
            mine[t].wait()

    return pl.pallas_call(
        body, name=name,
        out_shape=[jax.ShapeDtypeStruct(a.shape, a.dtype) for a in xs],
        in_specs=[_HBM] * n, out_specs=[_HBM] * n,
        scratch_shapes=[pltpu.SemaphoreType.DMA((n, 7)), pltpu.SemaphoreType.DMA((n, 7)),
                        pltpu.SemaphoreType.DMA((n,))],
    )(*xs)


def _row_tile(rows, cols, itemsize, copies, budget=24 * 1024 * 1024):
    padded = -(-cols // LANES) * LANES
    mult = 8 * (4 // itemsize)
    if rows % mult:
        return rows
    return _pick(rows, max(mult, budget // (copies * padded * itemsize)), mult)


def _sum_devices(x, name):
    _, rows, cols = x.shape
    tr = _row_tile(rows, cols, x.dtype.itemsize, 2 * N_DEV + 4)

    def body(x_ref, o_ref):
        acc = x_ref[0].astype(F32)
        for s in range(1, N_DEV):
            acc = acc + x_ref[s].astype(F32)
        o_ref[...] = acc

    return pl.pallas_call(
        body, name=name, grid=(rows // tr,),
        out_shape=jax.ShapeDtypeStruct((rows, cols), F32),
        in_specs=[pl.BlockSpec((N_DEV, tr, cols), lambda i: (0, i, 0))],
        out_specs=pl.BlockSpec((tr, cols), lambda i: (i, 0)),
        compiler_params=_params("parallel"),
    )(x)


def _mm(a, b, *, name, ta=False, tb=False, tm=512, tn=512, tk=512, out_dtype=F32, a_act=None,
        residual=None, gate=None, a_chunk=None, b_chunk=None, out_chunk=None):
    if a_chunk:
        assert not ta
        m, k_dim = a.shape[1], a.shape[0] * a.shape[2]
    else:
        k_dim, m = (a.shape if ta else a.shape[::-1])
    if b_chunk:
        assert not tb
        kb, n = b.shape[1], b.shape[0] * b.shape[2]
    else:
        n, kb = (b.shape if tb else b.shape[::-1])
    assert kb == k_dim, (a.shape, b.shape, ta, tb)
    tm = _pick(m, tm, 128)
    tn = _pick(b_chunk or out_chunk or n, tn, 128)
    tk = _pick(a_chunk or k_dim, tk, 128)
    assert not (b_chunk and out_chunk) or b_chunk == out_chunk
    nk = k_dim // tk
    ca, cb = (0 if ta else 1), (1 if tb else 0)

    def chunked(ch, t, row_of, col_of):
        per = ch // t
        return pl.BlockSpec((None, row_of[0], t),
                            lambda i, j, k: (col_of(i, j, k) // per, row_of[1](i, j, k), col_of(i, j, k) % per))

    if a_chunk:
        a_spec = chunked(a_chunk, tk, (tm, lambda i, j, k: i), lambda i, j, k: k)
    elif ta:
        a_spec = pl.BlockSpec((tk, tm), lambda i, j, k: (k, i))
    else:
        a_spec = pl.BlockSpec((tm, tk), lambda i, j, k: (i, k))
    if b_chunk:
        b_spec = chunked(b_chunk, tn, (tk, lambda i, j, k: k), lambda i, j, k: j)
    elif tb:
        b_spec = pl.BlockSpec((tn, tk), lambda i, j, k: (j, k))
    else:
        b_spec = pl.BlockSpec((tk, tn), lambda i, j, k: (k, j))
    mn_spec = pl.BlockSpec((tm, tn), lambda i, j, k: (i, j))
    if out_chunk:
        o_spec = chunked(out_chunk, tn, (tm, lambda i, j, k: i), lambda i, j, k: j)
        out_shape = jax.ShapeDtypeStruct((n // out_chunk, m, out_chunk), out_dtype)
    else:
        o_spec = mn_spec
        out_shape = jax.ShapeDtypeStruct((m, n), out_dtype)

    operands, in_specs = [a, b], [a_spec, b_spec]
    if gate is not None:
        operands.append(gate)
        in_specs.append(mn_spec)
    if residual is not None:
        operands.append(residual)
        in_specs.append(mn_spec)

    def body(*refs):
        a_ref, b_ref = refs[0], refs[1]
        pos = 2
        gate_ref = res_ref = None
        if gate is not None:
            gate_ref = refs[pos]
            pos += 1
        if residual is not None:
            res_ref = refs[pos]
            pos += 1
        o_ref = refs[pos]
        acc_ref = refs[pos + 1] if nk > 1 else None
        k = pl.program_id(2)

        av = a_ref[...]
        if a_act == "relu2":
            av = jnp.square(jnp.maximum(av.astype(F32), 0.0))
        part = _dot(av.astype(BF16), b_ref[...].astype(BF16), ca, cb)

        def finish(r):
            if gate_ref is not None:
                r = r * (2.0 * jnp.maximum(gate_ref[...].astype(F32), 0.0))
            if res_ref is not None:
                r = r + res_ref[...].astype(F32)
            o_ref[...] = r.astype(out_dtype)

        if nk == 1:
            finish(part)
        else:
            @pl.when(k == 0)
            def _():
                acc_ref[...] = part

            @pl.when(k > 0)
            def _():
                acc_ref[...] += part

            @pl.when(k == nk - 1)
            def _():
                finish(acc_ref[...])

    return pl.pallas_call(
        body, name=name, grid=(m // tm, n // tn, nk), out_shape=out_shape,
        in_specs=in_specs, out_specs=o_spec,
        scratch_shapes=[pltpu.VMEM((tm, tn), F32)] if nk > 1 else [],
        compiler_params=_params("parallel", "parallel", "arbitrary"),
    )(*operands)


def _rmsnorm_fwd(xs, seg_widths, g, name, tm=256):
    t_len = xs[0].shape[0]
    width = sum(x.shape[1] for x in xs)
    tm = _pick(t_len, tm, 16)
    n = len(xs)

    def body(*refs):
        x_refs, g_ref, o_ref = refs[:n], refs[n], refs[n + 1]
        col = 0
        for x_ref, widths in zip(x_refs, seg_widths):
            off = 0
            for w in widths:
                xv = x_ref[:, off:off + w].astype(F32)
                r = lax.rsqrt(jnp.mean(xv * xv, axis=1, keepdims=True) + NORM_EPS)
                o_ref[:, col:col + w] = (xv * r * g_ref[:, col:col + w]).astype(BF16)
                off += w
                col += w

    return pl.pallas_call(
        body, name=name, grid=(t_len // tm,),
        out_shape=jax.ShapeDtypeStruct((t_len, width), BF16),
        in_specs=[pl.BlockSpec((tm, x.shape[1]), lambda i: (i, 0)) for x in xs]
        + [pl.BlockSpec((1, width), lambda i: (0, 0))],
        out_specs=pl.BlockSpec((tm, width), lambda i: (i, 0)),
        compiler_params=_params("parallel"),
    )(*xs, g)


def _rmsnorm_bwd(xs, seg_widths, g, dh, residuals, name, tm=256):
    t_len = xs[0].shape[0]
    width = sum(x.shape[1] for x in xs)
    tm = _pick(t_len, tm, 8)
    n = len(xs)
    has_res = [r is not None for r in residuals]
    res_ops = [r for r in residuals if r is not None]

    def body(*refs):
        x_refs, g_ref, dh_ref = refs[:n], refs[n], refs[n + 1]
        res_refs = list(refs[n + 2:n + 2 + len(res_ops)])
        dx_refs = refs[n + 2 + len(res_ops):n + 2 + len(res_ops) + n]
        dg_ref = refs[-1]
        first = pl.program_id(0) == 0
        col = 0
        for idx, (x_ref, widths) in enumerate(zip(x_refs, seg_widths)):
            res_ref = res_refs.pop(0) if has_res[idx] else None
            off = 0
            for w in widths:
                xv = x_ref[:, off:off + w].astype(F32)
                r = lax.rsqrt(jnp.mean(xv * xv, axis=1, keepdims=True) + NORM_EPS)
                xh = xv * r
                dhv = dh_ref[:, col:col + w].astype(F32)
                gd = dhv * g_ref[:, col:col + w]
                dx = r * (gd - xh * jnp.mean(gd * xh, axis=1, keepdims=True))
                if res_ref is not None:
                    dx = dx + res_ref[:, off:off + w]
                dx_refs[idx][:, off:off + w] = dx
                part = jnp.sum(dhv * xh, axis=0, keepdims=True)

                @pl.when(first)
                def _(part=part, col=col, w=w):
                    dg_ref[:, col:col + w] = part

                @pl.when(jnp.logical_not(first))
                def _(part=part, col=col, w=w):
                    dg_ref[:, col:col + w] += part
                off += w
                col += w

    outs = pl.pallas_call(
        body, name=name, grid=(t_len // tm,),
        out_shape=[jax.ShapeDtypeStruct(x.shape, F32) for x in xs] + [jax.ShapeDtypeStruct((1, width), F32)],
        in_specs=[pl.BlockSpec((tm, x.shape[1]), lambda i: (i, 0)) for x in xs]
        + [pl.BlockSpec((1, width), lambda i: (0, 0)), pl.BlockSpec((tm, width), lambda i: (i, 0))]
        + [pl.BlockSpec((tm, r.shape[1]), lambda i: (i, 0)) for r in res_ops],
        out_specs=[pl.BlockSpec((tm, x.shape[1]), lambda i: (i, 0)) for x in xs]
        + [pl.BlockSpec((1, width), lambda i: (0, 0))],
        compiler_params=_params("arbitrary"),
    )(*xs, g, dh, *res_ops)
    return outs[:n], outs[n]


def _loss_head(x, g, target, name, tm=256):
    t_len, d = x.shape
    tm = _pick(t_len, tm, 8)

    def body(x_ref, g_ref, t_ref, dx_ref, dg_ref, loss_ref):
        first = pl.program_id(0) == 0
        xv = x_ref[...]
        r = lax.rsqrt(jnp.mean(xv * xv, axis=1, keepdims=True) + NORM_EPS)
        xh = xv * r
        gv = g_ref[...]
        err = xh * gv - t_ref[...]
        part_loss = 0.5 * jnp.sum(jnp.mean(err * err, axis=1, keepdims=True), axis=0, keepdims=True)
        dy = err * (1.0 / d)
        gd = dy * gv
        dx_ref[...] = r * (gd - xh * jnp.mean(gd * xh, axis=1, keepdims=True))
        part_g = jnp.sum(dy * xh, axis=0, keepdims=True)
        part_loss = jnp.broadcast_to(part_loss, (1, LANES))

        @pl.when(first)
        def _():
            dg_ref[...] = part_g
            loss_ref[...] = part_loss

        @pl.when(jnp.logical_not(first))
        def _():
            dg_ref[...] += part_g
            loss_ref[...] += part_loss

    return pl.pallas_call(
        body, name=name, grid=(t_len // tm,),
        out_shape=[jax.ShapeDtypeStruct((t_len, d), F32), jax.ShapeDtypeStruct((1, d), F32),
                   jax.ShapeDtypeStruct((1, LANES), F32)],
        in_specs=[pl.BlockSpec((tm, d), lambda i: (i, 0)), pl.BlockSpec((1, d), lambda i: (0, 0)),
                  pl.BlockSpec((tm, d), lambda i: (i, 0))],
        out_specs=[pl.BlockSpec((tm, d), lambda i: (i, 0)), pl.BlockSpec((1, d), lambda i: (0, 0)),
                   pl.BlockSpec((1, LANES), lambda i: (0, 0))],
        compiler_params=_params("arbitrary"),
    )(x, g, target)


def _alibi_slope(h, n_heads):
    return float(2.0 ** (-8.0 * (h + 1) / n_heads))


def _attn_masks(first_block):
    i = lax.broadcasted_iota(jnp.int32, (ATTN_BLOCK, ATTN_BLOCK), 0)
    j = lax.broadcasted_iota(jnp.int32, (ATTN_BLOCK, ATTN_BLOCK), 1)
    valid_cur = j <= i
    valid_prev = jnp.logical_and(j >= i, jnp.logical_not(first_block))
    delta_cur = (i - j).astype(F32)
    delta_prev = (i - j + ATTN_BLOCK).astype(F32)
    return valid_cur, valid_prev, delta_cur, delta_prev


def _head_lane_masks():
    lane = lax.broadcasted_iota(jnp.int32, (ATTN_BLOCK, LANES), 1)
    return [lane < HEAD_DIM, lane >= HEAD_DIM]


def _attn_branch_fwd(qkvz, dilation, n_heads, name):
    _, t_len, w = qkvz.shape
    d = dilation
    sub_len = t_len // d
    nb = sub_len // ATTN_BLOCK
    view = qkvz.reshape(4, sub_len, d * w)
    scale = HEAD_DIM ** -0.5
    n_pairs = w // LANES

    def body(q_ref, kp_ref, kc_ref, vp_ref, vc_ref, o_ref, lse_ref):
        first_block = pl.program_id(1) == 0
        valid_cur, valid_prev, delta_cur, delta_prev = _attn_masks(first_block)
        masks = _head_lane_masks()
        for p in range(n_pairs):
            cols = slice(p * LANES, (p + 1) * LANES)
            q = (q_ref[:, cols] * scale).astype(BF16)
            kp, kc = kp_ref[:, cols].astype(BF16), kc_ref[:, cols].astype(BF16)
            vp, vc = vp_ref[:, cols].astype(BF16), vc_ref[:, cols].astype(BF16)
            outs, lses = [], []
            for hh in range(2):
                slope = _alibi_slope(2 * p + hh, n_heads) * d
                qh = jnp.where(masks[hh], q, jnp.zeros_like(q))
                s_cur = jnp.where(valid_cur, _dot(qh, kc, 1, 1) - slope * delta_cur, NEG_INF)
                s_prev = jnp.where(valid_prev, _dot(qh, kp, 1, 1) - slope * delta_prev, NEG_INF)
                m = jnp.maximum(jnp.max(s_cur, axis=1, keepdims=True), jnp.max(s_prev, axis=1, keepdims=True))
                p_cur, p_prev = jnp.exp(s_cur - m), jnp.exp(s_prev - m)
                den = jnp.sum(p_cur, axis=1, keepdims=True) + jnp.sum(p_prev, axis=1, keepdims=True)
                acc = _dot(p_cur.astype(BF16), vc, 1, 0) + _dot(p_prev.astype(BF16), vp, 1, 0)
                outs.append(acc / den)
                lses.append(jnp.broadcast_to(m + jnp.log(den), (ATTN_BLOCK, LANES)))
            o_ref[:, cols] = jnp.where(masks[0], outs[0], outs[1])
            lse_ref[:, cols] = jnp.where(masks[0], lses[0], lses[1])

    def spec(which, prev):
        if prev:
            return pl.BlockSpec((None, ATTN_BLOCK, w), lambda r, n: (which, jnp.maximum(n - 1, 0), r))
        return pl.BlockSpec((None, ATTN_BLOCK, w), lambda r, n: (which, n, r))

    o_spec = pl.BlockSpec((ATTN_BLOCK, w), lambda r, n: (n, r))
    out, lse = pl.pallas_call(
        body, name=name, grid=(d, nb),
        out_shape=[jax.ShapeDtypeStruct((sub_len, d * w), F32)] * 2,
        in_specs=[spec(0, False), spec(1, True), spec(1, False), spec(2, True), spec(2, False)],
        out_specs=[o_spec, o_spec],
        compiler_params=_params("parallel", "parallel"),
    )(view, view, view, view, view)
    return out.reshape(t_len, w), lse.reshape(t_len, w)


def _attn_combine(outs, lses, name, tm=512):
    t_len, w = outs[0].shape
    tm = _pick(t_len, tm, 8)
    nbr = len(outs)

    def body(*refs):
        o_refs, l_refs = refs[:nbr], refs[nbr:2 * nbr]
        out_ref, lse_ref = refs[2 * nbr:]
        ls = [r[...] for r in l_refs]
        m = functools.reduce(jnp.maximum, ls)
        es = [jnp.exp(l - m) for l in ls]
        den = functools.reduce(lambda a, b: a + b, es)
        num = functools.reduce(lambda a, b: a + b, [e * r[...] for e, r in zip(es, o_refs)])
        out_ref[...] = num / den
        lse_ref[...] = m + jnp.log(den)

    spec = pl.BlockSpec((tm, w), lambda i: (i, 0))
    return pl.pallas_call(
        body, name=name, grid=(t_len // tm,),
        out_shape=[jax.ShapeDtypeStruct((t_len, w), F32)] * 2,
        in_specs=[spec] * (2 * nbr), out_specs=[spec, spec],
        compiler_params=_params("parallel"),
    )(*outs, *lses)


def _attn_branch_bwd(qkvz, out, lse, dout, dilation, n_heads, name):
    _, t_len, w = qkvz.shape
    d = dilation
    sub_len = t_len // d
    nb = sub_len // ATTN_BLOCK
    view = qkvz.reshape(4, sub_len, d * w)
    out_v, lse_v, dout_v = (a.reshape(sub_len, d * w) for a in (out, lse, dout))
    scale = HEAD_DIM ** -0.5
    n_pairs = w // LANES

    def body(q_ref, kp_ref, kc_ref, vp_ref, vc_ref, out_ref, lse_ref, do_ref,
             dq_ref, dk_ref, dv_ref, dk_carry, dv_carry):
        n = pl.program_id(1)
        first_block = n == 0
        valid_cur, valid_prev, delta_cur, delta_prev = _attn_masks(first_block)
        masks = _head_lane_masks()

        @pl.when(first_block)
        def _():
            dk_carry[...] = jnp.zeros_like(dk_carry)
            dv_carry[...] = jnp.zeros_like(dv_carry)

        @pl.when(n < nb)
        def _():
            for p in range(n_pairs):
                cols = slice(p * LANES, (p + 1) * LANES)
                q = (q_ref[:, cols] * scale).astype(BF16)
                kp, kc = kp_ref[:, cols].astype(BF16), kc_ref[:, cols].astype(BF16)
                vp, vc = vp_ref[:, cols].astype(BF16), vc_ref[:, cols].astype(BF16)
                do = do_ref[:, cols]
                dob = do.astype(BF16)
                do_out = do * out_ref[:, cols]
                lse_all = lse_ref[:, cols]
                dq = jnp.zeros((ATTN_BLOCK, LANES), F32)
                dk_cur = jnp.zeros((ATTN_BLOCK, LANES), F32)
                dk_prev = jnp.zeros((ATTN_BLOCK, LANES), F32)
                dv_cur = jnp.zeros((ATTN_BLOCK, LANES), F32)
                dv_prev = jnp.zeros((ATTN_BLOCK, LANES), F32)
                for hh in range(2):
                    slope = _alibi_slope(2 * p + hh, n_heads) * d
                    msk = masks[hh]
                    qh = jnp.where(msk, q, jnp.zeros_like(q))
                    doh = jnp.where(msk, dob, jnp.zeros_like(dob))
                    delta = jnp.sum(jnp.where(msk, do_out, 0.0), axis=1, keepdims=True)
                    lse_h = jnp.max(jnp.where(msk, lse_all, NEG_INF), axis=1, keepdims=True)
                    s_cur = jnp.where(valid_cur, _dot(qh, kc, 1, 1) - slope * delta_cur, NEG_INF)
                    s_prev = jnp.where(valid_prev, _dot(qh, kp, 1, 1) - slope * delta_prev, NEG_INF)
                    p_cur, p_prev = jnp.exp(s_cur - lse_h), jnp.exp(s_prev - lse_h)
                    ds_cur = (p_cur * (_dot(doh, vc, 1, 1) - delta)).astype(BF16)
                    ds_prev = (p_prev * (_dot(doh, vp, 1, 1) - delta)).astype(BF16)
                    kch = jnp.where(msk, kc, jnp.zeros_like(kc))
                    kph = jnp.where(msk, kp, jnp.zeros_like(kp))
                    dq = dq + _dot(ds_cur, kch, 1, 0) + _dot(ds_prev, kph, 1, 0)
                    dk_cur = dk_cur + _dot(ds_cur, qh, 0, 0)
                    dk_prev = dk_prev + _dot(ds_prev, qh, 0, 0)
                    dv_cur = dv_cur + _dot(p_cur.astype(BF16), doh, 0, 0)
                    dv_prev = dv_prev + _dot(p_prev.astype(BF16), doh, 0, 0)
                dq_ref[:, cols] = dq * scale
                dk_ref[:, cols] = dk_carry[:, cols] + dk_prev
                dv_ref[:, cols] = dv_carry[:, cols] + dv_prev
                dk_carry[:, cols] = dk_cur
                dv_carry[:, cols] = dv_cur

        @pl.when(n == nb)
        def _():
            dk_ref[...] = dk_carry[...]
            dv_ref[...] = dv_carry[...]

    def qkv_spec(which, shift):
        return pl.BlockSpec((None, ATTN_BLOCK, w),
                            lambda r, n: (which, jnp.clip(n - shift, 0, nb - 1), r))

    q_like = pl.BlockSpec((ATTN_BLOCK, w), lambda r, n: (jnp.minimum(n, nb - 1), r))
    k_like = pl.BlockSpec((ATTN_BLOCK, w), lambda r, n: (jnp.maximum(n - 1, 0), r))
    dq, dk, dv = pl.pallas_call(
        body, name=name, grid=(d, nb + 1),
        out_shape=[jax.ShapeDtypeStruct((sub_len, d * w), F32)] * 3,
        in_specs=[qkv_spec(0, 0), qkv_spec(1, 1), qkv_spec(1, 0), qkv_spec(2, 1), qkv_spec(2, 0),
                  q_like, q_like, q_like],
        out_specs=[q_like, k_like, k_like],
        scratch_shapes=[pltpu.VMEM((ATTN_BLOCK, w), F32), pltpu.VMEM((ATTN_BLOCK, w), F32)],
        compiler_params=_params("parallel", "arbitrary"),
    )(view, view, view, view, view, out_v, lse_v, dout_v)
    return dq.reshape(t_len, w), dk.reshape(t_len, w), dv.reshape(t_len, w)


def _shift_down(u, s):
    if s == 0:
        return u
    row = lax.broadcasted_iota(jnp.int32, u.shape, 0)
    return jnp.where(row >= s, pltpu.roll(u, s, 0), 0.0)


def _shift_up(u, s):
    if s == 0:
        return u
    n = u.shape[0]
    row = lax.broadcasted_iota(jnp.int32, u.shape, 0)
    return jnp.where(row < n - s, pltpu.roll(u, n - s, 0), 0.0)


def _conv_fwd(u, w, b, name):
    t_len, ch = u.shape

    def body(u_ref, w_ref, b_ref, o_ref):
        uv = u_ref[...]
        pre = b_ref[...] + jnp.zeros_like(uv)
        for k in range(SSD_CONV):
            pre = pre + w_ref[k:k + 1, :] * _shift_down(uv, SSD_CONV - 1 - k)
        o_ref[...] = pre * jax.nn.sigmoid(pre)

    return pl.pallas_call(
        body, name=name, grid=(ch // LANES,),
        out_shape=jax.ShapeDtypeStruct((t_len, ch), F32),
        in_specs=[pl.BlockSpec((t_len, LANES), lambda j: (0, j)), pl.BlockSpec((SSD_CONV, LANES), lambda j: (0, j)),
                  pl.BlockSpec((1, LANES), lambda j: (0, j))],
        out_specs=pl.BlockSpec((t_len, LANES), lambda j: (0, j)),
        compiler_params=_params("parallel"),
    )(u, w, b)


def _conv_bwd(u, w, b, dact, name):
    t_len, ch = u.shape

    def body(u_ref, w_ref, b_ref, da_ref, du_ref, dw_ref, db_ref):
        uv = u_ref[...]
        shifted = [_shift_down(uv, SSD_CONV - 1 - k) for k in range(SSD_CONV)]
        pre = b_ref[...] + jnp.zeros_like(uv)
        for k in range(SSD_CONV):
            pre = pre + w_ref[k:k + 1, :] * shifted[k]
        sig = jax.nn.sigmoid(pre)
        dpre = da_ref[...] * (sig * (1.0 + pre * (1.0 - sig)))
        du = jnp.zeros_like(uv)
        for k in range(SSD_CONV):
            du = du + w_ref[k:k + 1, :] * _shift_up(dpre, SSD_CONV - 1 - k)
            dw_ref[k:k + 1, :] = jnp.sum(dpre * shifted[k], axis=0, keepdims=True)
        du_ref[...] = du
        db_ref[...] = jnp.sum(dpre, axis=0, keepdims=True)

    col = pl.BlockSpec((t_len, LANES), lambda j: (0, j))
    w_spec = pl.BlockSpec((SSD_CONV, LANES), lambda j: (0, j))
    b_spec = pl.BlockSpec((1, LANES), lambda j: (0, j))
    return pl.pallas_call(
        body, name=name, grid=(ch // LANES,),
        out_shape=[jax.ShapeDtypeStruct((t_len, ch), F32), jax.ShapeDtypeStruct((SSD_CONV, ch), F32),
                   jax.ShapeDtypeStruct((1, ch), F32)],
        in_specs=[col, w_spec, b_spec, col], out_specs=[col, w_spec, b_spec],
        compiler_params=_params("parallel"),
    )(u, w, b, dact)


def _cumsum_rows(v):
    n = v.shape[0]
    row = lax.broadcasted_iota(jnp.int32, v.shape, 0)
    s = 1
    while s < n:
        v = v + jnp.where(row >= s, pltpu.roll(v, s, 0), 0.0)
        s *= 2
    return v


def _rev_cumsum_rows(v):
    n = v.shape[0]
    row = lax.broadcasted_iota(jnp.int32, v.shape, 0)
    s = 1
    while s < n:
        v = v + jnp.where(row < n - s, pltpu.roll(v, n - s, 0), 0.0)
        s *= 2
    return v


def _head_selector(heads, width):
    j = lax.broadcasted_iota(jnp.int32, (LANES, width), 0)
    lane = lax.broadcasted_iota(jnp.int32, (LANES, width), 1)
    return jnp.where(jnp.logical_and(lane // HEAD_DIM == j, j < heads), 1.0, 0.0).astype(BF16)


class _SsdChunk:
    def __init__(self, dtraw_ref, bias_ref, alog_ref, xs_ref, b_ref, c_ref, heads):
        q = SSD_CHUNK
        width = heads * HEAD_DIM
        lane = lax.broadcasted_iota(jnp.int32, (q, LANES), 1)
        self.head_lanes = lane < heads
        lane1 = lax.broadcasted_iota(jnp.int32, (1, LANES), 1)
        self.a = jnp.where(lane1 < heads, -jnp.exp(alog_ref[...]), 0.0)
        self.dt_arg = dtraw_ref[...] + bias_ref[...]
        self.dt = jnp.where(self.head_lanes, jax.nn.softplus(self.dt_arg), 0.0)
        self.cum = _cumsum_rows(self.dt * self.a)
        self.cum_t = self.cum.T
        last = self.cum[q - 1:q, :]
        self.sel = _head_selector(heads, width)
        self.expand = lambda v: _dot_exact(v, self.sel, 1, 0)
        self.segsum = lambda v: _dot_exact(v, self.sel, 1, 1)
        self.e_exp = self.expand(jnp.exp(self.cum))
        self.d_exp = self.expand(jnp.exp(last - self.cum))
        self.elast_exp = self.e_exp[q - 1:q, :]
        self.dt_exp = self.expand(self.dt)
        self.xs = xs_ref[...]
        self.x = self.xs * self.dt_exp
        self.xb = self.x.astype(BF16)
        self.bb = b_ref[...].astype(BF16)
        self.cb = c_ref[...].astype(BF16)
        self.cbm = _dot(self.cb, self.bb, 1, 1)
        li = lax.broadcasted_iota(jnp.int32, (q, q), 0)
        si = lax.broadcasted_iota(jnp.int32, (q, q), 1)
        self.tri = li >= si
        hl = lax.broadcasted_iota(jnp.int32, (q, LANES), 1)
        self.pair_masks = [hl < HEAD_DIM, hl >= HEAD_DIM]

    def decay(self, j):
        diff = self.cum[:, j:j + 1] - self.cum_t[j:j + 1, :]
        return jnp.exp(jnp.where(self.tri, diff, NEG_INF))


def _ssd_specs(t_len, heads, n_chunks, xbc_cols, rev):
    q, gw = SSD_CHUNK, heads * HEAD_DIM
    ssd_w = SSD_GROUPS * gw
    b_blk = ssd_w // SSD_STATE
    ch = (lambda c: n_chunks - 1 - c) if rev else (lambda c: c)
    return dict(
        dtraw=pl.BlockSpec((None, q, LANES), lambda g, c: (g, ch(c), 0)),
        small=pl.BlockSpec((None, 1, LANES), lambda g, c: (g, 0, 0)),
        dsk=pl.BlockSpec((None, 1, gw), lambda g, c: (g, 0, 0)),
        xs=pl.BlockSpec((q, gw), lambda g, c: (ch(c), g)),
        b=pl.BlockSpec((q, SSD_STATE), lambda g, c: (ch(c), b_blk + g)),
        c=pl.BlockSpec((q, SSD_STATE), lambda g, c: (ch(c), b_blk + SSD_GROUPS + g)),
        z=pl.BlockSpec((None, q, gw), lambda g, c: (3, ch(c), g)),
        tok=pl.BlockSpec((q, gw), lambda g, c: (ch(c), g)),
        state=pl.BlockSpec((None, SSD_STATE, gw), lambda g, c: (ch(c), 0, g)),
        bc=pl.BlockSpec((q, SSD_STATE), lambda g, c: (ch(c), g)),
    )


def _ssd_fwd(xbc, qkvz, dtraw_g, bias_g, alog_g, dsk_exp, heads, name):
    t_len = xbc.shape[0]
    q, gw = SSD_CHUNK, heads * HEAD_DIM
    n_chunks = t_len // q
    ssd_w = SSD_GROUPS * gw
    sp = _ssd_specs(t_len, heads, n_chunks, xbc.shape[1], rev=False)

    def body(dtraw_ref, bias_ref, alog_ref, dsk_ref, xs_ref, b_ref, c_ref, z_ref,
             yg_ref, ypre_ref, st_ref, s_scr):
        @pl.when(pl.program_id(1) == 0)
        def _():
            s_scr[...] = jnp.zeros_like(s_scr)

        k = _SsdChunk(dtraw_ref, bias_ref, alog_ref, xs_ref, b_ref, c_ref, heads)
        s_prev = s_scr[...]
        st_ref[...] = s_prev
        y_off = k.e_exp * _dot(k.cb, s_prev.astype(BF16), 1, 0)
        parts = []
        for p in range(heads // 2):
            xp = k.xb[:, p * LANES:(p + 1) * LANES]
            acc = jnp.zeros((q, LANES), F32)
            for hh in range(2):
                m = (k.cbm * k.decay(2 * p + hh)).astype(BF16)
                acc = acc + _dot(m, jnp.where(k.pair_masks[hh], xp, jnp.zeros_like(xp)), 1, 0)
            parts.append(acc)
        y = jnp.concatenate(parts, axis=1) + y_off
        xd = (k.x * k.d_exp).astype(BF16)
        s_scr[...] = k.elast_exp * s_prev + _dot(k.bb, xd, 0, 0)
        y_pre = y + dsk_ref[...] * k.xs
        zv = z_ref[...]
        ypre_ref[...] = y_pre
        yg_ref[...] = y_pre * (zv * jax.nn.sigmoid(zv))

    return pl.pallas_call(
        body, name=name, grid=(SSD_GROUPS, n_chunks),
        out_shape=[jax.ShapeDtypeStruct((t_len, ssd_w), F32), jax.ShapeDtypeStruct((t_len, ssd_w), F32),
                   jax.ShapeDtypeStruct((n_chunks, SSD_STATE, ssd_w), F32)],
        in_specs=[sp["dtraw"], sp["small"], sp["small"], sp["dsk"], sp["xs"], sp["b"], sp["c"], sp["z"]],
        out_specs=[sp["tok"], sp["tok"], sp["state"]],
        scratch_shapes=[pltpu.VMEM((SSD_STATE, gw), F32)],
        compiler_params=_params("parallel", "arbitrary"),
    )(dtraw_g, bias_g, alog_g, dsk_exp, xbc, xbc, xbc, qkvz)


def _ssd_bwd(xbc, qkvz, dtraw_g, bias_g, alog_g, dsk_exp, ypre, states, dyg, heads, name):
    t_len = xbc.shape[0]
    q, gw = SSD_CHUNK, heads * HEAD_DIM
    n_chunks = t_len // q
    ssd_w = SSD_GROUPS * gw
    sp = _ssd_specs(t_len, heads, n_chunks, xbc.shape[1], rev=True)

    def body(dtraw_ref, bias_ref, alog_ref, dsk_ref, xs_ref, b_ref, c_ref, z_ref, ypre_ref, st_ref, dyg_ref,
             dxs_ref, db_ref, dc_ref, dz_ref, ddt_ref, small_ref, g_scr):
        first = pl.program_id(1) == 0

        @pl.when(first)
        def _():
            g_scr[...] = jnp.zeros_like(g_scr)

        k = _SsdChunk(dtraw_ref, bias_ref, alog_ref, xs_ref, b_ref, c_ref, heads)
        zv = z_ref[...]
        sig = jax.nn.sigmoid(zv)
        dyg = dyg_ref[...]
        y_pre = ypre_ref[...]
        dy = dyg * (zv * sig)
        dz_ref[...] = dyg * y_pre * (sig * (1.0 + zv * (1.0 - sig)))
        dsk = dsk_ref[...]
        g_next = g_scr[...]
        s_prev = st_ref[...]
        sb = s_prev.astype(BF16)
        xd = k.x * k.d_exp
        xdb = xd.astype(BF16)
        gb = g_next.astype(BF16)
        dx_off = k.d_exp * _dot(k.bb, gb, 1, 0)
        dyb = dy.astype(BF16)
        dcb = jnp.zeros((q, q), F32)
        lane = lax.broadcasted_iota(jnp.int32, (q, LANES), 1)
        row_t = lax.broadcasted_iota(jnp.int32, (LANES, q), 0)
        w_rows = jnp.zeros((q, LANES), F32)
        w_cols_t = jnp.zeros((LANES, q), F32)
        parts = []
        for p in range(heads // 2):
            cols = slice(p * LANES, (p + 1) * LANES)
            dyp, xp = dyb[:, cols], k.xb[:, cols]
            acc = jnp.zeros((q, LANES), F32)
            for hh in range(2):
                j = 2 * p + hh
                lm = k.decay(j)
                m32 = k.cbm * lm
                dym = jnp.where(k.pair_masks[hh], dyp, jnp.zeros_like(dyp))
                acc = acc + _dot(m32.astype(BF16), dym, 0, 0)
                dm = _dot(dym, xp, 1, 1)
                dcb = dcb + dm * lm
                wmat = dm * m32
                w_rows = w_rows + jnp.where(lane == j, jnp.sum(wmat, axis=1, keepdims=True), 0.0)
                w_cols_t = w_cols_t + jnp.where(row_t == j, jnp.sum(wmat, axis=0, keepdims=True), 0.0)
            parts.append(acc)
        dx = jnp.concatenate(parts, axis=1) + dx_off
        dcbb = dcb.astype(BF16)
        edy = (k.e_exp * dy).astype(BF16)
        dc_ref[...] = _dot(dcbb, k.bb, 1, 0) + _dot(edy, sb, 1, 1)
        db_ref[...] = _dot(dcbb, k.cb, 0, 0) + _dot(xdb, gb, 1, 1)
        g_scr[...] = k.elast_exp * g_next + _dot(k.cb, edy, 0, 0)

        y_off = k.e_exp * _dot(k.cb, sb, 1, 0)
        dcum = w_rows - w_cols_t.T + k.segsum(dy * y_off)
        t_term = k.segsum(k.x * dx_off)
        gs = jnp.broadcast_to(jnp.sum(g_next * s_prev, axis=0, keepdims=True), (8, gw))
        carried = k.segsum(gs)[0:1, :] * jnp.exp(k.cum[q - 1:q, :])
        dda = _rev_cumsum_rows(dcum) + (_cumsum_rows(t_term) - t_term) + carried
        ddt = jnp.where(k.head_lanes, dda * k.a + k.segsum(dx * k.xs), 0.0)
        ddtraw = ddt * jax.nn.sigmoid(k.dt_arg)
        ddt_ref[...] = ddtraw
        dxs_ref[...] = dx * k.dt_exp + dsk * dy
        ds = jnp.broadcast_to(jnp.sum(dy * k.xs, axis=0, keepdims=True), (8, gw))
        d_alog = jnp.sum(jnp.where(k.head_lanes, dda * k.dt, 0.0), axis=0, keepdims=True) * k.a
        rows8 = lax.broadcasted_iota(jnp.int32, (8, LANES), 0)
        small = jnp.where(rows8 == 0, d_alog, 0.0)
        small = small + jnp.where(rows8 == 1, jnp.sum(ddtraw, axis=0, keepdims=True), 0.0)
        small = small + jnp.where(rows8 == 2, k.segsum(ds)[0:1, :], 0.0)

        @pl.when(first)
        def _():
            small_ref[...] = small

        @pl.when(jnp.logical_not(first))
        def _():
            small_ref[...] += small

    bc_out = sp["bc"]
    return pl.pallas_call(
        body, name=name, grid=(SSD_GROUPS, n_chunks),
        out_shape=[jax.ShapeDtypeStruct((t_len, ssd_w), F32),
                   jax.ShapeDtypeStruct((t_len, SSD_GROUPS * SSD_STATE), F32),
                   jax.ShapeDtypeStruct((t_len, SSD_GROUPS * SSD_STATE), F32),
                   jax.ShapeDtypeStruct((t_len, ssd_w), F32),
                   jax.ShapeDtypeStruct((SSD_GROUPS, t_len, LANES), F32),
                   jax.ShapeDtypeStruct((SSD_GROUPS, 8, LANES), F32)],
        in_specs=[sp["dtraw"], sp["small"], sp["small"], sp["dsk"], sp["xs"], sp["b"], sp["c"], sp["z"],
                  sp["tok"], sp["state"], sp["tok"]],
        out_specs=[sp["tok"], bc_out, bc_out, sp["tok"], sp["dtraw"],
                   pl.BlockSpec((None, 8, LANES), lambda g, c: (g, 0, 0))],
        scratch_shapes=[pltpu.VMEM((SSD_STATE, gw), F32)],
        compiler_params=_params("parallel", "arbitrary"),
    )(dtraw_g, bias_g, alog_g, dsk_exp, xbc, xbc, xbc, qkvz, ypre, states, dyg)


def _adamw(w, g, m, v, name):
    rows, lanes = w.shape
    tr = _row_tile(rows, lanes, 4, 14)
    c1 = 1.0 / (1.0 - ADAM_B1 ** ADAM_STEP)
    c2 = 1.0 / (1.0 - ADAM_B2 ** ADAM_STEP)

    def body(w_ref, g_ref, m_ref, v_ref, d_ref, nm_ref, nv_ref):
        gv = g_ref[...]
        nm = ADAM_B1 * m_ref[...] + (1.0 - ADAM_B1) * gv
        nv = ADAM_B2 * v_ref[...] + (1.0 - ADAM_B2) * (gv * gv)
        nm_ref[...] = nm
        nv_ref[...] = nv
        d_ref[...] = -ADAM_LR * ((nm * c1) / (jnp.sqrt(nv * c2) + ADAM_EPS) + ADAM_WD * w_ref[...])

    spec = pl.BlockSpec((tr, lanes), lambda i: (i, 0))
    return pl.pallas_call(
        body, name=name, grid=(rows // tr,),
        out_shape=[jax.ShapeDtypeStruct((rows, lanes), F32)] * 3,
        in_specs=[spec] * 4, out_specs=[spec] * 3,
        compiler_params=_params("parallel"),
    )(w, g, m, v)


def _pad_lanes(a, width=LANES):
    return jnp.pad(a, ((0, 0), (0, width - a.shape[1])))


def _group_pad(v, heads):
    return _pad_lanes(v.reshape(SSD_GROUPS, heads))[:, None, :]


def _layer_fwd(x0, p, dims, tag):
    w_attn, heads_g, n_heads = dims["w_attn"], dims["heads_g"], dims["n_heads"]
    h_ssd = heads_g * SSD_GROUPS
    h1 = _rmsnorm_fwd([x0], [[x0.shape[1]]], p["ln1_g"], f"ln1_fwd{tag}")
    qkvz = _mm(h1, p["wa"], name=f"proj_qkvz{tag}", out_chunk=w_attn, tk=1024)
    xbc_raw = _mm(h1, p["wx"], name=f"proj_xbc{tag}", tk=1024)
    dt_raw = _mm(h1, p["wd"], name=f"proj_dt{tag}", tk=1024)

    outs, lses = [], []
    for d in BRANCH_DILATIONS:
        o, l = _attn_branch_fwd(qkvz, d, n_heads, f"attn_fwd_d{d}{tag}")
        outs.append(o)
        lses.append(l)
    attn, lse = _attn_combine(outs, lses, f"attn_combine{tag}")

    xbc = _conv_fwd(xbc_raw, p["conv_w"], p["conv_b"], f"conv_fwd{tag}")
    dtraw_g = jnp.stack([_pad_lanes(dt_raw[:, g * heads_g:(g + 1) * heads_g]) for g in range(SSD_GROUPS)])
    bias_g, alog_g = _group_pad(p["dt_bias"], heads_g), _group_pad(p["a_log"], heads_g)
    dsk_exp = jnp.repeat(p["d_skip"], HEAD_DIM).reshape(SSD_GROUPS, 1, heads_g * HEAD_DIM)
    yg, ypre, states = _ssd_fwd(xbc, qkvz, dtraw_g, bias_g, alog_g, dsk_exp, heads_g, f"ssd_fwd{tag}")

    gw = heads_g * HEAD_DIM
    mix_g = jnp.concatenate([p["attn_norm_g"], p["ssd_norm_g"]])[None, :]
    mix = _rmsnorm_fwd([attn, yg], [[w_attn], [gw] * SSD_GROUPS], mix_g, f"mix_norm_fwd{tag}")
    x1 = _mm(mix, p["wo"], name=f"out_proj{tag}", residual=x0)
    h2 = _rmsnorm_fwd([x1], [[x1.shape[1]]], p["ln2_g"], f"ln2_fwd{tag}")
    u = _mm(h2, p["wmi"], name=f"mlp_in{tag}", out_dtype=BF16, tk=1024)
    x2 = _mm(u, p["wmo"], name=f"mlp_out{tag}", a_act="relu2", residual=x1)
    saved = dict(x0=x0, h1=h1, qkvz=qkvz, xbc_raw=xbc_raw, attn=attn, lse=lse, xbc=xbc, dtraw_g=dtraw_g,
                 bias_g=bias_g, alog_g=alog_g, dsk_exp=dsk_exp, yg=yg, ypre=ypre, states=states, mix=mix,
                 mix_g=mix_g, x1=x1, h2=h2, u=u)
    return x2, saved


def _layer_bwd(dx2, p, s, dims, tag):
    w_attn, heads_g, n_heads = dims["w_attn"], dims["heads_g"], dims["n_heads"]
    d_model = dx2.shape[1]
    gw = heads_g * HEAD_DIM
    du = _mm(dx2, p["wmo"], name=f"mlp_out_dx{tag}", tb=True, gate=s["u"], out_dtype=BF16, tk=1024)
    d_wmo = _mm(s["u"], dx2, name=f"mlp_out_dw{tag}", ta=True, a_act="relu2")
    d_wmi = _mm(s["h2"], du, name=f"mlp_in_dw{tag}", ta=True)
    dh2 = _mm(du, p["wmi"], name=f"mlp_in_dx{tag}", tb=True, tk=1024)
    (dx1,), d_ln2 = _rmsnorm_bwd([s["x1"]], [[d_model]], p["ln2_g"], dh2, [dx2], f"ln2_bwd{tag}")
    dmix = _mm(dx1, p["wo"], name=f"out_proj_dx{tag}", tb=True, tk=1024)
    d_wo = _mm(s["mix"], dx1, name=f"out_proj_dw{tag}", ta=True)
    (dattn, dyg), d_mix_g = _rmsnorm_bwd([s["attn"], s["yg"]], [[w_attn], [gw] * SSD_GROUPS], s["mix_g"], dmix,
                                        [None, None], f"mix_norm_bwd{tag}")
    dxs, db, dc, dz, ddtraw_g, ssd_small = _ssd_bwd(
        s["xbc"], s["qkvz"], s["dtraw_g"], s["bias_g"], s["alog_g"], s["dsk_exp"], s["ypre"], s["states"], dyg,
        heads_g, f"ssd_bwd{tag}")
    dxbc = jnp.concatenate([dxs, db, dc], axis=1)
    dxbc_raw, d_conv_w, d_conv_b = _conv_bwd(s["xbc_raw"], p["conv_w"], p["conv_b"], dxbc, f"conv_bwd{tag}")
    ddt_raw = _pad_lanes(jnp.concatenate([ddtraw_g[g, :, :heads_g] for g in range(SSD_GROUPS)], axis=1))
    dq = dk = dv = None
    for d in BRANCH_DILATIONS:
        bq, bk, bv = _attn_branch_bwd(s["qkvz"], s["attn"], s["lse"], dattn, d, n_heads, f"attn_bwd_d{d}{tag}")
        dq, dk, dv = (bq, bk, bv) if dq is None else (dq + bq, dk + bk, dv + bv)
    dqkvz = jnp.stack([dq, dk, dv, dz]).astype(BF16)
    d_wa = _mm(s["h1"], dqkvz, name=f"proj_qkvz_dw{tag}", ta=True, b_chunk=w_attn)
    d_wx = _mm(s["h1"], dxbc_raw, name=f"proj_xbc_dw{tag}", ta=True)
    d_wd = _mm(s["h1"], ddt_raw, name=f"proj_dt_dw{tag}", ta=True)
    dh1 = _mm(dqkvz, p["wa"], name=f"proj_qkvz_dx{tag}", tb=True, a_chunk=w_attn, tk=1024)
    dh1 = _mm(dxbc_raw, p["wx"], name=f"proj_xbc_dx{tag}", tb=True, residual=dh1, tk=1536)
    dh1 = _mm(ddt_raw, p["wd"], name=f"proj_dt_dx{tag}", tb=True, residual=dh1)
    (dx0,), d_ln1 = _rmsnorm_bwd([s["x0"]], [[d_model]], p["ln1_g"], dh1, [dx1], f"ln1_bwd{tag}")

    h_ssd = heads_g * SSD_GROUPS
    small = ssd_small[:, :, :heads_g]
    grads = dict(
        ln1_g=d_ln1[0], conv_w=d_conv_w, conv_b=d_conv_b[0],
        a_log=small[:, 0].reshape(h_ssd), dt_bias=small[:, 1].reshape(h_ssd), d_skip=small[:, 2].reshape(h_ssd),
        attn_norm_g=d_mix_g[0, :w_attn], ssd_norm_g=d_mix_g[0, w_attn:], ln2_g=d_ln2[0],
        w_in=jnp.concatenate([d_wa, d_wx, d_wd[:, :h_ssd]], axis=1), w_out=d_wo, w_mlp_in=d_wmi, w_mlp_out=d_wmo)
    return dx0, grads


_SMALL = ["ln1_g", "conv_w", "conv_b", "dt_bias", "a_log", "d_skip", "attn_norm_g", "ssd_norm_g", "ln2_g"]
_WEIGHTS = ["ln1_g", "w_in", "conv_w", "conv_b", "dt_bias", "a_log", "d_skip", "attn_norm_g", "ssd_norm_g",
            "w_out", "ln2_g", "w_mlp_in", "w_mlp_out", "final_norm_g"]


def _to_rows(a):
    flat = a.reshape(-1)
    rows = -(-flat.shape[0] // LANES)
    rows = -(-rows // 8) * 8
    return jnp.pad(flat, (0, rows * LANES - flat.shape[0])).reshape(rows, LANES)


def kernel(x, ln1_g, w_in, conv_w, conv_b, dt_bias, a_log, d_skip, attn_norm_g, ssd_norm_g, w_out, ln2_g, w_mlp_in, w_mlp_out, final_norm_g, loss_target, m_ln1_g, m_w_in, m_conv_w, m_conv_b, m_dt_bias, m_a_log, m_d_skip, m_attn_norm_g, m_ssd_norm_g, m_w_out, m_ln2_g, m_w_mlp_in, m_w_mlp_out, m_final_norm_g, v_ln1_g, v_w_in, v_conv_w, v_conv_b, v_dt_bias, v_a_log, v_d_skip, v_attn_norm_g, v_ssd_norm_g, v_w_out, v_ln2_g, v_w_mlp_in, v_w_mlp_out, v_final_norm_g):
    w = dict(ln1_g=ln1_g, w_in=w_in, conv_w=conv_w, conv_b=conv_b, dt_bias=dt_bias, a_log=a_log, d_skip=d_skip,
             attn_norm_g=attn_norm_g, ssd_norm_g=ssd_norm_g, w_out=w_out, ln2_g=ln2_g, w_mlp_in=w_mlp_in,
             w_mlp_out=w_mlp_out, final_norm_g=final_norm_g)
    mom = dict(ln1_g=m_ln1_g, w_in=m_w_in, conv_w=m_conv_w, conv_b=m_conv_b, dt_bias=m_dt_bias, a_log=m_a_log,
               d_skip=m_d_skip, attn_norm_g=m_attn_norm_g, ssd_norm_g=m_ssd_norm_g, w_out=m_w_out, ln2_g=m_ln2_g,
               w_mlp_in=m_w_mlp_in, w_mlp_out=m_w_mlp_out, final_norm_g=m_final_norm_g)
    var = dict(ln1_g=v_ln1_g, w_in=v_w_in, conv_w=v_conv_w, conv_b=v_conv_b, dt_bias=v_dt_bias, a_log=v_a_log,
               d_skip=v_d_skip, attn_norm_g=v_attn_norm_g, ssd_norm_g=v_ssd_norm_g, w_out=v_w_out, ln2_g=v_ln2_g,
               w_mlp_in=v_w_mlp_in, w_mlp_out=v_w_mlp_out, final_norm_g=v_final_norm_g)

    depth, d_model = ln1_g.shape
    t_len = x.shape[1]
    w_attn = attn_norm_g.shape[1]
    h_ssd = dt_bias.shape[1]
    conv_ch = conv_b.shape[1]
    in_proj = w_in.shape[2] * N_DEV
    assert ssd_norm_g.shape[1] == w_attn and in_proj == 4 * w_attn + conv_ch + h_ssd
    assert t_len % (BRANCH_DILATIONS[-1] * ATTN_BLOCK) == 0 and h_ssd % (2 * SSD_GROUPS) == 0
    dims = dict(w_attn=w_attn, heads_g=h_ssd // SSD_GROUPS, n_heads=w_attn // HEAD_DIM)

    g_in, g_out, g_mi, g_mo, g_cw = _all_gather(
        [w_in.astype(BF16), w_out.astype(BF16), w_mlp_in.astype(BF16), w_mlp_out.astype(BF16), conv_w],
        "gather_weights")
    full_in = g_in.transpose(1, 2, 0, 3).reshape(depth, d_model, in_proj)
    full_out = g_out.transpose(1, 0, 2, 3).reshape(depth, w_out.shape[1] * N_DEV, d_model)
    full_mi = g_mi.transpose(1, 2, 0, 3).reshape(depth, d_model, w_mlp_in.shape[2] * N_DEV)
    full_mo = g_mo.transpose(1, 0, 2, 3).reshape(depth, w_mlp_out.shape[1] * N_DEV, d_model)
    full_cw = g_cw.transpose(1, 2, 0, 3).reshape(depth, SSD_CONV, conv_ch)

    layers = []
    for l in range(depth):
        layers.append(dict(
            ln1_g=ln1_g[l][None, :], ln2_g=ln2_g[l][None, :],
            wa=full_in[l, :, :4 * w_attn], wx=full_in[l, :, 4 * w_attn:4 * w_attn + conv_ch],
            wd=_pad_lanes(full_in[l, :, 4 * w_attn + conv_ch:]),
            conv_w=full_cw[l], conv_b=conv_b[l][None, :], dt_bias=dt_bias[l], a_log=a_log[l], d_skip=d_skip[l],
            attn_norm_g=attn_norm_g[l], ssd_norm_g=ssd_norm_g[l], wo=full_out[l], wmi=full_mi[l], wmo=full_mo[l]))

    h = x[0]
    saved = []
    for l in range(depth):
        h, s = _layer_fwd(h, layers[l], dims, f"_l{l}")
        saved.append(s)
    dh, d_final_g, loss_part = _loss_head(h, final_norm_g[None, :], loss_target[0], "loss_head")

    grads = [None] * depth
    for l in reversed(range(depth)):
        dh, grads[l] = _layer_bwd(dh, layers[l], saved[l], dims, f"_l{l}")
    grad_x = dh[None]

    def blocks(name, axis):
        g = jnp.stack([grads[l][name] for l in range(depth)])
        parts = jnp.split(g, N_DEV, axis=axis)
        return jnp.stack(parts).astype(BF16)

    big = dict(w_in=blocks("w_in", 2), w_out=blocks("w_out", 1), w_mlp_in=blocks("w_mlp_in", 2),
               w_mlp_out=blocks("w_mlp_out", 1))
    names = list(big)
    recv = _exchange_blocks([big[n].reshape(N_DEV, -1, w[n].shape[-1]) for n in names], "exchange_grads")
    gsum = {n: _sum_devices(r, f"sum_{n}").reshape(w[n].shape) for n, r in zip(names, recv)}

    small_parts = [jnp.stack([grads[l][n] for l in range(depth)]).reshape(-1) for n in _SMALL]
    small_parts += [d_final_g.reshape(-1), loss_part[0, :1]]
    sizes = [int(a.shape[0]) for a in small_parts]
    packed = _to_rows(jnp.concatenate(small_parts))
    (gathered,) = _all_gather([packed], "gather_small_grads")
    total = _sum_devices(gathered, "sum_small_grads").reshape(-1)
    offs = np.cumsum([0] + sizes)
    pieces = [total[offs[i]:offs[i + 1]] for i in range(len(sizes))]
    for n, piece in zip(_SMALL, pieces):
        shape = (depth, SSD_CONV, conv_ch) if n == "conv_w" else w[n].shape
        gsum[n] = piece.reshape(shape)
    gsum["final_norm_g"] = pieces[len(_SMALL)]
    loss = pieces[len(_SMALL) + 1][0]
    my_id = 4 * lax.axis_index("x") + 2 * lax.axis_index("y") + lax.axis_index("c")
    cw = conv_w.shape[2]
    gsum["conv_w"] = lax.dynamic_slice_in_dim(gsum["conv_w"], my_id * cw, cw, axis=2)

    delta, new_m, new_v = {}, {}, {}
    for n in names:
        outs = _adamw(*(a.reshape(-1, w[n].shape[-1]) for a in (w[n], gsum[n], mom[n], var[n])), f"adamw_{n}")
        delta[n], new_m[n], new_v[n] = (o.reshape(w[n].shape) for o in outs)
    small_names = [n for n in _WEIGHTS if n not in names]
    sm_sizes = [int(np.prod(w[n].shape)) for n in small_names]
    pack = lambda d: _to_rows(jnp.concatenate([d[n].reshape(-1) for n in small_names]))
    outs = _adamw(pack(w), pack(gsum), pack(mom), pack(var), "adamw_small")
    sm_offs = np.cumsum([0] + sm_sizes)
    for res, o in zip((delta, new_m, new_v), outs):
        flat = o.reshape(-1)
        for i, n in enumerate(small_names):
            res[n] = flat[sm_offs[i]:sm_offs[i + 1]].reshape(w[n].shape)

    return (loss, grad_x, *[gsum[n] for n in _WEIGHTS], *[delta[n] for n in _WEIGHTS],
            *[new_m[n] for n in _WEIGHTS], *[new_v[n] for n in _WEIGHTS])
```

```python
import functools
import math

import numpy as np
import jax
import jax.numpy as jnp
from jax import lax
from jax.experimental import pallas as pl
from jax.experimental.pallas import tpu as pltpu

F32 = jnp.float32
BF16 = jnp.bfloat16

N_DEV = 8
LANES = 128
HEAD_DIM = 64
ATTN_BLOCK = 128
BRANCH_DILATIONS = (1, 4, 16)
SSD_GROUPS = 2
SSD_STATE = 128
SSD_CHUNK = 128
SSD_CONV = 4
NORM_EPS = 1e-5
ADAM_LR, ADAM_B1, ADAM_B2, ADAM_EPS, ADAM_WD, ADAM_STEP = 0.001, 0.9, 0.999, 1e-08, 0.01, 10
VMEM_LIMIT_BYTES = 56 * 1024 * 1024
MESH = pl.DeviceIdType.MESH
NEG_INF = float("-inf")


def _params(*sem):
    return pltpu.CompilerParams(dimension_semantics=tuple(sem), vmem_limit_bytes=VMEM_LIMIT_BYTES)


def _pick(n, target, mult):
    best = None
    for t in range(mult, min(n, target) + 1, mult):
        if n % t == 0:
            best = t
    assert best is not None, (n, target, mult)
    return best


def _dot(a, b, ca, cb):
    return lax.dot_general(a, b, (((ca,), (cb,)), ((), ())), preferred_element_type=F32)


def _split3(v):
    hi = v.astype(BF16)
    r = v - hi.astype(F32)
    mid = r.astype(BF16)
    lo = (r - mid.astype(F32)).astype(BF16)
    return hi, mid, lo


def _dot_exact(v, sel, ca, cb):
    hi, mid, lo = _split3(v)
    return _dot(hi, sel, ca, cb) + _dot(mid, sel, ca, cb) + _dot(lo, sel, ca, cb)


_HBM = pl.BlockSpec(memory_space=pltpu.HBM)


def _all_gather(xs, name):
    n = len(xs)

    def body(*refs):
        x_refs, o_refs = refs[:n], refs[n:2 * n]
        send_sems, recv_sems, local_sems = refs[2 * n:]
        x, y, c = lax.axis_index("x"), lax.axis_index("y"), lax.axis_index("c")
        me, sibling = (x, y, c), (x, y, 1 - c)
        chips = [(1 - x, y), (x, 1 - y), (1 - x, 1 - y)]

        def copy(t, k, block, to, src=None):
            bx, by, bc = block
            dst = o_refs[t].at[4 * bx + 2 * by + bc]
            return pltpu.make_async_remote_copy(
                src_ref=dst if src is None else src, dst_ref=dst,
                send_sem=send_sems.at[t, k], recv_sem=recv_sems.at[t, k],
                device_id=to, device_id_type=MESH)

        mine = [pltpu.make_async_copy(x_refs[t], o_refs[t].at[4 * x + 2 * y + c], local_sems.at[t])
                for t in range(n)]
        first, passed = [], []
        for t in range(n):
            mine[t].start()
            cps = [copy(t, 0, me, sibling, src=x_refs[t])]
            cps += [copy(t, 1 + j, me, (*chip, c), src=x_refs[t]) for j, chip in enumerate(chips)]
            for cp in cps:
                cp.start()
            first += cps
        for t in range(n):
            for j, chip in enumerate(chips):
                copy(t, 1 + j, (*chip, c), me).wait_recv()
                fwd = copy(t, 4 + j, (*chip, c), sibling)
                fwd.start()
                passed.append(fwd)
        for t in range(n):
            copy(t, 0, sibling, me).wait_recv()
            for j, chip in enumerate(chips):
                copy(t, 4 + j, (*chip, 1 - c), me).wait_recv()
        for cp in first + passed:
            cp.wait_send()
        for t in range(n):
            mine[t].wait()

    return pl.pallas_call(
        body, name=name,
        out_shape=[jax.ShapeDtypeStruct((N_DEV,) + a.shape, a.dtype) for a in xs],
        in_specs=[_HBM] * n, out_specs=[_HBM] * n,
        scratch_shapes=[pltpu.SemaphoreType.DMA((n, 7)), pltpu.SemaphoreType.DMA((n, 7)),
                        pltpu.SemaphoreType.DMA((n,))],
    )(*xs)


def _exchange_blocks(xs, name):
    n = len(xs)

    def body(*refs):
        x_refs, o_refs = refs[:n], refs[n:2 * n]
        send_sems, recv_sems, local_sems = refs[2 * n:]
        x, y, c = lax.axis_index("x"), lax.axis_index("y"), lax.axis_index("c")
        my_id = 4 * x + 2 * y + c

        def peer(k):
            px = 1 - x if (k >> 2) & 1 else x
            py = 1 - y if (k >> 1) & 1 else y
            pc = 1 - c if k & 1 else c
            return (px, py, pc)

        def copy(t, k):
            px, py, pc = peer(k)
            pid = 4 * px + 2 * py + pc
            return pltpu.make_async_remote_copy(
                src_ref=x_refs[t].at[pid], dst_ref=o_refs[t].at[my_id],
                send_sem=send_sems.at[t, k - 1], recv_sem=recv_sems.at[t, k - 1],
                device_id=(px, py, pc), device_id_type=MESH)

        def landed(t, k):
            px, py, pc = peer(k)
            pid = 4 * px + 2 * py + pc
            return pltpu.make_async_remote_copy(
                src_ref=x_refs[t].at[pid], dst_ref=o_refs[t].at[pid],
                send_sem=send_sems.at[t, k - 1], recv_sem=recv_sems.at[t, k - 1],
                device_id=(px, py, pc), device_id_type=MESH)

        mine = [pltpu.make_async_copy(x_refs[t].at[my_id], o_refs[t].at[my_id], local_sems.at[t])
                for t in range(n)]
        for t in range(n):
            mine[t].start()
            for k in range(1, N_DEV):
                copy(t, k).start()
        for t in range(n):
            for k in range(1, N_DEV):
                landed(t, k).wait_recv()
        for t in range(n):
            for k in range(1, N_DEV):
                copy(t, k).wait_send()
            mine[t].wait()

    return pl.pallas_call(
        body, name=name,
        out_shape=[jax.ShapeDtypeStruct(a.shape, a.dtype) for a in xs],
        in_specs=[_HBM] * n, out_specs=[_HBM] * n,
        scratch_shapes=[pltpu.SemaphoreType.DMA((n, 7)), pltpu.SemaphoreType.DMA((n, 7)),
                        pltpu.SemaphoreType.DMA((n,))],
    )(*xs)


def _row_tile(rows, cols, itemsize, copies, budget=24 * 1024 * 1024):
    padded = -(-cols // LANES) * LANES
    mult = 8 * (4 // itemsize)
    if rows % mult:
        return rows
    return _pick(rows, max(mult, budget // (copies * padded * itemsize)), mult)


def _sum_devices(x, name):
    _, rows, cols = x.shape
    tr = _row_tile(rows, cols, x.dtype.itemsize, 2 * N_DEV + 4)

    def body(x_ref, o_ref):
        acc = x_ref[0].astype(F32)
        for s in range(1, N_DEV):
            acc = acc + x_ref[s].astype(F32)
        o_ref[...] = acc

    return pl.pallas_call(
        body, name=name, grid=(rows // tr,),
        out_shape=jax.ShapeDtypeStruct((rows, cols), F32),
        in_specs=[pl.BlockSpec((N_DEV, tr, cols), lambda i: (0, i, 0))],
        out_specs=pl.BlockSpec((tr, cols), lambda i: (i, 0)),
        compiler_params=_params("parallel"),
    )(x)


def _mm(a, b, *, name, ta=False, tb=False, tm=1024, tn=512, out_dtype=F32, a_act=None,
        residual=None, gate=None, out_chunk=None):
    k_dim, m = (a.shape if ta else a.shape[::-1])
    n, kb = (b.shape if tb else b.shape[::-1])
    assert kb == k_dim, (a.shape, b.shape, ta, tb)
    tm, tn = _pick(m, tm, 128), _pick(out_chunk or n, tn, 128)
    ca, cb = (0 if ta else 1), (1 if tb else 0)
    a_spec = pl.BlockSpec((k_dim, tm), lambda i, j: (0, i)) if ta else pl.BlockSpec((tm, k_dim), lambda i, j: (i, 0))
    b_spec = pl.BlockSpec((tn, k_dim), lambda i, j: (j, 0)) if tb else pl.BlockSpec((k_dim, tn), lambda i, j: (0, j))
    mn_spec = pl.BlockSpec((tm, tn), lambda i, j: (i, j))
    if out_chunk:
        per = out_chunk // tn
        o_spec = pl.BlockSpec((None, tm, tn), lambda i, j: (j // per, i, j % per))
        out_shape = jax.ShapeDtypeStruct((n // out_chunk, m, out_chunk), out_dtype)
    else:
        o_spec = mn_spec
        out_shape = jax.ShapeDtypeStruct((m, n), out_dtype)
    operands, in_specs = [a, b], [a_spec, b_spec]
    for extra in (gate, residual):
        if extra is not None:
            operands.append(extra)
            in_specs.append(mn_spec)

    def body(*refs):
        a_ref, b_ref, o_ref = refs[0], refs[1], refs[-1]
        extras = list(refs[2:-1])
        gate_ref = extras.pop(0) if gate is not None else None
        res_ref = extras.pop(0) if residual is not None else None
        av = a_ref[...].astype(BF16)
        if a_act == "relu2":
            av = jnp.square(jnp.maximum(av, jnp.zeros_like(av)))
        r = _dot(av, b_ref[...].astype(BF16), ca, cb)
        if gate_ref is not None:
            r = r * (2.0 * jnp.maximum(gate_ref[...].astype(F32), 0.0))
        if res_ref is not None:
            r = r + res_ref[...].astype(F32)
        o_ref[...] = r.astype(out_dtype)

    return pl.pallas_call(
        body, name=name, grid=(m // tm, n // tn), out_shape=out_shape,
        in_specs=in_specs, out_specs=o_spec,
        compiler_params=_params("parallel", "arbitrary"),
    )(*operands)


def _rmsnorm_fwd(xs, seg_widths, g, name, tm=256):
    t_len = xs[0].shape[0]
    width = sum(x.shape[1] for x in xs)
    tm = _pick(t_len, tm, 16)
    n = len(xs)

    def body(*refs):
        x_refs, g_ref, o_ref = refs[:n], refs[n], refs[n + 1]
        col = 0
        for x_ref, widths in zip(x_refs, seg_widths):
            off = 0
            for w in widths:
                xv = x_ref[:, off:off + w].astype(F32)
                r = lax.rsqrt(jnp.mean(xv * xv, axis=1, keepdims=True) + NORM_EPS)
                o_ref[:, col:col + w] = (xv * r * g_ref[:, col:col + w]).astype(BF16)
                off += w
                col += w

    return pl.pallas_call(
        body, name=name, grid=(t_len // tm,),
        out_shape=jax.ShapeDtypeStruct((t_len, width), BF16),
        in_specs=[pl.BlockSpec((tm, x.shape[1]), lambda i: (i, 0)) for x in xs]
        + [pl.BlockSpec((1, width), lambda i: (0, 0))],
        out_specs=pl.BlockSpec((tm, width), lambda i: (i, 0)),
        compiler_params=_params("parallel"),
    )(*xs, g)


def _rmsnorm_bwd(xs, seg_widths, g, dh, residuals, name, tm=256, bf16_copy=False):
    t_len = xs[0].shape[0]
    width = sum(x.shape[1] for x in xs)
    tm = _pick(t_len, tm, 8)
    n = len(xs)
    has_res = [r is not None for r in residuals]
    res_ops = [r for r in residuals if r is not None]

    def body(*refs):
        x_refs, g_ref, dh_ref = refs[:n], refs[n], refs[n + 1]
        res_refs = list(refs[n + 2:n + 2 + len(res_ops)])
        dx_refs = refs[n + 2 + len(res_ops):n + 2 + len(res_ops) + n]
        dg_ref = refs[n + 2 + len(res_ops) + n]
        copy_refs = refs[n + 3 + len(res_ops) + n:]
        first = pl.program_id(0) == 0
        col = 0
        for idx, (x_ref, widths) in enumerate(zip(x_refs, seg_widths)):
            res_ref = res_refs.pop(0) if has_res[idx] else None
            off = 0
            for w in widths:
                xv = x_ref[:, off:off + w].astype(F32)
                r = lax.rsqrt(jnp.mean(xv * xv, axis=1, keepdims=True) + NORM_EPS)
                xh = xv * r
                dhv = dh_ref[:, col:col + w].astype(F32)
                gd = dhv * g_ref[:, col:col + w]
                dx = r * (gd - xh * jnp.mean(gd * xh, axis=1, keepdims=True))
                if res_ref is not None:
                    dx = dx + res_ref[:, off:off + w]
                dx_refs[idx][:, off:off + w] = dx
                if bf16_copy:
                    copy_refs[idx][:, off:off + w] = dx.astype(BF16)
                part = jnp.sum(dhv * xh, axis=0, keepdims=True)

                @pl.when(first)
                def _(part=part, col=col, w=w):
                    dg_ref[:, col:col + w] = part

                @pl.when(jnp.logical_not(first))
                def _(part=part, col=col, w=w):
                    dg_ref[:, col:col + w] += part
                off += w
                col += w

    outs = pl.pallas_call(
        body, name=name, grid=(t_len // tm,),
        out_shape=[jax.ShapeDtypeStruct(x.shape, F32) for x in xs] + [jax.ShapeDtypeStruct((1, width), F32)]
        + ([jax.ShapeDtypeStruct(x.shape, BF16) for x in xs] if bf16_copy else []),
        in_specs=[pl.BlockSpec((tm, x.shape[1]), lambda i: (i, 0)) for x in xs]
        + [pl.BlockSpec((1, width), lambda i: (0, 0)), pl.BlockSpec((tm, width), lambda i: (i, 0))]
        + [pl.BlockSpec((tm, r.shape[1]), lambda i: (i, 0)) for r in res_ops],
        out_specs=[pl.BlockSpec((tm, x.shape[1]), lambda i: (i, 0)) for x in xs]
        + [pl.BlockSpec((1, width), lambda i: (0, 0))]
        + ([pl.BlockSpec((tm, x.shape[1]), lambda i: (i, 0)) for x in xs] if bf16_copy else []),
        compiler_params=_params("arbitrary"),
    )(*xs, g, dh, *res_ops)
    return outs[:n], outs[n], outs[n + 1:]


def _loss_head(x, g, target, name, tm=256):
    t_len, d = x.shape
    tm = _pick(t_len, tm, 8)

    def body(x_ref, g_ref, t_ref, dx_ref, dg_ref, loss_ref, dxb_ref):
        first = pl.program_id(0) == 0
        xv = x_ref[...]
        r = lax.rsqrt(jnp.mean(xv * xv, axis=1, keepdims=True) + NORM_EPS)
        xh = xv * r
        gv = g_ref[...]
        err = xh * gv - t_ref[...]
        part_loss = 0.5 * jnp.sum(jnp.mean(err * err, axis=1, keepdims=True), axis=0, keepdims=True)
        dy = err * (1.0 / d)
        gd = dy * gv
        dx = r * (gd - xh * jnp.mean(gd * xh, axis=1, keepdims=True))
        dx_ref[...] = dx
        dxb_ref[...] = dx.astype(BF16)
        part_g = jnp.sum(dy * xh, axis=0, keepdims=True)
        part_loss = jnp.broadcast_to(part_loss, (1, LANES))

        @pl.when(first)
        def _():
            dg_ref[...] = part_g
            loss_ref[...] = part_loss

        @pl.when(jnp.logical_not(first))
        def _():
            dg_ref[...] += part_g
            loss_ref[...] += part_loss

    return pl.pallas_call(
        body, name=name, grid=(t_len // tm,),
        out_shape=[jax.ShapeDtypeStruct((t_len, d), F32), jax.ShapeDtypeStruct((1, d), F32),
                   jax.ShapeDtypeStruct((1, LANES), F32), jax.ShapeDtypeStruct((t_len, d), BF16)],
        in_specs=[pl.BlockSpec((tm, d), lambda i: (i, 0)), pl.BlockSpec((1, d), lambda i: (0, 0)),
                  pl.BlockSpec((tm, d), lambda i: (i, 0))],
        out_specs=[pl.BlockSpec((tm, d), lambda i: (i, 0)), pl.BlockSpec((1, d), lambda i: (0, 0)),
                   pl.BlockSpec((1, LANES), lambda i: (0, 0)), pl.BlockSpec((tm, d), lambda i: (i, 0))],
        compiler_params=_params("arbitrary"),
    )(x, g, target)


def _alibi_slope(h, n_heads):
    return jnp.exp(jnp.full((1, 1), -8.0 * math.log(2.0) / n_heads, F32) * (h + 1).astype(F32))


def _attn_tiles(d, w):
    return ATTN_BLOCK * d, (w if d == 1 else LANES)


def _residue_rows(r, d):
    return pl.ds(r, ATTN_BLOCK, stride=d) if d > 1 else pl.ds(0, ATTN_BLOCK)


def _attn_masks(first_block):
    i = lax.broadcasted_iota(jnp.int32, (ATTN_BLOCK, ATTN_BLOCK), 0)
    j = lax.broadcasted_iota(jnp.int32, (ATTN_BLOCK, ATTN_BLOCK), 1)
    valid_cur = j <= i
    valid_prev = jnp.logical_and(j >= i, jnp.logical_not(first_block))
    delta_cur = (i - j).astype(F32)
    delta_prev = (i - j + ATTN_BLOCK).astype(F32)
    return valid_cur, valid_prev, delta_cur, delta_prev


def _head_lane_masks():
    lane = lax.broadcasted_iota(jnp.int32, (ATTN_BLOCK, LANES), 1)
    return [lane < HEAD_DIM, lane >= HEAD_DIM]


def _attn_branch_fwd(proj, w, dilation, n_heads, name):
    t_len = proj.shape[0]
    d = dilation
    rows, lw = _attn_tiles(d, w)
    nb = t_len // rows
    n_pairs = lw // LANES
    per = w // lw
    scale = HEAD_DIM ** -0.5

    def body(q_ref, kp_ref, kc_ref, vp_ref, vc_ref, o_ref, lse_ref):
        first_head = pl.program_id(0) * (2 * n_pairs)
        first_block = pl.program_id(1) == 0
        valid_cur, valid_prev, delta_cur, delta_prev = _attn_masks(first_block)
        masks = _head_lane_masks()
        for p in range(n_pairs):
            cols = pl.ds(p * LANES, LANES)
            slopes = [_alibi_slope(first_head + 2 * p + hh, n_heads) * d for hh in range(2)]
            for r in range(d):
                rs = _residue_rows(r, d)
                q = (q_ref[rs, cols] * scale).astype(BF16)
                kp, kc = kp_ref[rs, cols].astype(BF16), kc_ref[rs, cols].astype(BF16)
                vp, vc = vp_ref[rs, cols].astype(BF16), vc_ref[rs, cols].astype(BF16)
                outs, lses = [], []
                for hh in range(2):
                    qh = jnp.where(masks[hh], q, jnp.zeros_like(q))
                    s_cur = jnp.where(valid_cur, _dot(qh, kc, 1, 1) - slopes[hh] * delta_cur, NEG_INF)
                    s_prev = jnp.where(valid_prev, _dot(qh, kp, 1, 1) - slopes[hh] * delta_prev, NEG_INF)
                    m = jnp.maximum(jnp.max(s_cur, axis=1, keepdims=True), jnp.max(s_prev, axis=1, keepdims=True))
                    p_cur, p_prev = jnp.exp(s_cur - m), jnp.exp(s_prev - m)
                    den = jnp.sum(p_cur, axis=1, keepdims=True) + jnp.sum(p_prev, axis=1, keepdims=True)
                    acc = _dot(p_cur.astype(BF16), vc, 1, 0) + _dot(p_prev.astype(BF16), vp, 1, 0)
                    outs.append(acc / den)
                    lses.append(jnp.broadcast_to(m + jnp.log(den), (ATTN_BLOCK, LANES)))
                o_ref[rs, cols] = jnp.where(masks[0], outs[0], outs[1])
                lse_ref[rs, cols] = jnp.where(masks[0], lses[0], lses[1])

    def spec(which, prev):
        if prev:
            return pl.BlockSpec((rows, lw), lambda b, n: (jnp.maximum(n - 1, 0), which * per + b))
        return pl.BlockSpec((rows, lw), lambda b, n: (n, which * per + b))

    o_spec = pl.BlockSpec((rows, lw), lambda b, n: (n, b))
    return pl.pallas_call(
        body, name=name, grid=(per, nb),
        out_shape=[jax.ShapeDtypeStruct((t_len, w), F32)] * 2,
        in_specs=[spec(0, False), spec(1, True), spec(1, False), spec(2, True), spec(2, False)],
        out_specs=[o_spec, o_spec],
        compiler_params=_params("parallel", "parallel"),
    )(proj, proj, proj, proj, proj)


def _attn_combine(outs, lses, name, tm=512):
    t_len, w = outs[0].shape
    tm = _pick(t_len, tm, 8)
    nbr = len(outs)

    def body(*refs):
        o_refs, l_refs = refs[:nbr], refs[nbr:2 * nbr]
        out_ref, lse_ref = refs[2 * nbr:]
        ls = [r[...] for r in l_refs]
        m = functools.reduce(jnp.maximum, ls)
        es = [jnp.exp(l - m) for l in ls]
        den = functools.reduce(lambda a, b: a + b, es)
        num = functools.reduce(lambda a, b: a + b, [e * r[...] for e, r in zip(es, o_refs)])
        out_ref[...] = num / den
        lse_ref[...] = m + jnp.log(den)

    spec = pl.BlockSpec((tm, w), lambda i: (i, 0))
    return pl.pallas_call(
        body, name=name, grid=(t_len // tm,),
        out_shape=[jax.ShapeDtypeStruct((t_len, w), F32)] * 2,
        in_specs=[spec] * (2 * nbr), out_specs=[spec, spec],
        compiler_params=_params("parallel"),
    )(*outs, *lses)


def _attn_branch_bwd(proj, w, out, lse, dout, dilation, n_heads, name, acc=None):
    t_len = proj.shape[0]
    d = dilation
    rows, lw = _attn_tiles(d, w)
    nb = t_len // rows
    n_pairs = lw // LANES
    per = w // lw
    scale = HEAD_DIM ** -0.5
    n_acc = 0 if acc is None else 3

    def body(*refs):
        q_ref, kp_ref, kc_ref, vp_ref, vc_ref, out_ref, lse_ref, do_ref = refs[:8]
        acc_refs = refs[8:8 + n_acc]
        dq_ref, dk_ref, dv_ref, dk_carry, dv_carry = refs[8 + n_acc:]
        first_head = pl.program_id(0) * (2 * n_pairs)
        n = pl.program_id(1)
        first_block = n == 0
        valid_cur, valid_prev, delta_cur, delta_prev = _attn_masks(first_block)
        masks = _head_lane_masks()

        def plus(value, idx, *where):
            return value + acc_refs[idx][where] if n_acc else value

        @pl.when(first_block)
        def _():
            dk_carry[...] = jnp.zeros_like(dk_carry)
            dv_carry[...] = jnp.zeros_like(dv_carry)

        @pl.when(n < nb)
        def _():
            for p in range(n_pairs):
                cols = pl.ds(p * LANES, LANES)
                slopes = [_alibi_slope(first_head + 2 * p + hh, n_heads) * d for hh in range(2)]
                for r in range(d):
                    rs = _residue_rows(r, d)
                    q = (q_ref[rs, cols] * scale).astype(BF16)
                    kp, kc = kp_ref[rs, cols].astype(BF16), kc_ref[rs, cols].astype(BF16)
                    vp, vc = vp_ref[rs, cols].astype(BF16), vc_ref[rs, cols].astype(BF16)
                    do = do_ref[rs, cols]
                    dob = do.astype(BF16)
                    do_out = do * out_ref[rs, cols]
                    lse_all = lse_ref[rs, cols]
                    dq = jnp.zeros((ATTN_BLOCK, LANES), F32)
                    dk_cur = jnp.zeros((ATTN_BLOCK, LANES), F32)
                    dk_prev = jnp.zeros((ATTN_BLOCK, LANES), F32)
                    dv_cur = jnp.zeros((ATTN_BLOCK, LANES), F32)
                    dv_prev = jnp.zeros((ATTN_BLOCK, LANES), F32)
                    for hh in range(2):
                        msk = masks[hh]
                        qh = jnp.where(msk, q, jnp.zeros_like(q))
                        doh = jnp.where(msk, dob, jnp.zeros_like(dob))
                        delta = jnp.sum(jnp.where(msk, do_out, 0.0), axis=1, keepdims=True)
                        lse_h = jnp.max(jnp.where(msk, lse_all, NEG_INF), axis=1, keepdims=True)
                        s_cur = jnp.where(valid_cur, _dot(qh, kc, 1, 1) - slopes[hh] * delta_cur, NEG_INF)
                        s_prev = jnp.where(valid_prev, _dot(qh, kp, 1, 1) - slopes[hh] * delta_prev, NEG_INF)
                        p_cur, p_prev = jnp.exp(s_cur - lse_h), jnp.exp(s_prev - lse_h)
                        ds_cur = (p_cur * (_dot(doh, vc, 1, 1) - delta)).astype(BF16)
                        ds_prev = (p_prev * (_dot(doh, vp, 1, 1) - delta)).astype(BF16)
                        kch = jnp.where(msk, kc, jnp.zeros_like(kc))
                        kph = jnp.where(msk, kp, jnp.zeros_like(kp))
                        dq = dq + _dot(ds_cur, kch, 1, 0) + _dot(ds_prev, kph, 1, 0)
                        dk_cur = dk_cur + _dot(ds_cur, qh, 0, 0)
                        dk_prev = dk_prev + _dot(ds_prev, qh, 0, 0)
                        dv_cur = dv_cur + _dot(p_cur.astype(BF16), doh, 0, 0)
                        dv_prev = dv_prev + _dot(p_prev.astype(BF16), doh, 0, 0)
                    dq_ref[rs, cols] = plus(dq * scale, 0, rs, cols)
                    dk_ref[rs, cols] = plus(dk_carry[rs, cols] + dk_prev, 1, rs, cols)
                    dv_ref[rs, cols] = plus(dv_carry[rs, cols] + dv_prev, 2, rs, cols)
                    dk_carry[rs, cols] = dk_cur
                    dv_carry[rs, cols] = dv_cur

        @pl.when(n == nb)
        def _():
            dk_ref[...] = plus(dk_carry[...], 1, Ellipsis)
            dv_ref[...] = plus(dv_carry[...], 2, Ellipsis)

    def qkv_spec(which, shift):
        return pl.BlockSpec((rows, lw), lambda b, n: (jnp.clip(n - shift, 0, nb - 1), which * per + b))

    q_like = pl.BlockSpec((rows, lw), lambda b, n: (jnp.minimum(n, nb - 1), b))
    k_like = pl.BlockSpec((rows, lw), lambda b, n: (jnp.maximum(n - 1, 0), b))
    return pl.pallas_call(
        body, name=name, grid=(per, nb + 1),
        out_shape=[jax.ShapeDtypeStruct((t_len, w), F32)] * 3,
        in_specs=[qkv_spec(0, 0), qkv_spec(1, 1), qkv_spec(1, 0), qkv_spec(2, 1), qkv_spec(2, 0),
                  q_like, q_like, q_like] + [q_like, k_like, k_like][:n_acc],
        out_specs=[q_like, k_like, k_like],
        scratch_shapes=[pltpu.VMEM((rows, lw), F32), pltpu.VMEM((rows, lw), F32)],
        compiler_params=_params("parallel", "arbitrary"),
    )(proj, proj, proj, proj, proj, out, lse, dout, *(acc or ()))


def _shift_down(u, s):
    if s == 0:
        return u
    row = lax.broadcasted_iota(jnp.int32, u.shape, 0)
    return jnp.where(row >= s, pltpu.roll(u, s, 0), 0.0)


def _shift_up(u, s):
    if s == 0:
        return u
    n = u.shape[0]
    row = lax.broadcasted_iota(jnp.int32, u.shape, 0)
    return jnp.where(row < n - s, pltpu.roll(u, n - s, 0), 0.0)


def _conv_fwd(u, col0, w, b, name):
    t_len, ch = u.shape[0], w.shape[1]
    blk0 = col0 // LANES

    def body(u_ref, w_ref, b_ref, o_ref):
        uv = u_ref[...]
        pre = b_ref[...] + jnp.zeros_like(uv)
        for k in range(SSD_CONV):
            pre = pre + w_ref[k:k + 1, :] * _shift_down(uv, SSD_CONV - 1 - k)
        o_ref[...] = pre * jax.nn.sigmoid(pre)

    return pl.pallas_call(
        body, name=name, grid=(ch // LANES,),
        out_shape=jax.ShapeDtypeStruct((t_len, ch), F32),
        in_specs=[pl.BlockSpec((t_len, LANES), lambda j: (0, blk0 + j)),
                  pl.BlockSpec((SSD_CONV, LANES), lambda j: (0, j)), pl.BlockSpec((1, LANES), lambda j: (0, j))],
        out_specs=pl.BlockSpec((t_len, LANES), lambda j: (0, j)),
        compiler_params=_params("parallel"),
    )(u, w, b)


def _conv_bwd(u, col0, w, b, dact, name):
    t_len, ch = u.shape[0], w.shape[1]
    blk0 = col0 // LANES

    def body(u_ref, w_ref, b_ref, da_ref, du_ref, dw_ref, db_ref):
        uv = u_ref[...]
        shifted = [_shift_down(uv, SSD_CONV - 1 - k) for k in range(SSD_CONV)]
        pre = b_ref[...] + jnp.zeros_like(uv)
        for k in range(SSD_CONV):
            pre = pre + w_ref[k:k + 1, :] * shifted[k]
        sig = jax.nn.sigmoid(pre)
        dpre = da_ref[...] * (sig * (1.0 + pre * (1.0 - sig)))
        du = jnp.zeros_like(uv)
        for k in range(SSD_CONV):
            du = du + w_ref[k:k + 1, :] * _shift_up(dpre, SSD_CONV - 1 - k)
            dw_ref[k:k + 1, :] = jnp.sum(dpre * shifted[k], axis=0, keepdims=True)
        du_ref[...] = du
        db_ref[...] = jnp.sum(dpre, axis=0, keepdims=True)

    col = pl.BlockSpec((t_len, LANES), lambda j: (0, j))
    w_spec = pl.BlockSpec((SSD_CONV, LANES), lambda j: (0, j))
    b_spec = pl.BlockSpec((1, LANES), lambda j: (0, j))
    return pl.pallas_call(
        body, name=name, grid=(ch // LANES,),
        out_shape=[jax.ShapeDtypeStruct((t_len, ch), F32), jax.ShapeDtypeStruct((SSD_CONV, ch), F32),
                   jax.ShapeDtypeStruct((1, ch), F32)],
        in_specs=[pl.BlockSpec((t_len, LANES), lambda j: (0, blk0 + j)), w_spec, b_spec, col],
        out_specs=[col, w_spec, b_spec],
        compiler_params=_params("parallel"),
    )(u, w, b, dact)


def _cumsum_rows(v):
    n = v.shape[0]
    row = lax.broadcasted_iota(jnp.int32, v.shape, 0)
    s = 1
    while s < n:
        v = v + jnp.where(row >= s, pltpu.roll(v, s, 0), 0.0)
        s *= 2
    return v


def _rev_cumsum_rows(v):
    n = v.shape[0]
    row = lax.broadcasted_iota(jnp.int32, v.shape, 0)
    s = 1
    while s < n:
        v = v + jnp.where(row < n - s, pltpu.roll(v, n - s, 0), 0.0)
        s *= 2
    return v


def _head_selector(heads, width):
    j = lax.broadcasted_iota(jnp.int32, (LANES, width), 0)
    lane = lax.broadcasted_iota(jnp.int32, (LANES, width), 1)
    return jnp.where(jnp.logical_and(lane // HEAD_DIM == j, j < heads), 1.0, 0.0).astype(BF16)


class _SsdChunk:
    def __init__(self, dtraw_ref, bias_ref, alog_ref, xs_ref, b_ref, c_ref, heads):
        q = SSD_CHUNK
        width = heads * HEAD_DIM
        lane = lax.broadcasted_iota(jnp.int32, (q, LANES), 1)
        self.head_lanes = lane < heads
        lane1 = lax.broadcasted_iota(jnp.int32, (1, LANES), 1)
        self.a = jnp.where(lane1 < heads, -jnp.exp(alog_ref[...]), 0.0)
        self.dt_arg = dtraw_ref[...] + bias_ref[...]
        self.dt = jnp.where(self.head_lanes, jax.nn.softplus(self.dt_arg), 0.0)
        self.cum = _cumsum_rows(self.dt * self.a)
        self.cum_t = self.cum.T
        last = self.cum[q - 1:q, :]
        self.sel = _head_selector(heads, width)
        self.expand = lambda v: _dot_exact(v, self.sel, 1, 0)
        self.segsum = lambda v: _dot_exact(v, self.sel, 1, 1)
        self.e_exp = self.expand(jnp.exp(self.cum))
        self.d_exp = self.expand(jnp.exp(last - self.cum))
        self.elast_exp = self.e_exp[q - 1:q, :]
        self.dt_exp = self.expand(self.dt)
        self.xs = xs_ref[...]
        self.x = self.xs * self.dt_exp
        self.xb = self.x.astype(BF16)
        self.bb = b_ref[...].astype(BF16)
        self.cb = c_ref[...].astype(BF16)
        self.cbm = _dot(self.cb, self.bb, 1, 1)
        li = lax.broadcasted_iota(jnp.int32, (q, q), 0)
        si = lax.broadcasted_iota(jnp.int32, (q, q), 1)
        self.tri = li >= si
        hl = lax.broadcasted_iota(jnp.int32, (q, LANES), 1)
        self.pair_masks = [hl < HEAD_DIM, hl >= HEAD_DIM]

    def decay(self, j):
        diff = self.cum[:, j:j + 1] - self.cum_t[j:j + 1, :]
        return jnp.exp(jnp.where(self.tri, diff, NEG_INF))


def _ssd_specs(t_len, heads, n_chunks, xbc_cols, rev):
    q, gw = SSD_CHUNK, heads * HEAD_DIM
    ssd_w = SSD_GROUPS * gw
    b_blk = ssd_w // SSD_STATE
    ch = (lambda c: n_chunks - 1 - c) if rev else (lambda c: c)
    return dict(
        dtraw=pl.BlockSpec((None, q, LANES), lambda g, c: (g, ch(c), 0)),
        small=pl.BlockSpec((None, 1, LANES), lambda g, c: (g, 0, 0)),
        dsk=pl.BlockSpec((None, 1, gw), lambda g, c: (g, 0, 0)),
        xs=pl.BlockSpec((q, gw), lambda g, c: (ch(c), g)),
        b=pl.BlockSpec((q, SSD_STATE), lambda g, c: (ch(c), b_blk + g)),
        c=pl.BlockSpec((q, SSD_STATE), lambda g, c: (ch(c), b_blk + SSD_GROUPS + g)),
        z=pl.BlockSpec((q, gw), lambda g, c: (ch(c), 3 * SSD_GROUPS + g)),
        tok=pl.BlockSpec((q, gw), lambda g, c: (ch(c), g)),
        state=pl.BlockSpec((None, SSD_STATE, gw), lambda g, c: (ch(c), 0, g)),
        bc=pl.BlockSpec((q, SSD_STATE), lambda g, c: (ch(c), g)),
    )


def _ssd_fwd(xbc, qkvz, dtraw_g, bias_g, alog_g, dsk_exp, heads, name):
    t_len = xbc.shape[0]
    q, gw = SSD_CHUNK, heads * HEAD_DIM
    n_chunks = t_len // q
    ssd_w = SSD_GROUPS * gw
    sp = _ssd_specs(t_len, heads, n_chunks, xbc.shape[1], rev=False)

    def body(dtraw_ref, bias_ref, alog_ref, dsk_ref, xs_ref, b_ref, c_ref, z_ref,
             yg_ref, ypre_ref, st_ref, s_scr):
        @pl.when(pl.program_id(1) == 0)
        def _():
            s_scr[...] = jnp.zeros_like(s_scr)

        k = _SsdChunk(dtraw_ref, bias_ref, alog_ref, xs_ref, b_ref, c_ref, heads)
        s_prev = s_scr[...]
        st_ref[...] = s_prev
        y_off = k.e_exp * _dot(k.cb, s_prev.astype(BF16), 1, 0)
        parts = []
        for p in range(heads // 2):
            xp = k.xb[:, p * LANES:(p + 1) * LANES]
            acc = jnp.zeros((q, LANES), F32)
            for hh in range(2):
                m = (k.cbm * k.decay(2 * p + hh)).astype(BF16)
                acc = acc + _dot(m, jnp.where(k.pair_masks[hh], xp, jnp.zeros_like(xp)), 1, 0)
            parts.append(acc)
        y = jnp.concatenate(parts, axis=1) + y_off
        xd = (k.x * k.d_exp).astype(BF16)
        s_scr[...] = k.elast_exp * s_prev + _dot(k.bb, xd, 0, 0)
        y_pre = y + dsk_ref[...] * k.xs
        zv = z_ref[...]
        ypre_ref[...] = y_pre
        yg_ref[...] = y_pre * (zv * jax.nn.sigmoid(zv))

    return pl.pallas_call(
        body, name=name, grid=(SSD_GROUPS, n_chunks),
        out_shape=[jax.ShapeDtypeStruct((t_len, ssd_w), F32), jax.ShapeDtypeStruct((t_len, ssd_w), F32),
                   jax.ShapeDtypeStruct((n_chunks, SSD_STATE, ssd_w), F32)],
        in_specs=[sp["dtraw"], sp["small"], sp["small"], sp["dsk"], sp["xs"], sp["b"], sp["c"], sp["z"]],
        out_specs=[sp["tok"], sp["tok"], sp["state"]],
        scratch_shapes=[pltpu.VMEM((SSD_STATE, gw), F32)],
        compiler_params=_params("parallel", "arbitrary"),
    )(dtraw_g, bias_g, alog_g, dsk_exp, xbc, xbc, xbc, qkvz)


def _ssd_bwd(xbc, qkvz, dtraw_g, bias_g, alog_g, dsk_exp, ypre, states, dyg, heads, name):
    t_len = xbc.shape[0]
    q, gw = SSD_CHUNK, heads * HEAD_DIM
    n_chunks = t_len // q
    ssd_w = SSD_GROUPS * gw
    sp = _ssd_specs(t_len, heads, n_chunks, xbc.shape[1], rev=True)

    def body(dtraw_ref, bias_ref, alog_ref, dsk_ref, xs_ref, b_ref, c_ref, z_ref, ypre_ref, st_ref, dyg_ref,
             dxs_ref, db_ref, dc_ref, dz_ref, ddt_ref, small_ref, g_scr):
        first = pl.program_id(1) == 0

        @pl.when(first)
        def _():
            g_scr[...] = jnp.zeros_like(g_scr)

        k = _SsdChunk(dtraw_ref, bias_ref, alog_ref, xs_ref, b_ref, c_ref, heads)
        zv = z_ref[...]
        sig = jax.nn.sigmoid(zv)
        dyg = dyg_ref[...]
        y_pre = ypre_ref[...]
        dy = dyg * (zv * sig)
        dz_ref[...] = dyg * y_pre * (sig * (1.0 + zv * (1.0 - sig)))
        dsk = dsk_ref[...]
        g_next = g_scr[...]
        s_prev = st_ref[...]
        sb = s_prev.astype(BF16)
        xd = k.x * k.d_exp
        xdb = xd.astype(BF16)
        gb = g_next.astype(BF16)
        dx_off = k.d_exp * _dot(k.bb, gb, 1, 0)
        dyb = dy.astype(BF16)
        dcb = jnp.zeros((q, q), F32)
        lane = lax.broadcasted_iota(jnp.int32, (q, LANES), 1)
        row_t = lax.broadcasted_iota(jnp.int32, (LANES, q), 0)
        w_rows = jnp.zeros((q, LANES), F32)
        w_cols_t = jnp.zeros((LANES, q), F32)
        parts = []
        for p in range(heads // 2):
            cols = slice(p * LANES, (p + 1) * LANES)
            dyp, xp = dyb[:, cols], k.xb[:, cols]
            acc = jnp.zeros((q, LANES), F32)
            for hh in range(2):
                j = 2 * p + hh
                lm = k.decay(j)
                m32 = k.cbm * lm
                dym = jnp.where(k.pair_masks[hh], dyp, jnp.zeros_like(dyp))
                acc = acc + _dot(m32.astype(BF16), dym, 0, 0)
                dm = _dot(dym, xp, 1, 1)
                dcb = dcb + dm * lm
                wmat = dm * m32
                w_rows = w_rows + jnp.where(lane == j, jnp.sum(wmat, axis=1, keepdims=True), 0.0)
                w_cols_t = w_cols_t + jnp.where(row_t == j, jnp.sum(wmat, axis=0, keepdims=True), 0.0)
            parts.append(acc)
        dx = jnp.concatenate(parts, axis=1) + dx_off
        dcbb = dcb.astype(BF16)
        edy = (k.e_exp * dy).astype(BF16)
        dc_ref[...] = _dot(dcbb, k.bb, 1, 0) + _dot(edy, sb, 1, 1)
        db_ref[...] = _dot(dcbb, k.cb, 0, 0) + _dot(xdb, gb, 1, 1)
        g_scr[...] = k.elast_exp * g_next + _dot(k.cb, edy, 0, 0)

        y_off = k.e_exp * _dot(k.cb, sb, 1, 0)
        dcum = w_rows - w_cols_t.T + k.segsum(dy * y_off)
        t_term = k.segsum(k.x * dx_off)
        gs = jnp.broadcast_to(jnp.sum(g_next * s_prev, axis=0, keepdims=True), (8, gw))
        carried = k.segsum(gs)[0:1, :] * jnp.exp(k.cum[q - 1:q, :])
        dda = _rev_cumsum_rows(dcum) + (_cumsum_rows(t_term) - t_term) + carried
        ddt = jnp.where(k.head_lanes, dda * k.a + k.segsum(dx * k.xs), 0.0)
        ddtraw = ddt * jax.nn.sigmoid(k.dt_arg)
        ddt_ref[...] = ddtraw
        dxs_ref[...] = dx * k.dt_exp + dsk * dy
        ds = jnp.broadcast_to(jnp.sum(dy * k.xs, axis=0, keepdims=True), (8, gw))
        d_alog = jnp.sum(jnp.where(k.head_lanes, dda * k.dt, 0.0), axis=0, keepdims=True) * k.a
        rows8 = lax.broadcasted_iota(jnp.int32, (8, LANES), 0)
        small = jnp.where(rows8 == 0, d_alog, 0.0)
        small = small + jnp.where(rows8 == 1, jnp.sum(ddtraw, axis=0, keepdims=True), 0.0)
        small = small + jnp.where(rows8 == 2, k.segsum(ds)[0:1, :], 0.0)

        @pl.when(first)
        def _():
            small_ref[...] = small

        @pl.when(jnp.logical_not(first))
        def _():
            small_ref[...] += small

    bc_out = sp["bc"]
    return pl.pallas_call(
        body, name=name, grid=(SSD_GROUPS, n_chunks),
        out_shape=[jax.ShapeDtypeStruct((t_len, ssd_w), F32),
                   jax.ShapeDtypeStruct((t_len, SSD_GROUPS * SSD_STATE), F32),
                   jax.ShapeDtypeStruct((t_len, SSD_GROUPS * SSD_STATE), F32),
                   jax.ShapeDtypeStruct((t_len, ssd_w), F32),
                   jax.ShapeDtypeStruct((SSD_GROUPS, t_len, LANES), F32),
                   jax.ShapeDtypeStruct((SSD_GROUPS, 8, LANES), F32)],
        in_specs=[sp["dtraw"], sp["small"], sp["small"], sp["dsk"], sp["xs"], sp["b"], sp["c"], sp["z"],
                  sp["tok"], sp["state"], sp["tok"]],
        out_specs=[sp["tok"], bc_out, bc_out, sp["tok"], sp["dtraw"],
                   pl.BlockSpec((None, 8, LANES), lambda g, c: (g, 0, 0))],
        scratch_shapes=[pltpu.VMEM((SSD_STATE, gw), F32)],
        compiler_params=_params("parallel", "arbitrary"),
    )(dtraw_g, bias_g, alog_g, dsk_exp, xbc, xbc, xbc, qkvz, ypre, states, dyg)


def _adamw(w, g, m, v, name):
    rows, lanes = w.shape
    tr = _row_tile(rows, lanes, 4, 14)
    c1 = 1.0 / (1.0 - ADAM_B1 ** ADAM_STEP)
    c2 = 1.0 / (1.0 - ADAM_B2 ** ADAM_STEP)

    def body(w_ref, g_ref, m_ref, v_ref, d_ref, nm_ref, nv_ref):
        gv = g_ref[...]
        nm = ADAM_B1 * m_ref[...] + (1.0 - ADAM_B1) * gv
        nv = ADAM_B2 * v_ref[...] + (1.0 - ADAM_B2) * (gv * gv)
        nm_ref[...] = nm
        nv_ref[...] = nv
        d_ref[...] = -ADAM_LR * ((nm * c1) / (jnp.sqrt(nv * c2) + ADAM_EPS) + ADAM_WD * w_ref[...])

    spec = pl.BlockSpec((tr, lanes), lambda i: (i, 0))
    return pl.pallas_call(
        body, name=name, grid=(rows // tr,),
        out_shape=[jax.ShapeDtypeStruct((rows, lanes), F32)] * 3,
        in_specs=[spec] * 4, out_specs=[spec] * 3,
        compiler_params=_params("parallel"),
    )(w, g, m, v)


def _pad_lanes(a, width=LANES):
    return jnp.pad(a, ((0, 0), (0, width - a.shape[1])))


def _group_pad(v, heads):
    return _pad_lanes(v.reshape(SSD_GROUPS, heads))[:, None, :]


def _layer_fwd(x0, p, dims, tag):
    w_attn, heads_g, n_heads, conv_ch = dims["w_attn"], dims["heads_g"], dims["n_heads"], dims["conv_ch"]
    h1 = _rmsnorm_fwd([x0], [[x0.shape[1]]], p["ln1_g"], f"ln1_fwd{tag}")
    proj = _mm(h1, p["w_in"], name=f"in_proj{tag}", tn=640)

    outs, lses = [], []
    for d in BRANCH_DILATIONS:
        o, l = _attn_branch_fwd(proj, w_attn, d, n_heads, f"attn_fwd_d{d}{tag}")
        outs.append(o)
        lses.append(l)
    attn, lse = _attn_combine(outs, lses, f"attn_combine{tag}")

    xbc = _conv_fwd(proj, 4 * w_attn, p["conv_w"], p["conv_b"], f"conv_fwd{tag}")
    dt_col = 4 * w_attn + conv_ch
    dtraw_g = jnp.stack([_pad_lanes(proj[:, dt_col + g * heads_g:dt_col + (g + 1) * heads_g])
                         for g in range(SSD_GROUPS)])
    bias_g, alog_g = _group_pad(p["dt_bias"], heads_g), _group_pad(p["a_log"], heads_g)
    dsk_exp = jnp.repeat(p["d_skip"], HEAD_DIM).reshape(SSD_GROUPS, 1, heads_g * HEAD_DIM)
    yg, ypre, states = _ssd_fwd(xbc, proj, dtraw_g, bias_g, alog_g, dsk_exp, heads_g, f"ssd_fwd{tag}")

    gw = heads_g * HEAD_DIM
    mix_g = jnp.concatenate([p["attn_norm_g"], p["ssd_norm_g"]])[None, :]
    mix = _rmsnorm_fwd([attn, yg], [[w_attn], [gw] * SSD_GROUPS], mix_g, f"mix_norm_fwd{tag}")
    x1 = _mm(mix, p["wo"], name=f"out_proj{tag}", residual=x0)
    h2 = _rmsnorm_fwd([x1], [[x1.shape[1]]], p["ln2_g"], f"ln2_fwd{tag}")
    u = _mm(h2, p["wmi"], name=f"mlp_in{tag}", out_dtype=BF16, tn=1024)
    x2 = _mm(u, p["wmo"], name=f"mlp_out{tag}", a_act="relu2", residual=x1, tm=512)
    saved = dict(x0=x0, h1=h1, proj=proj, attn=attn, lse=lse, xbc=xbc, dtraw_g=dtraw_g,
                 bias_g=bias_g, alog_g=alog_g, dsk_exp=dsk_exp, yg=yg, ypre=ypre, states=states, mix=mix,
                 mix_g=mix_g, x1=x1, h2=h2, u=u)
    return x2, saved


def _layer_bwd(dx2, dx2_b, p, s, dims, tag, copy_dx0):
    w_attn, heads_g, n_heads, conv_ch = dims["w_attn"], dims["heads_g"], dims["n_heads"], dims["conv_ch"]
    t_len, d_model = dx2.shape
    gw = heads_g * HEAD_DIM
    h_ssd = heads_g * SSD_GROUPS
    du = _mm(dx2_b, p["wmo"], name=f"mlp_out_dx{tag}", tb=True, gate=s["u"], out_dtype=BF16, tn=1024)
    d_wmo = _mm(s["u"], dx2_b, name=f"mlp_out_dw{tag}", ta=True, a_act="relu2", tm=512, tn=1024, out_dtype=BF16)
    d_wmi = _mm(s["h2"], du, name=f"mlp_in_dw{tag}", ta=True, tm=512, tn=1024, out_dtype=BF16,
                out_chunk=du.shape[1] // N_DEV)
    dh2 = _mm(du, p["wmi"], name=f"mlp_in_dx{tag}", tb=True, tm=512)
    (dx1,), d_ln2, (dx1_b,) = _rmsnorm_bwd([s["x1"]], [[d_model]], p["ln2_g"], dh2, [dx2], f"ln2_bwd{tag}",
                                           bf16_copy=True)
    dmix = _mm(dx1_b, p["wo"], name=f"out_proj_dx{tag}", tb=True)
    d_wo = _mm(s["mix"], dx1_b, name=f"out_proj_dw{tag}", ta=True, tm=512, tn=1024, out_dtype=BF16)
    (dattn, dyg), d_mix_g, _ = _rmsnorm_bwd([s["attn"], s["yg"]], [[w_attn], [gw] * SSD_GROUPS], s["mix_g"], dmix,
                                           [None, None], f"mix_norm_bwd{tag}")
    dxs, db, dc, dz, ddtraw_g, ssd_small = _ssd_bwd(
        s["xbc"], s["proj"], s["dtraw_g"], s["bias_g"], s["alog_g"], s["dsk_exp"], s["ypre"], s["states"], dyg,
        heads_g, f"ssd_bwd{tag}")
    dxbc = jnp.concatenate([dxs, db, dc], axis=1)
    dxbc_raw, d_conv_w, d_conv_b = _conv_bwd(s["proj"], 4 * w_attn, p["conv_w"], p["conv_b"], dxbc, f"conv_bwd{tag}")
    acc = None
    for d in BRANCH_DILATIONS:
        acc = _attn_branch_bwd(s["proj"], w_attn, s["attn"], s["lse"], dattn, d, n_heads, f"attn_bwd_d{d}{tag}", acc)
    pad = jnp.zeros((t_len, p["w_in"].shape[1] - (4 * w_attn + conv_ch + h_ssd)), F32)
    dproj = jnp.concatenate([*acc, dz, dxbc_raw] + [ddtraw_g[g, :, :heads_g] for g in range(SSD_GROUPS)] + [pad],
                            axis=1).astype(BF16)
    d_win = _mm(s["h1"], dproj, name=f"in_proj_dw{tag}", ta=True, tm=512, tn=1152)
    dh1 = _mm(dproj, p["w_in"], name=f"in_proj_dx{tag}", tb=True, tm=512)
    (dx0,), d_ln1, dx0_b = _rmsnorm_bwd([s["x0"]], [[d_model]], p["ln1_g"], dh1, [dx1], f"ln1_bwd{tag}",
                                        bf16_copy=copy_dx0)

    small = ssd_small[:, :, :heads_g]
    grads = dict(
        ln1_g=d_ln1[0], conv_w=d_conv_w, conv_b=d_conv_b[0],
        a_log=small[:, 0].reshape(h_ssd), dt_bias=small[:, 1].reshape(h_ssd), d_skip=small[:, 2].reshape(h_ssd),
        attn_norm_g=d_mix_g[0, :w_attn], ssd_norm_g=d_mix_g[0, w_attn:], ln2_g=d_ln2[0],
        w_in=d_win, w_out=d_wo, w_mlp_in=d_wmi, w_mlp_out=d_wmo)
    return dx0, (dx0_b[0] if copy_dx0 else None), grads


_SMALL = ["ln1_g", "conv_w", "conv_b", "dt_bias", "a_log", "d_skip", "attn_norm_g", "ssd_norm_g", "ln2_g"]
_WEIGHTS = ["ln1_g", "w_in", "conv_w", "conv_b", "dt_bias", "a_log", "d_skip", "attn_norm_g", "ssd_norm_g",
            "w_out", "ln2_g", "w_mlp_in", "w_mlp_out", "final_norm_g"]


def _to_rows(a):
    flat = a.reshape(-1)
    rows = -(-flat.shape[0] // LANES)
    rows = -(-rows // 8) * 8
    return jnp.pad(flat, (0, rows * LANES - flat.shape[0])).reshape(rows, LANES)


def kernel(x, ln1_g, w_in, conv_w, conv_b, dt_bias, a_log, d_skip, attn_norm_g, ssd_norm_g, w_out, ln2_g, w_mlp_in, w_mlp_out, final_norm_g, loss_target, m_ln1_g, m_w_in, m_conv_w, m_conv_b, m_dt_bias, m_a_log, m_d_skip, m_attn_norm_g, m_ssd_norm_g, m_w_out, m_ln2_g, m_w_mlp_in, m_w_mlp_out, m_final_norm_g, v_ln1_g, v_w_in, v_conv_w, v_conv_b, v_dt_bias, v_a_log, v_d_skip, v_attn_norm_g, v_ssd_norm_g, v_w_out, v_ln2_g, v_w_mlp_in, v_w_mlp_out, v_final_norm_g):
    w = dict(ln1_g=ln1_g, w_in=w_in, conv_w=conv_w, conv_b=conv_b, dt_bias=dt_bias, a_log=a_log, d_skip=d_skip,
             attn_norm_g=attn_norm_g, ssd_norm_g=ssd_norm_g, w_out=w_out, ln2_g=ln2_g, w_mlp_in=w_mlp_in,
             w_mlp_out=w_mlp_out, final_norm_g=final_norm_g)
    mom = dict(ln1_g=m_ln1_g, w_in=m_w_in, conv_w=m_conv_w, conv_b=m_conv_b, dt_bias=m_dt_bias, a_log=m_a_log,
               d_skip=m_d_skip, attn_norm_g=m_attn_norm_g, ssd_norm_g=m_ssd_norm_g, w_out=m_w_out, ln2_g=m_ln2_g,
               w_mlp_in=m_w_mlp_in, w_mlp_out=m_w_mlp_out, final_norm_g=m_final_norm_g)
    var = dict(ln1_g=v_ln1_g, w_in=v_w_in, conv_w=v_conv_w, conv_b=v_conv_b, dt_bias=v_dt_bias, a_log=v_a_log,
               d_skip=v_d_skip, attn_norm_g=v_attn_norm_g, ssd_norm_g=v_ssd_norm_g, w_out=v_w_out, ln2_g=v_ln2_g,
               w_mlp_in=v_w_mlp_in, w_mlp_out=v_w_mlp_out, final_norm_g=v_final_norm_g)

    depth, d_model = ln1_g.shape
    t_len = x.shape[1]
    w_attn = attn_norm_g.shape[1]
    h_ssd = dt_bias.shape[1]
    conv_ch = conv_b.shape[1]
    in_proj = w_in.shape[2] * N_DEV
    assert ssd_norm_g.shape[1] == w_attn and in_proj == 4 * w_attn + conv_ch + h_ssd
    assert t_len % (BRANCH_DILATIONS[-1] * ATTN_BLOCK) == 0 and h_ssd % (2 * SSD_GROUPS) == 0
    dims = dict(w_attn=w_attn, heads_g=h_ssd // SSD_GROUPS, n_heads=w_attn // HEAD_DIM, conv_ch=conv_ch)
    names = ["w_in", "w_out", "w_mlp_in", "w_mlp_out"]

    shards = [w[n][l].astype(BF16) for l in range(depth) for n in names]
    gathered = _all_gather(shards + [conv_w], "gather_weights")
    full_cw = gathered[-1].transpose(1, 2, 0, 3).reshape(depth, SSD_CONV, conv_ch)
    proj_cols = -(-in_proj // LANES) * LANES
    layers = []
    for l in range(depth):
        g_in, g_out, g_mi, g_mo = gathered[4 * l:4 * l + 4]
        full_in = g_in.transpose(1, 0, 2).reshape(d_model, in_proj)
        layers.append(dict(
            ln1_g=ln1_g[l][None, :], ln2_g=ln2_g[l][None, :], w_in=_pad_lanes(full_in, proj_cols),
            conv_w=full_cw[l], conv_b=conv_b[l][None, :], dt_bias=dt_bias[l], a_log=a_log[l], d_skip=d_skip[l],
            attn_norm_g=attn_norm_g[l], ssd_norm_g=ssd_norm_g[l], wo=g_out.reshape(-1, d_model),
            wmi=g_mi.transpose(1, 0, 2).reshape(d_model, -1), wmo=g_mo.reshape(-1, d_model)))

    h = x[0]
    saved = []
    for l in range(depth):
        h, s = _layer_fwd(h, layers[l], dims, f"_l{l}")
        saved.append(s)
    dh, d_final_g, loss_part, dh_b = _loss_head(h, final_norm_g[None, :], loss_target[0], "loss_head")

    grads = [None] * depth
    for l in reversed(range(depth)):
        dh, dh_b, grads[l] = _layer_bwd(dh, dh_b, layers[l], saved[l], dims, f"_l{l}", copy_dx0=l > 0)
    grad_x = dh[None]

    send = []
    for l in range(depth):
        g = grads[l]
        send += [g["w_in"][:, :in_proj].reshape(d_model, N_DEV, -1).transpose(1, 0, 2).astype(BF16),
                 g["w_out"].reshape(N_DEV, -1, d_model), g["w_mlp_in"], g["w_mlp_out"].reshape(N_DEV, -1, d_model)]
    recv = _exchange_blocks(send, "exchange_grads")
    sums = [_sum_devices(r, f"sum_grads_{i}") for i, r in enumerate(recv)]
    gsum = {n: jnp.stack([sums[4 * l + i] for l in range(depth)]) for i, n in enumerate(names)}

    small_parts = [jnp.stack([grads[l][n] for l in range(depth)]).reshape(-1) for n in _SMALL]
    small_parts += [d_final_g.reshape(-1), loss_part[0, :1]]
    sizes = [int(a.shape[0]) for a in small_parts]
    packed = _to_rows(jnp.concatenate(small_parts))
    (gathered,) = _all_gather([packed], "gather_small_grads")
    total = _sum_devices(gathered, "sum_small_grads").reshape(-1)
    offs = np.cumsum([0] + sizes)
    pieces = [total[offs[i]:offs[i + 1]] for i in range(len(sizes))]
    for n, piece in zip(_SMALL, pieces):
        shape = (depth, SSD_CONV, conv_ch) if n == "conv_w" else w[n].shape
        gsum[n] = piece.reshape(shape)
    gsum["final_norm_g"] = pieces[len(_SMALL)]
    loss = pieces[len(_SMALL) + 1][0]
    my_id = 4 * lax.axis_index("x") + 2 * lax.axis_index("y") + lax.axis_index("c")
    cw = conv_w.shape[2]
    gsum["conv_w"] = lax.dynamic_slice_in_dim(gsum["conv_w"], my_id * cw, cw, axis=2)

    delta, new_m, new_v = {}, {}, {}
    for n in names:
        outs = _adamw(*(a.reshape(-1, w[n].shape[-1]) for a in (w[n], gsum[n], mom[n], var[n])), f"adamw_{n}")
        delta[n], new_m[n], new_v[n] = (o.reshape(w[n].shape) for o in outs)
    small_names = [n for n in _WEIGHTS if n not in names]
    sm_sizes = [int(np.prod(w[n].shape)) for n in small_names]
    pack = lambda d: _to_rows(jnp.concatenate([d[n].reshape(-1) for n in small_names]))
    outs = _adamw(pack(w), pack(gsum), pack(mom), pack(var), "adamw_small")
    sm_offs = np.cumsum([0] + sm_sizes)
    for res, o in zip((delta, new_m, new_v), outs):
        flat = o.reshape(-1)
        for i, n in enumerate(small_names):
            res[n] = flat[sm_offs[i]:sm_offs[i + 1]].reshape(w[n].shape)

    return (loss, grad_x, *[gsum[n] for n in _WEIGHTS], *[delta[n] for n in _WEIGHTS],
            *[new_m[n] for n in _WEIGHTS], *[new_v[n] for n in _WEIGHTS])
```

```python
import functools
import math

import numpy as np
import jax
import jax.numpy as jnp
from jax import lax
from jax.experimental import pallas as pl
from jax.experimental.pallas import tpu as pltpu

F32 = jnp.float32
BF16 = jnp.bfloat16

N_DEV = 8
LANES = 128
HEAD_DIM = 64
ATTN_BLOCK = 128
BRANCH_DILATIONS = (1, 4, 16)
SSD_GROUPS = 2
SSD_STATE = 128
SSD_CHUNK = 128
SSD_CONV = 4
NORM_EPS = 1e-5
ADAM_LR, ADAM_B1, ADAM_B2, ADAM_EPS, ADAM_WD, ADAM_STEP = 0.001, 0.9, 0.999, 1e-08, 0.01, 10
VMEM_LIMIT_BYTES = 56 * 1024 * 1024
MESH = pl.DeviceIdType.MESH
NEG_INF = float("-inf")


def _params(*sem):
    return pltpu.CompilerParams(dimension_semantics=tuple(sem), vmem_limit_bytes=VMEM_LIMIT_BYTES)


def _pick(n, target, mult):
    best = None
    for t in range(mult, min(n, target) + 1, mult):
        if n % t == 0:
            best = t
    assert best is not None, (n, target, mult)
    return best


def _dot(a, b, ca, cb):
    return lax.dot_general(a, b, (((ca,), (cb,)), ((), ())), preferred_element_type=F32)


def _split3(v):
    hi = v.astype(BF16)
    r = v - hi.astype(F32)
    mid = r.astype(BF16)
    lo = (r - mid.astype(F32)).astype(BF16)
    return hi, mid, lo


def _dot_exact(v, sel, ca, cb):
    hi, mid, lo = _split3(v)
    return _dot(hi, sel, ca, cb) + _dot(mid, sel, ca, cb) + _dot(lo, sel, ca, cb)


_HBM = pl.BlockSpec(memory_space=pltpu.HBM)


def _all_gather(xs, name):
    n = len(xs)

    def body(*refs):
        x_refs, o_refs = refs[:n], refs[n:2 * n]
        send_sems, recv_sems, local_sems = refs[2 * n:]
        x, y, c = lax.axis_index("x"), lax.axis_index("y"), lax.axis_index("c")
        me, sibling = (x, y, c), (x, y, 1 - c)
        chips = [(1 - x, y), (x, 1 - y), (1 - x, 1 - y)]

        def copy(t, k, block, to, src=None):
            bx, by, bc = block
            dst = o_refs[t].at[4 * bx + 2 * by + bc]
            return pltpu.make_async_remote_copy(
                src_ref=dst if src is None else src, dst_ref=dst,
                send_sem=send_sems.at[t, k], recv_sem=recv_sems.at[t, k],
                device_id=to, device_id_type=MESH)

        mine = [pltpu.make_async_copy(x_refs[t], o_refs[t].at[4 * x + 2 * y + c], local_sems.at[t])
                for t in range(n)]
        first, passed = [], []
        for t in range(n):
            mine[t].start()
            cps = [copy(t, 0, me, sibling, src=x_refs[t])]
            cps += [copy(t, 1 + j, me, (*chip, c), src=x_refs[t]) for j, chip in enumerate(chips)]
            for cp in cps:
                cp.start()
            first += cps
        for t in range(n):
            for j, chip in enumerate(chips):
                copy(t, 1 + j, (*chip, c), me).wait_recv()
                fwd = copy(t, 4 + j, (*chip, c), sibling)
                fwd.start()
                passed.append(fwd)
        for t in range(n):
            copy(t, 0, sibling, me).wait_recv()
            for j, chip in enumerate(chips):
                copy(t, 4 + j, (*chip, 1 - c), me).wait_recv()
        for cp in first + passed:
            cp.wait_send()
        for t in range(n):
            mine[t].wait()

    return pl.pallas_call(
        body, name=name,
        out_shape=[jax.ShapeDtypeStruct((N_DEV,) + a.shape, a.dtype) for a in xs],
        in_specs=[_HBM] * n, out_specs=[_HBM] * n,
        scratch_shapes=[pltpu.SemaphoreType.DMA((n, 7)), pltpu.SemaphoreType.DMA((n, 7)),
                        pltpu.SemaphoreType.DMA((n,))],
    )(*xs)


def _place():
    x, y, c = lax.axis_index("x"), lax.axis_index("y"), lax.axis_index("c")
    return x, y, c, 4 * x + 2 * y + c, (x, y, 1 - c), [(1 - x, y), (x, 1 - y), (1 - x, 1 - y)]


def _remote(src, dst, send_sem, recv_sem, to):
    return pltpu.make_async_remote_copy(src_ref=src, dst_ref=dst, send_sem=send_sem, recv_sem=recv_sem,
                                        device_id=to, device_id_type=MESH)


class _Riding:
    aliases = {}

    def copies(self, ins, outs, sems):
        raise NotImplementedError

    def start(self, ins, outs, sems):
        local, out, _ = self.copies(ins, outs, sems)
        for cp in local + out:
            cp.start()

    def wait(self, ins, outs, sems):
        local, out, landing = self.copies(ins, outs, sems)
        for cp in landing:
            cp.wait_recv()
        for cp in out:
            cp.wait_send()
        for cp in local:
            cp.wait()


class _GatherSpread(_Riding):
    def __init__(self, xs):
        n = len(xs)
        self.ins = list(xs)
        self.out_shapes = [jax.ShapeDtypeStruct((N_DEV,) + a.shape, a.dtype) for a in xs]
        self.sem_shapes = [pltpu.SemaphoreType.DMA((n, 4)), pltpu.SemaphoreType.DMA((n, 4)),
                           pltpu.SemaphoreType.DMA((n,))]

    def copies(self, ins, outs, sems):
        send, recv, local_sems = sems
        _, _, c, me, sibling, chips = _place()
        targets = [sibling] + [(*chip, c) for chip in chips]
        local, out, landing = [], [], []
        for t in range(len(ins)):
            local.append(pltpu.make_async_copy(ins[t], outs[t].at[me], local_sems.at[t]))
            for k, to in enumerate(targets):
                out.append(_remote(ins[t], outs[t].at[me], send.at[t, k], recv.at[t, k], to))
                theirs = outs[t].at[4 * to[0] + 2 * to[1] + to[2]]
                landing.append(_remote(ins[t], theirs, send.at[t, k], recv.at[t, k], to))
        return local, out, landing


class _GatherPass(_Riding):
    def __init__(self, bufs):
        n = len(bufs)
        self.ins = list(bufs)
        self.out_shapes = [jax.ShapeDtypeStruct(b.shape, b.dtype) for b in bufs]
        self.aliases = {t: t for t in range(n)}
        self.sem_shapes = [pltpu.SemaphoreType.DMA((n, 3)), pltpu.SemaphoreType.DMA((n, 3))]

    def copies(self, ins, outs, sems):
        send, recv = sems
        _, _, c, _, sibling, chips = _place()
        out, landing = [], []
        for t in range(len(outs)):
            for j, (px, py) in enumerate(chips):
                got = outs[t].at[4 * px + 2 * py + c]
                out.append(_remote(got, got, send.at[t, j], recv.at[t, j], sibling))
                landing.append(_remote(got, outs[t].at[4 * px + 2 * py + 1 - c], send.at[t, j], recv.at[t, j], sibling))
        return [], out, landing


class _SiblingSwap(_Riding):
    def __init__(self, xs):
        n = len(xs)
        self.ins = list(xs)
        half = [jax.ShapeDtypeStruct((N_DEV // 2,) + a.shape[1:], a.dtype) for a in xs]
        self.out_shapes = half + half
        self.sem_shapes = [pltpu.SemaphoreType.DMA((n, 4)), pltpu.SemaphoreType.DMA((n, 4)),
                           pltpu.SemaphoreType.DMA((n, 4))]

    def copies(self, ins, outs, sems):
        send, recv, local_sems = sems
        n = len(ins)
        _, _, c, _, sibling, _ = _place()
        local, out, landing = [], [], []
        for t in range(n):
            for q in range(N_DEV // 2):
                local.append(pltpu.make_async_copy(ins[t].at[2 * q + c], outs[n + t].at[q], local_sems.at[t, q]))
                out.append(_remote(ins[t].at[2 * q + 1 - c], outs[t].at[q], send.at[t, q], recv.at[t, q], sibling))
                landing.append(_remote(ins[t].at[2 * q + 1 - c], outs[t].at[q], send.at[t, q], recv.at[t, q], sibling))
        return local, out, landing


class _ChipSend(_Riding):
    def __init__(self, ps):
        n = len(ps)
        self.ins = list(ps)
        self.out_shapes = [jax.ShapeDtypeStruct(a.shape, a.dtype) for a in ps]
        self.sem_shapes = [pltpu.SemaphoreType.DMA((n, 3)), pltpu.SemaphoreType.DMA((n, 3)),
                           pltpu.SemaphoreType.DMA((n,))]

    def copies(self, ins, outs, sems):
        send, recv, local_sems = sems
        x, y, c, _, _, chips = _place()
        my_chip = 2 * x + y
        local, out, landing = [], [], []
        for t in range(len(ins)):
            local.append(pltpu.make_async_copy(ins[t].at[my_chip], outs[t].at[my_chip], local_sems.at[t]))
            for j, (px, py) in enumerate(chips):
                q = 2 * px + py
                out.append(_remote(ins[t].at[q], outs[t].at[my_chip], send.at[t, j], recv.at[t, j], (px, py, c)))
                landing.append(_remote(ins[t].at[q], outs[t].at[q], send.at[t, j], recv.at[t, j], (px, py, c)))
        return local, out, landing


class _Rides:
    def __init__(self):
        self.plan, self.done = {}, {}

    def put(self, host, key, make):
        assert host not in self.plan, host
        self.plan[host] = (key, make)

    def board(self, host):
        return self.plan[host][1]() if host in self.plan else None

    def land(self, host, results):
        self.done[self.plan[host][0]] = list(results)


def _pallas(body, *, name, grid, out_shape, in_specs, out_specs, operands, semantics, scratch_shapes=(), rides=None):
    comm = rides.board(name) if rides is not None else None
    if comm is None:
        return pl.pallas_call(
            body, name=name, grid=grid, out_shape=list(out_shape), in_specs=list(in_specs),
            out_specs=list(out_specs), scratch_shapes=list(scratch_shapes), compiler_params=_params(*semantics),
        )(*operands)
    n_in, n_out, n_scr = len(in_specs), len(out_shape), len(scratch_shapes)
    n_ci, n_co = len(comm.ins), len(comm.out_shapes)

    def hosted(*refs):
        cuts = np.cumsum([0, n_in, n_ci, n_out, n_co, n_scr])
        ins, c_ins, outs, c_outs, scr = (refs[cuts[i]:cuts[i + 1]] for i in range(5))
        sems = refs[cuts[5]:]
        ids = [pl.program_id(a) for a in range(len(grid))]
        first = functools.reduce(jnp.logical_and, [i == 0 for i in ids])
        last = functools.reduce(jnp.logical_and, [i == g - 1 for i, g in zip(ids, grid)])

        @pl.when(first)
        def _():
            comm.start(c_ins, c_outs, sems)

        body(*ins, *outs, *scr)

        @pl.when(last)
        def _():
            comm.wait(c_ins, c_outs, sems)

    results = pl.pallas_call(
        hosted, name=name, grid=grid, out_shape=list(out_shape) + comm.out_shapes,
        in_specs=list(in_specs) + [_HBM] * n_ci, out_specs=list(out_specs) + [_HBM] * n_co,
        scratch_shapes=list(scratch_shapes) + comm.sem_shapes,
        input_output_aliases={n_in + i: n_out + j for i, j in comm.aliases.items()},
        compiler_params=_params(*["arbitrary"] * len(grid)),
    )(*operands, *comm.ins)
    rides.land(name, results[n_out:])
    return results[:n_out]


def _alone(rides, name):
    comm = rides.board(name)

    def body(*refs):
        n_ci, n_co = len(comm.ins), len(comm.out_shapes)
        ins, outs, sems = refs[:n_ci], refs[n_ci:n_ci + n_co], refs[n_ci + n_co:]
        comm.start(ins, outs, sems)
        comm.wait(ins, outs, sems)

    results = pl.pallas_call(
        body, name=name, out_shape=comm.out_shapes, in_specs=[_HBM] * len(comm.ins),
        out_specs=[_HBM] * len(comm.out_shapes), scratch_shapes=comm.sem_shapes,
        input_output_aliases=dict(comm.aliases),
    )(*comm.ins)
    rides.land(name, results)


def _row_tile(rows, cols, itemsize, copies, budget=24 * 1024 * 1024):
    padded = -(-cols // LANES) * LANES
    mult = 8 * (4 // itemsize)
    if rows % mult:
        return rows
    return _pick(rows, max(mult, budget // (copies * padded * itemsize)), mult)


def _sum_leading(xs, name, out_dtype=F32):
    n_src, rows, cols = xs[0].shape
    pairwise = len(xs) > 1
    blocks = (len(xs) + 1) if pairwise else (n_src + 2)
    tr = _row_tile(rows, cols, max(a.dtype.itemsize for a in xs), 2 * blocks)

    def body(*refs):
        o_ref = refs[-1]
        if pairwise:
            acc = refs[0][...].astype(F32)
            for r in refs[1:-1]:
                acc = acc + r[...].astype(F32)
        else:
            acc = refs[0][0].astype(F32)
            for s in range(1, n_src):
                acc = acc + refs[0][s].astype(F32)
        o_ref[...] = acc.astype(out_dtype)

    if pairwise:
        spec = pl.BlockSpec((None, tr, cols), lambda s, i: (s, i, 0))
        return pl.pallas_call(
            body, name=name, grid=(n_src, rows // tr), out_shape=jax.ShapeDtypeStruct((n_src, rows, cols), out_dtype),
            in_specs=[spec] * len(xs), out_specs=spec, compiler_params=_params("parallel", "parallel"),
        )(*xs)
    return pl.pallas_call(
        body, name=name, grid=(rows // tr,), out_shape=jax.ShapeDtypeStruct((rows, cols), out_dtype),
        in_specs=[pl.BlockSpec((n_src, tr, cols), lambda i: (0, i, 0))],
        out_specs=pl.BlockSpec((tr, cols), lambda i: (i, 0)), compiler_params=_params("parallel"),
    )(xs[0])


def _mm(a, b, *, name, ta=False, tb=False, tm=1024, tn=512, out_dtype=F32, a_act=None,
        residual=None, gate=None, out_chunk=None, rides=None):
    k_dim, m = (a.shape if ta else a.shape[::-1])
    n, kb = (b.shape if tb else b.shape[::-1])
    assert kb == k_dim, (a.shape, b.shape, ta, tb)
    tm, tn = _pick(m, tm, 128), _pick(out_chunk or n, tn, 128)
    ca, cb = (0 if ta else 1), (1 if tb else 0)
    a_spec = pl.BlockSpec((k_dim, tm), lambda i, j: (0, i)) if ta else pl.BlockSpec((tm, k_dim), lambda i, j: (i, 0))
    b_spec = pl.BlockSpec((tn, k_dim), lambda i, j: (j, 0)) if tb else pl.BlockSpec((k_dim, tn), lambda i, j: (0, j))
    mn_spec = pl.BlockSpec((tm, tn), lambda i, j: (i, j))
    if out_chunk:
        per = out_chunk // tn
        o_spec = pl.BlockSpec((None, tm, tn), lambda i, j: (j // per, i, j % per))
        out_shape = jax.ShapeDtypeStruct((n // out_chunk, m, out_chunk), out_dtype)
    else:
        o_spec = mn_spec
        out_shape = jax.ShapeDtypeStruct((m, n), out_dtype)
    operands, in_specs = [a, b], [a_spec, b_spec]
    for extra in (gate, residual):
        if extra is not None:
            operands.append(extra)
            in_specs.append(mn_spec)

    def body(*refs):
        a_ref, b_ref, o_ref = refs[0], refs[1], refs[-1]
        extras = list(refs[2:-1])
        gate_ref = extras.pop(0) if gate is not None else None
        res_ref = extras.pop(0) if residual is not None else None
        av = a_ref[...].astype(BF16)
        if a_act == "relu2":
            av = jnp.square(jnp.maximum(av, jnp.zeros_like(av)))
        r = _dot(av, b_ref[...].astype(BF16), ca, cb)
        if gate_ref is not None:
            r = r * (2.0 * jnp.maximum(gate_ref[...].astype(F32), 0.0))
        if res_ref is not None:
            r = r + res_ref[...].astype(F32)
        o_ref[...] = r.astype(out_dtype)

    return _pallas(body, name=name, grid=(m // tm, n // tn), out_shape=[out_shape], in_specs=in_specs,
                   out_specs=[o_spec], operands=operands, semantics=("parallel", "arbitrary"), rides=rides)[0]


def _rmsnorm_fwd(xs, seg_widths, g, name, tm=256):
    t_len = xs[0].shape[0]
    width = sum(x.shape[1] for x in xs)
    tm = _pick(t_len, tm, 16)
    n = len(xs)

    def body(*refs):
        x_refs, g_ref, o_ref = refs[:n], refs[n], refs[n + 1]
        col = 0
        for x_ref, widths in zip(x_refs, seg_widths):
            off = 0
            for w in widths:
                xv = x_ref[:, off:off + w].astype(F32)
                r = lax.rsqrt(jnp.mean(xv * xv, axis=1, keepdims=True) + NORM_EPS)
                o_ref[:, col:col + w] = (xv * r * g_ref[:, col:col + w]).astype(BF16)
                off += w
                col += w

    return pl.pallas_call(
        body, name=name, grid=(t_len // tm,),
        out_shape=jax.ShapeDtypeStruct((t_len, width), BF16),
        in_specs=[pl.BlockSpec((tm, x.shape[1]), lambda i: (i, 0)) for x in xs]
        + [pl.BlockSpec((1, width), lambda i: (0, 0))],
        out_specs=pl.BlockSpec((tm, width), lambda i: (i, 0)),
        compiler_params=_params("parallel"),
    )(*xs, g)


def _rmsnorm_bwd(xs, seg_widths, g, dh, residuals, name, tm=256, bf16_copy=False):
    t_len = xs[0].shape[0]
    width = sum(x.shape[1] for x in xs)
    tm = _pick(t_len, tm, 8)
    n = len(xs)
    has_res = [r is not None for r in residuals]
    res_ops = [r for r in residuals if r is not None]

    def body(*refs):
        x_refs, g_ref, dh_ref = refs[:n], refs[n], refs[n + 1]
        res_refs = list(refs[n + 2:n + 2 + len(res_ops)])
        dx_refs = refs[n + 2 + len(res_ops):n + 2 + len(res_ops) + n]
        dg_ref = refs[n + 2 + len(res_ops) + n]
        copy_refs = refs[n + 3 + len(res_ops) + n:]
        first = pl.program_id(0) == 0
        col = 0
        for idx, (x_ref, widths) in enumerate(zip(x_refs, seg_widths)):
            res_ref = res_refs.pop(0) if has_res[idx] else None
            off = 0
            for w in widths:
                xv = x_ref[:, off:off + w].astype(F32)
                r = lax.rsqrt(jnp.mean(xv * xv, axis=1, keepdims=True) + NORM_EPS)
                xh = xv * r
                dhv = dh_ref[:, col:col + w].astype(F32)
                gd = dhv * g_ref[:, col:col + w]
                dx = r * (gd - xh * jnp.mean(gd * xh, axis=1, keepdims=True))
                if res_ref is not None:
                    dx = dx + res_ref[:, off:off + w]
                dx_refs[idx][:, off:off + w] = dx
                if bf16_copy:
                    copy_refs[idx][:, off:off + w] = dx.astype(BF16)
                part = jnp.sum(dhv * xh, axis=0, keepdims=True)

                @pl.when(first)
                def _(part=part, col=col, w=w):
                    dg_ref[:, col:col + w] = part

                @pl.when(jnp.logical_not(first))
                def _(part=part, col=col, w=w):
                    dg_ref[:, col:col + w] += part
                off += w
                col += w

    outs = pl.pallas_call(
        body, name=name, grid=(t_len // tm,),
        out_shape=[jax.ShapeDtypeStruct(x.shape, F32) for x in xs] + [jax.ShapeDtypeStruct((1, width), F32)]
        + ([jax.ShapeDtypeStruct(x.shape, BF16) for x in xs] if bf16_copy else []),
        in_specs=[pl.BlockSpec((tm, x.shape[1]), lambda i: (i, 0)) for x in xs]
        + [pl.BlockSpec((1, width), lambda i: (0, 0)), pl.BlockSpec((tm, width), lambda i: (i, 0))]
        + [pl.BlockSpec((tm, r.shape[1]), lambda i: (i, 0)) for r in res_ops],
        out_specs=[pl.BlockSpec((tm, x.shape[1]), lambda i: (i, 0)) for x in xs]
        + [pl.BlockSpec((1, width), lambda i: (0, 0))]
        + ([pl.BlockSpec((tm, x.shape[1]), lambda i: (i, 0)) for x in xs] if bf16_copy else []),
        compiler_params=_params("arbitrary"),
    )(*xs, g, dh, *res_ops)
    return outs[:n], outs[n], outs[n + 1:]


def _loss_head(x, g, target, name, tm=256):
    t_len, d = x.shape
    tm = _pick(t_len, tm, 8)

    def body(x_ref, g_ref, t_ref, dx_ref, dg_ref, loss_ref, dxb_ref):
        first = pl.program_id(0) == 0
        xv = x_ref[...]
        r = lax.rsqrt(jnp.mean(xv * xv, axis=1, keepdims=True) + NORM_EPS)
        xh = xv * r
        gv = g_ref[...]
        err = xh * gv - t_ref[...]
        part_loss = 0.5 * jnp.sum(jnp.mean(err * err, axis=1, keepdims=True), axis=0, keepdims=True)
        dy = err * (1.0 / d)
        gd = dy * gv
        dx = r * (gd - xh * jnp.mean(gd * xh, axis=1, keepdims=True))
        dx_ref[...] = dx
        dxb_ref[...] = dx.astype(BF16)
        part_g = jnp.sum(dy * xh, axis=0, keepdims=True)
        part_loss = jnp.broadcast_to(part_loss, (1, LANES))

        @pl.when(first)
        def _():
            dg_ref[...] = part_g
            loss_ref[...] = part_loss

        @pl.when(jnp.logical_not(first))
        def _():
            dg_ref[...] += part_g
            loss_ref[...] += part_loss

    return pl.pallas_call(
        body, name=name, grid=(t_len // tm,),
        out_shape=[jax.ShapeDtypeStruct((t_len, d), F32), jax.ShapeDtypeStruct((1, d), F32),
                   jax.ShapeDtypeStruct((1, LANES), F32), jax.ShapeDtypeStruct((t_len, d), BF16)],
        in_specs=[pl.BlockSpec((tm, d), lambda i: (i, 0)), pl.BlockSpec((1, d), lambda i: (0, 0)),
                  pl.BlockSpec((tm, d), lambda i: (i, 0))],
        out_specs=[pl.BlockSpec((tm, d), lambda i: (i, 0)), pl.BlockSpec((1, d), lambda i: (0, 0)),
                   pl.BlockSpec((1, LANES), lambda i: (0, 0)), pl.BlockSpec((tm, d), lambda i: (i, 0))],
        compiler_params=_params("arbitrary"),
    )(x, g, target)


def _alibi_slope(h, n_heads):
    return jnp.exp(jnp.full((1, 1), -8.0 * math.log(2.0) / n_heads, F32) * (h + 1).astype(F32))


def _attn_tiles(d, w):
    return ATTN_BLOCK * d, (w if d == 1 else LANES)


def _residue_rows(r, d):
    return pl.ds(r, ATTN_BLOCK, stride=d) if d > 1 else pl.ds(0, ATTN_BLOCK)


def _attn_masks(first_block):
    i = lax.broadcasted_iota(jnp.int32, (ATTN_BLOCK, ATTN_BLOCK), 0)
    j = lax.broadcasted_iota(jnp.int32, (ATTN_BLOCK, ATTN_BLOCK), 1)
    valid_cur = j <= i
    valid_prev = jnp.logical_and(j >= i, jnp.logical_not(first_block))
    delta_cur = (i - j).astype(F32)
    delta_prev = (i - j + ATTN_BLOCK).astype(F32)
    return valid_cur, valid_prev, delta_cur, delta_prev


def _head_lane_masks():
    lane = lax.broadcasted_iota(jnp.int32, (ATTN_BLOCK, LANES), 1)
    return [lane < HEAD_DIM, lane >= HEAD_DIM]


def _attn_branch_fwd(proj, w, dilation, n_heads, name, rides=None):
    t_len = proj.shape[0]
    d = dilation
    rows, lw = _attn_tiles(d, w)
    nb = t_len // rows
    n_pairs = lw // LANES
    per = w // lw
    scale = HEAD_DIM ** -0.5

    def body(q_ref, kp_ref, kc_ref, vp_ref, vc_ref, o_ref, lse_ref):
        first_head = pl.program_id(0) * (2 * n_pairs)
        first_block = pl.program_id(1) == 0
        valid_cur, valid_prev, delta_cur, delta_prev = _attn_masks(first_block)
        masks = _head_lane_masks()
        for p in range(n_pairs):
            cols = pl.ds(p * LANES, LANES)
            slopes = [_alibi_slope(first_head + 2 * p + hh, n_heads) * d for hh in range(2)]
            for r in range(d):
                rs = _residue_rows(r, d)
                q = (q_ref[rs, cols] * scale).astype(BF16)
                kp, kc = kp_ref[rs, cols].astype(BF16), kc_ref[rs, cols].astype(BF16)
                vp, vc = vp_ref[rs, cols].astype(BF16), vc_ref[rs, cols].astype(BF16)
                outs, lses = [], []
                for hh in range(2):
                    qh = jnp.where(masks[hh], q, jnp.zeros_like(q))
                    s_cur = jnp.where(valid_cur, _dot(qh, kc, 1, 1) - slopes[hh] * delta_cur, NEG_INF)
                    s_prev = jnp.where(valid_prev, _dot(qh, kp, 1, 1) - slopes[hh] * delta_prev, NEG_INF)
                    m = jnp.maximum(jnp.max(s_cur, axis=1, keepdims=True), jnp.max(s_prev, axis=1, keepdims=True))
                    p_cur, p_prev = jnp.exp(s_cur - m), jnp.exp(s_prev - m)
                    den = jnp.sum(p_cur, axis=1, keepdims=True) + jnp.sum(p_prev, axis=1, keepdims=True)
                    acc = _dot(p_cur.astype(BF16), vc, 1, 0) + _dot(p_prev.astype(BF16), vp, 1, 0)
                    outs.append(acc / den)
                    lses.append(jnp.broadcast_to(m + jnp.log(den), (ATTN_BLOCK, LANES)))
                o_ref[rs, cols] = jnp.where(masks[0], outs[0], outs[1])
                lse_ref[rs, cols] = jnp.where(masks[0], lses[0], lses[1])

    def spec(which, prev):
        if prev:
            return pl.BlockSpec((rows, lw), lambda b, n: (jnp.maximum(n - 1, 0), which * per + b))
        return pl.BlockSpec((rows, lw), lambda b, n: (n, which * per + b))

    o_spec = pl.BlockSpec((rows, lw), lambda b, n: (n, b))
    return _pallas(
        body, name=name, grid=(per, nb), out_shape=[jax.ShapeDtypeStruct((t_len, w), F32)] * 2,
        in_specs=[spec(0, False), spec(1, True), spec(1, False), spec(2, True), spec(2, False)],
        out_specs=[o_spec, o_spec], operands=[proj] * 5, semantics=("parallel", "parallel"), rides=rides)


def _attn_combine(outs, lses, name, tm=512):
    t_len, w = outs[0].shape
    tm = _pick(t_len, tm, 8)
    nbr = len(outs)

    def body(*refs):
        o_refs, l_refs = refs[:nbr], refs[nbr:2 * nbr]
        out_ref, lse_ref = refs[2 * nbr:]
        ls = [r[...] for r in l_refs]
        m = functools.reduce(jnp.maximum, ls)
        es = [jnp.exp(l - m) for l in ls]
        den = functools.reduce(lambda a, b: a + b, es)
        num = functools.reduce(lambda a, b: a + b, [e * r[...] for e, r in zip(es, o_refs)])
        out_ref[...] = num / den
        lse_ref[...] = m + jnp.log(den)

    spec = pl.BlockSpec((tm, w), lambda i: (i, 0))
    return pl.pallas_call(
        body, name=name, grid=(t_len // tm,),
        out_shape=[jax.ShapeDtypeStruct((t_len, w), F32)] * 2,
        in_specs=[spec] * (2 * nbr), out_specs=[spec, spec],
        compiler_params=_params("parallel"),
    )(*outs, *lses)


def _attn_branch_bwd(proj, w, out, lse, dout, dilation, n_heads, name, acc=None, rides=None):
    t_len = proj.shape[0]
    d = dilation
    rows, lw = _attn_tiles(d, w)
    nb = t_len // rows
    n_pairs = lw // LANES
    per = w // lw
    scale = HEAD_DIM ** -0.5
    n_acc = 0 if acc is None else 3

    def body(*refs):
        q_ref, kp_ref, kc_ref, vp_ref, vc_ref, out_ref, lse_ref, do_ref = refs[:8]
        acc_refs = refs[8:8 + n_acc]
        dq_ref, dk_ref, dv_ref, dk_carry, dv_carry = refs[8 + n_acc:]
        first_head = pl.program_id(0) * (2 * n_pairs)
        n = pl.program_id(1)
        first_block = n == 0
        valid_cur, valid_prev, delta_cur, delta_prev = _attn_masks(first_block)
        masks = _head_lane_masks()

        def plus(value, idx, *where):
            return value + acc_refs[idx][where] if n_acc else value

        @pl.when(first_block)
        def _():
            dk_carry[...] = jnp.zeros_like(dk_carry)
            dv_carry[...] = jnp.zeros_like(dv_carry)

        @pl.when(n < nb)
        def _():
            for p in range(n_pairs):
                cols = pl.ds(p * LANES, LANES)
                slopes = [_alibi_slope(first_head + 2 * p + hh, n_heads) * d for hh in range(2)]
                for r in range(d):
                    rs = _residue_rows(r, d)
                    q = (q_ref[rs, cols] * scale).astype(BF16)
                    kp, kc = kp_ref[rs, cols].astype(BF16), kc_ref[rs, cols].astype(BF16)
                    vp, vc = vp_ref[rs, cols].astype(BF16), vc_ref[rs, cols].astype(BF16)
                    do = do_ref[rs, cols]
                    dob = do.astype(BF16)
                    do_out = do * out_ref[rs, cols]
                    lse_all = lse_ref[rs, cols]
                    dq = jnp.zeros((ATTN_BLOCK, LANES), F32)
                    dk_cur = jnp.zeros((ATTN_BLOCK, LANES), F32)
                    dk_prev = jnp.zeros((ATTN_BLOCK, LANES), F32)
                    dv_cur = jnp.zeros((ATTN_BLOCK, LANES), F32)
                    dv_prev = jnp.zeros((ATTN_BLOCK, LANES), F32)
                    for hh in range(2):
                        msk = masks[hh]
                        qh = jnp.where(msk, q, jnp.zeros_like(q))
                        doh = jnp.where(msk, dob, jnp.zeros_like(dob))
                        delta = jnp.sum(jnp.where(msk, do_out, 0.0), axis=1, keepdims=True)
                        lse_h = jnp.max(jnp.where(msk, lse_all, NEG_INF), axis=1, keepdims=True)
                        s_cur = jnp.where(valid_cur, _dot(qh, kc, 1, 1) - slopes[hh] * delta_cur, NEG_INF)
                        s_prev = jnp.where(valid_prev, _dot(qh, kp, 1, 1) - slopes[hh] * delta_prev, NEG_INF)
                        p_cur, p_prev = jnp.exp(s_cur - lse_h), jnp.exp(s_prev - lse_h)
                        ds_cur = (p_cur * (_dot(doh, vc, 1, 1) - delta)).astype(BF16)
                        ds_prev = (p_prev * (_dot(doh, vp, 1, 1) - delta)).astype(BF16)
                        kch = jnp.where(msk, kc, jnp.zeros_like(kc))
                        kph = jnp.where(msk, kp, jnp.zeros_like(kp))
                        dq = dq + _dot(ds_cur, kch, 1, 0) + _dot(ds_prev, kph, 1, 0)
                        dk_cur = dk_cur + _dot(ds_cur, qh, 0, 0)
                        dk_prev = dk_prev + _dot(ds_prev, qh, 0, 0)
                        dv_cur = dv_cur + _dot(p_cur.astype(BF16), doh, 0, 0)
                        dv_prev = dv_prev + _dot(p_prev.astype(BF16), doh, 0, 0)
                    dq_ref[rs, cols] = plus(dq * scale, 0, rs, cols)
                    dk_ref[rs, cols] = plus(dk_carry[rs, cols] + dk_prev, 1, rs, cols)
                    dv_ref[rs, cols] = plus(dv_carry[rs, cols] + dv_prev, 2, rs, cols)
                    dk_carry[rs, cols] = dk_cur
                    dv_carry[rs, cols] = dv_cur

        @pl.when(n == nb)
        def _():
            dk_ref[...] = plus(dk_carry[...], 1, Ellipsis)
            dv_ref[...] = plus(dv_carry[...], 2, Ellipsis)

    def qkv_spec(which, shift):
        return pl.BlockSpec((rows, lw), lambda b, n: (jnp.clip(n - shift, 0, nb - 1), which * per + b))

    q_like = pl.BlockSpec((rows, lw), lambda b, n: (jnp.minimum(n, nb - 1), b))
    k_like = pl.BlockSpec((rows, lw), lambda b, n: (jnp.maximum(n - 1, 0), b))
    return _pallas(
        body, name=name, grid=(per, nb + 1), out_shape=[jax.ShapeDtypeStruct((t_len, w), F32)] * 3,
        in_specs=[qkv_spec(0, 0), qkv_spec(1, 1), qkv_spec(1, 0), qkv_spec(2, 1), qkv_spec(2, 0),
                  q_like, q_like, q_like] + [q_like, k_like, k_like][:n_acc],
        out_specs=[q_like, k_like, k_like], operands=[proj] * 5 + [out, lse, dout, *(acc or ())],
        scratch_shapes=[pltpu.VMEM((rows, lw), F32), pltpu.VMEM((rows, lw), F32)],
        semantics=("parallel", "arbitrary"), rides=rides)


def _shift_down(u, s):
    if s == 0:
        return u
    row = lax.broadcasted_iota(jnp.int32, u.shape, 0)
    return jnp.where(row >= s, pltpu.roll(u, s, 0), 0.0)


def _shift_up(u, s):
    if s == 0:
        return u
    n = u.shape[0]
    row = lax.broadcasted_iota(jnp.int32, u.shape, 0)
    return jnp.where(row < n - s, pltpu.roll(u, n - s, 0), 0.0)


def _conv_fwd(u, col0, w, b, name):
    t_len, ch = u.shape[0], w.shape[1]
    blk0 = col0 // LANES

    def body(u_ref, w_ref, b_ref, o_ref):
        uv = u_ref[...]
        pre = b_ref[...] + jnp.zeros_like(uv)
        for k in range(SSD_CONV):
            pre = pre + w_ref[k:k + 1, :] * _shift_down(uv, SSD_CONV - 1 - k)
        o_ref[...] = pre * jax.nn.sigmoid(pre)

    return pl.pallas_call(
        body, name=name, grid=(ch // LANES,),
        out_shape=jax.ShapeDtypeStruct((t_len, ch), F32),
        in_specs=[pl.BlockSpec((t_len, LANES), lambda j: (0, blk0 + j)),
                  pl.BlockSpec((SSD_CONV, LANES), lambda j: (0, j)), pl.BlockSpec((1, LANES), lambda j: (0, j))],
        out_specs=pl.BlockSpec((t_len, LANES), lambda j: (0, j)),
        compiler_params=_params("parallel"),
    )(u, w, b)


def _conv_bwd(u, col0, w, b, dact, name):
    t_len, ch = u.shape[0], w.shape[1]
    blk0 = col0 // LANES

    def body(u_ref, w_ref, b_ref, da_ref, du_ref, dw_ref, db_ref):
        uv = u_ref[...]
        shifted = [_shift_down(uv, SSD_CONV - 1 - k) for k in range(SSD_CONV)]
        pre = b_ref[...] + jnp.zeros_like(uv)
        for k in range(SSD_CONV):
            pre = pre + w_ref[k:k + 1, :] * shifted[k]
        sig = jax.nn.sigmoid(pre)
        dpre = da_ref[...] * (sig * (1.0 + pre * (1.0 - sig)))
        du = jnp.zeros_like(uv)
        for k in range(SSD_CONV):
            du = du + w_ref[k:k + 1, :] * _shift_up(dpre, SSD_CONV - 1 - k)
            dw_ref[k:k + 1, :] = jnp.sum(dpre * shifted[k], axis=0, keepdims=True)
        du_ref[...] = du
        db_ref[...] = jnp.sum(dpre, axis=0, keepdims=True)

    col = pl.BlockSpec((t_len, LANES), lambda j: (0, j))
    w_spec = pl.BlockSpec((SSD_CONV, LANES), lambda j: (0, j))
    b_spec = pl.BlockSpec((1, LANES), lambda j: (0, j))
    return pl.pallas_call(
        body, name=name, grid=(ch // LANES,),
        out_shape=[jax.ShapeDtypeStruct((t_len, ch), F32), jax.ShapeDtypeStruct((SSD_CONV, ch), F32),
                   jax.ShapeDtypeStruct((1, ch), F32)],
        in_specs=[pl.BlockSpec((t_len, LANES), lambda j: (0, blk0 + j)), w_spec, b_spec, col],
        out_specs=[col, w_spec, b_spec],
        compiler_params=_params("parallel"),
    )(u, w, b, dact)


def _cumsum_rows(v):
    n = v.shape[0]
    row = lax.broadcasted_iota(jnp.int32, v.shape, 0)
    s = 1
    while s < n:
        v = v + jnp.where(row >= s, pltpu.roll(v, s, 0), 0.0)
        s *= 2
    return v


def _rev_cumsum_rows(v):
    n = v.shape[0]
    row = lax.broadcasted_iota(jnp.int32, v.shape, 0)
    s = 1
    while s < n:
        v = v + jnp.where(row < n - s, pltpu.roll(v, n - s, 0), 0.0)
        s *= 2
    return v


def _head_selector(heads, width):
    j = lax.broadcasted_iota(jnp.int32, (LANES, width), 0)
    lane = lax.broadcasted_iota(jnp.int32, (LANES, width), 1)
    return jnp.where(jnp.logical_and(lane // HEAD_DIM == j, j < heads), 1.0, 0.0).astype(BF16)


class _SsdChunk:
    def __init__(self, dtraw_ref, bias_ref, alog_ref, xs_ref, b_ref, c_ref, heads):
        q = SSD_CHUNK
        width = heads * HEAD_DIM
        lane = lax.broadcasted_iota(jnp.int32, (q, LANES), 1)
        self.head_lanes = lane < heads
        lane1 = lax.broadcasted_iota(jnp.int32, (1, LANES), 1)
        self.a = jnp.where(lane1 < heads, -jnp.exp(alog_ref[...]), 0.0)
        self.dt_arg = dtraw_ref[...] + bias_ref[...]
        self.dt = jnp.where(self.head_lanes, jax.nn.softplus(self.dt_arg), 0.0)
        self.cum = _cumsum_rows(self.dt * self.a)
        self.cum_t = self.cum.T
        last = self.cum[q - 1:q, :]
        self.sel = _head_selector(heads, width)
        self.expand = lambda v: _dot_exact(v, self.sel, 1, 0)
        self.segsum = lambda v: _dot_exact(v, self.sel, 1, 1)
        self.e_exp = self.expand(jnp.exp(self.cum))
        self.d_exp = self.expand(jnp.exp(last - self.cum))
        self.elast_exp = self.e_exp[q - 1:q, :]
        self.dt_exp = self.expand(self.dt)
        self.xs = xs_ref[...]
        self.x = self.xs * self.dt_exp
        self.xb = self.x.astype(BF16)
        self.bb = b_ref[...].astype(BF16)
        self.cb = c_ref[...].astype(BF16)
        self.cbm = _dot(self.cb, self.bb, 1, 1)
        li = lax.broadcasted_iota(jnp.int32, (q, q), 0)
        si = lax.broadcasted_iota(jnp.int32, (q, q), 1)
        self.tri = li >= si
        hl = lax.broadcasted_iota(jnp.int32, (q, LANES), 1)
        self.pair_masks = [hl < HEAD_DIM, hl >= HEAD_DIM]

    def decay(self, j):
        diff = self.cum[:, j:j + 1] - self.cum_t[j:j + 1, :]
        return jnp.exp(jnp.where(self.tri, diff, NEG_INF))


def _ssd_specs(t_len, heads, n_chunks, xbc_cols, rev):
    q, gw = SSD_CHUNK, heads * HEAD_DIM
    ssd_w = SSD_GROUPS * gw
    b_blk = ssd_w // SSD_STATE
    ch = (lambda c: n_chunks - 1 - c) if rev else (lambda c: c)
    return dict(
        dtraw=pl.BlockSpec((None, q, LANES), lambda g, c: (g, ch(c), 0)),
        small=pl.BlockSpec((None, 1, LANES), lambda g, c: (g, 0, 0)),
        dsk=pl.BlockSpec((None, 1, gw), lambda g, c: (g, 0, 0)),
        xs=pl.BlockSpec((q, gw), lambda g, c: (ch(c), g)),
        b=pl.BlockSpec((q, SSD_STATE), lambda g, c: (ch(c), b_blk + g)),
        c=pl.BlockSpec((q, SSD_STATE), lambda g, c: (ch(c), b_blk + SSD_GROUPS + g)),
        z=pl.BlockSpec((q, gw), lambda g, c: (ch(c), 3 * SSD_GROUPS + g)),
        tok=pl.BlockSpec((q, gw), lambda g, c: (ch(c), g)),
        state=pl.BlockSpec((None, SSD_STATE, gw), lambda g, c: (ch(c), 0, g)),
        bc=pl.BlockSpec((q, SSD_STATE), lambda g, c: (ch(c), g)),
    )


def _ssd_fwd(xbc, qkvz, dtraw_g, bias_g, alog_g, dsk_exp, heads, name):
    t_len = xbc.shape[0]
    q, gw = SSD_CHUNK, heads * HEAD_DIM
    n_chunks = t_len // q
    ssd_w = SSD_GROUPS * gw
    sp = _ssd_specs(t_len, heads, n_chunks, xbc.shape[1], rev=False)

    def body(dtraw_ref, bias_ref, alog_ref, dsk_ref, xs_ref, b_ref, c_ref, z_ref,
             yg_ref, ypre_ref, st_ref, s_scr):
        @pl.when(pl.program_id(1) == 0)
        def _():
            s_scr[...] = jnp.zeros_like(s_scr)

        k = _SsdChunk(dtraw_ref, bias_ref, alog_ref, xs_ref, b_ref, c_ref, heads)
        s_prev = s_scr[...]
        st_ref[...] = s_prev
        y_off = k.e_exp * _dot(k.cb, s_prev.astype(BF16), 1, 0)
        parts = []
        for p in range(heads // 2):
            xp = k.xb[:, p * LANES:(p + 1) * LANES]
            acc = jnp.zeros((q, LANES), F32)
            for hh in range(2):
                m = (k.cbm * k.decay(2 * p + hh)).astype(BF16)
                acc = acc + _dot(m, jnp.where(k.pair_masks[hh], xp, jnp.zeros_like(xp)), 1, 0)
            parts.append(acc)
        y = jnp.concatenate(parts, axis=1) + y_off
        xd = (k.x * k.d_exp).astype(BF16)
        s_scr[...] = k.elast_exp * s_prev + _dot(k.bb, xd, 0, 0)
        y_pre = y + dsk_ref[...] * k.xs
        zv = z_ref[...]
        ypre_ref[...] = y_pre
        yg_ref[...] = y_pre * (zv * jax.nn.sigmoid(zv))

    return pl.pallas_call(
        body, name=name, grid=(SSD_GROUPS, n_chunks),
        out_shape=[jax.ShapeDtypeStruct((t_len, ssd_w), F32), jax.ShapeDtypeStruct((t_len, ssd_w), F32),
                   jax.ShapeDtypeStruct((n_chunks, SSD_STATE, ssd_w), F32)],
        in_specs=[sp["dtraw"], sp["small"], sp["small"], sp["dsk"], sp["xs"], sp["b"], sp["c"], sp["z"]],
        out_specs=[sp["tok"], sp["tok"], sp["state"]],
        scratch_shapes=[pltpu.VMEM((SSD_STATE, gw), F32)],
        compiler_params=_params("parallel", "arbitrary"),
    )(dtraw_g, bias_g, alog_g, dsk_exp, xbc, xbc, xbc, qkvz)


def _ssd_bwd(xbc, qkvz, dtraw_g, bias_g, alog_g, dsk_exp, ypre, states, dyg, heads, name):
    t_len = xbc.shape[0]
    q, gw = SSD_CHUNK, heads * HEAD_DIM
    n_chunks = t_len // q
    ssd_w = SSD_GROUPS * gw
    sp = _ssd_specs(t_len, heads, n_chunks, xbc.shape[1], rev=True)

    def body(dtraw_ref, bias_ref, alog_ref, dsk_ref, xs_ref, b_ref, c_ref, z_ref, ypre_ref, st_ref, dyg_ref,
             dxs_ref, db_ref, dc_ref, dz_ref, ddt_ref, small_ref, g_scr):
        first = pl.program_id(1) == 0

        @pl.when(first)
        def _():
            g_scr[...] = jnp.zeros_like(g_scr)

        k = _SsdChunk(dtraw_ref, bias_ref, alog_ref, xs_ref, b_ref, c_ref, heads)
        zv = z_ref[...]
        sig = jax.nn.sigmoid(zv)
        dyg = dyg_ref[...]
        y_pre = ypre_ref[...]
        dy = dyg * (zv * sig)
        dz_ref[...] = dyg * y_pre * (sig * (1.0 + zv * (1.0 - sig)))
        dsk = dsk_ref[...]
        g_next = g_scr[...]
        s_prev = st_ref[...]
        sb = s_prev.astype(BF16)
        xd = k.x * k.d_exp
        xdb = xd.astype(BF16)
        gb = g_next.astype(BF16)
        dx_off = k.d_exp * _dot(k.bb, gb, 1, 0)
        dyb = dy.astype(BF16)
        dcb = jnp.zeros((q, q), F32)
        lane = lax.broadcasted_iota(jnp.int32, (q, LANES), 1)
        row_t = lax.broadcasted_iota(jnp.int32, (LANES, q), 0)
        w_rows = jnp.zeros((q, LANES), F32)
        w_cols_t = jnp.zeros((LANES, q), F32)
        parts = []
        for p in range(heads // 2):
            cols = slice(p * LANES, (p + 1) * LANES)
            dyp, xp = dyb[:, cols], k.xb[:, cols]
            acc = jnp.zeros((q, LANES), F32)
            for hh in range(2):
                j = 2 * p + hh
                lm = k.decay(j)
                m32 = k.cbm * lm
                dym = jnp.where(k.pair_masks[hh], dyp, jnp.zeros_like(dyp))
                acc = acc + _dot(m32.astype(BF16), dym, 0, 0)
                dm = _dot(dym, xp, 1, 1)
                dcb = dcb + dm * lm
                wmat = dm * m32
                w_rows = w_rows + jnp.where(lane == j, jnp.sum(wmat, axis=1, keepdims=True), 0.0)
                w_cols_t = w_cols_t + jnp.where(row_t == j, jnp.sum(wmat, axis=0, keepdims=True), 0.0)
            parts.append(acc)
        dx = jnp.concatenate(parts, axis=1) + dx_off
        dcbb = dcb.astype(BF16)
        edy = (k.e_exp * dy).astype(BF16)
        dc_ref[...] = _dot(dcbb, k.bb, 1, 0) + _dot(edy, sb, 1, 1)
        db_ref[...] = _dot(dcbb, k.cb, 0, 0) + _dot(xdb, gb, 1, 1)
        g_scr[...] = k.elast_exp * g_next + _dot(k.cb, edy, 0, 0)

        y_off = k.e_exp * _dot(k.cb, sb, 1, 0)
        dcum = w_rows - w_cols_t.T + k.segsum(dy * y_off)
        t_term = k.segsum(k.x * dx_off)
        gs = jnp.broadcast_to(jnp.sum(g_next * s_prev, axis=0, keepdims=True), (8, gw))
        carried = k.segsum(gs)[0:1, :] * jnp.exp(k.cum[q - 1:q, :])
        dda = _rev_cumsum_rows(dcum) + (_cumsum_rows(t_term) - t_term) + carried
        ddt = jnp.where(k.head_lanes, dda * k.a + k.segsum(dx * k.xs), 0.0)
        ddtraw = ddt * jax.nn.sigmoid(k.dt_arg)
        ddt_ref[...] = ddtraw
        dxs_ref[...] = dx * k.dt_exp + dsk * dy
        ds = jnp.broadcast_to(jnp.sum(dy * k.xs, axis=0, keepdims=True), (8, gw))
        d_alog = jnp.sum(jnp.where(k.head_lanes, dda * k.dt, 0.0), axis=0, keepdims=True) * k.a
        rows8 = lax.broadcasted_iota(jnp.int32, (8, LANES), 0)
        small = jnp.where(rows8 == 0, d_alog, 0.0)
        small = small + jnp.where(rows8 == 1, jnp.sum(ddtraw, axis=0, keepdims=True), 0.0)
        small = small + jnp.where(rows8 == 2, k.segsum(ds)[0:1, :], 0.0)

        @pl.when(first)
        def _():
            small_ref[...] = small

        @pl.when(jnp.logical_not(first))
        def _():
            small_ref[...] += small

    bc_out = sp["bc"]
    return pl.pallas_call(
        body, name=name, grid=(SSD_GROUPS, n_chunks),
        out_shape=[jax.ShapeDtypeStruct((t_len, ssd_w), F32),
                   jax.ShapeDtypeStruct((t_len, SSD_GROUPS * SSD_STATE), F32),
                   jax.ShapeDtypeStruct((t_len, SSD_GROUPS * SSD_STATE), F32),
                   jax.ShapeDtypeStruct((t_len, ssd_w), F32),
                   jax.ShapeDtypeStruct((SSD_GROUPS, t_len, LANES), F32),
                   jax.ShapeDtypeStruct((SSD_GROUPS, 8, LANES), F32)],
        in_specs=[sp["dtraw"], sp["small"], sp["small"], sp["dsk"], sp["xs"], sp["b"], sp["c"], sp["z"],
                  sp["tok"], sp["state"], sp["tok"]],
        out_specs=[sp["tok"], bc_out, bc_out, sp["tok"], sp["dtraw"],
                   pl.BlockSpec((None, 8, LANES), lambda g, c: (g, 0, 0))],
        scratch_shapes=[pltpu.VMEM((SSD_STATE, gw), F32)],
        compiler_params=_params("parallel", "arbitrary"),
    )(dtraw_g, bias_g, alog_g, dsk_exp, xbc, xbc, xbc, qkvz, ypre, states, dyg)


def _adamw(w, g, m, v, name):
    rows, lanes = w.shape
    tr = _row_tile(rows, lanes, 4, 14)
    c1 = 1.0 / (1.0 - ADAM_B1 ** ADAM_STEP)
    c2 = 1.0 / (1.0 - ADAM_B2 ** ADAM_STEP)

    def body(w_ref, g_ref, m_ref, v_ref, d_ref, nm_ref, nv_ref):
        gv = g_ref[...]
        nm = ADAM_B1 * m_ref[...] + (1.0 - ADAM_B1) * gv
        nv = ADAM_B2 * v_ref[...] + (1.0 - ADAM_B2) * (gv * gv)
        nm_ref[...] = nm
        nv_ref[...] = nv
        d_ref[...] = -ADAM_LR * ((nm * c1) / (jnp.sqrt(nv * c2) + ADAM_EPS) + ADAM_WD * w_ref[...])

    spec = pl.BlockSpec((tr, lanes), lambda i: (i, 0))
    return pl.pallas_call(
        body, name=name, grid=(rows // tr,),
        out_shape=[jax.ShapeDtypeStruct((rows, lanes), F32)] * 3,
        in_specs=[spec] * 4, out_specs=[spec] * 3,
        compiler_params=_params("parallel"),
    )(w, g, m, v)


def _pad_lanes(a, width=LANES):
    return jnp.pad(a, ((0, 0), (0, width - a.shape[1])))


def _group_pad(v, heads):
    return _pad_lanes(v.reshape(SSD_GROUPS, heads))[:, None, :]


def _layer_fwd(x0, p, wt, dims, tag, rides):
    w_attn, heads_g, n_heads, conv_ch = dims["w_attn"], dims["heads_g"], dims["n_heads"], dims["conv_ch"]
    h1 = _rmsnorm_fwd([x0], [[x0.shape[1]]], p["ln1_g"], f"ln1_fwd{tag}")
    proj = _mm(h1, wt("w_in"), name=f"in_proj{tag}", tn=640, rides=rides)

    outs, lses = [], []
    for d in BRANCH_DILATIONS:
        o, l = _attn_branch_fwd(proj, w_attn, d, n_heads, f"attn_fwd_d{d}{tag}", rides)
        outs.append(o)
        lses.append(l)
    attn, lse = _attn_combine(outs, lses, f"attn_combine{tag}")

    xbc = _conv_fwd(proj, 4 * w_attn, p["conv_w"], p["conv_b"], f"conv_fwd{tag}")
    dt_col = 4 * w_attn + conv_ch
    dtraw_g = jnp.stack([_pad_lanes(proj[:, dt_col + g * heads_g:dt_col + (g + 1) * heads_g])
                         for g in range(SSD_GROUPS)])
    bias_g, alog_g = _group_pad(p["dt_bias"], heads_g), _group_pad(p["a_log"], heads_g)
    dsk_exp = jnp.repeat(p["d_skip"], HEAD_DIM).reshape(SSD_GROUPS, 1, heads_g * HEAD_DIM)
    yg, ypre, states = _ssd_fwd(xbc, proj, dtraw_g, bias_g, alog_g, dsk_exp, heads_g, f"ssd_fwd{tag}")

    gw = heads_g * HEAD_DIM
    mix_g = jnp.concatenate([p["attn_norm_g"], p["ssd_norm_g"]])[None, :]
    mix = _rmsnorm_fwd([attn, yg], [[w_attn], [gw] * SSD_GROUPS], mix_g, f"mix_norm_fwd{tag}")
    x1 = _mm(mix, wt("w_out"), name=f"out_proj{tag}", residual=x0, rides=rides)
    h2 = _rmsnorm_fwd([x1], [[x1.shape[1]]], p["ln2_g"], f"ln2_fwd{tag}")
    u = _mm(h2, wt("w_mlp_in"), name=f"mlp_in{tag}", out_dtype=BF16, tn=1024, rides=rides)
    x2 = _mm(u, wt("w_mlp_out"), name=f"mlp_out{tag}", a_act="relu2", residual=x1, tm=512, rides=rides)
    saved = dict(x0=x0, h1=h1, proj=proj, attn=attn, lse=lse, xbc=xbc, dtraw_g=dtraw_g,
                 bias_g=bias_g, alog_g=alog_g, dsk_exp=dsk_exp, yg=yg, ypre=ypre, states=states, mix=mix,
                 mix_g=mix_g, x1=x1, h2=h2, u=u)
    return x2, saved


def _pair_sums(ex, host, items):
    swapped = ex["rides"].done[("swap", host)]
    for i, (n, l) in enumerate(items):
        ex["pair"][(n, l)] = _sum_leading([swapped[len(items) + i], swapped[i]], f"pair_sum_{n}_l{l}", out_dtype=BF16)


def _layer_bwd(dx2, dx2_b, p, wt, s, dims, l, ex, copy_dx0):
    w_attn, heads_g, n_heads, conv_ch = dims["w_attn"], dims["heads_g"], dims["n_heads"], dims["conv_ch"]
    t_len, d_model = dx2.shape
    gw = heads_g * HEAD_DIM
    h_ssd = heads_g * SSD_GROUPS
    tag, rides, bufs = f"_l{l}", ex["rides"], ex["bufs"]
    du = _mm(dx2_b, wt("w_mlp_out"), name=f"mlp_out_dx{tag}", tb=True, gate=s["u"], out_dtype=BF16, tn=1024,
             rides=rides)
    d_wmo = _mm(s["u"], dx2_b, name=f"mlp_out_dw{tag}", ta=True, a_act="relu2", tm=512, tn=1024, out_dtype=BF16)
    bufs[("w_mlp_out", l)] = d_wmo.reshape(N_DEV, -1, d_model)
    bufs[("w_mlp_in", l)] = _mm(s["h2"], du, name=f"mlp_in_dw{tag}", ta=True, tm=512, tn=1024, out_dtype=BF16,
                                out_chunk=du.shape[1] // N_DEV)
    dh2 = _mm(du, wt("w_mlp_in"), name=f"mlp_in_dx{tag}", tb=True, tm=512, rides=rides)
    _pair_sums(ex, f"mlp_in_dx{tag}", [("w_mlp_out", l), ("w_mlp_in", l)])
    (dx1,), d_ln2, (dx1_b,) = _rmsnorm_bwd([s["x1"]], [[d_model]], p["ln2_g"], dh2, [dx2], f"ln2_bwd{tag}",
                                           bf16_copy=True)
    dmix = _mm(dx1_b, wt("w_out"), name=f"out_proj_dx{tag}", tb=True)
    d_wo = _mm(s["mix"], dx1_b, name=f"out_proj_dw{tag}", ta=True, tm=512, tn=1024, out_dtype=BF16)
    bufs[("w_out", l)] = d_wo.reshape(N_DEV, -1, d_model)
    (dattn, dyg), d_mix_g, _ = _rmsnorm_bwd([s["attn"], s["yg"]], [[w_attn], [gw] * SSD_GROUPS], s["mix_g"], dmix,
                                           [None, None], f"mix_norm_bwd{tag}")
    dxs, db, dc, dz, ddtraw_g, ssd_small = _ssd_bwd(
        s["xbc"], s["proj"], s["dtraw_g"], s["bias_g"], s["alog_g"], s["dsk_exp"], s["ypre"], s["states"], dyg,
        heads_g, f"ssd_bwd{tag}")
    dxbc = jnp.concatenate([dxs, db, dc], axis=1)
    dxbc_raw, d_conv_w, d_conv_b = _conv_bwd(s["proj"], 4 * w_attn, p["conv_w"], p["conv_b"], dxbc, f"conv_bwd{tag}")
    acc = None
    for d in BRANCH_DILATIONS:
        acc = _attn_branch_bwd(s["proj"], w_attn, s["attn"], s["lse"], dattn, d, n_heads, f"attn_bwd_d{d}{tag}", acc,
                               rides)
    w_in = wt("w_in")
    in_proj = 4 * w_attn + conv_ch + h_ssd
    pad = jnp.zeros((t_len, w_in.shape[1] - in_proj), F32)
    dproj = jnp.concatenate([*acc, dz, dxbc_raw] + [ddtraw_g[g, :, :heads_g] for g in range(SSD_GROUPS)] + [pad],
                            axis=1).astype(BF16)
    d_win = _mm(s["h1"], dproj, name=f"in_proj_dw{tag}", ta=True, tm=512, tn=1152)
    bufs[("w_in", l)] = d_win[:, :in_proj].reshape(d_model, N_DEV, -1).transpose(1, 0, 2).astype(BF16)
    dh1 = _mm(dproj, w_in, name=f"in_proj_dx{tag}", tb=True, tm=512, rides=rides)
    _pair_sums(ex, f"in_proj_dx{tag}", [("w_out", l), ("w_in", l)])
    (dx0,), d_ln1, dx0_b = _rmsnorm_bwd([s["x0"]], [[d_model]], p["ln1_g"], dh1, [dx1], f"ln1_bwd{tag}",
                                        bf16_copy=copy_dx0)

    small = ssd_small[:, :, :heads_g]
    grads = dict(
        ln1_g=d_ln1[0], conv_w=d_conv_w, conv_b=d_conv_b[0],
        a_log=small[:, 0].reshape(h_ssd), dt_bias=small[:, 1].reshape(h_ssd), d_skip=small[:, 2].reshape(h_ssd),
        attn_norm_g=d_mix_g[0, :w_attn], ssd_norm_g=d_mix_g[0, w_attn:], ln2_g=d_ln2[0])
    return dx0, (dx0_b[0] if copy_dx0 else None), grads


_SMALL = ["ln1_g", "conv_w", "conv_b", "dt_bias", "a_log", "d_skip", "attn_norm_g", "ssd_norm_g", "ln2_g"]
_WEIGHTS = ["ln1_g", "w_in", "conv_w", "conv_b", "dt_bias", "a_log", "d_skip", "attn_norm_g", "ssd_norm_g",
            "w_out", "ln2_g", "w_mlp_in", "w_mlp_out", "final_norm_g"]


def _to_rows(a):
    flat = a.reshape(-1)
    rows = -(-flat.shape[0] // LANES)
    rows = -(-rows // 8) * 8
    return jnp.pad(flat, (0, rows * LANES - flat.shape[0])).reshape(rows, LANES)


def kernel(x, ln1_g, w_in, conv_w, conv_b, dt_bias, a_log, d_skip, attn_norm_g, ssd_norm_g, w_out, ln2_g, w_mlp_in, w_mlp_out, final_norm_g, loss_target, m_ln1_g, m_w_in, m_conv_w, m_conv_b, m_dt_bias, m_a_log, m_d_skip, m_attn_norm_g, m_ssd_norm_g, m_w_out, m_ln2_g, m_w_mlp_in, m_w_mlp_out, m_final_norm_g, v_ln1_g, v_w_in, v_conv_w, v_conv_b, v_dt_bias, v_a_log, v_d_skip, v_attn_norm_g, v_ssd_norm_g, v_w_out, v_ln2_g, v_w_mlp_in, v_w_mlp_out, v_final_norm_g):
    w = dict(ln1_g=ln1_g, w_in=w_in, conv_w=conv_w, conv_b=conv_b, dt_bias=dt_bias, a_log=a_log, d_skip=d_skip,
             attn_norm_g=attn_norm_g, ssd_norm_g=ssd_norm_g, w_out=w_out, ln2_g=ln2_g, w_mlp_in=w_mlp_in,
             w_mlp_out=w_mlp_out, final_norm_g=final_norm_g)
    mom = dict(ln1_g=m_ln1_g, w_in=m_w_in, conv_w=m_conv_w, conv_b=m_conv_b, dt_bias=m_dt_bias, a_log=m_a_log,
               d_skip=m_d_skip, attn_norm_g=m_attn_norm_g, ssd_norm_g=m_ssd_norm_g, w_out=m_w_out, ln2_g=m_ln2_g,
               w_mlp_in=m_w_mlp_in, w_mlp_out=m_w_mlp_out, final_norm_g=m_final_norm_g)
    var = dict(ln1_g=v_ln1_g, w_in=v_w_in, conv_w=v_conv_w, conv_b=v_conv_b, dt_bias=v_dt_bias, a_log=v_a_log,
               d_skip=v_d_skip, attn_norm_g=v_attn_norm_g, ssd_norm_g=v_ssd_norm_g, w_out=v_w_out, ln2_g=v_ln2_g,
               w_mlp_in=v_w_mlp_in, w_mlp_out=v_w_mlp_out, final_norm_g=v_final_norm_g)

    depth, d_model = ln1_g.shape
    t_len = x.shape[1]
    w_attn = attn_norm_g.shape[1]
    h_ssd = dt_bias.shape[1]
    conv_ch = conv_b.shape[1]
    in_proj = w_in.shape[2] * N_DEV
    assert ssd_norm_g.shape[1] == w_attn and in_proj == 4 * w_attn + conv_ch + h_ssd
    assert t_len % (BRANCH_DILATIONS[-1] * ATTN_BLOCK) == 0 and h_ssd % (2 * SSD_GROUPS) == 0
    dims = dict(w_attn=w_attn, heads_g=h_ssd // SSD_GROUPS, n_heads=w_attn // HEAD_DIM, conv_ch=conv_ch)
    names = ["w_in", "w_out", "w_mlp_in", "w_mlp_out"]

    rides = _Rides()
    ex = dict(rides=rides, bufs={}, pair={})
    where, sent = {}, {}

    def shard(n, l):
        return w[n][l].astype(BF16)

    def plan_spread(host, items):
        rides.put(host, ("spread", host), lambda: _GatherSpread([shard(n, l) for n, l in items]))
        return host, items

    def plan_pass(host, spreads):
        rides.put(host, ("pass", host),
                  lambda: _GatherPass([b for h, _ in spreads for b in rides.done[("spread", h)]]))
        for i, item in enumerate([it for _, items in spreads for it in items]):
            where[item] = (("pass", host), i)

    def plan_swap(host, items):
        rides.put(host, ("swap", host), lambda: _SiblingSwap([ex["bufs"][it] for it in items]))

    def plan_send(host, items):
        rides.put(host, ("send", host), lambda: _ChipSend([ex["pair"][it] for it in items]))
        for i, item in enumerate(items):
            sent[item] = (("send", host), i)

    for l in range(depth):
        t, nxt = f"_l{l}", f"_l{l + 1}"
        if l == 0:
            plan_pass(f"attn_fwd_d16{t}", [plan_spread(f"in_proj{t}", [("w_out", 0)]),
                                          plan_spread(f"attn_fwd_d1{t}", [("w_mlp_in", 0)]),
                                          plan_spread(f"attn_fwd_d4{t}", [("w_mlp_out", 0)])])
        else:
            plan_pass(f"attn_fwd_d1{t}", [plan_spread(f"in_proj{t}", [("w_mlp_out", l)])])
        if l + 1 < depth:
            plan_pass(f"pass_weights{nxt}", [plan_spread(f"out_proj{t}", [("w_out", l + 1)]),
                                             plan_spread(f"mlp_in{t}", [("w_in", l + 1)]),
                                             plan_spread(f"mlp_out{t}", [("w_mlp_in", l + 1)])])
        plan_swap(f"mlp_in_dx{t}", [("w_mlp_out", l), ("w_mlp_in", l)])
        plan_send(f"attn_bwd_d1{t}", [("w_mlp_out", l)])
        plan_send(f"attn_bwd_d4{t}", [("w_mlp_in", l)])
        plan_swap(f"in_proj_dx{t}", [("w_out", l), ("w_in", l)])
        plan_send(f"mlp_out_dx_l{l - 1}" if l > 0 else "send_last_grads", [("w_out", l), ("w_in", l)])

    g_in0, g_cw = _all_gather([shard("w_in", 0), conv_w], "gather_first")
    full_cw = g_cw.transpose(1, 2, 0, 3).reshape(depth, SSD_CONV, conv_ch)
    proj_cols = -(-in_proj // LANES) * LANES
    full = {}

    def weight(n, l):
        if (n, l) not in full:
            if (n, l) == ("w_in", 0):
                g = g_in0
            else:
                key, i = where[(n, l)]
                g = rides.done[key][i]
            if n == "w_in":
                g = _pad_lanes(g.transpose(1, 0, 2).reshape(d_model, in_proj), proj_cols)
            elif n == "w_mlp_in":
                g = g.transpose(1, 0, 2).reshape(d_model, -1)
            else:
                g = g.reshape(-1, d_model)
            full[(n, l)] = g
        return full[(n, l)]

    layers = [dict(ln1_g=ln1_g[l][None, :], ln2_g=ln2_g[l][None, :], conv_w=full_cw[l], conv_b=conv_b[l][None, :],
                   dt_bias=dt_bias[l], a_log=a_log[l], d_skip=d_skip[l], attn_norm_g=attn_norm_g[l],
                   ssd_norm_g=ssd_norm_g[l]) for l in range(depth)]

    h = x[0]
    saved = []
    for l in range(depth):
        h, s = _layer_fwd(h, layers[l], functools.partial(lambda n, l: weight(n, l), l=l), dims, f"_l{l}", rides)
        saved.append(s)
        if l + 1 < depth:
            _alone(rides, f"pass_weights_l{l + 1}")
    dh, d_final_g, loss_part, dh_b = _loss_head(h, final_norm_g[None, :], loss_target[0], "loss_head")

    grads = [None] * depth
    for l in reversed(range(depth)):
        dh, dh_b, grads[l] = _layer_bwd(dh, dh_b, layers[l], functools.partial(lambda n, l: weight(n, l), l=l),
                                        saved[l], dims, l, ex, copy_dx0=l > 0)
    grad_x = dh[None]
    _alone(rides, "send_last_grads")

    gsum = {}
    for n in names:
        per_layer = []
        for l in range(depth):
            key, i = sent[(n, l)]
            per_layer.append(_sum_leading([rides.done[key][i]], f"sum_{n}_l{l}"))
        gsum[n] = jnp.stack(per_layer)

    small_parts = [jnp.stack([grads[l][n] for l in range(depth)]).reshape(-1) for n in _SMALL]
    small_parts += [d_final_g.reshape(-1), loss_part[0, :1]]
    sizes = [int(a.shape[0]) for a in small_parts]
    packed = _to_rows(jnp.concatenate(small_parts))
    (gathered,) = _all_gather([packed], "gather_small_grads")
    total = _sum_leading([gathered], "sum_small_grads").reshape(-1)
    offs = np.cumsum([0] + sizes)
    pieces = [total[offs[i]:offs[i + 1]] for i in range(len(sizes))]
    for n, piece in zip(_SMALL, pieces):
        shape = (depth, SSD_CONV, conv_ch) if n == "conv_w" else w[n].shape
        gsum[n] = piece.reshape(shape)
    gsum["final_norm_g"] = pieces[len(_SMALL)]
    loss = pieces[len(_SMALL) + 1][0]
    my_id = 4 * lax.axis_index("x") + 2 * lax.axis_index("y") + lax.axis_index("c")
    cw = conv_w.shape[2]
    gsum["conv_w"] = lax.dynamic_slice_in_dim(gsum["conv_w"], my_id * cw, cw, axis=2)

    delta, new_m, new_v = {}, {}, {}
    for n in names:
        outs = _adamw(*(a.reshape(-1, w[n].shape[-1]) for a in (w[n], gsum[n], mom[n], var[n])), f"adamw_{n}")
        delta[n], new_m[n], new_v[n] = (o.reshape(w[n].shape) for o in outs)
    small_names = [n for n in _WEIGHTS if n not in names]
    sm_sizes = [int(np.prod(w[n].shape)) for n in small_names]
    pack = lambda d: _to_rows(jnp.concatenate([d[n].reshape(-1) for n in small_names]))
    outs = _adamw(pack(w), pack(gsum), pack(mom), pack(var), "adamw_small")
    sm_offs = np.cumsum([0] + sm_sizes)
    for res, o in zip((delta, new_m, new_v), outs):
        flat = o.reshape(-1)
        for i, n in enumerate(small_names):
            res[n] = flat[sm_offs[i]:sm_offs[i + 1]].reshape(w[n].shape)

    return (loss, grad_x, *[gsum[n] for n in _WEIGHTS], *[delta[n] for n in _WEIGHTS],
            *[new_m[n] for n in _WEIGHTS], *[new_v[n] for n in _WEIGHTS])
```

```python
import functools
import math

import numpy as np
import jax
import jax.numpy as jnp
from jax import lax
from jax.experimental import pallas as pl
from jax.experimental.pallas import tpu as pltpu

F32 = jnp.float32
BF16 = jnp.bfloat16

N_DEV = 8
LANES = 128
HEAD_DIM = 64
ATTN_BLOCK = 128
BRANCH_DILATIONS = (1, 4, 16)
SSD_GROUPS = 2
SSD_STATE = 128
SSD_CHUNK = 128
SSD_CONV = 4
NORM_EPS = 1e-5
ADAM_LR, ADAM_B1, ADAM_B2, ADAM_EPS, ADAM_WD, ADAM_STEP = 0.001, 0.9, 0.999, 1e-08, 0.01, 10
VMEM_LIMIT_BYTES = 56 * 1024 * 1024
MESH = pl.DeviceIdType.MESH
NEG_INF = float("-inf")


def _params(*sem):
    return pltpu.CompilerParams(dimension_semantics=tuple(sem), vmem_limit_bytes=VMEM_LIMIT_BYTES)


def _pick(n, target, mult):
    best = None
    for t in range(mult, min(n, target) + 1, mult):
        if n % t == 0:
            best = t
    assert best is not None, (n, target, mult)
    return best


def _dot(a, b, ca, cb):
    return lax.dot_general(a, b, (((ca,), (cb,)), ((), ())), preferred_element_type=F32)


def _split3(v):
    hi = v.astype(BF16)
    r = v - hi.astype(F32)
    mid = r.astype(BF16)
    lo = (r - mid.astype(F32)).astype(BF16)
    return hi, mid, lo


def _dot_exact(v, sel, ca, cb):
    hi, mid, lo = _split3(v)
    return _dot(hi, sel, ca, cb) + _dot(mid, sel, ca, cb) + _dot(lo, sel, ca, cb)


_HBM = pl.BlockSpec(memory_space=pltpu.HBM)


def _all_gather(xs, name):
    n = len(xs)

    def body(*refs):
        x_refs, o_refs = refs[:n], refs[n:2 * n]
        send_sems, recv_sems, local_sems = refs[2 * n:]
        x, y, c = lax.axis_index("x"), lax.axis_index("y"), lax.axis_index("c")
        me, sibling = (x, y, c), (x, y, 1 - c)
        chips = [(1 - x, y), (x, 1 - y), (1 - x, 1 - y)]

        def copy(t, k, block, to, src=None):
            bx, by, bc = block
            dst = o_refs[t].at[4 * bx + 2 * by + bc]
            return pltpu.make_async_remote_copy(
                src_ref=dst if src is None else src, dst_ref=dst,
                send_sem=send_sems.at[t, k], recv_sem=recv_sems.at[t, k],
                device_id=to, device_id_type=MESH)

        mine = [pltpu.make_async_copy(x_refs[t], o_refs[t].at[4 * x + 2 * y + c], local_sems.at[t])
                for t in range(n)]
        first, passed = [], []
        for t in range(n):
            mine[t].start()
            cps = [copy(t, 0, me, sibling, src=x_refs[t])]
            cps += [copy(t, 1 + j, me, (*chip, c), src=x_refs[t]) for j, chip in enumerate(chips)]
            for cp in cps:
                cp.start()
            first += cps
        for t in range(n):
            for j, chip in enumerate(chips):
                copy(t, 1 + j, (*chip, c), me).wait_recv()
                fwd = copy(t, 4 + j, (*chip, c), sibling)
                fwd.start()
                passed.append(fwd)
        for t in range(n):
            copy(t, 0, sibling, me).wait_recv()
            for j, chip in enumerate(chips):
                copy(t, 4 + j, (*chip, 1 - c), me).wait_recv()
        for cp in first + passed:
            cp.wait_send()
        for t in range(n):
            mine[t].wait()

    return pl.pallas_call(
        body, name=name,
        out_shape=[jax.ShapeDtypeStruct((N_DEV,) + a.shape, a.dtype) for a in xs],
        in_specs=[_HBM] * n, out_specs=[_HBM] * n,
        scratch_shapes=[pltpu.SemaphoreType.DMA((n, 7)), pltpu.SemaphoreType.DMA((n, 7)),
                        pltpu.SemaphoreType.DMA((n,))],
    )(*xs)


def _place():
    x, y, c = lax.axis_index("x"), lax.axis_index("y"), lax.axis_index("c")
    return x, y, c, 4 * x + 2 * y + c, (x, y, 1 - c), [(1 - x, y), (x, 1 - y), (1 - x, 1 - y)]


def _remote(src, dst, send_sem, recv_sem, to):
    return pltpu.make_async_remote_copy(src_ref=src, dst_ref=dst, send_sem=send_sem, recv_sem=recv_sem,
                                        device_id=to, device_id_type=MESH)


class _Riding:
    aliases = {}

    def copies(self, ins, outs, sems):
        raise NotImplementedError

    def start(self, ins, outs, sems):
        local, out, _ = self.copies(ins, outs, sems)
        for cp in local + out:
            cp.start()

    def wait(self, ins, outs, sems):
        local, out, landing = self.copies(ins, outs, sems)
        for cp in landing:
            cp.wait_recv()
        for cp in out:
            cp.wait_send()
        for cp in local:
            cp.wait()


class _GatherSpread(_Riding):
    def __init__(self, xs):
        n = len(xs)
        self.ins = list(xs)
        self.out_shapes = [jax.ShapeDtypeStruct((N_DEV,) + a.shape, a.dtype) for a in xs]
        self.sem_shapes = [pltpu.SemaphoreType.DMA((n, 4)), pltpu.SemaphoreType.DMA((n, 4)),
                           pltpu.SemaphoreType.DMA((n,))]

    def copies(self, ins, outs, sems):
        send, recv, local_sems = sems
        _, _, c, me, sibling, chips = _place()
        targets = [sibling] + [(*chip, c) for chip in chips]
        local, out, landing = [], [], []
        for t in range(len(ins)):
            local.append(pltpu.make_async_copy(ins[t], outs[t].at[me], local_sems.at[t]))
            for k, to in enumerate(targets):
                out.append(_remote(ins[t], outs[t].at[me], send.at[t, k], recv.at[t, k], to))
                theirs = outs[t].at[4 * to[0] + 2 * to[1] + to[2]]
                landing.append(_remote(ins[t], theirs, send.at[t, k], recv.at[t, k], to))
        return local, out, landing


class _GatherPass(_Riding):
    def __init__(self, bufs):
        n = len(bufs)
        self.ins = list(bufs)
        self.out_shapes = [jax.ShapeDtypeStruct(b.shape, b.dtype) for b in bufs]
        self.aliases = {t: t for t in range(n)}
        self.sem_shapes = [pltpu.SemaphoreType.DMA((n, 3)), pltpu.SemaphoreType.DMA((n, 3))]

    def copies(self, ins, outs, sems):
        send, recv = sems
        _, _, c, _, sibling, chips = _place()
        out, landing = [], []
        for t in range(len(outs)):
            for j, (px, py) in enumerate(chips):
                got = outs[t].at[4 * px + 2 * py + c]
                out.append(_remote(got, got, send.at[t, j], recv.at[t, j], sibling))
                landing.append(_remote(got, outs[t].at[4 * px + 2 * py + 1 - c], send.at[t, j], recv.at[t, j], sibling))
        return [], out, landing


class _SiblingSwap(_Riding):
    def __init__(self, xs):
        n = len(xs)
        self.ins = list(xs)
        self.out_shapes = [jax.ShapeDtypeStruct((N_DEV // 2,) + a.shape[1:], a.dtype) for a in xs]
        self.sem_shapes = [pltpu.SemaphoreType.DMA((n, 4)), pltpu.SemaphoreType.DMA((n, 4))]

    def copies(self, ins, outs, sems):
        send, recv = sems
        _, _, c, _, sibling, _ = _place()
        out = [_remote(ins[t].at[2 * q + 1 - c], outs[t].at[q], send.at[t, q], recv.at[t, q], sibling)
               for t in range(len(ins)) for q in range(N_DEV // 2)]
        return [], out, out


class _ChipSend(_Riding):
    def __init__(self, ps):
        n = len(ps)
        self.ins = list(ps)
        self.out_shapes = [jax.ShapeDtypeStruct((3,) + a.shape[1:], a.dtype) for a in ps]
        self.sem_shapes = [pltpu.SemaphoreType.DMA((n, 3)), pltpu.SemaphoreType.DMA((n, 3))]

    def copies(self, ins, outs, sems):
        send, recv = sems
        _, _, c, _, _, chips = _place()
        out = [_remote(ins[t].at[2 * px + py], outs[t].at[j], send.at[t, j], recv.at[t, j], (px, py, c))
               for t in range(len(ins)) for j, (px, py) in enumerate(chips)]
        return [], out, out


class _Rides:
    def __init__(self):
        self.plan, self.done = {}, {}

    def put(self, host, key, make):
        assert host not in self.plan, host
        self.plan[host] = (key, make)

    def board(self, host):
        return self.plan[host][1]() if host in self.plan else None

    def land(self, host, results):
        self.done[self.plan[host][0]] = list(results)


def _pallas(body, *, name, grid, out_shape, in_specs, out_specs, operands, semantics, scratch_shapes=(), rides=None):
    comm = rides.board(name) if rides is not None else None
    if comm is None:
        return pl.pallas_call(
            body, name=name, grid=grid, out_shape=list(out_shape), in_specs=list(in_specs),
            out_specs=list(out_specs), scratch_shapes=list(scratch_shapes), compiler_params=_params(*semantics),
        )(*operands)
    n_in, n_out, n_scr = len(in_specs), len(out_shape), len(scratch_shapes)
    n_ci, n_co = len(comm.ins), len(comm.out_shapes)

    def hosted(*refs):
        cuts = np.cumsum([0, n_in, n_ci, n_out, n_co, n_scr])
        ins, c_ins, outs, c_outs, scr = (refs[cuts[i]:cuts[i + 1]] for i in range(5))
        sems = refs[cuts[5]:]
        ids = [pl.program_id(a) for a in range(len(grid))]
        first = functools.reduce(jnp.logical_and, [i == 0 for i in ids])
        last = functools.reduce(jnp.logical_and, [i == g - 1 for i, g in zip(ids, grid)])

        @pl.when(first)
        def _():
            comm.start(c_ins, c_outs, sems)

        body(*ins, *outs, *scr)

        @pl.when(last)
        def _():
            comm.wait(c_ins, c_outs, sems)

    results = pl.pallas_call(
        hosted, name=name, grid=grid, out_shape=list(out_shape) + comm.out_shapes,
        in_specs=list(in_specs) + [_HBM] * n_ci, out_specs=list(out_specs) + [_HBM] * n_co,
        scratch_shapes=list(scratch_shapes) + comm.sem_shapes,
        input_output_aliases={n_in + i: n_out + j for i, j in comm.aliases.items()},
        compiler_params=_params(*["arbitrary"] * len(grid)),
    )(*operands, *comm.ins)
    rides.land(name, results[n_out:])
    return results[:n_out]


def _alone(rides, name):
    comm = rides.board(name)

    def body(*refs):
        n_ci, n_co = len(comm.ins), len(comm.out_shapes)
        ins, outs, sems = refs[:n_ci], refs[n_ci:n_ci + n_co], refs[n_ci + n_co:]
        comm.start(ins, outs, sems)
        comm.wait(ins, outs, sems)

    results = pl.pallas_call(
        body, name=name, out_shape=comm.out_shapes, in_specs=[_HBM] * len(comm.ins),
        out_specs=[_HBM] * len(comm.out_shapes), scratch_shapes=comm.sem_shapes,
        input_output_aliases=dict(comm.aliases),
    )(*comm.ins)
    rides.land(name, results)


def _row_tile(rows, cols, itemsize, copies, budget=24 * 1024 * 1024):
    padded = -(-cols // LANES) * LANES
    mult = 8 * (4 // itemsize)
    if rows % mult:
        return rows
    return _pick(rows, max(mult, budget // (copies * padded * itemsize)), mult)


def _sum_leading(xs, name, out_dtype=F32, first=None):
    n_src, rows, cols = xs[0].shape
    pairwise = len(xs) > 1
    blocks = (len(xs) + 1) if pairwise else (n_src + 3)
    tr = _row_tile(rows, cols, 4, 2 * blocks)

    def body(*refs):
        o_ref = refs[-1]
        if pairwise:
            acc = refs[0][...].astype(F32)
            for r in refs[1:-1]:
                acc = acc + r[...].astype(F32)
        else:
            terms = [refs[0][s] for s in range(n_src)]
            if first is not None:
                terms.insert(0, refs[1][...])
            acc = terms[0].astype(F32)
            for term in terms[1:]:
                acc = acc + term.astype(F32)
        o_ref[...] = acc.astype(out_dtype)

    if pairwise:
        spec = pl.BlockSpec((None, tr, cols), lambda s, i: (s, i, 0))
        return pl.pallas_call(
            body, name=name, grid=(n_src, rows // tr), out_shape=jax.ShapeDtypeStruct((n_src, rows, cols), out_dtype),
            in_specs=[spec] * len(xs), out_specs=spec, compiler_params=_params("parallel", "parallel"),
        )(*xs)
    flat = pl.BlockSpec((tr, cols), lambda i: (i, 0))
    return pl.pallas_call(
        body, name=name, grid=(rows // tr,), out_shape=jax.ShapeDtypeStruct((rows, cols), out_dtype),
        in_specs=[pl.BlockSpec((n_src, tr, cols), lambda i: (0, i, 0))] + ([flat] if first is not None else []),
        out_specs=flat, compiler_params=_params("parallel"),
    )(xs[0], *([first] if first is not None else []))


def _mm(a, b, *, name, ta=False, tb=False, tm=1024, tn=512, out_dtype=F32, a_act=None,
        residual=None, gate=None, out_chunk=None, rides=None):
    k_dim, m = (a.shape if ta else a.shape[::-1])
    n, kb = (b.shape if tb else b.shape[::-1])
    assert kb == k_dim, (a.shape, b.shape, ta, tb)
    tm, tn = _pick(m, tm, 128), _pick(out_chunk or n, tn, 128)
    ca, cb = (0 if ta else 1), (1 if tb else 0)
    a_spec = pl.BlockSpec((k_dim, tm), lambda i, j: (0, i)) if ta else pl.BlockSpec((tm, k_dim), lambda i, j: (i, 0))
    b_spec = pl.BlockSpec((tn, k_dim), lambda i, j: (j, 0)) if tb else pl.BlockSpec((k_dim, tn), lambda i, j: (0, j))
    mn_spec = pl.BlockSpec((tm, tn), lambda i, j: (i, j))
    if out_chunk:
        per = out_chunk // tn
        o_spec = pl.BlockSpec((None, tm, tn), lambda i, j: (j // per, i, j % per))
        out_shape = jax.ShapeDtypeStruct((n // out_chunk, m, out_chunk), out_dtype)
    else:
        o_spec = mn_spec
        out_shape = jax.ShapeDtypeStruct((m, n), out_dtype)
    operands, in_specs = [a, b], [a_spec, b_spec]
    for extra in (gate, residual):
        if extra is not None:
            operands.append(extra)
            in_specs.append(mn_spec)

    def body(*refs):
        a_ref, b_ref, o_ref = refs[0], refs[1], refs[-1]
        extras = list(refs[2:-1])
        gate_ref = extras.pop(0) if gate is not None else None
        res_ref = extras.pop(0) if residual is not None else None
        av = a_ref[...].astype(BF16)
        if a_act == "relu2":
            av = jnp.square(jnp.maximum(av, jnp.zeros_like(av)))
        r = _dot(av, b_ref[...].astype(BF16), ca, cb)
        if gate_ref is not None:
            r = r * (2.0 * jnp.maximum(gate_ref[...].astype(F32), 0.0))
        if res_ref is not None:
            r = r + res_ref[...].astype(F32)
        o_ref[...] = r.astype(out_dtype)

    return _pallas(body, name=name, grid=(m // tm, n // tn), out_shape=[out_shape], in_specs=in_specs,
                   out_specs=[o_spec], operands=operands, semantics=("parallel", "arbitrary"), rides=rides)[0]


def _rmsnorm_fwd(xs, seg_widths, g, name, tm=256):
    t_len = xs[0].shape[0]
    width = sum(x.shape[1] for x in xs)
    tm = _pick(t_len, tm, 16)
    n = len(xs)

    def body(*refs):
        x_refs, g_ref, o_ref = refs[:n], refs[n], refs[n + 1]
        col = 0
        for x_ref, widths in zip(x_refs, seg_widths):
            off = 0
            for w in widths:
                xv = x_ref[:, off:off + w].astype(F32)
                r = lax.rsqrt(jnp.mean(xv * xv, axis=1, keepdims=True) + NORM_EPS)
                o_ref[:, col:col + w] = (xv * r * g_ref[:, col:col + w]).astype(BF16)
                off += w
                col += w

    return pl.pallas_call(
        body, name=name, grid=(t_len // tm,),
        out_shape=jax.ShapeDtypeStruct((t_len, width), BF16),
        in_specs=[pl.BlockSpec((tm, x.shape[1]), lambda i: (i, 0)) for x in xs]
        + [pl.BlockSpec((1, width), lambda i: (0, 0))],
        out_specs=pl.BlockSpec((tm, width), lambda i: (i, 0)),
        compiler_params=_params("parallel"),
    )(*xs, g)


def _rmsnorm_bwd(xs, seg_widths, g, dh, residuals, name, tm=256, bf16_copy=False):
    t_len = xs[0].shape[0]
    width = sum(x.shape[1] for x in xs)
    tm = _pick(t_len, tm, 8)
    n = len(xs)
    has_res = [r is not None for r in residuals]
    res_ops = [r for r in residuals if r is not None]

    def body(*refs):
        x_refs, g_ref, dh_ref = refs[:n], refs[n], refs[n + 1]
        res_refs = list(refs[n + 2:n + 2 + len(res_ops)])
        dx_refs = refs[n + 2 + len(res_ops):n + 2 + len(res_ops) + n]
        dg_ref = refs[n + 2 + len(res_ops) + n]
        copy_refs = refs[n + 3 + len(res_ops) + n:]
        first = pl.program_id(0) == 0
        col = 0
        for idx, (x_ref, widths) in enumerate(zip(x_refs, seg_widths)):
            res_ref = res_refs.pop(0) if has_res[idx] else None
            off = 0
            for w in widths:
                xv = x_ref[:, off:off + w].astype(F32)
                r = lax.rsqrt(jnp.mean(xv * xv, axis=1, keepdims=True) + NORM_EPS)
                xh = xv * r
                dhv = dh_ref[:, col:col + w].astype(F32)
                gd = dhv * g_ref[:, col:col + w]
                dx = r * (gd - xh * jnp.mean(gd * xh, axis=1, keepdims=True))
                if res_ref is not None:
                    dx = dx + res_ref[:, off:off + w]
                dx_refs[idx][:, off:off + w] = dx
                if bf16_copy:
                    copy_refs[idx][:, off:off + w] = dx.astype(BF16)
                part = jnp.sum(dhv * xh, axis=0, keepdims=True)

                @pl.when(first)
                def _(part=part, col=col, w=w):
                    dg_ref[:, col:col + w] = part

                @pl.when(jnp.logical_not(first))
                def _(part=part, col=col, w=w):
                    dg_ref[:, col:col + w] += part
                off += w
                col += w

    outs = pl.pallas_call(
        body, name=name, grid=(t_len // tm,),
        out_shape=[jax.ShapeDtypeStruct(x.shape, F32) for x in xs] + [jax.ShapeDtypeStruct((1, width), F32)]
        + ([jax.ShapeDtypeStruct(x.shape, BF16) for x in xs] if bf16_copy else []),
        in_specs=[pl.BlockSpec((tm, x.shape[1]), lambda i: (i, 0)) for x in xs]
        + [pl.BlockSpec((1, width), lambda i: (0, 0)), pl.BlockSpec((tm, width), lambda i: (i, 0))]
        + [pl.BlockSpec((tm, r.shape[1]), lambda i: (i, 0)) for r in res_ops],
        out_specs=[pl.BlockSpec((tm, x.shape[1]), lambda i: (i, 0)) for x in xs]
        + [pl.BlockSpec((1, width), lambda i: (0, 0))]
        + ([pl.BlockSpec((tm, x.shape[1]), lambda i: (i, 0)) for x in xs] if bf16_copy else []),
        compiler_params=_params("arbitrary"),
    )(*xs, g, dh, *res_ops)
    return outs[:n], outs[n], outs[n + 1:]


def _loss_head(x, g, target, name, tm=256):
    t_len, d = x.shape
    tm = _pick(t_len, tm, 8)

    def body(x_ref, g_ref, t_ref, dx_ref, dg_ref, loss_ref, dxb_ref):
        first = pl.program_id(0) == 0
        xv = x_ref[...]
        r = lax.rsqrt(jnp.mean(xv * xv, axis=1, keepdims=True) + NORM_EPS)
        xh = xv * r
        gv = g_ref[...]
        err = xh * gv - t_ref[...]
        part_loss = 0.5 * jnp.sum(jnp.mean(err * err, axis=1, keepdims=True), axis=0, keepdims=True)
        dy = err * (1.0 / d)
        gd = dy * gv
        dx = r * (gd - xh * jnp.mean(gd * xh, axis=1, keepdims=True))
        dx_ref[...] = dx
        dxb_ref[...] = dx.astype(BF16)
        part_g = jnp.sum(dy * xh, axis=0, keepdims=True)
        part_loss = jnp.broadcast_to(part_loss, (1, LANES))

        @pl.when(first)
        def _():
            dg_ref[...] = part_g
            loss_ref[...] = part_loss

        @pl.when(jnp.logical_not(first))
        def _():
            dg_ref[...] += part_g
            loss_ref[...] += part_loss

    return pl.pallas_call(
        body, name=name, grid=(t_len // tm,),
        out_shape=[jax.ShapeDtypeStruct((t_len, d), F32), jax.ShapeDtypeStruct((1, d), F32),
                   jax.ShapeDtypeStruct((1, LANES), F32), jax.ShapeDtypeStruct((t_len, d), BF16)],
        in_specs=[pl.BlockSpec((tm, d), lambda i: (i, 0)), pl.BlockSpec((1, d), lambda i: (0, 0)),
                  pl.BlockSpec((tm, d), lambda i: (i, 0))],
        out_specs=[pl.BlockSpec((tm, d), lambda i: (i, 0)), pl.BlockSpec((1, d), lambda i: (0, 0)),
                   pl.BlockSpec((1, LANES), lambda i: (0, 0)), pl.BlockSpec((tm, d), lambda i: (i, 0))],
        compiler_params=_params("arbitrary"),
    )(x, g, target)


def _alibi_slope(h, n_heads):
    return jnp.exp(jnp.full((1, 1), -8.0 * math.log(2.0) / n_heads, F32) * (h + 1).astype(F32))


def _attn_tiles(d, w):
    return ATTN_BLOCK * d, (w if d == 1 else LANES)


def _residue_rows(r, d):
    return pl.ds(r, ATTN_BLOCK, stride=d) if d > 1 else pl.ds(0, ATTN_BLOCK)


def _attn_masks(first_block):
    i = lax.broadcasted_iota(jnp.int32, (ATTN_BLOCK, ATTN_BLOCK), 0)
    j = lax.broadcasted_iota(jnp.int32, (ATTN_BLOCK, ATTN_BLOCK), 1)
    valid_cur = j <= i
    valid_prev = jnp.logical_and(j >= i, jnp.logical_not(first_block))
    delta_cur = (i - j).astype(F32)
    delta_prev = (i - j + ATTN_BLOCK).astype(F32)
    return valid_cur, valid_prev, delta_cur, delta_prev


def _head_lane_masks():
    lane = lax.broadcasted_iota(jnp.int32, (ATTN_BLOCK, LANES), 1)
    return [lane < HEAD_DIM, lane >= HEAD_DIM]


def _attn_branch_fwd(proj, w, dilation, n_heads, name, rides=None):
    t_len = proj.shape[0]
    d = dilation
    rows, lw = _attn_tiles(d, w)
    nb = t_len // rows
    n_pairs = lw // LANES
    per = w // lw
    scale = HEAD_DIM ** -0.5

    def body(q_ref, kp_ref, kc_ref, vp_ref, vc_ref, o_ref, lse_ref):
        first_head = pl.program_id(0) * (2 * n_pairs)
        first_block = pl.program_id(1) == 0
        valid_cur, valid_prev, delta_cur, delta_prev = _attn_masks(first_block)
        masks = _head_lane_masks()
        for p in range(n_pairs):
            cols = pl.ds(p * LANES, LANES)
            slopes = [_alibi_slope(first_head + 2 * p + hh, n_heads) * d for hh in range(2)]
            for r in range(d):
                rs = _residue_rows(r, d)
                q = (q_ref[rs, cols] * scale).astype(BF16)
                kp, kc = kp_ref[rs, cols].astype(BF16), kc_ref[rs, cols].astype(BF16)
                vp, vc = vp_ref[rs, cols].astype(BF16), vc_ref[rs, cols].astype(BF16)
                outs, lses = [], []
                for hh in range(2):
                    qh = jnp.where(masks[hh], q, jnp.zeros_like(q))
                    s_cur = jnp.where(valid_cur, _dot(qh, kc, 1, 1) - slopes[hh] * delta_cur, NEG_INF)
                    s_prev = jnp.where(valid_prev, _dot(qh, kp, 1, 1) - slopes[hh] * delta_prev, NEG_INF)
                    m = jnp.maximum(jnp.max(s_cur, axis=1, keepdims=True), jnp.max(s_prev, axis=1, keepdims=True))
                    p_cur, p_prev = jnp.exp(s_cur - m), jnp.exp(s_prev - m)
                    den = jnp.sum(p_cur, axis=1, keepdims=True) + jnp.sum(p_prev, axis=1, keepdims=True)
                    acc = _dot(p_cur.astype(BF16), vc, 1, 0) + _dot(p_prev.astype(BF16), vp, 1, 0)
                    outs.append(acc / den)
                    lses.append(jnp.broadcast_to(m + jnp.log(den), (ATTN_BLOCK, LANES)))
                o_ref[rs, cols] = jnp.where(masks[0], outs[0], outs[1])
                lse_ref[rs, cols] = jnp.where(masks[0], lses[0], lses[1])

    def spec(which, prev):
        if prev:
            return pl.BlockSpec((rows, lw), lambda b, n: (jnp.maximum(n - 1, 0), which * per + b))
        return pl.BlockSpec((rows, lw), lambda b, n: (n, which * per + b))

    o_spec = pl.BlockSpec((rows, lw), lambda b, n: (n, b))
    return _pallas(
        body, name=name, grid=(per, nb), out_shape=[jax.ShapeDtypeStruct((t_len, w), F32)] * 2,
        in_specs=[spec(0, False), spec(1, True), spec(1, False), spec(2, True), spec(2, False)],
        out_specs=[o_spec, o_spec], operands=[proj] * 5, semantics=("parallel", "parallel"), rides=rides)


def _attn_combine(outs, lses, name, tm=512):
    t_len, w = outs[0].shape
    tm = _pick(t_len, tm, 8)
    nbr = len(outs)

    def body(*refs):
        o_refs, l_refs = refs[:nbr], refs[nbr:2 * nbr]
        out_ref, lse_ref = refs[2 * nbr:]
        ls = [r[...] for r in l_refs]
        m = functools.reduce(jnp.maximum, ls)
        es = [jnp.exp(l - m) for l in ls]
        den = functools.reduce(lambda a, b: a + b, es)
        num = functools.reduce(lambda a, b: a + b, [e * r[...] for e, r in zip(es, o_refs)])
        out_ref[...] = num / den
        lse_ref[...] = m + jnp.log(den)

    spec = pl.BlockSpec((tm, w), lambda i: (i, 0))
    return pl.pallas_call(
        body, name=name, grid=(t_len // tm,),
        out_shape=[jax.ShapeDtypeStruct((t_len, w), F32)] * 2,
        in_specs=[spec] * (2 * nbr), out_specs=[spec, spec],
        compiler_params=_params("parallel"),
    )(*outs, *lses)


def _attn_branch_bwd(proj, w, out, lse, dout, dilation, n_heads, name, acc=None, rides=None):
    t_len = proj.shape[0]
    d = dilation
    rows, lw = _attn_tiles(d, w)
    nb = t_len // rows
    n_pairs = lw // LANES
    per = w // lw
    scale = HEAD_DIM ** -0.5
    n_acc = 0 if acc is None else 3

    def body(*refs):
        q_ref, kp_ref, kc_ref, vp_ref, vc_ref, out_ref, lse_ref, do_ref = refs[:8]
        acc_refs = refs[8:8 + n_acc]
        dq_ref, dk_ref, dv_ref, dk_carry, dv_carry = refs[8 + n_acc:]
        first_head = pl.program_id(0) * (2 * n_pairs)
        n = pl.program_id(1)
        first_block = n == 0
        valid_cur, valid_prev, delta_cur, delta_prev = _attn_masks(first_block)
        masks = _head_lane_masks()

        def plus(value, idx, *where):
            return value + acc_refs[idx][where] if n_acc else value

        @pl.when(first_block)
        def _():
            dk_carry[...] = jnp.zeros_like(dk_carry)
            dv_carry[...] = jnp.zeros_like(dv_carry)

        @pl.when(n < nb)
        def _():
            for p in range(n_pairs):
                cols = pl.ds(p * LANES, LANES)
                slopes = [_alibi_slope(first_head + 2 * p + hh, n_heads) * d for hh in range(2)]
                for r in range(d):
                    rs = _residue_rows(r, d)
                    q = (q_ref[rs, cols] * scale).astype(BF16)
                    kp, kc = kp_ref[rs, cols].astype(BF16), kc_ref[rs, cols].astype(BF16)
                    vp, vc = vp_ref[rs, cols].astype(BF16), vc_ref[rs, cols].astype(BF16)
                    do = do_ref[rs, cols]
                    dob = do.astype(BF16)
                    do_out = do * out_ref[rs, cols]
                    lse_all = lse_ref[rs, cols]
                    dq = jnp.zeros((ATTN_BLOCK, LANES), F32)
                    dk_cur = jnp.zeros((ATTN_BLOCK, LANES), F32)
                    dk_prev = jnp.zeros((ATTN_BLOCK, LANES), F32)
                    dv_cur = jnp.zeros((ATTN_BLOCK, LANES), F32)
                    dv_prev = jnp.zeros((ATTN_BLOCK, LANES), F32)
                    for hh in range(2):
                        msk = masks[hh]
                        qh = jnp.where(msk, q, jnp.zeros_like(q))
                        doh = jnp.where(msk, dob, jnp.zeros_like(dob))
                        delta = jnp.sum(jnp.where(msk, do_out, 0.0), axis=1, keepdims=True)
                        lse_h = jnp.max(jnp.where(msk, lse_all, NEG_INF), axis=1, keepdims=True)
                        s_cur = jnp.where(valid_cur, _dot(qh, kc, 1, 1) - slopes[hh] * delta_cur, NEG_INF)
                        s_prev = jnp.where(valid_prev, _dot(qh, kp, 1, 1) - slopes[hh] * delta_prev, NEG_INF)
                        p_cur, p_prev = jnp.exp(s_cur - lse_h), jnp.exp(s_prev - lse_h)
                        ds_cur = (p_cur * (_dot(doh, vc, 1, 1) - delta)).astype(BF16)
                        ds_prev = (p_prev * (_dot(doh, vp, 1, 1) - delta)).astype(BF16)
                        kch = jnp.where(msk, kc, jnp.zeros_like(kc))
                        kph = jnp.where(msk, kp, jnp.zeros_like(kp))
                        dq = dq + _dot(ds_cur, kch, 1, 0) + _dot(ds_prev, kph, 1, 0)
                        dk_cur = dk_cur + _dot(ds_cur, qh, 0, 0)
                        dk_prev = dk_prev + _dot(ds_prev, qh, 0, 0)
                        dv_cur = dv_cur + _dot(p_cur.astype(BF16), doh, 0, 0)
                        dv_prev = dv_prev + _dot(p_prev.astype(BF16), doh, 0, 0)
                    dq_ref[rs, cols] = plus(dq * scale, 0, rs, cols)
                    dk_ref[rs, cols] = plus(dk_carry[rs, cols] + dk_prev, 1, rs, cols)
                    dv_ref[rs, cols] = plus(dv_carry[rs, cols] + dv_prev, 2, rs, cols)
                    dk_carry[rs, cols] = dk_cur
                    dv_carry[rs, cols] = dv_cur

        @pl.when(n == nb)
        def _():
            dk_ref[...] = plus(dk_carry[...], 1, Ellipsis)
            dv_ref[...] = plus(dv_carry[...], 2, Ellipsis)

    def qkv_spec(which, shift):
        return pl.BlockSpec((rows, lw), lambda b, n: (jnp.clip(n - shift, 0, nb - 1), which * per + b))

    q_like = pl.BlockSpec((rows, lw), lambda b, n: (jnp.minimum(n, nb - 1), b))
    k_like = pl.BlockSpec((rows, lw), lambda b, n: (jnp.maximum(n - 1, 0), b))
    return _pallas(
        body, name=name, grid=(per, nb + 1), out_shape=[jax.ShapeDtypeStruct((t_len, w), F32)] * 3,
        in_specs=[qkv_spec(0, 0), qkv_spec(1, 1), qkv_spec(1, 0), qkv_spec(2, 1), qkv_spec(2, 0),
                  q_like, q_like, q_like] + [q_like, k_like, k_like][:n_acc],
        out_specs=[q_like, k_like, k_like], operands=[proj] * 5 + [out, lse, dout, *(acc or ())],
        scratch_shapes=[pltpu.VMEM((rows, lw), F32), pltpu.VMEM((rows, lw), F32)],
        semantics=("parallel", "arbitrary"), rides=rides)


def _shift_down(u, s):
    if s == 0:
        return u
    row = lax.broadcasted_iota(jnp.int32, u.shape, 0)
    return jnp.where(row >= s, pltpu.roll(u, s, 0), 0.0)


def _shift_up(u, s):
    if s == 0:
        return u
    n = u.shape[0]
    row = lax.broadcasted_iota(jnp.int32, u.shape, 0)
    return jnp.where(row < n - s, pltpu.roll(u, n - s, 0), 0.0)


def _conv_fwd(u, col0, w, b, name):
    t_len, ch = u.shape[0], w.shape[1]
    blk0 = col0 // LANES

    def body(u_ref, w_ref, b_ref, o_ref):
        uv = u_ref[...]
        pre = b_ref[...] + jnp.zeros_like(uv)
        for k in range(SSD_CONV):
            pre = pre + w_ref[k:k + 1, :] * _shift_down(uv, SSD_CONV - 1 - k)
        o_ref[...] = pre * jax.nn.sigmoid(pre)

    return pl.pallas_call(
        body, name=name, grid=(ch // LANES,),
        out_shape=jax.ShapeDtypeStruct((t_len, ch), F32),
        in_specs=[pl.BlockSpec((t_len, LANES), lambda j: (0, blk0 + j)),
                  pl.BlockSpec((SSD_CONV, LANES), lambda j: (0, j)), pl.BlockSpec((1, LANES), lambda j: (0, j))],
        out_specs=pl.BlockSpec((t_len, LANES), lambda j: (0, j)),
        compiler_params=_params("parallel"),
    )(u, w, b)


def _conv_bwd(u, col0, w, b, dact, name):
    t_len, ch = u.shape[0], w.shape[1]
    blk0 = col0 // LANES

    def body(u_ref, w_ref, b_ref, da_ref, du_ref, dw_ref, db_ref):
        uv = u_ref[...]
        shifted = [_shift_down(uv, SSD_CONV - 1 - k) for k in range(SSD_CONV)]
        pre = b_ref[...] + jnp.zeros_like(uv)
        for k in range(SSD_CONV):
            pre = pre + w_ref[k:k + 1, :] * shifted[k]
        sig = jax.nn.sigmoid(pre)
        dpre = da_ref[...] * (sig * (1.0 + pre * (1.0 - sig)))
        du = jnp.zeros_like(uv)
        for k in range(SSD_CONV):
            du = du + w_ref[k:k + 1, :] * _shift_up(dpre, SSD_CONV - 1 - k)
            dw_ref[k:k + 1, :] = jnp.sum(dpre * shifted[k], axis=0, keepdims=True)
        du_ref[...] = du
        db_ref[...] = jnp.sum(dpre, axis=0, keepdims=True)

    col = pl.BlockSpec((t_len, LANES), lambda j: (0, j))
    w_spec = pl.BlockSpec((SSD_CONV, LANES), lambda j: (0, j))
    b_spec = pl.BlockSpec((1, LANES), lambda j: (0, j))
    return pl.pallas_call(
        body, name=name, grid=(ch // LANES,),
        out_shape=[jax.ShapeDtypeStruct((t_len, ch), F32), jax.ShapeDtypeStruct((SSD_CONV, ch), F32),
                   jax.ShapeDtypeStruct((1, ch), F32)],
        in_specs=[pl.BlockSpec((t_len, LANES), lambda j: (0, blk0 + j)), w_spec, b_spec, col],
        out_specs=[col, w_spec, b_spec],
        compiler_params=_params("parallel"),
    )(u, w, b, dact)


def _cumsum_rows(v):
    n = v.shape[0]
    row = lax.broadcasted_iota(jnp.int32, v.shape, 0)
    s = 1
    while s < n:
        v = v + jnp.where(row >= s, pltpu.roll(v, s, 0), 0.0)
        s *= 2
    return v


def _rev_cumsum_rows(v):
    n = v.shape[0]
    row = lax.broadcasted_iota(jnp.int32, v.shape, 0)
    s = 1
    while s < n:
        v = v + jnp.where(row < n - s, pltpu.roll(v, n - s, 0), 0.0)
        s *= 2
    return v


def _head_selector(heads, width):
    j = lax.broadcasted_iota(jnp.int32, (LANES, width), 0)
    lane = lax.broadcasted_iota(jnp.int32, (LANES, width), 1)
    return jnp.where(jnp.logical_and(lane // HEAD_DIM == j, j < heads), 1.0, 0.0).astype(BF16)


class _SsdChunk:
    def __init__(self, dtraw_ref, bias_ref, alog_ref, xs_ref, b_ref, c_ref, heads):
        q = SSD_CHUNK
        width = heads * HEAD_DIM
        lane = lax.broadcasted_iota(jnp.int32, (q, LANES), 1)
        self.head_lanes = lane < heads
        lane1 = lax.broadcasted_iota(jnp.int32, (1, LANES), 1)
        self.a = jnp.where(lane1 < heads, -jnp.exp(alog_ref[...]), 0.0)
        self.dt_arg = dtraw_ref[...] + bias_ref[...]
        self.dt = jnp.where(self.head_lanes, jax.nn.softplus(self.dt_arg), 0.0)
        self.cum = _cumsum_rows(self.dt * self.a)
        self.cum_t = self.cum.T
        last = self.cum[q - 1:q, :]
        self.sel = _head_selector(heads, width)
        self.expand = lambda v: _dot_exact(v, self.sel, 1, 0)
        self.segsum = lambda v: _dot_exact(v, self.sel, 1, 1)
        self.e_exp = self.expand(jnp.exp(self.cum))
        self.d_exp = self.expand(jnp.exp(last - self.cum))
        self.elast_exp = self.e_exp[q - 1:q, :]
        self.dt_exp = self.expand(self.dt)
        self.xs = xs_ref[...]
        self.x = self.xs * self.dt_exp
        self.xb = self.x.astype(BF16)
        self.bb = b_ref[...].astype(BF16)
        self.cb = c_ref[...].astype(BF16)
        self.cbm = _dot(self.cb, self.bb, 1, 1)
        li = lax.broadcasted_iota(jnp.int32, (q, q), 0)
        si = lax.broadcasted_iota(jnp.int32, (q, q), 1)
        self.tri = li >= si
        hl = lax.broadcasted_iota(jnp.int32, (q, LANES), 1)
        self.pair_masks = [hl < HEAD_DIM, hl >= HEAD_DIM]

    def decay(self, j):
        diff = self.cum[:, j:j + 1] - self.cum_t[j:j + 1, :]
        return jnp.exp(jnp.where(self.tri, diff, NEG_INF))


def _ssd_specs(t_len, heads, n_chunks, xbc_cols, rev):
    q, gw = SSD_CHUNK, heads * HEAD_DIM
    ssd_w = SSD_GROUPS * gw
    b_blk = ssd_w // SSD_STATE
    ch = (lambda c: n_chunks - 1 - c) if rev else (lambda c: c)
    return dict(
        dtraw=pl.BlockSpec((None, q, LANES), lambda g, c: (g, ch(c), 0)),
        small=pl.BlockSpec((None, 1, LANES), lambda g, c: (g, 0, 0)),
        dsk=pl.BlockSpec((None, 1, gw), lambda g, c: (g, 0, 0)),
        xs=pl.BlockSpec((q, gw), lambda g, c: (ch(c), g)),
        b=pl.BlockSpec((q, SSD_STATE), lambda g, c: (ch(c), b_blk + g)),
        c=pl.BlockSpec((q, SSD_STATE), lambda g, c: (ch(c), b_blk + SSD_GROUPS + g)),
        z=pl.BlockSpec((q, gw), lambda g, c: (ch(c), 3 * SSD_GROUPS + g)),
        tok=pl.BlockSpec((q, gw), lambda g, c: (ch(c), g)),
        state=pl.BlockSpec((None, SSD_STATE, gw), lambda g, c: (ch(c), 0, g)),
        bc=pl.BlockSpec((q, SSD_STATE), lambda g, c: (ch(c), g)),
    )


def _ssd_fwd(xbc, qkvz, dtraw_g, bias_g, alog_g, dsk_exp, heads, name):
    t_len = xbc.shape[0]
    q, gw = SSD_CHUNK, heads * HEAD_DIM
    n_chunks = t_len // q
    ssd_w = SSD_GROUPS * gw
    sp = _ssd_specs(t_len, heads, n_chunks, xbc.shape[1], rev=False)

    def body(dtraw_ref, bias_ref, alog_ref, dsk_ref, xs_ref, b_ref, c_ref, z_ref,
             yg_ref, ypre_ref, st_ref, s_scr):
        @pl.when(pl.program_id(1) == 0)
        def _():
            s_scr[...] = jnp.zeros_like(s_scr)

        k = _SsdChunk(dtraw_ref, bias_ref, alog_ref, xs_ref, b_ref, c_ref, heads)
        s_prev = s_scr[...]
        st_ref[...] = s_prev
        y_off = k.e_exp * _dot(k.cb, s_prev.astype(BF16), 1, 0)
        parts = []
        for p in range(heads // 2):
            xp = k.xb[:, p * LANES:(p + 1) * LANES]
            acc = jnp.zeros((q, LANES), F32)
            for hh in range(2):
                m = (k.cbm * k.decay(2 * p + hh)).astype(BF16)
                acc = acc + _dot(m, jnp.where(k.pair_masks[hh], xp, jnp.zeros_like(xp)), 1, 0)
            parts.append(acc)
        y = jnp.concatenate(parts, axis=1) + y_off
        xd = (k.x * k.d_exp).astype(BF16)
        s_scr[...] = k.elast_exp * s_prev + _dot(k.bb, xd, 0, 0)
        y_pre = y + dsk_ref[...] * k.xs
        zv = z_ref[...]
        ypre_ref[...] = y_pre
        yg_ref[...] = y_pre * (zv * jax.nn.sigmoid(zv))

    return pl.pallas_call(
        body, name=name, grid=(SSD_GROUPS, n_chunks),
        out_shape=[jax.ShapeDtypeStruct((t_len, ssd_w), F32), jax.ShapeDtypeStruct((t_len, ssd_w), F32),
                   jax.ShapeDtypeStruct((n_chunks, SSD_STATE, ssd_w), F32)],
        in_specs=[sp["dtraw"], sp["small"], sp["small"], sp["dsk"], sp["xs"], sp["b"], sp["c"], sp["z"]],
        out_specs=[sp["tok"], sp["tok"], sp["state"]],
        scratch_shapes=[pltpu.VMEM((SSD_STATE, gw), F32)],
        compiler_params=_params("parallel", "arbitrary"),
    )(dtraw_g, bias_g, alog_g, dsk_exp, xbc, xbc, xbc, qkvz)


def _ssd_bwd(xbc, qkvz, dtraw_g, bias_g, alog_g, dsk_exp, ypre, states, dyg, heads, name):
    t_len = xbc.shape[0]
    q, gw = SSD_CHUNK, heads * HEAD_DIM
    n_chunks = t_len // q
    ssd_w = SSD_GROUPS * gw
    sp = _ssd_specs(t_len, heads, n_chunks, xbc.shape[1], rev=True)

    def body(dtraw_ref, bias_ref, alog_ref, dsk_ref, xs_ref, b_ref, c_ref, z_ref, ypre_ref, st_ref, dyg_ref,
             dxs_ref, db_ref, dc_ref, dz_ref, ddt_ref, small_ref, g_scr):
        first = pl.program_id(1) == 0

        @pl.when(first)
        def _():
            g_scr[...] = jnp.zeros_like(g_scr)

        k = _SsdChunk(dtraw_ref, bias_ref, alog_ref, xs_ref, b_ref, c_ref, heads)
        zv = z_ref[...]
        sig = jax.nn.sigmoid(zv)
        dyg = dyg_ref[...]
        y_pre = ypre_ref[...]
        dy = dyg * (zv * sig)
        dz_ref[...] = dyg * y_pre * (sig * (1.0 + zv * (1.0 - sig)))
        dsk = dsk_ref[...]
        g_next = g_scr[...]
        s_prev = st_ref[...]
        sb = s_prev.astype(BF16)
        xd = k.x * k.d_exp
        xdb = xd.astype(BF16)
        gb = g_next.astype(BF16)
        dx_off = k.d_exp * _dot(k.bb, gb, 1, 0)
        dyb = dy.astype(BF16)
        dcb = jnp.zeros((q, q), F32)
        lane = lax.broadcasted_iota(jnp.int32, (q, LANES), 1)
        row_t = lax.broadcasted_iota(jnp.int32, (LANES, q), 0)
        w_rows = jnp.zeros((q, LANES), F32)
        w_cols_t = jnp.zeros((LANES, q), F32)
        parts = []
        for p in range(heads // 2):
            cols = slice(p * LANES, (p + 1) * LANES)
            dyp, xp = dyb[:, cols], k.xb[:, cols]
            acc = jnp.zeros((q, LANES), F32)
            for hh in range(2):
                j = 2 * p + hh
                lm = k.decay(j)
                m32 = k.cbm * lm
                dym = jnp.where(k.pair_masks[hh], dyp, jnp.zeros_like(dyp))
                acc = acc + _dot(m32.astype(BF16), dym, 0, 0)
                dm = _dot(dym, xp, 1, 1)
                dcb = dcb + dm * lm
                wmat = dm * m32
                w_rows = w_rows + jnp.where(lane == j, jnp.sum(wmat, axis=1, keepdims=True), 0.0)
                w_cols_t = w_cols_t + jnp.where(row_t == j, jnp.sum(wmat, axis=0, keepdims=True), 0.0)
            parts.append(acc)
        dx = jnp.concatenate(parts, axis=1) + dx_off
        dcbb = dcb.astype(BF16)
        edy = (k.e_exp * dy).astype(BF16)
        dc_ref[...] = _dot(dcbb, k.bb, 1, 0) + _dot(edy, sb, 1, 1)
        db_ref[...] = _dot(dcbb, k.cb, 0, 0) + _dot(xdb, gb, 1, 1)
        g_scr[...] = k.elast_exp * g_next + _dot(k.cb, edy, 0, 0)

        y_off = k.e_exp * _dot(k.cb, sb, 1, 0)
        dcum = w_rows - w_cols_t.T + k.segsum(dy * y_off)
        t_term = k.segsum(k.x * dx_off)
        gs = jnp.broadcast_to(jnp.sum(g_next * s_prev, axis=0, keepdims=True), (8, gw))
        carried = k.segsum(gs)[0:1, :] * jnp.exp(k.cum[q - 1:q, :])
        dda = _rev_cumsum_rows(dcum) + (_cumsum_rows(t_term) - t_term) + carried
        ddt = jnp.where(k.head_lanes, dda * k.a + k.segsum(dx * k.xs), 0.0)
        ddtraw = ddt * jax.nn.sigmoid(k.dt_arg)
        ddt_ref[...] = ddtraw
        dxs_ref[...] = dx * k.dt_exp + dsk * dy
        ds = jnp.broadcast_to(jnp.sum(dy * k.xs, axis=0, keepdims=True), (8, gw))
        d_alog = jnp.sum(jnp.where(k.head_lanes, dda * k.dt, 0.0), axis=0, keepdims=True) * k.a
        rows8 = lax.broadcasted_iota(jnp.int32, (8, LANES), 0)
        small = jnp.where(rows8 == 0, d_alog, 0.0)
        small = small + jnp.where(rows8 == 1, jnp.sum(ddtraw, axis=0, keepdims=True), 0.0)
        small = small + jnp.where(rows8 == 2, k.segsum(ds)[0:1, :], 0.0)

        @pl.when(first)
        def _():
            small_ref[...] = small

        @pl.when(jnp.logical_not(first))
        def _():
            small_ref[...] += small

    bc_out = sp["bc"]
    return pl.pallas_call(
        body, name=name, grid=(SSD_GROUPS, n_chunks),
        out_shape=[jax.ShapeDtypeStruct((t_len, ssd_w), F32),
                   jax.ShapeDtypeStruct((t_len, SSD_GROUPS * SSD_STATE), F32),
                   jax.ShapeDtypeStruct((t_len, SSD_GROUPS * SSD_STATE), F32),
                   jax.ShapeDtypeStruct((t_len, ssd_w), F32),
                   jax.ShapeDtypeStruct((SSD_GROUPS, t_len, LANES), F32),
                   jax.ShapeDtypeStruct((SSD_GROUPS, 8, LANES), F32)],
        in_specs=[sp["dtraw"], sp["small"], sp["small"], sp["dsk"], sp["xs"], sp["b"], sp["c"], sp["z"],
                  sp["tok"], sp["state"], sp["tok"]],
        out_specs=[sp["tok"], bc_out, bc_out, sp["tok"], sp["dtraw"],
                   pl.BlockSpec((None, 8, LANES), lambda g, c: (g, 0, 0))],
        scratch_shapes=[pltpu.VMEM((SSD_STATE, gw), F32)],
        compiler_params=_params("parallel", "arbitrary"),
    )(dtraw_g, bias_g, alog_g, dsk_exp, xbc, xbc, xbc, qkvz, ypre, states, dyg)


def _adamw(w, g, m, v, name):
    rows, lanes = w.shape
    tr = _row_tile(rows, lanes, 4, 14)
    c1 = 1.0 / (1.0 - ADAM_B1 ** ADAM_STEP)
    c2 = 1.0 / (1.0 - ADAM_B2 ** ADAM_STEP)

    def body(w_ref, g_ref, m_ref, v_ref, d_ref, nm_ref, nv_ref):
        gv = g_ref[...]
        nm = ADAM_B1 * m_ref[...] + (1.0 - ADAM_B1) * gv
        nv = ADAM_B2 * v_ref[...] + (1.0 - ADAM_B2) * (gv * gv)
        nm_ref[...] = nm
        nv_ref[...] = nv
        d_ref[...] = -ADAM_LR * ((nm * c1) / (jnp.sqrt(nv * c2) + ADAM_EPS) + ADAM_WD * w_ref[...])

    spec = pl.BlockSpec((tr, lanes), lambda i: (i, 0))
    return pl.pallas_call(
        body, name=name, grid=(rows // tr,),
        out_shape=[jax.ShapeDtypeStruct((rows, lanes), F32)] * 3,
        in_specs=[spec] * 4, out_specs=[spec] * 3,
        compiler_params=_params("parallel"),
    )(w, g, m, v)


def _pad_lanes(a, width=LANES):
    return jnp.pad(a, ((0, 0), (0, width - a.shape[1])))


def _group_pad(v, heads):
    return _pad_lanes(v.reshape(SSD_GROUPS, heads))[:, None, :]


def _layer_fwd(x0, p, wt, dims, tag, rides):
    w_attn, heads_g, n_heads, conv_ch = dims["w_attn"], dims["heads_g"], dims["n_heads"], dims["conv_ch"]
    h1 = _rmsnorm_fwd([x0], [[x0.shape[1]]], p["ln1_g"], f"ln1_fwd{tag}")
    proj = _mm(h1, wt("w_in"), name=f"in_proj{tag}", tn=640, rides=rides)

    outs, lses = [], []
    for d in BRANCH_DILATIONS:
        o, l = _attn_branch_fwd(proj, w_attn, d, n_heads, f"attn_fwd_d{d}{tag}", rides)
        outs.append(o)
        lses.append(l)
    attn, lse = _attn_combine(outs, lses, f"attn_combine{tag}")

    xbc = _conv_fwd(proj, 4 * w_attn, p["conv_w"], p["conv_b"], f"conv_fwd{tag}")
    dt_col = 4 * w_attn + conv_ch
    dtraw_g = jnp.stack([_pad_lanes(proj[:, dt_col + g * heads_g:dt_col + (g + 1) * heads_g])
                         for g in range(SSD_GROUPS)])
    bias_g, alog_g = _group_pad(p["dt_bias"], heads_g), _group_pad(p["a_log"], heads_g)
    dsk_exp = jnp.repeat(p["d_skip"], HEAD_DIM).reshape(SSD_GROUPS, 1, heads_g * HEAD_DIM)
    yg, ypre, states = _ssd_fwd(xbc, proj, dtraw_g, bias_g, alog_g, dsk_exp, heads_g, f"ssd_fwd{tag}")

    gw = heads_g * HEAD_DIM
    mix_g = jnp.concatenate([p["attn_norm_g"], p["ssd_norm_g"]])[None, :]
    mix = _rmsnorm_fwd([attn, yg], [[w_attn], [gw] * SSD_GROUPS], mix_g, f"mix_norm_fwd{tag}")
    x1 = _mm(mix, wt("w_out"), name=f"out_proj{tag}", residual=x0, rides=rides)
    h2 = _rmsnorm_fwd([x1], [[x1.shape[1]]], p["ln2_g"], f"ln2_fwd{tag}")
    u = _mm(h2, wt("w_mlp_in"), name=f"mlp_in{tag}", out_dtype=BF16, tn=1024, rides=rides)
    x2 = _mm(u, wt("w_mlp_out"), name=f"mlp_out{tag}", a_act="relu2", residual=x1, tm=512, rides=rides)
    saved = dict(x0=x0, h1=h1, proj=proj, attn=attn, lse=lse, xbc=xbc, dtraw_g=dtraw_g,
                 bias_g=bias_g, alog_g=alog_g, dsk_exp=dsk_exp, yg=yg, ypre=ypre, states=states, mix=mix,
                 mix_g=mix_g, x1=x1, h2=h2, u=u)
    return x2, saved


def _pair_sums(ex, host, items):
    swapped = ex["rides"].done[("swap", host)]
    core = lax.axis_index("c")
    for i, (n, l) in enumerate(items):
        buf = ex["bufs"][(n, l)]
        mine = lax.dynamic_index_in_dim(buf.reshape(N_DEV // 2, 2, *buf.shape[1:]), core, axis=1, keepdims=False)
        ex["pair"][(n, l)] = _sum_leading([mine, swapped[i]], f"pair_sum_{n}_l{l}", out_dtype=BF16)


def _layer_bwd(dx2, dx2_b, p, wt, s, dims, l, ex, copy_dx0):
    w_attn, heads_g, n_heads, conv_ch = dims["w_attn"], dims["heads_g"], dims["n_heads"], dims["conv_ch"]
    t_len, d_model = dx2.shape
    gw = heads_g * HEAD_DIM
    h_ssd = heads_g * SSD_GROUPS
    tag, rides, bufs = f"_l{l}", ex["rides"], ex["bufs"]
    du = _mm(dx2_b, wt("w_mlp_out"), name=f"mlp_out_dx{tag}", tb=True, gate=s["u"], out_dtype=BF16, tn=1024,
             rides=rides)
    d_wmo = _mm(s["u"], dx2_b, name=f"mlp_out_dw{tag}", ta=True, a_act="relu2", tm=512, tn=1024, out_dtype=BF16)
    bufs[("w_mlp_out", l)] = d_wmo.reshape(N_DEV, -1, d_model)
    bufs[("w_mlp_in", l)] = _mm(s["h2"], du, name=f"mlp_in_dw{tag}", ta=True, tm=512, tn=1024, out_dtype=BF16,
                                out_chunk=du.shape[1] // N_DEV)
    dh2 = _mm(du, wt("w_mlp_in"), name=f"mlp_in_dx{tag}", tb=True, tm=512, rides=rides)
    _pair_sums(ex, f"mlp_in_dx{tag}", [("w_mlp_out", l), ("w_mlp_in", l)])
    (dx1,), d_ln2, (dx1_b,) = _rmsnorm_bwd([s["x1"]], [[d_model]], p["ln2_g"], dh2, [dx2], f"ln2_bwd{tag}",
                                           bf16_copy=True)
    dmix = _mm(dx1_b, wt("w_out"), name=f"out_proj_dx{tag}", tb=True)
    d_wo = _mm(s["mix"], dx1_b, name=f"out_proj_dw{tag}", ta=True, tm=512, tn=1024, out_dtype=BF16)
    bufs[("w_out", l)] = d_wo.reshape(N_DEV, -1, d_model)
    (dattn, dyg), d_mix_g, _ = _rmsnorm_bwd([s["attn"], s["yg"]], [[w_attn], [gw] * SSD_GROUPS], s["mix_g"], dmix,
                                           [None, None], f"mix_norm_bwd{tag}")
    dxs, db, dc, dz, ddtraw_g, ssd_small = _ssd_bwd(
        s["xbc"], s["proj"], s["dtraw_g"], s["bias_g"], s["alog_g"], s["dsk_exp"], s["ypre"], s["states"], dyg,
        heads_g, f"ssd_bwd{tag}")
    dxbc = jnp.concatenate([dxs, db, dc], axis=1)
    dxbc_raw, d_conv_w, d_conv_b = _conv_bwd(s["proj"], 4 * w_attn, p["conv_w"], p["conv_b"], dxbc, f"conv_bwd{tag}")
    acc = None
    for d in BRANCH_DILATIONS:
        acc = _attn_branch_bwd(s["proj"], w_attn, s["attn"], s["lse"], dattn, d, n_heads, f"attn_bwd_d{d}{tag}", acc,
                               rides)
    w_in = wt("w_in")
    in_proj = 4 * w_attn + conv_ch + h_ssd
    pad = jnp.zeros((t_len, w_in.shape[1] - in_proj), F32)
    dproj = jnp.concatenate([*acc, dz, dxbc_raw] + [ddtraw_g[g, :, :heads_g] for g in range(SSD_GROUPS)] + [pad],
                            axis=1).astype(BF16)
    d_win = _mm(s["h1"], dproj, name=f"in_proj_dw{tag}", ta=True, tm=512, tn=1152)
    bufs[("w_in", l)] = d_win[:, :in_proj].reshape(d_model, N_DEV, -1).transpose(1, 0, 2).astype(BF16)
    dh1 = _mm(dproj, w_in, name=f"in_proj_dx{tag}", tb=True, tm=512, rides=rides)
    _pair_sums(ex, f"in_proj_dx{tag}", [("w_out", l), ("w_in", l)])
    (dx0,), d_ln1, dx0_b = _rmsnorm_bwd([s["x0"]], [[d_model]], p["ln1_g"], dh1, [dx1], f"ln1_bwd{tag}",
                                        bf16_copy=copy_dx0)

    small = ssd_small[:, :, :heads_g]
    grads = dict(
        ln1_g=d_ln1[0], conv_w=d_conv_w, conv_b=d_conv_b[0],
        a_log=small[:, 0].reshape(h_ssd), dt_bias=small[:, 1].reshape(h_ssd), d_skip=small[:, 2].reshape(h_ssd),
        attn_norm_g=d_mix_g[0, :w_attn], ssd_norm_g=d_mix_g[0, w_attn:], ln2_g=d_ln2[0])
    return dx0, (dx0_b[0] if copy_dx0 else None), grads


_SMALL = ["ln1_g", "conv_w", "conv_b", "dt_bias", "a_log", "d_skip", "attn_norm_g", "ssd_norm_g", "ln2_g"]
_WEIGHTS = ["ln1_g", "w_in", "conv_w", "conv_b", "dt_bias", "a_log", "d_skip", "attn_norm_g", "ssd_norm_g",
            "w_out", "ln2_g", "w_mlp_in", "w_mlp_out", "final_norm_g"]


def _to_rows(a):
    flat = a.reshape(-1)
    rows = -(-flat.shape[0] // LANES)
    rows = -(-rows // 8) * 8
    return jnp.pad(flat, (0, rows * LANES - flat.shape[0])).reshape(rows, LANES)


def kernel(x, ln1_g, w_in, conv_w, conv_b, dt_bias, a_log, d_skip, attn_norm_g, ssd_norm_g, w_out, ln2_g, w_mlp_in, w_mlp_out, final_norm_g, loss_target, m_ln1_g, m_w_in, m_conv_w, m_conv_b, m_dt_bias, m_a_log, m_d_skip, m_attn_norm_g, m_ssd_norm_g, m_w_out, m_ln2_g, m_w_mlp_in, m_w_mlp_out, m_final_norm_g, v_ln1_g, v_w_in, v_conv_w, v_conv_b, v_dt_bias, v_a_log, v_d_skip, v_attn_norm_g, v_ssd_norm_g, v_w_out, v_ln2_g, v_w_mlp_in, v_w_mlp_out, v_final_norm_g):
    w = dict(ln1_g=ln1_g, w_in=w_in, conv_w=conv_w, conv_b=conv_b, dt_bias=dt_bias, a_log=a_log, d_skip=d_skip,
             attn_norm_g=attn_norm_g, ssd_norm_g=ssd_norm_g, w_out=w_out, ln2_g=ln2_g, w_mlp_in=w_mlp_in,
             w_mlp_out=w_mlp_out, final_norm_g=final_norm_g)
    mom = dict(ln1_g=m_ln1_g, w_in=m_w_in, conv_w=m_conv_w, conv_b=m_conv_b, dt_bias=m_dt_bias, a_log=m_a_log,
               d_skip=m_d_skip, attn_norm_g=m_attn_norm_g, ssd_norm_g=m_ssd_norm_g, w_out=m_w_out, ln2_g=m_ln2_g,
               w_mlp_in=m_w_mlp_in, w_mlp_out=m_w_mlp_out, final_norm_g=m_final_norm_g)
    var = dict(ln1_g=v_ln1_g, w_in=v_w_in, conv_w=v_conv_w, conv_b=v_conv_b, dt_bias=v_dt_bias, a_log=v_a_log,
               d_skip=v_d_skip, attn_norm_g=v_attn_norm_g, ssd_norm_g=v_ssd_norm_g, w_out=v_w_out, ln2_g=v_ln2_g,
               w_mlp_in=v_w_mlp_in, w_mlp_out=v_w_mlp_out, final_norm_g=v_final_norm_g)

    depth, d_model = ln1_g.shape
    t_len = x.shape[1]
    w_attn = attn_norm_g.shape[1]
    h_ssd = dt_bias.shape[1]
    conv_ch = conv_b.shape[1]
    in_proj = w_in.shape[2] * N_DEV
    assert ssd_norm_g.shape[1] == w_attn and in_proj == 4 * w_attn + conv_ch + h_ssd
    assert t_len % (BRANCH_DILATIONS[-1] * ATTN_BLOCK) == 0 and h_ssd % (2 * SSD_GROUPS) == 0
    dims = dict(w_attn=w_attn, heads_g=h_ssd // SSD_GROUPS, n_heads=w_attn // HEAD_DIM, conv_ch=conv_ch)
    names = ["w_in", "w_out", "w_mlp_in", "w_mlp_out"]

    rides = _Rides()
    ex = dict(rides=rides, bufs={}, pair={})
    where, sent = {}, {}

    def shard(n, l):
        return w[n][l].astype(BF16)

    def plan_spread(host, items):
        rides.put(host, ("spread", host), lambda: _GatherSpread([shard(n, l) for n, l in items]))
        return host, items

    def plan_pass(host, spreads):
        rides.put(host, ("pass", host),
                  lambda: _GatherPass([b for h, _ in spreads for b in rides.done[("spread", h)]]))
        for i, item in enumerate([it for _, items in spreads for it in items]):
            where[item] = (("pass", host), i)

    def plan_swap(host, items):
        rides.put(host, ("swap", host), lambda: _SiblingSwap([ex["bufs"][it] for it in items]))

    def plan_send(host, items):
        rides.put(host, ("send", host), lambda: _ChipSend([ex["pair"][it] for it in items]))
        for i, item in enumerate(items):
            sent[item] = (("send", host), i)

    for l in range(depth):
        t, nxt = f"_l{l}", f"_l{l + 1}"
        if l == 0:
            plan_pass(f"attn_fwd_d16{t}", [plan_spread(f"in_proj{t}", [("w_out", 0)]),
                                          plan_spread(f"attn_fwd_d1{t}", [("w_mlp_in", 0)]),
                                          plan_spread(f"attn_fwd_d4{t}", [("w_mlp_out", 0)])])
        else:
            plan_pass(f"attn_fwd_d1{t}", [plan_spread(f"in_proj{t}", [("w_mlp_out", l)])])
        if l + 1 < depth:
            plan_pass(f"pass_weights{nxt}", [plan_spread(f"out_proj{t}", [("w_out", l + 1)]),
                                             plan_spread(f"mlp_in{t}", [("w_in", l + 1)]),
                                             plan_spread(f"mlp_out{t}", [("w_mlp_in", l + 1)])])
        plan_swap(f"mlp_in_dx{t}", [("w_mlp_out", l), ("w_mlp_in", l)])
        plan_send(f"attn_bwd_d1{t}", [("w_mlp_out", l)])
        plan_send(f"attn_bwd_d4{t}", [("w_mlp_in", l)])
        plan_swap(f"in_proj_dx{t}", [("w_out", l), ("w_in", l)])
        plan_send(f"mlp_out_dx_l{l - 1}" if l > 0 else "send_last_grads", [("w_out", l), ("w_in", l)])

    g_in0, g_cw = _all_gather([shard("w_in", 0), conv_w], "gather_first")
    full_cw = g_cw.transpose(1, 2, 0, 3).reshape(depth, SSD_CONV, conv_ch)
    proj_cols = -(-in_proj // LANES) * LANES
    full = {}

    def weight(n, l):
        if (n, l) not in full:
            if (n, l) == ("w_in", 0):
                g = g_in0
            else:
                key, i = where[(n, l)]
                g = rides.done[key][i]
            if n == "w_in":
                g = _pad_lanes(g.transpose(1, 0, 2).reshape(d_model, in_proj), proj_cols)
            elif n == "w_mlp_in":
                g = g.transpose(1, 0, 2).reshape(d_model, -1)
            else:
                g = g.reshape(-1, d_model)
            full[(n, l)] = g
        return full[(n, l)]

    layers = [dict(ln1_g=ln1_g[l][None, :], ln2_g=ln2_g[l][None, :], conv_w=full_cw[l], conv_b=conv_b[l][None, :],
                   dt_bias=dt_bias[l], a_log=a_log[l], d_skip=d_skip[l], attn_norm_g=attn_norm_g[l],
                   ssd_norm_g=ssd_norm_g[l]) for l in range(depth)]

    h = x[0]
    saved = []
    for l in range(depth):
        h, s = _layer_fwd(h, layers[l], functools.partial(lambda n, l: weight(n, l), l=l), dims, f"_l{l}", rides)
        saved.append(s)
        if l + 1 < depth:
            _alone(rides, f"pass_weights_l{l + 1}")
    dh, d_final_g, loss_part, dh_b = _loss_head(h, final_norm_g[None, :], loss_target[0], "loss_head")

    grads = [None] * depth
    for l in reversed(range(depth)):
        dh, dh_b, grads[l] = _layer_bwd(dh, dh_b, layers[l], functools.partial(lambda n, l: weight(n, l), l=l),
                                        saved[l], dims, l, ex, copy_dx0=l > 0)
    grad_x = dh[None]
    _alone(rides, "send_last_grads")

    my_chip = 2 * lax.axis_index("x") + lax.axis_index("y")
    gsum = {}
    for n in names:
        per_layer = []
        for l in range(depth):
            key, i = sent[(n, l)]
            own = lax.dynamic_index_in_dim(ex["pair"][(n, l)], my_chip, axis=0, keepdims=False)
            per_layer.append(_sum_leading([rides.done[key][i]], f"sum_{n}_l{l}", first=own))
        gsum[n] = jnp.stack(per_layer)

    small_parts = [jnp.stack([grads[l][n] for l in range(depth)]).reshape(-1) for n in _SMALL]
    small_parts += [d_final_g.reshape(-1), loss_part[0, :1]]
    sizes = [int(a.shape[0]) for a in small_parts]
    packed = _to_rows(jnp.concatenate(small_parts))
    (gathered,) = _all_gather([packed], "gather_small_grads")
    total = _sum_leading([gathered], "sum_small_grads").reshape(-1)
    offs = np.cumsum([0] + sizes)
    pieces = [total[offs[i]:offs[i + 1]] for i in range(len(sizes))]
    for n, piece in zip(_SMALL, pieces):
        shape = (depth, SSD_CONV, conv_ch) if n == "conv_w" else w[n].shape
        gsum[n] = piece.reshape(shape)
    gsum["final_norm_g"] = pieces[len(_SMALL)]
    loss = pieces[len(_SMALL) + 1][0]
    my_id = 4 * lax.axis_index("x") + 2 * lax.axis_index("y") + lax.axis_index("c")
    cw = conv_w.shape[2]
    gsum["conv_w"] = lax.dynamic_slice_in_dim(gsum["conv_w"], my_id * cw, cw, axis=2)

    delta, new_m, new_v = {}, {}, {}
    for n in names:
        outs = _adamw(*(a.reshape(-1, w[n].shape[-1]) for a in (w[n], gsum[n], mom[n], var[n])), f"adamw_{n}")
        delta[n], new_m[n], new_v[n] = (o.reshape(w[n].shape) for o in outs)
    small_names = [n for n in _WEIGHTS if n not in names]
    sm_sizes = [int(np.prod(w[n].shape)) for n in small_names]
    pack = lambda d: _to_rows(jnp.concatenate([d[n].reshape(-1) for n in small_names]))
    outs = _adamw(pack(w), pack(gsum), pack(mom), pack(var), "adamw_small")
    sm_offs = np.cumsum([0] + sm_sizes)
    for res, o in zip((delta, new_m, new_v), outs):
        flat = o.reshape(-1)
        for i, n in enumerate(small_names):
            res[n] = flat[sm_offs[i]:sm_offs[i + 1]].reshape(w[n].shape)

    return (loss, grad_x, *[gsum[n] for n in _WEIGHTS], *[delta[n] for n in _WEIGHTS],
            *[new_m[n] for n in _WEIGHTS], *[new_v[n] for n in _WEIGHTS])
```

```python
import functools
import math

import numpy as np
import jax
import jax.numpy as jnp
from jax import lax
from jax.experimental import pallas as pl
from jax.experimental.pallas import tpu as pltpu

F32 = jnp.float32
BF16 = jnp.bfloat16

N_DEV = 8
LANES = 128
HEAD_DIM = 64
ATTN_BLOCK = 128
BRANCH_DILATIONS = (1, 4, 16)
SSD_GROUPS = 2
SSD_STATE = 128
SSD_CHUNK = 128
SSD_CONV = 4
NORM_EPS = 1e-5
ADAM_LR, ADAM_B1, ADAM_B2, ADAM_EPS, ADAM_WD, ADAM_STEP = 0.001, 0.9, 0.999, 1e-08, 0.01, 10
VMEM_LIMIT_BYTES = 56 * 1024 * 1024
MESH = pl.DeviceIdType.MESH
NEG_INF = float("-inf")


def _params(*sem):
    return pltpu.CompilerParams(dimension_semantics=tuple(sem), vmem_limit_bytes=VMEM_LIMIT_BYTES)


def _pick(n, target, mult):
    best = None
    for t in range(mult, min(n, target) + 1, mult):
        if n % t == 0:
            best = t
    assert best is not None, (n, target, mult)
    return best


def _dot(a, b, ca, cb):
    return lax.dot_general(a, b, (((ca,), (cb,)), ((), ())), preferred_element_type=F32)


def _split3(v):
    hi = v.astype(BF16)
    r = v - hi.astype(F32)
    mid = r.astype(BF16)
    lo = (r - mid.astype(F32)).astype(BF16)
    return hi, mid, lo


def _dot_exact(v, sel, ca, cb):
    hi, mid, lo = _split3(v)
    return _dot(hi, sel, ca, cb) + _dot(mid, sel, ca, cb) + _dot(lo, sel, ca, cb)


_HBM = pl.BlockSpec(memory_space=pltpu.HBM)


def _all_gather(xs, name):
    n = len(xs)

    def body(*refs):
        x_refs, o_refs = refs[:n], refs[n:2 * n]
        send_sems, recv_sems, local_sems = refs[2 * n:]
        x, y, c = lax.axis_index("x"), lax.axis_index("y"), lax.axis_index("c")
        me, sibling = (x, y, c), (x, y, 1 - c)
        chips = [(1 - x, y), (x, 1 - y), (1 - x, 1 - y)]

        def copy(t, k, block, to, src=None):
            bx, by, bc = block
            dst = o_refs[t].at[4 * bx + 2 * by + bc]
            return pltpu.make_async_remote_copy(
                src_ref=dst if src is None else src, dst_ref=dst,
                send_sem=send_sems.at[t, k], recv_sem=recv_sems.at[t, k],
                device_id=to, device_id_type=MESH)

        mine = [pltpu.make_async_copy(x_refs[t], o_refs[t].at[4 * x + 2 * y + c], local_sems.at[t])
                for t in range(n)]
        first, passed = [], []
        for t in range(n):
            mine[t].start()
            cps = [copy(t, 0, me, sibling, src=x_refs[t])]
            cps += [copy(t, 1 + j, me, (*chip, c), src=x_refs[t]) for j, chip in enumerate(chips)]
            for cp in cps:
                cp.start()
            first += cps
        for t in range(n):
            for j, chip in enumerate(chips):
                copy(t, 1 + j, (*chip, c), me).wait_recv()
                fwd = copy(t, 4 + j, (*chip, c), sibling)
                fwd.start()
                passed.append(fwd)
        for t in range(n):
            copy(t, 0, sibling, me).wait_recv()
            for j, chip in enumerate(chips):
                copy(t, 4 + j, (*chip, 1 - c), me).wait_recv()
        for cp in first + passed:
            cp.wait_send()
        for t in range(n):
            mine[t].wait()

    return pl.pallas_call(
        body, name=name,
        out_shape=[jax.ShapeDtypeStruct((N_DEV,) + a.shape, a.dtype) for a in xs],
        in_specs=[_HBM] * n, out_specs=[_HBM] * n,
        scratch_shapes=[pltpu.SemaphoreType.DMA((n, 7)), pltpu.SemaphoreType.DMA((n, 7)),
                        pltpu.SemaphoreType.DMA((n,))],
    )(*xs)


def _place():
    x, y, c = lax.axis_index("x"), lax.axis_index("y"), lax.axis_index("c")
    return x, y, c, 4 * x + 2 * y + c, (x, y, 1 - c), [(1 - x, y), (x, 1 - y), (1 - x, 1 - y)]


def _remote(src, dst, send_sem, recv_sem, to):
    return pltpu.make_async_remote_copy(src_ref=src, dst_ref=dst, send_sem=send_sem, recv_sem=recv_sem,
                                        device_id=to, device_id_type=MESH)


class _Riding:
    aliases = {}

    def copies(self, ins, outs, sems):
        raise NotImplementedError

    def start(self, ins, outs, sems):
        local, out, _ = self.copies(ins, outs, sems)
        for cp in local + out:
            cp.start()

    def wait(self, ins, outs, sems):
        local, out, landing = self.copies(ins, outs, sems)
        for cp in landing:
            cp.wait_recv()
        for cp in out:
            cp.wait_send()
        for cp in local:
            cp.wait()


class _GatherSpread(_Riding):
    def __init__(self, xs):
        n = len(xs)
        self.ins = list(xs)
        self.out_shapes = [jax.ShapeDtypeStruct((N_DEV,) + a.shape, a.dtype) for a in xs]
        self.sem_shapes = [pltpu.SemaphoreType.DMA((n, 4)), pltpu.SemaphoreType.DMA((n, 4)),
                           pltpu.SemaphoreType.DMA((n,))]

    def copies(self, ins, outs, sems):
        send, recv, local_sems = sems
        _, _, c, me, sibling, chips = _place()
        targets = [sibling] + [(*chip, c) for chip in chips]
        local, out, landing = [], [], []
        for t in range(len(ins)):
            local.append(pltpu.make_async_copy(ins[t], outs[t].at[me], local_sems.at[t]))
            for k, to in enumerate(targets):
                out.append(_remote(ins[t], outs[t].at[me], send.at[t, k], recv.at[t, k], to))
                theirs = outs[t].at[4 * to[0] + 2 * to[1] + to[2]]
                landing.append(_remote(ins[t], theirs, send.at[t, k], recv.at[t, k], to))
        return local, out, landing


class _GatherPass(_Riding):
    def __init__(self, bufs):
        n = len(bufs)
        self.ins = list(bufs)
        self.out_shapes = [jax.ShapeDtypeStruct(b.shape, b.dtype) for b in bufs]
        self.aliases = {t: t for t in range(n)}
        self.sem_shapes = [pltpu.SemaphoreType.DMA((n, 3)), pltpu.SemaphoreType.DMA((n, 3))]

    def copies(self, ins, outs, sems):
        send, recv = sems
        _, _, c, _, sibling, chips = _place()
        out, landing = [], []
        for t in range(len(outs)):
            for j, (px, py) in enumerate(chips):
                got = outs[t].at[4 * px + 2 * py + c]
                out.append(_remote(got, got, send.at[t, j], recv.at[t, j], sibling))
                landing.append(_remote(got, outs[t].at[4 * px + 2 * py + 1 - c], send.at[t, j], recv.at[t, j], sibling))
        return [], out, landing


class _SiblingSwap(_Riding):
    def __init__(self, xs):
        n = len(xs)
        self.ins = list(xs)
        self.out_shapes = [jax.ShapeDtypeStruct((N_DEV // 2,) + a.shape[1:], a.dtype) for a in xs]
        self.sem_shapes = [pltpu.SemaphoreType.DMA((n, 4)), pltpu.SemaphoreType.DMA((n, 4))]

    def copies(self, ins, outs, sems):
        send, recv = sems
        _, _, c, _, sibling, _ = _place()
        out = [_remote(ins[t].at[2 * q + 1 - c], outs[t].at[q], send.at[t, q], recv.at[t, q], sibling)
               for t in range(len(ins)) for q in range(N_DEV // 2)]
        return [], out, out


class _ChipSend(_Riding):
    def __init__(self, ps):
        n = len(ps)
        self.ins = list(ps)
        self.out_shapes = [jax.ShapeDtypeStruct((3,) + a.shape[1:], a.dtype) for a in ps]
        self.sem_shapes = [pltpu.SemaphoreType.DMA((n, 3)), pltpu.SemaphoreType.DMA((n, 3))]

    def copies(self, ins, outs, sems):
        send, recv = sems
        _, _, c, _, _, chips = _place()
        out = [_remote(ins[t].at[2 * px + py], outs[t].at[j], send.at[t, j], recv.at[t, j], (px, py, c))
               for t in range(len(ins)) for j, (px, py) in enumerate(chips)]
        return [], out, out


class _Rides:
    def __init__(self):
        self.plan, self.done = {}, {}

    def put(self, host, key, make):
        assert host not in self.plan, host
        self.plan[host] = (key, make)

    def board(self, host):
        return self.plan[host][1]() if host in self.plan else None

    def land(self, host, results):
        self.done[self.plan[host][0]] = list(results)


def _pallas(body, *, name, grid, out_shape, in_specs, out_specs, operands, semantics, scratch_shapes=(), rides=None):
    comm = rides.board(name) if rides is not None else None
    if comm is None:
        return pl.pallas_call(
            body, name=name, grid=grid, out_shape=list(out_shape), in_specs=list(in_specs),
            out_specs=list(out_specs), scratch_shapes=list(scratch_shapes), compiler_params=_params(*semantics),
        )(*operands)
    n_in, n_out, n_scr = len(in_specs), len(out_shape), len(scratch_shapes)
    n_ci, n_co = len(comm.ins), len(comm.out_shapes)

    def hosted(*refs):
        cuts = np.cumsum([0, n_in, n_ci, n_out, n_co, n_scr])
        ins, c_ins, outs, c_outs, scr = (refs[cuts[i]:cuts[i + 1]] for i in range(5))
        sems = refs[cuts[5]:]
        ids = [pl.program_id(a) for a in range(len(grid))]
        first = functools.reduce(jnp.logical_and, [i == 0 for i in ids])
        last = functools.reduce(jnp.logical_and, [i == g - 1 for i, g in zip(ids, grid)])

        @pl.when(first)
        def _():
            comm.start(c_ins, c_outs, sems)

        body(*ins, *outs, *scr)

        @pl.when(last)
        def _():
            comm.wait(c_ins, c_outs, sems)

    results = pl.pallas_call(
        hosted, name=name, grid=grid, out_shape=list(out_shape) + comm.out_shapes,
        in_specs=list(in_specs) + [_HBM] * n_ci, out_specs=list(out_specs) + [_HBM] * n_co,
        scratch_shapes=list(scratch_shapes) + comm.sem_shapes,
        input_output_aliases={n_in + i: n_out + j for i, j in comm.aliases.items()},
        compiler_params=_params(*["arbitrary"] * len(grid)),
    )(*operands, *comm.ins)
    rides.land(name, results[n_out:])
    return results[:n_out]


def _alone(rides, name):
    comm = rides.board(name)

    def body(*refs):
        n_ci, n_co = len(comm.ins), len(comm.out_shapes)
        ins, outs, sems = refs[:n_ci], refs[n_ci:n_ci + n_co], refs[n_ci + n_co:]
        comm.start(ins, outs, sems)
        comm.wait(ins, outs, sems)

    results = pl.pallas_call(
        body, name=name, out_shape=comm.out_shapes, in_specs=[_HBM] * len(comm.ins),
        out_specs=[_HBM] * len(comm.out_shapes), scratch_shapes=comm.sem_shapes,
        input_output_aliases=dict(comm.aliases),
    )(*comm.ins)
    rides.land(name, results)


def _row_tile(rows, cols, itemsize, copies, budget=24 * 1024 * 1024):
    padded = -(-cols // LANES) * LANES
    mult = 8 * (4 // itemsize)
    if rows % mult:
        return rows
    return _pick(rows, max(mult, budget // (copies * padded * itemsize)), mult)


def _sum_leading(xs, name, out_dtype=F32, first=None):
    n_src, rows, cols = xs[0].shape
    pairwise = len(xs) > 1
    blocks = (len(xs) + 1) if pairwise else (n_src + 3)
    tr = _row_tile(rows, cols, 4, 2 * blocks)

    def body(*refs):
        o_ref = refs[-1]
        if pairwise:
            acc = refs[0][...].astype(F32)
            for r in refs[1:-1]:
                acc = acc + r[...].astype(F32)
        else:
            terms = [refs[0][s] for s in range(n_src)]
            if first is not None:
                terms.insert(0, refs[1][...])
            acc = terms[0].astype(F32)
            for term in terms[1:]:
                acc = acc + term.astype(F32)
        o_ref[...] = acc.astype(out_dtype)

    if pairwise:
        spec = pl.BlockSpec((None, tr, cols), lambda s, i: (s, i, 0))
        return pl.pallas_call(
            body, name=name, grid=(n_src, rows // tr), out_shape=jax.ShapeDtypeStruct((n_src, rows, cols), out_dtype),
            in_specs=[spec] * len(xs), out_specs=spec, compiler_params=_params("parallel", "parallel"),
        )(*xs)
    flat = pl.BlockSpec((tr, cols), lambda i: (i, 0))
    return pl.pallas_call(
        body, name=name, grid=(rows // tr,), out_shape=jax.ShapeDtypeStruct((rows, cols), out_dtype),
        in_specs=[pl.BlockSpec((n_src, tr, cols), lambda i: (0, i, 0))] + ([flat] if first is not None else []),
        out_specs=flat, compiler_params=_params("parallel"),
    )(xs[0], *([first] if first is not None else []))


def _mm(a, b, *, name, ta=False, tb=False, tm=1024, tn=512, out_dtype=F32, a_act=None,
        residual=None, gate=None, out_chunk=None, rides=None):
    k_dim, m = (a.shape if ta else a.shape[::-1])
    n, kb = (b.shape if tb else b.shape[::-1])
    assert kb == k_dim, (a.shape, b.shape, ta, tb)
    tm, tn = _pick(m, tm, 128), _pick(out_chunk or n, tn, 128)
    ca, cb = (0 if ta else 1), (1 if tb else 0)
    a_spec = pl.BlockSpec((k_dim, tm), lambda i, j: (0, i)) if ta else pl.BlockSpec((tm, k_dim), lambda i, j: (i, 0))
    b_spec = pl.BlockSpec((tn, k_dim), lambda i, j: (j, 0)) if tb else pl.BlockSpec((k_dim, tn), lambda i, j: (0, j))
    mn_spec = pl.BlockSpec((tm, tn), lambda i, j: (i, j))
    if out_chunk:
        per = out_chunk // tn
        o_spec = pl.BlockSpec((None, tm, tn), lambda i, j: (j // per, i, j % per))
        out_shape = jax.ShapeDtypeStruct((n // out_chunk, m, out_chunk), out_dtype)
    else:
        o_spec = mn_spec
        out_shape = jax.ShapeDtypeStruct((m, n), out_dtype)
    operands, in_specs = [a, b], [a_spec, b_spec]
    for extra in (gate, residual):
        if extra is not None:
            operands.append(extra)
            in_specs.append(mn_spec)

    def body(*refs):
        a_ref, b_ref, o_ref = refs[0], refs[1], refs[-1]
        extras = list(refs[2:-1])
        gate_ref = extras.pop(0) if gate is not None else None
        res_ref = extras.pop(0) if residual is not None else None
        av = a_ref[...].astype(BF16)
        if a_act == "relu2":
            av = jnp.square(jnp.maximum(av, jnp.zeros_like(av)))
        r = _dot(av, b_ref[...].astype(BF16), ca, cb)
        if gate_ref is not None:
            r = r * (2.0 * jnp.maximum(gate_ref[...].astype(F32), 0.0))
        if res_ref is not None:
            r = r + res_ref[...].astype(F32)
        o_ref[...] = r.astype(out_dtype)

    return _pallas(body, name=name, grid=(m // tm, n // tn), out_shape=[out_shape], in_specs=in_specs,
                   out_specs=[o_spec], operands=operands, semantics=("parallel", "arbitrary"), rides=rides)[0]


def _rmsnorm_fwd(xs, seg_widths, g, name, tm=256):
    t_len = xs[0].shape[0]
    width = sum(x.shape[1] for x in xs)
    tm = _pick(t_len, tm, 16)
    n = len(xs)

    def body(*refs):
        x_refs, g_ref, o_ref = refs[:n], refs[n], refs[n + 1]
        col = 0
        for x_ref, widths in zip(x_refs, seg_widths):
            off = 0
            for w in widths:
                xv = x_ref[:, off:off + w].astype(F32)
                r = lax.rsqrt(jnp.mean(xv * xv, axis=1, keepdims=True) + NORM_EPS)
                o_ref[:, col:col + w] = (xv * r * g_ref[:, col:col + w]).astype(BF16)
                off += w
                col += w

    return pl.pallas_call(
        body, name=name, grid=(t_len // tm,),
        out_shape=jax.ShapeDtypeStruct((t_len, width), BF16),
        in_specs=[pl.BlockSpec((tm, x.shape[1]), lambda i: (i, 0)) for x in xs]
        + [pl.BlockSpec((1, width), lambda i: (0, 0))],
        out_specs=pl.BlockSpec((tm, width), lambda i: (i, 0)),
        compiler_params=_params("parallel"),
    )(*xs, g)


def _rmsnorm_bwd(xs, seg_widths, g, dh, residuals, name, tm=256, bf16_copy=False):
    t_len = xs[0].shape[0]
    width = sum(x.shape[1] for x in xs)
    tm = _pick(t_len, tm, 8)
    n = len(xs)
    has_res = [r is not None for r in residuals]
    res_ops = [r for r in residuals if r is not None]

    def body(*refs):
        x_refs, g_ref, dh_ref = refs[:n], refs[n], refs[n + 1]
        res_refs = list(refs[n + 2:n + 2 + len(res_ops)])
        dx_refs = refs[n + 2 + len(res_ops):n + 2 + len(res_ops) + n]
        dg_ref = refs[n + 2 + len(res_ops) + n]
        copy_refs = refs[n + 3 + len(res_ops) + n:]
        first = pl.program_id(0) == 0
        col = 0
        for idx, (x_ref, widths) in enumerate(zip(x_refs, seg_widths)):
            res_ref = res_refs.pop(0) if has_res[idx] else None
            off = 0
            for w in widths:
                xv = x_ref[:, off:off + w].astype(F32)
                r = lax.rsqrt(jnp.mean(xv * xv, axis=1, keepdims=True) + NORM_EPS)
                xh = xv * r
                dhv = dh_ref[:, col:col + w].astype(F32)
                gd = dhv * g_ref[:, col:col + w]
                dx = r * (gd - xh * jnp.mean(gd * xh, axis=1, keepdims=True))
                if res_ref is not None:
                    dx = dx + res_ref[:, off:off + w]
                dx_refs[idx][:, off:off + w] = dx
                if bf16_copy:
                    copy_refs[idx][:, off:off + w] = dx.astype(BF16)
                part = jnp.sum(dhv * xh, axis=0, keepdims=True)

                @pl.when(first)
                def _(part=part, col=col, w=w):
                    dg_ref[:, col:col + w] = part

                @pl.when(jnp.logical_not(first))
                def _(part=part, col=col, w=w):
                    dg_ref[:, col:col + w] += part
                off += w
                col += w

    outs = pl.pallas_call(
        body, name=name, grid=(t_len // tm,),
        out_shape=[jax.ShapeDtypeStruct(x.shape, F32) for x in xs] + [jax.ShapeDtypeStruct((1, width), F32)]
        + ([jax.ShapeDtypeStruct(x.shape, BF16) for x in xs] if bf16_copy else []),
        in_specs=[pl.BlockSpec((tm, x.shape[1]), lambda i: (i, 0)) for x in xs]
        + [pl.BlockSpec((1, width), lambda i: (0, 0)), pl.BlockSpec((tm, width), lambda i: (i, 0))]
        + [pl.BlockSpec((tm, r.shape[1]), lambda i: (i, 0)) for r in res_ops],
        out_specs=[pl.BlockSpec((tm, x.shape[1]), lambda i: (i, 0)) for x in xs]
        + [pl.BlockSpec((1, width), lambda i: (0, 0))]
        + ([pl.BlockSpec((tm, x.shape[1]), lambda i: (i, 0)) for x in xs] if bf16_copy else []),
        compiler_params=_params("arbitrary"),
    )(*xs, g, dh, *res_ops)
    return outs[:n], outs[n], outs[n + 1:]


def _loss_head(x, g, target, name, tm=256):
    t_len, d = x.shape
    tm = _pick(t_len, tm, 8)

    def body(x_ref, g_ref, t_ref, dx_ref, dg_ref, loss_ref, dxb_ref):
        first = pl.program_id(0) == 0
        xv = x_ref[...]
        r = lax.rsqrt(jnp.mean(xv * xv, axis=1, keepdims=True) + NORM_EPS)
        xh = xv * r
        gv = g_ref[...]
        err = xh * gv - t_ref[...]
        part_loss = 0.5 * jnp.sum(jnp.mean(err * err, axis=1, keepdims=True), axis=0, keepdims=True)
        dy = err * (1.0 / d)
        gd = dy * gv
        dx = r * (gd - xh * jnp.mean(gd * xh, axis=1, keepdims=True))
        dx_ref[...] = dx
        dxb_ref[...] = dx.astype(BF16)
        part_g = jnp.sum(dy * xh, axis=0, keepdims=True)
        part_loss = jnp.broadcast_to(part_loss, (1, LANES))

        @pl.when(first)
        def _():
            dg_ref[...] = part_g
            loss_ref[...] = part_loss

        @pl.when(jnp.logical_not(first))
        def _():
            dg_ref[...] += part_g
            loss_ref[...] += part_loss

    return pl.pallas_call(
        body, name=name, grid=(t_len // tm,),
        out_shape=[jax.ShapeDtypeStruct((t_len, d), F32), jax.ShapeDtypeStruct((1, d), F32),
                   jax.ShapeDtypeStruct((1, LANES), F32), jax.ShapeDtypeStruct((t_len, d), BF16)],
        in_specs=[pl.BlockSpec((tm, d), lambda i: (i, 0)), pl.BlockSpec((1, d), lambda i: (0, 0)),
                  pl.BlockSpec((tm, d), lambda i: (i, 0))],
        out_specs=[pl.BlockSpec((tm, d), lambda i: (i, 0)), pl.BlockSpec((1, d), lambda i: (0, 0)),
                   pl.BlockSpec((1, LANES), lambda i: (0, 0)), pl.BlockSpec((tm, d), lambda i: (i, 0))],
        compiler_params=_params("arbitrary"),
    )(x, g, target)


def _alibi_slope(h, n_heads):
    return jnp.exp(jnp.full((1, 1), -8.0 * math.log(2.0) / n_heads, F32) * (h + 1).astype(F32))


def _attn_tiles(d, w):
    return ATTN_BLOCK * d, (w if d == 1 else LANES)


def _residue_rows(r, d):
    return pl.ds(r, ATTN_BLOCK, stride=d) if d > 1 else pl.ds(0, ATTN_BLOCK)


def _attn_masks(first_block):
    i = lax.broadcasted_iota(jnp.int32, (2 * ATTN_BLOCK, 2 * ATTN_BLOCK), 0) % ATTN_BLOCK
    j = lax.broadcasted_iota(jnp.int32, (2 * ATTN_BLOCK, 2 * ATTN_BLOCK), 1)
    delta = i - j + ATTN_BLOCK
    valid = jnp.logical_and(delta >= 0, delta <= ATTN_BLOCK)
    valid = jnp.logical_and(valid, jnp.logical_or(j >= ATTN_BLOCK, jnp.logical_not(first_block)))
    return valid, delta.astype(F32)


def _stack_heads(x, masks):
    zero = jnp.zeros_like(x)
    return jnp.concatenate([jnp.where(masks[0], x, zero), jnp.where(masks[1], x, zero)], axis=0)


def _unstack_heads(x2, masks):
    return jnp.where(masks[0], x2[:ATTN_BLOCK], x2[ATTN_BLOCK:])


def _pair_slopes(first_head, p, n_heads, d):
    row = lax.broadcasted_iota(jnp.int32, (2 * ATTN_BLOCK, 1), 0)
    sa, sb = (_alibi_slope(first_head + 2 * p + hh, n_heads) * d for hh in range(2))
    return jnp.where(row < ATTN_BLOCK, sa, sb)


def _head_lane_masks():
    lane = lax.broadcasted_iota(jnp.int32, (ATTN_BLOCK, LANES), 1)
    return [lane < HEAD_DIM, lane >= HEAD_DIM]


def _attn_branch_fwd(proj, w, dilation, n_heads, name, rides=None):
    t_len = proj.shape[0]
    d = dilation
    rows, lw = _attn_tiles(d, w)
    nb = t_len // rows
    n_pairs = lw // LANES
    per = w // lw
    scale = HEAD_DIM ** -0.5

    def body(q_ref, kp_ref, kc_ref, vp_ref, vc_ref, o_ref, lse_ref):
        first_head = pl.program_id(0) * (2 * n_pairs)
        first_block = pl.program_id(1) == 0
        valid, delta = _attn_masks(first_block)
        masks = _head_lane_masks()
        ones = jnp.ones((2 * ATTN_BLOCK, LANES), BF16)
        for p in range(n_pairs):
            cols = pl.ds(p * LANES, LANES)
            bias = _pair_slopes(first_head, p, n_heads, d) * delta
            for r in range(d):
                rs = _residue_rows(r, d)
                q2 = _stack_heads((q_ref[rs, cols] * scale).astype(BF16), masks)
                k2 = jnp.concatenate([kp_ref[rs, cols], kc_ref[rs, cols]], axis=0).astype(BF16)
                v2 = jnp.concatenate([vp_ref[rs, cols], vc_ref[rs, cols]], axis=0).astype(BF16)
                s = jnp.where(valid, _dot(q2, k2, 1, 1) - bias, NEG_INF)
                m = jnp.max(s, axis=1, keepdims=True)
                pr = jnp.exp(s - m).astype(BF16)
                den = _dot(pr, ones, 1, 0)
                o_ref[rs, cols] = _unstack_heads(_dot(pr, v2, 1, 0) / den, masks)
                lse_ref[rs, cols] = _unstack_heads(m + jnp.log(den), masks)

    def spec(which, prev):
        if prev:
            return pl.BlockSpec((rows, lw), lambda b, n: (jnp.maximum(n - 1, 0), which * per + b))
        return pl.BlockSpec((rows, lw), lambda b, n: (n, which * per + b))

    o_spec = pl.BlockSpec((rows, lw), lambda b, n: (n, b))
    return _pallas(
        body, name=name, grid=(per, nb), out_shape=[jax.ShapeDtypeStruct((t_len, w), F32)] * 2,
        in_specs=[spec(0, False), spec(1, True), spec(1, False), spec(2, True), spec(2, False)],
        out_specs=[o_spec, o_spec], operands=[proj] * 5, semantics=("parallel", "parallel"), rides=rides)


def _attn_combine(outs, lses, name, tm=512):
    t_len, w = outs[0].shape
    tm = _pick(t_len, tm, 8)
    nbr = len(outs)

    def body(*refs):
        o_refs, l_refs = refs[:nbr], refs[nbr:2 * nbr]
        out_ref, lse_ref = refs[2 * nbr:]
        ls = [r[...] for r in l_refs]
        m = functools.reduce(jnp.maximum, ls)
        es = [jnp.exp(l - m) for l in ls]
        den = functools.reduce(lambda a, b: a + b, es)
        num = functools.reduce(lambda a, b: a + b, [e * r[...] for e, r in zip(es, o_refs)])
        out_ref[...] = num / den
        lse_ref[...] = m + jnp.log(den)

    spec = pl.BlockSpec((tm, w), lambda i: (i, 0))
    return pl.pallas_call(
        body, name=name, grid=(t_len // tm,),
        out_shape=[jax.ShapeDtypeStruct((t_len, w), F32)] * 2,
        in_specs=[spec] * (2 * nbr), out_specs=[spec, spec],
        compiler_params=_params("parallel"),
    )(*outs, *lses)


def _attn_branch_bwd(proj, w, out, lse, dout, dilation, n_heads, name, acc=None, rides=None):
    t_len = proj.shape[0]
    d = dilation
    rows, lw = _attn_tiles(d, w)
    nb = t_len // rows
    n_pairs = lw // LANES
    per = w // lw
    scale = HEAD_DIM ** -0.5
    n_acc = 0 if acc is None else 3

    def body(*refs):
        q_ref, kp_ref, kc_ref, vp_ref, vc_ref, out_ref, lse_ref, do_ref = refs[:8]
        acc_refs = refs[8:8 + n_acc]
        dq_ref, dk_ref, dv_ref, dk_carry, dv_carry = refs[8 + n_acc:]
        first_head = pl.program_id(0) * (2 * n_pairs)
        n = pl.program_id(1)
        first_block = n == 0
        valid, dist = _attn_masks(first_block)
        masks = _head_lane_masks()

        def plus(value, idx, *where):
            return value + acc_refs[idx][where] if n_acc else value

        @pl.when(first_block)
        def _():
            dk_carry[...] = jnp.zeros_like(dk_carry)
            dv_carry[...] = jnp.zeros_like(dv_carry)

        @pl.when(n < nb)
        def _():
            for p in range(n_pairs):
                cols = pl.ds(p * LANES, LANES)
                bias = _pair_slopes(first_head, p, n_heads, d) * dist
                for r in range(d):
                    rs = _residue_rows(r, d)
                    q2 = _stack_heads((q_ref[rs, cols] * scale).astype(BF16), masks)
                    k2 = jnp.concatenate([kp_ref[rs, cols], kc_ref[rs, cols]], axis=0).astype(BF16)
                    v2 = jnp.concatenate([vp_ref[rs, cols], vc_ref[rs, cols]], axis=0).astype(BF16)
                    do = do_ref[rs, cols]
                    do2 = _stack_heads(do.astype(BF16), masks)
                    do_out = do * out_ref[rs, cols]
                    lse_all = lse_ref[rs, cols]
                    delta = jnp.concatenate([jnp.sum(jnp.where(masks[hh], do_out, 0.0), axis=1, keepdims=True)
                                             for hh in range(2)], axis=0)
                    lse2 = jnp.concatenate([jnp.max(jnp.where(masks[hh], lse_all, NEG_INF), axis=1, keepdims=True)
                                            for hh in range(2)], axis=0)
                    s = jnp.where(valid, _dot(q2, k2, 1, 1) - bias, NEG_INF)
                    pr = jnp.exp(s - lse2)
                    ds = (pr * (_dot(do2, v2, 1, 1) - delta)).astype(BF16)
                    dq = _unstack_heads(_dot(ds, k2, 1, 0), masks)
                    dk2 = _dot(ds, q2, 0, 0)
                    dv2 = _dot(pr.astype(BF16), do2, 0, 0)
                    dq_ref[rs, cols] = plus(dq * scale, 0, rs, cols)
                    dk_ref[rs, cols] = plus(dk_carry[rs, cols] + dk2[:ATTN_BLOCK], 1, rs, cols)
                    dv_ref[rs, cols] = plus(dv_carry[rs, cols] + dv2[:ATTN_BLOCK], 2, rs, cols)
                    dk_carry[rs, cols] = dk2[ATTN_BLOCK:]
                    dv_carry[rs, cols] = dv2[ATTN_BLOCK:]

        @pl.when(n == nb)
        def _():
            dk_ref[...] = plus(dk_carry[...], 1, Ellipsis)
            dv_ref[...] = plus(dv_carry[...], 2, Ellipsis)

    def qkv_spec(which, shift):
        return pl.BlockSpec((rows, lw), lambda b, n: (jnp.clip(n - shift, 0, nb - 1), which * per + b))

    q_like = pl.BlockSpec((rows, lw), lambda b, n: (jnp.minimum(n, nb - 1), b))
    k_like = pl.BlockSpec((rows, lw), lambda b, n: (jnp.maximum(n - 1, 0), b))
    return _pallas(
        body, name=name, grid=(per, nb + 1), out_shape=[jax.ShapeDtypeStruct((t_len, w), F32)] * 3,
        in_specs=[qkv_spec(0, 0), qkv_spec(1, 1), qkv_spec(1, 0), qkv_spec(2, 1), qkv_spec(2, 0),
                  q_like, q_like, q_like] + [q_like, k_like, k_like][:n_acc],
        out_specs=[q_like, k_like, k_like], operands=[proj] * 5 + [out, lse, dout, *(acc or ())],
        scratch_shapes=[pltpu.VMEM((rows, lw), F32), pltpu.VMEM((rows, lw), F32)],
        semantics=("parallel", "arbitrary"), rides=rides)


def _shift_down(u, s):
    if s == 0:
        return u
    row = lax.broadcasted_iota(jnp.int32, u.shape, 0)
    return jnp.where(row >= s, pltpu.roll(u, s, 0), 0.0)


def _shift_up(u, s):
    if s == 0:
        return u
    n = u.shape[0]
    row = lax.broadcasted_iota(jnp.int32, u.shape, 0)
    return jnp.where(row < n - s, pltpu.roll(u, n - s, 0), 0.0)


def _conv_fwd(u, col0, w, b, name):
    t_len, ch = u.shape[0], w.shape[1]
    blk0 = col0 // LANES

    def body(u_ref, w_ref, b_ref, o_ref):
        uv = u_ref[...]
        pre = b_ref[...] + jnp.zeros_like(uv)
        for k in range(SSD_CONV):
            pre = pre + w_ref[k:k + 1, :] * _shift_down(uv, SSD_CONV - 1 - k)
        o_ref[...] = pre * jax.nn.sigmoid(pre)

    return pl.pallas_call(
        body, name=name, grid=(ch // LANES,),
        out_shape=jax.ShapeDtypeStruct((t_len, ch), F32),
        in_specs=[pl.BlockSpec((t_len, LANES), lambda j: (0, blk0 + j)),
                  pl.BlockSpec((SSD_CONV, LANES), lambda j: (0, j)), pl.BlockSpec((1, LANES), lambda j: (0, j))],
        out_specs=pl.BlockSpec((t_len, LANES), lambda j: (0, j)),
        compiler_params=_params("parallel"),
    )(u, w, b)


def _conv_bwd(u, col0, w, b, dact, name):
    t_len, ch = u.shape[0], w.shape[1]
    blk0 = col0 // LANES

    def body(u_ref, w_ref, b_ref, da_ref, du_ref, dw_ref, db_ref):
        uv = u_ref[...]
        shifted = [_shift_down(uv, SSD_CONV - 1 - k) for k in range(SSD_CONV)]
        pre = b_ref[...] + jnp.zeros_like(uv)
        for k in range(SSD_CONV):
            pre = pre + w_ref[k:k + 1, :] * shifted[k]
        sig = jax.nn.sigmoid(pre)
        dpre = da_ref[...] * (sig * (1.0 + pre * (1.0 - sig)))
        du = jnp.zeros_like(uv)
        for k in range(SSD_CONV):
            du = du + w_ref[k:k + 1, :] * _shift_up(dpre, SSD_CONV - 1 - k)
            dw_ref[k:k + 1, :] = jnp.sum(dpre * shifted[k], axis=0, keepdims=True)
        du_ref[...] = du
        db_ref[...] = jnp.sum(dpre, axis=0, keepdims=True)

    col = pl.BlockSpec((t_len, LANES), lambda j: (0, j))
    w_spec = pl.BlockSpec((SSD_CONV, LANES), lambda j: (0, j))
    b_spec = pl.BlockSpec((1, LANES), lambda j: (0, j))
    return pl.pallas_call(
        body, name=name, grid=(ch // LANES,),
        out_shape=[jax.ShapeDtypeStruct((t_len, ch), F32), jax.ShapeDtypeStruct((SSD_CONV, ch), F32),
                   jax.ShapeDtypeStruct((1, ch), F32)],
        in_specs=[pl.BlockSpec((t_len, LANES), lambda j: (0, blk0 + j)), w_spec, b_spec, col],
        out_specs=[col, w_spec, b_spec],
        compiler_params=_params("parallel"),
    )(u, w, b, dact)


def _cumsum_rows(v):
    n = v.shape[0]
    row = lax.broadcasted_iota(jnp.int32, v.shape, 0)
    s = 1
    while s < n:
        v = v + jnp.where(row >= s, pltpu.roll(v, s, 0), 0.0)
        s *= 2
    return v


def _rev_cumsum_rows(v):
    n = v.shape[0]
    row = lax.broadcasted_iota(jnp.int32, v.shape, 0)
    s = 1
    while s < n:
        v = v + jnp.where(row < n - s, pltpu.roll(v, n - s, 0), 0.0)
        s *= 2
    return v


def _head_selector(heads, width):
    j = lax.broadcasted_iota(jnp.int32, (LANES, width), 0)
    lane = lax.broadcasted_iota(jnp.int32, (LANES, width), 1)
    return jnp.where(jnp.logical_and(lane // HEAD_DIM == j, j < heads), 1.0, 0.0).astype(BF16)


class _SsdChunk:
    def __init__(self, dtraw_ref, bias_ref, alog_ref, xs_ref, b_ref, c_ref, heads):
        q = SSD_CHUNK
        width = heads * HEAD_DIM
        lane = lax.broadcasted_iota(jnp.int32, (q, LANES), 1)
        self.head_lanes = lane < heads
        lane1 = lax.broadcasted_iota(jnp.int32, (1, LANES), 1)
        self.a = jnp.where(lane1 < heads, -jnp.exp(alog_ref[...]), 0.0)
        self.dt_arg = dtraw_ref[...] + bias_ref[...]
        self.dt = jnp.where(self.head_lanes, jax.nn.softplus(self.dt_arg), 0.0)
        self.cum = _cumsum_rows(self.dt * self.a)
        self.cum_t = self.cum.T
        last = self.cum[q - 1:q, :]
        self.sel = _head_selector(heads, width)
        self.expand = lambda v: _dot_exact(v, self.sel, 1, 0)
        self.segsum = lambda v: _dot_exact(v, self.sel, 1, 1)
        self.e_exp = self.expand(jnp.exp(self.cum))
        self.d_exp = self.expand(jnp.exp(last - self.cum))
        self.elast_exp = self.e_exp[q - 1:q, :]
        self.dt_exp = self.expand(self.dt)
        self.xs = xs_ref[...]
        self.x = self.xs * self.dt_exp
        self.xb = self.x.astype(BF16)
        self.bb = b_ref[...].astype(BF16)
        self.cb = c_ref[...].astype(BF16)
        self.cbm = _dot(self.cb, self.bb, 1, 1)
        li = lax.broadcasted_iota(jnp.int32, (q, q), 0)
        si = lax.broadcasted_iota(jnp.int32, (q, q), 1)
        self.tri = li >= si
        hl = lax.broadcasted_iota(jnp.int32, (q, LANES), 1)
        self.pair_masks = [hl < HEAD_DIM, hl >= HEAD_DIM]

    def decay(self, j):
        diff = self.cum[:, j:j + 1] - self.cum_t[j:j + 1, :]
        return jnp.exp(jnp.where(self.tri, diff, NEG_INF))


def _ssd_specs(t_len, heads, n_chunks, xbc_cols, rev):
    q, gw = SSD_CHUNK, heads * HEAD_DIM
    ssd_w = SSD_GROUPS * gw
    b_blk = ssd_w // SSD_STATE
    ch = (lambda c: n_chunks - 1 - c) if rev else (lambda c: c)
    return dict(
        dtraw=pl.BlockSpec((None, q, LANES), lambda g, c: (g, ch(c), 0)),
        small=pl.BlockSpec((None, 1, LANES), lambda g, c: (g, 0, 0)),
        dsk=pl.BlockSpec((None, 1, gw), lambda g, c: (g, 0, 0)),
        xs=pl.BlockSpec((q, gw), lambda g, c: (ch(c), g)),
        b=pl.BlockSpec((q, SSD_STATE), lambda g, c: (ch(c), b_blk + g)),
        c=pl.BlockSpec((q, SSD_STATE), lambda g, c: (ch(c), b_blk + SSD_GROUPS + g)),
        z=pl.BlockSpec((q, gw), lambda g, c: (ch(c), 3 * SSD_GROUPS + g)),
        tok=pl.BlockSpec((q, gw), lambda g, c: (ch(c), g)),
        state=pl.BlockSpec((None, SSD_STATE, gw), lambda g, c: (ch(c), 0, g)),
        bc=pl.BlockSpec((q, SSD_STATE), lambda g, c: (ch(c), g)),
    )


def _ssd_fwd(xbc, qkvz, dtraw_g, bias_g, alog_g, dsk_exp, heads, name):
    t_len = xbc.shape[0]
    q, gw = SSD_CHUNK, heads * HEAD_DIM
    n_chunks = t_len // q
    ssd_w = SSD_GROUPS * gw
    sp = _ssd_specs(t_len, heads, n_chunks, xbc.shape[1], rev=False)

    def body(dtraw_ref, bias_ref, alog_ref, dsk_ref, xs_ref, b_ref, c_ref, z_ref,
             yg_ref, ypre_ref, st_ref, s_scr):
        @pl.when(pl.program_id(1) == 0)
        def _():
            s_scr[...] = jnp.zeros_like(s_scr)

        k = _SsdChunk(dtraw_ref, bias_ref, alog_ref, xs_ref, b_ref, c_ref, heads)
        s_prev = s_scr[...]
        st_ref[...] = s_prev
        y_off = k.e_exp * _dot(k.cb, s_prev.astype(BF16), 1, 0)
        parts = []
        for p in range(heads // 2):
            xp = k.xb[:, p * LANES:(p + 1) * LANES]
            acc = jnp.zeros((q, LANES), F32)
            for hh in range(2):
                m = (k.cbm * k.decay(2 * p + hh)).astype(BF16)
                acc = acc + _dot(m, jnp.where(k.pair_masks[hh], xp, jnp.zeros_like(xp)), 1, 0)
            parts.append(acc)
        y = jnp.concatenate(parts, axis=1) + y_off
        xd = (k.x * k.d_exp).astype(BF16)
        s_scr[...] = k.elast_exp * s_prev + _dot(k.bb, xd, 0, 0)
        y_pre = y + dsk_ref[...] * k.xs
        zv = z_ref[...]
        ypre_ref[...] = y_pre
        yg_ref[...] = y_pre * (zv * jax.nn.sigmoid(zv))

    return pl.pallas_call(
        body, name=name, grid=(SSD_GROUPS, n_chunks),
        out_shape=[jax.ShapeDtypeStruct((t_len, ssd_w), F32), jax.ShapeDtypeStruct((t_len, ssd_w), F32),
                   jax.ShapeDtypeStruct((n_chunks, SSD_STATE, ssd_w), F32)],
        in_specs=[sp["dtraw"], sp["small"], sp["small"], sp["dsk"], sp["xs"], sp["b"], sp["c"], sp["z"]],
        out_specs=[sp["tok"], sp["tok"], sp["state"]],
        scratch_shapes=[pltpu.VMEM((SSD_STATE, gw), F32)],
        compiler_params=_params("parallel", "arbitrary"),
    )(dtraw_g, bias_g, alog_g, dsk_exp, xbc, xbc, xbc, qkvz)


def _ssd_bwd(xbc, qkvz, dtraw_g, bias_g, alog_g, dsk_exp, ypre, states, dyg, heads, name):
    t_len = xbc.shape[0]
    q, gw = SSD_CHUNK, heads * HEAD_DIM
    n_chunks = t_len // q
    ssd_w = SSD_GROUPS * gw
    sp = _ssd_specs(t_len, heads, n_chunks, xbc.shape[1], rev=True)

    def body(dtraw_ref, bias_ref, alog_ref, dsk_ref, xs_ref, b_ref, c_ref, z_ref, ypre_ref, st_ref, dyg_ref,
             dxs_ref, db_ref, dc_ref, dz_ref, ddt_ref, small_ref, g_scr):
        first = pl.program_id(1) == 0

        @pl.when(first)
        def _():
            g_scr[...] = jnp.zeros_like(g_scr)

        k = _SsdChunk(dtraw_ref, bias_ref, alog_ref, xs_ref, b_ref, c_ref, heads)
        zv = z_ref[...]
        sig = jax.nn.sigmoid(zv)
        dyg = dyg_ref[...]
        y_pre = ypre_ref[...]
        dy = dyg * (zv * sig)
        dz_ref[...] = dyg * y_pre * (sig * (1.0 + zv * (1.0 - sig)))
        dsk = dsk_ref[...]
        g_next = g_scr[...]
        s_prev = st_ref[...]
        sb = s_prev.astype(BF16)
        xd = k.x * k.d_exp
        xdb = xd.astype(BF16)
        gb = g_next.astype(BF16)
        dx_off = k.d_exp * _dot(k.bb, gb, 1, 0)
        dyb = dy.astype(BF16)
        dcb = jnp.zeros((q, q), F32)
        lane = lax.broadcasted_iota(jnp.int32, (q, LANES), 1)
        row_t = lax.broadcasted_iota(jnp.int32, (LANES, q), 0)
        w_rows = jnp.zeros((q, LANES), F32)
        w_cols_t = jnp.zeros((LANES, q), F32)
        parts = []
        for p in range(heads // 2):
            cols = slice(p * LANES, (p + 1) * LANES)
            dyp, xp = dyb[:, cols], k.xb[:, cols]
            acc = jnp.zeros((q, LANES), F32)
            for hh in range(2):
                j = 2 * p + hh
                lm = k.decay(j)
                m32 = k.cbm * lm
                dym = jnp.where(k.pair_masks[hh], dyp, jnp.zeros_like(dyp))
                acc = acc + _dot(m32.astype(BF16), dym, 0, 0)
                dm = _dot(dym, xp, 1, 1)
                dcb = dcb + dm * lm
                wmat = dm * m32
                w_rows = w_rows + jnp.where(lane == j, jnp.sum(wmat, axis=1, keepdims=True), 0.0)
                w_cols_t = w_cols_t + jnp.where(row_t == j, jnp.sum(wmat, axis=0, keepdims=True), 0.0)
            parts.append(acc)
        dx = jnp.concatenate(parts, axis=1) + dx_off
        dcbb = dcb.astype(BF16)
        edy = (k.e_exp * dy).astype(BF16)
        dc_ref[...] = _dot(dcbb, k.bb, 1, 0) + _dot(edy, sb, 1, 1)
        db_ref[...] = _dot(dcbb, k.cb, 0, 0) + _dot(xdb, gb, 1, 1)
        g_scr[...] = k.elast_exp * g_next + _dot(k.cb, edy, 0, 0)

        y_off = k.e_exp * _dot(k.cb, sb, 1, 0)
        dcum = w_rows - w_cols_t.T + k.segsum(dy * y_off)
        t_term = k.segsum(k.x * dx_off)
        gs = jnp.broadcast_to(jnp.sum(g_next * s_prev, axis=0, keepdims=True), (8, gw))
        carried = k.segsum(gs)[0:1, :] * jnp.exp(k.cum[q - 1:q, :])
        dda = _rev_cumsum_rows(dcum) + (_cumsum_rows(t_term) - t_term) + carried
        ddt = jnp.where(k.head_lanes, dda * k.a + k.segsum(dx * k.xs), 0.0)
        ddtraw = ddt * jax.nn.sigmoid(k.dt_arg)
        ddt_ref[...] = ddtraw
        dxs_ref[...] = dx * k.dt_exp + dsk * dy
        ds = jnp.broadcast_to(jnp.sum(dy * k.xs, axis=0, keepdims=True), (8, gw))
        d_alog = jnp.sum(jnp.where(k.head_lanes, dda * k.dt, 0.0), axis=0, keepdims=True) * k.a
        rows8 = lax.broadcasted_iota(jnp.int32, (8, LANES), 0)
        small = jnp.where(rows8 == 0, d_alog, 0.0)
        small = small + jnp.where(rows8 == 1, jnp.sum(ddtraw, axis=0, keepdims=True), 0.0)
        small = small + jnp.where(rows8 == 2, k.segsum(ds)[0:1, :], 0.0)

        @pl.when(first)
        def _():
            small_ref[...] = small

        @pl.when(jnp.logical_not(first))
        def _():
            small_ref[...] += small

    bc_out = sp["bc"]
    return pl.pallas_call(
        body, name=name, grid=(SSD_GROUPS, n_chunks),
        out_shape=[jax.ShapeDtypeStruct((t_len, ssd_w), F32),
                   jax.ShapeDtypeStruct((t_len, SSD_GROUPS * SSD_STATE), F32),
                   jax.ShapeDtypeStruct((t_len, SSD_GROUPS * SSD_STATE), F32),
                   jax.ShapeDtypeStruct((t_len, ssd_w), F32),
                   jax.ShapeDtypeStruct((SSD_GROUPS, t_len, LANES), F32),
                   jax.ShapeDtypeStruct((SSD_GROUPS, 8, LANES), F32)],
        in_specs=[sp["dtraw"], sp["small"], sp["small"], sp["dsk"], sp["xs"], sp["b"], sp["c"], sp["z"],
                  sp["tok"], sp["state"], sp["tok"]],
        out_specs=[sp["tok"], bc_out, bc_out, sp["tok"], sp["dtraw"],
                   pl.BlockSpec((None, 8, LANES), lambda g, c: (g, 0, 0))],
        scratch_shapes=[pltpu.VMEM((SSD_STATE, gw), F32)],
        compiler_params=_params("parallel", "arbitrary"),
    )(dtraw_g, bias_g, alog_g, dsk_exp, xbc, xbc, xbc, qkvz, ypre, states, dyg)


def _adamw(w, g, m, v, name):
    rows, lanes = w.shape
    tr = _row_tile(rows, lanes, 4, 14)
    c1 = 1.0 / (1.0 - ADAM_B1 ** ADAM_STEP)
    c2 = 1.0 / (1.0 - ADAM_B2 ** ADAM_STEP)

    def body(w_ref, g_ref, m_ref, v_ref, d_ref, nm_ref, nv_ref):
        gv = g_ref[...]
        nm = ADAM_B1 * m_ref[...] + (1.0 - ADAM_B1) * gv
        nv = ADAM_B2 * v_ref[...] + (1.0 - ADAM_B2) * (gv * gv)
        nm_ref[...] = nm
        nv_ref[...] = nv
        d_ref[...] = -ADAM_LR * ((nm * c1) / (jnp.sqrt(nv * c2) + ADAM_EPS) + ADAM_WD * w_ref[...])

    spec = pl.BlockSpec((tr, lanes), lambda i: (i, 0))
    return pl.pallas_call(
        body, name=name, grid=(rows // tr,),
        out_shape=[jax.ShapeDtypeStruct((rows, lanes), F32)] * 3,
        in_specs=[spec] * 4, out_specs=[spec] * 3,
        compiler_params=_params("parallel"),
    )(w, g, m, v)


def _pad_lanes(a, width=LANES):
    return jnp.pad(a, ((0, 0), (0, width - a.shape[1])))


def _group_pad(v, heads):
    return _pad_lanes(v.reshape(SSD_GROUPS, heads))[:, None, :]


def _layer_fwd(x0, p, wt, dims, tag, rides):
    w_attn, heads_g, n_heads, conv_ch = dims["w_attn"], dims["heads_g"], dims["n_heads"], dims["conv_ch"]
    h1 = _rmsnorm_fwd([x0], [[x0.shape[1]]], p["ln1_g"], f"ln1_fwd{tag}")
    proj = _mm(h1, wt("w_in"), name=f"in_proj{tag}", tn=640, rides=rides)

    outs, lses = [], []
    for d in BRANCH_DILATIONS:
        o, l = _attn_branch_fwd(proj, w_attn, d, n_heads, f"attn_fwd_d{d}{tag}", rides)
        outs.append(o)
        lses.append(l)
    attn, lse = _attn_combine(outs, lses, f"attn_combine{tag}")

    xbc = _conv_fwd(proj, 4 * w_attn, p["conv_w"], p["conv_b"], f"conv_fwd{tag}")
    dt_col = 4 * w_attn + conv_ch
    dtraw_g = jnp.stack([_pad_lanes(proj[:, dt_col + g * heads_g:dt_col + (g + 1) * heads_g])
                         for g in range(SSD_GROUPS)])
    bias_g, alog_g = _group_pad(p["dt_bias"], heads_g), _group_pad(p["a_log"], heads_g)
    dsk_exp = jnp.repeat(p["d_skip"], HEAD_DIM).reshape(SSD_GROUPS, 1, heads_g * HEAD_DIM)
    yg, ypre, states = _ssd_fwd(xbc, proj, dtraw_g, bias_g, alog_g, dsk_exp, heads_g, f"ssd_fwd{tag}")

    gw = heads_g * HEAD_DIM
    mix_g = jnp.concatenate([p["attn_norm_g"], p["ssd_norm_g"]])[None, :]
    mix = _rmsnorm_fwd([attn, yg], [[w_attn], [gw] * SSD_GROUPS], mix_g, f"mix_norm_fwd{tag}")
    x1 = _mm(mix, wt("w_out"), name=f"out_proj{tag}", residual=x0, rides=rides)
    h2 = _rmsnorm_fwd([x1], [[x1.shape[1]]], p["ln2_g"], f"ln2_fwd{tag}")
    u = _mm(h2, wt("w_mlp_in"), name=f"mlp_in{tag}", out_dtype=BF16, tn=1024, rides=rides)
    x2 = _mm(u, wt("w_mlp_out"), name=f"mlp_out{tag}", a_act="relu2", residual=x1, tm=512, rides=rides)
    saved = dict(x0=x0, h1=h1, proj=proj, attn=attn, lse=lse, xbc=xbc, dtraw_g=dtraw_g,
                 bias_g=bias_g, alog_g=alog_g, dsk_exp=dsk_exp, yg=yg, ypre=ypre, states=states, mix=mix,
                 mix_g=mix_g, x1=x1, h2=h2, u=u)
    return x2, saved


def _pair_sums(ex, host, items):
    swapped = ex["rides"].done[("swap", host)]
    core = lax.axis_index("c")
    for i, (n, l) in enumerate(items):
        buf = ex["bufs"][(n, l)]
        mine = lax.dynamic_index_in_dim(buf.reshape(N_DEV // 2, 2, *buf.shape[1:]), core, axis=1, keepdims=False)
        ex["pair"][(n, l)] = _sum_leading([mine, swapped[i]], f"pair_sum_{n}_l{l}", out_dtype=BF16)


def _layer_bwd(dx2, dx2_b, p, wt, s, dims, l, ex, copy_dx0):
    w_attn, heads_g, n_heads, conv_ch = dims["w_attn"], dims["heads_g"], dims["n_heads"], dims["conv_ch"]
    t_len, d_model = dx2.shape
    gw = heads_g * HEAD_DIM
    h_ssd = heads_g * SSD_GROUPS
    tag, rides, bufs = f"_l{l}", ex["rides"], ex["bufs"]
    du = _mm(dx2_b, wt("w_mlp_out"), name=f"mlp_out_dx{tag}", tb=True, gate=s["u"], out_dtype=BF16, tn=1024,
             rides=rides)
    d_wmo = _mm(s["u"], dx2_b, name=f"mlp_out_dw{tag}", ta=True, a_act="relu2", tm=512, tn=1024, out_dtype=BF16)
    bufs[("w_mlp_out", l)] = d_wmo.reshape(N_DEV, -1, d_model)
    bufs[("w_mlp_in", l)] = _mm(s["h2"], du, name=f"mlp_in_dw{tag}", ta=True, tm=512, tn=1024, out_dtype=BF16,
                                out_chunk=du.shape[1] // N_DEV)
    dh2 = _mm(du, wt("w_mlp_in"), name=f"mlp_in_dx{tag}", tb=True, tm=512, rides=rides)
    _pair_sums(ex, f"mlp_in_dx{tag}", [("w_mlp_out", l), ("w_mlp_in", l)])
    (dx1,), d_ln2, (dx1_b,) = _rmsnorm_bwd([s["x1"]], [[d_model]], p["ln2_g"], dh2, [dx2], f"ln2_bwd{tag}",
                                           bf16_copy=True)
    dmix = _mm(dx1_b, wt("w_out"), name=f"out_proj_dx{tag}", tb=True)
    d_wo = _mm(s["mix"], dx1_b, name=f"out_proj_dw{tag}", ta=True, tm=512, tn=1024, out_dtype=BF16)
    bufs[("w_out", l)] = d_wo.reshape(N_DEV, -1, d_model)
    (dattn, dyg), d_mix_g, _ = _rmsnorm_bwd([s["attn"], s["yg"]], [[w_attn], [gw] * SSD_GROUPS], s["mix_g"], dmix,
                                           [None, None], f"mix_norm_bwd{tag}")
    dxs, db, dc, dz, ddtraw_g, ssd_small = _ssd_bwd(
        s["xbc"], s["proj"], s["dtraw_g"], s["bias_g"], s["alog_g"], s["dsk_exp"], s["ypre"], s["states"], dyg,
        heads_g, f"ssd_bwd{tag}")
    dxbc = jnp.concatenate([dxs, db, dc], axis=1)
    dxbc_raw, d_conv_w, d_conv_b = _conv_bwd(s["proj"], 4 * w_attn, p["conv_w"], p["conv_b"], dxbc, f"conv_bwd{tag}")
    acc = None
    for d in BRANCH_DILATIONS:
        acc = _attn_branch_bwd(s["proj"], w_attn, s["attn"], s["lse"], dattn, d, n_heads, f"attn_bwd_d{d}{tag}", acc,
                               rides)
    w_in = wt("w_in")
    in_proj = 4 * w_attn + conv_ch + h_ssd
    pad = jnp.zeros((t_len, w_in.shape[1] - in_proj), F32)
    dproj = jnp.concatenate([*acc, dz, dxbc_raw] + [ddtraw_g[g, :, :heads_g] for g in range(SSD_GROUPS)] + [pad],
                            axis=1).astype(BF16)
    d_win = _mm(s["h1"], dproj, name=f"in_proj_dw{tag}", ta=True, tm=512, tn=1152)
    bufs[("w_in", l)] = d_win[:, :in_proj].reshape(d_model, N_DEV, -1).transpose(1, 0, 2).astype(BF16)
    dh1 = _mm(dproj, w_in, name=f"in_proj_dx{tag}", tb=True, tm=512, rides=rides)
    _pair_sums(ex, f"in_proj_dx{tag}", [("w_out", l), ("w_in", l)])
    (dx0,), d_ln1, dx0_b = _rmsnorm_bwd([s["x0"]], [[d_model]], p["ln1_g"], dh1, [dx1], f"ln1_bwd{tag}",
                                        bf16_copy=copy_dx0)

    small = ssd_small[:, :, :heads_g]
    grads = dict(
        ln1_g=d_ln1[0], conv_w=d_conv_w, conv_b=d_conv_b[0],
        a_log=small[:, 0].reshape(h_ssd), dt_bias=small[:, 1].reshape(h_ssd), d_skip=small[:, 2].reshape(h_ssd),
        attn_norm_g=d_mix_g[0, :w_attn], ssd_norm_g=d_mix_g[0, w_attn:], ln2_g=d_ln2[0])
    return dx0, (dx0_b[0] if copy_dx0 else None), grads


_SMALL = ["ln1_g", "conv_w", "conv_b", "dt_bias", "a_log", "d_skip", "attn_norm_g", "ssd_norm_g", "ln2_g"]
_WEIGHTS = ["ln1_g", "w_in", "conv_w", "conv_b", "dt_bias", "a_log", "d_skip", "attn_norm_g", "ssd_norm_g",
            "w_out", "ln2_g", "w_mlp_in", "w_mlp_out", "final_norm_g"]


def _to_rows(a):
    flat = a.reshape(-1)
    rows = -(-flat.shape[0] // LANES)
    rows = -(-rows // 8) * 8
    return jnp.pad(flat, (0, rows * LANES - flat.shape[0])).reshape(rows, LANES)


def kernel(x, ln1_g, w_in, conv_w, conv_b, dt_bias, a_log, d_skip, attn_norm_g, ssd_norm_g, w_out, ln2_g, w_mlp_in, w_mlp_out, final_norm_g, loss_target, m_ln1_g, m_w_in, m_conv_w, m_conv_b, m_dt_bias, m_a_log, m_d_skip, m_attn_norm_g, m_ssd_norm_g, m_w_out, m_ln2_g, m_w_mlp_in, m_w_mlp_out, m_final_norm_g, v_ln1_g, v_w_in, v_conv_w, v_conv_b, v_dt_bias, v_a_log, v_d_skip, v_attn_norm_g, v_ssd_norm_g, v_w_out, v_ln2_g, v_w_mlp_in, v_w_mlp_out, v_final_norm_g):
    w = dict(ln1_g=ln1_g, w_in=w_in, conv_w=conv_w, conv_b=conv_b, dt_bias=dt_bias, a_log=a_log, d_skip=d_skip,
             attn_norm_g=attn_norm_g, ssd_norm_g=ssd_norm_g, w_out=w_out, ln2_g=ln2_g, w_mlp_in=w_mlp_in,
             w_mlp_out=w_mlp_out, final_norm_g=final_norm_g)
    mom = dict(ln1_g=m_ln1_g, w_in=m_w_in, conv_w=m_conv_w, conv_b=m_conv_b, dt_bias=m_dt_bias, a_log=m_a_log,
               d_skip=m_d_skip, attn_norm_g=m_attn_norm_g, ssd_norm_g=m_ssd_norm_g, w_out=m_w_out, ln2_g=m_ln2_g,
               w_mlp_in=m_w_mlp_in, w_mlp_out=m_w_mlp_out, final_norm_g=m_final_norm_g)
    var = dict(ln1_g=v_ln1_g, w_in=v_w_in, conv_w=v_conv_w, conv_b=v_conv_b, dt_bias=v_dt_bias, a_log=v_a_log,
               d_skip=v_d_skip, attn_norm_g=v_attn_norm_g, ssd_norm_g=v_ssd_norm_g, w_out=v_w_out, ln2_g=v_ln2_g,
               w_mlp_in=v_w_mlp_in, w_mlp_out=v_w_mlp_out, final_norm_g=v_final_norm_g)

    depth, d_model = ln1_g.shape
    t_len = x.shape[1]
    w_attn = attn_norm_g.shape[1]
    h_ssd = dt_bias.shape[1]
    conv_ch = conv_b.shape[1]
    in_proj = w_in.shape[2] * N_DEV
    assert ssd_norm_g.shape[1] == w_attn and in_proj == 4 * w_attn + conv_ch + h_ssd
    assert t_len % (BRANCH_DILATIONS[-1] * ATTN_BLOCK) == 0 and h_ssd % (2 * SSD_GROUPS) == 0
    dims = dict(w_attn=w_attn, heads_g=h_ssd // SSD_GROUPS, n_heads=w_attn // HEAD_DIM, conv_ch=conv_ch)
    names = ["w_in", "w_out", "w_mlp_in", "w_mlp_out"]

    rides = _Rides()
    ex = dict(rides=rides, bufs={}, pair={})
    where, sent = {}, {}

    def shard(n, l):
        return w[n][l].astype(BF16)

    def plan_spread(host, items):
        rides.put(host, ("spread", host), lambda: _GatherSpread([shard(n, l) for n, l in items]))
        return host, items

    def plan_pass(host, spreads):
        rides.put(host, ("pass", host),
                  lambda: _GatherPass([b for h, _ in spreads for b in rides.done[("spread", h)]]))
        for i, item in enumerate([it for _, items in spreads for it in items]):
            where[item] = (("pass", host), i)

    def plan_swap(host, items):
        rides.put(host, ("swap", host), lambda: _SiblingSwap([ex["bufs"][it] for it in items]))

    def plan_send(host, items):
        rides.put(host, ("send", host), lambda: _ChipSend([ex["pair"][it] for it in items]))
        for i, item in enumerate(items):
            sent[item] = (("send", host), i)

    for l in range(depth):
        t, nxt = f"_l{l}", f"_l{l + 1}"
        if l == 0:
            plan_pass(f"attn_fwd_d16{t}", [plan_spread(f"in_proj{t}", [("w_out", 0)]),
                                          plan_spread(f"attn_fwd_d1{t}", [("w_mlp_in", 0)]),
                                          plan_spread(f"attn_fwd_d4{t}", [("w_mlp_out", 0)])])
        else:
            plan_pass(f"attn_fwd_d1{t}", [plan_spread(f"in_proj{t}", [("w_mlp_out", l)])])
        if l + 1 < depth:
            plan_pass(f"pass_weights{nxt}", [plan_spread(f"out_proj{t}", [("w_out", l + 1)]),
                                             plan_spread(f"mlp_in{t}", [("w_in", l + 1)]),
                                             plan_spread(f"mlp_out{t}", [("w_mlp_in", l + 1)])])
        plan_swap(f"mlp_in_dx{t}", [("w_mlp_out", l), ("w_mlp_in", l)])
        plan_send(f"attn_bwd_d1{t}", [("w_mlp_out", l)])
        plan_send(f"attn_bwd_d4{t}", [("w_mlp_in", l)])
        plan_swap(f"in_proj_dx{t}", [("w_out", l), ("w_in", l)])
        plan_send(f"mlp_out_dx_l{l - 1}" if l > 0 else "send_last_grads", [("w_out", l), ("w_in", l)])

    g_in0, g_cw = _all_gather([shard("w_in", 0), conv_w], "gather_first")
    full_cw = g_cw.transpose(1, 2, 0, 3).reshape(depth, SSD_CONV, conv_ch)
    proj_cols = -(-in_proj // LANES) * LANES
    full = {}

    def weight(n, l):
        if (n, l) not in full:
            if (n, l) == ("w_in", 0):
                g = g_in0
            else:
                key, i = where[(n, l)]
                g = rides.done[key][i]
            if n == "w_in":
                g = _pad_lanes(g.transpose(1, 0, 2).reshape(d_model, in_proj), proj_cols)
            elif n == "w_mlp_in":
                g = g.transpose(1, 0, 2).reshape(d_model, -1)
            else:
                g = g.reshape(-1, d_model)
            full[(n, l)] = g
        return full[(n, l)]

    layers = [dict(ln1_g=ln1_g[l][None, :], ln2_g=ln2_g[l][None, :], conv_w=full_cw[l], conv_b=conv_b[l][None, :],
                   dt_bias=dt_bias[l], a_log=a_log[l], d_skip=d_skip[l], attn_norm_g=attn_norm_g[l],
                   ssd_norm_g=ssd_norm_g[l]) for l in range(depth)]

    h = x[0]
    saved = []
    for l in range(depth):
        h, s = _layer_fwd(h, layers[l], functools.partial(lambda n, l: weight(n, l), l=l), dims, f"_l{l}", rides)
        saved.append(s)
        if l + 1 < depth:
            _alone(rides, f"pass_weights_l{l + 1}")
    dh, d_final_g, loss_part, dh_b = _loss_head(h, final_norm_g[None, :], loss_target[0], "loss_head")

    grads = [None] * depth
    for l in reversed(range(depth)):
        dh, dh_b, grads[l] = _layer_bwd(dh, dh_b, layers[l], functools.partial(lambda n, l: weight(n, l), l=l),
                                        saved[l], dims, l, ex, copy_dx0=l > 0)
    grad_x = dh[None]
    _alone(rides, "send_last_grads")

    my_chip = 2 * lax.axis_index("x") + lax.axis_index("y")
    gsum = {}
    for n in names:
        per_layer = []
        for l in range(depth):
            key, i = sent[(n, l)]
            own = lax.dynamic_index_in_dim(ex["pair"][(n, l)], my_chip, axis=0, keepdims=False)
            per_layer.append(_sum_leading([rides.done[key][i]], f"sum_{n}_l{l}", first=own))
        gsum[n] = jnp.stack(per_layer)

    small_parts = [jnp.stack([grads[l][n] for l in range(depth)]).reshape(-1) for n in _SMALL]
    small_parts += [d_final_g.reshape(-1), loss_part[0, :1]]
    sizes = [int(a.shape[0]) for a in small_parts]
    packed = _to_rows(jnp.concatenate(small_parts))
    (gathered,) = _all_gather([packed], "gather_small_grads")
    total = _sum_leading([gathered], "sum_small_grads").reshape(-1)
    offs = np.cumsum([0] + sizes)
    pieces = [total[offs[i]:offs[i + 1]] for i in range(len(sizes))]
    for n, piece in zip(_SMALL, pieces):
        shape = (depth, SSD_CONV, conv_ch) if n == "conv_w" else w[n].shape
        gsum[n] = piece.reshape(shape)
    gsum["final_norm_g"] = pieces[len(_SMALL)]
    loss = pieces[len(_SMALL) + 1][0]
    my_id = 4 * lax.axis_index("x") + 2 * lax.axis_index("y") + lax.axis_index("c")
    cw = conv_w.shape[2]
    gsum["conv_w"] = lax.dynamic_slice_in_dim(gsum["conv_w"], my_id * cw, cw, axis=2)

    delta, new_m, new_v = {}, {}, {}
    for n in names:
        outs = _adamw(*(a.reshape(-1, w[n].shape[-1]) for a in (w[n], gsum[n], mom[n], var[n])), f"adamw_{n}")
        delta[n], new_m[n], new_v[n] = (o.reshape(w[n].shape) for o in outs)
    small_names = [n for n in _WEIGHTS if n not in names]
    sm_sizes = [int(np.prod(w[n].shape)) for n in small_names]
    pack = lambda d: _to_rows(jnp.concatenate([d[n].reshape(-1) for n in small_names]))
    outs = _adamw(pack(w), pack(gsum), pack(mom), pack(var), "adamw_small")
    sm_offs = np.cumsum([0] + sm_sizes)
    for res, o in zip((delta, new_m, new_v), outs):
        flat = o.reshape(-1)
        for i, n in enumerate(small_names):
            res[n] = flat[sm_offs[i]:sm_offs[i + 1]].reshape(w[n].shape)

    return (loss, grad_x, *[gsum[n] for n in _WEIGHTS], *[delta[n] for n in _WEIGHTS],
            *[new_m[n] for n in _WEIGHTS], *[new_v[n] for n in _WEIGHTS])
```

```python
import functools
import math

import numpy as np
import jax
import jax.numpy as jnp
from jax import lax
from jax.experimental import pallas as pl
from jax.experimental.pallas import tpu as pltpu

F32 = jnp.float32
BF16 = jnp.bfloat16

N_DEV = 8
LANES = 128
HEAD_DIM = 64
ATTN_BLOCK = 128
BRANCH_DILATIONS = (1, 4, 16)
SSD_GROUPS = 2
SSD_STATE = 128
SSD_CHUNK = 128
SSD_CONV = 4
NORM_EPS = 1e-5
ADAM_LR, ADAM_B1, ADAM_B2, ADAM_EPS, ADAM_WD, ADAM_STEP = 0.001, 0.9, 0.999, 1e-08, 0.01, 10
VMEM_LIMIT_BYTES = 56 * 1024 * 1024
MESH = pl.DeviceIdType.MESH
NEG_INF = float("-inf")


def _params(*sem):
    return pltpu.CompilerParams(dimension_semantics=tuple(sem), vmem_limit_bytes=VMEM_LIMIT_BYTES)


def _pick(n, target, mult):
    best = None
    for t in range(mult, min(n, target) + 1, mult):
        if n % t == 0:
            best = t
    assert best is not None, (n, target, mult)
    return best


def _dot(a, b, ca, cb):
    return lax.dot_general(a, b, (((ca,), (cb,)), ((), ())), preferred_element_type=F32)


def _split3(v):
    hi = v.astype(BF16)
    r = v - hi.astype(F32)
    mid = r.astype(BF16)
    lo = (r - mid.astype(F32)).astype(BF16)
    return hi, mid, lo


def _dot_exact(v, sel, ca, cb):
    hi, mid, lo = _split3(v)
    return _dot(hi, sel, ca, cb) + _dot(mid, sel, ca, cb) + _dot(lo, sel, ca, cb)


_HBM = pl.BlockSpec(memory_space=pltpu.HBM)


def _all_gather(xs, name):
    n = len(xs)

    def body(*refs):
        x_refs, o_refs = refs[:n], refs[n:2 * n]
        send_sems, recv_sems = refs[2 * n:]
        x, y, c = lax.axis_index("x"), lax.axis_index("y"), lax.axis_index("c")
        me, sibling = (x, y, c), (x, y, 1 - c)
        chips = [(1 - x, y), (x, 1 - y), (1 - x, 1 - y)]

        def copy(t, k, block, to, src=None):
            bx, by, bc = block
            dst = o_refs[t].at[4 * bx + 2 * by + bc]
            return pltpu.make_async_remote_copy(
                src_ref=dst if src is None else src, dst_ref=dst,
                send_sem=send_sems.at[t, k], recv_sem=recv_sems.at[t, k],
                device_id=to, device_id_type=MESH)

        first, passed = [], []
        for t in range(n):
            cps = [copy(t, 0, me, sibling, src=x_refs[t])]
            cps += [copy(t, 1 + j, me, (*chip, c), src=x_refs[t]) for j, chip in enumerate(chips)]
            for cp in cps:
                cp.start()
            first += cps
        for t in range(n):
            for j, chip in enumerate(chips):
                copy(t, 1 + j, (*chip, c), me).wait_recv()
                fwd = copy(t, 4 + j, (*chip, c), sibling)
                fwd.start()
                passed.append(fwd)
        for t in range(n):
            copy(t, 0, sibling, me).wait_recv()
            for j, chip in enumerate(chips):
                copy(t, 4 + j, (*chip, 1 - c), me).wait_recv()
        for cp in first + passed:
            cp.wait_send()

    return pl.pallas_call(
        body, name=name,
        out_shape=[jax.ShapeDtypeStruct((N_DEV,) + a.shape, a.dtype) for a in xs],
        in_specs=[_HBM] * n, out_specs=[_HBM] * n,
        scratch_shapes=[pltpu.SemaphoreType.DMA((n, 7)), pltpu.SemaphoreType.DMA((n, 7))],
    )(*xs)


def _with_own(gathered, own):
    me = 4 * lax.axis_index("x") + 2 * lax.axis_index("y") + lax.axis_index("c")
    return lax.dynamic_update_index_in_dim(gathered, own, me, 0)


def _place():
    x, y, c = lax.axis_index("x"), lax.axis_index("y"), lax.axis_index("c")
    return x, y, c, 4 * x + 2 * y + c, (x, y, 1 - c), [(1 - x, y), (x, 1 - y), (1 - x, 1 - y)]


def _remote(src, dst, send_sem, recv_sem, to):
    return pltpu.make_async_remote_copy(src_ref=src, dst_ref=dst, send_sem=send_sem, recv_sem=recv_sem,
                                        device_id=to, device_id_type=MESH)


class _Riding:
    aliases = {}

    def copies(self, ins, outs, sems):
        raise NotImplementedError

    def start(self, ins, outs, sems):
        local, out, _ = self.copies(ins, outs, sems)
        for cp in local + out:
            cp.start()

    def wait(self, ins, outs, sems):
        local, out, landing = self.copies(ins, outs, sems)
        for cp in landing:
            cp.wait_recv()
        for cp in out:
            cp.wait_send()
        for cp in local:
            cp.wait()


def _rows_of(ref, rows):
    return ref if rows is None else ref.at[pl.ds(rows[0], rows[1])]


class _GatherSpread(_Riding):
    def __init__(self, xs, rows=None, into=None):
        n = len(xs)
        self.rows = rows
        self.ins = list(xs) + list(into or [])
        self.out_shapes = [jax.ShapeDtypeStruct((N_DEV,) + a.shape, a.dtype) for a in xs]
        self.aliases = {n + t: t for t in range(n)} if into else {}
        self.sem_shapes = [pltpu.SemaphoreType.DMA((n, 4)), pltpu.SemaphoreType.DMA((n, 4))]

    def copies(self, ins, outs, sems):
        send, recv = sems
        _, _, c, me, sibling, chips = _place()
        targets = [sibling] + [(*chip, c) for chip in chips]
        out, landing = [], []
        for t in range(len(outs)):
            src = _rows_of(ins[t], self.rows)
            for k, to in enumerate(targets):
                out.append(_remote(src, _rows_of(outs[t].at[me], self.rows), send.at[t, k], recv.at[t, k], to))
                theirs = _rows_of(outs[t].at[4 * to[0] + 2 * to[1] + to[2]], self.rows)
                landing.append(_remote(src, theirs, send.at[t, k], recv.at[t, k], to))
        return [], out, landing


class _GatherPass(_Riding):
    def __init__(self, bufs):
        n = len(bufs)
        self.ins = list(bufs)
        self.out_shapes = [jax.ShapeDtypeStruct(b.shape, b.dtype) for b in bufs]
        self.aliases = {t: t for t in range(n)}
        self.sem_shapes = [pltpu.SemaphoreType.DMA((n, 3)), pltpu.SemaphoreType.DMA((n, 3))]

    def copies(self, ins, outs, sems):
        send, recv = sems
        _, _, c, _, sibling, chips = _place()
        out, landing = [], []
        for t in range(len(outs)):
            for j, (px, py) in enumerate(chips):
                got = outs[t].at[4 * px + 2 * py + c]
                out.append(_remote(got, got, send.at[t, j], recv.at[t, j], sibling))
                landing.append(_remote(got, outs[t].at[4 * px + 2 * py + 1 - c], send.at[t, j], recv.at[t, j], sibling))
        return [], out, landing


class _SiblingSwap(_Riding):
    def __init__(self, xs):
        n = len(xs)
        self.ins = list(xs)
        self.out_shapes = [jax.ShapeDtypeStruct((N_DEV // 2,) + a.shape[1:], a.dtype) for a in xs]
        self.sem_shapes = [pltpu.SemaphoreType.DMA((n, 4)), pltpu.SemaphoreType.DMA((n, 4))]

    def copies(self, ins, outs, sems):
        send, recv = sems
        _, _, c, _, sibling, _ = _place()
        out = [_remote(ins[t].at[2 * q + 1 - c], outs[t].at[q], send.at[t, q], recv.at[t, q], sibling)
               for t in range(len(ins)) for q in range(N_DEV // 2)]
        return [], out, out


class _ChipSend(_Riding):
    def __init__(self, ps, rows=None, into=None):
        n = len(ps)
        self.rows = rows
        self.ins = list(ps) + list(into or [])
        self.out_shapes = [jax.ShapeDtypeStruct((3,) + a.shape[1:], a.dtype) for a in ps]
        self.aliases = {n + t: t for t in range(n)} if into else {}
        self.sem_shapes = [pltpu.SemaphoreType.DMA((n, 3)), pltpu.SemaphoreType.DMA((n, 3))]

    def copies(self, ins, outs, sems):
        send, recv = sems
        _, _, c, _, _, chips = _place()
        out = [_remote(_rows_of(ins[t].at[2 * px + py], self.rows), _rows_of(outs[t].at[j], self.rows),
                       send.at[t, j], recv.at[t, j], (px, py, c))
               for t in range(len(outs)) for j, (px, py) in enumerate(chips)]
        return [], out, out


class _Bundle(_Riding):
    def __init__(self, comms):
        self.comms = comms
        self.ins = [a for cm in comms for a in cm.ins]
        self.out_shapes = [s for cm in comms for s in cm.out_shapes]
        self.sem_shapes = [s for cm in comms for s in cm.sem_shapes]
        self.aliases = {}
        i0 = o0 = 0
        for cm in comms:
            self.aliases.update({i0 + i: o0 + j for i, j in cm.aliases.items()})
            i0, o0 = i0 + len(cm.ins), o0 + len(cm.out_shapes)

    def copies(self, ins, outs, sems):
        local, out, landing = [], [], []
        i0 = o0 = s0 = 0
        for cm in self.comms:
            i1, o1, s1 = i0 + len(cm.ins), o0 + len(cm.out_shapes), s0 + len(cm.sem_shapes)
            a, b, c = cm.copies(ins[i0:i1], outs[o0:o1], sems[s0:s1])
            local, out, landing = local + a, out + b, landing + c
            i0, o0, s0 = i1, o1, s1
        return local, out, landing


class _Rides:
    def __init__(self):
        self.plan, self.done, self.aboard = {}, {}, {}

    def put(self, host, key, make):
        self.plan.setdefault(host, []).append((key, make))

    def board(self, host):
        if host not in self.plan:
            return None
        self.aboard[host] = [make() for _, make in self.plan[host]]
        return _Bundle(self.aboard[host])

    def land(self, host, results):
        o0 = 0
        for (key, _), cm in zip(self.plan[host], self.aboard[host]):
            self.done[key] = list(results[o0:o0 + len(cm.out_shapes)])
            o0 += len(cm.out_shapes)


def _pallas(body, *, name, grid, out_shape, in_specs, out_specs, operands, semantics, scratch_shapes=(), rides=None):
    comm = rides.board(name) if rides is not None else None
    if comm is None:
        return pl.pallas_call(
            body, name=name, grid=grid, out_shape=list(out_shape), in_specs=list(in_specs),
            out_specs=list(out_specs), scratch_shapes=list(scratch_shapes), compiler_params=_params(*semantics),
        )(*operands)
    n_in, n_out, n_scr = len(in_specs), len(out_shape), len(scratch_shapes)
    n_ci, n_co = len(comm.ins), len(comm.out_shapes)

    def hosted(*refs):
        cuts = np.cumsum([0, n_in, n_ci, n_out, n_co, n_scr])
        ins, c_ins, outs, c_outs, scr = (refs[cuts[i]:cuts[i + 1]] for i in range(5))
        sems = refs[cuts[5]:]
        ids = [pl.program_id(a) for a in range(len(grid))]
        first = functools.reduce(jnp.logical_and, [i == 0 for i in ids])
        last = functools.reduce(jnp.logical_and, [i == g - 1 for i, g in zip(ids, grid)])

        @pl.when(first)
        def _():
            comm.start(c_ins, c_outs, sems)

        body(*ins, *outs, *scr)

        @pl.when(last)
        def _():
            comm.wait(c_ins, c_outs, sems)

    results = pl.pallas_call(
        hosted, name=name, grid=grid, out_shape=list(out_shape) + comm.out_shapes,
        in_specs=list(in_specs) + [_HBM] * n_ci, out_specs=list(out_specs) + [_HBM] * n_co,
        scratch_shapes=list(scratch_shapes) + comm.sem_shapes,
        input_output_aliases={n_in + i: n_out + j for i, j in comm.aliases.items()},
        compiler_params=_params(*["arbitrary"] * len(grid)),
    )(*operands, *comm.ins)
    rides.land(name, results[n_out:])
    return results[:n_out]


def _alone(rides, name):
    comm = rides.board(name)

    def body(*refs):
        n_ci, n_co = len(comm.ins), len(comm.out_shapes)
        ins, outs, sems = refs[:n_ci], refs[n_ci:n_ci + n_co], refs[n_ci + n_co:]
        comm.start(ins, outs, sems)
        comm.wait(ins, outs, sems)

    results = pl.pallas_call(
        body, name=name, out_shape=comm.out_shapes, in_specs=[_HBM] * len(comm.ins),
        out_specs=[_HBM] * len(comm.out_shapes), scratch_shapes=comm.sem_shapes,
        input_output_aliases=dict(comm.aliases),
    )(*comm.ins)
    rides.land(name, results)


def _row_tile(rows, cols, itemsize, copies, budget=24 * 1024 * 1024):
    padded = -(-cols // LANES) * LANES
    mult = 8 * (4 // itemsize)
    if rows % mult:
        return rows
    return _pick(rows, max(mult, budget // (copies * padded * itemsize)), mult)


def _sum_leading(x, name):
    n_src, rows, cols = x.shape
    tr = _row_tile(rows, cols, 4, 2 * (n_src + 2))

    def body(x_ref, o_ref):
        acc = x_ref[0].astype(F32)
        for s in range(1, n_src):
            acc = acc + x_ref[s].astype(F32)
        o_ref[...] = acc

    return pl.pallas_call(
        body, name=name, grid=(rows // tr,), out_shape=jax.ShapeDtypeStruct((rows, cols), F32),
        in_specs=[pl.BlockSpec((n_src, tr, cols), lambda i: (0, i, 0))],
        out_specs=pl.BlockSpec((tr, cols), lambda i: (i, 0)), compiler_params=_params("parallel"),
    )(x)


def _pair_sum(buf, theirs, core, name):
    n_q, rows, cols = theirs.shape
    tr = _row_tile(rows, cols, 4, 6)

    def body(core_ref, mine_ref, theirs_ref, o_ref):
        o_ref[...] = (mine_ref[...].astype(F32) + theirs_ref[...].astype(F32)).astype(BF16)

    spec = pl.BlockSpec((None, tr, cols), lambda q, i, core_ref: (q, i, 0))
    return pl.pallas_call(
        body, name=name, out_shape=jax.ShapeDtypeStruct(theirs.shape, BF16),
        grid_spec=pltpu.PrefetchScalarGridSpec(
            num_scalar_prefetch=1, grid=(n_q, rows // tr),
            in_specs=[pl.BlockSpec((None, tr, cols), lambda q, i, core_ref: (2 * q + core_ref[0], i, 0)), spec],
            out_specs=spec),
        compiler_params=_params("parallel", "parallel"),
    )(core, buf, theirs)


def _sum_with_own(recv, pair, chip, name):
    n_src, rows, cols = recv.shape
    tr = _row_tile(rows, cols, 4, 2 * (n_src + 3))

    def body(chip_ref, own_ref, recv_ref, o_ref):
        acc = own_ref[...].astype(F32)
        for s in range(n_src):
            acc = acc + recv_ref[s].astype(F32)
        o_ref[...] = acc

    return pl.pallas_call(
        body, name=name, out_shape=jax.ShapeDtypeStruct((rows, cols), F32),
        grid_spec=pltpu.PrefetchScalarGridSpec(
            num_scalar_prefetch=1, grid=(rows // tr,),
            in_specs=[pl.BlockSpec((None, tr, cols), lambda i, chip_ref: (chip_ref[0], i, 0)),
                      pl.BlockSpec((n_src, tr, cols), lambda i, chip_ref: (0, i, 0))],
            out_specs=pl.BlockSpec((tr, cols), lambda i, chip_ref: (i, 0))),
        compiler_params=_params("parallel"),
    )(chip, pair, recv)


def _mm(a, b, *, name, ta=False, tb=False, tm=1024, tn=512, out_dtype=F32, a_act=None,
        residual=None, gate=None, out_chunk=None, rides=None):
    k_dim, m = (a.shape if ta else a.shape[::-1])
    n, kb = (b.shape if tb else b.shape[::-1])
    assert kb == k_dim, (a.shape, b.shape, ta, tb)
    tm, tn = _pick(m, tm, 128), _pick(out_chunk or n, tn, 128)
    ca, cb = (0 if ta else 1), (1 if tb else 0)
    a_spec = pl.BlockSpec((k_dim, tm), lambda i, j: (0, i)) if ta else pl.BlockSpec((tm, k_dim), lambda i, j: (i, 0))
    b_spec = pl.BlockSpec((tn, k_dim), lambda i, j: (j, 0)) if tb else pl.BlockSpec((k_dim, tn), lambda i, j: (0, j))
    mn_spec = pl.BlockSpec((tm, tn), lambda i, j: (i, j))
    if out_chunk:
        per = out_chunk // tn
        o_spec = pl.BlockSpec((None, tm, tn), lambda i, j: (j // per, i, j % per))
        out_shape = jax.ShapeDtypeStruct((n // out_chunk, m, out_chunk), out_dtype)
    else:
        o_spec = mn_spec
        out_shape = jax.ShapeDtypeStruct((m, n), out_dtype)
    operands, in_specs = [a, b], [a_spec, b_spec]
    for extra in (gate, residual):
        if extra is not None:
            operands.append(extra)
            in_specs.append(mn_spec)

    def body(*refs):
        a_ref, b_ref, o_ref = refs[0], refs[1], refs[-1]
        extras = list(refs[2:-1])
        gate_ref = extras.pop(0) if gate is not None else None
        res_ref = extras.pop(0) if residual is not None else None
        av = a_ref[...].astype(BF16)
        if a_act == "relu2":
            av = jnp.square(jnp.maximum(av, jnp.zeros_like(av)))
        r = _dot(av, b_ref[...].astype(BF16), ca, cb)
        if gate_ref is not None:
            r = r * (2.0 * jnp.maximum(gate_ref[...].astype(F32), 0.0))
        if res_ref is not None:
            r = r + res_ref[...].astype(F32)
        o_ref[...] = r.astype(out_dtype)

    return _pallas(body, name=name, grid=(m // tm, n // tn), out_shape=[out_shape], in_specs=in_specs,
                   out_specs=[o_spec], operands=operands, semantics=("parallel", "arbitrary"), rides=rides)[0]


def _rmsnorm_fwd(xs, seg_widths, g, name, tm=256):
    t_len = xs[0].shape[0]
    width = sum(x.shape[1] for x in xs)
    tm = _pick(t_len, tm, 16)
    n = len(xs)

    def body(*refs):
        x_refs, g_ref, o_ref = refs[:n], refs[n], refs[n + 1]
        col = 0
        for x_ref, widths in zip(x_refs, seg_widths):
            off = 0
            for w in widths:
                xv = x_ref[:, off:off + w].astype(F32)
                r = lax.rsqrt(jnp.mean(xv * xv, axis=1, keepdims=True) + NORM_EPS)
                o_ref[:, col:col + w] = (xv * r * g_ref[:, col:col + w]).astype(BF16)
                off += w
                col += w

    return pl.pallas_call(
        body, name=name, grid=(t_len // tm,),
        out_shape=jax.ShapeDtypeStruct((t_len, width), BF16),
        in_specs=[pl.BlockSpec((tm, x.shape[1]), lambda i: (i, 0)) for x in xs]
        + [pl.BlockSpec((1, width), lambda i: (0, 0))],
        out_specs=pl.BlockSpec((tm, width), lambda i: (i, 0)),
        compiler_params=_params("parallel"),
    )(*xs, g)


def _rmsnorm_bwd(xs, seg_widths, g, dh, residuals, name, tm=256, bf16_copy=False):
    t_len = xs[0].shape[0]
    width = sum(x.shape[1] for x in xs)
    tm = _pick(t_len, tm, 8)
    n = len(xs)
    has_res = [r is not None for r in residuals]
    res_ops = [r for r in residuals if r is not None]

    def body(*refs):
        x_refs, g_ref, dh_ref = refs[:n], refs[n], refs[n + 1]
        res_refs = list(refs[n + 2:n + 2 + len(res_ops)])
        dx_refs = refs[n + 2 + len(res_ops):n + 2 + len(res_ops) + n]
        dg_ref = refs[n + 2 + len(res_ops) + n]
        copy_refs = refs[n + 3 + len(res_ops) + n:]
        first = pl.program_id(0) == 0
        col = 0
        for idx, (x_ref, widths) in enumerate(zip(x_refs, seg_widths)):
            res_ref = res_refs.pop(0) if has_res[idx] else None
            off = 0
            for w in widths:
                xv = x_ref[:, off:off + w].astype(F32)
                r = lax.rsqrt(jnp.mean(xv * xv, axis=1, keepdims=True) + NORM_EPS)
                xh = xv * r
                dhv = dh_ref[:, col:col + w].astype(F32)
                gd = dhv * g_ref[:, col:col + w]
                dx = r * (gd - xh * jnp.mean(gd * xh, axis=1, keepdims=True))
                if res_ref is not None:
                    dx = dx + res_ref[:, off:off + w]
                dx_refs[idx][:, off:off + w] = dx
                if bf16_copy:
                    copy_refs[idx][:, off:off + w] = dx.astype(BF16)
                part = jnp.sum(dhv * xh, axis=0, keepdims=True)

                @pl.when(first)
                def _(part=part, col=col, w=w):
                    dg_ref[:, col:col + w] = part

                @pl.when(jnp.logical_not(first))
                def _(part=part, col=col, w=w):
                    dg_ref[:, col:col + w] += part
                off += w
                col += w

    outs = pl.pallas_call(
        body, name=name, grid=(t_len // tm,),
        out_shape=[jax.ShapeDtypeStruct(x.shape, F32) for x in xs] + [jax.ShapeDtypeStruct((1, width), F32)]
        + ([jax.ShapeDtypeStruct(x.shape, BF16) for x in xs] if bf16_copy else []),
        in_specs=[pl.BlockSpec((tm, x.shape[1]), lambda i: (i, 0)) for x in xs]
        + [pl.BlockSpec((1, width), lambda i: (0, 0)), pl.BlockSpec((tm, width), lambda i: (i, 0))]
        + [pl.BlockSpec((tm, r.shape[1]), lambda i: (i, 0)) for r in res_ops],
        out_specs=[pl.BlockSpec((tm, x.shape[1]), lambda i: (i, 0)) for x in xs]
        + [pl.BlockSpec((1, width), lambda i: (0, 0))]
        + ([pl.BlockSpec((tm, x.shape[1]), lambda i: (i, 0)) for x in xs] if bf16_copy else []),
        compiler_params=_params("arbitrary"),
    )(*xs, g, dh, *res_ops)
    return outs[:n], outs[n], outs[n + 1:]


def _loss_head(x, g, target, name, tm=256):
    t_len, d = x.shape
    tm = _pick(t_len, tm, 8)

    def body(x_ref, g_ref, t_ref, dx_ref, dg_ref, loss_ref, dxb_ref):
        first = pl.program_id(0) == 0
        xv = x_ref[...]
        r = lax.rsqrt(jnp.mean(xv * xv, axis=1, keepdims=True) + NORM_EPS)
        xh = xv * r
        gv = g_ref[...]
        err = xh * gv - t_ref[...]
        part_loss = 0.5 * jnp.sum(jnp.mean(err * err, axis=1, keepdims=True), axis=0, keepdims=True)
        dy = err * (1.0 / d)
        gd = dy * gv
        dx = r * (gd - xh * jnp.mean(gd * xh, axis=1, keepdims=True))
        dx_ref[...] = dx
        dxb_ref[...] = dx.astype(BF16)
        part_g = jnp.sum(dy * xh, axis=0, keepdims=True)
        part_loss = jnp.broadcast_to(part_loss, (1, LANES))

        @pl.when(first)
        def _():
            dg_ref[...] = part_g
            loss_ref[...] = part_loss

        @pl.when(jnp.logical_not(first))
        def _():
            dg_ref[...] += part_g
            loss_ref[...] += part_loss

    return pl.pallas_call(
        body, name=name, grid=(t_len // tm,),
        out_shape=[jax.ShapeDtypeStruct((t_len, d), F32), jax.ShapeDtypeStruct((1, d), F32),
                   jax.ShapeDtypeStruct((1, LANES), F32), jax.ShapeDtypeStruct((t_len, d), BF16)],
        in_specs=[pl.BlockSpec((tm, d), lambda i: (i, 0)), pl.BlockSpec((1, d), lambda i: (0, 0)),
                  pl.BlockSpec((tm, d), lambda i: (i, 0))],
        out_specs=[pl.BlockSpec((tm, d), lambda i: (i, 0)), pl.BlockSpec((1, d), lambda i: (0, 0)),
                   pl.BlockSpec((1, LANES), lambda i: (0, 0)), pl.BlockSpec((tm, d), lambda i: (i, 0))],
        compiler_params=_params("arbitrary"),
    )(x, g, target)


def _alibi_slope(h, n_heads):
    return jnp.exp(jnp.full((1, 1), -8.0 * math.log(2.0) / n_heads, F32) * (h + 1).astype(F32))


def _attn_tiles(d, w):
    return ATTN_BLOCK * d, (w if d == 1 else LANES)


def _residue_rows(r, d):
    return pl.ds(r, ATTN_BLOCK, stride=d) if d > 1 else pl.ds(0, ATTN_BLOCK)


def _attn_masks(first_block):
    i = lax.broadcasted_iota(jnp.int32, (2 * ATTN_BLOCK, 2 * ATTN_BLOCK), 0) % ATTN_BLOCK
    j = lax.broadcasted_iota(jnp.int32, (2 * ATTN_BLOCK, 2 * ATTN_BLOCK), 1)
    delta = i - j + ATTN_BLOCK
    valid = jnp.logical_and(delta >= 0, delta <= ATTN_BLOCK)
    valid = jnp.logical_and(valid, jnp.logical_or(j >= ATTN_BLOCK, jnp.logical_not(first_block)))
    return valid, delta.astype(F32)


def _stack_heads(x, masks):
    zero = jnp.zeros_like(x)
    return jnp.concatenate([jnp.where(masks[0], x, zero), jnp.where(masks[1], x, zero)], axis=0)


def _unstack_heads(x2, masks):
    return jnp.where(masks[0], x2[:ATTN_BLOCK], x2[ATTN_BLOCK:])


def _pair_slopes(first_head, p, n_heads, d):
    row = lax.broadcasted_iota(jnp.int32, (2 * ATTN_BLOCK, 1), 0)
    sa, sb = (_alibi_slope(first_head + 2 * p + hh, n_heads) * d for hh in range(2))
    return jnp.where(row < ATTN_BLOCK, sa, sb)


def _head_lane_masks():
    lane = lax.broadcasted_iota(jnp.int32, (ATTN_BLOCK, LANES), 1)
    return [lane < HEAD_DIM, lane >= HEAD_DIM]


def _attn_branch_fwd(proj, w, dilation, n_heads, name, rides=None):
    t_len = proj.shape[0]
    d = dilation
    rows, lw = _attn_tiles(d, w)
    nb = t_len // rows
    n_pairs = lw // LANES
    per = w // lw
    scale = HEAD_DIM ** -0.5

    def body(q_ref, kp_ref, kc_ref, vp_ref, vc_ref, o_ref, lse_ref):
        first_head = pl.program_id(0) * (2 * n_pairs)
        first_block = pl.program_id(1) == 0
        valid, delta = _attn_masks(first_block)
        masks = _head_lane_masks()
        ones = jnp.ones((2 * ATTN_BLOCK, LANES), BF16)
        for p in range(n_pairs):
            cols = pl.ds(p * LANES, LANES)
            bias = _pair_slopes(first_head, p, n_heads, d) * delta
            for r in range(d):
                rs = _residue_rows(r, d)
                q2 = _stack_heads((q_ref[rs, cols] * scale).astype(BF16), masks)
                k2 = jnp.concatenate([kp_ref[rs, cols], kc_ref[rs, cols]], axis=0).astype(BF16)
                v2 = jnp.concatenate([vp_ref[rs, cols], vc_ref[rs, cols]], axis=0).astype(BF16)
                s = jnp.where(valid, _dot(q2, k2, 1, 1) - bias, NEG_INF)
                m = jnp.max(s, axis=1, keepdims=True)
                pr = jnp.exp(s - m).astype(BF16)
                den = _dot(pr, ones, 1, 0)
                o_ref[rs, cols] = _unstack_heads(_dot(pr, v2, 1, 0) / den, masks)
                lse_ref[rs, cols] = _unstack_heads(m + jnp.log(den), masks)

    def spec(which, prev):
        if prev:
            return pl.BlockSpec((rows, lw), lambda b, n: (jnp.maximum(n - 1, 0), which * per + b))
        return pl.BlockSpec((rows, lw), lambda b, n: (n, which * per + b))

    o_spec = pl.BlockSpec((rows, lw), lambda b, n: (n, b))
    return _pallas(
        body, name=name, grid=(per, nb), out_shape=[jax.ShapeDtypeStruct((t_len, w), F32)] * 2,
        in_specs=[spec(0, False), spec(1, True), spec(1, False), spec(2, True), spec(2, False)],
        out_specs=[o_spec, o_spec], operands=[proj] * 5, semantics=("parallel", "parallel"), rides=rides)


def _attn_combine(outs, lses, name, tm=512):
    t_len, w = outs[0].shape
    tm = _pick(t_len, tm, 8)
    nbr = len(outs)

    def body(*refs):
        o_refs, l_refs = refs[:nbr], refs[nbr:2 * nbr]
        out_ref, lse_ref = refs[2 * nbr:]
        ls = [r[...] for r in l_refs]
        m = functools.reduce(jnp.maximum, ls)
        es = [jnp.exp(l - m) for l in ls]
        den = functools.reduce(lambda a, b: a + b, es)
        num = functools.reduce(lambda a, b: a + b, [e * r[...] for e, r in zip(es, o_refs)])
        out_ref[...] = num / den
        lse_ref[...] = m + jnp.log(den)

    spec = pl.BlockSpec((tm, w), lambda i: (i, 0))
    return pl.pallas_call(
        body, name=name, grid=(t_len // tm,),
        out_shape=[jax.ShapeDtypeStruct((t_len, w), F32)] * 2,
        in_specs=[spec] * (2 * nbr), out_specs=[spec, spec],
        compiler_params=_params("parallel"),
    )(*outs, *lses)


def _attn_branch_bwd(proj, w, out, lse, dout, dilation, n_heads, name, acc=None, rides=None):
    t_len = proj.shape[0]
    d = dilation
    rows, lw = _attn_tiles(d, w)
    nb = t_len // rows
    n_pairs = lw // LANES
    per = w // lw
    scale = HEAD_DIM ** -0.5
    n_acc = 0 if acc is None else 3

    def body(*refs):
        q_ref, kp_ref, kc_ref, vp_ref, vc_ref, out_ref, lse_ref, do_ref = refs[:8]
        acc_refs = refs[8:8 + n_acc]
        dq_ref, dk_ref, dv_ref, dk_carry, dv_carry = refs[8 + n_acc:]
        first_head = pl.program_id(0) * (2 * n_pairs)
        n = pl.program_id(1)
        first_block = n == 0
        valid, dist = _attn_masks(first_block)
        masks = _head_lane_masks()

        def plus(value, idx, *where):
            return value + acc_refs[idx][where] if n_acc else value

        @pl.when(first_block)
        def _():
            dk_carry[...] = jnp.zeros_like(dk_carry)
            dv_carry[...] = jnp.zeros_like(dv_carry)

        @pl.when(n < nb)
        def _():
            for p in range(n_pairs):
                cols = pl.ds(p * LANES, LANES)
                bias = _pair_slopes(first_head, p, n_heads, d) * dist
                for r in range(d):
                    rs = _residue_rows(r, d)
                    q2 = _stack_heads((q_ref[rs, cols] * scale).astype(BF16), masks)
                    k2 = jnp.concatenate([kp_ref[rs, cols], kc_ref[rs, cols]], axis=0).astype(BF16)
                    v2 = jnp.concatenate([vp_ref[rs, cols], vc_ref[rs, cols]], axis=0).astype(BF16)
                    do = do_ref[rs, cols]
                    do2 = _stack_heads(do.astype(BF16), masks)
                    do_out = do * out_ref[rs, cols]
                    lse_all = lse_ref[rs, cols]
                    delta = jnp.concatenate([jnp.sum(jnp.where(masks[hh], do_out, 0.0), axis=1, keepdims=True)
                                             for hh in range(2)], axis=0)
                    lse2 = jnp.concatenate([jnp.max(jnp.where(masks[hh], lse_all, NEG_INF), axis=1, keepdims=True)
                                            for hh in range(2)], axis=0)
                    s = jnp.where(valid, _dot(q2, k2, 1, 1) - bias, NEG_INF)
                    pr = jnp.exp(s - lse2)
                    ds = (pr * (_dot(do2, v2, 1, 1) - delta)).astype(BF16)
                    dq = _unstack_heads(_dot(ds, k2, 1, 0), masks)
                    dk2 = _dot(ds, q2, 0, 0)
                    dv2 = _dot(pr.astype(BF16), do2, 0, 0)
                    dq_ref[rs, cols] = plus(dq * scale, 0, rs, cols)
                    dk_ref[rs, cols] = plus(dk_carry[rs, cols] + dk2[:ATTN_BLOCK], 1, rs, cols)
                    dv_ref[rs, cols] = plus(dv_carry[rs, cols] + dv2[:ATTN_BLOCK], 2, rs, cols)
                    dk_carry[rs, cols] = dk2[ATTN_BLOCK:]
                    dv_carry[rs, cols] = dv2[ATTN_BLOCK:]

        @pl.when(n == nb)
        def _():
            dk_ref[...] = plus(dk_carry[...], 1, Ellipsis)
            dv_ref[...] = plus(dv_carry[...], 2, Ellipsis)

    def qkv_spec(which, shift):
        return pl.BlockSpec((rows, lw), lambda b, n: (jnp.clip(n - shift, 0, nb - 1), which * per + b))

    q_like = pl.BlockSpec((rows, lw), lambda b, n: (jnp.minimum(n, nb - 1), b))
    k_like = pl.BlockSpec((rows, lw), lambda b, n: (jnp.maximum(n - 1, 0), b))
    return _pallas(
        body, name=name, grid=(per, nb + 1), out_shape=[jax.ShapeDtypeStruct((t_len, w), F32)] * 3,
        in_specs=[qkv_spec(0, 0), qkv_spec(1, 1), qkv_spec(1, 0), qkv_spec(2, 1), qkv_spec(2, 0),
                  q_like, q_like, q_like] + [q_like, k_like, k_like][:n_acc],
        out_specs=[q_like, k_like, k_like], operands=[proj] * 5 + [out, lse, dout, *(acc or ())],
        scratch_shapes=[pltpu.VMEM((rows, lw), F32), pltpu.VMEM((rows, lw), F32)],
        semantics=("parallel", "arbitrary"), rides=rides)


def _shift_down(u, s):
    if s == 0:
        return u
    row = lax.broadcasted_iota(jnp.int32, u.shape, 0)
    return jnp.where(row >= s, pltpu.roll(u, s, 0), 0.0)


def _shift_up(u, s):
    if s == 0:
        return u
    n = u.shape[0]
    row = lax.broadcasted_iota(jnp.int32, u.shape, 0)
    return jnp.where(row < n - s, pltpu.roll(u, n - s, 0), 0.0)


def _conv_fwd(u, col0, w, b, name):
    t_len, ch = u.shape[0], w.shape[1]
    blk0 = col0 // LANES

    def body(u_ref, w_ref, b_ref, o_ref):
        uv = u_ref[...]
        pre = b_ref[...] + jnp.zeros_like(uv)
        for k in range(SSD_CONV):
            pre = pre + w_ref[k:k + 1, :] * _shift_down(uv, SSD_CONV - 1 - k)
        o_ref[...] = pre * jax.nn.sigmoid(pre)

    return pl.pallas_call(
        body, name=name, grid=(ch // LANES,),
        out_shape=jax.ShapeDtypeStruct((t_len, ch), F32),
        in_specs=[pl.BlockSpec((t_len, LANES), lambda j: (0, blk0 + j)),
                  pl.BlockSpec((SSD_CONV, LANES), lambda j: (0, j)), pl.BlockSpec((1, LANES), lambda j: (0, j))],
        out_specs=pl.BlockSpec((t_len, LANES), lambda j: (0, j)),
        compiler_params=_params("parallel"),
    )(u, w, b)


def _conv_bwd(u, col0, w, b, dact, name):
    t_len, ch = u.shape[0], w.shape[1]
    blk0 = col0 // LANES

    def body(u_ref, w_ref, b_ref, da_ref, du_ref, dw_ref, db_ref):
        uv = u_ref[...]
        shifted = [_shift_down(uv, SSD_CONV - 1 - k) for k in range(SSD_CONV)]
        pre = b_ref[...] + jnp.zeros_like(uv)
        for k in range(SSD_CONV):
            pre = pre + w_ref[k:k + 1, :] * shifted[k]
        sig = jax.nn.sigmoid(pre)
        dpre = da_ref[...] * (sig * (1.0 + pre * (1.0 - sig)))
        du = jnp.zeros_like(uv)
        for k in range(SSD_CONV):
            du = du + w_ref[k:k + 1, :] * _shift_up(dpre, SSD_CONV - 1 - k)
            dw_ref[k:k + 1, :] = jnp.sum(dpre * shifted[k], axis=0, keepdims=True)
        du_ref[...] = du
        db_ref[...] = jnp.sum(dpre, axis=0, keepdims=True)

    col = pl.BlockSpec((t_len, LANES), lambda j: (0, j))
    w_spec = pl.BlockSpec((SSD_CONV, LANES), lambda j: (0, j))
    b_spec = pl.BlockSpec((1, LANES), lambda j: (0, j))
    return pl.pallas_call(
        body, name=name, grid=(ch // LANES,),
        out_shape=[jax.ShapeDtypeStruct((t_len, ch), F32), jax.ShapeDtypeStruct((SSD_CONV, ch), F32),
                   jax.ShapeDtypeStruct((1, ch), F32)],
        in_specs=[pl.BlockSpec((t_len, LANES), lambda j: (0, blk0 + j)), w_spec, b_spec, col],
        out_specs=[col, w_spec, b_spec],
        compiler_params=_params("parallel"),
    )(u, w, b, dact)


def _cumsum_rows(v):
    n = v.shape[0]
    row = lax.broadcasted_iota(jnp.int32, v.shape, 0)
    s = 1
    while s < n:
        v = v + jnp.where(row >= s, pltpu.roll(v, s, 0), 0.0)
        s *= 2
    return v


def _rev_cumsum_rows(v):
    n = v.shape[0]
    row = lax.broadcasted_iota(jnp.int32, v.shape, 0)
    s = 1
    while s < n:
        v = v + jnp.where(row < n - s, pltpu.roll(v, n - s, 0), 0.0)
        s *= 2
    return v


def _head_selector(heads, width):
    j = lax.broadcasted_iota(jnp.int32, (LANES, width), 0)
    lane = lax.broadcasted_iota(jnp.int32, (LANES, width), 1)
    return jnp.where(jnp.logical_and(lane // HEAD_DIM == j, j < heads), 1.0, 0.0).astype(BF16)


class _SsdChunk:
    def __init__(self, dtraw_ref, bias_ref, alog_ref, xs_ref, b_ref, c_ref, heads):
        q = SSD_CHUNK
        width = heads * HEAD_DIM
        lane = lax.broadcasted_iota(jnp.int32, (q, LANES), 1)
        self.head_lanes = lane < heads
        lane1 = lax.broadcasted_iota(jnp.int32, (1, LANES), 1)
        self.a = jnp.where(lane1 < heads, -jnp.exp(alog_ref[...]), 0.0)
        self.dt_arg = dtraw_ref[...] + bias_ref[...]
        self.dt = jnp.where(self.head_lanes, jax.nn.softplus(self.dt_arg), 0.0)
        self.cum = _cumsum_rows(self.dt * self.a)
        self.cum_t = self.cum.T
        last = self.cum[q - 1:q, :]
        self.sel = _head_selector(heads, width)
        self.expand = lambda v: _dot_exact(v, self.sel, 1, 0)
        self.segsum = lambda v: _dot_exact(v, self.sel, 1, 1)
        self.e_exp = self.expand(jnp.exp(self.cum))
        self.d_exp = self.expand(jnp.exp(last - self.cum))
        self.elast_exp = self.e_exp[q - 1:q, :]
        self.dt_exp = self.expand(self.dt)
        self.xs = xs_ref[...]
        self.x = self.xs * self.dt_exp
        self.xb = self.x.astype(BF16)
        self.bb = b_ref[...].astype(BF16)
        self.cb = c_ref[...].astype(BF16)
        self.cbm = _dot(self.cb, self.bb, 1, 1)
        li = lax.broadcasted_iota(jnp.int32, (q, q), 0)
        si = lax.broadcasted_iota(jnp.int32, (q, q), 1)
        self.tri = li >= si
        hl = lax.broadcasted_iota(jnp.int32, (q, LANES), 1)
        self.pair_masks = [hl < HEAD_DIM, hl >= HEAD_DIM]

    def decay(self, j):
        diff = self.cum[:, j:j + 1] - self.cum_t[j:j + 1, :]
        return jnp.exp(jnp.where(self.tri, diff, NEG_INF))


def _ssd_specs(t_len, heads, n_chunks, xbc_cols, rev):
    q, gw = SSD_CHUNK, heads * HEAD_DIM
    ssd_w = SSD_GROUPS * gw
    b_blk = ssd_w // SSD_STATE
    ch = (lambda c: n_chunks - 1 - c) if rev else (lambda c: c)
    return dict(
        dtraw=pl.BlockSpec((None, q, LANES), lambda g, c: (g, ch(c), 0)),
        small=pl.BlockSpec((None, 1, LANES), lambda g, c: (g, 0, 0)),
        dsk=pl.BlockSpec((None, 1, gw), lambda g, c: (g, 0, 0)),
        xs=pl.BlockSpec((q, gw), lambda g, c: (ch(c), g)),
        b=pl.BlockSpec((q, SSD_STATE), lambda g, c: (ch(c), b_blk + g)),
        c=pl.BlockSpec((q, SSD_STATE), lambda g, c: (ch(c), b_blk + SSD_GROUPS + g)),
        z=pl.BlockSpec((q, gw), lambda g, c: (ch(c), 3 * SSD_GROUPS + g)),
        tok=pl.BlockSpec((q, gw), lambda g, c: (ch(c), g)),
        state=pl.BlockSpec((None, SSD_STATE, gw), lambda g, c: (ch(c), 0, g)),
        bc=pl.BlockSpec((q, SSD_STATE), lambda g, c: (ch(c), g)),
    )


def _ssd_fwd(xbc, qkvz, dtraw_g, bias_g, alog_g, dsk_exp, heads, name):
    t_len = xbc.shape[0]
    q, gw = SSD_CHUNK, heads * HEAD_DIM
    n_chunks = t_len // q
    ssd_w = SSD_GROUPS * gw
    sp = _ssd_specs(t_len, heads, n_chunks, xbc.shape[1], rev=False)

    def body(dtraw_ref, bias_ref, alog_ref, dsk_ref, xs_ref, b_ref, c_ref, z_ref,
             yg_ref, ypre_ref, st_ref, s_scr):
        @pl.when(pl.program_id(1) == 0)
        def _():
            s_scr[...] = jnp.zeros_like(s_scr)

        k = _SsdChunk(dtraw_ref, bias_ref, alog_ref, xs_ref, b_ref, c_ref, heads)
        s_prev = s_scr[...]
        st_ref[...] = s_prev
        y_off = k.e_exp * _dot(k.cb, s_prev.astype(BF16), 1, 0)
        parts = []
        for p in range(heads // 2):
            xp = k.xb[:, p * LANES:(p + 1) * LANES]
            acc = jnp.zeros((q, LANES), F32)
            for hh in range(2):
                m = (k.cbm * k.decay(2 * p + hh)).astype(BF16)
                acc = acc + _dot(m, jnp.where(k.pair_masks[hh], xp, jnp.zeros_like(xp)), 1, 0)
            parts.append(acc)
        y = jnp.concatenate(parts, axis=1) + y_off
        xd = (k.x * k.d_exp).astype(BF16)
        s_scr[...] = k.elast_exp * s_prev + _dot(k.bb, xd, 0, 0)
        y_pre = y + dsk_ref[...] * k.xs
        zv = z_ref[...]
        ypre_ref[...] = y_pre
        yg_ref[...] = y_pre * (zv * jax.nn.sigmoid(zv))

    return pl.pallas_call(
        body, name=name, grid=(SSD_GROUPS, n_chunks),
        out_shape=[jax.ShapeDtypeStruct((t_len, ssd_w), F32), jax.ShapeDtypeStruct((t_len, ssd_w), F32),
                   jax.ShapeDtypeStruct((n_chunks, SSD_STATE, ssd_w), F32)],
        in_specs=[sp["dtraw"], sp["small"], sp["small"], sp["dsk"], sp["xs"], sp["b"], sp["c"], sp["z"]],
        out_specs=[sp["tok"], sp["tok"], sp["state"]],
        scratch_shapes=[pltpu.VMEM((SSD_STATE, gw), F32)],
        compiler_params=_params("parallel", "arbitrary"),
    )(dtraw_g, bias_g, alog_g, dsk_exp, xbc, xbc, xbc, qkvz)


def _ssd_bwd(xbc, qkvz, dtraw_g, bias_g, alog_g, dsk_exp, ypre, states, dyg, heads, name):
    t_len = xbc.shape[0]
    q, gw = SSD_CHUNK, heads * HEAD_DIM
    n_chunks = t_len // q
    ssd_w = SSD_GROUPS * gw
    sp = _ssd_specs(t_len, heads, n_chunks, xbc.shape[1], rev=True)

    def body(dtraw_ref, bias_ref, alog_ref, dsk_ref, xs_ref, b_ref, c_ref, z_ref, ypre_ref, st_ref, dyg_ref,
             dxs_ref, db_ref, dc_ref, dz_ref, ddt_ref, small_ref, g_scr):
        first = pl.program_id(1) == 0

        @pl.when(first)
        def _():
            g_scr[...] = jnp.zeros_like(g_scr)

        k = _SsdChunk(dtraw_ref, bias_ref, alog_ref, xs_ref, b_ref, c_ref, heads)
        zv = z_ref[...]
        sig = jax.nn.sigmoid(zv)
        dyg = dyg_ref[...]
        y_pre = ypre_ref[...]
        dy = dyg * (zv * sig)
        dz_ref[...] = dyg * y_pre * (sig * (1.0 + zv * (1.0 - sig)))
        dsk = dsk_ref[...]
        g_next = g_scr[...]
        s_prev = st_ref[...]
        sb = s_prev.astype(BF16)
        xd = k.x * k.d_exp
        xdb = xd.astype(BF16)
        gb = g_next.astype(BF16)
        dx_off = k.d_exp * _dot(k.bb, gb, 1, 0)
        dyb = dy.astype(BF16)
        dcb = jnp.zeros((q, q), F32)
        lane = lax.broadcasted_iota(jnp.int32, (q, LANES), 1)
        row_t = lax.broadcasted_iota(jnp.int32, (LANES, q), 0)
        w_rows = jnp.zeros((q, LANES), F32)
        w_cols_t = jnp.zeros((LANES, q), F32)
        parts = []
        for p in range(heads // 2):
            cols = slice(p * LANES, (p + 1) * LANES)
            dyp, xp = dyb[:, cols], k.xb[:, cols]
            acc = jnp.zeros((q, LANES), F32)
            for hh in range(2):
                j = 2 * p + hh
                lm = k.decay(j)
                m32 = k.cbm * lm
                dym = jnp.where(k.pair_masks[hh], dyp, jnp.zeros_like(dyp))
                acc = acc + _dot(m32.astype(BF16), dym, 0, 0)
                dm = _dot(dym, xp, 1, 1)
                dcb = dcb + dm * lm
                wmat = dm * m32
                w_rows = w_rows + jnp.where(lane == j, jnp.sum(wmat, axis=1, keepdims=True), 0.0)
                w_cols_t = w_cols_t + jnp.where(row_t == j, jnp.sum(wmat, axis=0, keepdims=True), 0.0)
            parts.append(acc)
        dx = jnp.concatenate(parts, axis=1) + dx_off
        dcbb = dcb.astype(BF16)
        edy = (k.e_exp * dy).astype(BF16)
        dc_ref[...] = _dot(dcbb, k.bb, 1, 0) + _dot(edy, sb, 1, 1)
        db_ref[...] = _dot(dcbb, k.cb, 0, 0) + _dot(xdb, gb, 1, 1)
        g_scr[...] = k.elast_exp * g_next + _dot(k.cb, edy, 0, 0)

        y_off = k.e_exp * _dot(k.cb, sb, 1, 0)
        dcum = w_rows - w_cols_t.T + k.segsum(dy * y_off)
        t_term = k.segsum(k.x * dx_off)
        gs = jnp.broadcast_to(jnp.sum(g_next * s_prev, axis=0, keepdims=True), (8, gw))
        carried = k.segsum(gs)[0:1, :] * jnp.exp(k.cum[q - 1:q, :])
        dda = _rev_cumsum_rows(dcum) + (_cumsum_rows(t_term) - t_term) + carried
        ddt = jnp.where(k.head_lanes, dda * k.a + k.segsum(dx * k.xs), 0.0)
        ddtraw = ddt * jax.nn.sigmoid(k.dt_arg)
        ddt_ref[...] = ddtraw
        dxs_ref[...] = dx * k.dt_exp + dsk * dy
        ds = jnp.broadcast_to(jnp.sum(dy * k.xs, axis=0, keepdims=True), (8, gw))
        d_alog = jnp.sum(jnp.where(k.head_lanes, dda * k.dt, 0.0), axis=0, keepdims=True) * k.a
        rows8 = lax.broadcasted_iota(jnp.int32, (8, LANES), 0)
        small = jnp.where(rows8 == 0, d_alog, 0.0)
        small = small + jnp.where(rows8 == 1, jnp.sum(ddtraw, axis=0, keepdims=True), 0.0)
        small = small + jnp.where(rows8 == 2, k.segsum(ds)[0:1, :], 0.0)

        @pl.when(first)
        def _():
            small_ref[...] = small

        @pl.when(jnp.logical_not(first))
        def _():
            small_ref[...] += small

    bc_out = sp["bc"]
    return pl.pallas_call(
        body, name=name, grid=(SSD_GROUPS, n_chunks),
        out_shape=[jax.ShapeDtypeStruct((t_len, ssd_w), F32),
                   jax.ShapeDtypeStruct((t_len, SSD_GROUPS * SSD_STATE), F32),
                   jax.ShapeDtypeStruct((t_len, SSD_GROUPS * SSD_STATE), F32),
                   jax.ShapeDtypeStruct((t_len, ssd_w), F32),
                   jax.ShapeDtypeStruct((SSD_GROUPS, t_len, LANES), F32),
                   jax.ShapeDtypeStruct((SSD_GROUPS, 8, LANES), F32)],
        in_specs=[sp["dtraw"], sp["small"], sp["small"], sp["dsk"], sp["xs"], sp["b"], sp["c"], sp["z"],
                  sp["tok"], sp["state"], sp["tok"]],
        out_specs=[sp["tok"], bc_out, bc_out, sp["tok"], sp["dtraw"],
                   pl.BlockSpec((None, 8, LANES), lambda g, c: (g, 0, 0))],
        scratch_shapes=[pltpu.VMEM((SSD_STATE, gw), F32)],
        compiler_params=_params("parallel", "arbitrary"),
    )(dtraw_g, bias_g, alog_g, dsk_exp, xbc, xbc, xbc, qkvz, ypre, states, dyg)


def _adamw(w, g, m, v, name):
    n_lead, rows, lanes = w.shape
    tr = _row_tile(rows, lanes, 4, 14)
    c1 = 1.0 / (1.0 - ADAM_B1 ** ADAM_STEP)
    c2 = 1.0 / (1.0 - ADAM_B2 ** ADAM_STEP)

    def body(w_ref, g_ref, m_ref, v_ref, d_ref, nm_ref, nv_ref):
        gv = g_ref[...]
        nm = ADAM_B1 * m_ref[...] + (1.0 - ADAM_B1) * gv
        nv = ADAM_B2 * v_ref[...] + (1.0 - ADAM_B2) * (gv * gv)
        nm_ref[...] = nm
        nv_ref[...] = nv
        d_ref[...] = -ADAM_LR * ((nm * c1) / (jnp.sqrt(nv * c2) + ADAM_EPS) + ADAM_WD * w_ref[...])

    spec = pl.BlockSpec((None, tr, lanes), lambda l, i: (l, i, 0))
    return pl.pallas_call(
        body, name=name, grid=(n_lead, rows // tr),
        out_shape=[jax.ShapeDtypeStruct(w.shape, F32)] * 3,
        in_specs=[spec] * 4, out_specs=[spec] * 3,
        compiler_params=_params("parallel", "parallel"),
    )(w, g, m, v)


def _pad_lanes(a, width=LANES):
    return jnp.pad(a, ((0, 0), (0, width - a.shape[1])))


def _group_pad(v, heads):
    return _pad_lanes(v.reshape(SSD_GROUPS, heads))[:, None, :]


def _layer_fwd(x0, p, wt, dims, tag, rides):
    w_attn, heads_g, n_heads, conv_ch = dims["w_attn"], dims["heads_g"], dims["n_heads"], dims["conv_ch"]
    h1 = _rmsnorm_fwd([x0], [[x0.shape[1]]], p["ln1_g"], f"ln1_fwd{tag}")
    proj = _mm(h1, wt("w_in"), name=f"in_proj{tag}", tn=640, rides=rides)

    outs, lses = [], []
    for d in BRANCH_DILATIONS:
        o, l = _attn_branch_fwd(proj, w_attn, d, n_heads, f"attn_fwd_d{d}{tag}", rides)
        outs.append(o)
        lses.append(l)
    attn, lse = _attn_combine(outs, lses, f"attn_combine{tag}")

    xbc = _conv_fwd(proj, 4 * w_attn, p["conv_w"], p["conv_b"], f"conv_fwd{tag}")
    dt_col = 4 * w_attn + conv_ch
    dtraw_g = jnp.stack([_pad_lanes(proj[:, dt_col + g * heads_g:dt_col + (g + 1) * heads_g])
                         for g in range(SSD_GROUPS)])
    bias_g, alog_g = _group_pad(p["dt_bias"], heads_g), _group_pad(p["a_log"], heads_g)
    dsk_exp = jnp.repeat(p["d_skip"], HEAD_DIM).reshape(SSD_GROUPS, 1, heads_g * HEAD_DIM)
    yg, ypre, states = _ssd_fwd(xbc, proj, dtraw_g, bias_g, alog_g, dsk_exp, heads_g, f"ssd_fwd{tag}")

    gw = heads_g * HEAD_DIM
    mix_g = jnp.concatenate([p["attn_norm_g"], p["ssd_norm_g"]])[None, :]
    mix = _rmsnorm_fwd([attn, yg], [[w_attn], [gw] * SSD_GROUPS], mix_g, f"mix_norm_fwd{tag}")
    x1 = _mm(mix, wt("w_out"), name=f"out_proj{tag}", residual=x0, rides=rides)
    h2 = _rmsnorm_fwd([x1], [[x1.shape[1]]], p["ln2_g"], f"ln2_fwd{tag}")
    u = _mm(h2, wt("w_mlp_in"), name=f"mlp_in{tag}", out_dtype=BF16, tn=1024, rides=rides)
    x2 = _mm(u, wt("w_mlp_out"), name=f"mlp_out{tag}", a_act="relu2", residual=x1, tm=512, rides=rides)
    saved = dict(x0=x0, h1=h1, proj=proj, attn=attn, lse=lse, xbc=xbc, dtraw_g=dtraw_g,
                 bias_g=bias_g, alog_g=alog_g, dsk_exp=dsk_exp, yg=yg, ypre=ypre, states=states, mix=mix,
                 mix_g=mix_g, x1=x1, h2=h2, u=u)
    return x2, saved


def _pair_sums(ex, host, items):
    swapped = ex["rides"].done[("swap", host)]
    core = lax.axis_index("c").astype(jnp.int32).reshape(1)
    for i, (n, l) in enumerate(items):
        ex["pair"][(n, l)] = _pair_sum(ex["bufs"][(n, l)], swapped[i], core, f"pair_sum_{n}_l{l}")


def _layer_bwd(dx2, dx2_b, p, wt, s, dims, l, ex, copy_dx0):
    w_attn, heads_g, n_heads, conv_ch = dims["w_attn"], dims["heads_g"], dims["n_heads"], dims["conv_ch"]
    t_len, d_model = dx2.shape
    gw = heads_g * HEAD_DIM
    h_ssd = heads_g * SSD_GROUPS
    tag, rides, bufs = f"_l{l}", ex["rides"], ex["bufs"]
    du = _mm(dx2_b, wt("w_mlp_out"), name=f"mlp_out_dx{tag}", tb=True, gate=s["u"], out_dtype=BF16, tn=1024,
             rides=rides)
    d_wmo = _mm(s["u"], dx2_b, name=f"mlp_out_dw{tag}", ta=True, a_act="relu2", tm=512, tn=1024, out_dtype=BF16)
    bufs[("w_mlp_out", l)] = d_wmo.reshape(N_DEV, -1, d_model)
    bufs[("w_mlp_in", l)] = _mm(s["h2"], du, name=f"mlp_in_dw{tag}", ta=True, tm=512, tn=1024, out_dtype=BF16,
                                out_chunk=du.shape[1] // N_DEV)
    dh2 = _mm(du, wt("w_mlp_in"), name=f"mlp_in_dx{tag}", tb=True, tm=512, rides=rides)
    _pair_sums(ex, f"mlp_in_dx{tag}", [("w_mlp_out", l), ("w_mlp_in", l)])
    (dx1,), d_ln2, (dx1_b,) = _rmsnorm_bwd([s["x1"]], [[d_model]], p["ln2_g"], dh2, [dx2], f"ln2_bwd{tag}",
                                           bf16_copy=True)
    dmix = _mm(dx1_b, wt("w_out"), name=f"out_proj_dx{tag}", tb=True)
    d_wo = _mm(s["mix"], dx1_b, name=f"out_proj_dw{tag}", ta=True, tm=512, tn=1024, out_dtype=BF16)
    bufs[("w_out", l)] = d_wo.reshape(N_DEV, -1, d_model)
    after_branch = {BRANCH_DILATIONS[0]: [("w_out", l)]}
    (dattn, dyg), d_mix_g, _ = _rmsnorm_bwd([s["attn"], s["yg"]], [[w_attn], [gw] * SSD_GROUPS], s["mix_g"], dmix,
                                           [None, None], f"mix_norm_bwd{tag}")
    dxs, db, dc, dz, ddtraw_g, ssd_small = _ssd_bwd(
        s["xbc"], s["proj"], s["dtraw_g"], s["bias_g"], s["alog_g"], s["dsk_exp"], s["ypre"], s["states"], dyg,
        heads_g, f"ssd_bwd{tag}")
    dxbc = jnp.concatenate([dxs, db, dc], axis=1)
    dxbc_raw, d_conv_w, d_conv_b = _conv_bwd(s["proj"], 4 * w_attn, p["conv_w"], p["conv_b"], dxbc, f"conv_bwd{tag}")
    acc = None
    for d in BRANCH_DILATIONS:
        acc = _attn_branch_bwd(s["proj"], w_attn, s["attn"], s["lse"], dattn, d, n_heads, f"attn_bwd_d{d}{tag}", acc,
                               rides)
        if d in after_branch:
            _pair_sums(ex, f"attn_bwd_d{d}{tag}", after_branch[d])
    w_in = wt("w_in")
    in_proj = 4 * w_attn + conv_ch + h_ssd
    pad = jnp.zeros((t_len, w_in.shape[1] - in_proj), F32)
    dproj = jnp.concatenate([*acc, dz, dxbc_raw] + [ddtraw_g[g, :, :heads_g] for g in range(SSD_GROUPS)] + [pad],
                            axis=1).astype(BF16)
    d_win = _mm(s["h1"], dproj, name=f"in_proj_dw{tag}", ta=True, tm=512, tn=1152, out_dtype=BF16, rides=rides)
    bufs[("w_in", l)] = d_win[:, :in_proj].reshape(d_model, N_DEV, -1).transpose(1, 0, 2)
    dh1 = _mm(dproj, w_in, name=f"in_proj_dx{tag}", tb=True, tm=512, rides=rides)
    _pair_sums(ex, f"in_proj_dx{tag}", [("w_in", l)])
    (dx0,), d_ln1, dx0_b = _rmsnorm_bwd([s["x0"]], [[d_model]], p["ln1_g"], dh1, [dx1], f"ln1_bwd{tag}",
                                        bf16_copy=copy_dx0)

    small = ssd_small[:, :, :heads_g]
    grads = dict(
        ln1_g=d_ln1[0], conv_w=d_conv_w, conv_b=d_conv_b[0],
        a_log=small[:, 0].reshape(h_ssd), dt_bias=small[:, 1].reshape(h_ssd), d_skip=small[:, 2].reshape(h_ssd),
        attn_norm_g=d_mix_g[0, :w_attn], ssd_norm_g=d_mix_g[0, w_attn:], ln2_g=d_ln2[0])
    return dx0, (dx0_b[0] if copy_dx0 else None), grads


_SMALL = ["ln1_g", "conv_w", "conv_b", "dt_bias", "a_log", "d_skip", "attn_norm_g", "ssd_norm_g", "ln2_g"]
_WEIGHTS = ["ln1_g", "w_in", "conv_w", "conv_b", "dt_bias", "a_log", "d_skip", "attn_norm_g", "ssd_norm_g",
            "w_out", "ln2_g", "w_mlp_in", "w_mlp_out", "final_norm_g"]


def _to_rows(a):
    flat = a.reshape(-1)
    rows = -(-flat.shape[0] // LANES)
    rows = -(-rows // 8) * 8
    return jnp.pad(flat, (0, rows * LANES - flat.shape[0])).reshape(rows, LANES)


def kernel(x, ln1_g, w_in, conv_w, conv_b, dt_bias, a_log, d_skip, attn_norm_g, ssd_norm_g, w_out, ln2_g, w_mlp_in, w_mlp_out, final_norm_g, loss_target, m_ln1_g, m_w_in, m_conv_w, m_conv_b, m_dt_bias, m_a_log, m_d_skip, m_attn_norm_g, m_ssd_norm_g, m_w_out, m_ln2_g, m_w_mlp_in, m_w_mlp_out, m_final_norm_g, v_ln1_g, v_w_in, v_conv_w, v_conv_b, v_dt_bias, v_a_log, v_d_skip, v_attn_norm_g, v_ssd_norm_g, v_w_out, v_ln2_g, v_w_mlp_in, v_w_mlp_out, v_final_norm_g):
    w = dict(ln1_g=ln1_g, w_in=w_in, conv_w=conv_w, conv_b=conv_b, dt_bias=dt_bias, a_log=a_log, d_skip=d_skip,
             attn_norm_g=attn_norm_g, ssd_norm_g=ssd_norm_g, w_out=w_out, ln2_g=ln2_g, w_mlp_in=w_mlp_in,
             w_mlp_out=w_mlp_out, final_norm_g=final_norm_g)
    mom = dict(ln1_g=m_ln1_g, w_in=m_w_in, conv_w=m_conv_w, conv_b=m_conv_b, dt_bias=m_dt_bias, a_log=m_a_log,
               d_skip=m_d_skip, attn_norm_g=m_attn_norm_g, ssd_norm_g=m_ssd_norm_g, w_out=m_w_out, ln2_g=m_ln2_g,
               w_mlp_in=m_w_mlp_in, w_mlp_out=m_w_mlp_out, final_norm_g=m_final_norm_g)
    var = dict(ln1_g=v_ln1_g, w_in=v_w_in, conv_w=v_conv_w, conv_b=v_conv_b, dt_bias=v_dt_bias, a_log=v_a_log,
               d_skip=v_d_skip, attn_norm_g=v_attn_norm_g, ssd_norm_g=v_ssd_norm_g, w_out=v_w_out, ln2_g=v_ln2_g,
               w_mlp_in=v_w_mlp_in, w_mlp_out=v_w_mlp_out, final_norm_g=v_final_norm_g)

    depth, d_model = ln1_g.shape
    t_len = x.shape[1]
    w_attn = attn_norm_g.shape[1]
    h_ssd = dt_bias.shape[1]
    conv_ch = conv_b.shape[1]
    in_proj = w_in.shape[2] * N_DEV
    assert ssd_norm_g.shape[1] == w_attn and in_proj == 4 * w_attn + conv_ch + h_ssd
    assert t_len % (BRANCH_DILATIONS[-1] * ATTN_BLOCK) == 0 and h_ssd % (2 * SSD_GROUPS) == 0
    dims = dict(w_attn=w_attn, heads_g=h_ssd // SSD_GROUPS, n_heads=w_attn // HEAD_DIM, conv_ch=conv_ch)
    names = ["w_in", "w_out", "w_mlp_in", "w_mlp_out"]

    rides = _Rides()
    ex = dict(rides=rides, bufs={}, pair={})
    latest, sent = {}, {}

    def shard(n, l):
        return w[n][l].astype(BF16)

    def half(rows, part):
        return None if part is None else (part * (rows // 2), rows // 2)

    def plan_spread(host, n, l, part=None):
        key, prev = ("spread", host, n, l, part), latest.get((n, l))
        rides.put(host, key, lambda: _GatherSpread([shard(n, l)], rows=half(w[n].shape[1], part),
                                                   into=[rides.done[prev[0]][prev[1]]] if prev else None))
        latest[(n, l)] = (key, 0)

    def plan_pass(host, items):
        key, srcs = ("pass", host), [latest[it] for it in items]
        rides.put(host, key, lambda: _GatherPass([rides.done[k][i] for k, i in srcs]))
        for i, it in enumerate(items):
            latest[it] = (key, i)

    def plan_swap(host, items):
        rides.put(host, ("swap", host), lambda: _SiblingSwap([ex["bufs"][it] for it in items]))

    def plan_send(host, n, l, part=None):
        key, prev = ("send", host, n, l, part), sent.get((n, l))
        rides.put(host, key, lambda: _ChipSend([ex["pair"][(n, l)]], rows=half(ex["pair"][(n, l)].shape[1], part),
                                               into=[rides.done[prev[0]][prev[1]]] if prev else None))
        sent[(n, l)] = (key, 0)

    d_first, d_mid, d_last = (f"d{d}" for d in BRANCH_DILATIONS)
    for l in range(depth):
        t = f"_l{l}"
        if l == 0:
            plan_spread(f"in_proj{t}", "w_out", 0)
            plan_spread(f"in_proj{t}", "w_mlp_in", 0, 0)
            plan_spread(f"attn_fwd_{d_first}{t}", "w_mlp_in", 0, 1)
            plan_spread(f"attn_fwd_{d_mid}{t}", "w_mlp_out", 0, 0)
            plan_pass(f"attn_fwd_{d_mid}{t}", [("w_out", 0), ("w_mlp_in", 0)])
            plan_spread(f"attn_fwd_{d_last}{t}", "w_mlp_out", 0, 1)
            plan_pass(f"out_proj{t}", [("w_mlp_out", 0)])
        else:
            plan_spread(f"in_proj{t}", "w_mlp_out", l, 0)
            plan_spread(f"attn_fwd_{d_first}{t}", "w_mlp_out", l, 1)
            plan_pass(f"attn_fwd_{d_mid}{t}", [("w_mlp_out", l)])
        if l + 1 < depth:
            plan_spread(f"out_proj{t}", "w_out", l + 1)
            plan_spread(f"mlp_in{t}", "w_in", l + 1)
            plan_spread(f"mlp_out{t}", "w_mlp_in", l + 1)
            plan_pass(f"pass_weights_l{l + 1}", [("w_out", l + 1), ("w_in", l + 1), ("w_mlp_in", l + 1)])
        plan_swap(f"mlp_in_dx{t}", [("w_mlp_out", l), ("w_mlp_in", l)])
        plan_send(f"attn_bwd_{d_first}{t}", "w_mlp_out", l, 0)
        plan_swap(f"attn_bwd_{d_first}{t}", [("w_out", l)])
        plan_send(f"attn_bwd_{d_mid}{t}", "w_mlp_out", l, 1)
        plan_send(f"attn_bwd_{d_last}{t}", "w_mlp_in", l)
        plan_send(f"in_proj_dw{t}", "w_out", l)
        plan_swap(f"in_proj_dx{t}", [("w_in", l)])
        plan_send(f"mlp_out_dx_l{l - 1}" if l > 0 else "send_last_grads", "w_in", l)

    g_in0, g_cw = _all_gather([shard("w_in", 0), conv_w], "gather_first")
    full_cw = _with_own(g_cw, conv_w).transpose(1, 2, 0, 3).reshape(depth, SSD_CONV, conv_ch)
    proj_cols = -(-in_proj // LANES) * LANES
    full = {}

    def weight(n, l):
        if (n, l) not in full:
            if (n, l) == ("w_in", 0):
                g = g_in0
            else:
                key, i = latest[(n, l)]
                g = rides.done[key][i]
            g = _with_own(g, shard(n, l))
            if n == "w_in":
                g = _pad_lanes(g.transpose(1, 0, 2).reshape(d_model, in_proj), proj_cols)
            elif n == "w_mlp_in":
                g = g.transpose(1, 0, 2).reshape(d_model, -1)
            else:
                g = g.reshape(-1, d_model)
            full[(n, l)] = g
        return full[(n, l)]

    layers = [dict(ln1_g=ln1_g[l][None, :], ln2_g=ln2_g[l][None, :], conv_w=full_cw[l], conv_b=conv_b[l][None, :],
                   dt_bias=dt_bias[l], a_log=a_log[l], d_skip=d_skip[l], attn_norm_g=attn_norm_g[l],
                   ssd_norm_g=ssd_norm_g[l]) for l in range(depth)]

    h = x[0]
    saved = []
    for l in range(depth):
        h, s = _layer_fwd(h, layers[l], functools.partial(lambda n, l: weight(n, l), l=l), dims, f"_l{l}", rides)
        saved.append(s)
        if l + 1 < depth:
            _alone(rides, f"pass_weights_l{l + 1}")
    dh, d_final_g, loss_part, dh_b = _loss_head(h, final_norm_g[None, :], loss_target[0], "loss_head")

    grads = [None] * depth
    for l in reversed(range(depth)):
        dh, dh_b, grads[l] = _layer_bwd(dh, dh_b, layers[l], functools.partial(lambda n, l: weight(n, l), l=l),
                                        saved[l], dims, l, ex, copy_dx0=l > 0)
    grad_x = dh[None]
    _alone(rides, "send_last_grads")

    my_chip = (2 * lax.axis_index("x") + lax.axis_index("y")).astype(jnp.int32).reshape(1)
    gsum = {}
    for n in names:
        per_layer = []
        for l in range(depth):
            key, i = sent[(n, l)]
            per_layer.append(_sum_with_own(rides.done[key][i], ex["pair"][(n, l)], my_chip, f"sum_{n}_l{l}"))
        gsum[n] = jnp.stack(per_layer)

    small_parts = [jnp.stack([grads[l][n] for l in range(depth)]).reshape(-1) for n in _SMALL]
    small_parts += [d_final_g.reshape(-1), loss_part[0, :1]]
    sizes = [int(a.shape[0]) for a in small_parts]
    packed = _to_rows(jnp.concatenate(small_parts))
    (gathered,) = _all_gather([packed], "gather_small_grads")
    total = _sum_leading(_with_own(gathered, packed), "sum_small_grads").reshape(-1)
    offs = np.cumsum([0] + sizes)
    pieces = [total[offs[i]:offs[i + 1]] for i in range(len(sizes))]
    for n, piece in zip(_SMALL, pieces):
        shape = (depth, SSD_CONV, conv_ch) if n == "conv_w" else w[n].shape
        gsum[n] = piece.reshape(shape)
    gsum["final_norm_g"] = pieces[len(_SMALL)]
    loss = pieces[len(_SMALL) + 1][0]
    my_id = 4 * lax.axis_index("x") + 2 * lax.axis_index("y") + lax.axis_index("c")
    cw = conv_w.shape[2]
    gsum["conv_w"] = lax.dynamic_slice_in_dim(gsum["conv_w"], my_id * cw, cw, axis=2)

    delta, new_m, new_v = {}, {}, {}
    for n in names:
        delta[n], new_m[n], new_v[n] = _adamw(w[n], gsum[n], mom[n], var[n], f"adamw_{n}")
    small_names = [n for n in _WEIGHTS if n not in names]
    sm_sizes = [int(np.prod(w[n].shape)) for n in small_names]
    pack = lambda d: _to_rows(jnp.concatenate([d[n].reshape(-1) for n in small_names]))[None]
    outs = _adamw(pack(w), pack(gsum), pack(mom), pack(var), "adamw_small")
    sm_offs = np.cumsum([0] + sm_sizes)
    for res, o in zip((delta, new_m, new_v), outs):
        flat = o.reshape(-1)
        for i, n in enumerate(small_names):
            res[n] = flat[sm_offs[i]:sm_offs[i + 1]].reshape(w[n].shape)

    return (loss, grad_x, *[gsum[n] for n in _WEIGHTS], *[delta[n] for n in _WEIGHTS],
            *[new_m[n] for n in _WEIGHTS], *[new_v[n] for n in _WEIGHTS])
```

```python
import functools
import math

import numpy as np
import jax
import jax.numpy as jnp
from jax import lax
from jax.experimental import pallas as pl
from jax.experimental.pallas import tpu as pltpu

F32 = jnp.float32
BF16 = jnp.bfloat16

N_DEV = 8
LANES = 128
HEAD_DIM = 64
ATTN_BLOCK = 128
BRANCH_DILATIONS = (1, 4, 16)
SSD_GROUPS = 2
SSD_STATE = 128
SSD_CHUNK = 128
SSD_CONV = 4
NORM_EPS = 1e-5
ADAM_LR, ADAM_B1, ADAM_B2, ADAM_EPS, ADAM_WD, ADAM_STEP = 0.001, 0.9, 0.999, 1e-08, 0.01, 10
VMEM_LIMIT_BYTES = 56 * 1024 * 1024
MESH = pl.DeviceIdType.MESH
NEG_INF = float("-inf")


def _params(*sem):
    return pltpu.CompilerParams(dimension_semantics=tuple(sem), vmem_limit_bytes=VMEM_LIMIT_BYTES)


def _pick(n, target, mult):
    best = None
    for t in range(mult, min(n, target) + 1, mult):
        if n % t == 0:
            best = t
    assert best is not None, (n, target, mult)
    return best


def _dot(a, b, ca, cb):
    return lax.dot_general(a, b, (((ca,), (cb,)), ((), ())), preferred_element_type=F32)


def _split3(v):
    hi = v.astype(BF16)
    r = v - hi.astype(F32)
    mid = r.astype(BF16)
    lo = (r - mid.astype(F32)).astype(BF16)
    return hi, mid, lo


def _dot_exact(v, sel, ca, cb):
    hi, mid, lo = _split3(v)
    return _dot(hi, sel, ca, cb) + _dot(mid, sel, ca, cb) + _dot(lo, sel, ca, cb)


_HBM = pl.BlockSpec(memory_space=pltpu.HBM)


def _all_gather(xs, name):
    n = len(xs)

    def body(*refs):
        x_refs, o_refs = refs[:n], refs[n:2 * n]
        send_sems, recv_sems = refs[2 * n:]
        x, y, c = lax.axis_index("x"), lax.axis_index("y"), lax.axis_index("c")
        me, sibling = (x, y, c), (x, y, 1 - c)
        chips = [(1 - x, y), (x, 1 - y), (1 - x, 1 - y)]

        def copy(t, k, block, to, src=None):
            bx, by, bc = block
            dst = o_refs[t].at[4 * bx + 2 * by + bc]
            return pltpu.make_async_remote_copy(
                src_ref=dst if src is None else src, dst_ref=dst,
                send_sem=send_sems.at[t, k], recv_sem=recv_sems.at[t, k],
                device_id=to, device_id_type=MESH)

        first, passed = [], []
        for t in range(n):
            cps = [copy(t, 0, me, sibling, src=x_refs[t])]
            cps += [copy(t, 1 + j, me, (*chip, c), src=x_refs[t]) for j, chip in enumerate(chips)]
            for cp in cps:
                cp.start()
            first += cps
        for t in range(n):
            for j, chip in enumerate(chips):
                copy(t, 1 + j, (*chip, c), me).wait_recv()
                fwd = copy(t, 4 + j, (*chip, c), sibling)
                fwd.start()
                passed.append(fwd)
        for t in range(n):
            copy(t, 0, sibling, me).wait_recv()
            for j, chip in enumerate(chips):
                copy(t, 4 + j, (*chip, 1 - c), me).wait_recv()
        for cp in first + passed:
            cp.wait_send()

    return pl.pallas_call(
        body, name=name,
        out_shape=[jax.ShapeDtypeStruct((N_DEV,) + a.shape, a.dtype) for a in xs],
        in_specs=[_HBM] * n, out_specs=[_HBM] * n,
        scratch_shapes=[pltpu.SemaphoreType.DMA((n, 7)), pltpu.SemaphoreType.DMA((n, 7))],
    )(*xs)


def _with_own(gathered, own):
    me = 4 * lax.axis_index("x") + 2 * lax.axis_index("y") + lax.axis_index("c")
    return lax.dynamic_update_index_in_dim(gathered, own, me, 0)


def _place():
    x, y, c = lax.axis_index("x"), lax.axis_index("y"), lax.axis_index("c")
    return x, y, c, 4 * x + 2 * y + c, (x, y, 1 - c), [(1 - x, y), (x, 1 - y), (1 - x, 1 - y)]


def _remote(src, dst, send_sem, recv_sem, to):
    return pltpu.make_async_remote_copy(src_ref=src, dst_ref=dst, send_sem=send_sem, recv_sem=recv_sem,
                                        device_id=to, device_id_type=MESH)


class _Riding:
    aliases = {}

    def copies(self, ins, outs, sems):
        raise NotImplementedError

    def start(self, ins, outs, sems):
        local, out, _ = self.copies(ins, outs, sems)
        for cp in local + out:
            cp.start()

    def wait(self, ins, outs, sems):
        local, out, landing = self.copies(ins, outs, sems)
        for cp in landing:
            cp.wait_recv()
        for cp in out:
            cp.wait_send()
        for cp in local:
            cp.wait()


def _rows_of(ref, rows):
    return ref if rows is None else ref.at[pl.ds(rows[0], rows[1])]


class _GatherSpread(_Riding):
    def __init__(self, xs, rows=None, into=None):
        n = len(xs)
        self.rows = rows
        self.ins = list(xs) + list(into or [])
        self.out_shapes = [jax.ShapeDtypeStruct((N_DEV,) + a.shape, a.dtype) for a in xs]
        self.aliases = {n + t: t for t in range(n)} if into else {}
        self.sem_shapes = [pltpu.SemaphoreType.DMA((n, 4)), pltpu.SemaphoreType.DMA((n, 4))]

    def copies(self, ins, outs, sems):
        send, recv = sems
        _, _, c, me, sibling, chips = _place()
        targets = [sibling] + [(*chip, c) for chip in chips]
        out, landing = [], []
        for t in range(len(outs)):
            src = _rows_of(ins[t], self.rows)
            for k, to in enumerate(targets):
                out.append(_remote(src, _rows_of(outs[t].at[me], self.rows), send.at[t, k], recv.at[t, k], to))
                theirs = _rows_of(outs[t].at[4 * to[0] + 2 * to[1] + to[2]], self.rows)
                landing.append(_remote(src, theirs, send.at[t, k], recv.at[t, k], to))
        return [], out, landing


class _GatherPass(_Riding):
    def __init__(self, bufs):
        n = len(bufs)
        self.ins = list(bufs)
        self.out_shapes = [jax.ShapeDtypeStruct(b.shape, b.dtype) for b in bufs]
        self.aliases = {t: t for t in range(n)}
        self.sem_shapes = [pltpu.SemaphoreType.DMA((n, 3)), pltpu.SemaphoreType.DMA((n, 3))]

    def copies(self, ins, outs, sems):
        send, recv = sems
        _, _, c, _, sibling, chips = _place()
        out, landing = [], []
        for t in range(len(outs)):
            for j, (px, py) in enumerate(chips):
                got = outs[t].at[4 * px + 2 * py + c]
                out.append(_remote(got, got, send.at[t, j], recv.at[t, j], sibling))
                landing.append(_remote(got, outs[t].at[4 * px + 2 * py + 1 - c], send.at[t, j], recv.at[t, j], sibling))
        return [], out, landing


class _SiblingSwap(_Riding):
    def __init__(self, xs):
        n = len(xs)
        self.ins = list(xs)
        self.out_shapes = [jax.ShapeDtypeStruct((N_DEV // 2,) + a.shape[1:], a.dtype) for a in xs]
        self.sem_shapes = [pltpu.SemaphoreType.DMA((n, 4)), pltpu.SemaphoreType.DMA((n, 4))]

    def copies(self, ins, outs, sems):
        send, recv = sems
        _, _, c, _, sibling, _ = _place()
        out = [_remote(ins[t].at[2 * q + 1 - c], outs[t].at[q], send.at[t, q], recv.at[t, q], sibling)
               for t in range(len(ins)) for q in range(N_DEV // 2)]
        return [], out, out


class _ChipSend(_Riding):
    def __init__(self, ps, rows=None, into=None):
        n = len(ps)
        self.rows = rows
        self.ins = list(ps) + list(into or [])
        self.out_shapes = [jax.ShapeDtypeStruct((3,) + a.shape[1:], a.dtype) for a in ps]
        self.aliases = {n + t: t for t in range(n)} if into else {}
        self.sem_shapes = [pltpu.SemaphoreType.DMA((n, 3)), pltpu.SemaphoreType.DMA((n, 3))]

    def copies(self, ins, outs, sems):
        send, recv = sems
        _, _, c, _, _, chips = _place()
        out = [_remote(_rows_of(ins[t].at[2 * px + py], self.rows), _rows_of(outs[t].at[j], self.rows),
                       send.at[t, j], recv.at[t, j], (px, py, c))
               for t in range(len(outs)) for j, (px, py) in enumerate(chips)]
        return [], out, out


class _Bundle(_Riding):
    def __init__(self, comms):
        self.comms = comms
        self.ins = [a for cm in comms for a in cm.ins]
        self.out_shapes = [s for cm in comms for s in cm.out_shapes]
        self.sem_shapes = [s for cm in comms for s in cm.sem_shapes]
        self.aliases = {}
        i0 = o0 = 0
        for cm in comms:
            self.aliases.update({i0 + i: o0 + j for i, j in cm.aliases.items()})
            i0, o0 = i0 + len(cm.ins), o0 + len(cm.out_shapes)

    def copies(self, ins, outs, sems):
        local, out, landing = [], [], []
        i0 = o0 = s0 = 0
        for cm in self.comms:
            i1, o1, s1 = i0 + len(cm.ins), o0 + len(cm.out_shapes), s0 + len(cm.sem_shapes)
            a, b, c = cm.copies(ins[i0:i1], outs[o0:o1], sems[s0:s1])
            local, out, landing = local + a, out + b, landing + c
            i0, o0, s0 = i1, o1, s1
        return local, out, landing


class _Rides:
    def __init__(self):
        self.plan, self.done, self.aboard = {}, {}, {}

    def put(self, host, key, make):
        self.plan.setdefault(host, []).append((key, make))

    def board(self, host):
        if host not in self.plan:
            return None
        self.aboard[host] = [make() for _, make in self.plan[host]]
        return _Bundle(self.aboard[host])

    def land(self, host, results):
        o0 = 0
        for (key, _), cm in zip(self.plan[host], self.aboard[host]):
            self.done[key] = list(results[o0:o0 + len(cm.out_shapes)])
            o0 += len(cm.out_shapes)


def _pallas(body, *, name, grid, out_shape, in_specs, out_specs, operands, semantics, scratch_shapes=(), rides=None):
    comm = rides.board(name) if rides is not None else None
    if comm is None:
        return pl.pallas_call(
            body, name=name, grid=grid, out_shape=list(out_shape), in_specs=list(in_specs),
            out_specs=list(out_specs), scratch_shapes=list(scratch_shapes), compiler_params=_params(*semantics),
        )(*operands)
    n_in, n_out, n_scr = len(in_specs), len(out_shape), len(scratch_shapes)
    n_ci, n_co = len(comm.ins), len(comm.out_shapes)

    def hosted(*refs):
        cuts = np.cumsum([0, n_in, n_ci, n_out, n_co, n_scr])
        ins, c_ins, outs, c_outs, scr = (refs[cuts[i]:cuts[i + 1]] for i in range(5))
        sems = refs[cuts[5]:]
        ids = [pl.program_id(a) for a in range(len(grid))]
        first = functools.reduce(jnp.logical_and, [i == 0 for i in ids])
        last = functools.reduce(jnp.logical_and, [i == g - 1 for i, g in zip(ids, grid)])

        @pl.when(first)
        def _():
            comm.start(c_ins, c_outs, sems)

        body(*ins, *outs, *scr)

        @pl.when(last)
        def _():
            comm.wait(c_ins, c_outs, sems)

    results = pl.pallas_call(
        hosted, name=name, grid=grid, out_shape=list(out_shape) + comm.out_shapes,
        in_specs=list(in_specs) + [_HBM] * n_ci, out_specs=list(out_specs) + [_HBM] * n_co,
        scratch_shapes=list(scratch_shapes) + comm.sem_shapes,
        input_output_aliases={n_in + i: n_out + j for i, j in comm.aliases.items()},
        compiler_params=_params(*["arbitrary"] * len(grid)),
    )(*operands, *comm.ins)
    rides.land(name, results[n_out:])
    return results[:n_out]


def _alone(rides, name):
    comm = rides.board(name)

    def body(*refs):
        n_ci, n_co = len(comm.ins), len(comm.out_shapes)
        ins, outs, sems = refs[:n_ci], refs[n_ci:n_ci + n_co], refs[n_ci + n_co:]
        comm.start(ins, outs, sems)
        comm.wait(ins, outs, sems)

    results = pl.pallas_call(
        body, name=name, out_shape=comm.out_shapes, in_specs=[_HBM] * len(comm.ins),
        out_specs=[_HBM] * len(comm.out_shapes), scratch_shapes=comm.sem_shapes,
        input_output_aliases=dict(comm.aliases),
    )(*comm.ins)
    rides.land(name, results)


def _row_tile(rows, cols, itemsize, copies, budget=24 * 1024 * 1024):
    padded = -(-cols // LANES) * LANES
    mult = 8 * (4 // itemsize)
    if rows % mult:
        return rows
    return _pick(rows, max(mult, budget // (copies * padded * itemsize)), mult)


def _sum_leading(x, name):
    n_src, rows, cols = x.shape
    tr = _row_tile(rows, cols, 4, 2 * (n_src + 2))

    def body(x_ref, o_ref):
        acc = x_ref[0].astype(F32)
        for s in range(1, n_src):
            acc = acc + x_ref[s].astype(F32)
        o_ref[...] = acc

    return pl.pallas_call(
        body, name=name, grid=(rows // tr,), out_shape=jax.ShapeDtypeStruct((rows, cols), F32),
        in_specs=[pl.BlockSpec((n_src, tr, cols), lambda i: (0, i, 0))],
        out_specs=pl.BlockSpec((tr, cols), lambda i: (i, 0)), compiler_params=_params("parallel"),
    )(x)


def _pair_sum(buf, theirs, core, name):
    n_q, rows, cols = theirs.shape
    tr = _row_tile(rows, cols, 4, 6)

    def body(core_ref, mine_ref, theirs_ref, o_ref):
        o_ref[...] = (mine_ref[...].astype(F32) + theirs_ref[...].astype(F32)).astype(BF16)

    spec = pl.BlockSpec((None, tr, cols), lambda q, i, core_ref: (q, i, 0))
    return pl.pallas_call(
        body, name=name, out_shape=jax.ShapeDtypeStruct(theirs.shape, BF16),
        grid_spec=pltpu.PrefetchScalarGridSpec(
            num_scalar_prefetch=1, grid=(n_q, rows // tr),
            in_specs=[pl.BlockSpec((None, tr, cols), lambda q, i, core_ref: (2 * q + core_ref[0], i, 0)), spec],
            out_specs=spec),
        compiler_params=_params("parallel", "parallel"),
    )(core, buf, theirs)


def _sum_with_own(recv, pair, chip, name):
    n_src, rows, cols = recv.shape
    tr = _row_tile(rows, cols, 4, 2 * (n_src + 3))

    def body(chip_ref, own_ref, recv_ref, o_ref):
        acc = own_ref[...].astype(F32)
        for s in range(n_src):
            acc = acc + recv_ref[s].astype(F32)
        o_ref[...] = acc

    return pl.pallas_call(
        body, name=name, out_shape=jax.ShapeDtypeStruct((rows, cols), F32),
        grid_spec=pltpu.PrefetchScalarGridSpec(
            num_scalar_prefetch=1, grid=(rows // tr,),
            in_specs=[pl.BlockSpec((None, tr, cols), lambda i, chip_ref: (chip_ref[0], i, 0)),
                      pl.BlockSpec((n_src, tr, cols), lambda i, chip_ref: (0, i, 0))],
            out_specs=pl.BlockSpec((tr, cols), lambda i, chip_ref: (i, 0))),
        compiler_params=_params("parallel"),
    )(chip, pair, recv)


def _mm(a, b, *, name, ta=False, tb=False, tm=1024, tn=512, out_dtype=F32, a_act=None,
        residual=None, gate=None, out_chunk=None, rides=None):
    k_dim, m = (a.shape if ta else a.shape[::-1])
    n, kb = (b.shape if tb else b.shape[::-1])
    assert kb == k_dim, (a.shape, b.shape, ta, tb)
    tm, tn = _pick(m, tm, 128), _pick(out_chunk or n, tn, 128)
    ca, cb = (0 if ta else 1), (1 if tb else 0)
    a_spec = pl.BlockSpec((k_dim, tm), lambda i, j: (0, i)) if ta else pl.BlockSpec((tm, k_dim), lambda i, j: (i, 0))
    b_spec = pl.BlockSpec((tn, k_dim), lambda i, j: (j, 0)) if tb else pl.BlockSpec((k_dim, tn), lambda i, j: (0, j))
    mn_spec = pl.BlockSpec((tm, tn), lambda i, j: (i, j))
    if out_chunk:
        per = out_chunk // tn
        o_spec = pl.BlockSpec((None, tm, tn), lambda i, j: (j // per, i, j % per))
        out_shape = jax.ShapeDtypeStruct((n // out_chunk, m, out_chunk), out_dtype)
    else:
        o_spec = mn_spec
        out_shape = jax.ShapeDtypeStruct((m, n), out_dtype)
    operands, in_specs = [a, b], [a_spec, b_spec]
    for extra in (gate, residual):
        if extra is not None:
            operands.append(extra)
            in_specs.append(mn_spec)

    def body(*refs):
        a_ref, b_ref, o_ref = refs[0], refs[1], refs[-1]
        extras = list(refs[2:-1])
        gate_ref = extras.pop(0) if gate is not None else None
        res_ref = extras.pop(0) if residual is not None else None
        av = a_ref[...].astype(BF16)
        if a_act == "relu2":
            av = jnp.square(jnp.maximum(av, jnp.zeros_like(av)))
        r = _dot(av, b_ref[...].astype(BF16), ca, cb)
        if gate_ref is not None:
            r = r * (2.0 * jnp.maximum(gate_ref[...].astype(F32), 0.0))
        if res_ref is not None:
            r = r + res_ref[...].astype(F32)
        o_ref[...] = r.astype(out_dtype)

    return _pallas(body, name=name, grid=(m // tm, n // tn), out_shape=[out_shape], in_specs=in_specs,
                   out_specs=[o_spec], operands=operands, semantics=("parallel", "arbitrary"), rides=rides)[0]


def _rmsnorm_fwd(xs, seg_widths, g, name, tm=256):
    t_len = xs[0].shape[0]
    width = sum(x.shape[1] for x in xs)
    tm = _pick(t_len, tm, 16)
    n = len(xs)

    def body(*refs):
        x_refs, g_ref, o_ref = refs[:n], refs[n], refs[n + 1]
        col = 0
        for x_ref, widths in zip(x_refs, seg_widths):
            off = 0
            for w in widths:
                xv = x_ref[:, off:off + w].astype(F32)
                r = lax.rsqrt(jnp.mean(xv * xv, axis=1, keepdims=True) + NORM_EPS)
                o_ref[:, col:col + w] = (xv * r * g_ref[:, col:col + w]).astype(BF16)
                off += w
                col += w

    return pl.pallas_call(
        body, name=name, grid=(t_len // tm,),
        out_shape=jax.ShapeDtypeStruct((t_len, width), BF16),
        in_specs=[pl.BlockSpec((tm, x.shape[1]), lambda i: (i, 0)) for x in xs]
        + [pl.BlockSpec((1, width), lambda i: (0, 0))],
        out_specs=pl.BlockSpec((tm, width), lambda i: (i, 0)),
        compiler_params=_params("parallel"),
    )(*xs, g)


def _rmsnorm_bwd(xs, seg_widths, g, dh, residuals, name, tm=256, bf16_copy=False):
    t_len = xs[0].shape[0]
    width = sum(x.shape[1] for x in xs)
    tm = _pick(t_len, tm, 8)
    n = len(xs)
    has_res = [r is not None for r in residuals]
    res_ops = [r for r in residuals if r is not None]

    def body(*refs):
        x_refs, g_ref, dh_ref = refs[:n], refs[n], refs[n + 1]
        res_refs = list(refs[n + 2:n + 2 + len(res_ops)])
        dx_refs = refs[n + 2 + len(res_ops):n + 2 + len(res_ops) + n]
        dg_ref = refs[n + 2 + len(res_ops) + n]
        copy_refs = refs[n + 3 + len(res_ops) + n:]
        first = pl.program_id(0) == 0
        col = 0
        for idx, (x_ref, widths) in enumerate(zip(x_refs, seg_widths)):
            res_ref = res_refs.pop(0) if has_res[idx] else None
            off = 0
            for w in widths:
                xv = x_ref[:, off:off + w].astype(F32)
                r = lax.rsqrt(jnp.mean(xv * xv, axis=1, keepdims=True) + NORM_EPS)
                xh = xv * r
                dhv = dh_ref[:, col:col + w].astype(F32)
                gd = dhv * g_ref[:, col:col + w]
                dx = r * (gd - xh * jnp.mean(gd * xh, axis=1, keepdims=True))
                if res_ref is not None:
                    dx = dx + res_ref[:, off:off + w]
                dx_refs[idx][:, off:off + w] = dx
                if bf16_copy:
                    copy_refs[idx][:, off:off + w] = dx.astype(BF16)
                part = jnp.sum(dhv * xh, axis=0, keepdims=True)

                @pl.when(first)
                def _(part=part, col=col, w=w):
                    dg_ref[:, col:col + w] = part

                @pl.when(jnp.logical_not(first))
                def _(part=part, col=col, w=w):
                    dg_ref[:, col:col + w] += part
                off += w
                col += w

    outs = pl.pallas_call(
        body, name=name, grid=(t_len // tm,),
        out_shape=[jax.ShapeDtypeStruct(x.shape, F32) for x in xs] + [jax.ShapeDtypeStruct((1, width), F32)]
        + ([jax.ShapeDtypeStruct(x.shape, BF16) for x in xs] if bf16_copy else []),
        in_specs=[pl.BlockSpec((tm, x.shape[1]), lambda i: (i, 0)) for x in xs]
        + [pl.BlockSpec((1, width), lambda i: (0, 0)), pl.BlockSpec((tm, width), lambda i: (i, 0))]
        + [pl.BlockSpec((tm, r.shape[1]), lambda i: (i, 0)) for r in res_ops],
        out_specs=[pl.BlockSpec((tm, x.shape[1]), lambda i: (i, 0)) for x in xs]
        + [pl.BlockSpec((1, width), lambda i: (0, 0))]
        + ([pl.BlockSpec((tm, x.shape[1]), lambda i: (i, 0)) for x in xs] if bf16_copy else []),
        compiler_params=_params("arbitrary"),
    )(*xs, g, dh, *res_ops)
    return outs[:n], outs[n], outs[n + 1:]


def _loss_head(x, g, target, name, tm=256):
    t_len, d = x.shape
    tm = _pick(t_len, tm, 8)

    def body(x_ref, g_ref, t_ref, dx_ref, dg_ref, loss_ref, dxb_ref):
        first = pl.program_id(0) == 0
        xv = x_ref[...]
        r = lax.rsqrt(jnp.mean(xv * xv, axis=1, keepdims=True) + NORM_EPS)
        xh = xv * r
        gv = g_ref[...]
        err = xh * gv - t_ref[...]
        part_loss = 0.5 * jnp.sum(jnp.mean(err * err, axis=1, keepdims=True), axis=0, keepdims=True)
        dy = err * (1.0 / d)
        gd = dy * gv
        dx = r * (gd - xh * jnp.mean(gd * xh, axis=1, keepdims=True))
        dx_ref[...] = dx
        dxb_ref[...] = dx.astype(BF16)
        part_g = jnp.sum(dy * xh, axis=0, keepdims=True)
        part_loss = jnp.broadcast_to(part_loss, (1, LANES))

        @pl.when(first)
        def _():
            dg_ref[...] = part_g
            loss_ref[...] = part_loss

        @pl.when(jnp.logical_not(first))
        def _():
            dg_ref[...] += part_g
            loss_ref[...] += part_loss

    return pl.pallas_call(
        body, name=name, grid=(t_len // tm,),
        out_shape=[jax.ShapeDtypeStruct((t_len, d), F32), jax.ShapeDtypeStruct((1, d), F32),
                   jax.ShapeDtypeStruct((1, LANES), F32), jax.ShapeDtypeStruct((t_len, d), BF16)],
        in_specs=[pl.BlockSpec((tm, d), lambda i: (i, 0)), pl.BlockSpec((1, d), lambda i: (0, 0)),
                  pl.BlockSpec((tm, d), lambda i: (i, 0))],
        out_specs=[pl.BlockSpec((tm, d), lambda i: (i, 0)), pl.BlockSpec((1, d), lambda i: (0, 0)),
                   pl.BlockSpec((1, LANES), lambda i: (0, 0)), pl.BlockSpec((tm, d), lambda i: (i, 0))],
        compiler_params=_params("arbitrary"),
    )(x, g, target)


def _alibi_slope(h, n_heads):
    return jnp.exp(jnp.full((1, 1), -8.0 * math.log(2.0) / n_heads, F32) * (h + 1).astype(F32))


def _attn_tiles(d, w):
    return ATTN_BLOCK * d, (w if d == 1 else LANES)


def _residue_rows(r, d):
    return pl.ds(r, ATTN_BLOCK, stride=d) if d > 1 else pl.ds(0, ATTN_BLOCK)


def _attn_masks(first_block):
    i = lax.broadcasted_iota(jnp.int32, (2 * ATTN_BLOCK, 2 * ATTN_BLOCK), 0) % ATTN_BLOCK
    j = lax.broadcasted_iota(jnp.int32, (2 * ATTN_BLOCK, 2 * ATTN_BLOCK), 1)
    delta = i - j + ATTN_BLOCK
    valid = jnp.logical_and(delta >= 0, delta <= ATTN_BLOCK)
    valid = jnp.logical_and(valid, jnp.logical_or(j >= ATTN_BLOCK, jnp.logical_not(first_block)))
    return valid, delta.astype(F32)


def _stack_heads(x, masks):
    zero = jnp.zeros_like(x)
    return jnp.concatenate([jnp.where(masks[0], x, zero), jnp.where(masks[1], x, zero)], axis=0)


def _unstack_heads(x2, masks):
    return jnp.where(masks[0], x2[:ATTN_BLOCK], x2[ATTN_BLOCK:])


def _pair_slopes(first_head, p, n_heads, d):
    row = lax.broadcasted_iota(jnp.int32, (2 * ATTN_BLOCK, 1), 0)
    sa, sb = (_alibi_slope(first_head + 2 * p + hh, n_heads) * d for hh in range(2))
    return jnp.where(row < ATTN_BLOCK, sa, sb)


def _head_lane_masks():
    lane = lax.broadcasted_iota(jnp.int32, (ATTN_BLOCK, LANES), 1)
    return [lane < HEAD_DIM, lane >= HEAD_DIM]


def _attn_branch_fwd(proj, w, dilation, n_heads, name, rides=None):
    t_len = proj.shape[0]
    d = dilation
    rows, lw = _attn_tiles(d, w)
    nb = t_len // rows
    n_pairs = lw // LANES
    per = w // lw
    scale = HEAD_DIM ** -0.5

    def body(q_ref, kp_ref, kc_ref, vp_ref, vc_ref, o_ref, lse_ref):
        first_head = pl.program_id(0) * (2 * n_pairs)
        first_block = pl.program_id(1) == 0
        valid, delta = _attn_masks(first_block)
        masks = _head_lane_masks()
        ones = jnp.ones((2 * ATTN_BLOCK, LANES), BF16)
        for p in range(n_pairs):
            cols = pl.ds(p * LANES, LANES)
            bias = _pair_slopes(first_head, p, n_heads, d) * delta
            for r in range(d):
                rs = _residue_rows(r, d)
                q2 = _stack_heads((q_ref[rs, cols] * scale).astype(BF16), masks)
                k2 = jnp.concatenate([kp_ref[rs, cols], kc_ref[rs, cols]], axis=0).astype(BF16)
                v2 = jnp.concatenate([vp_ref[rs, cols], vc_ref[rs, cols]], axis=0).astype(BF16)
                s = jnp.where(valid, _dot(q2, k2, 1, 1) - bias, NEG_INF)
                m = jnp.max(s, axis=1, keepdims=True)
                pr = jnp.exp(s - m).astype(BF16)
                den = _dot(pr, ones, 1, 0)
                o_ref[rs, cols] = _unstack_heads(_dot(pr, v2, 1, 0) / den, masks)
                lse_ref[rs, cols] = _unstack_heads(m + jnp.log(den), masks)

    def spec(which, prev):
        if prev:
            return pl.BlockSpec((rows, lw), lambda b, n: (jnp.maximum(n - 1, 0), which * per + b))
        return pl.BlockSpec((rows, lw), lambda b, n: (n, which * per + b))

    o_spec = pl.BlockSpec((rows, lw), lambda b, n: (n, b))
    return _pallas(
        body, name=name, grid=(per, nb), out_shape=[jax.ShapeDtypeStruct((t_len, w), F32)] * 2,
        in_specs=[spec(0, False), spec(1, True), spec(1, False), spec(2, True), spec(2, False)],
        out_specs=[o_spec, o_spec], operands=[proj] * 5, semantics=("parallel", "parallel"), rides=rides)


def _attn_combine(outs, lses, name, tm=512):
    t_len, w = outs[0].shape
    tm = _pick(t_len, tm, 8)
    nbr = len(outs)

    def body(*refs):
        o_refs, l_refs = refs[:nbr], refs[nbr:2 * nbr]
        out_ref, lse_ref = refs[2 * nbr:]
        ls = [r[...] for r in l_refs]
        m = functools.reduce(jnp.maximum, ls)
        es = [jnp.exp(l - m) for l in ls]
        den = functools.reduce(lambda a, b: a + b, es)
        num = functools.reduce(lambda a, b: a + b, [e * r[...] for e, r in zip(es, o_refs)])
        out_ref[...] = num / den
        lse_ref[...] = m + jnp.log(den)

    spec = pl.BlockSpec((tm, w), lambda i: (i, 0))
    return pl.pallas_call(
        body, name=name, grid=(t_len // tm,),
        out_shape=[jax.ShapeDtypeStruct((t_len, w), F32)] * 2,
        in_specs=[spec] * (2 * nbr), out_specs=[spec, spec],
        compiler_params=_params("parallel"),
    )(*outs, *lses)


def _attn_branch_bwd(proj, w, out, lse, dout, dilation, n_heads, name, acc=None, rides=None):
    t_len = proj.shape[0]
    d = dilation
    rows, lw = _attn_tiles(d, w)
    nb = t_len // rows
    n_pairs = lw // LANES
    per = w // lw
    scale = HEAD_DIM ** -0.5
    n_acc = 0 if acc is None else 3

    def body(*refs):
        q_ref, kp_ref, kc_ref, vp_ref, vc_ref, out_ref, lse_ref, do_ref = refs[:8]
        acc_refs = refs[8:8 + n_acc]
        dq_ref, dk_ref, dv_ref, dk_carry, dv_carry = refs[8 + n_acc:]
        first_head = pl.program_id(0) * (2 * n_pairs)
        n = pl.program_id(1)
        first_block = n == 0
        valid, dist = _attn_masks(first_block)
        masks = _head_lane_masks()

        def plus(value, idx, *where):
            return value + acc_refs[idx][where] if n_acc else value

        @pl.when(first_block)
        def _():
            dk_carry[...] = jnp.zeros_like(dk_carry)
            dv_carry[...] = jnp.zeros_like(dv_carry)

        @pl.when(n < nb)
        def _():
            for p in range(n_pairs):
                cols = pl.ds(p * LANES, LANES)
                bias = _pair_slopes(first_head, p, n_heads, d) * dist
                for r in range(d):
                    rs = _residue_rows(r, d)
                    q2 = _stack_heads((q_ref[rs, cols] * scale).astype(BF16), masks)
                    k2 = jnp.concatenate([kp_ref[rs, cols], kc_ref[rs, cols]], axis=0).astype(BF16)
                    v2 = jnp.concatenate([vp_ref[rs, cols], vc_ref[rs, cols]], axis=0).astype(BF16)
                    do = do_ref[rs, cols]
                    do2 = _stack_heads(do.astype(BF16), masks)
                    do_out = do * out_ref[rs, cols]
                    lse_all = lse_ref[rs, cols]
                    delta = jnp.concatenate([jnp.sum(jnp.where(masks[hh], do_out, 0.0), axis=1, keepdims=True)
                                             for hh in range(2)], axis=0)
                    lse2 = jnp.concatenate([jnp.max(jnp.where(masks[hh], lse_all, NEG_INF), axis=1, keepdims=True)
                                            for hh in range(2)], axis=0)
                    s = jnp.where(valid, _dot(q2, k2, 1, 1) - bias, NEG_INF)
                    pr = jnp.exp(s - lse2)
                    ds = (pr * (_dot(do2, v2, 1, 1) - delta)).astype(BF16)
                    dq = _unstack_heads(_dot(ds, k2, 1, 0), masks)
                    dk2 = _dot(ds, q2, 0, 0)
                    dv2 = _dot(pr.astype(BF16), do2, 0, 0)
                    dq_ref[rs, cols] = plus(dq * scale, 0, rs, cols)
                    dk_ref[rs, cols] = plus(dk_carry[rs, cols] + dk2[:ATTN_BLOCK], 1, rs, cols)
                    dv_ref[rs, cols] = plus(dv_carry[rs, cols] + dv2[:ATTN_BLOCK], 2, rs, cols)
                    dk_carry[rs, cols] = dk2[ATTN_BLOCK:]
                    dv_carry[rs, cols] = dv2[ATTN_BLOCK:]

        @pl.when(n == nb)
        def _():
            dk_ref[...] = plus(dk_carry[...], 1, Ellipsis)
            dv_ref[...] = plus(dv_carry[...], 2, Ellipsis)

    def qkv_spec(which, shift):
        return pl.BlockSpec((rows, lw), lambda b, n: (jnp.clip(n - shift, 0, nb - 1), which * per + b))

    q_like = pl.BlockSpec((rows, lw), lambda b, n: (jnp.minimum(n, nb - 1), b))
    k_like = pl.BlockSpec((rows, lw), lambda b, n: (jnp.maximum(n - 1, 0), b))
    return _pallas(
        body, name=name, grid=(per, nb + 1), out_shape=[jax.ShapeDtypeStruct((t_len, w), F32)] * 3,
        in_specs=[qkv_spec(0, 0), qkv_spec(1, 1), qkv_spec(1, 0), qkv_spec(2, 1), qkv_spec(2, 0),
                  q_like, q_like, q_like] + [q_like, k_like, k_like][:n_acc],
        out_specs=[q_like, k_like, k_like], operands=[proj] * 5 + [out, lse, dout, *(acc or ())],
        scratch_shapes=[pltpu.VMEM((rows, lw), F32), pltpu.VMEM((rows, lw), F32)],
        semantics=("parallel", "arbitrary"), rides=rides)


def _shift_down(u, s):
    if s == 0:
        return u
    row = lax.broadcasted_iota(jnp.int32, u.shape, 0)
    return jnp.where(row >= s, pltpu.roll(u, s, 0), 0.0)


def _shift_up(u, s):
    if s == 0:
        return u
    n = u.shape[0]
    row = lax.broadcasted_iota(jnp.int32, u.shape, 0)
    return jnp.where(row < n - s, pltpu.roll(u, n - s, 0), 0.0)


def _conv_fwd(u, col0, w, b, name):
    t_len, ch = u.shape[0], w.shape[1]
    blk0 = col0 // LANES

    def body(u_ref, w_ref, b_ref, o_ref):
        uv = u_ref[...]
        pre = b_ref[...] + jnp.zeros_like(uv)
        for k in range(SSD_CONV):
            pre = pre + w_ref[k:k + 1, :] * _shift_down(uv, SSD_CONV - 1 - k)
        o_ref[...] = pre * jax.nn.sigmoid(pre)

    return pl.pallas_call(
        body, name=name, grid=(ch // LANES,),
        out_shape=jax.ShapeDtypeStruct((t_len, ch), F32),
        in_specs=[pl.BlockSpec((t_len, LANES), lambda j: (0, blk0 + j)),
                  pl.BlockSpec((SSD_CONV, LANES), lambda j: (0, j)), pl.BlockSpec((1, LANES), lambda j: (0, j))],
        out_specs=pl.BlockSpec((t_len, LANES), lambda j: (0, j)),
        compiler_params=_params("parallel"),
    )(u, w, b)


def _conv_bwd(u, col0, w, b, dact, name):
    t_len, ch = u.shape[0], w.shape[1]
    blk0 = col0 // LANES

    def body(u_ref, w_ref, b_ref, da_ref, du_ref, dw_ref, db_ref):
        uv = u_ref[...]
        shifted = [_shift_down(uv, SSD_CONV - 1 - k) for k in range(SSD_CONV)]
        pre = b_ref[...] + jnp.zeros_like(uv)
        for k in range(SSD_CONV):
            pre = pre + w_ref[k:k + 1, :] * shifted[k]
        sig = jax.nn.sigmoid(pre)
        dpre = da_ref[...] * (sig * (1.0 + pre * (1.0 - sig)))
        du = jnp.zeros_like(uv)
        for k in range(SSD_CONV):
            du = du + w_ref[k:k + 1, :] * _shift_up(dpre, SSD_CONV - 1 - k)
            dw_ref[k:k + 1, :] = jnp.sum(dpre * shifted[k], axis=0, keepdims=True)
        du_ref[...] = du
        db_ref[...] = jnp.sum(dpre, axis=0, keepdims=True)

    col = pl.BlockSpec((t_len, LANES), lambda j: (0, j))
    w_spec = pl.BlockSpec((SSD_CONV, LANES), lambda j: (0, j))
    b_spec = pl.BlockSpec((1, LANES), lambda j: (0, j))
    return pl.pallas_call(
        body, name=name, grid=(ch // LANES,),
        out_shape=[jax.ShapeDtypeStruct((t_len, ch), F32), jax.ShapeDtypeStruct((SSD_CONV, ch), F32),
                   jax.ShapeDtypeStruct((1, ch), F32)],
        in_specs=[pl.BlockSpec((t_len, LANES), lambda j: (0, blk0 + j)), w_spec, b_spec, col],
        out_specs=[col, w_spec, b_spec],
        compiler_params=_params("parallel"),
    )(u, w, b, dact)


def _cumsum_rows(v):
    n = v.shape[0]
    row = lax.broadcasted_iota(jnp.int32, v.shape, 0)
    s = 1
    while s < n:
        v = v + jnp.where(row >= s, pltpu.roll(v, s, 0), 0.0)
        s *= 2
    return v


def _rev_cumsum_rows(v):
    n = v.shape[0]
    row = lax.broadcasted_iota(jnp.int32, v.shape, 0)
    s = 1
    while s < n:
        v = v + jnp.where(row < n - s, pltpu.roll(v, n - s, 0), 0.0)
        s *= 2
    return v


def _head_selector(heads, width):
    j = lax.broadcasted_iota(jnp.int32, (LANES, width), 0)
    lane = lax.broadcasted_iota(jnp.int32, (LANES, width), 1)
    return jnp.where(jnp.logical_and(lane // HEAD_DIM == j, j < heads), 1.0, 0.0).astype(BF16)


class _SsdChunk:
    def __init__(self, dtraw_ref, bias_ref, alog_ref, xs_ref, b_ref, c_ref, heads):
        q = SSD_CHUNK
        width = heads * HEAD_DIM
        lane = lax.broadcasted_iota(jnp.int32, (q, LANES), 1)
        self.head_lanes = lane < heads
        lane1 = lax.broadcasted_iota(jnp.int32, (1, LANES), 1)
        self.a = jnp.where(lane1 < heads, -jnp.exp(alog_ref[...]), 0.0)
        self.dt_arg = dtraw_ref[...] + bias_ref[...]
        self.dt = jnp.where(self.head_lanes, jax.nn.softplus(self.dt_arg), 0.0)
        self.cum = _cumsum_rows(self.dt * self.a)
        self.cum_t = self.cum.T
        last = self.cum[q - 1:q, :]
        self.sel = _head_selector(heads, width)
        self.expand = lambda v: _dot_exact(v, self.sel, 1, 0)
        self.segsum = lambda v: _dot_exact(v, self.sel, 1, 1)
        self.e_exp = self.expand(jnp.exp(self.cum))
        self.d_exp = self.expand(jnp.exp(last - self.cum))
        self.elast_exp = self.e_exp[q - 1:q, :]
        self.dt_exp = self.expand(self.dt)
        self.xs = xs_ref[...]
        self.x = self.xs * self.dt_exp
        self.xb = self.x.astype(BF16)
        self.bb = b_ref[...].astype(BF16)
        self.cb = c_ref[...].astype(BF16)
        self.cbm = _dot(self.cb, self.bb, 1, 1)
        li = lax.broadcasted_iota(jnp.int32, (q, q), 0)
        si = lax.broadcasted_iota(jnp.int32, (q, q), 1)
        self.tri = li >= si
        hl = lax.broadcasted_iota(jnp.int32, (q, LANES), 1)
        self.pair_masks = [hl < HEAD_DIM, hl >= HEAD_DIM]

    def decay(self, j):
        diff = self.cum[:, j:j + 1] - self.cum_t[j:j + 1, :]
        return jnp.exp(jnp.where(self.tri, diff, NEG_INF))


def _ssd_specs(t_len, heads, n_chunks, xbc_cols, rev):
    q, gw = SSD_CHUNK, heads * HEAD_DIM
    ssd_w = SSD_GROUPS * gw
    b_blk = ssd_w // SSD_STATE
    ch = (lambda c: n_chunks - 1 - c) if rev else (lambda c: c)
    return dict(
        dtraw=pl.BlockSpec((None, q, LANES), lambda g, c: (g, ch(c), 0)),
        small=pl.BlockSpec((None, 1, LANES), lambda g, c: (g, 0, 0)),
        dsk=pl.BlockSpec((None, 1, gw), lambda g, c: (g, 0, 0)),
        xs=pl.BlockSpec((q, gw), lambda g, c: (ch(c), g)),
        b=pl.BlockSpec((q, SSD_STATE), lambda g, c: (ch(c), b_blk + g)),
        c=pl.BlockSpec((q, SSD_STATE), lambda g, c: (ch(c), b_blk + SSD_GROUPS + g)),
        z=pl.BlockSpec((q, gw), lambda g, c: (ch(c), 3 * SSD_GROUPS + g)),
        tok=pl.BlockSpec((q, gw), lambda g, c: (ch(c), g)),
        state=pl.BlockSpec((None, SSD_STATE, gw), lambda g, c: (ch(c), 0, g)),
        bc=pl.BlockSpec((q, SSD_STATE), lambda g, c: (ch(c), g)),
    )


def _ssd_fwd(xbc, qkvz, dtraw_g, bias_g, alog_g, dsk_exp, heads, name):
    t_len = xbc.shape[0]
    q, gw = SSD_CHUNK, heads * HEAD_DIM
    n_chunks = t_len // q
    ssd_w = SSD_GROUPS * gw
    sp = _ssd_specs(t_len, heads, n_chunks, xbc.shape[1], rev=False)

    def body(dtraw_ref, bias_ref, alog_ref, dsk_ref, xs_ref, b_ref, c_ref, z_ref,
             yg_ref, ypre_ref, st_ref, s_scr):
        @pl.when(pl.program_id(1) == 0)
        def _():
            s_scr[...] = jnp.zeros_like(s_scr)

        k = _SsdChunk(dtraw_ref, bias_ref, alog_ref, xs_ref, b_ref, c_ref, heads)
        s_prev = s_scr[...]
        st_ref[...] = s_prev
        y_off = k.e_exp * _dot(k.cb, s_prev.astype(BF16), 1, 0)
        parts = []
        for p in range(heads // 2):
            xp = k.xb[:, p * LANES:(p + 1) * LANES]
            acc = jnp.zeros((q, LANES), F32)
            for hh in range(2):
                m = (k.cbm * k.decay(2 * p + hh)).astype(BF16)
                acc = acc + _dot(m, jnp.where(k.pair_masks[hh], xp, jnp.zeros_like(xp)), 1, 0)
            parts.append(acc)
        y = jnp.concatenate(parts, axis=1) + y_off
        xd = (k.x * k.d_exp).astype(BF16)
        s_scr[...] = k.elast_exp * s_prev + _dot(k.bb, xd, 0, 0)
        y_pre = y + dsk_ref[...] * k.xs
        zv = z_ref[...]
        ypre_ref[...] = y_pre
        yg_ref[...] = y_pre * (zv * jax.nn.sigmoid(zv))

    return pl.pallas_call(
        body, name=name, grid=(SSD_GROUPS, n_chunks),
        out_shape=[jax.ShapeDtypeStruct((t_len, ssd_w), F32), jax.ShapeDtypeStruct((t_len, ssd_w), F32),
                   jax.ShapeDtypeStruct((n_chunks, SSD_STATE, ssd_w), F32)],
        in_specs=[sp["dtraw"], sp["small"], sp["small"], sp["dsk"], sp["xs"], sp["b"], sp["c"], sp["z"]],
        out_specs=[sp["tok"], sp["tok"], sp["state"]],
        scratch_shapes=[pltpu.VMEM((SSD_STATE, gw), F32)],
        compiler_params=_params("parallel", "arbitrary"),
    )(dtraw_g, bias_g, alog_g, dsk_exp, xbc, xbc, xbc, qkvz)


def _ssd_bwd(xbc, qkvz, dtraw_g, bias_g, alog_g, dsk_exp, ypre, states, dyg, heads, name, rides=None):
    t_len = xbc.shape[0]
    q, gw = SSD_CHUNK, heads * HEAD_DIM
    n_chunks = t_len // q
    ssd_w = SSD_GROUPS * gw
    sp = _ssd_specs(t_len, heads, n_chunks, xbc.shape[1], rev=True)

    def body(dtraw_ref, bias_ref, alog_ref, dsk_ref, xs_ref, b_ref, c_ref, z_ref, ypre_ref, st_ref, dyg_ref,
             dxs_ref, db_ref, dc_ref, dz_ref, ddt_ref, small_ref, g_scr):
        first = pl.program_id(1) == 0

        @pl.when(first)
        def _():
            g_scr[...] = jnp.zeros_like(g_scr)

        k = _SsdChunk(dtraw_ref, bias_ref, alog_ref, xs_ref, b_ref, c_ref, heads)
        zv = z_ref[...]
        sig = jax.nn.sigmoid(zv)
        dyg = dyg_ref[...]
        y_pre = ypre_ref[...]
        dy = dyg * (zv * sig)
        dz_ref[...] = dyg * y_pre * (sig * (1.0 + zv * (1.0 - sig)))
        dsk = dsk_ref[...]
        g_next = g_scr[...]
        s_prev = st_ref[...]
        sb = s_prev.astype(BF16)
        xd = k.x * k.d_exp
        xdb = xd.astype(BF16)
        gb = g_next.astype(BF16)
        dx_off = k.d_exp * _dot(k.bb, gb, 1, 0)
        dyb = dy.astype(BF16)
        dcb = jnp.zeros((q, q), F32)
        lane = lax.broadcasted_iota(jnp.int32, (q, LANES), 1)
        row_t = lax.broadcasted_iota(jnp.int32, (LANES, q), 0)
        w_rows = jnp.zeros((q, LANES), F32)
        w_cols_t = jnp.zeros((LANES, q), F32)
        parts = []
        for p in range(heads // 2):
            cols = slice(p * LANES, (p + 1) * LANES)
            dyp, xp = dyb[:, cols], k.xb[:, cols]
            acc = jnp.zeros((q, LANES), F32)
            for hh in range(2):
                j = 2 * p + hh
                lm = k.decay(j)
                m32 = k.cbm * lm
                dym = jnp.where(k.pair_masks[hh], dyp, jnp.zeros_like(dyp))
                acc = acc + _dot(m32.astype(BF16), dym, 0, 0)
                dm = _dot(dym, xp, 1, 1)
                dcb = dcb + dm * lm
                wmat = dm * m32
                w_rows = w_rows + jnp.where(lane == j, jnp.sum(wmat, axis=1, keepdims=True), 0.0)
                w_cols_t = w_cols_t + jnp.where(row_t == j, jnp.sum(wmat, axis=0, keepdims=True), 0.0)
            parts.append(acc)
        dx = jnp.concatenate(parts, axis=1) + dx_off
        dcbb = dcb.astype(BF16)
        edy = (k.e_exp * dy).astype(BF16)
        dc_ref[...] = _dot(dcbb, k.bb, 1, 0) + _dot(edy, sb, 1, 1)
        db_ref[...] = _dot(dcbb, k.cb, 0, 0) + _dot(xdb, gb, 1, 1)
        g_scr[...] = k.elast_exp * g_next + _dot(k.cb, edy, 0, 0)

        y_off = k.e_exp * _dot(k.cb, sb, 1, 0)
        dcum = w_rows - w_cols_t.T + k.segsum(dy * y_off)
        t_term = k.segsum(k.x * dx_off)
        gs = jnp.broadcast_to(jnp.sum(g_next * s_prev, axis=0, keepdims=True), (8, gw))
        carried = k.segsum(gs)[0:1, :] * jnp.exp(k.cum[q - 1:q, :])
        dda = _rev_cumsum_rows(dcum) + (_cumsum_rows(t_term) - t_term) + carried
        ddt = jnp.where(k.head_lanes, dda * k.a + k.segsum(dx * k.xs), 0.0)
        ddtraw = ddt * jax.nn.sigmoid(k.dt_arg)
        ddt_ref[...] = ddtraw
        dxs_ref[...] = dx * k.dt_exp + dsk * dy
        ds = jnp.broadcast_to(jnp.sum(dy * k.xs, axis=0, keepdims=True), (8, gw))
        d_alog = jnp.sum(jnp.where(k.head_lanes, dda * k.dt, 0.0), axis=0, keepdims=True) * k.a
        rows8 = lax.broadcasted_iota(jnp.int32, (8, LANES), 0)
        small = jnp.where(rows8 == 0, d_alog, 0.0)
        small = small + jnp.where(rows8 == 1, jnp.sum(ddtraw, axis=0, keepdims=True), 0.0)
        small = small + jnp.where(rows8 == 2, k.segsum(ds)[0:1, :], 0.0)

        @pl.when(first)
        def _():
            small_ref[...] = small

        @pl.when(jnp.logical_not(first))
        def _():
            small_ref[...] += small

    bc_out = sp["bc"]
    return _pallas(
        body, name=name, grid=(SSD_GROUPS, n_chunks),
        out_shape=[jax.ShapeDtypeStruct((t_len, ssd_w), F32),
                   jax.ShapeDtypeStruct((t_len, SSD_GROUPS * SSD_STATE), F32),
                   jax.ShapeDtypeStruct((t_len, SSD_GROUPS * SSD_STATE), F32),
                   jax.ShapeDtypeStruct((t_len, ssd_w), F32),
                   jax.ShapeDtypeStruct((SSD_GROUPS, t_len, LANES), F32),
                   jax.ShapeDtypeStruct((SSD_GROUPS, 8, LANES), F32)],
        in_specs=[sp["dtraw"], sp["small"], sp["small"], sp["dsk"], sp["xs"], sp["b"], sp["c"], sp["z"],
                  sp["tok"], sp["state"], sp["tok"]],
        out_specs=[sp["tok"], bc_out, bc_out, sp["tok"], sp["dtraw"],
                   pl.BlockSpec((None, 8, LANES), lambda g, c: (g, 0, 0))],
        operands=[dtraw_g, bias_g, alog_g, dsk_exp, xbc, xbc, xbc, qkvz, ypre, states, dyg],
        scratch_shapes=[pltpu.VMEM((SSD_STATE, gw), F32)], semantics=("parallel", "arbitrary"), rides=rides)


def _adamw(w, g, m, v, name):
    n_lead, rows, lanes = w.shape
    tr = _row_tile(rows, lanes, 4, 14)
    c1 = 1.0 / (1.0 - ADAM_B1 ** ADAM_STEP)
    c2 = 1.0 / (1.0 - ADAM_B2 ** ADAM_STEP)

    def body(w_ref, g_ref, m_ref, v_ref, d_ref, nm_ref, nv_ref):
        gv = g_ref[...]
        nm = ADAM_B1 * m_ref[...] + (1.0 - ADAM_B1) * gv
        nv = ADAM_B2 * v_ref[...] + (1.0 - ADAM_B2) * (gv * gv)
        nm_ref[...] = nm
        nv_ref[...] = nv
        d_ref[...] = -ADAM_LR * ((nm * c1) / (jnp.sqrt(nv * c2) + ADAM_EPS) + ADAM_WD * w_ref[...])

    spec = pl.BlockSpec((None, tr, lanes), lambda l, i: (l, i, 0))
    return pl.pallas_call(
        body, name=name, grid=(n_lead, rows // tr),
        out_shape=[jax.ShapeDtypeStruct(w.shape, F32)] * 3,
        in_specs=[spec] * 4, out_specs=[spec] * 3,
        compiler_params=_params("parallel", "parallel"),
    )(w, g, m, v)


def _pad_lanes(a, width=LANES):
    return jnp.pad(a, ((0, 0), (0, width - a.shape[1])))


def _group_pad(v, heads):
    return _pad_lanes(v.reshape(SSD_GROUPS, heads))[:, None, :]


def _layer_fwd(x0, p, wt, dims, tag, rides):
    w_attn, heads_g, n_heads, conv_ch = dims["w_attn"], dims["heads_g"], dims["n_heads"], dims["conv_ch"]
    h1 = _rmsnorm_fwd([x0], [[x0.shape[1]]], p["ln1_g"], f"ln1_fwd{tag}")
    proj = _mm(h1, wt("w_in"), name=f"in_proj{tag}", tn=640, rides=rides)

    outs, lses = [], []
    for d in BRANCH_DILATIONS:
        o, l = _attn_branch_fwd(proj, w_attn, d, n_heads, f"attn_fwd_d{d}{tag}", rides)
        outs.append(o)
        lses.append(l)
    attn, lse = _attn_combine(outs, lses, f"attn_combine{tag}")

    xbc = _conv_fwd(proj, 4 * w_attn, p["conv_w"], p["conv_b"], f"conv_fwd{tag}")
    dt_col = 4 * w_attn + conv_ch
    dtraw_g = jnp.stack([_pad_lanes(proj[:, dt_col + g * heads_g:dt_col + (g + 1) * heads_g])
                         for g in range(SSD_GROUPS)])
    bias_g, alog_g = _group_pad(p["dt_bias"], heads_g), _group_pad(p["a_log"], heads_g)
    dsk_exp = jnp.repeat(p["d_skip"], HEAD_DIM).reshape(SSD_GROUPS, 1, heads_g * HEAD_DIM)
    yg, ypre, states = _ssd_fwd(xbc, proj, dtraw_g, bias_g, alog_g, dsk_exp, heads_g, f"ssd_fwd{tag}")

    gw = heads_g * HEAD_DIM
    mix_g = jnp.concatenate([p["attn_norm_g"], p["ssd_norm_g"]])[None, :]
    mix = _rmsnorm_fwd([attn, yg], [[w_attn], [gw] * SSD_GROUPS], mix_g, f"mix_norm_fwd{tag}")
    x1 = _mm(mix, wt("w_out"), name=f"out_proj{tag}", residual=x0, rides=rides)
    h2 = _rmsnorm_fwd([x1], [[x1.shape[1]]], p["ln2_g"], f"ln2_fwd{tag}")
    u = _mm(h2, wt("w_mlp_in"), name=f"mlp_in{tag}", out_dtype=BF16, tn=1024, rides=rides)
    x2 = _mm(u, wt("w_mlp_out"), name=f"mlp_out{tag}", a_act="relu2", residual=x1, tm=512, rides=rides)
    saved = dict(x0=x0, h1=h1, proj=proj, attn=attn, lse=lse, xbc=xbc, dtraw_g=dtraw_g,
                 bias_g=bias_g, alog_g=alog_g, dsk_exp=dsk_exp, yg=yg, ypre=ypre, states=states, mix=mix,
                 mix_g=mix_g, x1=x1, h2=h2, u=u)
    return x2, saved


def _pair_sums(ex, host, items):
    swapped = ex["rides"].done[("swap", host)]
    core = lax.axis_index("c").astype(jnp.int32).reshape(1)
    for i, (n, l) in enumerate(items):
        ex["pair"][(n, l)] = _pair_sum(ex["bufs"][(n, l)], swapped[i], core, f"pair_sum_{n}_l{l}")


def _layer_bwd(dx2, dx2_b, p, wt, s, dims, l, ex, copy_dx0):
    w_attn, heads_g, n_heads, conv_ch = dims["w_attn"], dims["heads_g"], dims["n_heads"], dims["conv_ch"]
    t_len, d_model = dx2.shape
    gw = heads_g * HEAD_DIM
    h_ssd = heads_g * SSD_GROUPS
    tag, rides, bufs = f"_l{l}", ex["rides"], ex["bufs"]
    du = _mm(dx2_b, wt("w_mlp_out"), name=f"mlp_out_dx{tag}", tb=True, gate=s["u"], out_dtype=BF16, tn=1024,
             rides=rides)
    d_wmo = _mm(s["u"], dx2_b, name=f"mlp_out_dw{tag}", ta=True, a_act="relu2", tm=512, tn=1024, out_dtype=BF16)
    bufs[("w_mlp_out", l)] = d_wmo.reshape(N_DEV, -1, d_model)
    bufs[("w_mlp_in", l)] = _mm(s["h2"], du, name=f"mlp_in_dw{tag}", ta=True, tm=512, tn=1024, out_dtype=BF16,
                                out_chunk=du.shape[1] // N_DEV)
    dh2 = _mm(du, wt("w_mlp_in"), name=f"mlp_in_dx{tag}", tb=True, tm=512, rides=rides)
    _pair_sums(ex, f"mlp_in_dx{tag}", [("w_mlp_out", l), ("w_mlp_in", l)])
    (dx1,), d_ln2, (dx1_b,) = _rmsnorm_bwd([s["x1"]], [[d_model]], p["ln2_g"], dh2, [dx2], f"ln2_bwd{tag}",
                                           bf16_copy=True)
    dmix = _mm(dx1_b, wt("w_out"), name=f"out_proj_dx{tag}", tb=True)
    d_wo = _mm(s["mix"], dx1_b, name=f"out_proj_dw{tag}", ta=True, tm=512, tn=1024, out_dtype=BF16)
    bufs[("w_out", l)] = d_wo.reshape(N_DEV, -1, d_model)
    after_branch = {BRANCH_DILATIONS[0]: [("w_out", l)]}
    (dattn, dyg), d_mix_g, _ = _rmsnorm_bwd([s["attn"], s["yg"]], [[w_attn], [gw] * SSD_GROUPS], s["mix_g"], dmix,
                                           [None, None], f"mix_norm_bwd{tag}")
    dxs, db, dc, dz, ddtraw_g, ssd_small = _ssd_bwd(
        s["xbc"], s["proj"], s["dtraw_g"], s["bias_g"], s["alog_g"], s["dsk_exp"], s["ypre"], s["states"], dyg,
        heads_g, f"ssd_bwd{tag}", rides)
    dxbc = jnp.concatenate([dxs, db, dc], axis=1)
    dxbc_raw, d_conv_w, d_conv_b = _conv_bwd(s["proj"], 4 * w_attn, p["conv_w"], p["conv_b"], dxbc, f"conv_bwd{tag}")
    acc = None
    for d in BRANCH_DILATIONS:
        acc = _attn_branch_bwd(s["proj"], w_attn, s["attn"], s["lse"], dattn, d, n_heads, f"attn_bwd_d{d}{tag}", acc,
                               rides)
        if d in after_branch:
            _pair_sums(ex, f"attn_bwd_d{d}{tag}", after_branch[d])
    w_in = wt("w_in")
    in_proj = 4 * w_attn + conv_ch + h_ssd
    pad = jnp.zeros((t_len, w_in.shape[1] - in_proj), F32)
    dproj = jnp.concatenate([*acc, dz, dxbc_raw] + [ddtraw_g[g, :, :heads_g] for g in range(SSD_GROUPS)] + [pad],
                            axis=1).astype(BF16)
    d_win = _mm(s["h1"], dproj, name=f"in_proj_dw{tag}", ta=True, tm=512, tn=1152, out_dtype=BF16, rides=rides)
    bufs[("w_in", l)] = d_win[:, :in_proj].reshape(d_model, N_DEV, -1).transpose(1, 0, 2)
    dh1 = _mm(dproj, w_in, name=f"in_proj_dx{tag}", tb=True, tm=512, rides=rides)
    _pair_sums(ex, f"in_proj_dx{tag}", [("w_in", l)])
    (dx0,), d_ln1, dx0_b = _rmsnorm_bwd([s["x0"]], [[d_model]], p["ln1_g"], dh1, [dx1], f"ln1_bwd{tag}",
                                        bf16_copy=copy_dx0)

    small = ssd_small[:, :, :heads_g]
    grads = dict(
        ln1_g=d_ln1[0], conv_w=d_conv_w, conv_b=d_conv_b[0],
        a_log=small[:, 0].reshape(h_ssd), dt_bias=small[:, 1].reshape(h_ssd), d_skip=small[:, 2].reshape(h_ssd),
        attn_norm_g=d_mix_g[0, :w_attn], ssd_norm_g=d_mix_g[0, w_attn:], ln2_g=d_ln2[0])
    return dx0, (dx0_b[0] if copy_dx0 else None), grads


_SMALL = ["ln1_g", "conv_w", "conv_b", "dt_bias", "a_log", "d_skip", "attn_norm_g", "ssd_norm_g", "ln2_g"]
_WEIGHTS = ["ln1_g", "w_in", "conv_w", "conv_b", "dt_bias", "a_log", "d_skip", "attn_norm_g", "ssd_norm_g",
            "w_out", "ln2_g", "w_mlp_in", "w_mlp_out", "final_norm_g"]


def _to_rows(a):
    flat = a.reshape(-1)
    rows = -(-flat.shape[0] // LANES)
    rows = -(-rows // 8) * 8
    return jnp.pad(flat, (0, rows * LANES - flat.shape[0])).reshape(rows, LANES)


def kernel(x, ln1_g, w_in, conv_w, conv_b, dt_bias, a_log, d_skip, attn_norm_g, ssd_norm_g, w_out, ln2_g, w_mlp_in, w_mlp_out, final_norm_g, loss_target, m_ln1_g, m_w_in, m_conv_w, m_conv_b, m_dt_bias, m_a_log, m_d_skip, m_attn_norm_g, m_ssd_norm_g, m_w_out, m_ln2_g, m_w_mlp_in, m_w_mlp_out, m_final_norm_g, v_ln1_g, v_w_in, v_conv_w, v_conv_b, v_dt_bias, v_a_log, v_d_skip, v_attn_norm_g, v_ssd_norm_g, v_w_out, v_ln2_g, v_w_mlp_in, v_w_mlp_out, v_final_norm_g):
    w = dict(ln1_g=ln1_g, w_in=w_in, conv_w=conv_w, conv_b=conv_b, dt_bias=dt_bias, a_log=a_log, d_skip=d_skip,
             attn_norm_g=attn_norm_g, ssd_norm_g=ssd_norm_g, w_out=w_out, ln2_g=ln2_g, w_mlp_in=w_mlp_in,
             w_mlp_out=w_mlp_out, final_norm_g=final_norm_g)
    mom = dict(ln1_g=m_ln1_g, w_in=m_w_in, conv_w=m_conv_w, conv_b=m_conv_b, dt_bias=m_dt_bias, a_log=m_a_log,
               d_skip=m_d_skip, attn_norm_g=m_attn_norm_g, ssd_norm_g=m_ssd_norm_g, w_out=m_w_out, ln2_g=m_ln2_g,
               w_mlp_in=m_w_mlp_in, w_mlp_out=m_w_mlp_out, final_norm_g=m_final_norm_g)
    var = dict(ln1_g=v_ln1_g, w_in=v_w_in, conv_w=v_conv_w, conv_b=v_conv_b, dt_bias=v_dt_bias, a_log=v_a_log,
               d_skip=v_d_skip, attn_norm_g=v_attn_norm_g, ssd_norm_g=v_ssd_norm_g, w_out=v_w_out, ln2_g=v_ln2_g,
               w_mlp_in=v_w_mlp_in, w_mlp_out=v_w_mlp_out, final_norm_g=v_final_norm_g)

    depth, d_model = ln1_g.shape
    t_len = x.shape[1]
    w_attn = attn_norm_g.shape[1]
    h_ssd = dt_bias.shape[1]
    conv_ch = conv_b.shape[1]
    in_proj = w_in.shape[2] * N_DEV
    assert ssd_norm_g.shape[1] == w_attn and in_proj == 4 * w_attn + conv_ch + h_ssd
    assert t_len % (BRANCH_DILATIONS[-1] * ATTN_BLOCK) == 0 and h_ssd % (2 * SSD_GROUPS) == 0
    dims = dict(w_attn=w_attn, heads_g=h_ssd // SSD_GROUPS, n_heads=w_attn // HEAD_DIM, conv_ch=conv_ch)
    names = ["w_in", "w_out", "w_mlp_in", "w_mlp_out"]

    rides = _Rides()
    ex = dict(rides=rides, bufs={}, pair={})
    latest, sent = {}, {}

    def shard(n, l):
        return w[n][l].astype(BF16)

    def half(rows, part):
        return None if part is None else (part * (rows // 2), rows // 2)

    def plan_spread(host, n, l, part=None):
        key, prev = ("spread", host, n, l, part), latest.get((n, l))
        rides.put(host, key, lambda: _GatherSpread([shard(n, l)], rows=half(w[n].shape[1], part),
                                                   into=[rides.done[prev[0]][prev[1]]] if prev else None))
        latest[(n, l)] = (key, 0)

    def plan_pass(host, items):
        key, srcs = ("pass", host), [latest[it] for it in items]
        rides.put(host, key, lambda: _GatherPass([rides.done[k][i] for k, i in srcs]))
        for i, it in enumerate(items):
            latest[it] = (key, i)

    def plan_swap(host, items):
        rides.put(host, ("swap", host), lambda: _SiblingSwap([ex["bufs"][it] for it in items]))

    def plan_send(host, n, l, part=None):
        key, prev = ("send", host, n, l, part), sent.get((n, l))
        rides.put(host, key, lambda: _ChipSend([ex["pair"][(n, l)]], rows=half(ex["pair"][(n, l)].shape[1], part),
                                               into=[rides.done[prev[0]][prev[1]]] if prev else None))
        sent[(n, l)] = (key, 0)

    d_first, d_mid, d_last = (f"d{d}" for d in BRANCH_DILATIONS)
    for l in range(depth):
        t = f"_l{l}"
        if l == 0:
            plan_spread(f"in_proj{t}", "w_out", 0)
            plan_spread(f"in_proj{t}", "w_mlp_in", 0, 0)
            plan_spread(f"attn_fwd_{d_first}{t}", "w_mlp_in", 0, 1)
            plan_spread(f"attn_fwd_{d_mid}{t}", "w_mlp_out", 0, 0)
            plan_pass(f"attn_fwd_{d_mid}{t}", [("w_out", 0), ("w_mlp_in", 0)])
            plan_spread(f"attn_fwd_{d_last}{t}", "w_mlp_out", 0, 1)
            plan_pass(f"out_proj{t}", [("w_mlp_out", 0)])
        else:
            plan_spread(f"in_proj{t}", "w_mlp_out", l, 0)
            plan_spread(f"attn_fwd_{d_first}{t}", "w_mlp_out", l, 1)
            plan_pass(f"attn_fwd_{d_mid}{t}", [("w_mlp_out", l)])
        if l + 1 < depth:
            plan_spread(f"out_proj{t}", "w_out", l + 1)
            plan_spread(f"mlp_in{t}", "w_in", l + 1)
            plan_spread(f"mlp_out{t}", "w_mlp_in", l + 1)
            plan_pass(f"pass_weights_l{l + 1}", [("w_out", l + 1), ("w_in", l + 1), ("w_mlp_in", l + 1)])
        plan_swap(f"mlp_in_dx{t}", [("w_mlp_out", l), ("w_mlp_in", l)])
        plan_send(f"ssd_bwd{t}", "w_mlp_out", l, 0)
        plan_send(f"attn_bwd_{d_first}{t}", "w_mlp_out", l, 1)
        plan_swap(f"attn_bwd_{d_first}{t}", [("w_out", l)])
        plan_send(f"attn_bwd_{d_mid}{t}", "w_mlp_in", l, 0)
        plan_send(f"attn_bwd_{d_last}{t}", "w_mlp_in", l, 1)
        plan_send(f"in_proj_dw{t}", "w_out", l)
        plan_swap(f"in_proj_dx{t}", [("w_in", l)])
        plan_send(f"mlp_out_dx_l{l - 1}" if l > 0 else "send_last_grads", "w_in", l)

    g_in0, g_cw = _all_gather([shard("w_in", 0), conv_w], "gather_first")
    full_cw = _with_own(g_cw, conv_w).transpose(1, 2, 0, 3).reshape(depth, SSD_CONV, conv_ch)
    proj_cols = -(-in_proj // LANES) * LANES
    full = {}

    def weight(n, l):
        if (n, l) not in full:
            if (n, l) == ("w_in", 0):
                g = g_in0
            else:
                key, i = latest[(n, l)]
                g = rides.done[key][i]
            g = _with_own(g, shard(n, l))
            if n == "w_in":
                g = _pad_lanes(g.transpose(1, 0, 2).reshape(d_model, in_proj), proj_cols)
            elif n == "w_mlp_in":
                g = g.transpose(1, 0, 2).reshape(d_model, -1)
            else:
                g = g.reshape(-1, d_model)
            full[(n, l)] = g
        return full[(n, l)]

    layers = [dict(ln1_g=ln1_g[l][None, :], ln2_g=ln2_g[l][None, :], conv_w=full_cw[l], conv_b=conv_b[l][None, :],
                   dt_bias=dt_bias[l], a_log=a_log[l], d_skip=d_skip[l], attn_norm_g=attn_norm_g[l],
                   ssd_norm_g=ssd_norm_g[l]) for l in range(depth)]

    h = x[0]
    saved = []
    for l in range(depth):
        h, s = _layer_fwd(h, layers[l], functools.partial(lambda n, l: weight(n, l), l=l), dims, f"_l{l}", rides)
        saved.append(s)
        if l + 1 < depth:
            _alone(rides, f"pass_weights_l{l + 1}")
    dh, d_final_g, loss_part, dh_b = _loss_head(h, final_norm_g[None, :], loss_target[0], "loss_head")

    grads = [None] * depth
    for l in reversed(range(depth)):
        dh, dh_b, grads[l] = _layer_bwd(dh, dh_b, layers[l], functools.partial(lambda n, l: weight(n, l), l=l),
                                        saved[l], dims, l, ex, copy_dx0=l > 0)
    grad_x = dh[None]
    _alone(rides, "send_last_grads")

    my_chip = (2 * lax.axis_index("x") + lax.axis_index("y")).astype(jnp.int32).reshape(1)
    gsum = {}
    for n in names:
        per_layer = []
        for l in range(depth):
            key, i = sent[(n, l)]
            per_layer.append(_sum_with_own(rides.done[key][i], ex["pair"][(n, l)], my_chip, f"sum_{n}_l{l}"))
        gsum[n] = jnp.stack(per_layer)

    small_parts = [jnp.stack([grads[l][n] for l in range(depth)]).reshape(-1) for n in _SMALL]
    small_parts += [d_final_g.reshape(-1), loss_part[0, :1]]
    sizes = [int(a.shape[0]) for a in small_parts]
    packed = _to_rows(jnp.concatenate(small_parts))
    (gathered,) = _all_gather([packed], "gather_small_grads")
    total = _sum_leading(_with_own(gathered, packed), "sum_small_grads").reshape(-1)
    offs = np.cumsum([0] + sizes)
    pieces = [total[offs[i]:offs[i + 1]] for i in range(len(sizes))]
    for n, piece in zip(_SMALL, pieces):
        shape = (depth, SSD_CONV, conv_ch) if n == "conv_w" else w[n].shape
        gsum[n] = piece.reshape(shape)
    gsum["final_norm_g"] = pieces[len(_SMALL)]
    loss = pieces[len(_SMALL) + 1][0]
    my_id = 4 * lax.axis_index("x") + 2 * lax.axis_index("y") + lax.axis_index("c")
    cw = conv_w.shape[2]
    gsum["conv_w"] = lax.dynamic_slice_in_dim(gsum["conv_w"], my_id * cw, cw, axis=2)

    delta, new_m, new_v = {}, {}, {}
    for n in names:
        delta[n], new_m[n], new_v[n] = _adamw(w[n], gsum[n], mom[n], var[n], f"adamw_{n}")
    small_names = [n for n in _WEIGHTS if n not in names]
    sm_sizes = [int(np.prod(w[n].shape)) for n in small_names]
    pack = lambda d: _to_rows(jnp.concatenate([d[n].reshape(-1) for n in small_names]))[None]
    outs = _adamw(pack(w), pack(gsum), pack(mom), pack(var), "adamw_small")
    sm_offs = np.cumsum([0] + sm_sizes)
    for res, o in zip((delta, new_m, new_v), outs):
        flat = o.reshape(-1)
        for i, n in enumerate(small_names):
            res[n] = flat[sm_offs[i]:sm_offs[i + 1]].reshape(w[n].shape)

    return (loss, grad_x, *[gsum[n] for n in _WEIGHTS], *[delta[n] for n in _WEIGHTS],
            *[new_m[n] for n in _WEIGHTS], *[new_v[n] for n in _WEIGHTS])
```

```python
import functools
import math

import numpy as np
import jax
import jax.numpy as jnp
from jax import lax
from jax.experimental import pallas as pl
from jax.experimental.pallas import tpu as pltpu

F32 = jnp.float32
BF16 = jnp.bfloat16

N_DEV = 8
LANES = 128
HEAD_DIM = 64
ATTN_BLOCK = 128
BRANCH_DILATIONS = (1, 4, 16)
SSD_GROUPS = 2
SSD_STATE = 128
SSD_CHUNK = 128
SSD_CONV = 4
NORM_EPS = 1e-5
ADAM_LR, ADAM_B1, ADAM_B2, ADAM_EPS, ADAM_WD, ADAM_STEP = 0.001, 0.9, 0.999, 1e-08, 0.01, 10
VMEM_LIMIT_BYTES = 56 * 1024 * 1024
MESH = pl.DeviceIdType.MESH
NEG_INF = float("-inf")


def _params(*sem):
    return pltpu.CompilerParams(dimension_semantics=tuple(sem), vmem_limit_bytes=VMEM_LIMIT_BYTES)


def _pick(n, target, mult):
    best = None
    for t in range(mult, min(n, target) + 1, mult):
        if n % t == 0:
            best = t
    assert best is not None, (n, target, mult)
    return best


def _dot(a, b, ca, cb):
    return lax.dot_general(a, b, (((ca,), (cb,)), ((), ())), preferred_element_type=F32)


def _split3(v):
    hi = v.astype(BF16)
    r = v - hi.astype(F32)
    mid = r.astype(BF16)
    lo = (r - mid.astype(F32)).astype(BF16)
    return hi, mid, lo


def _dot_exact(v, sel, ca, cb):
    hi, mid, lo = _split3(v)
    return _dot(hi, sel, ca, cb) + _dot(mid, sel, ca, cb) + _dot(lo, sel, ca, cb)


_HBM = pl.BlockSpec(memory_space=pltpu.HBM)


def _all_gather(xs, name):
    n = len(xs)

    def body(*refs):
        x_refs, o_refs = refs[:n], refs[n:2 * n]
        send_sems, recv_sems = refs[2 * n:]
        x, y, c = lax.axis_index("x"), lax.axis_index("y"), lax.axis_index("c")
        me, sibling = (x, y, c), (x, y, 1 - c)
        chips = [(1 - x, y), (x, 1 - y), (1 - x, 1 - y)]

        def copy(t, k, block, to, src=None):
            bx, by, bc = block
            dst = o_refs[t].at[4 * bx + 2 * by + bc]
            return pltpu.make_async_remote_copy(
                src_ref=dst if src is None else src, dst_ref=dst,
                send_sem=send_sems.at[t, k], recv_sem=recv_sems.at[t, k],
                device_id=to, device_id_type=MESH)

        first, passed = [], []
        for t in range(n):
            cps = [copy(t, 0, me, sibling, src=x_refs[t])]
            cps += [copy(t, 1 + j, me, (*chip, c), src=x_refs[t]) for j, chip in enumerate(chips)]
            for cp in cps:
                cp.start()
            first += cps
        for t in range(n):
            for j, chip in enumerate(chips):
                copy(t, 1 + j, (*chip, c), me).wait_recv()
                fwd = copy(t, 4 + j, (*chip, c), sibling)
                fwd.start()
                passed.append(fwd)
        for t in range(n):
            copy(t, 0, sibling, me).wait_recv()
            for j, chip in enumerate(chips):
                copy(t, 4 + j, (*chip, 1 - c), me).wait_recv()
        for cp in first + passed:
            cp.wait_send()

    return pl.pallas_call(
        body, name=name,
        out_shape=[jax.ShapeDtypeStruct((N_DEV,) + a.shape, a.dtype) for a in xs],
        in_specs=[_HBM] * n, out_specs=[_HBM] * n,
        scratch_shapes=[pltpu.SemaphoreType.DMA((n, 7)), pltpu.SemaphoreType.DMA((n, 7))],
    )(*xs)


def _with_own(gathered, own):
    me = 4 * lax.axis_index("x") + 2 * lax.axis_index("y") + lax.axis_index("c")
    return lax.dynamic_update_index_in_dim(gathered, own, me, 0)


def _place():
    x, y, c = lax.axis_index("x"), lax.axis_index("y"), lax.axis_index("c")
    return x, y, c, 4 * x + 2 * y + c, (x, y, 1 - c), [(1 - x, y), (x, 1 - y), (1 - x, 1 - y)]


def _remote(src, dst, send_sem, recv_sem, to):
    return pltpu.make_async_remote_copy(src_ref=src, dst_ref=dst, send_sem=send_sem, recv_sem=recv_sem,
                                        device_id=to, device_id_type=MESH)


class _Riding:
    aliases = {}

    def copies(self, ins, outs, sems):
        raise NotImplementedError

    def start(self, ins, outs, sems):
        local, out, _ = self.copies(ins, outs, sems)
        for cp in local + out:
            cp.start()

    def wait(self, ins, outs, sems):
        local, out, landing = self.copies(ins, outs, sems)
        for cp in landing:
            cp.wait_recv()
        for cp in out:
            cp.wait_send()
        for cp in local:
            cp.wait()


def _rows_of(ref, rows):
    return ref if rows is None else ref.at[pl.ds(rows[0], rows[1])]


class _GatherSpread(_Riding):
    def __init__(self, xs, rows=None, into=None):
        n = len(xs)
        self.rows = rows
        self.ins = list(xs) + list(into or [])
        self.out_shapes = [jax.ShapeDtypeStruct((N_DEV,) + a.shape, a.dtype) for a in xs]
        self.aliases = {n + t: t for t in range(n)} if into else {}
        self.sem_shapes = [pltpu.SemaphoreType.DMA((n, 4)), pltpu.SemaphoreType.DMA((n, 4))]

    def copies(self, ins, outs, sems):
        send, recv = sems
        _, _, c, me, sibling, chips = _place()
        targets = [sibling] + [(*chip, c) for chip in chips]
        out, landing = [], []
        for t in range(len(outs)):
            src = _rows_of(ins[t], self.rows)
            for k, to in enumerate(targets):
                out.append(_remote(src, _rows_of(outs[t].at[me], self.rows), send.at[t, k], recv.at[t, k], to))
                theirs = _rows_of(outs[t].at[4 * to[0] + 2 * to[1] + to[2]], self.rows)
                landing.append(_remote(src, theirs, send.at[t, k], recv.at[t, k], to))
        return [], out, landing


class _GatherPass(_Riding):
    def __init__(self, bufs):
        n = len(bufs)
        self.ins = list(bufs)
        self.out_shapes = [jax.ShapeDtypeStruct(b.shape, b.dtype) for b in bufs]
        self.aliases = {t: t for t in range(n)}
        self.sem_shapes = [pltpu.SemaphoreType.DMA((n, 3)), pltpu.SemaphoreType.DMA((n, 3))]

    def copies(self, ins, outs, sems):
        send, recv = sems
        _, _, c, _, sibling, chips = _place()
        out, landing = [], []
        for t in range(len(outs)):
            for j, (px, py) in enumerate(chips):
                got = outs[t].at[4 * px + 2 * py + c]
                out.append(_remote(got, got, send.at[t, j], recv.at[t, j], sibling))
                landing.append(_remote(got, outs[t].at[4 * px + 2 * py + 1 - c], send.at[t, j], recv.at[t, j], sibling))
        return [], out, landing


class _SiblingSwap(_Riding):
    def __init__(self, xs):
        n = len(xs)
        self.ins = list(xs)
        self.out_shapes = [jax.ShapeDtypeStruct((N_DEV // 2,) + a.shape[1:], a.dtype) for a in xs]
        self.sem_shapes = [pltpu.SemaphoreType.DMA((n, 4)), pltpu.SemaphoreType.DMA((n, 4))]

    def copies(self, ins, outs, sems):
        send, recv = sems
        _, _, c, _, sibling, _ = _place()
        out = [_remote(ins[t].at[2 * q + 1 - c], outs[t].at[q], send.at[t, q], recv.at[t, q], sibling)
               for t in range(len(ins)) for q in range(N_DEV // 2)]
        return [], out, out


class _ChipSend(_Riding):
    def __init__(self, ps, rows=None, into=None):
        n = len(ps)
        self.rows = rows
        self.ins = list(ps) + list(into or [])
        self.out_shapes = [jax.ShapeDtypeStruct((3,) + a.shape[1:], a.dtype) for a in ps]
        self.aliases = {n + t: t for t in range(n)} if into else {}
        self.sem_shapes = [pltpu.SemaphoreType.DMA((n, 3)), pltpu.SemaphoreType.DMA((n, 3))]

    def copies(self, ins, outs, sems):
        send, recv = sems
        _, _, c, _, _, chips = _place()
        out = [_remote(_rows_of(ins[t].at[2 * px + py], self.rows), _rows_of(outs[t].at[j], self.rows),
                       send.at[t, j], recv.at[t, j], (px, py, c))
               for t in range(len(outs)) for j, (px, py) in enumerate(chips)]
        return [], out, out


class _Bundle(_Riding):
    def __init__(self, comms):
        self.comms = comms
        self.ins = [a for cm in comms for a in cm.ins]
        self.out_shapes = [s for cm in comms for s in cm.out_shapes]
        self.sem_shapes = [s for cm in comms for s in cm.sem_shapes]
        self.aliases = {}
        i0 = o0 = 0
        for cm in comms:
            self.aliases.update({i0 + i: o0 + j for i, j in cm.aliases.items()})
            i0, o0 = i0 + len(cm.ins), o0 + len(cm.out_shapes)

    def copies(self, ins, outs, sems):
        local, out, landing = [], [], []
        i0 = o0 = s0 = 0
        for cm in self.comms:
            i1, o1, s1 = i0 + len(cm.ins), o0 + len(cm.out_shapes), s0 + len(cm.sem_shapes)
            a, b, c = cm.copies(ins[i0:i1], outs[o0:o1], sems[s0:s1])
            local, out, landing = local + a, out + b, landing + c
            i0, o0, s0 = i1, o1, s1
        return local, out, landing


class _Rides:
    def __init__(self):
        self.plan, self.done, self.aboard = {}, {}, {}

    def put(self, host, key, make):
        self.plan.setdefault(host, []).append((key, make))

    def board(self, host):
        if host not in self.plan:
            return None
        self.aboard[host] = [make() for _, make in self.plan[host]]
        return _Bundle(self.aboard[host])

    def land(self, host, results):
        o0 = 0
        for (key, _), cm in zip(self.plan[host], self.aboard[host]):
            self.done[key] = list(results[o0:o0 + len(cm.out_shapes)])
            o0 += len(cm.out_shapes)


def _pallas(body, *, name, grid, out_shape, in_specs, out_specs, operands, semantics, scratch_shapes=(), rides=None):
    comm = rides.board(name) if rides is not None else None
    if comm is None:
        return pl.pallas_call(
            body, name=name, grid=grid, out_shape=list(out_shape), in_specs=list(in_specs),
            out_specs=list(out_specs), scratch_shapes=list(scratch_shapes), compiler_params=_params(*semantics),
        )(*operands)
    n_in, n_out, n_scr = len(in_specs), len(out_shape), len(scratch_shapes)
    n_ci, n_co = len(comm.ins), len(comm.out_shapes)

    def hosted(*refs):
        cuts = np.cumsum([0, n_in, n_ci, n_out, n_co, n_scr])
        ins, c_ins, outs, c_outs, scr = (refs[cuts[i]:cuts[i + 1]] for i in range(5))
        sems = refs[cuts[5]:]
        ids = [pl.program_id(a) for a in range(len(grid))]
        first = functools.reduce(jnp.logical_and, [i == 0 for i in ids])
        last = functools.reduce(jnp.logical_and, [i == g - 1 for i, g in zip(ids, grid)])

        @pl.when(first)
        def _():
            comm.start(c_ins, c_outs, sems)

        body(*ins, *outs, *scr)

        @pl.when(last)
        def _():
            comm.wait(c_ins, c_outs, sems)

    results = pl.pallas_call(
        hosted, name=name, grid=grid, out_shape=list(out_shape) + comm.out_shapes,
        in_specs=list(in_specs) + [_HBM] * n_ci, out_specs=list(out_specs) + [_HBM] * n_co,
        scratch_shapes=list(scratch_shapes) + comm.sem_shapes,
        input_output_aliases={n_in + i: n_out + j for i, j in comm.aliases.items()},
        compiler_params=_params(*["arbitrary"] * len(grid)),
    )(*operands, *comm.ins)
    rides.land(name, results[n_out:])
    return results[:n_out]


def _alone(rides, name):
    comm = rides.board(name)

    def body(*refs):
        n_ci, n_co = len(comm.ins), len(comm.out_shapes)
        ins, outs, sems = refs[:n_ci], refs[n_ci:n_ci + n_co], refs[n_ci + n_co:]
        comm.start(ins, outs, sems)
        comm.wait(ins, outs, sems)

    results = pl.pallas_call(
        body, name=name, out_shape=comm.out_shapes, in_specs=[_HBM] * len(comm.ins),
        out_specs=[_HBM] * len(comm.out_shapes), scratch_shapes=comm.sem_shapes,
        input_output_aliases=dict(comm.aliases),
    )(*comm.ins)
    rides.land(name, results)


def _row_tile(rows, cols, itemsize, copies, budget=24 * 1024 * 1024):
    padded = -(-cols // LANES) * LANES
    mult = 8 * (4 // itemsize)
    if rows % mult:
        return rows
    return _pick(rows, max(mult, budget // (copies * padded * itemsize)), mult)


def _sum_leading(x, name):
    n_src, rows, cols = x.shape
    tr = _row_tile(rows, cols, 4, 2 * (n_src + 2))

    def body(x_ref, o_ref):
        acc = x_ref[0].astype(F32)
        for s in range(1, n_src):
            acc = acc + x_ref[s].astype(F32)
        o_ref[...] = acc

    return pl.pallas_call(
        body, name=name, grid=(rows // tr,), out_shape=jax.ShapeDtypeStruct((rows, cols), F32),
        in_specs=[pl.BlockSpec((n_src, tr, cols), lambda i: (0, i, 0))],
        out_specs=pl.BlockSpec((tr, cols), lambda i: (i, 0)), compiler_params=_params("parallel"),
    )(x)


def _pair_sum(buf, theirs, core, name):
    n_q, rows, cols = theirs.shape
    tr = _row_tile(rows, cols, 4, 6)

    def body(core_ref, mine_ref, theirs_ref, o_ref):
        o_ref[...] = (mine_ref[...].astype(F32) + theirs_ref[...].astype(F32)).astype(BF16)

    spec = pl.BlockSpec((None, tr, cols), lambda q, i, core_ref: (q, i, 0))
    return pl.pallas_call(
        body, name=name, out_shape=jax.ShapeDtypeStruct(theirs.shape, BF16),
        grid_spec=pltpu.PrefetchScalarGridSpec(
            num_scalar_prefetch=1, grid=(n_q, rows // tr),
            in_specs=[pl.BlockSpec((None, tr, cols), lambda q, i, core_ref: (2 * q + core_ref[0], i, 0)), spec],
            out_specs=spec),
        compiler_params=_params("parallel", "parallel"),
    )(core, buf, theirs)


def _sum_with_own(recv, pair, chip, name):
    n_src, rows, cols = recv.shape
    tr = _row_tile(rows, cols, 4, 2 * (n_src + 3))

    def body(chip_ref, own_ref, recv_ref, o_ref):
        acc = own_ref[...].astype(F32)
        for s in range(n_src):
            acc = acc + recv_ref[s].astype(F32)
        o_ref[...] = acc

    return pl.pallas_call(
        body, name=name, out_shape=jax.ShapeDtypeStruct((rows, cols), F32),
        grid_spec=pltpu.PrefetchScalarGridSpec(
            num_scalar_prefetch=1, grid=(rows // tr,),
            in_specs=[pl.BlockSpec((None, tr, cols), lambda i, chip_ref: (chip_ref[0], i, 0)),
                      pl.BlockSpec((n_src, tr, cols), lambda i, chip_ref: (0, i, 0))],
            out_specs=pl.BlockSpec((tr, cols), lambda i, chip_ref: (i, 0))),
        compiler_params=_params("parallel"),
    )(chip, pair, recv)


def _mm(a, b, *, name, ta=False, tb=False, tm=1024, tn=512, out_dtype=F32, a_act=None,
        residual=None, gate=None, out_chunk=None, rides=None):
    k_dim, m = (a.shape if ta else a.shape[::-1])
    n, kb = (b.shape if tb else b.shape[::-1])
    assert kb == k_dim, (a.shape, b.shape, ta, tb)
    tm, tn = _pick(m, tm, 128), _pick(out_chunk or n, tn, 128)
    ca, cb = (0 if ta else 1), (1 if tb else 0)
    a_spec = pl.BlockSpec((k_dim, tm), lambda i, j: (0, i)) if ta else pl.BlockSpec((tm, k_dim), lambda i, j: (i, 0))
    b_spec = pl.BlockSpec((tn, k_dim), lambda i, j: (j, 0)) if tb else pl.BlockSpec((k_dim, tn), lambda i, j: (0, j))
    mn_spec = pl.BlockSpec((tm, tn), lambda i, j: (i, j))
    if out_chunk:
        per = out_chunk // tn
        o_spec = pl.BlockSpec((None, tm, tn), lambda i, j: (j // per, i, j % per))
        out_shape = jax.ShapeDtypeStruct((n // out_chunk, m, out_chunk), out_dtype)
    else:
        o_spec = mn_spec
        out_shape = jax.ShapeDtypeStruct((m, n), out_dtype)
    operands, in_specs = [a, b], [a_spec, b_spec]
    for extra in (gate, residual):
        if extra is not None:
            operands.append(extra)
            in_specs.append(mn_spec)

    def body(*refs):
        a_ref, b_ref, o_ref = refs[0], refs[1], refs[-1]
        extras = list(refs[2:-1])
        gate_ref = extras.pop(0) if gate is not None else None
        res_ref = extras.pop(0) if residual is not None else None
        av = a_ref[...].astype(BF16)
        if a_act == "relu2":
            av = jnp.square(jnp.maximum(av, jnp.zeros_like(av)))
        r = _dot(av, b_ref[...].astype(BF16), ca, cb)
        if gate_ref is not None:
            r = r * (2.0 * jnp.maximum(gate_ref[...].astype(F32), 0.0))
        if res_ref is not None:
            r = r + res_ref[...].astype(F32)
        o_ref[...] = r.astype(out_dtype)

    return _pallas(body, name=name, grid=(m // tm, n // tn), out_shape=[out_shape], in_specs=in_specs,
                   out_specs=[o_spec], operands=operands, semantics=("parallel", "arbitrary"), rides=rides)[0]


def _rmsnorm_fwd(xs, seg_widths, g, name, tm=256):
    t_len = xs[0].shape[0]
    width = sum(x.shape[1] for x in xs)
    tm = _pick(t_len, tm, 16)
    n = len(xs)

    def body(*refs):
        x_refs, g_ref, o_ref = refs[:n], refs[n], refs[n + 1]
        col = 0
        for x_ref, widths in zip(x_refs, seg_widths):
            off = 0
            for w in widths:
                xv = x_ref[:, off:off + w].astype(F32)
                r = lax.rsqrt(jnp.mean(xv * xv, axis=1, keepdims=True) + NORM_EPS)
                o_ref[:, col:col + w] = (xv * r * g_ref[:, col:col + w]).astype(BF16)
                off += w
                col += w

    return pl.pallas_call(
        body, name=name, grid=(t_len // tm,),
        out_shape=jax.ShapeDtypeStruct((t_len, width), BF16),
        in_specs=[pl.BlockSpec((tm, x.shape[1]), lambda i: (i, 0)) for x in xs]
        + [pl.BlockSpec((1, width), lambda i: (0, 0))],
        out_specs=pl.BlockSpec((tm, width), lambda i: (i, 0)),
        compiler_params=_params("parallel"),
    )(*xs, g)


def _rmsnorm_bwd(xs, seg_widths, g, dh, residuals, name, tm=256, bf16_copy=False):
    t_len = xs[0].shape[0]
    width = sum(x.shape[1] for x in xs)
    tm = _pick(t_len, tm, 8)
    n = len(xs)
    has_res = [r is not None for r in residuals]
    res_ops = [r for r in residuals if r is not None]

    def body(*refs):
        x_refs, g_ref, dh_ref = refs[:n], refs[n], refs[n + 1]
        res_refs = list(refs[n + 2:n + 2 + len(res_ops)])
        dx_refs = refs[n + 2 + len(res_ops):n + 2 + len(res_ops) + n]
        dg_ref = refs[n + 2 + len(res_ops) + n]
        copy_refs = refs[n + 3 + len(res_ops) + n:]
        first = pl.program_id(0) == 0
        col = 0
        for idx, (x_ref, widths) in enumerate(zip(x_refs, seg_widths)):
            res_ref = res_refs.pop(0) if has_res[idx] else None
            off = 0
            for w in widths:
                xv = x_ref[:, off:off + w].astype(F32)
                r = lax.rsqrt(jnp.mean(xv * xv, axis=1, keepdims=True) + NORM_EPS)
                xh = xv * r
                dhv = dh_ref[:, col:col + w].astype(F32)
                gd = dhv * g_ref[:, col:col + w]
                dx = r * (gd - xh * jnp.mean(gd * xh, axis=1, keepdims=True))
                if res_ref is not None:
                    dx = dx + res_ref[:, off:off + w]
                dx_refs[idx][:, off:off + w] = dx
                if bf16_copy:
                    copy_refs[idx][:, off:off + w] = dx.astype(BF16)
                part = jnp.sum(dhv * xh, axis=0, keepdims=True)

                @pl.when(first)
                def _(part=part, col=col, w=w):
                    dg_ref[:, col:col + w] = part

                @pl.when(jnp.logical_not(first))
                def _(part=part, col=col, w=w):
                    dg_ref[:, col:col + w] += part
                off += w
                col += w

    outs = pl.pallas_call(
        body, name=name, grid=(t_len // tm,),
        out_shape=[jax.ShapeDtypeStruct(x.shape, F32) for x in xs] + [jax.ShapeDtypeStruct((1, width), F32)]
        + ([jax.ShapeDtypeStruct(x.shape, BF16) for x in xs] if bf16_copy else []),
        in_specs=[pl.BlockSpec((tm, x.shape[1]), lambda i: (i, 0)) for x in xs]
        + [pl.BlockSpec((1, width), lambda i: (0, 0)), pl.BlockSpec((tm, width), lambda i: (i, 0))]
        + [pl.BlockSpec((tm, r.shape[1]), lambda i: (i, 0)) for r in res_ops],
        out_specs=[pl.BlockSpec((tm, x.shape[1]), lambda i: (i, 0)) for x in xs]
        + [pl.BlockSpec((1, width), lambda i: (0, 0))]
        + ([pl.BlockSpec((tm, x.shape[1]), lambda i: (i, 0)) for x in xs] if bf16_copy else []),
        compiler_params=_params("arbitrary"),
    )(*xs, g, dh, *res_ops)
    return outs[:n], outs[n], outs[n + 1:]


def _loss_head(x, g, target, name, tm=256):
    t_len, d = x.shape
    tm = _pick(t_len, tm, 8)

    def body(x_ref, g_ref, t_ref, dx_ref, dg_ref, loss_ref, dxb_ref):
        first = pl.program_id(0) == 0
        xv = x_ref[...]
        r = lax.rsqrt(jnp.mean(xv * xv, axis=1, keepdims=True) + NORM_EPS)
        xh = xv * r
        gv = g_ref[...]
        err = xh * gv - t_ref[...]
        part_loss = 0.5 * jnp.sum(jnp.mean(err * err, axis=1, keepdims=True), axis=0, keepdims=True)
        dy = err * (1.0 / d)
        gd = dy * gv
        dx = r * (gd - xh * jnp.mean(gd * xh, axis=1, keepdims=True))
        dx_ref[...] = dx
        dxb_ref[...] = dx.astype(BF16)
        part_g = jnp.sum(dy * xh, axis=0, keepdims=True)
        part_loss = jnp.broadcast_to(part_loss, (1, LANES))

        @pl.when(first)
        def _():
            dg_ref[...] = part_g
            loss_ref[...] = part_loss

        @pl.when(jnp.logical_not(first))
        def _():
            dg_ref[...] += part_g
            loss_ref[...] += part_loss

    return pl.pallas_call(
        body, name=name, grid=(t_len // tm,),
        out_shape=[jax.ShapeDtypeStruct((t_len, d), F32), jax.ShapeDtypeStruct((1, d), F32),
                   jax.ShapeDtypeStruct((1, LANES), F32), jax.ShapeDtypeStruct((t_len, d), BF16)],
        in_specs=[pl.BlockSpec((tm, d), lambda i: (i, 0)), pl.BlockSpec((1, d), lambda i: (0, 0)),
                  pl.BlockSpec((tm, d), lambda i: (i, 0))],
        out_specs=[pl.BlockSpec((tm, d), lambda i: (i, 0)), pl.BlockSpec((1, d), lambda i: (0, 0)),
                   pl.BlockSpec((1, LANES), lambda i: (0, 0)), pl.BlockSpec((tm, d), lambda i: (i, 0))],
        compiler_params=_params("arbitrary"),
    )(x, g, target)


def _alibi_slope(h, n_heads):
    return jnp.exp(jnp.full((1, 1), -8.0 * math.log(2.0) / n_heads, F32) * (h + 1).astype(F32))


def _attn_tiles(d, w):
    return ATTN_BLOCK * d, (w if d == 1 else LANES)


def _residue_rows(r, d):
    return pl.ds(r, ATTN_BLOCK, stride=d) if d > 1 else pl.ds(0, ATTN_BLOCK)


def _attn_masks(first_block):
    i = lax.broadcasted_iota(jnp.int32, (2 * ATTN_BLOCK, 2 * ATTN_BLOCK), 0) % ATTN_BLOCK
    j = lax.broadcasted_iota(jnp.int32, (2 * ATTN_BLOCK, 2 * ATTN_BLOCK), 1)
    delta = i - j + ATTN_BLOCK
    valid = jnp.logical_and(delta >= 0, delta <= ATTN_BLOCK)
    valid = jnp.logical_and(valid, jnp.logical_or(j >= ATTN_BLOCK, jnp.logical_not(first_block)))
    return valid, delta.astype(F32)


def _stack_heads(x, masks):
    zero = jnp.zeros_like(x)
    return jnp.concatenate([jnp.where(masks[0], x, zero), jnp.where(masks[1], x, zero)], axis=0)


def _unstack_heads(x2, masks):
    return jnp.where(masks[0], x2[:ATTN_BLOCK], x2[ATTN_BLOCK:])


def _pair_slopes(first_head, p, n_heads, d):
    row = lax.broadcasted_iota(jnp.int32, (2 * ATTN_BLOCK, 1), 0)
    sa, sb = (_alibi_slope(first_head + 2 * p + hh, n_heads) * d for hh in range(2))
    return jnp.where(row < ATTN_BLOCK, sa, sb)


def _head_lane_masks():
    lane = lax.broadcasted_iota(jnp.int32, (ATTN_BLOCK, LANES), 1)
    return [lane < HEAD_DIM, lane >= HEAD_DIM]


def _attn_branch_fwd(proj, w, dilation, n_heads, name, rides=None):
    t_len = proj.shape[0]
    d = dilation
    rows, lw = _attn_tiles(d, w)
    nb = t_len // rows
    n_pairs = lw // LANES
    per = w // lw
    scale = HEAD_DIM ** -0.5

    def body(q_ref, kp_ref, kc_ref, vp_ref, vc_ref, o_ref, lse_ref):
        first_head = pl.program_id(0) * (2 * n_pairs)
        first_block = pl.program_id(1) == 0
        valid, delta = _attn_masks(first_block)
        masks = _head_lane_masks()
        ones = jnp.ones((2 * ATTN_BLOCK, LANES), BF16)
        for p in range(n_pairs):
            cols = pl.ds(p * LANES, LANES)
            bias = _pair_slopes(first_head, p, n_heads, d) * delta
            for r in range(d):
                rs = _residue_rows(r, d)
                q2 = _stack_heads((q_ref[rs, cols] * scale).astype(BF16), masks)
                k2 = jnp.concatenate([kp_ref[rs, cols], kc_ref[rs, cols]], axis=0).astype(BF16)
                v2 = jnp.concatenate([vp_ref[rs, cols], vc_ref[rs, cols]], axis=0).astype(BF16)
                s = jnp.where(valid, _dot(q2, k2, 1, 1) - bias, NEG_INF)
                m = jnp.max(s, axis=1, keepdims=True)
                pr = jnp.exp(s - m).astype(BF16)
                den = _dot(pr, ones, 1, 0)
                o_ref[rs, cols] = _unstack_heads(_dot(pr, v2, 1, 0) / den, masks)
                lse_ref[rs, cols] = _unstack_heads(m + jnp.log(den), masks)

    def spec(which, prev):
        if prev:
            return pl.BlockSpec((rows, lw), lambda b, n: (jnp.maximum(n - 1, 0), which * per + b))
        return pl.BlockSpec((rows, lw), lambda b, n: (n, which * per + b))

    o_spec = pl.BlockSpec((rows, lw), lambda b, n: (n, b))
    return _pallas(
        body, name=name, grid=(per, nb), out_shape=[jax.ShapeDtypeStruct((t_len, w), F32)] * 2,
        in_specs=[spec(0, False), spec(1, True), spec(1, False), spec(2, True), spec(2, False)],
        out_specs=[o_spec, o_spec], operands=[proj] * 5, semantics=("parallel", "parallel"), rides=rides)


def _attn_combine(outs, lses, name, tm=512):
    t_len, w = outs[0].shape
    tm = _pick(t_len, tm, 8)
    nbr = len(outs)

    def body(*refs):
        o_refs, l_refs = refs[:nbr], refs[nbr:2 * nbr]
        out_ref, lse_ref = refs[2 * nbr:]
        ls = [r[...] for r in l_refs]
        m = functools.reduce(jnp.maximum, ls)
        es = [jnp.exp(l - m) for l in ls]
        den = functools.reduce(lambda a, b: a + b, es)
        num = functools.reduce(lambda a, b: a + b, [e * r[...] for e, r in zip(es, o_refs)])
        out_ref[...] = num / den
        lse_ref[...] = m + jnp.log(den)

    spec = pl.BlockSpec((tm, w), lambda i: (i, 0))
    return pl.pallas_call(
        body, name=name, grid=(t_len // tm,),
        out_shape=[jax.ShapeDtypeStruct((t_len, w), F32)] * 2,
        in_specs=[spec] * (2 * nbr), out_specs=[spec, spec],
        compiler_params=_params("parallel"),
    )(*outs, *lses)


def _attn_branch_bwd(proj, w, out, lse, dout, dilation, n_heads, name, acc=None, rides=None):
    t_len = proj.shape[0]
    d = dilation
    rows, lw = _attn_tiles(d, w)
    nb = t_len // rows
    n_pairs = lw // LANES
    per = w // lw
    scale = HEAD_DIM ** -0.5
    n_acc = 0 if acc is None else 3

    def body(*refs):
        q_ref, kp_ref, kc_ref, vp_ref, vc_ref, out_ref, lse_ref, do_ref = refs[:8]
        acc_refs = refs[8:8 + n_acc]
        dq_ref, dk_ref, dv_ref, dk_carry, dv_carry = refs[8 + n_acc:]
        first_head = pl.program_id(0) * (2 * n_pairs)
        n = pl.program_id(1)
        first_block = n == 0
        valid, dist = _attn_masks(first_block)
        masks = _head_lane_masks()

        def plus(value, idx, *where):
            return value + acc_refs[idx][where] if n_acc else value

        @pl.when(first_block)
        def _():
            dk_carry[...] = jnp.zeros_like(dk_carry)
            dv_carry[...] = jnp.zeros_like(dv_carry)

        @pl.when(n < nb)
        def _():
            for p in range(n_pairs):
                cols = pl.ds(p * LANES, LANES)
                bias = _pair_slopes(first_head, p, n_heads, d) * dist
                for r in range(d):
                    rs = _residue_rows(r, d)
                    q2 = _stack_heads((q_ref[rs, cols] * scale).astype(BF16), masks)
                    k2 = jnp.concatenate([kp_ref[rs, cols], kc_ref[rs, cols]], axis=0).astype(BF16)
                    v2 = jnp.concatenate([vp_ref[rs, cols], vc_ref[rs, cols]], axis=0).astype(BF16)
                    do = do_ref[rs, cols]
                    do2 = _stack_heads(do.astype(BF16), masks)
                    do_out = do * out_ref[rs, cols]
                    lse_all = lse_ref[rs, cols]
                    delta = jnp.concatenate([jnp.sum(jnp.where(masks[hh], do_out, 0.0), axis=1, keepdims=True)
                                             for hh in range(2)], axis=0)
                    lse2 = jnp.concatenate([jnp.max(jnp.where(masks[hh], lse_all, NEG_INF), axis=1, keepdims=True)
                                            for hh in range(2)], axis=0)
                    s = jnp.where(valid, _dot(q2, k2, 1, 1) - bias, NEG_INF)
                    pr = jnp.exp(s - lse2)
                    ds = (pr * (_dot(do2, v2, 1, 1) - delta)).astype(BF16)
                    dq = _unstack_heads(_dot(ds, k2, 1, 0), masks)
                    dk2 = _dot(ds, q2, 0, 0)
                    dv2 = _dot(pr.astype(BF16), do2, 0, 0)
                    dq_ref[rs, cols] = plus(dq * scale, 0, rs, cols)
                    dk_ref[rs, cols] = plus(dk_carry[r, :, cols] + dk2[:ATTN_BLOCK], 1, rs, cols)
                    dv_ref[rs, cols] = plus(dv_carry[r, :, cols] + dv2[:ATTN_BLOCK], 2, rs, cols)
                    dk_carry[r, :, cols] = dk2[ATTN_BLOCK:]
                    dv_carry[r, :, cols] = dv2[ATTN_BLOCK:]

        @pl.when(n == nb)
        def _():
            for r in range(d):
                rs = _residue_rows(r, d)
                dk_ref[rs, :] = plus(dk_carry[r], 1, rs, slice(None))
                dv_ref[rs, :] = plus(dv_carry[r], 2, rs, slice(None))

    def qkv_spec(which, shift):
        return pl.BlockSpec((rows, lw), lambda b, n: (jnp.clip(n - shift, 0, nb - 1), which * per + b))

    q_like = pl.BlockSpec((rows, lw), lambda b, n: (jnp.minimum(n, nb - 1), b))
    k_like = pl.BlockSpec((rows, lw), lambda b, n: (jnp.maximum(n - 1, 0), b))
    return _pallas(
        body, name=name, grid=(per, nb + 1), out_shape=[jax.ShapeDtypeStruct((t_len, w), F32)] * 3,
        in_specs=[qkv_spec(0, 0), qkv_spec(1, 1), qkv_spec(1, 0), qkv_spec(2, 1), qkv_spec(2, 0),
                  q_like, q_like, q_like] + [q_like, k_like, k_like][:n_acc],
        out_specs=[q_like, k_like, k_like], operands=[proj] * 5 + [out, lse, dout, *(acc or ())],
        scratch_shapes=[pltpu.VMEM((d, ATTN_BLOCK, lw), F32), pltpu.VMEM((d, ATTN_BLOCK, lw), F32)],
        semantics=("parallel", "arbitrary"), rides=rides)


def _shift_down(u, s):
    if s == 0:
        return u
    row = lax.broadcasted_iota(jnp.int32, u.shape, 0)
    return jnp.where(row >= s, pltpu.roll(u, s, 0), 0.0)


def _shift_up(u, s):
    if s == 0:
        return u
    n = u.shape[0]
    row = lax.broadcasted_iota(jnp.int32, u.shape, 0)
    return jnp.where(row < n - s, pltpu.roll(u, n - s, 0), 0.0)


def _conv_fwd(u, col0, w, b, name):
    t_len, ch = u.shape[0], w.shape[1]
    blk0 = col0 // LANES

    def body(u_ref, w_ref, b_ref, o_ref):
        uv = u_ref[...]
        pre = b_ref[...] + jnp.zeros_like(uv)
        for k in range(SSD_CONV):
            pre = pre + w_ref[k:k + 1, :] * _shift_down(uv, SSD_CONV - 1 - k)
        o_ref[...] = pre * jax.nn.sigmoid(pre)

    return pl.pallas_call(
        body, name=name, grid=(ch // LANES,),
        out_shape=jax.ShapeDtypeStruct((t_len, ch), F32),
        in_specs=[pl.BlockSpec((t_len, LANES), lambda j: (0, blk0 + j)),
                  pl.BlockSpec((SSD_CONV, LANES), lambda j: (0, j)), pl.BlockSpec((1, LANES), lambda j: (0, j))],
        out_specs=pl.BlockSpec((t_len, LANES), lambda j: (0, j)),
        compiler_params=_params("parallel"),
    )(u, w, b)


def _conv_bwd(u, col0, w, b, dact, name):
    t_len, ch = u.shape[0], w.shape[1]
    blk0 = col0 // LANES

    def body(u_ref, w_ref, b_ref, da_ref, du_ref, dw_ref, db_ref):
        uv = u_ref[...]
        shifted = [_shift_down(uv, SSD_CONV - 1 - k) for k in range(SSD_CONV)]
        pre = b_ref[...] + jnp.zeros_like(uv)
        for k in range(SSD_CONV):
            pre = pre + w_ref[k:k + 1, :] * shifted[k]
        sig = jax.nn.sigmoid(pre)
        dpre = da_ref[...] * (sig * (1.0 + pre * (1.0 - sig)))
        du = jnp.zeros_like(uv)
        for k in range(SSD_CONV):
            du = du + w_ref[k:k + 1, :] * _shift_up(dpre, SSD_CONV - 1 - k)
            dw_ref[k:k + 1, :] = jnp.sum(dpre * shifted[k], axis=0, keepdims=True)
        du_ref[...] = du
        db_ref[...] = jnp.sum(dpre, axis=0, keepdims=True)

    col = pl.BlockSpec((t_len, LANES), lambda j: (0, j))
    w_spec = pl.BlockSpec((SSD_CONV, LANES), lambda j: (0, j))
    b_spec = pl.BlockSpec((1, LANES), lambda j: (0, j))
    return pl.pallas_call(
        body, name=name, grid=(ch // LANES,),
        out_shape=[jax.ShapeDtypeStruct((t_len, ch), F32), jax.ShapeDtypeStruct((SSD_CONV, ch), F32),
                   jax.ShapeDtypeStruct((1, ch), F32)],
        in_specs=[pl.BlockSpec((t_len, LANES), lambda j: (0, blk0 + j)), w_spec, b_spec, col],
        out_specs=[col, w_spec, b_spec],
        compiler_params=_params("parallel"),
    )(u, w, b, dact)


def _cumsum_rows(v):
    n = v.shape[0]
    row = lax.broadcasted_iota(jnp.int32, v.shape, 0)
    s = 1
    while s < n:
        v = v + jnp.where(row >= s, pltpu.roll(v, s, 0), 0.0)
        s *= 2
    return v


def _rev_cumsum_rows(v):
    n = v.shape[0]
    row = lax.broadcasted_iota(jnp.int32, v.shape, 0)
    s = 1
    while s < n:
        v = v + jnp.where(row < n - s, pltpu.roll(v, n - s, 0), 0.0)
        s *= 2
    return v


def _head_selector(heads, width):
    j = lax.broadcasted_iota(jnp.int32, (LANES, width), 0)
    lane = lax.broadcasted_iota(jnp.int32, (LANES, width), 1)
    return jnp.where(jnp.logical_and(lane // HEAD_DIM == j, j < heads), 1.0, 0.0).astype(BF16)


class _SsdChunk:
    def __init__(self, dtraw_ref, bias_ref, alog_ref, xs_ref, b_ref, c_ref, heads):
        q = SSD_CHUNK
        width = heads * HEAD_DIM
        lane = lax.broadcasted_iota(jnp.int32, (q, LANES), 1)
        self.head_lanes = lane < heads
        lane1 = lax.broadcasted_iota(jnp.int32, (1, LANES), 1)
        self.a = jnp.where(lane1 < heads, -jnp.exp(alog_ref[...]), 0.0)
        self.dt_arg = dtraw_ref[...] + bias_ref[...]
        self.dt = jnp.where(self.head_lanes, jax.nn.softplus(self.dt_arg), 0.0)
        self.cum = _cumsum_rows(self.dt * self.a)
        self.cum_t = self.cum.T
        last = self.cum[q - 1:q, :]
        self.sel = _head_selector(heads, width)
        self.expand = lambda v: _dot_exact(v, self.sel, 1, 0)
        self.segsum = lambda v: _dot_exact(v, self.sel, 1, 1)
        self.e_exp = self.expand(jnp.exp(self.cum))
        self.d_exp = self.expand(jnp.exp(last - self.cum))
        self.elast_exp = self.e_exp[q - 1:q, :]
        self.dt_exp = self.expand(self.dt)
        self.xs = xs_ref[...]
        self.x = self.xs * self.dt_exp
        self.xb = self.x.astype(BF16)
        self.bb = b_ref[...].astype(BF16)
        self.cb = c_ref[...].astype(BF16)
        self.cbm = _dot(self.cb, self.bb, 1, 1)
        li = lax.broadcasted_iota(jnp.int32, (q, q), 0)
        si = lax.broadcasted_iota(jnp.int32, (q, q), 1)
        self.tri = li >= si
        hl = lax.broadcasted_iota(jnp.int32, (q, LANES), 1)
        self.pair_masks = [hl < HEAD_DIM, hl >= HEAD_DIM]

    def decay(self, j):
        diff = self.cum[:, j:j + 1] - self.cum_t[j:j + 1, :]
        return jnp.exp(jnp.where(self.tri, diff, NEG_INF))


def _ssd_specs(t_len, heads, n_chunks, xbc_cols, rev):
    q, gw = SSD_CHUNK, heads * HEAD_DIM
    ssd_w = SSD_GROUPS * gw
    b_blk = ssd_w // SSD_STATE
    ch = (lambda c: n_chunks - 1 - c) if rev else (lambda c: c)
    return dict(
        dtraw=pl.BlockSpec((None, q, LANES), lambda g, c: (g, ch(c), 0)),
        small=pl.BlockSpec((None, 1, LANES), lambda g, c: (g, 0, 0)),
        dsk=pl.BlockSpec((None, 1, gw), lambda g, c: (g, 0, 0)),
        xs=pl.BlockSpec((q, gw), lambda g, c: (ch(c), g)),
        b=pl.BlockSpec((q, SSD_STATE), lambda g, c: (ch(c), b_blk + g)),
        c=pl.BlockSpec((q, SSD_STATE), lambda g, c: (ch(c), b_blk + SSD_GROUPS + g)),
        z=pl.BlockSpec((q, gw), lambda g, c: (ch(c), 3 * SSD_GROUPS + g)),
        tok=pl.BlockSpec((q, gw), lambda g, c: (ch(c), g)),
        state=pl.BlockSpec((None, SSD_STATE, gw), lambda g, c: (ch(c), 0, g)),
        bc=pl.BlockSpec((q, SSD_STATE), lambda g, c: (ch(c), g)),
    )


def _ssd_fwd(xbc, qkvz, dtraw_g, bias_g, alog_g, dsk_exp, heads, name):
    t_len = xbc.shape[0]
    q, gw = SSD_CHUNK, heads * HEAD_DIM
    n_chunks = t_len // q
    ssd_w = SSD_GROUPS * gw
    sp = _ssd_specs(t_len, heads, n_chunks, xbc.shape[1], rev=False)

    def body(dtraw_ref, bias_ref, alog_ref, dsk_ref, xs_ref, b_ref, c_ref, z_ref,
             yg_ref, ypre_ref, st_ref, s_scr):
        @pl.when(pl.program_id(1) == 0)
        def _():
            s_scr[...] = jnp.zeros_like(s_scr)

        k = _SsdChunk(dtraw_ref, bias_ref, alog_ref, xs_ref, b_ref, c_ref, heads)
        s_prev = s_scr[...]
        st_ref[...] = s_prev
        y_off = k.e_exp * _dot(k.cb, s_prev.astype(BF16), 1, 0)
        parts = []
        for p in range(heads // 2):
            xp = k.xb[:, p * LANES:(p + 1) * LANES]
            acc = jnp.zeros((q, LANES), F32)
            for hh in range(2):
                m = (k.cbm * k.decay(2 * p + hh)).astype(BF16)
                acc = acc + _dot(m, jnp.where(k.pair_masks[hh], xp, jnp.zeros_like(xp)), 1, 0)
            parts.append(acc)
        y = jnp.concatenate(parts, axis=1) + y_off
        xd = (k.x * k.d_exp).astype(BF16)
        s_scr[...] = k.elast_exp * s_prev + _dot(k.bb, xd, 0, 0)
        y_pre = y + dsk_ref[...] * k.xs
        zv = z_ref[...]
        ypre_ref[...] = y_pre
        yg_ref[...] = y_pre * (zv * jax.nn.sigmoid(zv))

    return pl.pallas_call(
        body, name=name, grid=(SSD_GROUPS, n_chunks),
        out_shape=[jax.ShapeDtypeStruct((t_len, ssd_w), F32), jax.ShapeDtypeStruct((t_len, ssd_w), F32),
                   jax.ShapeDtypeStruct((n_chunks, SSD_STATE, ssd_w), F32)],
        in_specs=[sp["dtraw"], sp["small"], sp["small"], sp["dsk"], sp["xs"], sp["b"], sp["c"], sp["z"]],
        out_specs=[sp["tok"], sp["tok"], sp["state"]],
        scratch_shapes=[pltpu.VMEM((SSD_STATE, gw), F32)],
        compiler_params=_params("parallel", "arbitrary"),
    )(dtraw_g, bias_g, alog_g, dsk_exp, xbc, xbc, xbc, qkvz)


def _ssd_bwd(xbc, qkvz, dtraw_g, bias_g, alog_g, dsk_exp, ypre, states, dyg, heads, name):
    t_len = xbc.shape[0]
    q, gw = SSD_CHUNK, heads * HEAD_DIM
    n_chunks = t_len // q
    ssd_w = SSD_GROUPS * gw
    sp = _ssd_specs(t_len, heads, n_chunks, xbc.shape[1], rev=True)

    def body(dtraw_ref, bias_ref, alog_ref, dsk_ref, xs_ref, b_ref, c_ref, z_ref, ypre_ref, st_ref, dyg_ref,
             dxs_ref, db_ref, dc_ref, dz_ref, ddt_ref, small_ref, g_scr):
        first = pl.program_id(1) == 0

        @pl.when(first)
        def _():
            g_scr[...] = jnp.zeros_like(g_scr)

        k = _SsdChunk(dtraw_ref, bias_ref, alog_ref, xs_ref, b_ref, c_ref, heads)
        zv = z_ref[...]
        sig = jax.nn.sigmoid(zv)
        dyg = dyg_ref[...]
        y_pre = ypre_ref[...]
        dy = dyg * (zv * sig)
        dz_ref[...] = dyg * y_pre * (sig * (1.0 + zv * (1.0 - sig)))
        dsk = dsk_ref[...]
        g_next = g_scr[...]
        s_prev = st_ref[...]
        sb = s_prev.astype(BF16)
        xd = k.x * k.d_exp
        xdb = xd.astype(BF16)
        gb = g_next.astype(BF16)
        dx_off = k.d_exp * _dot(k.bb, gb, 1, 0)
        dyb = dy.astype(BF16)
        dcb = jnp.zeros((q, q), F32)
        lane = lax.broadcasted_iota(jnp.int32, (q, LANES), 1)
        row_t = lax.broadcasted_iota(jnp.int32, (LANES, q), 0)
        w_rows = jnp.zeros((q, LANES), F32)
        w_cols_t = jnp.zeros((LANES, q), F32)
        parts = []
        for p in range(heads // 2):
            cols = slice(p * LANES, (p + 1) * LANES)
            dyp, xp = dyb[:, cols], k.xb[:, cols]
            acc = jnp.zeros((q, LANES), F32)
            for hh in range(2):
                j = 2 * p + hh
                lm = k.decay(j)
                m32 = k.cbm * lm
                dym = jnp.where(k.pair_masks[hh], dyp, jnp.zeros_like(dyp))
                acc = acc + _dot(m32.astype(BF16), dym, 0, 0)
                dm = _dot(dym, xp, 1, 1)
                dcb = dcb + dm * lm
                wmat = dm * m32
                w_rows = w_rows + jnp.where(lane == j, jnp.sum(wmat, axis=1, keepdims=True), 0.0)
                w_cols_t = w_cols_t + jnp.where(row_t == j, jnp.sum(wmat, axis=0, keepdims=True), 0.0)
            parts.append(acc)
        dx = jnp.concatenate(parts, axis=1) + dx_off
        dcbb = dcb.astype(BF16)
        edy = (k.e_exp * dy).astype(BF16)
        dc_ref[...] = _dot(dcbb, k.bb, 1, 0) + _dot(edy, sb, 1, 1)
        db_ref[...] = _dot(dcbb, k.cb, 0, 0) + _dot(xdb, gb, 1, 1)
        g_scr[...] = k.elast_exp * g_next + _dot(k.cb, edy, 0, 0)

        y_off = k.e_exp * _dot(k.cb, sb, 1, 0)
        dcum = w_rows - w_cols_t.T + k.segsum(dy * y_off)
        t_term = k.segsum(k.x * dx_off)
        gs = jnp.broadcast_to(jnp.sum(g_next * s_prev, axis=0, keepdims=True), (8, gw))
        carried = k.segsum(gs)[0:1, :] * jnp.exp(k.cum[q - 1:q, :])
        dda = _rev_cumsum_rows(dcum) + (_cumsum_rows(t_term) - t_term) + carried
        ddt = jnp.where(k.head_lanes, dda * k.a + k.segsum(dx * k.xs), 0.0)
        ddtraw = ddt * jax.nn.sigmoid(k.dt_arg)
        ddt_ref[...] = ddtraw
        dxs_ref[...] = dx * k.dt_exp + dsk * dy
        ds = jnp.broadcast_to(jnp.sum(dy * k.xs, axis=0, keepdims=True), (8, gw))
        d_alog = jnp.sum(jnp.where(k.head_lanes, dda * k.dt, 0.0), axis=0, keepdims=True) * k.a
        rows8 = lax.broadcasted_iota(jnp.int32, (8, LANES), 0)
        small = jnp.where(rows8 == 0, d_alog, 0.0)
        small = small + jnp.where(rows8 == 1, jnp.sum(ddtraw, axis=0, keepdims=True), 0.0)
        small = small + jnp.where(rows8 == 2, k.segsum(ds)[0:1, :], 0.0)

        @pl.when(first)
        def _():
            small_ref[...] = small

        @pl.when(jnp.logical_not(first))
        def _():
            small_ref[...] += small

    bc_out = sp["bc"]
    return pl.pallas_call(
        body, name=name, grid=(SSD_GROUPS, n_chunks),
        out_shape=[jax.ShapeDtypeStruct((t_len, ssd_w), F32),
                   jax.ShapeDtypeStruct((t_len, SSD_GROUPS * SSD_STATE), F32),
                   jax.ShapeDtypeStruct((t_len, SSD_GROUPS * SSD_STATE), F32),
                   jax.ShapeDtypeStruct((t_len, ssd_w), F32),
                   jax.ShapeDtypeStruct((SSD_GROUPS, t_len, LANES), F32),
                   jax.ShapeDtypeStruct((SSD_GROUPS, 8, LANES), F32)],
        in_specs=[sp["dtraw"], sp["small"], sp["small"], sp["dsk"], sp["xs"], sp["b"], sp["c"], sp["z"],
                  sp["tok"], sp["state"], sp["tok"]],
        out_specs=[sp["tok"], bc_out, bc_out, sp["tok"], sp["dtraw"],
                   pl.BlockSpec((None, 8, LANES), lambda g, c: (g, 0, 0))],
        scratch_shapes=[pltpu.VMEM((SSD_STATE, gw), F32)],
        compiler_params=_params("parallel", "arbitrary"),
    )(dtraw_g, bias_g, alog_g, dsk_exp, xbc, xbc, xbc, qkvz, ypre, states, dyg)


def _adamw(w, g, m, v, name, rides=None):
    n_lead, rows, lanes = w.shape
    tr = _row_tile(rows, lanes, 4, 14)
    c1 = 1.0 / (1.0 - ADAM_B1 ** ADAM_STEP)
    c2 = 1.0 / (1.0 - ADAM_B2 ** ADAM_STEP)

    def body(w_ref, g_ref, m_ref, v_ref, d_ref, nm_ref, nv_ref):
        gv = g_ref[...]
        nm = ADAM_B1 * m_ref[...] + (1.0 - ADAM_B1) * gv
        nv = ADAM_B2 * v_ref[...] + (1.0 - ADAM_B2) * (gv * gv)
        nm_ref[...] = nm
        nv_ref[...] = nv
        d_ref[...] = -ADAM_LR * ((nm * c1) / (jnp.sqrt(nv * c2) + ADAM_EPS) + ADAM_WD * w_ref[...])

    spec = pl.BlockSpec((None, tr, lanes), lambda l, i: (l, i, 0))
    return _pallas(body, name=name, grid=(n_lead, rows // tr), out_shape=[jax.ShapeDtypeStruct(w.shape, F32)] * 3,
                   in_specs=[spec] * 4, out_specs=[spec] * 3, operands=[w, g, m, v],
                   semantics=("parallel", "parallel"), rides=rides)


def _pad_lanes(a, width=LANES):
    return jnp.pad(a, ((0, 0), (0, width - a.shape[1])))


def _group_pad(v, heads):
    return _pad_lanes(v.reshape(SSD_GROUPS, heads))[:, None, :]


def _layer_fwd(x0, p, wt, dims, tag, rides):
    w_attn, heads_g, n_heads, conv_ch = dims["w_attn"], dims["heads_g"], dims["n_heads"], dims["conv_ch"]
    h1 = _rmsnorm_fwd([x0], [[x0.shape[1]]], p["ln1_g"], f"ln1_fwd{tag}")
    proj = _mm(h1, wt("w_in"), name=f"in_proj{tag}", tn=1152, rides=rides)

    outs, lses = [], []
    for d in BRANCH_DILATIONS:
        o, l = _attn_branch_fwd(proj, w_attn, d, n_heads, f"attn_fwd_d{d}{tag}", rides)
        outs.append(o)
        lses.append(l)
    attn, lse = _attn_combine(outs, lses, f"attn_combine{tag}")

    xbc = _conv_fwd(proj, 4 * w_attn, p["conv_w"], p["conv_b"], f"conv_fwd{tag}")
    dt_col = 4 * w_attn + conv_ch
    dtraw_g = jnp.stack([_pad_lanes(proj[:, dt_col + g * heads_g:dt_col + (g + 1) * heads_g])
                         for g in range(SSD_GROUPS)])
    bias_g, alog_g = _group_pad(p["dt_bias"], heads_g), _group_pad(p["a_log"], heads_g)
    dsk_exp = jnp.repeat(p["d_skip"], HEAD_DIM).reshape(SSD_GROUPS, 1, heads_g * HEAD_DIM)
    yg, ypre, states = _ssd_fwd(xbc, proj, dtraw_g, bias_g, alog_g, dsk_exp, heads_g, f"ssd_fwd{tag}")

    gw = heads_g * HEAD_DIM
    mix_g = jnp.concatenate([p["attn_norm_g"], p["ssd_norm_g"]])[None, :]
    mix = _rmsnorm_fwd([attn, yg], [[w_attn], [gw] * SSD_GROUPS], mix_g, f"mix_norm_fwd{tag}")
    x1 = _mm(mix, wt("w_out"), name=f"out_proj{tag}", residual=x0, rides=rides)
    h2 = _rmsnorm_fwd([x1], [[x1.shape[1]]], p["ln2_g"], f"ln2_fwd{tag}")
    u = _mm(h2, wt("w_mlp_in"), name=f"mlp_in{tag}", out_dtype=BF16, tn=1024, rides=rides)
    x2 = _mm(u, wt("w_mlp_out"), name=f"mlp_out{tag}", a_act="relu2", residual=x1, tm=512, rides=rides)
    saved = dict(x0=x0, h1=h1, proj=proj, attn=attn, lse=lse, xbc=xbc, dtraw_g=dtraw_g,
                 bias_g=bias_g, alog_g=alog_g, dsk_exp=dsk_exp, yg=yg, ypre=ypre, states=states, mix=mix,
                 mix_g=mix_g, x1=x1, h2=h2, u=u)
    return x2, saved


def _pair_sums(ex, host, items):
    swapped = ex["rides"].done[("swap", host)]
    core = lax.axis_index("c").astype(jnp.int32).reshape(1)
    for i, (n, l) in enumerate(items):
        ex["pair"][(n, l)] = _pair_sum(ex["bufs"][(n, l)], swapped[i], core, f"pair_sum_{n}_l{l}")


def _layer_bwd(dx2, dx2_b, p, wt, s, dims, l, ex, copy_dx0):
    w_attn, heads_g, n_heads, conv_ch = dims["w_attn"], dims["heads_g"], dims["n_heads"], dims["conv_ch"]
    t_len, d_model = dx2.shape
    gw = heads_g * HEAD_DIM
    h_ssd = heads_g * SSD_GROUPS
    tag, rides, bufs = f"_l{l}", ex["rides"], ex["bufs"]
    du = _mm(dx2_b, wt("w_mlp_out"), name=f"mlp_out_dx{tag}", tb=True, gate=s["u"], out_dtype=BF16, tn=1024,
             rides=rides)
    d_wmo = _mm(s["u"], dx2_b, name=f"mlp_out_dw{tag}", ta=True, a_act="relu2", tm=512, tn=1024, out_dtype=BF16)
    bufs[("w_mlp_out", l)] = d_wmo.reshape(N_DEV, -1, d_model)
    bufs[("w_mlp_in", l)] = _mm(s["h2"], du, name=f"mlp_in_dw{tag}", ta=True, tm=512, tn=1024, out_dtype=BF16,
                                out_chunk=du.shape[1] // N_DEV)
    dh2 = _mm(du, wt("w_mlp_in"), name=f"mlp_in_dx{tag}", tb=True, tm=512, rides=rides)
    _pair_sums(ex, f"mlp_in_dx{tag}", [("w_mlp_out", l), ("w_mlp_in", l)])
    (dx1,), d_ln2, (dx1_b,) = _rmsnorm_bwd([s["x1"]], [[d_model]], p["ln2_g"], dh2, [dx2], f"ln2_bwd{tag}",
                                           bf16_copy=True)
    dmix = _mm(dx1_b, wt("w_out"), name=f"out_proj_dx{tag}", tb=True)
    d_wo = _mm(s["mix"], dx1_b, name=f"out_proj_dw{tag}", ta=True, tm=512, tn=1024, out_dtype=BF16)
    bufs[("w_out", l)] = d_wo.reshape(N_DEV, -1, d_model)
    after_branch = {BRANCH_DILATIONS[0]: [("w_out", l)]}
    (dattn, dyg), d_mix_g, _ = _rmsnorm_bwd([s["attn"], s["yg"]], [[w_attn], [gw] * SSD_GROUPS], s["mix_g"], dmix,
                                           [None, None], f"mix_norm_bwd{tag}")
    dxs, db, dc, dz, ddtraw_g, ssd_small = _ssd_bwd(
        s["xbc"], s["proj"], s["dtraw_g"], s["bias_g"], s["alog_g"], s["dsk_exp"], s["ypre"], s["states"], dyg,
        heads_g, f"ssd_bwd{tag}")
    dxbc = jnp.concatenate([dxs, db, dc], axis=1)
    dxbc_raw, d_conv_w, d_conv_b = _conv_bwd(s["proj"], 4 * w_attn, p["conv_w"], p["conv_b"], dxbc, f"conv_bwd{tag}")
    acc = None
    for d in BRANCH_DILATIONS:
        acc = _attn_branch_bwd(s["proj"], w_attn, s["attn"], s["lse"], dattn, d, n_heads, f"attn_bwd_d{d}{tag}", acc,
                               rides)
        if d in after_branch:
            _pair_sums(ex, f"attn_bwd_d{d}{tag}", after_branch[d])
    w_in = wt("w_in")
    in_proj = 4 * w_attn + conv_ch + h_ssd
    pad = jnp.zeros((t_len, w_in.shape[1] - in_proj), F32)
    dproj = jnp.concatenate([*acc, dz, dxbc_raw] + [ddtraw_g[g, :, :heads_g] for g in range(SSD_GROUPS)] + [pad],
                            axis=1).astype(BF16)
    d_win = _mm(s["h1"], dproj, name=f"in_proj_dw{tag}", ta=True, tm=512, tn=1152, out_dtype=BF16, rides=rides)
    bufs[("w_in", l)] = d_win[:, :in_proj].reshape(d_model, N_DEV, -1).transpose(1, 0, 2)
    dh1 = _mm(dproj, w_in, name=f"in_proj_dx{tag}", tb=True, rides=rides)
    _pair_sums(ex, f"in_proj_dx{tag}", [("w_in", l)])
    (dx0,), d_ln1, dx0_b = _rmsnorm_bwd([s["x0"]], [[d_model]], p["ln1_g"], dh1, [dx1], f"ln1_bwd{tag}",
                                        bf16_copy=copy_dx0)

    small = ssd_small[:, :, :heads_g]
    grads = dict(
        ln1_g=d_ln1[0], conv_w=d_conv_w, conv_b=d_conv_b[0],
        a_log=small[:, 0].reshape(h_ssd), dt_bias=small[:, 1].reshape(h_ssd), d_skip=small[:, 2].reshape(h_ssd),
        attn_norm_g=d_mix_g[0, :w_attn], ssd_norm_g=d_mix_g[0, w_attn:], ln2_g=d_ln2[0])
    return dx0, (dx0_b[0] if copy_dx0 else None), grads


_SMALL = ["ln1_g", "conv_w", "conv_b", "dt_bias", "a_log", "d_skip", "attn_norm_g", "ssd_norm_g", "ln2_g"]
_WEIGHTS = ["ln1_g", "w_in", "conv_w", "conv_b", "dt_bias", "a_log", "d_skip", "attn_norm_g", "ssd_norm_g",
            "w_out", "ln2_g", "w_mlp_in", "w_mlp_out", "final_norm_g"]


def _to_rows(a):
    flat = a.reshape(-1)
    rows = -(-flat.shape[0] // LANES)
    rows = -(-rows // 8) * 8
    return jnp.pad(flat, (0, rows * LANES - flat.shape[0])).reshape(rows, LANES)


def kernel(x, ln1_g, w_in, conv_w, conv_b, dt_bias, a_log, d_skip, attn_norm_g, ssd_norm_g, w_out, ln2_g, w_mlp_in, w_mlp_out, final_norm_g, loss_target, m_ln1_g, m_w_in, m_conv_w, m_conv_b, m_dt_bias, m_a_log, m_d_skip, m_attn_norm_g, m_ssd_norm_g, m_w_out, m_ln2_g, m_w_mlp_in, m_w_mlp_out, m_final_norm_g, v_ln1_g, v_w_in, v_conv_w, v_conv_b, v_dt_bias, v_a_log, v_d_skip, v_attn_norm_g, v_ssd_norm_g, v_w_out, v_ln2_g, v_w_mlp_in, v_w_mlp_out, v_final_norm_g):
    w = dict(ln1_g=ln1_g, w_in=w_in, conv_w=conv_w, conv_b=conv_b, dt_bias=dt_bias, a_log=a_log, d_skip=d_skip,
             attn_norm_g=attn_norm_g, ssd_norm_g=ssd_norm_g, w_out=w_out, ln2_g=ln2_g, w_mlp_in=w_mlp_in,
             w_mlp_out=w_mlp_out, final_norm_g=final_norm_g)
    mom = dict(ln1_g=m_ln1_g, w_in=m_w_in, conv_w=m_conv_w, conv_b=m_conv_b, dt_bias=m_dt_bias, a_log=m_a_log,
               d_skip=m_d_skip, attn_norm_g=m_attn_norm_g, ssd_norm_g=m_ssd_norm_g, w_out=m_w_out, ln2_g=m_ln2_g,
               w_mlp_in=m_w_mlp_in, w_mlp_out=m_w_mlp_out, final_norm_g=m_final_norm_g)
    var = dict(ln1_g=v_ln1_g, w_in=v_w_in, conv_w=v_conv_w, conv_b=v_conv_b, dt_bias=v_dt_bias, a_log=v_a_log,
               d_skip=v_d_skip, attn_norm_g=v_attn_norm_g, ssd_norm_g=v_ssd_norm_g, w_out=v_w_out, ln2_g=v_ln2_g,
               w_mlp_in=v_w_mlp_in, w_mlp_out=v_w_mlp_out, final_norm_g=v_final_norm_g)

    depth, d_model = ln1_g.shape
    t_len = x.shape[1]
    w_attn = attn_norm_g.shape[1]
    h_ssd = dt_bias.shape[1]
    conv_ch = conv_b.shape[1]
    in_proj = w_in.shape[2] * N_DEV
    assert ssd_norm_g.shape[1] == w_attn and in_proj == 4 * w_attn + conv_ch + h_ssd
    assert t_len % (BRANCH_DILATIONS[-1] * ATTN_BLOCK) == 0 and h_ssd % (2 * SSD_GROUPS) == 0
    dims = dict(w_attn=w_attn, heads_g=h_ssd // SSD_GROUPS, n_heads=w_attn // HEAD_DIM, conv_ch=conv_ch)
    names = ["w_in", "w_out", "w_mlp_in", "w_mlp_out"]

    rides = _Rides()
    ex = dict(rides=rides, bufs={}, pair={})
    latest, sent = {}, {}

    def shard(n, l):
        return w[n][l].astype(BF16)

    def half(rows, part):
        return None if part is None else (part * (rows // 2), rows // 2)

    def plan_spread(host, n, l, part=None):
        key, prev = ("spread", host, n, l, part), latest.get((n, l))
        rides.put(host, key, lambda: _GatherSpread([shard(n, l)], rows=half(w[n].shape[1], part),
                                                   into=[rides.done[prev[0]][prev[1]]] if prev else None))
        latest[(n, l)] = (key, 0)

    def plan_pass(host, items):
        key, srcs = ("pass", host), [latest[it] for it in items]
        rides.put(host, key, lambda: _GatherPass([rides.done[k][i] for k, i in srcs]))
        for i, it in enumerate(items):
            latest[it] = (key, i)

    def plan_swap(host, items):
        rides.put(host, ("swap", host), lambda: _SiblingSwap([ex["bufs"][it] for it in items]))

    def plan_send(host, n, l, part=None):
        key, prev = ("send", host, n, l, part), sent.get((n, l))
        rides.put(host, key, lambda: _ChipSend([ex["pair"][(n, l)]], rows=half(ex["pair"][(n, l)].shape[1], part),
                                               into=[rides.done[prev[0]][prev[1]]] if prev else None))
        sent[(n, l)] = (key, 0)

    d_first, d_mid, d_last = (f"d{d}" for d in BRANCH_DILATIONS)
    for l in range(depth):
        t = f"_l{l}"
        if l == 0:
            plan_spread(f"in_proj{t}", "w_out", 0)
            plan_spread(f"in_proj{t}", "w_mlp_in", 0, 0)
            plan_spread(f"attn_fwd_{d_first}{t}", "w_mlp_in", 0, 1)
            plan_spread(f"attn_fwd_{d_mid}{t}", "w_mlp_out", 0, 0)
            plan_pass(f"attn_fwd_{d_mid}{t}", [("w_out", 0), ("w_mlp_in", 0)])
            plan_spread(f"attn_fwd_{d_last}{t}", "w_mlp_out", 0, 1)
            plan_pass(f"out_proj{t}", [("w_mlp_out", 0)])
        else:
            plan_spread(f"in_proj{t}", "w_mlp_out", l, 0)
            plan_spread(f"attn_fwd_{d_first}{t}", "w_mlp_out", l, 1)
            plan_pass(f"attn_fwd_{d_mid}{t}", [("w_mlp_out", l)])
        if l + 1 < depth:
            plan_spread(f"out_proj{t}", "w_out", l + 1)
            plan_spread(f"mlp_in{t}", "w_in", l + 1)
            plan_spread(f"mlp_out{t}", "w_mlp_in", l + 1)
            plan_pass(f"pass_weights_l{l + 1}", [("w_out", l + 1), ("w_in", l + 1), ("w_mlp_in", l + 1)])
        plan_swap(f"mlp_in_dx{t}", [("w_mlp_out", l), ("w_mlp_in", l)])
        plan_send(f"attn_bwd_{d_first}{t}", "w_mlp_out", l, 0)
        plan_swap(f"attn_bwd_{d_first}{t}", [("w_out", l)])
        plan_send(f"attn_bwd_{d_mid}{t}", "w_mlp_out", l, 1)
        plan_send(f"attn_bwd_{d_last}{t}", "w_mlp_in", l)
        plan_send(f"in_proj_dw{t}", "w_out", l)
        plan_swap(f"in_proj_dx{t}", [("w_in", l)])
        if l > 0:
            plan_send(f"mlp_out_dx_l{l - 1}", "w_in", l)
        else:
            plan_send("adamw_w_mlp_in", "w_in", l, 0)
            plan_send("adamw_w_mlp_out", "w_in", l, 1)

    g_in0, g_cw = _all_gather([shard("w_in", 0), conv_w], "gather_first")
    full_cw = _with_own(g_cw, conv_w).transpose(1, 2, 0, 3).reshape(depth, SSD_CONV, conv_ch)
    proj_cols = -(-in_proj // LANES) * LANES
    full = {}

    def weight(n, l):
        if (n, l) not in full:
            if (n, l) == ("w_in", 0):
                g = g_in0
            else:
                key, i = latest[(n, l)]
                g = rides.done[key][i]
            g = _with_own(g, shard(n, l))
            if n == "w_in":
                g = _pad_lanes(g.transpose(1, 0, 2).reshape(d_model, in_proj), proj_cols)
            elif n == "w_mlp_in":
                g = g.transpose(1, 0, 2).reshape(d_model, -1)
            else:
                g = g.reshape(-1, d_model)
            full[(n, l)] = g
        return full[(n, l)]

    layers = [dict(ln1_g=ln1_g[l][None, :], ln2_g=ln2_g[l][None, :], conv_w=full_cw[l], conv_b=conv_b[l][None, :],
                   dt_bias=dt_bias[l], a_log=a_log[l], d_skip=d_skip[l], attn_norm_g=attn_norm_g[l],
                   ssd_norm_g=ssd_norm_g[l]) for l in range(depth)]

    h = x[0]
    saved = []
    for l in range(depth):
        h, s = _layer_fwd(h, layers[l], functools.partial(lambda n, l: weight(n, l), l=l), dims, f"_l{l}", rides)
        saved.append(s)
        if l + 1 < depth:
            _alone(rides, f"pass_weights_l{l + 1}")
    dh, d_final_g, loss_part, dh_b = _loss_head(h, final_norm_g[None, :], loss_target[0], "loss_head")

    grads = [None] * depth
    for l in reversed(range(depth)):
        dh, dh_b, grads[l] = _layer_bwd(dh, dh_b, layers[l], functools.partial(lambda n, l: weight(n, l), l=l),
                                        saved[l], dims, l, ex, copy_dx0=l > 0)
    grad_x = dh[None]

    my_chip = (2 * lax.axis_index("x") + lax.axis_index("y")).astype(jnp.int32).reshape(1)
    gsum, delta, new_m, new_v = {}, {}, {}, {}
    for n in names[1:] + names[:1]:
        per_layer = []
        for l in range(depth):
            key, i = sent[(n, l)]
            per_layer.append(_sum_with_own(rides.done[key][i], ex["pair"][(n, l)], my_chip, f"sum_{n}_l{l}"))
        gsum[n] = jnp.stack(per_layer)
        delta[n], new_m[n], new_v[n] = _adamw(w[n], gsum[n], mom[n], var[n], f"adamw_{n}", rides)

    small_parts = [jnp.stack([grads[l][n] for l in range(depth)]).reshape(-1) for n in _SMALL]
    small_parts += [d_final_g.reshape(-1), loss_part[0, :1]]
    sizes = [int(a.shape[0]) for a in small_parts]
    packed = _to_rows(jnp.concatenate(small_parts))
    (gathered,) = _all_gather([packed], "gather_small_grads")
    total = _sum_leading(_with_own(gathered, packed), "sum_small_grads").reshape(-1)
    offs = np.cumsum([0] + sizes)
    pieces = [total[offs[i]:offs[i + 1]] for i in range(len(sizes))]
    for n, piece in zip(_SMALL, pieces):
        shape = (depth, SSD_CONV, conv_ch) if n == "conv_w" else w[n].shape
        gsum[n] = piece.reshape(shape)
    gsum["final_norm_g"] = pieces[len(_SMALL)]
    loss = pieces[len(_SMALL) + 1][0]
    my_id = 4 * lax.axis_index("x") + 2 * lax.axis_index("y") + lax.axis_index("c")
    cw = conv_w.shape[2]
    gsum["conv_w"] = lax.dynamic_slice_in_dim(gsum["conv_w"], my_id * cw, cw, axis=2)

    small_names = [n for n in _WEIGHTS if n not in names]
    sm_sizes = [int(np.prod(w[n].shape)) for n in small_names]
    pack = lambda d: _to_rows(jnp.concatenate([d[n].reshape(-1) for n in small_names]))[None]
    outs = _adamw(pack(w), pack(gsum), pack(mom), pack(var), "adamw_small")
    sm_offs = np.cumsum([0] + sm_sizes)
    for res, o in zip((delta, new_m, new_v), outs):
        flat = o.reshape(-1)
        for i, n in enumerate(small_names):
            res[n] = flat[sm_offs[i]:sm_offs[i + 1]].reshape(w[n].shape)

    return (loss, grad_x, *[gsum[n] for n in _WEIGHTS], *[delta[n] for n in _WEIGHTS],
            *[new_m[n] for n in _WEIGHTS], *[new_v[n] for n in _WEIGHTS])
```

```python
import functools
import math

import numpy as np
import jax
import jax.numpy as jnp
from jax import lax
from jax.experimental import pallas as pl
from jax.experimental.pallas import tpu as pltpu

F32 = jnp.float32
BF16 = jnp.bfloat16

N_DEV = 8
LANES = 128
HEAD_DIM = 64
ATTN_BLOCK = 128
BRANCH_DILATIONS = (1, 4, 16)
SSD_GROUPS = 2
SSD_STATE = 128
SSD_CHUNK = 128
SSD_CONV = 4
NORM_EPS = 1e-5
ADAM_LR, ADAM_B1, ADAM_B2, ADAM_EPS, ADAM_WD, ADAM_STEP = 0.001, 0.9, 0.999, 1e-08, 0.01, 10
VMEM_LIMIT_BYTES = 56 * 1024 * 1024
MESH = pl.DeviceIdType.MESH
NEG_INF = float("-inf")


def _params(*sem):
    return pltpu.CompilerParams(dimension_semantics=tuple(sem), vmem_limit_bytes=VMEM_LIMIT_BYTES)


def _pick(n, target, mult):
    best = None
    for t in range(mult, min(n, target) + 1, mult):
        if n % t == 0:
            best = t
    assert best is not None, (n, target, mult)
    return best


def _dot(a, b, ca, cb):
    return lax.dot_general(a, b, (((ca,), (cb,)), ((), ())), preferred_element_type=F32)


def _split3(v):
    hi = v.astype(BF16)
    r = v - hi.astype(F32)
    mid = r.astype(BF16)
    lo = (r - mid.astype(F32)).astype(BF16)
    return hi, mid, lo


def _dot_exact(v, sel, ca, cb):
    hi, mid, lo = _split3(v)
    return _dot(hi, sel, ca, cb) + _dot(mid, sel, ca, cb) + _dot(lo, sel, ca, cb)


_HBM = pl.BlockSpec(memory_space=pltpu.HBM)


def _all_gather(xs, name):
    n = len(xs)

    def body(*refs):
        x_refs, o_refs = refs[:n], refs[n:2 * n]
        send_sems, recv_sems = refs[2 * n:]
        x, y, c = lax.axis_index("x"), lax.axis_index("y"), lax.axis_index("c")
        me, sibling = (x, y, c), (x, y, 1 - c)
        chips = [(1 - x, y), (x, 1 - y), (1 - x, 1 - y)]

        def copy(t, k, block, to, src=None):
            bx, by, bc = block
            dst = o_refs[t].at[4 * bx + 2 * by + bc]
            return pltpu.make_async_remote_copy(
                src_ref=dst if src is None else src, dst_ref=dst,
                send_sem=send_sems.at[t, k], recv_sem=recv_sems.at[t, k],
                device_id=to, device_id_type=MESH)

        first, passed = [], []
        for t in range(n):
            cps = [copy(t, 0, me, sibling, src=x_refs[t])]
            cps += [copy(t, 1 + j, me, (*chip, c), src=x_refs[t]) for j, chip in enumerate(chips)]
            for cp in cps:
                cp.start()
            first += cps
        for t in range(n):
            for j, chip in enumerate(chips):
                copy(t, 1 + j, (*chip, c), me).wait_recv()
                fwd = copy(t, 4 + j, (*chip, c), sibling)
                fwd.start()
                passed.append(fwd)
        for t in range(n):
            copy(t, 0, sibling, me).wait_recv()
            back = copy(t, 7, sibling, sibling)
            back.start()
            passed.append(back)
        for t in range(n):
            copy(t, 7, me, me).wait_recv()
            for j, chip in enumerate(chips):
                copy(t, 4 + j, (*chip, 1 - c), me).wait_recv()
        for cp in first + passed:
            cp.wait_send()

    return pl.pallas_call(
        body, name=name,
        out_shape=[jax.ShapeDtypeStruct((N_DEV,) + a.shape, a.dtype) for a in xs],
        in_specs=[_HBM] * n, out_specs=[_HBM] * n,
        scratch_shapes=[pltpu.SemaphoreType.DMA((n, 8)), pltpu.SemaphoreType.DMA((n, 8))],
    )(*xs)


def _place():
    x, y, c = lax.axis_index("x"), lax.axis_index("y"), lax.axis_index("c")
    return x, y, c, 4 * x + 2 * y + c, (x, y, 1 - c), [(1 - x, y), (x, 1 - y), (1 - x, 1 - y)]


def _remote(src, dst, send_sem, recv_sem, to):
    return pltpu.make_async_remote_copy(src_ref=src, dst_ref=dst, send_sem=send_sem, recv_sem=recv_sem,
                                        device_id=to, device_id_type=MESH)


class _Riding:
    aliases = {}

    def copies(self, ins, outs, sems):
        raise NotImplementedError

    def start(self, ins, outs, sems):
        local, out, _ = self.copies(ins, outs, sems)
        for cp in local + out:
            cp.start()

    def wait(self, ins, outs, sems):
        local, out, landing = self.copies(ins, outs, sems)
        for cp in landing:
            cp.wait_recv()
        for cp in out:
            cp.wait_send()
        for cp in local:
            cp.wait()


def _rows_of(ref, rows):
    return ref if rows is None else ref.at[pl.ds(rows[0], rows[1])]


class _GatherSpread(_Riding):
    def __init__(self, xs, rows=None, into=None):
        n = len(xs)
        self.rows = rows
        self.ins = list(xs) + list(into or [])
        self.out_shapes = [jax.ShapeDtypeStruct((N_DEV,) + a.shape, a.dtype) for a in xs]
        self.aliases = {n + t: t for t in range(n)} if into else {}
        self.sem_shapes = [pltpu.SemaphoreType.DMA((n, 4)), pltpu.SemaphoreType.DMA((n, 4))]

    def copies(self, ins, outs, sems):
        send, recv = sems
        _, _, c, me, sibling, chips = _place()
        targets = [sibling] + [(*chip, c) for chip in chips]
        out, landing = [], []
        for t in range(len(outs)):
            src = _rows_of(ins[t], self.rows)
            for k, to in enumerate(targets):
                out.append(_remote(src, _rows_of(outs[t].at[me], self.rows), send.at[t, k], recv.at[t, k], to))
                theirs = _rows_of(outs[t].at[4 * to[0] + 2 * to[1] + to[2]], self.rows)
                landing.append(_remote(src, theirs, send.at[t, k], recv.at[t, k], to))
        return [], out, landing


class _GatherPass(_Riding):
    def __init__(self, bufs):
        n = len(bufs)
        self.ins = list(bufs)
        self.out_shapes = [jax.ShapeDtypeStruct(b.shape, b.dtype) for b in bufs]
        self.aliases = {t: t for t in range(n)}
        self.sem_shapes = [pltpu.SemaphoreType.DMA((n, 4)), pltpu.SemaphoreType.DMA((n, 4))]

    def copies(self, ins, outs, sems):
        send, recv = sems
        x, y, c, me, sibling, chips = _place()
        out, landing = [], []
        for t in range(len(outs)):
            for j, (px, py) in enumerate(chips + [(x, y)]):
                held = outs[t].at[4 * px + 2 * py + c] if j < 3 else outs[t].at[4 * x + 2 * y + 1 - c]
                lands = outs[t].at[4 * px + 2 * py + 1 - c] if j < 3 else outs[t].at[me]
                out.append(_remote(held, held, send.at[t, j], recv.at[t, j], sibling))
                landing.append(_remote(held, lands, send.at[t, j], recv.at[t, j], sibling))
        return [], out, landing


class _SiblingSwap(_Riding):
    def __init__(self, xs):
        n = len(xs)
        self.ins = list(xs)
        self.out_shapes = [jax.ShapeDtypeStruct((N_DEV // 2,) + a.shape[1:], a.dtype) for a in xs]
        self.sem_shapes = [pltpu.SemaphoreType.DMA((n, 4)), pltpu.SemaphoreType.DMA((n, 4))]

    def copies(self, ins, outs, sems):
        send, recv = sems
        _, _, c, _, sibling, _ = _place()
        out = [_remote(ins[t].at[2 * q + 1 - c], outs[t].at[q], send.at[t, q], recv.at[t, q], sibling)
               for t in range(len(ins)) for q in range(N_DEV // 2)]
        return [], out, out


class _ChipSend(_Riding):
    def __init__(self, ps, rows=None, into=None):
        n = len(ps)
        self.rows = rows
        self.ins = list(ps) + list(into or [])
        self.out_shapes = [jax.ShapeDtypeStruct((3,) + a.shape[1:], a.dtype) for a in ps]
        self.aliases = {n + t: t for t in range(n)} if into else {}
        self.sem_shapes = [pltpu.SemaphoreType.DMA((n, 3)), pltpu.SemaphoreType.DMA((n, 3))]

    def copies(self, ins, outs, sems):
        send, recv = sems
        _, _, c, _, _, chips = _place()
        out = [_remote(_rows_of(ins[t].at[2 * px + py], self.rows), _rows_of(outs[t].at[j], self.rows),
                       send.at[t, j], recv.at[t, j], (px, py, c))
               for t in range(len(outs)) for j, (px, py) in enumerate(chips)]
        return [], out, out


class _Bundle(_Riding):
    def __init__(self, comms):
        self.comms = comms
        self.ins = [a for cm in comms for a in cm.ins]
        self.out_shapes = [s for cm in comms for s in cm.out_shapes]
        self.sem_shapes = [s for cm in comms for s in cm.sem_shapes]
        self.aliases = {}
        i0 = o0 = 0
        for cm in comms:
            self.aliases.update({i0 + i: o0 + j for i, j in cm.aliases.items()})
            i0, o0 = i0 + len(cm.ins), o0 + len(cm.out_shapes)

    def copies(self, ins, outs, sems):
        local, out, landing = [], [], []
        i0 = o0 = s0 = 0
        for cm in self.comms:
            i1, o1, s1 = i0 + len(cm.ins), o0 + len(cm.out_shapes), s0 + len(cm.sem_shapes)
            a, b, c = cm.copies(ins[i0:i1], outs[o0:o1], sems[s0:s1])
            local, out, landing = local + a, out + b, landing + c
            i0, o0, s0 = i1, o1, s1
        return local, out, landing


class _Rides:
    def __init__(self):
        self.plan, self.done, self.aboard = {}, {}, {}

    def put(self, host, key, make):
        self.plan.setdefault(host, []).append((key, make))

    def board(self, host):
        if host not in self.plan:
            return None
        self.aboard[host] = [make() for _, make in self.plan[host]]
        return _Bundle(self.aboard[host])

    def land(self, host, results):
        o0 = 0
        for (key, _), cm in zip(self.plan[host], self.aboard[host]):
            self.done[key] = list(results[o0:o0 + len(cm.out_shapes)])
            o0 += len(cm.out_shapes)


def _pallas(body, *, name, grid, out_shape, in_specs, out_specs, operands, semantics, scratch_shapes=(), rides=None):
    comm = rides.board(name) if rides is not None else None
    if comm is None:
        return pl.pallas_call(
            body, name=name, grid=grid, out_shape=list(out_shape), in_specs=list(in_specs),
            out_specs=list(out_specs), scratch_shapes=list(scratch_shapes), compiler_params=_params(*semantics),
        )(*operands)
    n_in, n_out, n_scr = len(in_specs), len(out_shape), len(scratch_shapes)
    n_ci, n_co = len(comm.ins), len(comm.out_shapes)

    def hosted(*refs):
        cuts = np.cumsum([0, n_in, n_ci, n_out, n_co, n_scr])
        ins, c_ins, outs, c_outs, scr = (refs[cuts[i]:cuts[i + 1]] for i in range(5))
        sems = refs[cuts[5]:]
        ids = [pl.program_id(a) for a in range(len(grid))]
        first = functools.reduce(jnp.logical_and, [i == 0 for i in ids])
        last = functools.reduce(jnp.logical_and, [i == g - 1 for i, g in zip(ids, grid)])

        @pl.when(first)
        def _():
            comm.start(c_ins, c_outs, sems)

        body(*ins, *outs, *scr)

        @pl.when(last)
        def _():
            comm.wait(c_ins, c_outs, sems)

    results = pl.pallas_call(
        hosted, name=name, grid=grid, out_shape=list(out_shape) + comm.out_shapes,
        in_specs=list(in_specs) + [_HBM] * n_ci, out_specs=list(out_specs) + [_HBM] * n_co,
        scratch_shapes=list(scratch_shapes) + comm.sem_shapes,
        input_output_aliases={n_in + i: n_out + j for i, j in comm.aliases.items()},
        compiler_params=_params(*["arbitrary"] * len(grid)),
    )(*operands, *comm.ins)
    rides.land(name, results[n_out:])
    return results[:n_out]


def _alone(rides, name):
    comm = rides.board(name)

    def body(*refs):
        n_ci, n_co = len(comm.ins), len(comm.out_shapes)
        ins, outs, sems = refs[:n_ci], refs[n_ci:n_ci + n_co], refs[n_ci + n_co:]
        comm.start(ins, outs, sems)
        comm.wait(ins, outs, sems)

    results = pl.pallas_call(
        body, name=name, out_shape=comm.out_shapes, in_specs=[_HBM] * len(comm.ins),
        out_specs=[_HBM] * len(comm.out_shapes), scratch_shapes=comm.sem_shapes,
        input_output_aliases=dict(comm.aliases),
    )(*comm.ins)
    rides.land(name, results)


def _row_tile(rows, cols, itemsize, copies, budget=24 * 1024 * 1024):
    padded = -(-cols // LANES) * LANES
    mult = 8 * (4 // itemsize)
    if rows % mult:
        return rows
    return _pick(rows, max(mult, budget // (copies * padded * itemsize)), mult)


def _sum_leading(x, name):
    n_src, rows, cols = x.shape
    tr = _row_tile(rows, cols, 4, 2 * (n_src + 2))

    def body(x_ref, o_ref):
        acc = x_ref[0].astype(F32)
        for s in range(1, n_src):
            acc = acc + x_ref[s].astype(F32)
        o_ref[...] = acc

    return pl.pallas_call(
        body, name=name, grid=(rows // tr,), out_shape=jax.ShapeDtypeStruct((rows, cols), F32),
        in_specs=[pl.BlockSpec((n_src, tr, cols), lambda i: (0, i, 0))],
        out_specs=pl.BlockSpec((tr, cols), lambda i: (i, 0)), compiler_params=_params("parallel"),
    )(x)


def _pair_sum(buf, theirs, core, name):
    n_q, rows, cols = theirs.shape
    tr = _row_tile(rows, cols, 4, 6)

    def body(core_ref, mine_ref, theirs_ref, o_ref):
        o_ref[...] = (mine_ref[...].astype(F32) + theirs_ref[...].astype(F32)).astype(BF16)

    spec = pl.BlockSpec((None, tr, cols), lambda q, i, core_ref: (q, i, 0))
    return pl.pallas_call(
        body, name=name, out_shape=jax.ShapeDtypeStruct(theirs.shape, BF16),
        grid_spec=pltpu.PrefetchScalarGridSpec(
            num_scalar_prefetch=1, grid=(n_q, rows // tr),
            in_specs=[pl.BlockSpec((None, tr, cols), lambda q, i, core_ref: (2 * q + core_ref[0], i, 0)), spec],
            out_specs=spec),
        compiler_params=_params("parallel", "parallel"),
    )(core, buf, theirs)


def _sum_with_own(recv, pair, chip, name):
    n_src, rows, cols = recv.shape
    tr = _row_tile(rows, cols, 4, 2 * (n_src + 3))

    def body(chip_ref, own_ref, recv_ref, o_ref):
        acc = own_ref[...].astype(F32)
        for s in range(n_src):
            acc = acc + recv_ref[s].astype(F32)
        o_ref[...] = acc

    return pl.pallas_call(
        body, name=name, out_shape=jax.ShapeDtypeStruct((rows, cols), F32),
        grid_spec=pltpu.PrefetchScalarGridSpec(
            num_scalar_prefetch=1, grid=(rows // tr,),
            in_specs=[pl.BlockSpec((None, tr, cols), lambda i, chip_ref: (chip_ref[0], i, 0)),
                      pl.BlockSpec((n_src, tr, cols), lambda i, chip_ref: (0, i, 0))],
            out_specs=pl.BlockSpec((tr, cols), lambda i, chip_ref: (i, 0))),
        compiler_params=_params("parallel"),
    )(chip, pair, recv)


def _mm(a, b, *, name, ta=False, tb=False, tm=1024, tn=512, out_dtype=F32, a_act=None,
        residual=None, gate=None, out_chunk=None, rides=None):
    k_dim, m = (a.shape if ta else a.shape[::-1])
    n, kb = (b.shape if tb else b.shape[::-1])
    assert kb == k_dim, (a.shape, b.shape, ta, tb)
    tm, tn = _pick(m, tm, 128), _pick(out_chunk or n, tn, 128)
    ca, cb = (0 if ta else 1), (1 if tb else 0)
    a_spec = pl.BlockSpec((k_dim, tm), lambda i, j: (0, i)) if ta else pl.BlockSpec((tm, k_dim), lambda i, j: (i, 0))
    b_spec = pl.BlockSpec((tn, k_dim), lambda i, j: (j, 0)) if tb else pl.BlockSpec((k_dim, tn), lambda i, j: (0, j))
    mn_spec = pl.BlockSpec((tm, tn), lambda i, j: (i, j))
    if out_chunk:
        per = out_chunk // tn
        o_spec = pl.BlockSpec((None, tm, tn), lambda i, j: (j // per, i, j % per))
        out_shape = jax.ShapeDtypeStruct((n // out_chunk, m, out_chunk), out_dtype)
    else:
        o_spec = mn_spec
        out_shape = jax.ShapeDtypeStruct((m, n), out_dtype)
    operands, in_specs = [a, b], [a_spec, b_spec]
    for extra in (gate, residual):
        if extra is not None:
            operands.append(extra)
            in_specs.append(mn_spec)

    def body(*refs):
        a_ref, b_ref, o_ref = refs[0], refs[1], refs[-1]
        extras = list(refs[2:-1])
        gate_ref = extras.pop(0) if gate is not None else None
        res_ref = extras.pop(0) if residual is not None else None
        av = a_ref[...].astype(BF16)
        if a_act == "relu2":
            av = jnp.square(jnp.maximum(av, jnp.zeros_like(av)))
        r = _dot(av, b_ref[...].astype(BF16), ca, cb)
        if gate_ref is not None:
            r = r * (2.0 * jnp.maximum(gate_ref[...].astype(F32), 0.0))
        if res_ref is not None:
            r = r + res_ref[...].astype(F32)
        o_ref[...] = r.astype(out_dtype)

    return _pallas(body, name=name, grid=(m // tm, n // tn), out_shape=[out_shape], in_specs=in_specs,
                   out_specs=[o_spec], operands=operands, semantics=("parallel", "arbitrary"), rides=rides)[0]


def _rmsnorm_fwd(xs, seg_widths, g, name, tm=256):
    t_len = xs[0].shape[0]
    width = sum(x.shape[1] for x in xs)
    tm = _pick(t_len, tm, 16)
    n = len(xs)

    def body(*refs):
        x_refs, g_ref, o_ref = refs[:n], refs[n], refs[n + 1]
        col = 0
        for x_ref, widths in zip(x_refs, seg_widths):
            off = 0
            for w in widths:
                xv = x_ref[:, off:off + w].astype(F32)
                r = lax.rsqrt(jnp.mean(xv * xv, axis=1, keepdims=True) + NORM_EPS)
                o_ref[:, col:col + w] = (xv * r * g_ref[:, col:col + w]).astype(BF16)
                off += w
                col += w

    return pl.pallas_call(
        body, name=name, grid=(t_len // tm,),
        out_shape=jax.ShapeDtypeStruct((t_len, width), BF16),
        in_specs=[pl.BlockSpec((tm, x.shape[1]), lambda i: (i, 0)) for x in xs]
        + [pl.BlockSpec((1, width), lambda i: (0, 0))],
        out_specs=pl.BlockSpec((tm, width), lambda i: (i, 0)),
        compiler_params=_params("parallel"),
    )(*xs, g)


def _rmsnorm_bwd(xs, seg_widths, g, dh, residuals, name, tm=256, bf16_copy=False):
    t_len = xs[0].shape[0]
    width = sum(x.shape[1] for x in xs)
    tm = _pick(t_len, tm, 8)
    n = len(xs)
    has_res = [r is not None for r in residuals]
    res_ops = [r for r in residuals if r is not None]

    def body(*refs):
        x_refs, g_ref, dh_ref = refs[:n], refs[n], refs[n + 1]
        res_refs = list(refs[n + 2:n + 2 + len(res_ops)])
        dx_refs = refs[n + 2 + len(res_ops):n + 2 + len(res_ops) + n]
        dg_ref = refs[n + 2 + len(res_ops) + n]
        copy_refs = refs[n + 3 + len(res_ops) + n:]
        first = pl.program_id(0) == 0
        col = 0
        for idx, (x_ref, widths) in enumerate(zip(x_refs, seg_widths)):
            res_ref = res_refs.pop(0) if has_res[idx] else None
            off = 0
            for w in widths:
                xv = x_ref[:, off:off + w].astype(F32)
                r = lax.rsqrt(jnp.mean(xv * xv, axis=1, keepdims=True) + NORM_EPS)
                xh = xv * r
                dhv = dh_ref[:, col:col + w].astype(F32)
                gd = dhv * g_ref[:, col:col + w]
                dx = r * (gd - xh * jnp.mean(gd * xh, axis=1, keepdims=True))
                if res_ref is not None:
                    dx = dx + res_ref[:, off:off + w]
                dx_refs[idx][:, off:off + w] = dx
                if bf16_copy:
                    copy_refs[idx][:, off:off + w] = dx.astype(BF16)
                part = jnp.sum(dhv * xh, axis=0, keepdims=True)

                @pl.when(first)
                def _(part=part, col=col, w=w):
                    dg_ref[:, col:col + w] = part

                @pl.when(jnp.logical_not(first))
                def _(part=part, col=col, w=w):
                    dg_ref[:, col:col + w] += part
                off += w
                col += w

    outs = pl.pallas_call(
        body, name=name, grid=(t_len // tm,),
        out_shape=[jax.ShapeDtypeStruct(x.shape, F32) for x in xs] + [jax.ShapeDtypeStruct((1, width), F32)]
        + ([jax.ShapeDtypeStruct(x.shape, BF16) for x in xs] if bf16_copy else []),
        in_specs=[pl.BlockSpec((tm, x.shape[1]), lambda i: (i, 0)) for x in xs]
        + [pl.BlockSpec((1, width), lambda i: (0, 0)), pl.BlockSpec((tm, width), lambda i: (i, 0))]
        + [pl.BlockSpec((tm, r.shape[1]), lambda i: (i, 0)) for r in res_ops],
        out_specs=[pl.BlockSpec((tm, x.shape[1]), lambda i: (i, 0)) for x in xs]
        + [pl.BlockSpec((1, width), lambda i: (0, 0))]
        + ([pl.BlockSpec((tm, x.shape[1]), lambda i: (i, 0)) for x in xs] if bf16_copy else []),
        compiler_params=_params("arbitrary"),
    )(*xs, g, dh, *res_ops)
    return outs[:n], outs[n], outs[n + 1:]


def _loss_head(x, g, target, name, tm=256):
    t_len, d = x.shape
    tm = _pick(t_len, tm, 8)

    def body(x_ref, g_ref, t_ref, dx_ref, dg_ref, loss_ref, dxb_ref):
        first = pl.program_id(0) == 0
        xv = x_ref[...]
        r = lax.rsqrt(jnp.mean(xv * xv, axis=1, keepdims=True) + NORM_EPS)
        xh = xv * r
        gv = g_ref[...]
        err = xh * gv - t_ref[...]
        part_loss = 0.5 * jnp.sum(jnp.mean(err * err, axis=1, keepdims=True), axis=0, keepdims=True)
        dy = err * (1.0 / d)
        gd = dy * gv
        dx = r * (gd - xh * jnp.mean(gd * xh, axis=1, keepdims=True))
        dx_ref[...] = dx
        dxb_ref[...] = dx.astype(BF16)
        part_g = jnp.sum(dy * xh, axis=0, keepdims=True)
        part_loss = jnp.broadcast_to(part_loss, (1, LANES))

        @pl.when(first)
        def _():
            dg_ref[...] = part_g
            loss_ref[...] = part_loss

        @pl.when(jnp.logical_not(first))
        def _():
            dg_ref[...] += part_g
            loss_ref[...] += part_loss

    return pl.pallas_call(
        body, name=name, grid=(t_len // tm,),
        out_shape=[jax.ShapeDtypeStruct((t_len, d), F32), jax.ShapeDtypeStruct((1, d), F32),
                   jax.ShapeDtypeStruct((1, LANES), F32), jax.ShapeDtypeStruct((t_len, d), BF16)],
        in_specs=[pl.BlockSpec((tm, d), lambda i: (i, 0)), pl.BlockSpec((1, d), lambda i: (0, 0)),
                  pl.BlockSpec((tm, d), lambda i: (i, 0))],
        out_specs=[pl.BlockSpec((tm, d), lambda i: (i, 0)), pl.BlockSpec((1, d), lambda i: (0, 0)),
                   pl.BlockSpec((1, LANES), lambda i: (0, 0)), pl.BlockSpec((tm, d), lambda i: (i, 0))],
        compiler_params=_params("arbitrary"),
    )(x, g, target)


def _alibi_slope(h, n_heads):
    return jnp.exp(jnp.full((1, 1), -8.0 * math.log(2.0) / n_heads, F32) * (h + 1).astype(F32))


def _attn_tiles(d, w):
    return ATTN_BLOCK * d, (w if d == 1 else LANES)


def _residue_rows(r, d):
    return pl.ds(r, ATTN_BLOCK, stride=d) if d > 1 else pl.ds(0, ATTN_BLOCK)


def _attn_masks(first_block):
    i = lax.broadcasted_iota(jnp.int32, (2 * ATTN_BLOCK, 2 * ATTN_BLOCK), 0) % ATTN_BLOCK
    j = lax.broadcasted_iota(jnp.int32, (2 * ATTN_BLOCK, 2 * ATTN_BLOCK), 1)
    delta = i - j + ATTN_BLOCK
    valid = jnp.logical_and(delta >= 0, delta <= ATTN_BLOCK)
    valid = jnp.logical_and(valid, jnp.logical_or(j >= ATTN_BLOCK, jnp.logical_not(first_block)))
    return valid, delta.astype(F32)


def _stack_heads(x, masks):
    zero = jnp.zeros_like(x)
    return jnp.concatenate([jnp.where(masks[0], x, zero), jnp.where(masks[1], x, zero)], axis=0)


def _unstack_heads(x2, masks):
    return jnp.where(masks[0], x2[:ATTN_BLOCK], x2[ATTN_BLOCK:])


def _pair_slopes(first_head, p, n_heads, d):
    row = lax.broadcasted_iota(jnp.int32, (2 * ATTN_BLOCK, 1), 0)
    sa, sb = (_alibi_slope(first_head + 2 * p + hh, n_heads) * d for hh in range(2))
    return jnp.where(row < ATTN_BLOCK, sa, sb)


def _head_lane_masks():
    lane = lax.broadcasted_iota(jnp.int32, (ATTN_BLOCK, LANES), 1)
    return [lane < HEAD_DIM, lane >= HEAD_DIM]


def _attn_branch_fwd(proj, w, dilation, n_heads, name, rides=None):
    t_len = proj.shape[0]
    d = dilation
    rows, lw = _attn_tiles(d, w)
    nb = t_len // rows
    n_pairs = lw // LANES
    per = w // lw
    scale = HEAD_DIM ** -0.5

    def body(q_ref, kp_ref, kc_ref, vp_ref, vc_ref, o_ref, lse_ref):
        first_head = pl.program_id(0) * (2 * n_pairs)
        first_block = pl.program_id(1) == 0
        valid, delta = _attn_masks(first_block)
        masks = _head_lane_masks()
        ones = jnp.ones((2 * ATTN_BLOCK, LANES), BF16)
        for p in range(n_pairs):
            cols = pl.ds(p * LANES, LANES)
            bias = _pair_slopes(first_head, p, n_heads, d) * delta
            for r in range(d):
                rs = _residue_rows(r, d)
                q2 = _stack_heads((q_ref[rs, cols] * scale).astype(BF16), masks)
                k2 = jnp.concatenate([kp_ref[rs, cols], kc_ref[rs, cols]], axis=0).astype(BF16)
                v2 = jnp.concatenate([vp_ref[rs, cols], vc_ref[rs, cols]], axis=0).astype(BF16)
                s = jnp.where(valid, _dot(q2, k2, 1, 1) - bias, NEG_INF)
                m = jnp.max(s, axis=1, keepdims=True)
                pr = jnp.exp(s - m).astype(BF16)
                den = _dot(pr, ones, 1, 0)
                o_ref[rs, cols] = _unstack_heads(_dot(pr, v2, 1, 0) / den, masks)
                lse_ref[rs, cols] = _unstack_heads(m + jnp.log(den), masks)

    def spec(which, prev):
        if prev:
            return pl.BlockSpec((rows, lw), lambda b, n: (jnp.maximum(n - 1, 0), which * per + b))
        return pl.BlockSpec((rows, lw), lambda b, n: (n, which * per + b))

    o_spec = pl.BlockSpec((rows, lw), lambda b, n: (n, b))
    return _pallas(
        body, name=name, grid=(per, nb), out_shape=[jax.ShapeDtypeStruct((t_len, w), F32)] * 2,
        in_specs=[spec(0, False), spec(1, True), spec(1, False), spec(2, True), spec(2, False)],
        out_specs=[o_spec, o_spec], operands=[proj] * 5, semantics=("parallel", "parallel"), rides=rides)


def _attn_combine(outs, lses, name, tm=512):
    t_len, w = outs[0].shape
    tm = _pick(t_len, tm, 8)
    nbr = len(outs)

    def body(*refs):
        o_refs, l_refs = refs[:nbr], refs[nbr:2 * nbr]
        out_ref, lse_ref = refs[2 * nbr:]
        ls = [r[...] for r in l_refs]
        m = functools.reduce(jnp.maximum, ls)
        es = [jnp.exp(l - m) for l in ls]
        den = functools.reduce(lambda a, b: a + b, es)
        num = functools.reduce(lambda a, b: a + b, [e * r[...] for e, r in zip(es, o_refs)])
        out_ref[...] = num / den
        lse_ref[...] = m + jnp.log(den)

    spec = pl.BlockSpec((tm, w), lambda i: (i, 0))
    return pl.pallas_call(
        body, name=name, grid=(t_len // tm,),
        out_shape=[jax.ShapeDtypeStruct((t_len, w), F32)] * 2,
        in_specs=[spec] * (2 * nbr), out_specs=[spec, spec],
        compiler_params=_params("parallel"),
    )(*outs, *lses)


def _attn_branch_bwd(proj, w, out, lse, dout, dilation, n_heads, name, acc=None, rides=None):
    t_len = proj.shape[0]
    d = dilation
    rows, lw = _attn_tiles(d, w)
    nb = t_len // rows
    n_pairs = lw // LANES
    per = w // lw
    scale = HEAD_DIM ** -0.5
    n_acc = 0 if acc is None else 3

    def body(*refs):
        q_ref, kp_ref, kc_ref, vp_ref, vc_ref, out_ref, lse_ref, do_ref = refs[:8]
        acc_refs = refs[8:8 + n_acc]
        dq_ref, dk_ref, dv_ref, dk_carry, dv_carry = refs[8 + n_acc:]
        first_head = pl.program_id(0) * (2 * n_pairs)
        n = pl.program_id(1)
        first_block = n == 0
        valid, dist = _attn_masks(first_block)
        masks = _head_lane_masks()

        def plus(value, idx, *where):
            return value + acc_refs[idx][where] if n_acc else value

        @pl.when(first_block)
        def _():
            dk_carry[...] = jnp.zeros_like(dk_carry)
            dv_carry[...] = jnp.zeros_like(dv_carry)

        @pl.when(n < nb)
        def _():
            for p in range(n_pairs):
                cols = pl.ds(p * LANES, LANES)
                bias = _pair_slopes(first_head, p, n_heads, d) * dist
                for r in range(d):
                    rs = _residue_rows(r, d)
                    q2 = _stack_heads((q_ref[rs, cols] * scale).astype(BF16), masks)
                    k2 = jnp.concatenate([kp_ref[rs, cols], kc_ref[rs, cols]], axis=0).astype(BF16)
                    v2 = jnp.concatenate([vp_ref[rs, cols], vc_ref[rs, cols]], axis=0).astype(BF16)
                    do = do_ref[rs, cols]
                    do2 = _stack_heads(do.astype(BF16), masks)
                    do_out = do * out_ref[rs, cols]
                    lse_all = lse_ref[rs, cols]
                    delta = jnp.concatenate([jnp.sum(jnp.where(masks[hh], do_out, 0.0), axis=1, keepdims=True)
                                             for hh in range(2)], axis=0)
                    lse2 = jnp.concatenate([jnp.max(jnp.where(masks[hh], lse_all, NEG_INF), axis=1, keepdims=True)
                                            for hh in range(2)], axis=0)
                    s = jnp.where(valid, _dot(q2, k2, 1, 1) - bias, NEG_INF)
                    pr = jnp.exp(s - lse2)
                    ds = (pr * (_dot(do2, v2, 1, 1) - delta)).astype(BF16)
                    dq = _unstack_heads(_dot(ds, k2, 1, 0), masks)
                    dk2 = _dot(ds, q2, 0, 0)
                    dv2 = _dot(pr.astype(BF16), do2, 0, 0)
                    dq_ref[rs, cols] = plus(dq * scale, 0, rs, cols)
                    dk_ref[rs, cols] = plus(dk_carry[r, :, cols] + dk2[:ATTN_BLOCK], 1, rs, cols)
                    dv_ref[rs, cols] = plus(dv_carry[r, :, cols] + dv2[:ATTN_BLOCK], 2, rs, cols)
                    dk_carry[r, :, cols] = dk2[ATTN_BLOCK:]
                    dv_carry[r, :, cols] = dv2[ATTN_BLOCK:]

        @pl.when(n == nb)
        def _():
            for r in range(d):
                rs = _residue_rows(r, d)
                dk_ref[rs, :] = plus(dk_carry[r], 1, rs, slice(None))
                dv_ref[rs, :] = plus(dv_carry[r], 2, rs, slice(None))

    def qkv_spec(which, shift):
        return pl.BlockSpec((rows, lw), lambda b, n: (jnp.clip(n - shift, 0, nb - 1), which * per + b))

    q_like = pl.BlockSpec((rows, lw), lambda b, n: (jnp.minimum(n, nb - 1), b))
    k_like = pl.BlockSpec((rows, lw), lambda b, n: (jnp.maximum(n - 1, 0), b))
    return _pallas(
        body, name=name, grid=(per, nb + 1), out_shape=[jax.ShapeDtypeStruct((t_len, w), F32)] * 3,
        in_specs=[qkv_spec(0, 0), qkv_spec(1, 1), qkv_spec(1, 0), qkv_spec(2, 1), qkv_spec(2, 0),
                  q_like, q_like, q_like] + [q_like, k_like, k_like][:n_acc],
        out_specs=[q_like, k_like, k_like], operands=[proj] * 5 + [out, lse, dout, *(acc or ())],
        scratch_shapes=[pltpu.VMEM((d, ATTN_BLOCK, lw), F32), pltpu.VMEM((d, ATTN_BLOCK, lw), F32)],
        semantics=("parallel", "arbitrary"), rides=rides)


def _shift_down(u, s):
    if s == 0:
        return u
    row = lax.broadcasted_iota(jnp.int32, u.shape, 0)
    return jnp.where(row >= s, pltpu.roll(u, s, 0), 0.0)


def _shift_up(u, s):
    if s == 0:
        return u
    n = u.shape[0]
    row = lax.broadcasted_iota(jnp.int32, u.shape, 0)
    return jnp.where(row < n - s, pltpu.roll(u, n - s, 0), 0.0)


def _conv_fwd(u, col0, w, b, name):
    t_len, ch = u.shape[0], w.shape[1]
    blk0 = col0 // LANES

    def body(u_ref, w_ref, b_ref, o_ref):
        uv = u_ref[...]
        pre = b_ref[...] + jnp.zeros_like(uv)
        for k in range(SSD_CONV):
            pre = pre + w_ref[k:k + 1, :] * _shift_down(uv, SSD_CONV - 1 - k)
        o_ref[...] = pre * jax.nn.sigmoid(pre)

    return pl.pallas_call(
        body, name=name, grid=(ch // LANES,),
        out_shape=jax.ShapeDtypeStruct((t_len, ch), F32),
        in_specs=[pl.BlockSpec((t_len, LANES), lambda j: (0, blk0 + j)),
                  pl.BlockSpec((SSD_CONV, LANES), lambda j: (0, j)), pl.BlockSpec((1, LANES), lambda j: (0, j))],
        out_specs=pl.BlockSpec((t_len, LANES), lambda j: (0, j)),
        compiler_params=_params("parallel"),
    )(u, w, b)


def _conv_bwd(u, col0, w, b, dact, name):
    t_len, ch = u.shape[0], w.shape[1]
    blk0 = col0 // LANES

    def body(u_ref, w_ref, b_ref, da_ref, du_ref, dw_ref, db_ref):
        uv = u_ref[...]
        shifted = [_shift_down(uv, SSD_CONV - 1 - k) for k in range(SSD_CONV)]
        pre = b_ref[...] + jnp.zeros_like(uv)
        for k in range(SSD_CONV):
            pre = pre + w_ref[k:k + 1, :] * shifted[k]
        sig = jax.nn.sigmoid(pre)
        dpre = da_ref[...] * (sig * (1.0 + pre * (1.0 - sig)))
        du = jnp.zeros_like(uv)
        for k in range(SSD_CONV):
            du = du + w_ref[k:k + 1, :] * _shift_up(dpre, SSD_CONV - 1 - k)
            dw_ref[k:k + 1, :] = jnp.sum(dpre * shifted[k], axis=0, keepdims=True)
        du_ref[...] = du
        db_ref[...] = jnp.sum(dpre, axis=0, keepdims=True)

    col = pl.BlockSpec((t_len, LANES), lambda j: (0, j))
    w_spec = pl.BlockSpec((SSD_CONV, LANES), lambda j: (0, j))
    b_spec = pl.BlockSpec((1, LANES), lambda j: (0, j))
    return pl.pallas_call(
        body, name=name, grid=(ch // LANES,),
        out_shape=[jax.ShapeDtypeStruct((t_len, ch), F32), jax.ShapeDtypeStruct((SSD_CONV, ch), F32),
                   jax.ShapeDtypeStruct((1, ch), F32)],
        in_specs=[pl.BlockSpec((t_len, LANES), lambda j: (0, blk0 + j)), w_spec, b_spec, col],
        out_specs=[col, w_spec, b_spec],
        compiler_params=_params("parallel"),
    )(u, w, b, dact)


def _cumsum_rows(v):
    n = v.shape[0]
    row = lax.broadcasted_iota(jnp.int32, v.shape, 0)
    s = 1
    while s < n:
        v = v + jnp.where(row >= s, pltpu.roll(v, s, 0), 0.0)
        s *= 2
    return v


def _rev_cumsum_rows(v):
    n = v.shape[0]
    row = lax.broadcasted_iota(jnp.int32, v.shape, 0)
    s = 1
    while s < n:
        v = v + jnp.where(row < n - s, pltpu.roll(v, n - s, 0), 0.0)
        s *= 2
    return v


def _head_selector(heads, width):
    j = lax.broadcasted_iota(jnp.int32, (LANES, width), 0)
    lane = lax.broadcasted_iota(jnp.int32, (LANES, width), 1)
    return jnp.where(jnp.logical_and(lane // HEAD_DIM == j, j < heads), 1.0, 0.0).astype(BF16)


class _SsdChunk:
    def __init__(self, dtraw_ref, bias_ref, alog_ref, xs_ref, b_ref, c_ref, heads):
        q = SSD_CHUNK
        width = heads * HEAD_DIM
        lane = lax.broadcasted_iota(jnp.int32, (q, LANES), 1)
        self.head_lanes = lane < heads
        lane1 = lax.broadcasted_iota(jnp.int32, (1, LANES), 1)
        self.a = jnp.where(lane1 < heads, -jnp.exp(alog_ref[...]), 0.0)
        self.dt_arg = dtraw_ref[...] + bias_ref[...]
        self.dt = jnp.where(self.head_lanes, jax.nn.softplus(self.dt_arg), 0.0)
        self.cum = _cumsum_rows(self.dt * self.a)
        self.cum_t = self.cum.T
        last = self.cum[q - 1:q, :]
        self.sel = _head_selector(heads, width)
        self.expand = lambda v: _dot_exact(v, self.sel, 1, 0)
        self.segsum = lambda v: _dot_exact(v, self.sel, 1, 1)
        self.e_exp = self.expand(jnp.exp(self.cum))
        self.d_exp = self.expand(jnp.exp(last - self.cum))
        self.elast_exp = self.e_exp[q - 1:q, :]
        self.dt_exp = self.expand(self.dt)
        self.xs = xs_ref[...]
        self.x = self.xs * self.dt_exp
        self.xb = self.x.astype(BF16)
        self.bb = b_ref[...].astype(BF16)
        self.cb = c_ref[...].astype(BF16)
        self.cbm = _dot(self.cb, self.bb, 1, 1)
        li = lax.broadcasted_iota(jnp.int32, (q, q), 0)
        si = lax.broadcasted_iota(jnp.int32, (q, q), 1)
        self.tri = li >= si
        hl = lax.broadcasted_iota(jnp.int32, (q, LANES), 1)
        self.pair_masks = [hl < HEAD_DIM, hl >= HEAD_DIM]

    def decay(self, j):
        diff = self.cum[:, j:j + 1] - self.cum_t[j:j + 1, :]
        return jnp.exp(jnp.where(self.tri, diff, NEG_INF))


def _ssd_specs(t_len, heads, n_chunks, xbc_cols, rev):
    q, gw = SSD_CHUNK, heads * HEAD_DIM
    ssd_w = SSD_GROUPS * gw
    b_blk = ssd_w // SSD_STATE
    ch = (lambda c: n_chunks - 1 - c) if rev else (lambda c: c)
    return dict(
        dtraw=pl.BlockSpec((None, q, LANES), lambda g, c: (g, ch(c), 0)),
        small=pl.BlockSpec((None, 1, LANES), lambda g, c: (g, 0, 0)),
        dsk=pl.BlockSpec((None, 1, gw), lambda g, c: (g, 0, 0)),
        xs=pl.BlockSpec((q, gw), lambda g, c: (ch(c), g)),
        b=pl.BlockSpec((q, SSD_STATE), lambda g, c: (ch(c), b_blk + g)),
        c=pl.BlockSpec((q, SSD_STATE), lambda g, c: (ch(c), b_blk + SSD_GROUPS + g)),
        z=pl.BlockSpec((q, gw), lambda g, c: (ch(c), 3 * SSD_GROUPS + g)),
        tok=pl.BlockSpec((q, gw), lambda g, c: (ch(c), g)),
        state=pl.BlockSpec((None, SSD_STATE, gw), lambda g, c: (ch(c), 0, g)),
        bc=pl.BlockSpec((q, SSD_STATE), lambda g, c: (ch(c), g)),
    )


def _ssd_fwd(xbc, qkvz, dtraw_g, bias_g, alog_g, dsk_exp, heads, name):
    t_len = xbc.shape[0]
    q, gw = SSD_CHUNK, heads * HEAD_DIM
    n_chunks = t_len // q
    ssd_w = SSD_GROUPS * gw
    sp = _ssd_specs(t_len, heads, n_chunks, xbc.shape[1], rev=False)

    def body(dtraw_ref, bias_ref, alog_ref, dsk_ref, xs_ref, b_ref, c_ref, z_ref,
             yg_ref, ypre_ref, st_ref, s_scr):
        @pl.when(pl.program_id(1) == 0)
        def _():
            s_scr[...] = jnp.zeros_like(s_scr)

        k = _SsdChunk(dtraw_ref, bias_ref, alog_ref, xs_ref, b_ref, c_ref, heads)
        s_prev = s_scr[...]
        st_ref[...] = s_prev
        y_off = k.e_exp * _dot(k.cb, s_prev.astype(BF16), 1, 0)
        parts = []
        for p in range(heads // 2):
            xp = k.xb[:, p * LANES:(p + 1) * LANES]
            acc = jnp.zeros((q, LANES), F32)
            for hh in range(2):
                m = (k.cbm * k.decay(2 * p + hh)).astype(BF16)
                acc = acc + _dot(m, jnp.where(k.pair_masks[hh], xp, jnp.zeros_like(xp)), 1, 0)
            parts.append(acc)
        y = jnp.concatenate(parts, axis=1) + y_off
        xd = (k.x * k.d_exp).astype(BF16)
        s_scr[...] = k.elast_exp * s_prev + _dot(k.bb, xd, 0, 0)
        y_pre = y + dsk_ref[...] * k.xs
        zv = z_ref[...]
        ypre_ref[...] = y_pre
        yg_ref[...] = y_pre * (zv * jax.nn.sigmoid(zv))

    return pl.pallas_call(
        body, name=name, grid=(SSD_GROUPS, n_chunks),
        out_shape=[jax.ShapeDtypeStruct((t_len, ssd_w), F32), jax.ShapeDtypeStruct((t_len, ssd_w), F32),
                   jax.ShapeDtypeStruct((n_chunks, SSD_STATE, ssd_w), F32)],
        in_specs=[sp["dtraw"], sp["small"], sp["small"], sp["dsk"], sp["xs"], sp["b"], sp["c"], sp["z"]],
        out_specs=[sp["tok"], sp["tok"], sp["state"]],
        scratch_shapes=[pltpu.VMEM((SSD_STATE, gw), F32)],
        compiler_params=_params("parallel", "arbitrary"),
    )(dtraw_g, bias_g, alog_g, dsk_exp, xbc, xbc, xbc, qkvz)


def _ssd_bwd(xbc, qkvz, dtraw_g, bias_g, alog_g, dsk_exp, ypre, states, dyg, heads, name):
    t_len = xbc.shape[0]
    q, gw = SSD_CHUNK, heads * HEAD_DIM
    n_chunks = t_len // q
    ssd_w = SSD_GROUPS * gw
    sp = _ssd_specs(t_len, heads, n_chunks, xbc.shape[1], rev=True)

    def body(dtraw_ref, bias_ref, alog_ref, dsk_ref, xs_ref, b_ref, c_ref, z_ref, ypre_ref, st_ref, dyg_ref,
             dxs_ref, db_ref, dc_ref, dz_ref, ddt_ref, small_ref, g_scr):
        first = pl.program_id(1) == 0

        @pl.when(first)
        def _():
            g_scr[...] = jnp.zeros_like(g_scr)

        k = _SsdChunk(dtraw_ref, bias_ref, alog_ref, xs_ref, b_ref, c_ref, heads)
        zv = z_ref[...]
        sig = jax.nn.sigmoid(zv)
        dyg = dyg_ref[...]
        y_pre = ypre_ref[...]
        dy = dyg * (zv * sig)
        dz_ref[...] = dyg * y_pre * (sig * (1.0 + zv * (1.0 - sig)))
        dsk = dsk_ref[...]
        g_next = g_scr[...]
        s_prev = st_ref[...]
        sb = s_prev.astype(BF16)
        xd = k.x * k.d_exp
        xdb = xd.astype(BF16)
        gb = g_next.astype(BF16)
        dx_off = k.d_exp * _dot(k.bb, gb, 1, 0)
        dyb = dy.astype(BF16)
        dcb = jnp.zeros((q, q), F32)
        lane = lax.broadcasted_iota(jnp.int32, (q, LANES), 1)
        row_t = lax.broadcasted_iota(jnp.int32, (LANES, q), 0)
        w_rows = jnp.zeros((q, LANES), F32)
        w_cols_t = jnp.zeros((LANES, q), F32)
        parts = []
        for p in range(heads // 2):
            cols = slice(p * LANES, (p + 1) * LANES)
            dyp, xp = dyb[:, cols], k.xb[:, cols]
            acc = jnp.zeros((q, LANES), F32)
            for hh in range(2):
                j = 2 * p + hh
                lm = k.decay(j)
                m32 = k.cbm * lm
                dym = jnp.where(k.pair_masks[hh], dyp, jnp.zeros_like(dyp))
                acc = acc + _dot(m32.astype(BF16), dym, 0, 0)
                dm = _dot(dym, xp, 1, 1)
                dcb = dcb + dm * lm
                wmat = dm * m32
                w_rows = w_rows + jnp.where(lane == j, jnp.sum(wmat, axis=1, keepdims=True), 0.0)
                w_cols_t = w_cols_t + jnp.where(row_t == j, jnp.sum(wmat, axis=0, keepdims=True), 0.0)
            parts.append(acc)
        dx = jnp.concatenate(parts, axis=1) + dx_off
        dcbb = dcb.astype(BF16)
        edy = (k.e_exp * dy).astype(BF16)
        dc_ref[...] = _dot(dcbb, k.bb, 1, 0) + _dot(edy, sb, 1, 1)
        db_ref[...] = _dot(dcbb, k.cb, 0, 0) + _dot(xdb, gb, 1, 1)
        g_scr[...] = k.elast_exp * g_next + _dot(k.cb, edy, 0, 0)

        y_off = k.e_exp * _dot(k.cb, sb, 1, 0)
        dcum = w_rows - w_cols_t.T + k.segsum(dy * y_off)
        t_term = k.segsum(k.x * dx_off)
        gs = jnp.broadcast_to(jnp.sum(g_next * s_prev, axis=0, keepdims=True), (8, gw))
        carried = k.segsum(gs)[0:1, :] * jnp.exp(k.cum[q - 1:q, :])
        dda = _rev_cumsum_rows(dcum) + (_cumsum_rows(t_term) - t_term) + carried
        ddt = jnp.where(k.head_lanes, dda * k.a + k.segsum(dx * k.xs), 0.0)
        ddtraw = ddt * jax.nn.sigmoid(k.dt_arg)
        ddt_ref[...] = ddtraw
        dxs_ref[...] = dx * k.dt_exp + dsk * dy
        ds = jnp.broadcast_to(jnp.sum(dy * k.xs, axis=0, keepdims=True), (8, gw))
        d_alog = jnp.sum(jnp.where(k.head_lanes, dda * k.dt, 0.0), axis=0, keepdims=True) * k.a
        rows8 = lax.broadcasted_iota(jnp.int32, (8, LANES), 0)
        small = jnp.where(rows8 == 0, d_alog, 0.0)
        small = small + jnp.where(rows8 == 1, jnp.sum(ddtraw, axis=0, keepdims=True), 0.0)
        small = small + jnp.where(rows8 == 2, k.segsum(ds)[0:1, :], 0.0)

        @pl.when(first)
        def _():
            small_ref[...] = small

        @pl.when(jnp.logical_not(first))
        def _():
            small_ref[...] += small

    bc_out = sp["bc"]
    return pl.pallas_call(
        body, name=name, grid=(SSD_GROUPS, n_chunks),
        out_shape=[jax.ShapeDtypeStruct((t_len, ssd_w), F32),
                   jax.ShapeDtypeStruct((t_len, SSD_GROUPS * SSD_STATE), F32),
                   jax.ShapeDtypeStruct((t_len, SSD_GROUPS * SSD_STATE), F32),
                   jax.ShapeDtypeStruct((t_len, ssd_w), F32),
                   jax.ShapeDtypeStruct((SSD_GROUPS, t_len, LANES), F32),
                   jax.ShapeDtypeStruct((SSD_GROUPS, 8, LANES), F32)],
        in_specs=[sp["dtraw"], sp["small"], sp["small"], sp["dsk"], sp["xs"], sp["b"], sp["c"], sp["z"],
                  sp["tok"], sp["state"], sp["tok"]],
        out_specs=[sp["tok"], bc_out, bc_out, sp["tok"], sp["dtraw"],
                   pl.BlockSpec((None, 8, LANES), lambda g, c: (g, 0, 0))],
        scratch_shapes=[pltpu.VMEM((SSD_STATE, gw), F32)],
        compiler_params=_params("parallel", "arbitrary"),
    )(dtraw_g, bias_g, alog_g, dsk_exp, xbc, xbc, xbc, qkvz, ypre, states, dyg)


def _adamw(w, g, m, v, name, rides=None):
    n_lead, rows, lanes = w.shape
    tr = _row_tile(rows, lanes, 4, 14)
    c1 = 1.0 / (1.0 - ADAM_B1 ** ADAM_STEP)
    c2 = 1.0 / (1.0 - ADAM_B2 ** ADAM_STEP)

    def body(w_ref, g_ref, m_ref, v_ref, d_ref, nm_ref, nv_ref):
        gv = g_ref[...]
        nm = ADAM_B1 * m_ref[...] + (1.0 - ADAM_B1) * gv
        nv = ADAM_B2 * v_ref[...] + (1.0 - ADAM_B2) * (gv * gv)
        nm_ref[...] = nm
        nv_ref[...] = nv
        d_ref[...] = -ADAM_LR * ((nm * c1) / (jnp.sqrt(nv * c2) + ADAM_EPS) + ADAM_WD * w_ref[...])

    spec = pl.BlockSpec((None, tr, lanes), lambda l, i: (l, i, 0))
    return _pallas(body, name=name, grid=(n_lead, rows // tr), out_shape=[jax.ShapeDtypeStruct(w.shape, F32)] * 3,
                   in_specs=[spec] * 4, out_specs=[spec] * 3, operands=[w, g, m, v],
                   semantics=("parallel", "parallel"), rides=rides)


def _pad_lanes(a, width=LANES):
    return jnp.pad(a, ((0, 0), (0, width - a.shape[1])))


def _group_pad(v, heads):
    return _pad_lanes(v.reshape(SSD_GROUPS, heads))[:, None, :]


def _layer_fwd(x0, p, wt, dims, tag, rides):
    w_attn, heads_g, n_heads, conv_ch = dims["w_attn"], dims["heads_g"], dims["n_heads"], dims["conv_ch"]
    h1 = _rmsnorm_fwd([x0], [[x0.shape[1]]], p["ln1_g"], f"ln1_fwd{tag}")
    proj = _mm(h1, wt("w_in"), name=f"in_proj{tag}", tn=1152, rides=rides)

    outs, lses = [], []
    for d in BRANCH_DILATIONS:
        o, l = _attn_branch_fwd(proj, w_attn, d, n_heads, f"attn_fwd_d{d}{tag}", rides)
        outs.append(o)
        lses.append(l)
    attn, lse = _attn_combine(outs, lses, f"attn_combine{tag}")

    xbc = _conv_fwd(proj, 4 * w_attn, p["conv_w"], p["conv_b"], f"conv_fwd{tag}")
    dt_col = 4 * w_attn + conv_ch
    dtraw_g = jnp.stack([_pad_lanes(proj[:, dt_col + g * heads_g:dt_col + (g + 1) * heads_g])
                         for g in range(SSD_GROUPS)])
    bias_g, alog_g = _group_pad(p["dt_bias"], heads_g), _group_pad(p["a_log"], heads_g)
    dsk_exp = jnp.repeat(p["d_skip"], HEAD_DIM).reshape(SSD_GROUPS, 1, heads_g * HEAD_DIM)
    yg, ypre, states = _ssd_fwd(xbc, proj, dtraw_g, bias_g, alog_g, dsk_exp, heads_g, f"ssd_fwd{tag}")

    gw = heads_g * HEAD_DIM
    mix_g = jnp.concatenate([p["attn_norm_g"], p["ssd_norm_g"]])[None, :]
    mix = _rmsnorm_fwd([attn, yg], [[w_attn], [gw] * SSD_GROUPS], mix_g, f"mix_norm_fwd{tag}")
    x1 = _mm(mix, wt("w_out"), name=f"out_proj{tag}", residual=x0, rides=rides)
    h2 = _rmsnorm_fwd([x1], [[x1.shape[1]]], p["ln2_g"], f"ln2_fwd{tag}")
    u = _mm(h2, wt("w_mlp_in"), name=f"mlp_in{tag}", out_dtype=BF16, tn=1024, rides=rides)
    x2 = _mm(u, wt("w_mlp_out"), name=f"mlp_out{tag}", a_act="relu2", residual=x1, tn=256, rides=rides)
    saved = dict(x0=x0, h1=h1, proj=proj, attn=attn, lse=lse, xbc=xbc, dtraw_g=dtraw_g,
                 bias_g=bias_g, alog_g=alog_g, dsk_exp=dsk_exp, yg=yg, ypre=ypre, states=states, mix=mix,
                 mix_g=mix_g, x1=x1, h2=h2, u=u)
    return x2, saved


def _pair_sums(ex, host, items):
    swapped = ex["rides"].done[("swap", host)]
    core = lax.axis_index("c").astype(jnp.int32).reshape(1)
    for i, (n, l) in enumerate(items):
        ex["pair"][(n, l)] = _pair_sum(ex["bufs"][(n, l)], swapped[i], core, f"pair_sum_{n}_l{l}")


def _layer_bwd(dx2, dx2_b, p, wt, s, dims, l, ex, copy_dx0):
    w_attn, heads_g, n_heads, conv_ch = dims["w_attn"], dims["heads_g"], dims["n_heads"], dims["conv_ch"]
    t_len, d_model = dx2.shape
    gw = heads_g * HEAD_DIM
    h_ssd = heads_g * SSD_GROUPS
    tag, rides, bufs = f"_l{l}", ex["rides"], ex["bufs"]
    du = _mm(dx2_b, wt("w_mlp_out"), name=f"mlp_out_dx{tag}", tb=True, gate=s["u"], out_dtype=BF16, tn=1024,
             rides=rides)
    d_wmo = _mm(s["u"], dx2_b, name=f"mlp_out_dw{tag}", ta=True, a_act="relu2", tm=512, tn=1024, out_dtype=BF16)
    bufs[("w_mlp_out", l)] = d_wmo.reshape(N_DEV, -1, d_model)
    bufs[("w_mlp_in", l)] = _mm(s["h2"], du, name=f"mlp_in_dw{tag}", ta=True, tm=512, tn=1024, out_dtype=BF16,
                                out_chunk=du.shape[1] // N_DEV)
    dh2 = _mm(du, wt("w_mlp_in"), name=f"mlp_in_dx{tag}", tb=True, tn=256, rides=rides)
    _pair_sums(ex, f"mlp_in_dx{tag}", [("w_mlp_out", l), ("w_mlp_in", l)])
    (dx1,), d_ln2, (dx1_b,) = _rmsnorm_bwd([s["x1"]], [[d_model]], p["ln2_g"], dh2, [dx2], f"ln2_bwd{tag}",
                                           bf16_copy=True)
    dmix = _mm(dx1_b, wt("w_out"), name=f"out_proj_dx{tag}", tb=True)
    d_wo = _mm(s["mix"], dx1_b, name=f"out_proj_dw{tag}", ta=True, tm=512, tn=1024, out_dtype=BF16)
    bufs[("w_out", l)] = d_wo.reshape(N_DEV, -1, d_model)
    after_branch = {BRANCH_DILATIONS[0]: [("w_out", l)]}
    (dattn, dyg), d_mix_g, _ = _rmsnorm_bwd([s["attn"], s["yg"]], [[w_attn], [gw] * SSD_GROUPS], s["mix_g"], dmix,
                                           [None, None], f"mix_norm_bwd{tag}")
    dxs, db, dc, dz, ddtraw_g, ssd_small = _ssd_bwd(
        s["xbc"], s["proj"], s["dtraw_g"], s["bias_g"], s["alog_g"], s["dsk_exp"], s["ypre"], s["states"], dyg,
        heads_g, f"ssd_bwd{tag}")
    dxbc = jnp.concatenate([dxs, db, dc], axis=1)
    dxbc_raw, d_conv_w, d_conv_b = _conv_bwd(s["proj"], 4 * w_attn, p["conv_w"], p["conv_b"], dxbc, f"conv_bwd{tag}")
    acc = None
    for d in BRANCH_DILATIONS:
        acc = _attn_branch_bwd(s["proj"], w_attn, s["attn"], s["lse"], dattn, d, n_heads, f"attn_bwd_d{d}{tag}", acc,
                               rides)
        if d in after_branch:
            _pair_sums(ex, f"attn_bwd_d{d}{tag}", after_branch[d])
    w_in = wt("w_in")
    in_proj = 4 * w_attn + conv_ch + h_ssd
    pad = jnp.zeros((t_len, w_in.shape[1] - in_proj), F32)
    dproj = jnp.concatenate([*acc, dz, dxbc_raw] + [ddtraw_g[g, :, :heads_g] for g in range(SSD_GROUPS)] + [pad],
                            axis=1).astype(BF16)
    d_win = _mm(s["h1"], dproj, name=f"in_proj_dw{tag}", ta=True, tm=512, tn=1152, out_dtype=BF16, rides=rides)
    bufs[("w_in", l)] = d_win[:, :in_proj].reshape(d_model, N_DEV, -1).transpose(1, 0, 2)
    dh1 = _mm(dproj, w_in, name=f"in_proj_dx{tag}", tb=True, rides=rides)
    _pair_sums(ex, f"in_proj_dx{tag}", [("w_in", l)])
    (dx0,), d_ln1, dx0_b = _rmsnorm_bwd([s["x0"]], [[d_model]], p["ln1_g"], dh1, [dx1], f"ln1_bwd{tag}",
                                        bf16_copy=copy_dx0)

    small = ssd_small[:, :, :heads_g]
    grads = dict(
        ln1_g=d_ln1[0], conv_w=d_conv_w, conv_b=d_conv_b[0],
        a_log=small[:, 0].reshape(h_ssd), dt_bias=small[:, 1].reshape(h_ssd), d_skip=small[:, 2].reshape(h_ssd),
        attn_norm_g=d_mix_g[0, :w_attn], ssd_norm_g=d_mix_g[0, w_attn:], ln2_g=d_ln2[0])
    return dx0, (dx0_b[0] if copy_dx0 else None), grads


_SMALL = ["ln1_g", "conv_w", "conv_b", "dt_bias", "a_log", "d_skip", "attn_norm_g", "ssd_norm_g", "ln2_g"]
_WEIGHTS = ["ln1_g", "w_in", "conv_w", "conv_b", "dt_bias", "a_log", "d_skip", "attn_norm_g", "ssd_norm_g",
            "w_out", "ln2_g", "w_mlp_in", "w_mlp_out", "final_norm_g"]


def _to_rows(a):
    flat = a.reshape(-1)
    rows = -(-flat.shape[0] // LANES)
    rows = -(-rows // 8) * 8
    return jnp.pad(flat, (0, rows * LANES - flat.shape[0])).reshape(rows, LANES)


def kernel(x, ln1_g, w_in, conv_w, conv_b, dt_bias, a_log, d_skip, attn_norm_g, ssd_norm_g, w_out, ln2_g, w_mlp_in, w_mlp_out, final_norm_g, loss_target, m_ln1_g, m_w_in, m_conv_w, m_conv_b, m_dt_bias, m_a_log, m_d_skip, m_attn_norm_g, m_ssd_norm_g, m_w_out, m_ln2_g, m_w_mlp_in, m_w_mlp_out, m_final_norm_g, v_ln1_g, v_w_in, v_conv_w, v_conv_b, v_dt_bias, v_a_log, v_d_skip, v_attn_norm_g, v_ssd_norm_g, v_w_out, v_ln2_g, v_w_mlp_in, v_w_mlp_out, v_final_norm_g):
    w = dict(ln1_g=ln1_g, w_in=w_in, conv_w=conv_w, conv_b=conv_b, dt_bias=dt_bias, a_log=a_log, d_skip=d_skip,
             attn_norm_g=attn_norm_g, ssd_norm_g=ssd_norm_g, w_out=w_out, ln2_g=ln2_g, w_mlp_in=w_mlp_in,
             w_mlp_out=w_mlp_out, final_norm_g=final_norm_g)
    mom = dict(ln1_g=m_ln1_g, w_in=m_w_in, conv_w=m_conv_w, conv_b=m_conv_b, dt_bias=m_dt_bias, a_log=m_a_log,
               d_skip=m_d_skip, attn_norm_g=m_attn_norm_g, ssd_norm_g=m_ssd_norm_g, w_out=m_w_out, ln2_g=m_ln2_g,
               w_mlp_in=m_w_mlp_in, w_mlp_out=m_w_mlp_out, final_norm_g=m_final_norm_g)
    var = dict(ln1_g=v_ln1_g, w_in=v_w_in, conv_w=v_conv_w, conv_b=v_conv_b, dt_bias=v_dt_bias, a_log=v_a_log,
               d_skip=v_d_skip, attn_norm_g=v_attn_norm_g, ssd_norm_g=v_ssd_norm_g, w_out=v_w_out, ln2_g=v_ln2_g,
               w_mlp_in=v_w_mlp_in, w_mlp_out=v_w_mlp_out, final_norm_g=v_final_norm_g)

    depth, d_model = ln1_g.shape
    t_len = x.shape[1]
    w_attn = attn_norm_g.shape[1]
    h_ssd = dt_bias.shape[1]
    conv_ch = conv_b.shape[1]
    in_proj = w_in.shape[2] * N_DEV
    assert ssd_norm_g.shape[1] == w_attn and in_proj == 4 * w_attn + conv_ch + h_ssd
    assert t_len % (BRANCH_DILATIONS[-1] * ATTN_BLOCK) == 0 and h_ssd % (2 * SSD_GROUPS) == 0
    dims = dict(w_attn=w_attn, heads_g=h_ssd // SSD_GROUPS, n_heads=w_attn // HEAD_DIM, conv_ch=conv_ch)
    names = ["w_in", "w_out", "w_mlp_in", "w_mlp_out"]

    rides = _Rides()
    ex = dict(rides=rides, bufs={}, pair={})
    latest, sent = {}, {}

    def shard(n, l):
        return w[n][l].astype(BF16)

    def half(rows, part):
        return None if part is None else (part * (rows // 2), rows // 2)

    def plan_spread(host, n, l, part=None):
        key, prev = ("spread", host, n, l, part), latest.get((n, l))
        rides.put(host, key, lambda: _GatherSpread([shard(n, l)], rows=half(w[n].shape[1], part),
                                                   into=[rides.done[prev[0]][prev[1]]] if prev else None))
        latest[(n, l)] = (key, 0)

    def plan_pass(host, items):
        key, srcs = ("pass", host), [latest[it] for it in items]
        rides.put(host, key, lambda: _GatherPass([rides.done[k][i] for k, i in srcs]))
        for i, it in enumerate(items):
            latest[it] = (key, i)

    def plan_swap(host, items):
        rides.put(host, ("swap", host), lambda: _SiblingSwap([ex["bufs"][it] for it in items]))

    def plan_send(host, n, l, part=None):
        key, prev = ("send", host, n, l, part), sent.get((n, l))
        rides.put(host, key, lambda: _ChipSend([ex["pair"][(n, l)]], rows=half(ex["pair"][(n, l)].shape[1], part),
                                               into=[rides.done[prev[0]][prev[1]]] if prev else None))
        sent[(n, l)] = (key, 0)

    d_first, d_mid, d_last = (f"d{d}" for d in BRANCH_DILATIONS)
    for l in range(depth):
        t = f"_l{l}"
        if l == 0:
            plan_spread(f"in_proj{t}", "w_out", 0)
            plan_spread(f"in_proj{t}", "w_mlp_in", 0, 0)
            plan_spread(f"attn_fwd_{d_first}{t}", "w_mlp_in", 0, 1)
            plan_spread(f"attn_fwd_{d_mid}{t}", "w_mlp_out", 0, 0)
            plan_pass(f"attn_fwd_{d_mid}{t}", [("w_out", 0), ("w_mlp_in", 0)])
            plan_spread(f"attn_fwd_{d_last}{t}", "w_mlp_out", 0, 1)
            plan_pass(f"out_proj{t}", [("w_mlp_out", 0)])
        else:
            plan_spread(f"in_proj{t}", "w_mlp_out", l, 0)
            plan_spread(f"attn_fwd_{d_first}{t}", "w_mlp_out", l, 1)
            plan_pass(f"attn_fwd_{d_mid}{t}", [("w_mlp_out", l)])
        if l + 1 < depth:
            plan_spread(f"out_proj{t}", "w_out", l + 1)
            plan_spread(f"mlp_in{t}", "w_in", l + 1)
            plan_spread(f"mlp_out{t}", "w_mlp_in", l + 1)
            plan_pass(f"pass_weights_l{l + 1}", [("w_out", l + 1), ("w_in", l + 1), ("w_mlp_in", l + 1)])
        plan_swap(f"mlp_in_dx{t}", [("w_mlp_out", l), ("w_mlp_in", l)])
        plan_send(f"attn_bwd_{d_first}{t}", "w_mlp_out", l, 0)
        plan_swap(f"attn_bwd_{d_first}{t}", [("w_out", l)])
        plan_send(f"attn_bwd_{d_mid}{t}", "w_mlp_out", l, 1)
        plan_send(f"attn_bwd_{d_last}{t}", "w_mlp_in", l)
        plan_send(f"in_proj_dw{t}", "w_out", l)
        plan_swap(f"in_proj_dx{t}", [("w_in", l)])
        if l > 0:
            plan_send(f"mlp_out_dx_l{l - 1}", "w_in", l)
        else:
            plan_send("adamw_w_mlp_in", "w_in", l, 0)
            plan_send("adamw_w_mlp_out", "w_in", l, 1)

    g_in0, g_cw = _all_gather([shard("w_in", 0), conv_w], "gather_first")
    full_cw = g_cw.transpose(1, 2, 0, 3).reshape(depth, SSD_CONV, conv_ch)
    proj_cols = -(-in_proj // LANES) * LANES
    full = {}

    def weight(n, l):
        if (n, l) not in full:
            if (n, l) == ("w_in", 0):
                g = g_in0
            else:
                key, i = latest[(n, l)]
                g = rides.done[key][i]
            if n == "w_in":
                g = _pad_lanes(g.transpose(1, 0, 2).reshape(d_model, in_proj), proj_cols)
            elif n == "w_mlp_in":
                g = g.transpose(1, 0, 2).reshape(d_model, -1)
            else:
                g = g.reshape(-1, d_model)
            full[(n, l)] = g
        return full[(n, l)]

    layers = [dict(ln1_g=ln1_g[l][None, :], ln2_g=ln2_g[l][None, :], conv_w=full_cw[l], conv_b=conv_b[l][None, :],
                   dt_bias=dt_bias[l], a_log=a_log[l], d_skip=d_skip[l], attn_norm_g=attn_norm_g[l],
                   ssd_norm_g=ssd_norm_g[l]) for l in range(depth)]

    h = x[0]
    saved = []
    for l in range(depth):
        h, s = _layer_fwd(h, layers[l], functools.partial(lambda n, l: weight(n, l), l=l), dims, f"_l{l}", rides)
        saved.append(s)
        if l + 1 < depth:
            _alone(rides, f"pass_weights_l{l + 1}")
    dh, d_final_g, loss_part, dh_b = _loss_head(h, final_norm_g[None, :], loss_target[0], "loss_head")

    grads = [None] * depth
    for l in reversed(range(depth)):
        dh, dh_b, grads[l] = _layer_bwd(dh, dh_b, layers[l], functools.partial(lambda n, l: weight(n, l), l=l),
                                        saved[l], dims, l, ex, copy_dx0=l > 0)
    grad_x = dh[None]

    my_chip = (2 * lax.axis_index("x") + lax.axis_index("y")).astype(jnp.int32).reshape(1)
    gsum, delta, new_m, new_v = {}, {}, {}, {}
    for n in names[1:] + names[:1]:
        per_layer = []
        for l in range(depth):
            key, i = sent[(n, l)]
            per_layer.append(_sum_with_own(rides.done[key][i], ex["pair"][(n, l)], my_chip, f"sum_{n}_l{l}"))
        gsum[n] = jnp.stack(per_layer)
        delta[n], new_m[n], new_v[n] = _adamw(w[n], gsum[n], mom[n], var[n], f"adamw_{n}", rides)

    small_parts = [jnp.stack([grads[l][n] for l in range(depth)]).reshape(-1) for n in _SMALL]
    small_parts += [d_final_g.reshape(-1), loss_part[0, :1]]
    sizes = [int(a.shape[0]) for a in small_parts]
    packed = _to_rows(jnp.concatenate(small_parts))
    (gathered,) = _all_gather([packed], "gather_small_grads")
    total = _sum_leading(gathered, "sum_small_grads").reshape(-1)
    offs = np.cumsum([0] + sizes)
    pieces = [total[offs[i]:offs[i + 1]] for i in range(len(sizes))]
    for n, piece in zip(_SMALL, pieces):
        shape = (depth, SSD_CONV, conv_ch) if n == "conv_w" else w[n].shape
        gsum[n] = piece.reshape(shape)
    gsum["final_norm_g"] = pieces[len(_SMALL)]
    loss = pieces[len(_SMALL) + 1][0]
    my_id = 4 * lax.axis_index("x") + 2 * lax.axis_index("y") + lax.axis_index("c")
    cw = conv_w.shape[2]
    gsum["conv_w"] = lax.dynamic_slice_in_dim(gsum["conv_w"], my_id * cw, cw, axis=2)

    small_names = [n for n in _WEIGHTS if n not in names]
    sm_sizes = [int(np.prod(w[n].shape)) for n in small_names]
    pack = lambda d: _to_rows(jnp.concatenate([d[n].reshape(-1) for n in small_names]))[None]
    outs = _adamw(pack(w), pack(gsum), pack(mom), pack(var), "adamw_small")
    sm_offs = np.cumsum([0] + sm_sizes)
    for res, o in zip((delta, new_m, new_v), outs):
        flat = o.reshape(-1)
        for i, n in enumerate(small_names):
            res[n] = flat[sm_offs[i]:sm_offs[i + 1]].reshape(w[n].shape)

    return (loss, grad_x, *[gsum[n] for n in _WEIGHTS], *[delta[n] for n in _WEIGHTS],
            *[new_m[n] for n in _WEIGHTS], *[new_v[n] for n in _WEIGHTS])
```

```python
import functools
import math

import numpy as np
import jax
import jax.numpy as jnp
from jax import lax
from jax.experimental import pallas as pl
from jax.experimental.pallas import tpu as pltpu

F32 = jnp.float32
BF16 = jnp.bfloat16

N_DEV = 8
LANES = 128
HEAD_DIM = 64
ATTN_BLOCK = 128
BRANCH_DILATIONS = (1, 4, 16)
SSD_GROUPS = 2
SSD_STATE = 128
SSD_CHUNK = 128
SSD_CONV = 4
NORM_EPS = 1e-5
ADAM_LR, ADAM_B1, ADAM_B2, ADAM_EPS, ADAM_WD, ADAM_STEP = 0.001, 0.9, 0.999, 1e-08, 0.01, 10
VMEM_LIMIT_BYTES = 56 * 1024 * 1024
MESH = pl.DeviceIdType.MESH
NEG_INF = float("-inf")


def _params(*sem):
    return pltpu.CompilerParams(dimension_semantics=tuple(sem), vmem_limit_bytes=VMEM_LIMIT_BYTES)


def _pick(n, target, mult):
    best = None
    for t in range(mult, min(n, target) + 1, mult):
        if n % t == 0:
            best = t
    assert best is not None, (n, target, mult)
    return best


def _dot(a, b, ca, cb):
    return lax.dot_general(a, b, (((ca,), (cb,)), ((), ())), preferred_element_type=F32)


def _split3(v):
    hi = v.astype(BF16)
    r = v - hi.astype(F32)
    mid = r.astype(BF16)
    lo = (r - mid.astype(F32)).astype(BF16)
    return hi, mid, lo


def _dot_exact(v, sel, ca, cb):
    hi, mid, lo = _split3(v)
    return _dot(hi, sel, ca, cb) + _dot(mid, sel, ca, cb) + _dot(lo, sel, ca, cb)


_HBM = pl.BlockSpec(memory_space=pltpu.HBM)


def _all_gather(xs, name):
    n = len(xs)

    def body(*refs):
        x_refs, o_refs = refs[:n], refs[n:2 * n]
        send_sems, recv_sems = refs[2 * n:]
        x, y, c = lax.axis_index("x"), lax.axis_index("y"), lax.axis_index("c")
        me, sibling = (x, y, c), (x, y, 1 - c)
        chips = [(1 - x, y), (x, 1 - y), (1 - x, 1 - y)]

        def copy(t, k, block, to, src=None):
            bx, by, bc = block
            dst = o_refs[t].at[4 * bx + 2 * by + bc]
            return pltpu.make_async_remote_copy(
                src_ref=dst if src is None else src, dst_ref=dst,
                send_sem=send_sems.at[t, k], recv_sem=recv_sems.at[t, k],
                device_id=to, device_id_type=MESH)

        first, passed = [], []
        for t in range(n):
            cps = [copy(t, 0, me, sibling, src=x_refs[t])]
            cps += [copy(t, 1 + j, me, (*chip, c), src=x_refs[t]) for j, chip in enumerate(chips)]
            for cp in cps:
                cp.start()
            first += cps
        for t in range(n):
            for j, chip in enumerate(chips):
                copy(t, 1 + j, (*chip, c), me).wait_recv()
                fwd = copy(t, 4 + j, (*chip, c), sibling)
                fwd.start()
                passed.append(fwd)
        for t in range(n):
            copy(t, 0, sibling, me).wait_recv()
            back = copy(t, 7, sibling, sibling)
            back.start()
            passed.append(back)
        for t in range(n):
            copy(t, 7, me, me).wait_recv()
            for j, chip in enumerate(chips):
                copy(t, 4 + j, (*chip, 1 - c), me).wait_recv()
        for cp in first + passed:
            cp.wait_send()

    return pl.pallas_call(
        body, name=name,
        out_shape=[jax.ShapeDtypeStruct((N_DEV,) + a.shape, a.dtype) for a in xs],
        in_specs=[_HBM] * n, out_specs=[_HBM] * n,
        scratch_shapes=[pltpu.SemaphoreType.DMA((n, 8)), pltpu.SemaphoreType.DMA((n, 8))],
    )(*xs)


def _place():
    x, y, c = lax.axis_index("x"), lax.axis_index("y"), lax.axis_index("c")
    return x, y, c, 4 * x + 2 * y + c, (x, y, 1 - c), [(1 - x, y), (x, 1 - y), (1 - x, 1 - y)]


def _remote(src, dst, send_sem, recv_sem, to):
    return pltpu.make_async_remote_copy(src_ref=src, dst_ref=dst, send_sem=send_sem, recv_sem=recv_sem,
                                        device_id=to, device_id_type=MESH)


class _Riding:
    aliases = {}

    def copies(self, ins, outs, sems):
        raise NotImplementedError

    def start(self, ins, outs, sems):
        local, out, _ = self.copies(ins, outs, sems)
        for cp in local + out:
            cp.start()

    def wait(self, ins, outs, sems):
        local, out, landing = self.copies(ins, outs, sems)
        for cp in landing:
            cp.wait_recv()
        for cp in out:
            cp.wait_send()
        for cp in local:
            cp.wait()


def _rows_of(ref, rows):
    return ref if rows is None else ref.at[pl.ds(rows[0], rows[1])]


class _GatherSpread(_Riding):
    def __init__(self, xs, rows=None, into=None):
        n = len(xs)
        self.rows = rows
        self.ins = list(xs) + list(into or [])
        self.out_shapes = [jax.ShapeDtypeStruct((N_DEV,) + a.shape, a.dtype) for a in xs]
        self.aliases = {n + t: t for t in range(n)} if into else {}
        self.sem_shapes = [pltpu.SemaphoreType.DMA((n, 4)), pltpu.SemaphoreType.DMA((n, 4))]

    def copies(self, ins, outs, sems):
        send, recv = sems
        _, _, c, me, sibling, chips = _place()
        targets = [sibling] + [(*chip, c) for chip in chips]
        out, landing = [], []
        for t in range(len(outs)):
            src = _rows_of(ins[t], self.rows)
            for k, to in enumerate(targets):
                out.append(_remote(src, _rows_of(outs[t].at[me], self.rows), send.at[t, k], recv.at[t, k], to))
                theirs = _rows_of(outs[t].at[4 * to[0] + 2 * to[1] + to[2]], self.rows)
                landing.append(_remote(src, theirs, send.at[t, k], recv.at[t, k], to))
        return [], out, landing


class _GatherPass(_Riding):
    def __init__(self, bufs):
        n = len(bufs)
        self.ins = list(bufs)
        self.out_shapes = [jax.ShapeDtypeStruct(b.shape, b.dtype) for b in bufs]
        self.aliases = {t: t for t in range(n)}
        self.sem_shapes = [pltpu.SemaphoreType.DMA((n, 4)), pltpu.SemaphoreType.DMA((n, 4))]

    def copies(self, ins, outs, sems):
        send, recv = sems
        x, y, c, me, sibling, chips = _place()
        out, landing = [], []
        for t in range(len(outs)):
            for j, (px, py) in enumerate(chips + [(x, y)]):
                held = outs[t].at[4 * px + 2 * py + c] if j < 3 else outs[t].at[4 * x + 2 * y + 1 - c]
                lands = outs[t].at[4 * px + 2 * py + 1 - c] if j < 3 else outs[t].at[me]
                out.append(_remote(held, held, send.at[t, j], recv.at[t, j], sibling))
                landing.append(_remote(held, lands, send.at[t, j], recv.at[t, j], sibling))
        return [], out, landing


class _SiblingSwap(_Riding):
    def __init__(self, xs):
        n = len(xs)
        self.ins = list(xs)
        self.out_shapes = [jax.ShapeDtypeStruct((N_DEV // 2,) + a.shape[1:], a.dtype) for a in xs]
        self.sem_shapes = [pltpu.SemaphoreType.DMA((n, 4)), pltpu.SemaphoreType.DMA((n, 4))]

    def copies(self, ins, outs, sems):
        send, recv = sems
        _, _, c, _, sibling, _ = _place()
        out = [_remote(ins[t].at[2 * q + 1 - c], outs[t].at[q], send.at[t, q], recv.at[t, q], sibling)
               for t in range(len(ins)) for q in range(N_DEV // 2)]
        return [], out, out


class _ChipSend(_Riding):
    def __init__(self, ps, rows=None, into=None):
        n = len(ps)
        self.rows = rows
        self.ins = list(ps) + list(into or [])
        self.out_shapes = [jax.ShapeDtypeStruct((3,) + a.shape[1:], a.dtype) for a in ps]
        self.aliases = {n + t: t for t in range(n)} if into else {}
        self.sem_shapes = [pltpu.SemaphoreType.DMA((n, 3)), pltpu.SemaphoreType.DMA((n, 3))]

    def copies(self, ins, outs, sems):
        send, recv = sems
        _, _, c, _, _, chips = _place()
        out = [_remote(_rows_of(ins[t].at[2 * px + py], self.rows), _rows_of(outs[t].at[j], self.rows),
                       send.at[t, j], recv.at[t, j], (px, py, c))
               for t in range(len(outs)) for j, (px, py) in enumerate(chips)]
        return [], out, out


class _Bundle(_Riding):
    def __init__(self, comms):
        self.comms = comms
        self.ins = [a for cm in comms for a in cm.ins]
        self.out_shapes = [s for cm in comms for s in cm.out_shapes]
        self.sem_shapes = [s for cm in comms for s in cm.sem_shapes]
        self.aliases = {}
        i0 = o0 = 0
        for cm in comms:
            self.aliases.update({i0 + i: o0 + j for i, j in cm.aliases.items()})
            i0, o0 = i0 + len(cm.ins), o0 + len(cm.out_shapes)

    def copies(self, ins, outs, sems):
        local, out, landing = [], [], []
        i0 = o0 = s0 = 0
        for cm in self.comms:
            i1, o1, s1 = i0 + len(cm.ins), o0 + len(cm.out_shapes), s0 + len(cm.sem_shapes)
            a, b, c = cm.copies(ins[i0:i1], outs[o0:o1], sems[s0:s1])
            local, out, landing = local + a, out + b, landing + c
            i0, o0, s0 = i1, o1, s1
        return local, out, landing


class _Rides:
    def __init__(self):
        self.plan, self.done, self.aboard = {}, {}, {}

    def put(self, host, key, make):
        self.plan.setdefault(host, []).append((key, make))

    def board(self, host):
        if host not in self.plan:
            return None
        self.aboard[host] = [make() for _, make in self.plan[host]]
        return _Bundle(self.aboard[host])

    def land(self, host, results):
        o0 = 0
        for (key, _), cm in zip(self.plan[host], self.aboard[host]):
            self.done[key] = list(results[o0:o0 + len(cm.out_shapes)])
            o0 += len(cm.out_shapes)


def _pallas(body, *, name, grid, out_shape, in_specs, out_specs, operands, semantics, scratch_shapes=(), rides=None):
    comm = rides.board(name) if rides is not None else None
    if comm is None:
        return pl.pallas_call(
            body, name=name, grid=grid, out_shape=list(out_shape), in_specs=list(in_specs),
            out_specs=list(out_specs), scratch_shapes=list(scratch_shapes), compiler_params=_params(*semantics),
        )(*operands)
    n_in, n_out, n_scr = len(in_specs), len(out_shape), len(scratch_shapes)
    n_ci, n_co = len(comm.ins), len(comm.out_shapes)

    def hosted(*refs):
        cuts = np.cumsum([0, n_in, n_ci, n_out, n_co, n_scr])
        ins, c_ins, outs, c_outs, scr = (refs[cuts[i]:cuts[i + 1]] for i in range(5))
        sems = refs[cuts[5]:]
        ids = [pl.program_id(a) for a in range(len(grid))]
        first = functools.reduce(jnp.logical_and, [i == 0 for i in ids])
        last = functools.reduce(jnp.logical_and, [i == g - 1 for i, g in zip(ids, grid)])

        @pl.when(first)
        def _():
            comm.start(c_ins, c_outs, sems)

        body(*ins, *outs, *scr)

        @pl.when(last)
        def _():
            comm.wait(c_ins, c_outs, sems)

    results = pl.pallas_call(
        hosted, name=name, grid=grid, out_shape=list(out_shape) + comm.out_shapes,
        in_specs=list(in_specs) + [_HBM] * n_ci, out_specs=list(out_specs) + [_HBM] * n_co,
        scratch_shapes=list(scratch_shapes) + comm.sem_shapes,
        input_output_aliases={n_in + i: n_out + j for i, j in comm.aliases.items()},
        compiler_params=_params(*["arbitrary"] * len(grid)),
    )(*operands, *comm.ins)
    rides.land(name, results[n_out:])
    return results[:n_out]


def _alone(rides, name):
    comm = rides.board(name)

    def body(*refs):
        n_ci, n_co = len(comm.ins), len(comm.out_shapes)
        ins, outs, sems = refs[:n_ci], refs[n_ci:n_ci + n_co], refs[n_ci + n_co:]
        comm.start(ins, outs, sems)
        comm.wait(ins, outs, sems)

    results = pl.pallas_call(
        body, name=name, out_shape=comm.out_shapes, in_specs=[_HBM] * len(comm.ins),
        out_specs=[_HBM] * len(comm.out_shapes), scratch_shapes=comm.sem_shapes,
        input_output_aliases=dict(comm.aliases),
    )(*comm.ins)
    rides.land(name, results)


def _cast_layers(ws, picks, name, rides=None, steps=8):
    in_specs, out_specs, out_shape = [], [], []
    for t, l in picks:
        _, rows, cols = ws[t].shape
        tr = rows // steps
        assert rows % steps == 0 and tr % 16 == 0, (rows, steps)
        in_specs.append(pl.BlockSpec((None, tr, cols), lambda i, l=l: (l, i, 0)))
        out_specs.append(pl.BlockSpec((tr, cols), lambda i: (i, 0)))
        out_shape.append(jax.ShapeDtypeStruct((rows, cols), BF16))
    n = len(picks)

    def body(*refs):
        for src, dst in zip(refs[:n], refs[n:]):
            dst[...] = src[...].astype(BF16)

    return _pallas(body, name=name, grid=(steps,), out_shape=out_shape, in_specs=in_specs, out_specs=out_specs,
                   operands=[ws[t] for t, _ in picks], semantics=("parallel",), rides=rides)


def _row_tile(rows, cols, itemsize, copies, budget=24 * 1024 * 1024):
    padded = -(-cols // LANES) * LANES
    mult = 8 * (4 // itemsize)
    if rows % mult:
        return rows
    return _pick(rows, max(mult, budget // (copies * padded * itemsize)), mult)


def _sum_leading(x, name):
    n_src, rows, cols = x.shape
    tr = _row_tile(rows, cols, 4, 2 * (n_src + 2))

    def body(x_ref, o_ref):
        acc = x_ref[0].astype(F32)
        for s in range(1, n_src):
            acc = acc + x_ref[s].astype(F32)
        o_ref[...] = acc

    return pl.pallas_call(
        body, name=name, grid=(rows // tr,), out_shape=jax.ShapeDtypeStruct((rows, cols), F32),
        in_specs=[pl.BlockSpec((n_src, tr, cols), lambda i: (0, i, 0))],
        out_specs=pl.BlockSpec((tr, cols), lambda i: (i, 0)), compiler_params=_params("parallel"),
    )(x)


def _pair_sum(buf, theirs, core, name):
    n_q, rows, cols = theirs.shape
    tr = _row_tile(rows, cols, 4, 6)

    def body(core_ref, mine_ref, theirs_ref, o_ref):
        o_ref[...] = (mine_ref[...].astype(F32) + theirs_ref[...].astype(F32)).astype(BF16)

    spec = pl.BlockSpec((None, tr, cols), lambda q, i, core_ref: (q, i, 0))
    return pl.pallas_call(
        body, name=name, out_shape=jax.ShapeDtypeStruct(theirs.shape, BF16),
        grid_spec=pltpu.PrefetchScalarGridSpec(
            num_scalar_prefetch=1, grid=(n_q, rows // tr),
            in_specs=[pl.BlockSpec((None, tr, cols), lambda q, i, core_ref: (2 * q + core_ref[0], i, 0)), spec],
            out_specs=spec),
        compiler_params=_params("parallel", "parallel"),
    )(core, buf, theirs)


def _sum_with_own(recv, pair, chip, name):
    n_src, rows, cols = recv.shape
    tr = _row_tile(rows, cols, 4, 2 * (n_src + 3))

    def body(chip_ref, own_ref, recv_ref, o_ref):
        acc = own_ref[...].astype(F32)
        for s in range(n_src):
            acc = acc + recv_ref[s].astype(F32)
        o_ref[...] = acc

    return pl.pallas_call(
        body, name=name, out_shape=jax.ShapeDtypeStruct((rows, cols), F32),
        grid_spec=pltpu.PrefetchScalarGridSpec(
            num_scalar_prefetch=1, grid=(rows // tr,),
            in_specs=[pl.BlockSpec((None, tr, cols), lambda i, chip_ref: (chip_ref[0], i, 0)),
                      pl.BlockSpec((n_src, tr, cols), lambda i, chip_ref: (0, i, 0))],
            out_specs=pl.BlockSpec((tr, cols), lambda i, chip_ref: (i, 0))),
        compiler_params=_params("parallel"),
    )(chip, pair, recv)


def _mm(a, b, *, name, ta=False, tb=False, tm=1024, tn=512, out_dtype=F32, a_act=None,
        residual=None, gate=None, out_chunk=None, rides=None):
    k_dim, m = (a.shape if ta else a.shape[::-1])
    n, kb = (b.shape if tb else b.shape[::-1])
    assert kb == k_dim, (a.shape, b.shape, ta, tb)
    tm, tn = _pick(m, tm, 128), _pick(out_chunk or n, tn, 128)
    ca, cb = (0 if ta else 1), (1 if tb else 0)
    a_spec = pl.BlockSpec((k_dim, tm), lambda i, j: (0, i)) if ta else pl.BlockSpec((tm, k_dim), lambda i, j: (i, 0))
    b_spec = pl.BlockSpec((tn, k_dim), lambda i, j: (j, 0)) if tb else pl.BlockSpec((k_dim, tn), lambda i, j: (0, j))
    mn_spec = pl.BlockSpec((tm, tn), lambda i, j: (i, j))
    if out_chunk:
        per = out_chunk // tn
        o_spec = pl.BlockSpec((None, tm, tn), lambda i, j: (j // per, i, j % per))
        out_shape = jax.ShapeDtypeStruct((n // out_chunk, m, out_chunk), out_dtype)
    else:
        o_spec = mn_spec
        out_shape = jax.ShapeDtypeStruct((m, n), out_dtype)
    operands, in_specs = [a, b], [a_spec, b_spec]
    for extra in (gate, residual):
        if extra is not None:
            operands.append(extra)
            in_specs.append(mn_spec)

    def body(*refs):
        a_ref, b_ref, o_ref = refs[0], refs[1], refs[-1]
        extras = list(refs[2:-1])
        gate_ref = extras.pop(0) if gate is not None else None
        res_ref = extras.pop(0) if residual is not None else None
        av = a_ref[...].astype(BF16)
        if a_act == "relu2":
            av = jnp.square(jnp.maximum(av, jnp.zeros_like(av)))
        r = _dot(av, b_ref[...].astype(BF16), ca, cb)
        if gate_ref is not None:
            r = r * (2.0 * jnp.maximum(gate_ref[...].astype(F32), 0.0))
        if res_ref is not None:
            r = r + res_ref[...].astype(F32)
        o_ref[...] = r.astype(out_dtype)

    return _pallas(body, name=name, grid=(m // tm, n // tn), out_shape=[out_shape], in_specs=in_specs,
                   out_specs=[o_spec], operands=operands, semantics=("parallel", "arbitrary"), rides=rides)[0]


def _rmsnorm_fwd(xs, seg_widths, g, name, tm=256):
    t_len = xs[0].shape[0]
    width = sum(x.shape[1] for x in xs)
    tm = _pick(t_len, tm, 16)
    n = len(xs)

    def body(*refs):
        x_refs, g_ref, o_ref = refs[:n], refs[n], refs[n + 1]
        col = 0
        for x_ref, widths in zip(x_refs, seg_widths):
            off = 0
            for w in widths:
                xv = x_ref[:, off:off + w].astype(F32)
                r = lax.rsqrt(jnp.mean(xv * xv, axis=1, keepdims=True) + NORM_EPS)
                o_ref[:, col:col + w] = (xv * r * g_ref[:, col:col + w]).astype(BF16)
                off += w
                col += w

    return pl.pallas_call(
        body, name=name, grid=(t_len // tm,),
        out_shape=jax.ShapeDtypeStruct((t_len, width), BF16),
        in_specs=[pl.BlockSpec((tm, x.shape[1]), lambda i: (i, 0)) for x in xs]
        + [pl.BlockSpec((1, width), lambda i: (0, 0))],
        out_specs=pl.BlockSpec((tm, width), lambda i: (i, 0)),
        compiler_params=_params("parallel"),
    )(*xs, g)


def _rmsnorm_bwd(xs, seg_widths, g, dh, residuals, name, tm=256, bf16_copy=False):
    t_len = xs[0].shape[0]
    width = sum(x.shape[1] for x in xs)
    tm = _pick(t_len, tm, 8)
    n = len(xs)
    has_res = [r is not None for r in residuals]
    res_ops = [r for r in residuals if r is not None]

    def body(*refs):
        x_refs, g_ref, dh_ref = refs[:n], refs[n], refs[n + 1]
        res_refs = list(refs[n + 2:n + 2 + len(res_ops)])
        dx_refs = refs[n + 2 + len(res_ops):n + 2 + len(res_ops) + n]
        dg_ref = refs[n + 2 + len(res_ops) + n]
        copy_refs = refs[n + 3 + len(res_ops) + n:]
        first = pl.program_id(0) == 0
        col = 0
        for idx, (x_ref, widths) in enumerate(zip(x_refs, seg_widths)):
            res_ref = res_refs.pop(0) if has_res[idx] else None
            off = 0
            for w in widths:
                xv = x_ref[:, off:off + w].astype(F32)
                r = lax.rsqrt(jnp.mean(xv * xv, axis=1, keepdims=True) + NORM_EPS)
                xh = xv * r
                dhv = dh_ref[:, col:col + w].astype(F32)
                gd = dhv * g_ref[:, col:col + w]
                dx = r * (gd - xh * jnp.mean(gd * xh, axis=1, keepdims=True))
                if res_ref is not None:
                    dx = dx + res_ref[:, off:off + w]
                dx_refs[idx][:, off:off + w] = dx
                if bf16_copy:
                    copy_refs[idx][:, off:off + w] = dx.astype(BF16)
                part = jnp.sum(dhv * xh, axis=0, keepdims=True)

                @pl.when(first)
                def _(part=part, col=col, w=w):
                    dg_ref[:, col:col + w] = part

                @pl.when(jnp.logical_not(first))
                def _(part=part, col=col, w=w):
                    dg_ref[:, col:col + w] += part
                off += w
                col += w

    outs = pl.pallas_call(
        body, name=name, grid=(t_len // tm,),
        out_shape=[jax.ShapeDtypeStruct(x.shape, F32) for x in xs] + [jax.ShapeDtypeStruct((1, width), F32)]
        + ([jax.ShapeDtypeStruct(x.shape, BF16) for x in xs] if bf16_copy else []),
        in_specs=[pl.BlockSpec((tm, x.shape[1]), lambda i: (i, 0)) for x in xs]
        + [pl.BlockSpec((1, width), lambda i: (0, 0)), pl.BlockSpec((tm, width), lambda i: (i, 0))]
        + [pl.BlockSpec((tm, r.shape[1]), lambda i: (i, 0)) for r in res_ops],
        out_specs=[pl.BlockSpec((tm, x.shape[1]), lambda i: (i, 0)) for x in xs]
        + [pl.BlockSpec((1, width), lambda i: (0, 0))]
        + ([pl.BlockSpec((tm, x.shape[1]), lambda i: (i, 0)) for x in xs] if bf16_copy else []),
        compiler_params=_params("arbitrary"),
    )(*xs, g, dh, *res_ops)
    return outs[:n], outs[n], outs[n + 1:]


def _loss_head(x, g, target, name, tm=256):
    t_len, d = x.shape
    tm = _pick(t_len, tm, 8)

    def body(x_ref, g_ref, t_ref, dx_ref, dg_ref, loss_ref, dxb_ref):
        first = pl.program_id(0) == 0
        xv = x_ref[...]
        r = lax.rsqrt(jnp.mean(xv * xv, axis=1, keepdims=True) + NORM_EPS)
        xh = xv * r
        gv = g_ref[...]
        err = xh * gv - t_ref[...]
        part_loss = 0.5 * jnp.sum(jnp.mean(err * err, axis=1, keepdims=True), axis=0, keepdims=True)
        dy = err * (1.0 / d)
        gd = dy * gv
        dx = r * (gd - xh * jnp.mean(gd * xh, axis=1, keepdims=True))
        dx_ref[...] = dx
        dxb_ref[...] = dx.astype(BF16)
        part_g = jnp.sum(dy * xh, axis=0, keepdims=True)
        part_loss = jnp.broadcast_to(part_loss, (1, LANES))

        @pl.when(first)
        def _():
            dg_ref[...] = part_g
            loss_ref[...] = part_loss

        @pl.when(jnp.logical_not(first))
        def _():
            dg_ref[...] += part_g
            loss_ref[...] += part_loss

    return pl.pallas_call(
        body, name=name, grid=(t_len // tm,),
        out_shape=[jax.ShapeDtypeStruct((t_len, d), F32), jax.ShapeDtypeStruct((1, d), F32),
                   jax.ShapeDtypeStruct((1, LANES), F32), jax.ShapeDtypeStruct((t_len, d), BF16)],
        in_specs=[pl.BlockSpec((tm, d), lambda i: (i, 0)), pl.BlockSpec((1, d), lambda i: (0, 0)),
                  pl.BlockSpec((tm, d), lambda i: (i, 0))],
        out_specs=[pl.BlockSpec((tm, d), lambda i: (i, 0)), pl.BlockSpec((1, d), lambda i: (0, 0)),
                   pl.BlockSpec((1, LANES), lambda i: (0, 0)), pl.BlockSpec((tm, d), lambda i: (i, 0))],
        compiler_params=_params("arbitrary"),
    )(x, g, target)


def _alibi_slope(h, n_heads):
    return jnp.exp(jnp.full((1, 1), -8.0 * math.log(2.0) / n_heads, F32) * (h + 1).astype(F32))


def _attn_tiles(d, w):
    return ATTN_BLOCK * d, (w if d == 1 else LANES)


def _residue_rows(r, d):
    return pl.ds(r, ATTN_BLOCK, stride=d) if d > 1 else pl.ds(0, ATTN_BLOCK)


def _attn_masks(first_block):
    i = lax.broadcasted_iota(jnp.int32, (2 * ATTN_BLOCK, 2 * ATTN_BLOCK), 0) % ATTN_BLOCK
    j = lax.broadcasted_iota(jnp.int32, (2 * ATTN_BLOCK, 2 * ATTN_BLOCK), 1)
    delta = i - j + ATTN_BLOCK
    valid = jnp.logical_and(delta >= 0, delta <= ATTN_BLOCK)
    valid = jnp.logical_and(valid, jnp.logical_or(j >= ATTN_BLOCK, jnp.logical_not(first_block)))
    return valid, delta.astype(F32)


def _stack_heads(x, masks):
    zero = jnp.zeros_like(x)
    return jnp.concatenate([jnp.where(masks[0], x, zero), jnp.where(masks[1], x, zero)], axis=0)


def _unstack_heads(x2, masks):
    return jnp.where(masks[0], x2[:ATTN_BLOCK], x2[ATTN_BLOCK:])


def _pair_slopes(first_head, p, n_heads, d):
    row = lax.broadcasted_iota(jnp.int32, (2 * ATTN_BLOCK, 1), 0)
    sa, sb = (_alibi_slope(first_head + 2 * p + hh, n_heads) * d for hh in range(2))
    return jnp.where(row < ATTN_BLOCK, sa, sb)


def _head_lane_masks():
    lane = lax.broadcasted_iota(jnp.int32, (ATTN_BLOCK, LANES), 1)
    return [lane < HEAD_DIM, lane >= HEAD_DIM]


def _attn_branch_fwd(proj, w, dilation, n_heads, name, rides=None):
    t_len = proj.shape[0]
    d = dilation
    rows, lw = _attn_tiles(d, w)
    nb = t_len // rows
    n_pairs = lw // LANES
    per = w // lw
    scale = HEAD_DIM ** -0.5

    def body(q_ref, kp_ref, kc_ref, vp_ref, vc_ref, o_ref, lse_ref):
        first_head = pl.program_id(0) * (2 * n_pairs)
        first_block = pl.program_id(1) == 0
        valid, delta = _attn_masks(first_block)
        masks = _head_lane_masks()
        ones = jnp.ones((2 * ATTN_BLOCK, LANES), BF16)
        for p in range(n_pairs):
            cols = pl.ds(p * LANES, LANES)
            bias = _pair_slopes(first_head, p, n_heads, d) * delta
            for r in range(d):
                rs = _residue_rows(r, d)
                q2 = _stack_heads((q_ref[rs, cols] * scale).astype(BF16), masks)
                k2 = jnp.concatenate([kp_ref[rs, cols], kc_ref[rs, cols]], axis=0).astype(BF16)
                v2 = jnp.concatenate([vp_ref[rs, cols], vc_ref[rs, cols]], axis=0).astype(BF16)
                s = jnp.where(valid, _dot(q2, k2, 1, 1) - bias, NEG_INF)
                m = jnp.max(s, axis=1, keepdims=True)
                pr = jnp.exp(s - m).astype(BF16)
                den = _dot(pr, ones, 1, 0)
                o_ref[rs, cols] = _unstack_heads(_dot(pr, v2, 1, 0) / den, masks)
                lse_ref[rs, cols] = _unstack_heads(m + jnp.log(den), masks)

    def spec(which, prev):
        if prev:
            return pl.BlockSpec((rows, lw), lambda b, n: (jnp.maximum(n - 1, 0), which * per + b))
        return pl.BlockSpec((rows, lw), lambda b, n: (n, which * per + b))

    o_spec = pl.BlockSpec((rows, lw), lambda b, n: (n, b))
    return _pallas(
        body, name=name, grid=(per, nb), out_shape=[jax.ShapeDtypeStruct((t_len, w), F32)] * 2,
        in_specs=[spec(0, False), spec(1, True), spec(1, False), spec(2, True), spec(2, False)],
        out_specs=[o_spec, o_spec], operands=[proj] * 5, semantics=("parallel", "parallel"), rides=rides)


def _attn_combine(outs, lses, name, tm=512):
    t_len, w = outs[0].shape
    tm = _pick(t_len, tm, 8)
    nbr = len(outs)

    def body(*refs):
        o_refs, l_refs = refs[:nbr], refs[nbr:2 * nbr]
        out_ref, lse_ref = refs[2 * nbr:]
        ls = [r[...] for r in l_refs]
        m = functools.reduce(jnp.maximum, ls)
        es = [jnp.exp(l - m) for l in ls]
        den = functools.reduce(lambda a, b: a + b, es)
        num = functools.reduce(lambda a, b: a + b, [e * r[...] for e, r in zip(es, o_refs)])
        out_ref[...] = num / den
        lse_ref[...] = m + jnp.log(den)

    spec = pl.BlockSpec((tm, w), lambda i: (i, 0))
    return pl.pallas_call(
        body, name=name, grid=(t_len // tm,),
        out_shape=[jax.ShapeDtypeStruct((t_len, w), F32)] * 2,
        in_specs=[spec] * (2 * nbr), out_specs=[spec, spec],
        compiler_params=_params("parallel"),
    )(*outs, *lses)


def _attn_branch_bwd(proj, w, out, lse, dout, dilation, n_heads, name, acc=None, rides=None):
    t_len = proj.shape[0]
    d = dilation
    rows, lw = _attn_tiles(d, w)
    nb = t_len // rows
    n_pairs = lw // LANES
    per = w // lw
    scale = HEAD_DIM ** -0.5
    n_acc = 0 if acc is None else 3

    def body(*refs):
        q_ref, kp_ref, kc_ref, vp_ref, vc_ref, out_ref, lse_ref, do_ref = refs[:8]
        acc_refs = refs[8:8 + n_acc]
        dq_ref, dk_ref, dv_ref, dk_carry, dv_carry = refs[8 + n_acc:]
        first_head = pl.program_id(0) * (2 * n_pairs)
        n = pl.program_id(1)
        first_block = n == 0
        valid, dist = _attn_masks(first_block)
        masks = _head_lane_masks()

        def plus(value, idx, *where):
            return value + acc_refs[idx][where] if n_acc else value

        @pl.when(first_block)
        def _():
            dk_carry[...] = jnp.zeros_like(dk_carry)
            dv_carry[...] = jnp.zeros_like(dv_carry)

        @pl.when(n < nb)
        def _():
            for p in range(n_pairs):
                cols = pl.ds(p * LANES, LANES)
                bias = _pair_slopes(first_head, p, n_heads, d) * dist
                for r in range(d):
                    rs = _residue_rows(r, d)
                    q2 = _stack_heads((q_ref[rs, cols] * scale).astype(BF16), masks)
                    k2 = jnp.concatenate([kp_ref[rs, cols], kc_ref[rs, cols]], axis=0).astype(BF16)
                    v2 = jnp.concatenate([vp_ref[rs, cols], vc_ref[rs, cols]], axis=0).astype(BF16)
                    do = do_ref[rs, cols]
                    do2 = _stack_heads(do.astype(BF16), masks)
                    do_out = do * out_ref[rs, cols]
                    lse_all = lse_ref[rs, cols]
                    delta = jnp.concatenate([jnp.sum(jnp.where(masks[hh], do_out, 0.0), axis=1, keepdims=True)
                                             for hh in range(2)], axis=0)
                    lse2 = jnp.concatenate([jnp.max(jnp.where(masks[hh], lse_all, NEG_INF), axis=1, keepdims=True)
                                            for hh in range(2)], axis=0)
                    s = jnp.where(valid, _dot(q2, k2, 1, 1) - bias, NEG_INF)
                    pr = jnp.exp(s - lse2)
                    ds = (pr * (_dot(do2, v2, 1, 1) - delta)).astype(BF16)
                    dq = _unstack_heads(_dot(ds, k2, 1, 0), masks)
                    dk2 = _dot(ds, q2, 0, 0)
                    dv2 = _dot(pr.astype(BF16), do2, 0, 0)
                    dq_ref[rs, cols] = plus(dq * scale, 0, rs, cols)
                    dk_ref[rs, cols] = plus(dk_carry[r, :, cols] + dk2[:ATTN_BLOCK], 1, rs, cols)
                    dv_ref[rs, cols] = plus(dv_carry[r, :, cols] + dv2[:ATTN_BLOCK], 2, rs, cols)
                    dk_carry[r, :, cols] = dk2[ATTN_BLOCK:]
                    dv_carry[r, :, cols] = dv2[ATTN_BLOCK:]

        @pl.when(n == nb)
        def _():
            for r in range(d):
                rs = _residue_rows(r, d)
                dk_ref[rs, :] = plus(dk_carry[r], 1, rs, slice(None))
                dv_ref[rs, :] = plus(dv_carry[r], 2, rs, slice(None))

    def qkv_spec(which, shift):
        return pl.BlockSpec((rows, lw), lambda b, n: (jnp.clip(n - shift, 0, nb - 1), which * per + b))

    q_like = pl.BlockSpec((rows, lw), lambda b, n: (jnp.minimum(n, nb - 1), b))
    k_like = pl.BlockSpec((rows, lw), lambda b, n: (jnp.maximum(n - 1, 0), b))
    return _pallas(
        body, name=name, grid=(per, nb + 1), out_shape=[jax.ShapeDtypeStruct((t_len, w), F32)] * 3,
        in_specs=[qkv_spec(0, 0), qkv_spec(1, 1), qkv_spec(1, 0), qkv_spec(2, 1), qkv_spec(2, 0),
                  q_like, q_like, q_like] + [q_like, k_like, k_like][:n_acc],
        out_specs=[q_like, k_like, k_like], operands=[proj] * 5 + [out, lse, dout, *(acc or ())],
        scratch_shapes=[pltpu.VMEM((d, ATTN_BLOCK, lw), F32), pltpu.VMEM((d, ATTN_BLOCK, lw), F32)],
        semantics=("parallel", "arbitrary"), rides=rides)


def _shift_down(u, s):
    if s == 0:
        return u
    row = lax.broadcasted_iota(jnp.int32, u.shape, 0)
    return jnp.where(row >= s, pltpu.roll(u, s, 0), 0.0)


def _shift_up(u, s):
    if s == 0:
        return u
    n = u.shape[0]
    row = lax.broadcasted_iota(jnp.int32, u.shape, 0)
    return jnp.where(row < n - s, pltpu.roll(u, n - s, 0), 0.0)


def _conv_fwd(u, col0, w, b, name):
    t_len, ch = u.shape[0], w.shape[1]
    blk0 = col0 // LANES

    def body(u_ref, w_ref, b_ref, o_ref):
        uv = u_ref[...]
        pre = b_ref[...] + jnp.zeros_like(uv)
        for k in range(SSD_CONV):
            pre = pre + w_ref[k:k + 1, :] * _shift_down(uv, SSD_CONV - 1 - k)
        o_ref[...] = pre * jax.nn.sigmoid(pre)

    return pl.pallas_call(
        body, name=name, grid=(ch // LANES,),
        out_shape=jax.ShapeDtypeStruct((t_len, ch), F32),
        in_specs=[pl.BlockSpec((t_len, LANES), lambda j: (0, blk0 + j)),
                  pl.BlockSpec((SSD_CONV, LANES), lambda j: (0, j)), pl.BlockSpec((1, LANES), lambda j: (0, j))],
        out_specs=pl.BlockSpec((t_len, LANES), lambda j: (0, j)),
        compiler_params=_params("parallel"),
    )(u, w, b)


def _conv_bwd(u, col0, w, b, dact, name):
    t_len, ch = u.shape[0], w.shape[1]
    blk0 = col0 // LANES

    def body(u_ref, w_ref, b_ref, da_ref, du_ref, dw_ref, db_ref):
        uv = u_ref[...]
        shifted = [_shift_down(uv, SSD_CONV - 1 - k) for k in range(SSD_CONV)]
        pre = b_ref[...] + jnp.zeros_like(uv)
        for k in range(SSD_CONV):
            pre = pre + w_ref[k:k + 1, :] * shifted[k]
        sig = jax.nn.sigmoid(pre)
        dpre = da_ref[...] * (sig * (1.0 + pre * (1.0 - sig)))
        du = jnp.zeros_like(uv)
        for k in range(SSD_CONV):
            du = du + w_ref[k:k + 1, :] * _shift_up(dpre, SSD_CONV - 1 - k)
            dw_ref[k:k + 1, :] = jnp.sum(dpre * shifted[k], axis=0, keepdims=True)
        du_ref[...] = du
        db_ref[...] = jnp.sum(dpre, axis=0, keepdims=True)

    col = pl.BlockSpec((t_len, LANES), lambda j: (0, j))
    w_spec = pl.BlockSpec((SSD_CONV, LANES), lambda j: (0, j))
    b_spec = pl.BlockSpec((1, LANES), lambda j: (0, j))
    return pl.pallas_call(
        body, name=name, grid=(ch // LANES,),
        out_shape=[jax.ShapeDtypeStruct((t_len, ch), F32), jax.ShapeDtypeStruct((SSD_CONV, ch), F32),
                   jax.ShapeDtypeStruct((1, ch), F32)],
        in_specs=[pl.BlockSpec((t_len, LANES), lambda j: (0, blk0 + j)), w_spec, b_spec, col],
        out_specs=[col, w_spec, b_spec],
        compiler_params=_params("parallel"),
    )(u, w, b, dact)


def _cumsum_rows(v):
    n = v.shape[0]
    row = lax.broadcasted_iota(jnp.int32, v.shape, 0)
    s = 1
    while s < n:
        v = v + jnp.where(row >= s, pltpu.roll(v, s, 0), 0.0)
        s *= 2
    return v


def _rev_cumsum_rows(v):
    n = v.shape[0]
    row = lax.broadcasted_iota(jnp.int32, v.shape, 0)
    s = 1
    while s < n:
        v = v + jnp.where(row < n - s, pltpu.roll(v, n - s, 0), 0.0)
        s *= 2
    return v


def _head_selector(heads, width):
    j = lax.broadcasted_iota(jnp.int32, (LANES, width), 0)
    lane = lax.broadcasted_iota(jnp.int32, (LANES, width), 1)
    return jnp.where(jnp.logical_and(lane // HEAD_DIM == j, j < heads), 1.0, 0.0).astype(BF16)


class _SsdChunk:
    def __init__(self, dtraw_ref, bias_ref, alog_ref, xs_ref, b_ref, c_ref, heads):
        q = SSD_CHUNK
        width = heads * HEAD_DIM
        lane = lax.broadcasted_iota(jnp.int32, (q, LANES), 1)
        self.head_lanes = lane < heads
        lane1 = lax.broadcasted_iota(jnp.int32, (1, LANES), 1)
        self.a = jnp.where(lane1 < heads, -jnp.exp(alog_ref[...]), 0.0)
        self.dt_arg = dtraw_ref[...] + bias_ref[...]
        self.dt = jnp.where(self.head_lanes, jax.nn.softplus(self.dt_arg), 0.0)
        self.cum = _cumsum_rows(self.dt * self.a)
        self.cum_t = self.cum.T
        last = self.cum[q - 1:q, :]
        self.sel = _head_selector(heads, width)
        self.expand = lambda v: _dot_exact(v, self.sel, 1, 0)
        self.segsum = lambda v: _dot_exact(v, self.sel, 1, 1)
        self.e_exp = self.expand(jnp.exp(self.cum))
        self.d_exp = self.expand(jnp.exp(last - self.cum))
        self.elast_exp = self.e_exp[q - 1:q, :]
        self.dt_exp = self.expand(self.dt)
        self.xs = xs_ref[...]
        self.x = self.xs * self.dt_exp
        self.xb = self.x.astype(BF16)
        self.bb = b_ref[...].astype(BF16)
        self.cb = c_ref[...].astype(BF16)
        self.cbm = _dot(self.cb, self.bb, 1, 1)
        li = lax.broadcasted_iota(jnp.int32, (q, q), 0)
        si = lax.broadcasted_iota(jnp.int32, (q, q), 1)
        self.tri = li >= si
        hl = lax.broadcasted_iota(jnp.int32, (q, LANES), 1)
        self.pair_masks = [hl < HEAD_DIM, hl >= HEAD_DIM]

    def decay(self, j):
        diff = self.cum[:, j:j + 1] - self.cum_t[j:j + 1, :]
        return jnp.exp(jnp.where(self.tri, diff, NEG_INF))


def _ssd_specs(t_len, heads, n_chunks, xbc_cols, rev):
    q, gw = SSD_CHUNK, heads * HEAD_DIM
    ssd_w = SSD_GROUPS * gw
    b_blk = ssd_w // SSD_STATE
    ch = (lambda c: n_chunks - 1 - c) if rev else (lambda c: c)
    return dict(
        dtraw=pl.BlockSpec((None, q, LANES), lambda g, c: (g, ch(c), 0)),
        small=pl.BlockSpec((None, 1, LANES), lambda g, c: (g, 0, 0)),
        dsk=pl.BlockSpec((None, 1, gw), lambda g, c: (g, 0, 0)),
        xs=pl.BlockSpec((q, gw), lambda g, c: (ch(c), g)),
        b=pl.BlockSpec((q, SSD_STATE), lambda g, c: (ch(c), b_blk + g)),
        c=pl.BlockSpec((q, SSD_STATE), lambda g, c: (ch(c), b_blk + SSD_GROUPS + g)),
        z=pl.BlockSpec((q, gw), lambda g, c: (ch(c), 3 * SSD_GROUPS + g)),
        tok=pl.BlockSpec((q, gw), lambda g, c: (ch(c), g)),
        state=pl.BlockSpec((None, SSD_STATE, gw), lambda g, c: (ch(c), 0, g)),
        bc=pl.BlockSpec((q, SSD_STATE), lambda g, c: (ch(c), g)),
    )


def _ssd_fwd(xbc, qkvz, dtraw_g, bias_g, alog_g, dsk_exp, heads, name):
    t_len = xbc.shape[0]
    q, gw = SSD_CHUNK, heads * HEAD_DIM
    n_chunks = t_len // q
    ssd_w = SSD_GROUPS * gw
    sp = _ssd_specs(t_len, heads, n_chunks, xbc.shape[1], rev=False)

    def body(dtraw_ref, bias_ref, alog_ref, dsk_ref, xs_ref, b_ref, c_ref, z_ref,
             yg_ref, ypre_ref, st_ref, s_scr):
        @pl.when(pl.program_id(1) == 0)
        def _():
            s_scr[...] = jnp.zeros_like(s_scr)

        k = _SsdChunk(dtraw_ref, bias_ref, alog_ref, xs_ref, b_ref, c_ref, heads)
        s_prev = s_scr[...]
        st_ref[...] = s_prev
        y_off = k.e_exp * _dot(k.cb, s_prev.astype(BF16), 1, 0)
        parts = []
        for p in range(heads // 2):
            xp = k.xb[:, p * LANES:(p + 1) * LANES]
            acc = jnp.zeros((q, LANES), F32)
            for hh in range(2):
                m = (k.cbm * k.decay(2 * p + hh)).astype(BF16)
                acc = acc + _dot(m, jnp.where(k.pair_masks[hh], xp, jnp.zeros_like(xp)), 1, 0)
            parts.append(acc)
        y = jnp.concatenate(parts, axis=1) + y_off
        xd = (k.x * k.d_exp).astype(BF16)
        s_scr[...] = k.elast_exp * s_prev + _dot(k.bb, xd, 0, 0)
        y_pre = y + dsk_ref[...] * k.xs
        zv = z_ref[...]
        ypre_ref[...] = y_pre
        yg_ref[...] = y_pre * (zv * jax.nn.sigmoid(zv))

    return pl.pallas_call(
        body, name=name, grid=(SSD_GROUPS, n_chunks),
        out_shape=[jax.ShapeDtypeStruct((t_len, ssd_w), F32), jax.ShapeDtypeStruct((t_len, ssd_w), F32),
                   jax.ShapeDtypeStruct((n_chunks, SSD_STATE, ssd_w), F32)],
        in_specs=[sp["dtraw"], sp["small"], sp["small"], sp["dsk"], sp["xs"], sp["b"], sp["c"], sp["z"]],
        out_specs=[sp["tok"], sp["tok"], sp["state"]],
        scratch_shapes=[pltpu.VMEM((SSD_STATE, gw), F32)],
        compiler_params=_params("parallel", "arbitrary"),
    )(dtraw_g, bias_g, alog_g, dsk_exp, xbc, xbc, xbc, qkvz)


def _ssd_bwd(xbc, qkvz, dtraw_g, bias_g, alog_g, dsk_exp, ypre, states, dyg, heads, name):
    t_len = xbc.shape[0]
    q, gw = SSD_CHUNK, heads * HEAD_DIM
    n_chunks = t_len // q
    ssd_w = SSD_GROUPS * gw
    sp = _ssd_specs(t_len, heads, n_chunks, xbc.shape[1], rev=True)

    def body(dtraw_ref, bias_ref, alog_ref, dsk_ref, xs_ref, b_ref, c_ref, z_ref, ypre_ref, st_ref, dyg_ref,
             dxs_ref, db_ref, dc_ref, dz_ref, ddt_ref, small_ref, g_scr):
        first = pl.program_id(1) == 0

        @pl.when(first)
        def _():
            g_scr[...] = jnp.zeros_like(g_scr)

        k = _SsdChunk(dtraw_ref, bias_ref, alog_ref, xs_ref, b_ref, c_ref, heads)
        zv = z_ref[...]
        sig = jax.nn.sigmoid(zv)
        dyg = dyg_ref[...]
        y_pre = ypre_ref[...]
        dy = dyg * (zv * sig)
        dz_ref[...] = dyg * y_pre * (sig * (1.0 + zv * (1.0 - sig)))
        dsk = dsk_ref[...]
        g_next = g_scr[...]
        s_prev = st_ref[...]
        sb = s_prev.astype(BF16)
        xd = k.x * k.d_exp
        xdb = xd.astype(BF16)
        gb = g_next.astype(BF16)
        dx_off = k.d_exp * _dot(k.bb, gb, 1, 0)
        dyb = dy.astype(BF16)
        dcb = jnp.zeros((q, q), F32)
        lane = lax.broadcasted_iota(jnp.int32, (q, LANES), 1)
        row_t = lax.broadcasted_iota(jnp.int32, (LANES, q), 0)
        w_rows = jnp.zeros((q, LANES), F32)
        w_cols_t = jnp.zeros((LANES, q), F32)
        parts = []
        for p in range(heads // 2):
            cols = slice(p * LANES, (p + 1) * LANES)
            dyp, xp = dyb[:, cols], k.xb[:, cols]
            acc = jnp.zeros((q, LANES), F32)
            for hh in range(2):
                j = 2 * p + hh
                lm = k.decay(j)
                m32 = k.cbm * lm
                dym = jnp.where(k.pair_masks[hh], dyp, jnp.zeros_like(dyp))
                acc = acc + _dot(m32.astype(BF16), dym, 0, 0)
                dm = _dot(dym, xp, 1, 1)
                dcb = dcb + dm * lm
                wmat = dm * m32
                w_rows = w_rows + jnp.where(lane == j, jnp.sum(wmat, axis=1, keepdims=True), 0.0)
                w_cols_t = w_cols_t + jnp.where(row_t == j, jnp.sum(wmat, axis=0, keepdims=True), 0.0)
            parts.append(acc)
        dx = jnp.concatenate(parts, axis=1) + dx_off
        dcbb = dcb.astype(BF16)
        edy = (k.e_exp * dy).astype(BF16)
        dc_ref[...] = _dot(dcbb, k.bb, 1, 0) + _dot(edy, sb, 1, 1)
        db_ref[...] = _dot(dcbb, k.cb, 0, 0) + _dot(xdb, gb, 1, 1)
        g_scr[...] = k.elast_exp * g_next + _dot(k.cb, edy, 0, 0)

        y_off = k.e_exp * _dot(k.cb, sb, 1, 0)
        dcum = w_rows - w_cols_t.T + k.segsum(dy * y_off)
        t_term = k.segsum(k.x * dx_off)
        gs = jnp.broadcast_to(jnp.sum(g_next * s_prev, axis=0, keepdims=True), (8, gw))
        carried = k.segsum(gs)[0:1, :] * jnp.exp(k.cum[q - 1:q, :])
        dda = _rev_cumsum_rows(dcum) + (_cumsum_rows(t_term) - t_term) + carried
        ddt = jnp.where(k.head_lanes, dda * k.a + k.segsum(dx * k.xs), 0.0)
        ddtraw = ddt * jax.nn.sigmoid(k.dt_arg)
        ddt_ref[...] = ddtraw
        dxs_ref[...] = dx * k.dt_exp + dsk * dy
        ds = jnp.broadcast_to(jnp.sum(dy * k.xs, axis=0, keepdims=True), (8, gw))
        d_alog = jnp.sum(jnp.where(k.head_lanes, dda * k.dt, 0.0), axis=0, keepdims=True) * k.a
        rows8 = lax.broadcasted_iota(jnp.int32, (8, LANES), 0)
        small = jnp.where(rows8 == 0, d_alog, 0.0)
        small = small + jnp.where(rows8 == 1, jnp.sum(ddtraw, axis=0, keepdims=True), 0.0)
        small = small + jnp.where(rows8 == 2, k.segsum(ds)[0:1, :], 0.0)

        @pl.when(first)
        def _():
            small_ref[...] = small

        @pl.when(jnp.logical_not(first))
        def _():
            small_ref[...] += small

    bc_out = sp["bc"]
    return pl.pallas_call(
        body, name=name, grid=(SSD_GROUPS, n_chunks),
        out_shape=[jax.ShapeDtypeStruct((t_len, ssd_w), F32),
                   jax.ShapeDtypeStruct((t_len, SSD_GROUPS * SSD_STATE), F32),
                   jax.ShapeDtypeStruct((t_len, SSD_GROUPS * SSD_STATE), F32),
                   jax.ShapeDtypeStruct((t_len, ssd_w), F32),
                   jax.ShapeDtypeStruct((SSD_GROUPS, t_len, LANES), F32),
                   jax.ShapeDtypeStruct((SSD_GROUPS, 8, LANES), F32)],
        in_specs=[sp["dtraw"], sp["small"], sp["small"], sp["dsk"], sp["xs"], sp["b"], sp["c"], sp["z"],
                  sp["tok"], sp["state"], sp["tok"]],
        out_specs=[sp["tok"], bc_out, bc_out, sp["tok"], sp["dtraw"],
                   pl.BlockSpec((None, 8, LANES), lambda g, c: (g, 0, 0))],
        scratch_shapes=[pltpu.VMEM((SSD_STATE, gw), F32)],
        compiler_params=_params("parallel", "arbitrary"),
    )(dtraw_g, bias_g, alog_g, dsk_exp, xbc, xbc, xbc, qkvz, ypre, states, dyg)


def _adamw(w, g, m, v, name, rides=None):
    n_lead, rows, lanes = w.shape
    tr = _row_tile(rows, lanes, 4, 14)
    c1 = 1.0 / (1.0 - ADAM_B1 ** ADAM_STEP)
    c2 = 1.0 / (1.0 - ADAM_B2 ** ADAM_STEP)

    def body(w_ref, g_ref, m_ref, v_ref, d_ref, nm_ref, nv_ref):
        gv = g_ref[...]
        nm = ADAM_B1 * m_ref[...] + (1.0 - ADAM_B1) * gv
        nv = ADAM_B2 * v_ref[...] + (1.0 - ADAM_B2) * (gv * gv)
        nm_ref[...] = nm
        nv_ref[...] = nv
        d_ref[...] = -ADAM_LR * ((nm * c1) / (jnp.sqrt(nv * c2) + ADAM_EPS) + ADAM_WD * w_ref[...])

    spec = pl.BlockSpec((None, tr, lanes), lambda l, i: (l, i, 0))
    return _pallas(body, name=name, grid=(n_lead, rows // tr), out_shape=[jax.ShapeDtypeStruct(w.shape, F32)] * 3,
                   in_specs=[spec] * 4, out_specs=[spec] * 3, operands=[w, g, m, v],
                   semantics=("parallel", "parallel"), rides=rides)


def _pad_lanes(a, width=LANES):
    return jnp.pad(a, ((0, 0), (0, width - a.shape[1])))


def _group_pad(v, heads):
    return _pad_lanes(v.reshape(SSD_GROUPS, heads))[:, None, :]


def _layer_fwd(x0, p, wt, dims, tag, rides):
    w_attn, heads_g, n_heads, conv_ch = dims["w_attn"], dims["heads_g"], dims["n_heads"], dims["conv_ch"]
    h1 = _rmsnorm_fwd([x0], [[x0.shape[1]]], p["ln1_g"], f"ln1_fwd{tag}")
    proj = _mm(h1, wt("w_in"), name=f"in_proj{tag}", tn=1152, rides=rides)

    outs, lses = [], []
    for d in BRANCH_DILATIONS:
        o, l = _attn_branch_fwd(proj, w_attn, d, n_heads, f"attn_fwd_d{d}{tag}", rides)
        outs.append(o)
        lses.append(l)
    attn, lse = _attn_combine(outs, lses, f"attn_combine{tag}")

    xbc = _conv_fwd(proj, 4 * w_attn, p["conv_w"], p["conv_b"], f"conv_fwd{tag}")
    dt_col = 4 * w_attn + conv_ch
    dtraw_g = jnp.stack([_pad_lanes(proj[:, dt_col + g * heads_g:dt_col + (g + 1) * heads_g])
                         for g in range(SSD_GROUPS)])
    bias_g, alog_g = _group_pad(p["dt_bias"], heads_g), _group_pad(p["a_log"], heads_g)
    dsk_exp = jnp.repeat(p["d_skip"], HEAD_DIM).reshape(SSD_GROUPS, 1, heads_g * HEAD_DIM)
    yg, ypre, states = _ssd_fwd(xbc, proj, dtraw_g, bias_g, alog_g, dsk_exp, heads_g, f"ssd_fwd{tag}")

    gw = heads_g * HEAD_DIM
    mix_g = jnp.concatenate([p["attn_norm_g"], p["ssd_norm_g"]])[None, :]
    mix = _rmsnorm_fwd([attn, yg], [[w_attn], [gw] * SSD_GROUPS], mix_g, f"mix_norm_fwd{tag}")
    x1 = _mm(mix, wt("w_out"), name=f"out_proj{tag}", residual=x0, rides=rides)
    h2 = _rmsnorm_fwd([x1], [[x1.shape[1]]], p["ln2_g"], f"ln2_fwd{tag}")
    u = _mm(h2, wt("w_mlp_in"), name=f"mlp_in{tag}", out_dtype=BF16, tn=1024, rides=rides)
    x2 = _mm(u, wt("w_mlp_out"), name=f"mlp_out{tag}", a_act="relu2", residual=x1, tn=256, rides=rides)
    saved = dict(x0=x0, h1=h1, proj=proj, attn=attn, lse=lse, xbc=xbc, dtraw_g=dtraw_g,
                 bias_g=bias_g, alog_g=alog_g, dsk_exp=dsk_exp, yg=yg, ypre=ypre, states=states, mix=mix,
                 mix_g=mix_g, x1=x1, h2=h2, u=u)
    return x2, saved


def _pair_sums(ex, host, items):
    swapped = ex["rides"].done[("swap", host)]
    core = lax.axis_index("c").astype(jnp.int32).reshape(1)
    for i, (n, l) in enumerate(items):
        ex["pair"][(n, l)] = _pair_sum(ex["bufs"][(n, l)], swapped[i], core, f"pair_sum_{n}_l{l}")


def _layer_bwd(dx2, dx2_b, p, wt, s, dims, l, ex, copy_dx0):
    w_attn, heads_g, n_heads, conv_ch = dims["w_attn"], dims["heads_g"], dims["n_heads"], dims["conv_ch"]
    t_len, d_model = dx2.shape
    gw = heads_g * HEAD_DIM
    h_ssd = heads_g * SSD_GROUPS
    tag, rides, bufs = f"_l{l}", ex["rides"], ex["bufs"]
    du = _mm(dx2_b, wt("w_mlp_out"), name=f"mlp_out_dx{tag}", tb=True, gate=s["u"], out_dtype=BF16, tn=1024,
             rides=rides)
    d_wmo = _mm(s["u"], dx2_b, name=f"mlp_out_dw{tag}", ta=True, a_act="relu2", tm=512, tn=1024, out_dtype=BF16)
    bufs[("w_mlp_out", l)] = d_wmo.reshape(N_DEV, -1, d_model)
    bufs[("w_mlp_in", l)] = _mm(s["h2"], du, name=f"mlp_in_dw{tag}", ta=True, tm=512, tn=1024, out_dtype=BF16,
                                out_chunk=du.shape[1] // N_DEV)
    dh2 = _mm(du, wt("w_mlp_in"), name=f"mlp_in_dx{tag}", tb=True, tn=256, rides=rides)
    _pair_sums(ex, f"mlp_in_dx{tag}", [("w_mlp_out", l), ("w_mlp_in", l)])
    (dx1,), d_ln2, (dx1_b,) = _rmsnorm_bwd([s["x1"]], [[d_model]], p["ln2_g"], dh2, [dx2], f"ln2_bwd{tag}",
                                           bf16_copy=True)
    dmix = _mm(dx1_b, wt("w_out"), name=f"out_proj_dx{tag}", tb=True)
    d_wo = _mm(s["mix"], dx1_b, name=f"out_proj_dw{tag}", ta=True, tm=512, tn=1024, out_dtype=BF16)
    bufs[("w_out", l)] = d_wo.reshape(N_DEV, -1, d_model)
    after_branch = {BRANCH_DILATIONS[0]: [("w_out", l)]}
    (dattn, dyg), d_mix_g, _ = _rmsnorm_bwd([s["attn"], s["yg"]], [[w_attn], [gw] * SSD_GROUPS], s["mix_g"], dmix,
                                           [None, None], f"mix_norm_bwd{tag}")
    dxs, db, dc, dz, ddtraw_g, ssd_small = _ssd_bwd(
        s["xbc"], s["proj"], s["dtraw_g"], s["bias_g"], s["alog_g"], s["dsk_exp"], s["ypre"], s["states"], dyg,
        heads_g, f"ssd_bwd{tag}")
    dxbc = jnp.concatenate([dxs, db, dc], axis=1)
    dxbc_raw, d_conv_w, d_conv_b = _conv_bwd(s["proj"], 4 * w_attn, p["conv_w"], p["conv_b"], dxbc, f"conv_bwd{tag}")
    acc = None
    for d in BRANCH_DILATIONS:
        acc = _attn_branch_bwd(s["proj"], w_attn, s["attn"], s["lse"], dattn, d, n_heads, f"attn_bwd_d{d}{tag}", acc,
                               rides)
        if d in after_branch:
            _pair_sums(ex, f"attn_bwd_d{d}{tag}", after_branch[d])
    w_in = wt("w_in")
    in_proj = 4 * w_attn + conv_ch + h_ssd
    pad = jnp.zeros((t_len, w_in.shape[1] - in_proj), F32)
    dproj = jnp.concatenate([*acc, dz, dxbc_raw] + [ddtraw_g[g, :, :heads_g] for g in range(SSD_GROUPS)] + [pad],
                            axis=1).astype(BF16)
    d_win = _mm(s["h1"], dproj, name=f"in_proj_dw{tag}", ta=True, tm=512, tn=1152, out_dtype=BF16, rides=rides)
    bufs[("w_in", l)] = d_win[:, :in_proj].reshape(d_model, N_DEV, -1).transpose(1, 0, 2)
    dh1 = _mm(dproj, w_in, name=f"in_proj_dx{tag}", tb=True, rides=rides)
    _pair_sums(ex, f"in_proj_dx{tag}", [("w_in", l)])
    (dx0,), d_ln1, dx0_b = _rmsnorm_bwd([s["x0"]], [[d_model]], p["ln1_g"], dh1, [dx1], f"ln1_bwd{tag}",
                                        bf16_copy=copy_dx0)

    small = ssd_small[:, :, :heads_g]
    grads = dict(
        ln1_g=d_ln1[0], conv_w=d_conv_w, conv_b=d_conv_b[0],
        a_log=small[:, 0].reshape(h_ssd), dt_bias=small[:, 1].reshape(h_ssd), d_skip=small[:, 2].reshape(h_ssd),
        attn_norm_g=d_mix_g[0, :w_attn], ssd_norm_g=d_mix_g[0, w_attn:], ln2_g=d_ln2[0])
    return dx0, (dx0_b[0] if copy_dx0 else None), grads


_SMALL = ["ln1_g", "conv_w", "conv_b", "dt_bias", "a_log", "d_skip", "attn_norm_g", "ssd_norm_g", "ln2_g"]
_WEIGHTS = ["ln1_g", "w_in", "conv_w", "conv_b", "dt_bias", "a_log", "d_skip", "attn_norm_g", "ssd_norm_g",
            "w_out", "ln2_g", "w_mlp_in", "w_mlp_out", "final_norm_g"]


def _to_rows(a):
    flat = a.reshape(-1)
    rows = -(-flat.shape[0] // LANES)
    rows = -(-rows // 8) * 8
    return jnp.pad(flat, (0, rows * LANES - flat.shape[0])).reshape(rows, LANES)


def kernel(x, ln1_g, w_in, conv_w, conv_b, dt_bias, a_log, d_skip, attn_norm_g, ssd_norm_g, w_out, ln2_g, w_mlp_in, w_mlp_out, final_norm_g, loss_target, m_ln1_g, m_w_in, m_conv_w, m_conv_b, m_dt_bias, m_a_log, m_d_skip, m_attn_norm_g, m_ssd_norm_g, m_w_out, m_ln2_g, m_w_mlp_in, m_w_mlp_out, m_final_norm_g, v_ln1_g, v_w_in, v_conv_w, v_conv_b, v_dt_bias, v_a_log, v_d_skip, v_attn_norm_g, v_ssd_norm_g, v_w_out, v_ln2_g, v_w_mlp_in, v_w_mlp_out, v_final_norm_g):
    w = dict(ln1_g=ln1_g, w_in=w_in, conv_w=conv_w, conv_b=conv_b, dt_bias=dt_bias, a_log=a_log, d_skip=d_skip,
             attn_norm_g=attn_norm_g, ssd_norm_g=ssd_norm_g, w_out=w_out, ln2_g=ln2_g, w_mlp_in=w_mlp_in,
             w_mlp_out=w_mlp_out, final_norm_g=final_norm_g)
    mom = dict(ln1_g=m_ln1_g, w_in=m_w_in, conv_w=m_conv_w, conv_b=m_conv_b, dt_bias=m_dt_bias, a_log=m_a_log,
               d_skip=m_d_skip, attn_norm_g=m_attn_norm_g, ssd_norm_g=m_ssd_norm_g, w_out=m_w_out, ln2_g=m_ln2_g,
               w_mlp_in=m_w_mlp_in, w_mlp_out=m_w_mlp_out, final_norm_g=m_final_norm_g)
    var = dict(ln1_g=v_ln1_g, w_in=v_w_in, conv_w=v_conv_w, conv_b=v_conv_b, dt_bias=v_dt_bias, a_log=v_a_log,
               d_skip=v_d_skip, attn_norm_g=v_attn_norm_g, ssd_norm_g=v_ssd_norm_g, w_out=v_w_out, ln2_g=v_ln2_g,
               w_mlp_in=v_w_mlp_in, w_mlp_out=v_w_mlp_out, final_norm_g=v_final_norm_g)

    depth, d_model = ln1_g.shape
    t_len = x.shape[1]
    w_attn = attn_norm_g.shape[1]
    h_ssd = dt_bias.shape[1]
    conv_ch = conv_b.shape[1]
    in_proj = w_in.shape[2] * N_DEV
    assert ssd_norm_g.shape[1] == w_attn and in_proj == 4 * w_attn + conv_ch + h_ssd
    assert t_len % (BRANCH_DILATIONS[-1] * ATTN_BLOCK) == 0 and h_ssd % (2 * SSD_GROUPS) == 0
    dims = dict(w_attn=w_attn, heads_g=h_ssd // SSD_GROUPS, n_heads=w_attn // HEAD_DIM, conv_ch=conv_ch)
    names = ["w_in", "w_out", "w_mlp_in", "w_mlp_out"]

    rides = _Rides()
    ex = dict(rides=rides, bufs={}, pair={})
    latest, sent = {}, {}

    w_in0, casted = w_in[0].astype(BF16), {}

    def shard(n, l):
        if n == "conv_w":
            return conv_w
        return w_in0 if (n, l) == ("w_in", 0) else casted[(n, l)]

    def half(rows, part):
        return None if part is None else (part * (rows // 2), rows // 2)

    def plan_spread(host, n, l, part=None):
        key, prev = ("spread", host, n, l, part), latest.get((n, l))
        rides.put(host, key, lambda: _GatherSpread([shard(n, l)], rows=half(w[n].shape[1], part),
                                                   into=[rides.done[prev[0]][prev[1]]] if prev else None))
        latest[(n, l)] = (key, 0)

    def plan_pass(host, items):
        key, srcs = ("pass", host), [latest[it] for it in items]
        rides.put(host, key, lambda: _GatherPass([rides.done[k][i] for k, i in srcs]))
        for i, it in enumerate(items):
            latest[it] = (key, i)

    def plan_swap(host, items):
        rides.put(host, ("swap", host), lambda: _SiblingSwap([ex["bufs"][it] for it in items]))

    def plan_send(host, n, l, part=None):
        key, prev = ("send", host, n, l, part), sent.get((n, l))
        rides.put(host, key, lambda: _ChipSend([ex["pair"][(n, l)]], rows=half(ex["pair"][(n, l)].shape[1], part),
                                               into=[rides.done[prev[0]][prev[1]]] if prev else None))
        sent[(n, l)] = (key, 0)

    plan_spread("cast_weights", "w_in", 0)
    plan_spread("cast_weights", "conv_w", 0)
    plan_pass("pass_first", [("w_in", 0), ("conv_w", 0)])
    d_first, d_mid, d_last = (f"d{d}" for d in BRANCH_DILATIONS)
    for l in range(depth):
        t = f"_l{l}"
        if l == 0:
            plan_spread(f"in_proj{t}", "w_out", 0)
            plan_spread(f"in_proj{t}", "w_mlp_in", 0, 0)
            plan_spread(f"attn_fwd_{d_first}{t}", "w_mlp_in", 0, 1)
            plan_spread(f"attn_fwd_{d_mid}{t}", "w_mlp_out", 0, 0)
            plan_pass(f"attn_fwd_{d_mid}{t}", [("w_out", 0), ("w_mlp_in", 0)])
            plan_spread(f"attn_fwd_{d_last}{t}", "w_mlp_out", 0, 1)
            plan_pass(f"out_proj{t}", [("w_mlp_out", 0)])
        else:
            plan_spread(f"in_proj{t}", "w_mlp_out", l, 0)
            plan_spread(f"attn_fwd_{d_first}{t}", "w_mlp_out", l, 1)
            plan_pass(f"attn_fwd_{d_mid}{t}", [("w_mlp_out", l)])
        if l + 1 < depth:
            plan_spread(f"out_proj{t}", "w_out", l + 1)
            plan_spread(f"mlp_in{t}", "w_in", l + 1)
            plan_spread(f"mlp_out{t}", "w_mlp_in", l + 1)
            plan_pass(f"pass_weights_l{l + 1}", [("w_out", l + 1), ("w_in", l + 1), ("w_mlp_in", l + 1)])
        plan_swap(f"mlp_in_dx{t}", [("w_mlp_out", l), ("w_mlp_in", l)])
        plan_send(f"attn_bwd_{d_first}{t}", "w_mlp_out", l, 0)
        plan_swap(f"attn_bwd_{d_first}{t}", [("w_out", l)])
        plan_send(f"attn_bwd_{d_mid}{t}", "w_mlp_out", l, 1)
        plan_send(f"attn_bwd_{d_last}{t}", "w_mlp_in", l)
        plan_send(f"in_proj_dw{t}", "w_out", l)
        plan_swap(f"in_proj_dx{t}", [("w_in", l)])
        if l > 0:
            plan_send(f"mlp_out_dx_l{l - 1}", "w_in", l)
        else:
            plan_send("adamw_w_mlp_in", "w_in", l, 0)
            plan_send("adamw_w_mlp_out", "w_in", l, 1)

    picks = [(t, l) for l in range(depth) for t in range(len(names)) if (names[t], l) != ("w_in", 0)]
    for (t, l), cast in zip(picks, _cast_layers([w[n] for n in names], picks, "cast_weights", rides)):
        casted[(names[t], l)] = cast
    _alone(rides, "pass_first")
    full_cw = rides.done[("pass", "pass_first")][1].transpose(1, 2, 0, 3).reshape(depth, SSD_CONV, conv_ch)
    proj_cols = -(-in_proj // LANES) * LANES
    full = {}

    def weight(n, l):
        if (n, l) not in full:
            key, i = latest[(n, l)]
            g = rides.done[key][i]
            if n == "w_in":
                g = _pad_lanes(g.transpose(1, 0, 2).reshape(d_model, in_proj), proj_cols)
            elif n == "w_mlp_in":
                g = g.transpose(1, 0, 2).reshape(d_model, -1)
            else:
                g = g.reshape(-1, d_model)
            full[(n, l)] = g
        return full[(n, l)]

    layers = [dict(ln1_g=ln1_g[l][None, :], ln2_g=ln2_g[l][None, :], conv_w=full_cw[l], conv_b=conv_b[l][None, :],
                   dt_bias=dt_bias[l], a_log=a_log[l], d_skip=d_skip[l], attn_norm_g=attn_norm_g[l],
                   ssd_norm_g=ssd_norm_g[l]) for l in range(depth)]

    h = x[0]
    saved = []
    for l in range(depth):
        h, s = _layer_fwd(h, layers[l], functools.partial(lambda n, l: weight(n, l), l=l), dims, f"_l{l}", rides)
        saved.append(s)
        if l + 1 < depth:
            _alone(rides, f"pass_weights_l{l + 1}")
    dh, d_final_g, loss_part, dh_b = _loss_head(h, final_norm_g[None, :], loss_target[0], "loss_head")

    grads = [None] * depth
    for l in reversed(range(depth)):
        dh, dh_b, grads[l] = _layer_bwd(dh, dh_b, layers[l], functools.partial(lambda n, l: weight(n, l), l=l),
                                        saved[l], dims, l, ex, copy_dx0=l > 0)
    grad_x = dh[None]

    my_chip = (2 * lax.axis_index("x") + lax.axis_index("y")).astype(jnp.int32).reshape(1)
    gsum, delta, new_m, new_v = {}, {}, {}, {}
    for n in names[1:] + names[:1]:
        per_layer = []
        for l in range(depth):
            key, i = sent[(n, l)]
            per_layer.append(_sum_with_own(rides.done[key][i], ex["pair"][(n, l)], my_chip, f"sum_{n}_l{l}"))
        gsum[n] = jnp.stack(per_layer)
        delta[n], new_m[n], new_v[n] = _adamw(w[n], gsum[n], mom[n], var[n], f"adamw_{n}", rides)

    small_parts = [jnp.stack([grads[l][n] for l in range(depth)]).reshape(-1) for n in _SMALL]
    small_parts += [d_final_g.reshape(-1), loss_part[0, :1]]
    sizes = [int(a.shape[0]) for a in small_parts]
    packed = _to_rows(jnp.concatenate(small_parts))
    (gathered,) = _all_gather([packed], "gather_small_grads")
    total = _sum_leading(gathered, "sum_small_grads").reshape(-1)
    offs = np.cumsum([0] + sizes)
    pieces = [total[offs[i]:offs[i + 1]] for i in range(len(sizes))]
    for n, piece in zip(_SMALL, pieces):
        shape = (depth, SSD_CONV, conv_ch) if n == "conv_w" else w[n].shape
        gsum[n] = piece.reshape(shape)
    gsum["final_norm_g"] = pieces[len(_SMALL)]
    loss = pieces[len(_SMALL) + 1][0]
    my_id = 4 * lax.axis_index("x") + 2 * lax.axis_index("y") + lax.axis_index("c")
    cw = conv_w.shape[2]
    gsum["conv_w"] = lax.dynamic_slice_in_dim(gsum["conv_w"], my_id * cw, cw, axis=2)

    small_names = [n for n in _WEIGHTS if n not in names]
    sm_sizes = [int(np.prod(w[n].shape)) for n in small_names]
    pack = lambda d: _to_rows(jnp.concatenate([d[n].reshape(-1) for n in small_names]))[None]
    outs = _adamw(pack(w), pack(gsum), pack(mom), pack(var), "adamw_small")
    sm_offs = np.cumsum([0] + sm_sizes)
    for res, o in zip((delta, new_m, new_v), outs):
        flat = o.reshape(-1)
        for i, n in enumerate(small_names):
            res[n] = flat[sm_offs[i]:sm_offs[i + 1]].reshape(w[n].shape)

    return (loss, grad_x, *[gsum[n] for n in _WEIGHTS], *[delta[n] for n in _WEIGHTS],
            *[new_m[n] for n in _WEIGHTS], *[new_v[n] for n in _WEIGHTS])
```

```python
import functools
import math

import numpy as np
import jax
import jax.numpy as jnp
from jax import lax
from jax.experimental import pallas as pl
from jax.experimental.pallas import tpu as pltpu

F32 = jnp.float32
BF16 = jnp.bfloat16

N_DEV = 8
LANES = 128
HEAD_DIM = 64
ATTN_BLOCK = 128
BRANCH_DILATIONS = (1, 4, 16)
SSD_GROUPS = 2
SSD_STATE = 128
SSD_CHUNK = 128
SSD_CONV = 4
NORM_EPS = 1e-5
ADAM_LR, ADAM_B1, ADAM_B2, ADAM_EPS, ADAM_WD, ADAM_STEP = 0.001, 0.9, 0.999, 1e-08, 0.01, 10
VMEM_LIMIT_BYTES = 56 * 1024 * 1024
MESH = pl.DeviceIdType.MESH
NEG_INF = float("-inf")


def _params(*sem):
    return pltpu.CompilerParams(dimension_semantics=tuple(sem), vmem_limit_bytes=VMEM_LIMIT_BYTES)


def _pick(n, target, mult):
    best = None
    for t in range(mult, min(n, target) + 1, mult):
        if n % t == 0:
            best = t
    assert best is not None, (n, target, mult)
    return best


def _dot(a, b, ca, cb):
    return lax.dot_general(a, b, (((ca,), (cb,)), ((), ())), preferred_element_type=F32)


def _split3(v):
    hi = v.astype(BF16)
    r = v - hi.astype(F32)
    mid = r.astype(BF16)
    lo = (r - mid.astype(F32)).astype(BF16)
    return hi, mid, lo


def _dot_exact(v, sel, ca, cb):
    hi, mid, lo = _split3(v)
    return _dot(hi, sel, ca, cb) + _dot(mid, sel, ca, cb) + _dot(lo, sel, ca, cb)


_HBM = pl.BlockSpec(memory_space=pltpu.HBM)


def _all_gather(xs, name):
    n = len(xs)

    def body(*refs):
        x_refs, o_refs = refs[:n], refs[n:2 * n]
        send_sems, recv_sems = refs[2 * n:]
        x, y, c = lax.axis_index("x"), lax.axis_index("y"), lax.axis_index("c")
        me, sibling = (x, y, c), (x, y, 1 - c)
        chips = [(1 - x, y), (x, 1 - y), (1 - x, 1 - y)]

        def copy(t, k, block, to, src=None):
            bx, by, bc = block
            dst = o_refs[t].at[4 * bx + 2 * by + bc]
            return pltpu.make_async_remote_copy(
                src_ref=dst if src is None else src, dst_ref=dst,
                send_sem=send_sems.at[t, k], recv_sem=recv_sems.at[t, k],
                device_id=to, device_id_type=MESH)

        first, passed = [], []
        for t in range(n):
            cps = [copy(t, 0, me, sibling, src=x_refs[t])]
            cps += [copy(t, 1 + j, me, (*chip, c), src=x_refs[t]) for j, chip in enumerate(chips)]
            for cp in cps:
                cp.start()
            first += cps
        for t in range(n):
            for j, chip in enumerate(chips):
                copy(t, 1 + j, (*chip, c), me).wait_recv()
                fwd = copy(t, 4 + j, (*chip, c), sibling)
                fwd.start()
                passed.append(fwd)
        for t in range(n):
            copy(t, 0, sibling, me).wait_recv()
            back = copy(t, 7, sibling, sibling)
            back.start()
            passed.append(back)
        for t in range(n):
            copy(t, 7, me, me).wait_recv()
            for j, chip in enumerate(chips):
                copy(t, 4 + j, (*chip, 1 - c), me).wait_recv()
        for cp in first + passed:
            cp.wait_send()

    return pl.pallas_call(
        body, name=name,
        out_shape=[jax.ShapeDtypeStruct((N_DEV,) + a.shape, a.dtype) for a in xs],
        in_specs=[_HBM] * n, out_specs=[_HBM] * n,
        scratch_shapes=[pltpu.SemaphoreType.DMA((n, 8)), pltpu.SemaphoreType.DMA((n, 8))],
    )(*xs)


def _place():
    x, y, c = lax.axis_index("x"), lax.axis_index("y"), lax.axis_index("c")
    return x, y, c, 4 * x + 2 * y + c, (x, y, 1 - c), [(1 - x, y), (x, 1 - y), (1 - x, 1 - y)]


def _remote(src, dst, send_sem, recv_sem, to):
    return pltpu.make_async_remote_copy(src_ref=src, dst_ref=dst, send_sem=send_sem, recv_sem=recv_sem,
                                        device_id=to, device_id_type=MESH)


class _Riding:
    aliases = {}

    def copies(self, ins, outs, sems):
        raise NotImplementedError

    def start(self, ins, outs, sems):
        local, out, _ = self.copies(ins, outs, sems)
        for cp in local + out:
            cp.start()

    def wait(self, ins, outs, sems):
        local, out, landing = self.copies(ins, outs, sems)
        for cp in landing:
            cp.wait_recv()
        for cp in out:
            cp.wait_send()
        for cp in local:
            cp.wait()


def _rows_of(ref, rows):
    return ref if rows is None else ref.at[pl.ds(rows[0], rows[1])]


class _GatherSpread(_Riding):
    def __init__(self, xs, rows=None, into=None):
        n = len(xs)
        self.rows = rows
        self.ins = list(xs) + list(into or [])
        self.out_shapes = [jax.ShapeDtypeStruct((N_DEV,) + a.shape, a.dtype) for a in xs]
        self.aliases = {n + t: t for t in range(n)} if into else {}
        self.sem_shapes = [pltpu.SemaphoreType.DMA((n, 4)), pltpu.SemaphoreType.DMA((n, 4))]

    def copies(self, ins, outs, sems):
        send, recv = sems
        _, _, c, me, sibling, chips = _place()
        targets = [sibling] + [(*chip, c) for chip in chips]
        out, landing = [], []
        for t in range(len(outs)):
            src = _rows_of(ins[t], self.rows)
            for k, to in enumerate(targets):
                out.append(_remote(src, _rows_of(outs[t].at[me], self.rows), send.at[t, k], recv.at[t, k], to))
                theirs = _rows_of(outs[t].at[4 * to[0] + 2 * to[1] + to[2]], self.rows)
                landing.append(_remote(src, theirs, send.at[t, k], recv.at[t, k], to))
        return [], out, landing


class _GatherPass(_Riding):
    def __init__(self, bufs):
        n = len(bufs)
        self.ins = list(bufs)
        self.out_shapes = [jax.ShapeDtypeStruct(b.shape, b.dtype) for b in bufs]
        self.aliases = {t: t for t in range(n)}
        self.sem_shapes = [pltpu.SemaphoreType.DMA((n, 4)), pltpu.SemaphoreType.DMA((n, 4))]

    def copies(self, ins, outs, sems):
        send, recv = sems
        x, y, c, me, sibling, chips = _place()
        out, landing = [], []
        for t in range(len(outs)):
            for j, (px, py) in enumerate(chips + [(x, y)]):
                held = outs[t].at[4 * px + 2 * py + c] if j < 3 else outs[t].at[4 * x + 2 * y + 1 - c]
                lands = outs[t].at[4 * px + 2 * py + 1 - c] if j < 3 else outs[t].at[me]
                out.append(_remote(held, held, send.at[t, j], recv.at[t, j], sibling))
                landing.append(_remote(held, lands, send.at[t, j], recv.at[t, j], sibling))
        return [], out, landing


class _SiblingSwap(_Riding):
    def __init__(self, xs):
        n = len(xs)
        self.ins = list(xs)
        self.out_shapes = [jax.ShapeDtypeStruct((N_DEV // 2,) + a.shape[1:], a.dtype) for a in xs]
        self.sem_shapes = [pltpu.SemaphoreType.DMA((n, 4)), pltpu.SemaphoreType.DMA((n, 4))]

    def copies(self, ins, outs, sems):
        send, recv = sems
        _, _, c, _, sibling, _ = _place()
        out = [_remote(ins[t].at[2 * q + 1 - c], outs[t].at[q], send.at[t, q], recv.at[t, q], sibling)
               for t in range(len(ins)) for q in range(N_DEV // 2)]
        return [], out, out


class _ChipSend(_Riding):
    def __init__(self, ps, rows=None, into=None):
        n = len(ps)
        self.rows = rows
        self.ins = list(ps) + list(into or [])
        self.out_shapes = [jax.ShapeDtypeStruct((3,) + a.shape[1:], a.dtype) for a in ps]
        self.aliases = {n + t: t for t in range(n)} if into else {}
        self.sem_shapes = [pltpu.SemaphoreType.DMA((n, 3)), pltpu.SemaphoreType.DMA((n, 3))]

    def copies(self, ins, outs, sems):
        send, recv = sems
        _, _, c, _, _, chips = _place()
        out = [_remote(_rows_of(ins[t].at[2 * px + py], self.rows), _rows_of(outs[t].at[j], self.rows),
                       send.at[t, j], recv.at[t, j], (px, py, c))
               for t in range(len(outs)) for j, (px, py) in enumerate(chips)]
        return [], out, out


class _Bundle(_Riding):
    def __init__(self, comms):
        self.comms = comms
        self.ins = [a for cm in comms for a in cm.ins]
        self.out_shapes = [s for cm in comms for s in cm.out_shapes]
        self.sem_shapes = [s for cm in comms for s in cm.sem_shapes]
        self.aliases = {}
        i0 = o0 = 0
        for cm in comms:
            self.aliases.update({i0 + i: o0 + j for i, j in cm.aliases.items()})
            i0, o0 = i0 + len(cm.ins), o0 + len(cm.out_shapes)

    def copies(self, ins, outs, sems):
        local, out, landing = [], [], []
        i0 = o0 = s0 = 0
        for cm in self.comms:
            i1, o1, s1 = i0 + len(cm.ins), o0 + len(cm.out_shapes), s0 + len(cm.sem_shapes)
            a, b, c = cm.copies(ins[i0:i1], outs[o0:o1], sems[s0:s1])
            local, out, landing = local + a, out + b, landing + c
            i0, o0, s0 = i1, o1, s1
        return local, out, landing


class _Rides:
    def __init__(self):
        self.plan, self.done, self.aboard = {}, {}, {}

    def put(self, host, key, make):
        self.plan.setdefault(host, []).append((key, make))

    def board(self, host):
        if host not in self.plan:
            return None
        self.aboard[host] = [make() for _, make in self.plan[host]]
        return _Bundle(self.aboard[host])

    def land(self, host, results):
        o0 = 0
        for (key, _), cm in zip(self.plan[host], self.aboard[host]):
            self.done[key] = list(results[o0:o0 + len(cm.out_shapes)])
            o0 += len(cm.out_shapes)


def _pallas(body, *, name, grid, out_shape, in_specs, out_specs, operands, semantics, scratch_shapes=(), rides=None):
    comm = rides.board(name) if rides is not None else None
    if comm is None:
        return pl.pallas_call(
            body, name=name, grid=grid, out_shape=list(out_shape), in_specs=list(in_specs),
            out_specs=list(out_specs), scratch_shapes=list(scratch_shapes), compiler_params=_params(*semantics),
        )(*operands)
    n_in, n_out, n_scr = len(in_specs), len(out_shape), len(scratch_shapes)
    n_ci, n_co = len(comm.ins), len(comm.out_shapes)

    def hosted(*refs):
        cuts = np.cumsum([0, n_in, n_ci, n_out, n_co, n_scr])
        ins, c_ins, outs, c_outs, scr = (refs[cuts[i]:cuts[i + 1]] for i in range(5))
        sems = refs[cuts[5]:]
        ids = [pl.program_id(a) for a in range(len(grid))]
        first = functools.reduce(jnp.logical_and, [i == 0 for i in ids])
        last = functools.reduce(jnp.logical_and, [i == g - 1 for i, g in zip(ids, grid)])

        @pl.when(first)
        def _():
            comm.start(c_ins, c_outs, sems)

        body(*ins, *outs, *scr)

        @pl.when(last)
        def _():
            comm.wait(c_ins, c_outs, sems)

    results = pl.pallas_call(
        hosted, name=name, grid=grid, out_shape=list(out_shape) + comm.out_shapes,
        in_specs=list(in_specs) + [_HBM] * n_ci, out_specs=list(out_specs) + [_HBM] * n_co,
        scratch_shapes=list(scratch_shapes) + comm.sem_shapes,
        input_output_aliases={n_in + i: n_out + j for i, j in comm.aliases.items()},
        compiler_params=_params(*["arbitrary"] * len(grid)),
    )(*operands, *comm.ins)
    rides.land(name, results[n_out:])
    return results[:n_out]


def _alone(rides, name):
    comm = rides.board(name)

    def body(*refs):
        n_ci, n_co = len(comm.ins), len(comm.out_shapes)
        ins, outs, sems = refs[:n_ci], refs[n_ci:n_ci + n_co], refs[n_ci + n_co:]
        comm.start(ins, outs, sems)
        comm.wait(ins, outs, sems)

    results = pl.pallas_call(
        body, name=name, out_shape=comm.out_shapes, in_specs=[_HBM] * len(comm.ins),
        out_specs=[_HBM] * len(comm.out_shapes), scratch_shapes=comm.sem_shapes,
        input_output_aliases=dict(comm.aliases),
    )(*comm.ins)
    rides.land(name, results)


def _row_tile(rows, cols, itemsize, copies, budget=24 * 1024 * 1024):
    padded = -(-cols // LANES) * LANES
    mult = 8 * (4 // itemsize)
    if rows % mult:
        return rows
    return _pick(rows, max(mult, budget // (copies * padded * itemsize)), mult)


def _sum_leading(x, name):
    n_src, rows, cols = x.shape
    tr = _row_tile(rows, cols, 4, 2 * (n_src + 2))

    def body(x_ref, o_ref):
        acc = x_ref[0].astype(F32)
        for s in range(1, n_src):
            acc = acc + x_ref[s].astype(F32)
        o_ref[...] = acc

    return pl.pallas_call(
        body, name=name, grid=(rows // tr,), out_shape=jax.ShapeDtypeStruct((rows, cols), F32),
        in_specs=[pl.BlockSpec((n_src, tr, cols), lambda i: (0, i, 0))],
        out_specs=pl.BlockSpec((tr, cols), lambda i: (i, 0)), compiler_params=_params("parallel"),
    )(x)


def _pair_sum(buf, theirs, core, name):
    n_q, rows, cols = theirs.shape
    tr = _row_tile(rows, cols, 4, 6)

    def body(core_ref, mine_ref, theirs_ref, o_ref):
        o_ref[...] = (mine_ref[...].astype(F32) + theirs_ref[...].astype(F32)).astype(BF16)

    spec = pl.BlockSpec((None, tr, cols), lambda q, i, core_ref: (q, i, 0))
    return pl.pallas_call(
        body, name=name, out_shape=jax.ShapeDtypeStruct(theirs.shape, BF16),
        grid_spec=pltpu.PrefetchScalarGridSpec(
            num_scalar_prefetch=1, grid=(n_q, rows // tr),
            in_specs=[pl.BlockSpec((None, tr, cols), lambda q, i, core_ref: (2 * q + core_ref[0], i, 0)), spec],
            out_specs=spec),
        compiler_params=_params("parallel", "parallel"),
    )(core, buf, theirs)


def _sum_with_own(recv, pair, chip, name):
    n_src, rows, cols = recv.shape
    tr = _row_tile(rows, cols, 4, 2 * (n_src + 3))

    def body(chip_ref, own_ref, recv_ref, o_ref):
        acc = own_ref[...].astype(F32)
        for s in range(n_src):
            acc = acc + recv_ref[s].astype(F32)
        o_ref[...] = acc

    return pl.pallas_call(
        body, name=name, out_shape=jax.ShapeDtypeStruct((rows, cols), F32),
        grid_spec=pltpu.PrefetchScalarGridSpec(
            num_scalar_prefetch=1, grid=(rows // tr,),
            in_specs=[pl.BlockSpec((None, tr, cols), lambda i, chip_ref: (chip_ref[0], i, 0)),
                      pl.BlockSpec((n_src, tr, cols), lambda i, chip_ref: (0, i, 0))],
            out_specs=pl.BlockSpec((tr, cols), lambda i, chip_ref: (i, 0))),
        compiler_params=_params("parallel"),
    )(chip, pair, recv)


def _mm(a, b, *, name, ta=False, tb=False, tm=1024, tn=512, out_dtype=F32, a_act=None,
        residual=None, gate=None, out_chunk=None, rides=None):
    k_dim, m = (a.shape if ta else a.shape[::-1])
    n, kb = (b.shape if tb else b.shape[::-1])
    assert kb == k_dim, (a.shape, b.shape, ta, tb)
    tm, tn = _pick(m, tm, 128), _pick(out_chunk or n, tn, 128)
    ca, cb = (0 if ta else 1), (1 if tb else 0)
    a_spec = pl.BlockSpec((k_dim, tm), lambda i, j: (0, i)) if ta else pl.BlockSpec((tm, k_dim), lambda i, j: (i, 0))
    b_spec = pl.BlockSpec((tn, k_dim), lambda i, j: (j, 0)) if tb else pl.BlockSpec((k_dim, tn), lambda i, j: (0, j))
    mn_spec = pl.BlockSpec((tm, tn), lambda i, j: (i, j))
    if out_chunk:
        per = out_chunk // tn
        o_spec = pl.BlockSpec((None, tm, tn), lambda i, j: (j // per, i, j % per))
        out_shape = jax.ShapeDtypeStruct((n // out_chunk, m, out_chunk), out_dtype)
    else:
        o_spec = mn_spec
        out_shape = jax.ShapeDtypeStruct((m, n), out_dtype)
    operands, in_specs = [a, b], [a_spec, b_spec]
    for extra in (gate, residual):
        if extra is not None:
            operands.append(extra)
            in_specs.append(mn_spec)

    def body(*refs):
        a_ref, b_ref, o_ref = refs[0], refs[1], refs[-1]
        extras = list(refs[2:-1])
        gate_ref = extras.pop(0) if gate is not None else None
        res_ref = extras.pop(0) if residual is not None else None
        av = a_ref[...].astype(BF16)
        if a_act == "relu2":
            av = jnp.square(jnp.maximum(av, jnp.zeros_like(av)))
        r = _dot(av, b_ref[...].astype(BF16), ca, cb)
        if gate_ref is not None:
            r = r * (2.0 * jnp.maximum(gate_ref[...].astype(F32), 0.0))
        if res_ref is not None:
            r = r + res_ref[...].astype(F32)
        o_ref[...] = r.astype(out_dtype)

    return _pallas(body, name=name, grid=(m // tm, n // tn), out_shape=[out_shape], in_specs=in_specs,
                   out_specs=[o_spec], operands=operands, semantics=("parallel", "arbitrary"), rides=rides)[0]


def _rmsnorm_fwd(xs, seg_widths, g, name, tm=256):
    t_len = xs[0].shape[0]
    width = sum(x.shape[1] for x in xs)
    tm = _pick(t_len, tm, 16)
    n = len(xs)

    def body(*refs):
        x_refs, g_ref, o_ref = refs[:n], refs[n], refs[n + 1]
        col = 0
        for x_ref, widths in zip(x_refs, seg_widths):
            off = 0
            for w in widths:
                xv = x_ref[:, off:off + w].astype(F32)
                r = lax.rsqrt(jnp.mean(xv * xv, axis=1, keepdims=True) + NORM_EPS)
                o_ref[:, col:col + w] = (xv * r * g_ref[:, col:col + w]).astype(BF16)
                off += w
                col += w

    return pl.pallas_call(
        body, name=name, grid=(t_len // tm,),
        out_shape=jax.ShapeDtypeStruct((t_len, width), BF16),
        in_specs=[pl.BlockSpec((tm, x.shape[1]), lambda i: (i, 0)) for x in xs]
        + [pl.BlockSpec((1, width), lambda i: (0, 0))],
        out_specs=pl.BlockSpec((tm, width), lambda i: (i, 0)),
        compiler_params=_params("parallel"),
    )(*xs, g)


def _rmsnorm_bwd(xs, seg_widths, g, dh, residuals, name, tm=256, bf16_copy=False):
    t_len = xs[0].shape[0]
    width = sum(x.shape[1] for x in xs)
    tm = _pick(t_len, tm, 8)
    n = len(xs)
    has_res = [r is not None for r in residuals]
    res_ops = [r for r in residuals if r is not None]

    def body(*refs):
        x_refs, g_ref, dh_ref = refs[:n], refs[n], refs[n + 1]
        res_refs = list(refs[n + 2:n + 2 + len(res_ops)])
        dx_refs = refs[n + 2 + len(res_ops):n + 2 + len(res_ops) + n]
        dg_ref = refs[n + 2 + len(res_ops) + n]
        copy_refs = refs[n + 3 + len(res_ops) + n:]
        first = pl.program_id(0) == 0
        col = 0
        for idx, (x_ref, widths) in enumerate(zip(x_refs, seg_widths)):
            res_ref = res_refs.pop(0) if has_res[idx] else None
            off = 0
            for w in widths:
                xv = x_ref[:, off:off + w].astype(F32)
                r = lax.rsqrt(jnp.mean(xv * xv, axis=1, keepdims=True) + NORM_EPS)
                xh = xv * r
                dhv = dh_ref[:, col:col + w].astype(F32)
                gd = dhv * g_ref[:, col:col + w]
                dx = r * (gd - xh * jnp.mean(gd * xh, axis=1, keepdims=True))
                if res_ref is not None:
                    dx = dx + res_ref[:, off:off + w]
                dx_refs[idx][:, off:off + w] = dx
                if bf16_copy:
                    copy_refs[idx][:, off:off + w] = dx.astype(BF16)
                part = jnp.sum(dhv * xh, axis=0, keepdims=True)

                @pl.when(first)
                def _(part=part, col=col, w=w):
                    dg_ref[:, col:col + w] = part

                @pl.when(jnp.logical_not(first))
                def _(part=part, col=col, w=w):
                    dg_ref[:, col:col + w] += part
                off += w
                col += w

    outs = pl.pallas_call(
        body, name=name, grid=(t_len // tm,),
        out_shape=[jax.ShapeDtypeStruct(x.shape, F32) for x in xs] + [jax.ShapeDtypeStruct((1, width), F32)]
        + ([jax.ShapeDtypeStruct(x.shape, BF16) for x in xs] if bf16_copy else []),
        in_specs=[pl.BlockSpec((tm, x.shape[1]), lambda i: (i, 0)) for x in xs]
        + [pl.BlockSpec((1, width), lambda i: (0, 0)), pl.BlockSpec((tm, width), lambda i: (i, 0))]
        + [pl.BlockSpec((tm, r.shape[1]), lambda i: (i, 0)) for r in res_ops],
        out_specs=[pl.BlockSpec((tm, x.shape[1]), lambda i: (i, 0)) for x in xs]
        + [pl.BlockSpec((1, width), lambda i: (0, 0))]
        + ([pl.BlockSpec((tm, x.shape[1]), lambda i: (i, 0)) for x in xs] if bf16_copy else []),
        compiler_params=_params("arbitrary"),
    )(*xs, g, dh, *res_ops)
    return outs[:n], outs[n], outs[n + 1:]


def _loss_head(x, g, target, name, tm=256):
    t_len, d = x.shape
    tm = _pick(t_len, tm, 8)

    def body(x_ref, g_ref, t_ref, dx_ref, dg_ref, loss_ref, dxb_ref):
        first = pl.program_id(0) == 0
        xv = x_ref[...]
        r = lax.rsqrt(jnp.mean(xv * xv, axis=1, keepdims=True) + NORM_EPS)
        xh = xv * r
        gv = g_ref[...]
        err = xh * gv - t_ref[...]
        part_loss = 0.5 * jnp.sum(jnp.mean(err * err, axis=1, keepdims=True), axis=0, keepdims=True)
        dy = err * (1.0 / d)
        gd = dy * gv
        dx = r * (gd - xh * jnp.mean(gd * xh, axis=1, keepdims=True))
        dx_ref[...] = dx
        dxb_ref[...] = dx.astype(BF16)
        part_g = jnp.sum(dy * xh, axis=0, keepdims=True)
        part_loss = jnp.broadcast_to(part_loss, (1, LANES))

        @pl.when(first)
        def _():
            dg_ref[...] = part_g
            loss_ref[...] = part_loss

        @pl.when(jnp.logical_not(first))
        def _():
            dg_ref[...] += part_g
            loss_ref[...] += part_loss

    return pl.pallas_call(
        body, name=name, grid=(t_len // tm,),
        out_shape=[jax.ShapeDtypeStruct((t_len, d), F32), jax.ShapeDtypeStruct((1, d), F32),
                   jax.ShapeDtypeStruct((1, LANES), F32), jax.ShapeDtypeStruct((t_len, d), BF16)],
        in_specs=[pl.BlockSpec((tm, d), lambda i: (i, 0)), pl.BlockSpec((1, d), lambda i: (0, 0)),
                  pl.BlockSpec((tm, d), lambda i: (i, 0))],
        out_specs=[pl.BlockSpec((tm, d), lambda i: (i, 0)), pl.BlockSpec((1, d), lambda i: (0, 0)),
                   pl.BlockSpec((1, LANES), lambda i: (0, 0)), pl.BlockSpec((tm, d), lambda i: (i, 0))],
        compiler_params=_params("arbitrary"),
    )(x, g, target)


def _alibi_slope(h, n_heads):
    return jnp.exp(jnp.full((1, 1), -8.0 * math.log(2.0) / n_heads, F32) * (h + 1).astype(F32))


def _attn_tiles(d, w):
    return ATTN_BLOCK * d, (w if d == 1 else LANES)


def _residue_rows(r, d):
    return pl.ds(r, ATTN_BLOCK, stride=d) if d > 1 else pl.ds(0, ATTN_BLOCK)


def _attn_masks(first_block):
    i = lax.broadcasted_iota(jnp.int32, (2 * ATTN_BLOCK, 2 * ATTN_BLOCK), 0) % ATTN_BLOCK
    j = lax.broadcasted_iota(jnp.int32, (2 * ATTN_BLOCK, 2 * ATTN_BLOCK), 1)
    delta = i - j + ATTN_BLOCK
    valid = jnp.logical_and(delta >= 0, delta <= ATTN_BLOCK)
    valid = jnp.logical_and(valid, jnp.logical_or(j >= ATTN_BLOCK, jnp.logical_not(first_block)))
    return valid, delta.astype(F32)


def _stack_heads(x, masks):
    zero = jnp.zeros_like(x)
    return jnp.concatenate([jnp.where(masks[0], x, zero), jnp.where(masks[1], x, zero)], axis=0)


def _unstack_heads(x2, masks):
    return jnp.where(masks[0], x2[:ATTN_BLOCK], x2[ATTN_BLOCK:])


def _pair_slopes(first_head, p, n_heads, d):
    row = lax.broadcasted_iota(jnp.int32, (2 * ATTN_BLOCK, 1), 0)
    sa, sb = (_alibi_slope(first_head + 2 * p + hh, n_heads) * d for hh in range(2))
    return jnp.where(row < ATTN_BLOCK, sa, sb)


def _head_lane_masks():
    lane = lax.broadcasted_iota(jnp.int32, (ATTN_BLOCK, LANES), 1)
    return [lane < HEAD_DIM, lane >= HEAD_DIM]


def _attn_branch_fwd(proj, w, dilation, n_heads, name, rides=None):
    t_len = proj.shape[0]
    d = dilation
    rows, lw = _attn_tiles(d, w)
    nb = t_len // rows
    n_pairs = lw // LANES
    per = w // lw
    scale = HEAD_DIM ** -0.5

    def body(q_ref, kp_ref, kc_ref, vp_ref, vc_ref, o_ref, lse_ref):
        first_head = pl.program_id(0) * (2 * n_pairs)
        first_block = pl.program_id(1) == 0
        valid, delta = _attn_masks(first_block)
        masks = _head_lane_masks()
        ones = jnp.ones((2 * ATTN_BLOCK, LANES), BF16)
        for p in range(n_pairs):
            cols = pl.ds(p * LANES, LANES)
            bias = _pair_slopes(first_head, p, n_heads, d) * delta
            for r in range(d):
                rs = _residue_rows(r, d)
                q2 = _stack_heads((q_ref[rs, cols] * scale).astype(BF16), masks)
                k2 = jnp.concatenate([kp_ref[rs, cols], kc_ref[rs, cols]], axis=0).astype(BF16)
                v2 = jnp.concatenate([vp_ref[rs, cols], vc_ref[rs, cols]], axis=0).astype(BF16)
                s = jnp.where(valid, _dot(q2, k2, 1, 1) - bias, NEG_INF)
                m = jnp.max(s, axis=1, keepdims=True)
                pr = jnp.exp(s - m).astype(BF16)
                den = _dot(pr, ones, 1, 0)
                o_ref[rs, cols] = _unstack_heads(_dot(pr, v2, 1, 0) / den, masks)
                lse_ref[rs, cols] = _unstack_heads(m + jnp.log(den), masks)

    def spec(which, prev):
        if prev:
            return pl.BlockSpec((rows, lw), lambda b, n: (jnp.maximum(n - 1, 0), which * per + b))
        return pl.BlockSpec((rows, lw), lambda b, n: (n, which * per + b))

    o_spec = pl.BlockSpec((rows, lw), lambda b, n: (n, b))
    return _pallas(
        body, name=name, grid=(per, nb), out_shape=[jax.ShapeDtypeStruct((t_len, w), F32)] * 2,
        in_specs=[spec(0, False), spec(1, True), spec(1, False), spec(2, True), spec(2, False)],
        out_specs=[o_spec, o_spec], operands=[proj] * 5, semantics=("parallel", "parallel"), rides=rides)


def _attn_combine(outs, lses, name, tm=512):
    t_len, w = outs[0].shape
    tm = _pick(t_len, tm, 8)
    nbr = len(outs)

    def body(*refs):
        o_refs, l_refs = refs[:nbr], refs[nbr:2 * nbr]
        out_ref, lse_ref = refs[2 * nbr:]
        ls = [r[...] for r in l_refs]
        m = functools.reduce(jnp.maximum, ls)
        es = [jnp.exp(l - m) for l in ls]
        den = functools.reduce(lambda a, b: a + b, es)
        num = functools.reduce(lambda a, b: a + b, [e * r[...] for e, r in zip(es, o_refs)])
        out_ref[...] = num / den
        lse_ref[...] = m + jnp.log(den)

    spec = pl.BlockSpec((tm, w), lambda i: (i, 0))
    return pl.pallas_call(
        body, name=name, grid=(t_len // tm,),
        out_shape=[jax.ShapeDtypeStruct((t_len, w), F32)] * 2,
        in_specs=[spec] * (2 * nbr), out_specs=[spec, spec],
        compiler_params=_params("parallel"),
    )(*outs, *lses)


def _attn_branch_bwd(proj, w, out, lse, dout, dilation, n_heads, name, acc=None, rides=None):
    t_len = proj.shape[0]
    d = dilation
    rows, lw = _attn_tiles(d, w)
    nb = t_len // rows
    n_pairs = lw // LANES
    per = w // lw
    scale = HEAD_DIM ** -0.5
    n_acc = 0 if acc is None else 3

    def body(*refs):
        q_ref, kp_ref, kc_ref, vp_ref, vc_ref, out_ref, lse_ref, do_ref = refs[:8]
        acc_refs = refs[8:8 + n_acc]
        dq_ref, dk_ref, dv_ref, dk_carry, dv_carry = refs[8 + n_acc:]
        first_head = pl.program_id(0) * (2 * n_pairs)
        n = pl.program_id(1)
        first_block = n == 0
        valid, dist = _attn_masks(first_block)
        masks = _head_lane_masks()

        def plus(value, idx, *where):
            return value + acc_refs[idx][where] if n_acc else value

        @pl.when(first_block)
        def _():
            dk_carry[...] = jnp.zeros_like(dk_carry)
            dv_carry[...] = jnp.zeros_like(dv_carry)

        @pl.when(n < nb)
        def _():
            for p in range(n_pairs):
                cols = pl.ds(p * LANES, LANES)
                bias = _pair_slopes(first_head, p, n_heads, d) * dist
                for r in range(d):
                    rs = _residue_rows(r, d)
                    q2 = _stack_heads((q_ref[rs, cols] * scale).astype(BF16), masks)
                    k2 = jnp.concatenate([kp_ref[rs, cols], kc_ref[rs, cols]], axis=0).astype(BF16)
                    v2 = jnp.concatenate([vp_ref[rs, cols], vc_ref[rs, cols]], axis=0).astype(BF16)
                    do = do_ref[rs, cols]
                    do2 = _stack_heads(do.astype(BF16), masks)
                    do_out = do * out_ref[rs, cols]
                    lse_all = lse_ref[rs, cols]
                    delta = jnp.concatenate([jnp.sum(jnp.where(masks[hh], do_out, 0.0), axis=1, keepdims=True)
                                             for hh in range(2)], axis=0)
                    lse2 = jnp.concatenate([jnp.max(jnp.where(masks[hh], lse_all, NEG_INF), axis=1, keepdims=True)
                                            for hh in range(2)], axis=0)
                    s = jnp.where(valid, _dot(q2, k2, 1, 1) - bias, NEG_INF)
                    pr = jnp.exp(s - lse2)
                    ds = (pr * (_dot(do2, v2, 1, 1) - delta)).astype(BF16)
                    dq = _unstack_heads(_dot(ds, k2, 1, 0), masks)
                    dk2 = _dot(ds, q2, 0, 0)
                    dv2 = _dot(pr.astype(BF16), do2, 0, 0)
                    dq_ref[rs, cols] = plus(dq * scale, 0, rs, cols)
                    dk_ref[rs, cols] = plus(dk_carry[r, :, cols] + dk2[:ATTN_BLOCK], 1, rs, cols)
                    dv_ref[rs, cols] = plus(dv_carry[r, :, cols] + dv2[:ATTN_BLOCK], 2, rs, cols)
                    dk_carry[r, :, cols] = dk2[ATTN_BLOCK:]
                    dv_carry[r, :, cols] = dv2[ATTN_BLOCK:]

        @pl.when(n == nb)
        def _():
            for r in range(d):
                rs = _residue_rows(r, d)
                dk_ref[rs, :] = plus(dk_carry[r], 1, rs, slice(None))
                dv_ref[rs, :] = plus(dv_carry[r], 2, rs, slice(None))

    def qkv_spec(which, shift):
        return pl.BlockSpec((rows, lw), lambda b, n: (jnp.clip(n - shift, 0, nb - 1), which * per + b))

    q_like = pl.BlockSpec((rows, lw), lambda b, n: (jnp.minimum(n, nb - 1), b))
    k_like = pl.BlockSpec((rows, lw), lambda b, n: (jnp.maximum(n - 1, 0), b))
    return _pallas(
        body, name=name, grid=(per, nb + 1), out_shape=[jax.ShapeDtypeStruct((t_len, w), F32)] * 3,
        in_specs=[qkv_spec(0, 0), qkv_spec(1, 1), qkv_spec(1, 0), qkv_spec(2, 1), qkv_spec(2, 0),
                  q_like, q_like, q_like] + [q_like, k_like, k_like][:n_acc],
        out_specs=[q_like, k_like, k_like], operands=[proj] * 5 + [out, lse, dout, *(acc or ())],
        scratch_shapes=[pltpu.VMEM((d, ATTN_BLOCK, lw), F32), pltpu.VMEM((d, ATTN_BLOCK, lw), F32)],
        semantics=("parallel", "arbitrary"), rides=rides)


def _shift_down(u, s):
    if s == 0:
        return u
    row = lax.broadcasted_iota(jnp.int32, u.shape, 0)
    return jnp.where(row >= s, pltpu.roll(u, s, 0), 0.0)


def _shift_up(u, s):
    if s == 0:
        return u
    n = u.shape[0]
    row = lax.broadcasted_iota(jnp.int32, u.shape, 0)
    return jnp.where(row < n - s, pltpu.roll(u, n - s, 0), 0.0)


def _conv_fwd(u, col0, w, b, name):
    t_len, ch = u.shape[0], w.shape[1]
    blk0 = col0 // LANES

    def body(u_ref, w_ref, b_ref, o_ref):
        uv = u_ref[...]
        pre = b_ref[...] + jnp.zeros_like(uv)
        for k in range(SSD_CONV):
            pre = pre + w_ref[k:k + 1, :] * _shift_down(uv, SSD_CONV - 1 - k)
        o_ref[...] = pre * jax.nn.sigmoid(pre)

    return pl.pallas_call(
        body, name=name, grid=(ch // LANES,),
        out_shape=jax.ShapeDtypeStruct((t_len, ch), F32),
        in_specs=[pl.BlockSpec((t_len, LANES), lambda j: (0, blk0 + j)),
                  pl.BlockSpec((SSD_CONV, LANES), lambda j: (0, j)), pl.BlockSpec((1, LANES), lambda j: (0, j))],
        out_specs=pl.BlockSpec((t_len, LANES), lambda j: (0, j)),
        compiler_params=_params("parallel"),
    )(u, w, b)


def _conv_bwd(u, col0, w, b, dact, name):
    t_len, ch = u.shape[0], w.shape[1]
    blk0 = col0 // LANES

    def body(u_ref, w_ref, b_ref, da_ref, du_ref, dw_ref, db_ref):
        uv = u_ref[...]
        shifted = [_shift_down(uv, SSD_CONV - 1 - k) for k in range(SSD_CONV)]
        pre = b_ref[...] + jnp.zeros_like(uv)
        for k in range(SSD_CONV):
            pre = pre + w_ref[k:k + 1, :] * shifted[k]
        sig = jax.nn.sigmoid(pre)
        dpre = da_ref[...] * (sig * (1.0 + pre * (1.0 - sig)))
        du = jnp.zeros_like(uv)
        for k in range(SSD_CONV):
            du = du + w_ref[k:k + 1, :] * _shift_up(dpre, SSD_CONV - 1 - k)
            dw_ref[k:k + 1, :] = jnp.sum(dpre * shifted[k], axis=0, keepdims=True)
        du_ref[...] = du
        db_ref[...] = jnp.sum(dpre, axis=0, keepdims=True)

    col = pl.BlockSpec((t_len, LANES), lambda j: (0, j))
    w_spec = pl.BlockSpec((SSD_CONV, LANES), lambda j: (0, j))
    b_spec = pl.BlockSpec((1, LANES), lambda j: (0, j))
    return pl.pallas_call(
        body, name=name, grid=(ch // LANES,),
        out_shape=[jax.ShapeDtypeStruct((t_len, ch), F32), jax.ShapeDtypeStruct((SSD_CONV, ch), F32),
                   jax.ShapeDtypeStruct((1, ch), F32)],
        in_specs=[pl.BlockSpec((t_len, LANES), lambda j: (0, blk0 + j)), w_spec, b_spec, col],
        out_specs=[col, w_spec, b_spec],
        compiler_params=_params("parallel"),
    )(u, w, b, dact)


def _cumsum_rows(v):
    n = v.shape[0]
    row = lax.broadcasted_iota(jnp.int32, v.shape, 0)
    s = 1
    while s < n:
        v = v + jnp.where(row >= s, pltpu.roll(v, s, 0), 0.0)
        s *= 2
    return v


def _rev_cumsum_rows(v):
    n = v.shape[0]
    row = lax.broadcasted_iota(jnp.int32, v.shape, 0)
    s = 1
    while s < n:
        v = v + jnp.where(row < n - s, pltpu.roll(v, n - s, 0), 0.0)
        s *= 2
    return v


def _head_selector(heads, width):
    j = lax.broadcasted_iota(jnp.int32, (LANES, width), 0)
    lane = lax.broadcasted_iota(jnp.int32, (LANES, width), 1)
    return jnp.where(jnp.logical_and(lane // HEAD_DIM == j, j < heads), 1.0, 0.0).astype(BF16)


class _SsdChunk:
    def __init__(self, dtraw_ref, bias_ref, alog_ref, xs_ref, b_ref, c_ref, heads):
        q = SSD_CHUNK
        width = heads * HEAD_DIM
        lane = lax.broadcasted_iota(jnp.int32, (q, LANES), 1)
        self.head_lanes = lane < heads
        lane1 = lax.broadcasted_iota(jnp.int32, (1, LANES), 1)
        self.a = jnp.where(lane1 < heads, -jnp.exp(alog_ref[...]), 0.0)
        self.dt_arg = dtraw_ref[...] + bias_ref[...]
        self.dt = jnp.where(self.head_lanes, jax.nn.softplus(self.dt_arg), 0.0)
        self.cum = _cumsum_rows(self.dt * self.a)
        self.cum_t = self.cum.T
        last = self.cum[q - 1:q, :]
        self.sel = _head_selector(heads, width)
        self.expand = lambda v: _dot_exact(v, self.sel, 1, 0)
        self.segsum = lambda v: _dot_exact(v, self.sel, 1, 1)
        self.e_exp = self.expand(jnp.exp(self.cum))
        self.d_exp = self.expand(jnp.exp(last - self.cum))
        self.elast_exp = self.e_exp[q - 1:q, :]
        self.dt_exp = self.expand(self.dt)
        self.xs = xs_ref[...]
        self.x = self.xs * self.dt_exp
        self.xb = self.x.astype(BF16)
        self.bb = b_ref[...].astype(BF16)
        self.cb = c_ref[...].astype(BF16)
        self.cbm = _dot(self.cb, self.bb, 1, 1)
        li = lax.broadcasted_iota(jnp.int32, (q, q), 0)
        si = lax.broadcasted_iota(jnp.int32, (q, q), 1)
        self.tri = li >= si
        hl = lax.broadcasted_iota(jnp.int32, (q, LANES), 1)
        self.pair_masks = [hl < HEAD_DIM, hl >= HEAD_DIM]

    def decay(self, j):
        diff = self.cum[:, j:j + 1] - self.cum_t[j:j + 1, :]
        return jnp.exp(jnp.where(self.tri, diff, NEG_INF))


def _ssd_specs(t_len, heads, n_chunks, xbc_cols, rev):
    q, gw = SSD_CHUNK, heads * HEAD_DIM
    ssd_w = SSD_GROUPS * gw
    b_blk = ssd_w // SSD_STATE
    ch = (lambda c: n_chunks - 1 - c) if rev else (lambda c: c)
    return dict(
        dtraw=pl.BlockSpec((None, q, LANES), lambda g, c: (g, ch(c), 0)),
        small=pl.BlockSpec((None, 1, LANES), lambda g, c: (g, 0, 0)),
        dsk=pl.BlockSpec((None, 1, gw), lambda g, c: (g, 0, 0)),
        xs=pl.BlockSpec((q, gw), lambda g, c: (ch(c), g)),
        b=pl.BlockSpec((q, SSD_STATE), lambda g, c: (ch(c), b_blk + g)),
        c=pl.BlockSpec((q, SSD_STATE), lambda g, c: (ch(c), b_blk + SSD_GROUPS + g)),
        z=pl.BlockSpec((q, gw), lambda g, c: (ch(c), 3 * SSD_GROUPS + g)),
        tok=pl.BlockSpec((q, gw), lambda g, c: (ch(c), g)),
        state=pl.BlockSpec((None, SSD_STATE, gw), lambda g, c: (ch(c), 0, g)),
        bc=pl.BlockSpec((q, SSD_STATE), lambda g, c: (ch(c), g)),
    )


def _ssd_fwd(xbc, qkvz, dtraw_g, bias_g, alog_g, dsk_exp, heads, name):
    t_len = xbc.shape[0]
    q, gw = SSD_CHUNK, heads * HEAD_DIM
    n_chunks = t_len // q
    ssd_w = SSD_GROUPS * gw
    sp = _ssd_specs(t_len, heads, n_chunks, xbc.shape[1], rev=False)

    def body(dtraw_ref, bias_ref, alog_ref, dsk_ref, xs_ref, b_ref, c_ref, z_ref,
             yg_ref, ypre_ref, st_ref, s_scr):
        @pl.when(pl.program_id(1) == 0)
        def _():
            s_scr[...] = jnp.zeros_like(s_scr)

        k = _SsdChunk(dtraw_ref, bias_ref, alog_ref, xs_ref, b_ref, c_ref, heads)
        s_prev = s_scr[...]
        st_ref[...] = s_prev
        y_off = k.e_exp * _dot(k.cb, s_prev.astype(BF16), 1, 0)
        parts = []
        for p in range(heads // 2):
            xp = k.xb[:, p * LANES:(p + 1) * LANES]
            acc = jnp.zeros((q, LANES), F32)
            for hh in range(2):
                m = (k.cbm * k.decay(2 * p + hh)).astype(BF16)
                acc = acc + _dot(m, jnp.where(k.pair_masks[hh], xp, jnp.zeros_like(xp)), 1, 0)
            parts.append(acc)
        y = jnp.concatenate(parts, axis=1) + y_off
        xd = (k.x * k.d_exp).astype(BF16)
        s_scr[...] = k.elast_exp * s_prev + _dot(k.bb, xd, 0, 0)
        y_pre = y + dsk_ref[...] * k.xs
        zv = z_ref[...]
        ypre_ref[...] = y_pre
        yg_ref[...] = y_pre * (zv * jax.nn.sigmoid(zv))

    return pl.pallas_call(
        body, name=name, grid=(SSD_GROUPS, n_chunks),
        out_shape=[jax.ShapeDtypeStruct((t_len, ssd_w), F32), jax.ShapeDtypeStruct((t_len, ssd_w), F32),
                   jax.ShapeDtypeStruct((n_chunks, SSD_STATE, ssd_w), F32)],
        in_specs=[sp["dtraw"], sp["small"], sp["small"], sp["dsk"], sp["xs"], sp["b"], sp["c"], sp["z"]],
        out_specs=[sp["tok"], sp["tok"], sp["state"]],
        scratch_shapes=[pltpu.VMEM((SSD_STATE, gw), F32)],
        compiler_params=_params("parallel", "arbitrary"),
    )(dtraw_g, bias_g, alog_g, dsk_exp, xbc, xbc, xbc, qkvz)


def _ssd_bwd(xbc, qkvz, dtraw_g, bias_g, alog_g, dsk_exp, ypre, states, dyg, heads, name):
    t_len = xbc.shape[0]
    q, gw = SSD_CHUNK, heads * HEAD_DIM
    n_chunks = t_len // q
    ssd_w = SSD_GROUPS * gw
    sp = _ssd_specs(t_len, heads, n_chunks, xbc.shape[1], rev=True)

    def body(dtraw_ref, bias_ref, alog_ref, dsk_ref, xs_ref, b_ref, c_ref, z_ref, ypre_ref, st_ref, dyg_ref,
             dxs_ref, db_ref, dc_ref, dz_ref, ddt_ref, small_ref, g_scr):
        first = pl.program_id(1) == 0

        @pl.when(first)
        def _():
            g_scr[...] = jnp.zeros_like(g_scr)

        k = _SsdChunk(dtraw_ref, bias_ref, alog_ref, xs_ref, b_ref, c_ref, heads)
        zv = z_ref[...]
        sig = jax.nn.sigmoid(zv)
        dyg = dyg_ref[...]
        y_pre = ypre_ref[...]
        dy = dyg * (zv * sig)
        dz_ref[...] = dyg * y_pre * (sig * (1.0 + zv * (1.0 - sig)))
        dsk = dsk_ref[...]
        g_next = g_scr[...]
        s_prev = st_ref[...]
        sb = s_prev.astype(BF16)
        xd = k.x * k.d_exp
        xdb = xd.astype(BF16)
        gb = g_next.astype(BF16)
        dx_off = k.d_exp * _dot(k.bb, gb, 1, 0)
        dyb = dy.astype(BF16)
        dcb = jnp.zeros((q, q), F32)
        lane = lax.broadcasted_iota(jnp.int32, (q, LANES), 1)
        row_t = lax.broadcasted_iota(jnp.int32, (LANES, q), 0)
        w_rows = jnp.zeros((q, LANES), F32)
        w_cols_t = jnp.zeros((LANES, q), F32)
        parts = []
        for p in range(heads // 2):
            cols = slice(p * LANES, (p + 1) * LANES)
            dyp, xp = dyb[:, cols], k.xb[:, cols]
            acc = jnp.zeros((q, LANES), F32)
            for hh in range(2):
                j = 2 * p + hh
                lm = k.decay(j)
                m32 = k.cbm * lm
                dym = jnp.where(k.pair_masks[hh], dyp, jnp.zeros_like(dyp))
                acc = acc + _dot(m32.astype(BF16), dym, 0, 0)
                dm = _dot(dym, xp, 1, 1)
                dcb = dcb + dm * lm
                wmat = dm * m32
                w_rows = w_rows + jnp.where(lane == j, jnp.sum(wmat, axis=1, keepdims=True), 0.0)
                w_cols_t = w_cols_t + jnp.where(row_t == j, jnp.sum(wmat, axis=0, keepdims=True), 0.0)
            parts.append(acc)
        dx = jnp.concatenate(parts, axis=1) + dx_off
        dcbb = dcb.astype(BF16)
        edy = (k.e_exp * dy).astype(BF16)
        dc_ref[...] = _dot(dcbb, k.bb, 1, 0) + _dot(edy, sb, 1, 1)
        db_ref[...] = _dot(dcbb, k.cb, 0, 0) + _dot(xdb, gb, 1, 1)
        g_scr[...] = k.elast_exp * g_next + _dot(k.cb, edy, 0, 0)

        y_off = k.e_exp * _dot(k.cb, sb, 1, 0)
        dcum = w_rows - w_cols_t.T + k.segsum(dy * y_off)
        t_term = k.segsum(k.x * dx_off)
        gs = jnp.broadcast_to(jnp.sum(g_next * s_prev, axis=0, keepdims=True), (8, gw))
        carried = k.segsum(gs)[0:1, :] * jnp.exp(k.cum[q - 1:q, :])
        dda = _rev_cumsum_rows(dcum) + (_cumsum_rows(t_term) - t_term) + carried
        ddt = jnp.where(k.head_lanes, dda * k.a + k.segsum(dx * k.xs), 0.0)
        ddtraw = ddt * jax.nn.sigmoid(k.dt_arg)
        ddt_ref[...] = ddtraw
        dxs_ref[...] = dx * k.dt_exp + dsk * dy
        ds = jnp.broadcast_to(jnp.sum(dy * k.xs, axis=0, keepdims=True), (8, gw))
        d_alog = jnp.sum(jnp.where(k.head_lanes, dda * k.dt, 0.0), axis=0, keepdims=True) * k.a
        rows8 = lax.broadcasted_iota(jnp.int32, (8, LANES), 0)
        small = jnp.where(rows8 == 0, d_alog, 0.0)
        small = small + jnp.where(rows8 == 1, jnp.sum(ddtraw, axis=0, keepdims=True), 0.0)
        small = small + jnp.where(rows8 == 2, k.segsum(ds)[0:1, :], 0.0)

        @pl.when(first)
        def _():
            small_ref[...] = small

        @pl.when(jnp.logical_not(first))
        def _():
            small_ref[...] += small

    bc_out = sp["bc"]
    return pl.pallas_call(
        body, name=name, grid=(SSD_GROUPS, n_chunks),
        out_shape=[jax.ShapeDtypeStruct((t_len, ssd_w), F32),
                   jax.ShapeDtypeStruct((t_len, SSD_GROUPS * SSD_STATE), F32),
                   jax.ShapeDtypeStruct((t_len, SSD_GROUPS * SSD_STATE), F32),
                   jax.ShapeDtypeStruct((t_len, ssd_w), F32),
                   jax.ShapeDtypeStruct((SSD_GROUPS, t_len, LANES), F32),
                   jax.ShapeDtypeStruct((SSD_GROUPS, 8, LANES), F32)],
        in_specs=[sp["dtraw"], sp["small"], sp["small"], sp["dsk"], sp["xs"], sp["b"], sp["c"], sp["z"],
                  sp["tok"], sp["state"], sp["tok"]],
        out_specs=[sp["tok"], bc_out, bc_out, sp["tok"], sp["dtraw"],
                   pl.BlockSpec((None, 8, LANES), lambda g, c: (g, 0, 0))],
        scratch_shapes=[pltpu.VMEM((SSD_STATE, gw), F32)],
        compiler_params=_params("parallel", "arbitrary"),
    )(dtraw_g, bias_g, alog_g, dsk_exp, xbc, xbc, xbc, qkvz, ypre, states, dyg)


def _adamw(w, g, m, v, name, rides=None):
    n_lead, rows, lanes = w.shape
    tr = _row_tile(rows, lanes, 4, 14)
    c1 = 1.0 / (1.0 - ADAM_B1 ** ADAM_STEP)
    c2 = 1.0 / (1.0 - ADAM_B2 ** ADAM_STEP)

    def body(w_ref, g_ref, m_ref, v_ref, d_ref, nm_ref, nv_ref):
        gv = g_ref[...]
        nm = ADAM_B1 * m_ref[...] + (1.0 - ADAM_B1) * gv
        nv = ADAM_B2 * v_ref[...] + (1.0 - ADAM_B2) * (gv * gv)
        nm_ref[...] = nm
        nv_ref[...] = nv
        d_ref[...] = -ADAM_LR * ((nm * c1) / (jnp.sqrt(nv * c2) + ADAM_EPS) + ADAM_WD * w_ref[...])

    spec = pl.BlockSpec((None, tr, lanes), lambda l, i: (l, i, 0))
    return _pallas(body, name=name, grid=(n_lead, rows // tr), out_shape=[jax.ShapeDtypeStruct(w.shape, F32)] * 3,
                   in_specs=[spec] * 4, out_specs=[spec] * 3, operands=[w, g, m, v],
                   semantics=("parallel", "parallel"), rides=rides)


def _pad_lanes(a, width=LANES):
    return jnp.pad(a, ((0, 0), (0, width - a.shape[1])))


def _group_pad(v, heads):
    return _pad_lanes(v.reshape(SSD_GROUPS, heads))[:, None, :]


def _layer_fwd(x0, p, wt, dims, tag, rides):
    w_attn, heads_g, n_heads, conv_ch = dims["w_attn"], dims["heads_g"], dims["n_heads"], dims["conv_ch"]
    h1 = _rmsnorm_fwd([x0], [[x0.shape[1]]], p["ln1_g"], f"ln1_fwd{tag}")
    proj = _mm(h1, wt("w_in"), name=f"in_proj{tag}", tn=1152, rides=rides)

    outs, lses = [], []
    for d in BRANCH_DILATIONS:
        o, l = _attn_branch_fwd(proj, w_attn, d, n_heads, f"attn_fwd_d{d}{tag}", rides)
        outs.append(o)
        lses.append(l)
    attn, lse = _attn_combine(outs, lses, f"attn_combine{tag}")

    xbc = _conv_fwd(proj, 4 * w_attn, p["conv_w"], p["conv_b"], f"conv_fwd{tag}")
    dt_col = 4 * w_attn + conv_ch
    dtraw_g = jnp.stack([_pad_lanes(proj[:, dt_col + g * heads_g:dt_col + (g + 1) * heads_g])
                         for g in range(SSD_GROUPS)])
    bias_g, alog_g = _group_pad(p["dt_bias"], heads_g), _group_pad(p["a_log"], heads_g)
    dsk_exp = jnp.repeat(p["d_skip"], HEAD_DIM).reshape(SSD_GROUPS, 1, heads_g * HEAD_DIM)
    yg, ypre, states = _ssd_fwd(xbc, proj, dtraw_g, bias_g, alog_g, dsk_exp, heads_g, f"ssd_fwd{tag}")

    gw = heads_g * HEAD_DIM
    mix_g = jnp.concatenate([p["attn_norm_g"], p["ssd_norm_g"]])[None, :]
    mix = _rmsnorm_fwd([attn, yg], [[w_attn], [gw] * SSD_GROUPS], mix_g, f"mix_norm_fwd{tag}")
    x1 = _mm(mix, wt("w_out"), name=f"out_proj{tag}", residual=x0, rides=rides)
    h2 = _rmsnorm_fwd([x1], [[x1.shape[1]]], p["ln2_g"], f"ln2_fwd{tag}")
    u = _mm(h2, wt("w_mlp_in"), name=f"mlp_in{tag}", out_dtype=BF16, tn=1024, rides=rides)
    x2 = _mm(u, wt("w_mlp_out"), name=f"mlp_out{tag}", a_act="relu2", residual=x1, tn=256, rides=rides)
    saved = dict(x0=x0, h1=h1, proj=proj, attn=attn, lse=lse, xbc=xbc, dtraw_g=dtraw_g,
                 bias_g=bias_g, alog_g=alog_g, dsk_exp=dsk_exp, yg=yg, ypre=ypre, states=states, mix=mix,
                 mix_g=mix_g, x1=x1, h2=h2, u=u)
    return x2, saved


def _pair_sums(ex, host, items):
    swapped = ex["rides"].done[("swap", host)]
    core = lax.axis_index("c").astype(jnp.int32).reshape(1)
    for i, (n, l) in enumerate(items):
        ex["pair"][(n, l)] = _pair_sum(ex["bufs"][(n, l)], swapped[i], core, f"pair_sum_{n}_l{l}")


def _layer_bwd(dx2, dx2_b, p, wt, s, dims, l, ex, copy_dx0):
    w_attn, heads_g, n_heads, conv_ch = dims["w_attn"], dims["heads_g"], dims["n_heads"], dims["conv_ch"]
    t_len, d_model = dx2.shape
    gw = heads_g * HEAD_DIM
    h_ssd = heads_g * SSD_GROUPS
    tag, rides, bufs = f"_l{l}", ex["rides"], ex["bufs"]
    du = _mm(dx2_b, wt("w_mlp_out"), name=f"mlp_out_dx{tag}", tb=True, gate=s["u"], out_dtype=BF16, tn=1024,
             rides=rides)
    d_wmo = _mm(s["u"], dx2_b, name=f"mlp_out_dw{tag}", ta=True, a_act="relu2", tm=512, tn=1024, out_dtype=BF16)
    bufs[("w_mlp_out", l)] = d_wmo.reshape(N_DEV, -1, d_model)
    bufs[("w_mlp_in", l)] = _mm(s["h2"], du, name=f"mlp_in_dw{tag}", ta=True, tn=1024, out_dtype=BF16,
                                out_chunk=du.shape[1] // N_DEV)
    dh2 = _mm(du, wt("w_mlp_in"), name=f"mlp_in_dx{tag}", tb=True, tn=256, rides=rides)
    _pair_sums(ex, f"mlp_in_dx{tag}", [("w_mlp_out", l), ("w_mlp_in", l)])
    (dx1,), d_ln2, (dx1_b,) = _rmsnorm_bwd([s["x1"]], [[d_model]], p["ln2_g"], dh2, [dx2], f"ln2_bwd{tag}",
                                           bf16_copy=True)
    dmix = _mm(dx1_b, wt("w_out"), name=f"out_proj_dx{tag}", tb=True)
    d_wo = _mm(s["mix"], dx1_b, name=f"out_proj_dw{tag}", ta=True, tn=1024, out_dtype=BF16)
    bufs[("w_out", l)] = d_wo.reshape(N_DEV, -1, d_model)
    after_branch = {BRANCH_DILATIONS[0]: [("w_out", l)]}
    (dattn, dyg), d_mix_g, _ = _rmsnorm_bwd([s["attn"], s["yg"]], [[w_attn], [gw] * SSD_GROUPS], s["mix_g"], dmix,
                                           [None, None], f"mix_norm_bwd{tag}")
    dxs, db, dc, dz, ddtraw_g, ssd_small = _ssd_bwd(
        s["xbc"], s["proj"], s["dtraw_g"], s["bias_g"], s["alog_g"], s["dsk_exp"], s["ypre"], s["states"], dyg,
        heads_g, f"ssd_bwd{tag}")
    dxbc = jnp.concatenate([dxs, db, dc], axis=1)
    dxbc_raw, d_conv_w, d_conv_b = _conv_bwd(s["proj"], 4 * w_attn, p["conv_w"], p["conv_b"], dxbc, f"conv_bwd{tag}")
    acc = None
    for d in BRANCH_DILATIONS:
        acc = _attn_branch_bwd(s["proj"], w_attn, s["attn"], s["lse"], dattn, d, n_heads, f"attn_bwd_d{d}{tag}", acc,
                               rides)
        if d in after_branch:
            _pair_sums(ex, f"attn_bwd_d{d}{tag}", after_branch[d])
    w_in = wt("w_in")
    in_proj = 4 * w_attn + conv_ch + h_ssd
    pad = jnp.zeros((t_len, w_in.shape[1] - in_proj), F32)
    dproj = jnp.concatenate([*acc, dz, dxbc_raw] + [ddtraw_g[g, :, :heads_g] for g in range(SSD_GROUPS)] + [pad],
                            axis=1).astype(BF16)
    d_win = _mm(s["h1"], dproj, name=f"in_proj_dw{tag}", ta=True, tm=512, tn=1152, out_dtype=BF16, rides=rides)
    bufs[("w_in", l)] = d_win[:, :in_proj].reshape(d_model, N_DEV, -1).transpose(1, 0, 2)
    dh1 = _mm(dproj, w_in, name=f"in_proj_dx{tag}", tb=True, rides=rides)
    _pair_sums(ex, f"in_proj_dx{tag}", [("w_in", l)])
    (dx0,), d_ln1, dx0_b = _rmsnorm_bwd([s["x0"]], [[d_model]], p["ln1_g"], dh1, [dx1], f"ln1_bwd{tag}",
                                        bf16_copy=copy_dx0)

    small = ssd_small[:, :, :heads_g]
    grads = dict(
        ln1_g=d_ln1[0], conv_w=d_conv_w, conv_b=d_conv_b[0],
        a_log=small[:, 0].reshape(h_ssd), dt_bias=small[:, 1].reshape(h_ssd), d_skip=small[:, 2].reshape(h_ssd),
        attn_norm_g=d_mix_g[0, :w_attn], ssd_norm_g=d_mix_g[0, w_attn:], ln2_g=d_ln2[0])
    return dx0, (dx0_b[0] if copy_dx0 else None), grads


_SMALL = ["ln1_g", "conv_w", "conv_b", "dt_bias", "a_log", "d_skip", "attn_norm_g", "ssd_norm_g", "ln2_g"]
_WEIGHTS = ["ln1_g", "w_in", "conv_w", "conv_b", "dt_bias", "a_log", "d_skip", "attn_norm_g", "ssd_norm_g",
            "w_out", "ln2_g", "w_mlp_in", "w_mlp_out", "final_norm_g"]


def _to_rows(a):
    flat = a.reshape(-1)
    rows = -(-flat.shape[0] // LANES)
    rows = -(-rows // 8) * 8
    return jnp.pad(flat, (0, rows * LANES - flat.shape[0])).reshape(rows, LANES)


def kernel(x, ln1_g, w_in, conv_w, conv_b, dt_bias, a_log, d_skip, attn_norm_g, ssd_norm_g, w_out, ln2_g, w_mlp_in, w_mlp_out, final_norm_g, loss_target, m_ln1_g, m_w_in, m_conv_w, m_conv_b, m_dt_bias, m_a_log, m_d_skip, m_attn_norm_g, m_ssd_norm_g, m_w_out, m_ln2_g, m_w_mlp_in, m_w_mlp_out, m_final_norm_g, v_ln1_g, v_w_in, v_conv_w, v_conv_b, v_dt_bias, v_a_log, v_d_skip, v_attn_norm_g, v_ssd_norm_g, v_w_out, v_ln2_g, v_w_mlp_in, v_w_mlp_out, v_final_norm_g):
    w = dict(ln1_g=ln1_g, w_in=w_in, conv_w=conv_w, conv_b=conv_b, dt_bias=dt_bias, a_log=a_log, d_skip=d_skip,
             attn_norm_g=attn_norm_g, ssd_norm_g=ssd_norm_g, w_out=w_out, ln2_g=ln2_g, w_mlp_in=w_mlp_in,
             w_mlp_out=w_mlp_out, final_norm_g=final_norm_g)
    mom = dict(ln1_g=m_ln1_g, w_in=m_w_in, conv_w=m_conv_w, conv_b=m_conv_b, dt_bias=m_dt_bias, a_log=m_a_log,
               d_skip=m_d_skip, attn_norm_g=m_attn_norm_g, ssd_norm_g=m_ssd_norm_g, w_out=m_w_out, ln2_g=m_ln2_g,
               w_mlp_in=m_w_mlp_in, w_mlp_out=m_w_mlp_out, final_norm_g=m_final_norm_g)
    var = dict(ln1_g=v_ln1_g, w_in=v_w_in, conv_w=v_conv_w, conv_b=v_conv_b, dt_bias=v_dt_bias, a_log=v_a_log,
               d_skip=v_d_skip, attn_norm_g=v_attn_norm_g, ssd_norm_g=v_ssd_norm_g, w_out=v_w_out, ln2_g=v_ln2_g,
               w_mlp_in=v_w_mlp_in, w_mlp_out=v_w_mlp_out, final_norm_g=v_final_norm_g)

    depth, d_model = ln1_g.shape
    t_len = x.shape[1]
    w_attn = attn_norm_g.shape[1]
    h_ssd = dt_bias.shape[1]
    conv_ch = conv_b.shape[1]
    in_proj = w_in.shape[2] * N_DEV
    assert ssd_norm_g.shape[1] == w_attn and in_proj == 4 * w_attn + conv_ch + h_ssd
    assert t_len % (BRANCH_DILATIONS[-1] * ATTN_BLOCK) == 0 and h_ssd % (2 * SSD_GROUPS) == 0
    dims = dict(w_attn=w_attn, heads_g=h_ssd // SSD_GROUPS, n_heads=w_attn // HEAD_DIM, conv_ch=conv_ch)
    names = ["w_in", "w_out", "w_mlp_in", "w_mlp_out"]

    rides = _Rides()
    ex = dict(rides=rides, bufs={}, pair={})
    latest, sent = {}, {}

    def shard(n, l):
        return w[n][l].astype(BF16)

    def half(rows, part):
        return None if part is None else (part * (rows // 2), rows // 2)

    def plan_spread(host, n, l, part=None):
        key, prev = ("spread", host, n, l, part), latest.get((n, l))
        rides.put(host, key, lambda: _GatherSpread([shard(n, l)], rows=half(w[n].shape[1], part),
                                                   into=[rides.done[prev[0]][prev[1]]] if prev else None))
        latest[(n, l)] = (key, 0)

    def plan_pass(host, items):
        key, srcs = ("pass", host), [latest[it] for it in items]
        rides.put(host, key, lambda: _GatherPass([rides.done[k][i] for k, i in srcs]))
        for i, it in enumerate(items):
            latest[it] = (key, i)

    def plan_swap(host, items):
        rides.put(host, ("swap", host), lambda: _SiblingSwap([ex["bufs"][it] for it in items]))

    def plan_send(host, n, l, part=None):
        key, prev = ("send", host, n, l, part), sent.get((n, l))
        rides.put(host, key, lambda: _ChipSend([ex["pair"][(n, l)]], rows=half(ex["pair"][(n, l)].shape[1], part),
                                               into=[rides.done[prev[0]][prev[1]]] if prev else None))
        sent[(n, l)] = (key, 0)

    d_first, d_mid, d_last = (f"d{d}" for d in BRANCH_DILATIONS)
    for l in range(depth):
        t = f"_l{l}"
        if l == 0:
            plan_spread(f"in_proj{t}", "w_out", 0)
            plan_spread(f"in_proj{t}", "w_mlp_in", 0, 0)
            plan_spread(f"attn_fwd_{d_first}{t}", "w_mlp_in", 0, 1)
            plan_spread(f"attn_fwd_{d_mid}{t}", "w_mlp_out", 0, 0)
            plan_pass(f"attn_fwd_{d_mid}{t}", [("w_out", 0), ("w_mlp_in", 0)])
            plan_spread(f"attn_fwd_{d_last}{t}", "w_mlp_out", 0, 1)
            plan_pass(f"out_proj{t}", [("w_mlp_out", 0)])
        else:
            plan_spread(f"in_proj{t}", "w_mlp_out", l, 0)
            plan_spread(f"attn_fwd_{d_first}{t}", "w_mlp_out", l, 1)
            plan_pass(f"attn_fwd_{d_mid}{t}", [("w_mlp_out", l)])
        if l + 1 < depth:
            plan_spread(f"out_proj{t}", "w_out", l + 1)
            plan_spread(f"mlp_in{t}", "w_in", l + 1)
            plan_spread(f"mlp_out{t}", "w_mlp_in", l + 1)
            plan_pass(f"pass_weights_l{l + 1}", [("w_out", l + 1), ("w_in", l + 1), ("w_mlp_in", l + 1)])
        plan_swap(f"mlp_in_dx{t}", [("w_mlp_out", l), ("w_mlp_in", l)])
        plan_send(f"attn_bwd_{d_first}{t}", "w_mlp_out", l, 0)
        plan_swap(f"attn_bwd_{d_first}{t}", [("w_out", l)])
        plan_send(f"attn_bwd_{d_mid}{t}", "w_mlp_out", l, 1)
        plan_send(f"attn_bwd_{d_last}{t}", "w_mlp_in", l)
        plan_send(f"in_proj_dw{t}", "w_out", l)
        plan_swap(f"in_proj_dx{t}", [("w_in", l)])
        if l > 0:
            plan_send(f"mlp_out_dx_l{l - 1}", "w_in", l)
        else:
            plan_send("adamw_w_mlp_in", "w_in", l, 0)
            plan_send("adamw_w_mlp_out", "w_in", l, 1)

    g_in0, g_cw = _all_gather([shard("w_in", 0), conv_w], "gather_first")
    full_cw = g_cw.transpose(1, 2, 0, 3).reshape(depth, SSD_CONV, conv_ch)
    proj_cols = -(-in_proj // LANES) * LANES
    full = {}

    def weight(n, l):
        if (n, l) not in full:
            if (n, l) == ("w_in", 0):
                g = g_in0
            else:
                key, i = latest[(n, l)]
                g = rides.done[key][i]
            if n == "w_in":
                g = _pad_lanes(g.transpose(1, 0, 2).reshape(d_model, in_proj), proj_cols)
            elif n == "w_mlp_in":
                g = g.transpose(1, 0, 2).reshape(d_model, -1)
            else:
                g = g.reshape(-1, d_model)
            full[(n, l)] = g
        return full[(n, l)]

    layers = [dict(ln1_g=ln1_g[l][None, :], ln2_g=ln2_g[l][None, :], conv_w=full_cw[l], conv_b=conv_b[l][None, :],
                   dt_bias=dt_bias[l], a_log=a_log[l], d_skip=d_skip[l], attn_norm_g=attn_norm_g[l],
                   ssd_norm_g=ssd_norm_g[l]) for l in range(depth)]

    h = x[0]
    saved = []
    for l in range(depth):
        h, s = _layer_fwd(h, layers[l], functools.partial(lambda n, l: weight(n, l), l=l), dims, f"_l{l}", rides)
        saved.append(s)
        if l + 1 < depth:
            _alone(rides, f"pass_weights_l{l + 1}")
    dh, d_final_g, loss_part, dh_b = _loss_head(h, final_norm_g[None, :], loss_target[0], "loss_head")

    grads = [None] * depth
    for l in reversed(range(depth)):
        dh, dh_b, grads[l] = _layer_bwd(dh, dh_b, layers[l], functools.partial(lambda n, l: weight(n, l), l=l),
                                        saved[l], dims, l, ex, copy_dx0=l > 0)
    grad_x = dh[None]

    my_chip = (2 * lax.axis_index("x") + lax.axis_index("y")).astype(jnp.int32).reshape(1)
    gsum, delta, new_m, new_v = {}, {}, {}, {}
    for n in names[1:] + names[:1]:
        per_layer = []
        for l in range(depth):
            key, i = sent[(n, l)]
            per_layer.append(_sum_with_own(rides.done[key][i], ex["pair"][(n, l)], my_chip, f"sum_{n}_l{l}"))
        gsum[n] = jnp.stack(per_layer)
        delta[n], new_m[n], new_v[n] = _adamw(w[n], gsum[n], mom[n], var[n], f"adamw_{n}", rides)

    small_parts = [jnp.stack([grads[l][n] for l in range(depth)]).reshape(-1) for n in _SMALL]
    small_parts += [d_final_g.reshape(-1), loss_part[0, :1]]
    sizes = [int(a.shape[0]) for a in small_parts]
    packed = _to_rows(jnp.concatenate(small_parts))
    (gathered,) = _all_gather([packed], "gather_small_grads")
    total = _sum_leading(gathered, "sum_small_grads").reshape(-1)
    offs = np.cumsum([0] + sizes)
    pieces = [total[offs[i]:offs[i + 1]] for i in range(len(sizes))]
    for n, piece in zip(_SMALL, pieces):
        shape = (depth, SSD_CONV, conv_ch) if n == "conv_w" else w[n].shape
        gsum[n] = piece.reshape(shape)
    gsum["final_norm_g"] = pieces[len(_SMALL)]
    loss = pieces[len(_SMALL) + 1][0]
    my_id = 4 * lax.axis_index("x") + 2 * lax.axis_index("y") + lax.axis_index("c")
    cw = conv_w.shape[2]
    gsum["conv_w"] = lax.dynamic_slice_in_dim(gsum["conv_w"], my_id * cw, cw, axis=2)

    small_names = [n for n in _WEIGHTS if n not in names]
    sm_sizes = [int(np.prod(w[n].shape)) for n in small_names]
    pack = lambda d: _to_rows(jnp.concatenate([d[n].reshape(-1) for n in small_names]))[None]
    outs = _adamw(pack(w), pack(gsum), pack(mom), pack(var), "adamw_small")
    sm_offs = np.cumsum([0] + sm_sizes)
    for res, o in zip((delta, new_m, new_v), outs):
        flat = o.reshape(-1)
        for i, n in enumerate(small_names):
            res[n] = flat[sm_offs[i]:sm_offs[i + 1]].reshape(w[n].shape)

    return (loss, grad_x, *[gsum[n] for n in _WEIGHTS], *[delta[n] for n in _WEIGHTS],
            *[new_m[n] for n in _WEIGHTS], *[new_v[n] for n in _WEIGHTS])
```

```python
import functools
import math

import numpy as np
import jax
import jax.numpy as jnp
from jax import lax
from jax.experimental import pallas as pl
from jax.experimental.pallas import tpu as pltpu

F32 = jnp.float32
BF16 = jnp.bfloat16

N_DEV = 8
LANES = 128
HEAD_DIM = 64
ATTN_BLOCK = 128
BRANCH_DILATIONS = (1, 4, 16)
SSD_GROUPS = 2
SSD_STATE = 128
SSD_CHUNK = 128
SSD_CONV = 4
NORM_EPS = 1e-5
ADAM_LR, ADAM_B1, ADAM_B2, ADAM_EPS, ADAM_WD, ADAM_STEP = 0.001, 0.9, 0.999, 1e-08, 0.01, 10
VMEM_LIMIT_BYTES = 56 * 1024 * 1024
MESH = pl.DeviceIdType.MESH
NEG_INF = float("-inf")


def _params(*sem):
    return pltpu.CompilerParams(dimension_semantics=tuple(sem), vmem_limit_bytes=VMEM_LIMIT_BYTES)


def _pick(n, target, mult):
    best = None
    for t in range(mult, min(n, target) + 1, mult):
        if n % t == 0:
            best = t
    assert best is not None, (n, target, mult)
    return best


def _dot(a, b, ca, cb):
    return lax.dot_general(a, b, (((ca,), (cb,)), ((), ())), preferred_element_type=F32)


def _split3(v):
    hi = v.astype(BF16)
    r = v - hi.astype(F32)
    mid = r.astype(BF16)
    lo = (r - mid.astype(F32)).astype(BF16)
    return hi, mid, lo


def _dot_exact(v, sel, ca, cb):
    hi, mid, lo = _split3(v)
    return _dot(hi, sel, ca, cb) + _dot(mid, sel, ca, cb) + _dot(lo, sel, ca, cb)


_HBM = pl.BlockSpec(memory_space=pltpu.HBM)


def _all_gather(xs, name):
    n = len(xs)

    def body(*refs):
        x_refs, o_refs = refs[:n], refs[n:2 * n]
        send_sems, recv_sems = refs[2 * n:]
        x, y, c = lax.axis_index("x"), lax.axis_index("y"), lax.axis_index("c")
        me, sibling = (x, y, c), (x, y, 1 - c)
        chips = [(1 - x, y), (x, 1 - y), (1 - x, 1 - y)]

        def copy(t, k, block, to, src=None):
            bx, by, bc = block
            dst = o_refs[t].at[4 * bx + 2 * by + bc]
            return pltpu.make_async_remote_copy(
                src_ref=dst if src is None else src, dst_ref=dst,
                send_sem=send_sems.at[t, k], recv_sem=recv_sems.at[t, k],
                device_id=to, device_id_type=MESH)

        first, passed = [], []
        for t in range(n):
            cps = [copy(t, 0, me, sibling, src=x_refs[t])]
            cps += [copy(t, 1 + j, me, (*chip, c), src=x_refs[t]) for j, chip in enumerate(chips)]
            for cp in cps:
                cp.start()
            first += cps
        for t in range(n):
            for j, chip in enumerate(chips):
                copy(t, 1 + j, (*chip, c), me).wait_recv()
                fwd = copy(t, 4 + j, (*chip, c), sibling)
                fwd.start()
                passed.append(fwd)
        for t in range(n):
            copy(t, 0, sibling, me).wait_recv()
            back = copy(t, 7, sibling, sibling)
            back.start()
            passed.append(back)
        for t in range(n):
            copy(t, 7, me, me).wait_recv()
            for j, chip in enumerate(chips):
                copy(t, 4 + j, (*chip, 1 - c), me).wait_recv()
        for cp in first + passed:
            cp.wait_send()

    return pl.pallas_call(
        body, name=name,
        out_shape=[jax.ShapeDtypeStruct((N_DEV,) + a.shape, a.dtype) for a in xs],
        in_specs=[_HBM] * n, out_specs=[_HBM] * n,
        scratch_shapes=[pltpu.SemaphoreType.DMA((n, 8)), pltpu.SemaphoreType.DMA((n, 8))],
    )(*xs)


def _place():
    x, y, c = lax.axis_index("x"), lax.axis_index("y"), lax.axis_index("c")
    return x, y, c, 4 * x + 2 * y + c, (x, y, 1 - c), [(1 - x, y), (x, 1 - y), (1 - x, 1 - y)]


def _remote(src, dst, send_sem, recv_sem, to):
    return pltpu.make_async_remote_copy(src_ref=src, dst_ref=dst, send_sem=send_sem, recv_sem=recv_sem,
                                        device_id=to, device_id_type=MESH)


class _Riding:
    aliases = {}

    def copies(self, ins, outs, sems):
        raise NotImplementedError

    def start(self, ins, outs, sems):
        local, out, _ = self.copies(ins, outs, sems)
        for cp in local + out:
            cp.start()

    def wait(self, ins, outs, sems):
        local, out, landing = self.copies(ins, outs, sems)
        for cp in landing:
            cp.wait_recv()
        for cp in out:
            cp.wait_send()
        for cp in local:
            cp.wait()


def _rows_of(ref, rows):
    return ref if rows is None else ref.at[pl.ds(rows[0], rows[1])]


class _GatherSpread(_Riding):
    def __init__(self, xs, rows=None, into=None):
        n = len(xs)
        self.rows = rows
        self.ins = list(xs) + list(into or [])
        self.out_shapes = [jax.ShapeDtypeStruct((N_DEV,) + a.shape, a.dtype) for a in xs]
        self.aliases = {n + t: t for t in range(n)} if into else {}
        self.sem_shapes = [pltpu.SemaphoreType.DMA((n, 4)), pltpu.SemaphoreType.DMA((n, 4))]

    def copies(self, ins, outs, sems):
        send, recv = sems
        _, _, c, me, sibling, chips = _place()
        targets = [sibling] + [(*chip, c) for chip in chips]
        out, landing = [], []
        for t in range(len(outs)):
            src = _rows_of(ins[t], self.rows)
            for k, to in enumerate(targets):
                out.append(_remote(src, _rows_of(outs[t].at[me], self.rows), send.at[t, k], recv.at[t, k], to))
                theirs = _rows_of(outs[t].at[4 * to[0] + 2 * to[1] + to[2]], self.rows)
                landing.append(_remote(src, theirs, send.at[t, k], recv.at[t, k], to))
        return [], out, landing


class _GatherPass(_Riding):
    def __init__(self, bufs):
        n = len(bufs)
        self.ins = list(bufs)
        self.out_shapes = [jax.ShapeDtypeStruct(b.shape, b.dtype) for b in bufs]
        self.aliases = {t: t for t in range(n)}
        self.sem_shapes = [pltpu.SemaphoreType.DMA((n, 4)), pltpu.SemaphoreType.DMA((n, 4))]

    def copies(self, ins, outs, sems):
        send, recv = sems
        x, y, c, me, sibling, chips = _place()
        out, landing = [], []
        for t in range(len(outs)):
            for j, (px, py) in enumerate(chips + [(x, y)]):
                held = outs[t].at[4 * px + 2 * py + c] if j < 3 else outs[t].at[4 * x + 2 * y + 1 - c]
                lands = outs[t].at[4 * px + 2 * py + 1 - c] if j < 3 else outs[t].at[me]
                out.append(_remote(held, held, send.at[t, j], recv.at[t, j], sibling))
                landing.append(_remote(held, lands, send.at[t, j], recv.at[t, j], sibling))
        return [], out, landing


class _SiblingSwap(_Riding):
    def __init__(self, xs):
        n = len(xs)
        self.ins = list(xs)
        self.out_shapes = [jax.ShapeDtypeStruct((N_DEV // 2,) + a.shape[1:], a.dtype) for a in xs]
        self.sem_shapes = [pltpu.SemaphoreType.DMA((n, 4)), pltpu.SemaphoreType.DMA((n, 4))]

    def copies(self, ins, outs, sems):
        send, recv = sems
        _, _, c, _, sibling, _ = _place()
        out = [_remote(ins[t].at[2 * q + 1 - c], outs[t].at[q], send.at[t, q], recv.at[t, q], sibling)
               for t in range(len(ins)) for q in range(N_DEV // 2)]
        return [], out, out


class _ChipSend(_Riding):
    def __init__(self, ps, rows=None, into=None):
        n = len(ps)
        self.rows = rows
        self.ins = list(ps) + list(into or [])
        self.out_shapes = [jax.ShapeDtypeStruct((3,) + a.shape[1:], a.dtype) for a in ps]
        self.aliases = {n + t: t for t in range(n)} if into else {}
        self.sem_shapes = [pltpu.SemaphoreType.DMA((n, 3)), pltpu.SemaphoreType.DMA((n, 3))]

    def copies(self, ins, outs, sems):
        send, recv = sems
        _, _, c, _, _, chips = _place()
        out = [_remote(_rows_of(ins[t].at[2 * px + py], self.rows), _rows_of(outs[t].at[j], self.rows),
                       send.at[t, j], recv.at[t, j], (px, py, c))
               for t in range(len(outs)) for j, (px, py) in enumerate(chips)]
        return [], out, out


class _Bundle(_Riding):
    def __init__(self, comms):
        self.comms = comms
        self.ins = [a for cm in comms for a in cm.ins]
        self.out_shapes = [s for cm in comms for s in cm.out_shapes]
        self.sem_shapes = [s for cm in comms for s in cm.sem_shapes]
        self.aliases = {}
        i0 = o0 = 0
        for cm in comms:
            self.aliases.update({i0 + i: o0 + j for i, j in cm.aliases.items()})
            i0, o0 = i0 + len(cm.ins), o0 + len(cm.out_shapes)

    def copies(self, ins, outs, sems):
        local, out, landing = [], [], []
        i0 = o0 = s0 = 0
        for cm in self.comms:
            i1, o1, s1 = i0 + len(cm.ins), o0 + len(cm.out_shapes), s0 + len(cm.sem_shapes)
            a, b, c = cm.copies(ins[i0:i1], outs[o0:o1], sems[s0:s1])
            local, out, landing = local + a, out + b, landing + c
            i0, o0, s0 = i1, o1, s1
        return local, out, landing


class _Rides:
    def __init__(self):
        self.plan, self.done, self.aboard = {}, {}, {}

    def put(self, host, key, make):
        self.plan.setdefault(host, []).append((key, make))

    def board(self, host):
        if host not in self.plan:
            return None
        self.aboard[host] = [make() for _, make in self.plan[host]]
        return _Bundle(self.aboard[host])

    def land(self, host, results):
        o0 = 0
        for (key, _), cm in zip(self.plan[host], self.aboard[host]):
            self.done[key] = list(results[o0:o0 + len(cm.out_shapes)])
            o0 += len(cm.out_shapes)


def _pallas(body, *, name, grid, out_shape, in_specs, out_specs, operands, semantics, scratch_shapes=(), rides=None):
    comm = rides.board(name) if rides is not None else None
    if comm is None:
        return pl.pallas_call(
            body, name=name, grid=grid, out_shape=list(out_shape), in_specs=list(in_specs),
            out_specs=list(out_specs), scratch_shapes=list(scratch_shapes), compiler_params=_params(*semantics),
        )(*operands)
    n_in, n_out, n_scr = len(in_specs), len(out_shape), len(scratch_shapes)
    n_ci, n_co = len(comm.ins), len(comm.out_shapes)

    def hosted(*refs):
        cuts = np.cumsum([0, n_in, n_ci, n_out, n_co, n_scr])
        ins, c_ins, outs, c_outs, scr = (refs[cuts[i]:cuts[i + 1]] for i in range(5))
        sems = refs[cuts[5]:]
        ids = [pl.program_id(a) for a in range(len(grid))]
        first = functools.reduce(jnp.logical_and, [i == 0 for i in ids])
        last = functools.reduce(jnp.logical_and, [i == g - 1 for i, g in zip(ids, grid)])

        @pl.when(first)
        def _():
            comm.start(c_ins, c_outs, sems)

        body(*ins, *outs, *scr)

        @pl.when(last)
        def _():
            comm.wait(c_ins, c_outs, sems)

    results = pl.pallas_call(
        hosted, name=name, grid=grid, out_shape=list(out_shape) + comm.out_shapes,
        in_specs=list(in_specs) + [_HBM] * n_ci, out_specs=list(out_specs) + [_HBM] * n_co,
        scratch_shapes=list(scratch_shapes) + comm.sem_shapes,
        input_output_aliases={n_in + i: n_out + j for i, j in comm.aliases.items()},
        compiler_params=_params(*["arbitrary"] * len(grid)),
    )(*operands, *comm.ins)
    rides.land(name, results[n_out:])
    return results[:n_out]


def _alone(rides, name):
    comm = rides.board(name)

    def body(*refs):
        n_ci, n_co = len(comm.ins), len(comm.out_shapes)
        ins, outs, sems = refs[:n_ci], refs[n_ci:n_ci + n_co], refs[n_ci + n_co:]
        comm.start(ins, outs, sems)
        comm.wait(ins, outs, sems)

    results = pl.pallas_call(
        body, name=name, out_shape=comm.out_shapes, in_specs=[_HBM] * len(comm.ins),
        out_specs=[_HBM] * len(comm.out_shapes), scratch_shapes=comm.sem_shapes,
        input_output_aliases=dict(comm.aliases),
    )(*comm.ins)
    rides.land(name, results)


def _row_tile(rows, cols, itemsize, copies, budget=24 * 1024 * 1024):
    padded = -(-cols // LANES) * LANES
    mult = 8 * (4 // itemsize)
    if rows % mult:
        return rows
    return _pick(rows, max(mult, budget // (copies * padded * itemsize)), mult)


def _sum_leading(x, name):
    n_src, rows, cols = x.shape
    tr = _row_tile(rows, cols, 4, 2 * (n_src + 2))

    def body(x_ref, o_ref):
        acc = x_ref[0].astype(F32)
        for s in range(1, n_src):
            acc = acc + x_ref[s].astype(F32)
        o_ref[...] = acc

    return pl.pallas_call(
        body, name=name, grid=(rows // tr,), out_shape=jax.ShapeDtypeStruct((rows, cols), F32),
        in_specs=[pl.BlockSpec((n_src, tr, cols), lambda i: (0, i, 0))],
        out_specs=pl.BlockSpec((tr, cols), lambda i: (i, 0)), compiler_params=_params("parallel"),
    )(x)


def _pair_sum(buf, theirs, core, name):
    n_q, rows, cols = theirs.shape
    tr = _row_tile(rows, cols, 4, 6)

    def body(core_ref, mine_ref, theirs_ref, o_ref):
        o_ref[...] = (mine_ref[...].astype(F32) + theirs_ref[...].astype(F32)).astype(BF16)

    spec = pl.BlockSpec((None, tr, cols), lambda q, i, core_ref: (q, i, 0))
    return pl.pallas_call(
        body, name=name, out_shape=jax.ShapeDtypeStruct(theirs.shape, BF16),
        grid_spec=pltpu.PrefetchScalarGridSpec(
            num_scalar_prefetch=1, grid=(n_q, rows // tr),
            in_specs=[pl.BlockSpec((None, tr, cols), lambda q, i, core_ref: (2 * q + core_ref[0], i, 0)), spec],
            out_specs=spec),
        compiler_params=_params("parallel", "parallel"),
    )(core, buf, theirs)


def _sum_with_own(recv, pair, chip, name):
    n_src, rows, cols = recv.shape
    tr = _row_tile(rows, cols, 4, 2 * (n_src + 3))

    def body(chip_ref, own_ref, recv_ref, o_ref):
        acc = own_ref[...].astype(F32)
        for s in range(n_src):
            acc = acc + recv_ref[s].astype(F32)
        o_ref[...] = acc

    return pl.pallas_call(
        body, name=name, out_shape=jax.ShapeDtypeStruct((rows, cols), F32),
        grid_spec=pltpu.PrefetchScalarGridSpec(
            num_scalar_prefetch=1, grid=(rows // tr,),
            in_specs=[pl.BlockSpec((None, tr, cols), lambda i, chip_ref: (chip_ref[0], i, 0)),
                      pl.BlockSpec((n_src, tr, cols), lambda i, chip_ref: (0, i, 0))],
            out_specs=pl.BlockSpec((tr, cols), lambda i, chip_ref: (i, 0))),
        compiler_params=_params("parallel"),
    )(chip, pair, recv)


def _mm(a, b, *, name, ta=False, tb=False, tm=1024, tn=512, out_dtype=F32, a_act=None,
        residual=None, gate=None, out_chunk=None, rides=None):
    k_dim, m = (a.shape if ta else a.shape[::-1])
    n, kb = (b.shape if tb else b.shape[::-1])
    assert kb == k_dim, (a.shape, b.shape, ta, tb)
    tm, tn = _pick(m, tm, 128), _pick(out_chunk or n, tn, 128)
    ca, cb = (0 if ta else 1), (1 if tb else 0)
    a_spec = pl.BlockSpec((k_dim, tm), lambda i, j: (0, i)) if ta else pl.BlockSpec((tm, k_dim), lambda i, j: (i, 0))
    b_spec = pl.BlockSpec((tn, k_dim), lambda i, j: (j, 0)) if tb else pl.BlockSpec((k_dim, tn), lambda i, j: (0, j))
    mn_spec = pl.BlockSpec((tm, tn), lambda i, j: (i, j))
    if out_chunk:
        per = out_chunk // tn
        o_spec = pl.BlockSpec((None, tm, tn), lambda i, j: (j // per, i, j % per))
        out_shape = jax.ShapeDtypeStruct((n // out_chunk, m, out_chunk), out_dtype)
    else:
        o_spec = mn_spec
        out_shape = jax.ShapeDtypeStruct((m, n), out_dtype)
    operands, in_specs = [a, b], [a_spec, b_spec]
    for extra in (gate, residual):
        if extra is not None:
            operands.append(extra)
            in_specs.append(mn_spec)

    def body(*refs):
        a_ref, b_ref, o_ref = refs[0], refs[1], refs[-1]
        extras = list(refs[2:-1])
        gate_ref = extras.pop(0) if gate is not None else None
        res_ref = extras.pop(0) if residual is not None else None
        av = a_ref[...].astype(BF16)
        if a_act == "relu2":
            av = jnp.square(jnp.maximum(av, jnp.zeros_like(av)))
        r = _dot(av, b_ref[...].astype(BF16), ca, cb)
        if gate_ref is not None:
            r = r * (2.0 * jnp.maximum(gate_ref[...].astype(F32), 0.0))
        if res_ref is not None:
            r = r + res_ref[...].astype(F32)
        o_ref[...] = r.astype(out_dtype)

    return _pallas(body, name=name, grid=(m // tm, n // tn), out_shape=[out_shape], in_specs=in_specs,
                   out_specs=[o_spec], operands=operands, semantics=("parallel", "arbitrary"), rides=rides)[0]


def _rmsnorm_fwd(xs, seg_widths, g, name, tm=256):
    t_len = xs[0].shape[0]
    width = sum(x.shape[1] for x in xs)
    tm = _pick(t_len, tm, 16)
    n = len(xs)

    def body(*refs):
        x_refs, g_ref, o_ref = refs[:n], refs[n], refs[n + 1]
        col = 0
        for x_ref, widths in zip(x_refs, seg_widths):
            off = 0
            for w in widths:
                xv = x_ref[:, off:off + w].astype(F32)
                r = lax.rsqrt(jnp.mean(xv * xv, axis=1, keepdims=True) + NORM_EPS)
                o_ref[:, col:col + w] = (xv * r * g_ref[:, col:col + w]).astype(BF16)
                off += w
                col += w

    return pl.pallas_call(
        body, name=name, grid=(t_len // tm,),
        out_shape=jax.ShapeDtypeStruct((t_len, width), BF16),
        in_specs=[pl.BlockSpec((tm, x.shape[1]), lambda i: (i, 0)) for x in xs]
        + [pl.BlockSpec((1, width), lambda i: (0, 0))],
        out_specs=pl.BlockSpec((tm, width), lambda i: (i, 0)),
        compiler_params=_params("parallel"),
    )(*xs, g)


def _rmsnorm_bwd(xs, seg_widths, g, dh, residuals, name, tm=256, bf16_copy=False):
    t_len = xs[0].shape[0]
    width = sum(x.shape[1] for x in xs)
    tm = _pick(t_len, tm, 8)
    n = len(xs)
    has_res = [r is not None for r in residuals]
    res_ops = [r for r in residuals if r is not None]

    def body(*refs):
        x_refs, g_ref, dh_ref = refs[:n], refs[n], refs[n + 1]
        res_refs = list(refs[n + 2:n + 2 + len(res_ops)])
        dx_refs = refs[n + 2 + len(res_ops):n + 2 + len(res_ops) + n]
        dg_ref = refs[n + 2 + len(res_ops) + n]
        copy_refs = refs[n + 3 + len(res_ops) + n:]
        first = pl.program_id(0) == 0
        col = 0
        for idx, (x_ref, widths) in enumerate(zip(x_refs, seg_widths)):
            res_ref = res_refs.pop(0) if has_res[idx] else None
            off = 0
            for w in widths:
                xv = x_ref[:, off:off + w].astype(F32)
                r = lax.rsqrt(jnp.mean(xv * xv, axis=1, keepdims=True) + NORM_EPS)
                xh = xv * r
                dhv = dh_ref[:, col:col + w].astype(F32)
                gd = dhv * g_ref[:, col:col + w]
                dx = r * (gd - xh * jnp.mean(gd * xh, axis=1, keepdims=True))
                if res_ref is not None:
                    dx = dx + res_ref[:, off:off + w]
                dx_refs[idx][:, off:off + w] = dx
                if bf16_copy:
                    copy_refs[idx][:, off:off + w] = dx.astype(BF16)
                part = jnp.sum(dhv * xh, axis=0, keepdims=True)

                @pl.when(first)
                def _(part=part, col=col, w=w):
                    dg_ref[:, col:col + w] = part

                @pl.when(jnp.logical_not(first))
                def _(part=part, col=col, w=w):
                    dg_ref[:, col:col + w] += part
                off += w
                col += w

    outs = pl.pallas_call(
        body, name=name, grid=(t_len // tm,),
        out_shape=[jax.ShapeDtypeStruct(x.shape, F32) for x in xs] + [jax.ShapeDtypeStruct((1, width), F32)]
        + ([jax.ShapeDtypeStruct(x.shape, BF16) for x in xs] if bf16_copy else []),
        in_specs=[pl.BlockSpec((tm, x.shape[1]), lambda i: (i, 0)) for x in xs]
        + [pl.BlockSpec((1, width), lambda i: (0, 0)), pl.BlockSpec((tm, width), lambda i: (i, 0))]
        + [pl.BlockSpec((tm, r.shape[1]), lambda i: (i, 0)) for r in res_ops],
        out_specs=[pl.BlockSpec((tm, x.shape[1]), lambda i: (i, 0)) for x in xs]
        + [pl.BlockSpec((1, width), lambda i: (0, 0))]
        + ([pl.BlockSpec((tm, x.shape[1]), lambda i: (i, 0)) for x in xs] if bf16_copy else []),
        compiler_params=_params("arbitrary"),
    )(*xs, g, dh, *res_ops)
    return outs[:n], outs[n], outs[n + 1:]


def _loss_head(x, g, target, name, tm=256):
    t_len, d = x.shape
    tm = _pick(t_len, tm, 8)

    def body(x_ref, g_ref, t_ref, dx_ref, dg_ref, loss_ref, dxb_ref):
        first = pl.program_id(0) == 0
        xv = x_ref[...]
        r = lax.rsqrt(jnp.mean(xv * xv, axis=1, keepdims=True) + NORM_EPS)
        xh = xv * r
        gv = g_ref[...]
        err = xh * gv - t_ref[...]
        part_loss = 0.5 * jnp.sum(jnp.mean(err * err, axis=1, keepdims=True), axis=0, keepdims=True)
        dy = err * (1.0 / d)
        gd = dy * gv
        dx = r * (gd - xh * jnp.mean(gd * xh, axis=1, keepdims=True))
        dx_ref[...] = dx
        dxb_ref[...] = dx.astype(BF16)
        part_g = jnp.sum(dy * xh, axis=0, keepdims=True)
        part_loss = jnp.broadcast_to(part_loss, (1, LANES))

        @pl.when(first)
        def _():
            dg_ref[...] = part_g
            loss_ref[...] = part_loss

        @pl.when(jnp.logical_not(first))
        def _():
            dg_ref[...] += part_g
            loss_ref[...] += part_loss

    return pl.pallas_call(
        body, name=name, grid=(t_len // tm,),
        out_shape=[jax.ShapeDtypeStruct((t_len, d), F32), jax.ShapeDtypeStruct((1, d), F32),
                   jax.ShapeDtypeStruct((1, LANES), F32), jax.ShapeDtypeStruct((t_len, d), BF16)],
        in_specs=[pl.BlockSpec((tm, d), lambda i: (i, 0)), pl.BlockSpec((1, d), lambda i: (0, 0)),
                  pl.BlockSpec((tm, d), lambda i: (i, 0))],
        out_specs=[pl.BlockSpec((tm, d), lambda i: (i, 0)), pl.BlockSpec((1, d), lambda i: (0, 0)),
                   pl.BlockSpec((1, LANES), lambda i: (0, 0)), pl.BlockSpec((tm, d), lambda i: (i, 0))],
        compiler_params=_params("arbitrary"),
    )(x, g, target)


def _alibi_slope(h, n_heads):
    return jnp.exp(jnp.full((1, 1), -8.0 * math.log(2.0) / n_heads, F32) * (h + 1).astype(F32))


def _attn_tiles(d, w):
    return ATTN_BLOCK * d, (w if d == 1 else LANES)


def _residue_rows(r, d):
    return pl.ds(r, ATTN_BLOCK, stride=d) if d > 1 else pl.ds(0, ATTN_BLOCK)


def _attn_masks(first_block):
    i = lax.broadcasted_iota(jnp.int32, (2 * ATTN_BLOCK, 2 * ATTN_BLOCK), 0) % ATTN_BLOCK
    j = lax.broadcasted_iota(jnp.int32, (2 * ATTN_BLOCK, 2 * ATTN_BLOCK), 1)
    delta = i - j + ATTN_BLOCK
    valid = jnp.logical_and(delta >= 0, delta <= ATTN_BLOCK)
    valid = jnp.logical_and(valid, jnp.logical_or(j >= ATTN_BLOCK, jnp.logical_not(first_block)))
    return valid, delta.astype(F32)


def _stack_heads(x, masks):
    zero = jnp.zeros_like(x)
    return jnp.concatenate([jnp.where(masks[0], x, zero), jnp.where(masks[1], x, zero)], axis=0)


def _unstack_heads(x2, masks):
    return jnp.where(masks[0], x2[:ATTN_BLOCK], x2[ATTN_BLOCK:])


def _pair_slopes(first_head, p, n_heads, d):
    row = lax.broadcasted_iota(jnp.int32, (2 * ATTN_BLOCK, 1), 0)
    sa, sb = (_alibi_slope(first_head + 2 * p + hh, n_heads) * d for hh in range(2))
    return jnp.where(row < ATTN_BLOCK, sa, sb)


def _head_lane_masks():
    lane = lax.broadcasted_iota(jnp.int32, (ATTN_BLOCK, LANES), 1)
    return [lane < HEAD_DIM, lane >= HEAD_DIM]


def _attn_branch_fwd(proj, w, dilation, n_heads, name, rides=None):
    t_len = proj.shape[0]
    d = dilation
    rows, lw = _attn_tiles(d, w)
    nb = t_len // rows
    n_pairs = lw // LANES
    per = w // lw
    scale = HEAD_DIM ** -0.5

    def body(q_ref, kp_ref, kc_ref, vp_ref, vc_ref, o_ref, lse_ref):
        first_head = pl.program_id(0) * (2 * n_pairs)
        first_block = pl.program_id(1) == 0
        valid, delta = _attn_masks(first_block)
        masks = _head_lane_masks()
        ones = jnp.ones((2 * ATTN_BLOCK, LANES), BF16)
        for p in range(n_pairs):
            cols = pl.ds(p * LANES, LANES)
            bias = _pair_slopes(first_head, p, n_heads, d) * delta
            for r in range(d):
                rs = _residue_rows(r, d)
                q2 = _stack_heads((q_ref[rs, cols] * scale).astype(BF16), masks)
                k2 = jnp.concatenate([kp_ref[rs, cols], kc_ref[rs, cols]], axis=0).astype(BF16)
                v2 = jnp.concatenate([vp_ref[rs, cols], vc_ref[rs, cols]], axis=0).astype(BF16)
                s = jnp.where(valid, _dot(q2, k2, 1, 1) - bias, NEG_INF)
                m = jnp.max(s, axis=1, keepdims=True)
                pr = jnp.exp(s - m).astype(BF16)
                den = _dot(pr, ones, 1, 0)
                o_ref[rs, cols] = _unstack_heads(_dot(pr, v2, 1, 0) / den, masks)
                lse_ref[rs, cols] = _unstack_heads(m + jnp.log(den), masks)

    def spec(which, prev):
        if prev:
            return pl.BlockSpec((rows, lw), lambda b, n: (jnp.maximum(n - 1, 0), which * per + b))
        return pl.BlockSpec((rows, lw), lambda b, n: (n, which * per + b))

    o_spec = pl.BlockSpec((rows, lw), lambda b, n: (n, b))
    return _pallas(
        body, name=name, grid=(per, nb), out_shape=[jax.ShapeDtypeStruct((t_len, w), F32)] * 2,
        in_specs=[spec(0, False), spec(1, True), spec(1, False), spec(2, True), spec(2, False)],
        out_specs=[o_spec, o_spec], operands=[proj] * 5, semantics=("parallel", "parallel"), rides=rides)


def _attn_combine(outs, lses, name, tm=512):
    t_len, w = outs[0].shape
    tm = _pick(t_len, tm, 8)
    nbr = len(outs)

    def body(*refs):
        o_refs, l_refs = refs[:nbr], refs[nbr:2 * nbr]
        out_ref, lse_ref = refs[2 * nbr:]
        ls = [r[...] for r in l_refs]
        m = functools.reduce(jnp.maximum, ls)
        es = [jnp.exp(l - m) for l in ls]
        den = functools.reduce(lambda a, b: a + b, es)
        num = functools.reduce(lambda a, b: a + b, [e * r[...] for e, r in zip(es, o_refs)])
        out_ref[...] = num / den
        lse_ref[...] = m + jnp.log(den)

    spec = pl.BlockSpec((tm, w), lambda i: (i, 0))
    return pl.pallas_call(
        body, name=name, grid=(t_len // tm,),
        out_shape=[jax.ShapeDtypeStruct((t_len, w), F32)] * 2,
        in_specs=[spec] * (2 * nbr), out_specs=[spec, spec],
        compiler_params=_params("parallel"),
    )(*outs, *lses)


def _attn_branch_bwd(proj, w, out, lse, dout, dilation, n_heads, name, acc=None, rides=None):
    t_len = proj.shape[0]
    d = dilation
    rows, lw = _attn_tiles(d, w)
    nb = t_len // rows
    n_pairs = lw // LANES
    per = w // lw
    scale = HEAD_DIM ** -0.5
    n_acc = 0 if acc is None else 3

    def body(*refs):
        q_ref, kp_ref, kc_ref, vp_ref, vc_ref, out_ref, lse_ref, do_ref = refs[:8]
        acc_refs = refs[8:8 + n_acc]
        dq_ref, dk_ref, dv_ref, dk_carry, dv_carry = refs[8 + n_acc:]
        first_head = pl.program_id(0) * (2 * n_pairs)
        n = pl.program_id(1)
        first_block = n == 0
        valid, dist = _attn_masks(first_block)
        masks = _head_lane_masks()

        def plus(value, idx, *where):
            return value + acc_refs[idx][where] if n_acc else value

        @pl.when(first_block)
        def _():
            dk_carry[...] = jnp.zeros_like(dk_carry)
            dv_carry[...] = jnp.zeros_like(dv_carry)

        @pl.when(n < nb)
        def _():
            for p in range(n_pairs):
                cols = pl.ds(p * LANES, LANES)
                bias = _pair_slopes(first_head, p, n_heads, d) * dist
                for r in range(d):
                    rs = _residue_rows(r, d)
                    q2 = _stack_heads((q_ref[rs, cols] * scale).astype(BF16), masks)
                    k2 = jnp.concatenate([kp_ref[rs, cols], kc_ref[rs, cols]], axis=0).astype(BF16)
                    v2 = jnp.concatenate([vp_ref[rs, cols], vc_ref[rs, cols]], axis=0).astype(BF16)
                    do = do_ref[rs, cols]
                    do2 = _stack_heads(do.astype(BF16), masks)
                    do_out = do * out_ref[rs, cols]
                    lse_all = lse_ref[rs, cols]
                    delta = jnp.concatenate([jnp.sum(jnp.where(masks[hh], do_out, 0.0), axis=1, keepdims=True)
                                             for hh in range(2)], axis=0)
                    lse2 = jnp.concatenate([jnp.max(jnp.where(masks[hh], lse_all, NEG_INF), axis=1, keepdims=True)
                                            for hh in range(2)], axis=0)
                    s = jnp.where(valid, _dot(q2, k2, 1, 1) - bias, NEG_INF)
                    pr = jnp.exp(s - lse2)
                    ds = (pr * (_dot(do2, v2, 1, 1) - delta)).astype(BF16)
                    dq = _unstack_heads(_dot(ds, k2, 1, 0), masks)
                    dk2 = _dot(ds, q2, 0, 0)
                    dv2 = _dot(pr.astype(BF16), do2, 0, 0)
                    dq_ref[rs, cols] = plus(dq * scale, 0, rs, cols)
                    dk_ref[rs, cols] = plus(dk_carry[r, :, cols] + dk2[:ATTN_BLOCK], 1, rs, cols)
                    dv_ref[rs, cols] = plus(dv_carry[r, :, cols] + dv2[:ATTN_BLOCK], 2, rs, cols)
                    dk_carry[r, :, cols] = dk2[ATTN_BLOCK:]
                    dv_carry[r, :, cols] = dv2[ATTN_BLOCK:]

        @pl.when(n == nb)
        def _():
            for r in range(d):
                rs = _residue_rows(r, d)
                dk_ref[rs, :] = plus(dk_carry[r], 1, rs, slice(None))
                dv_ref[rs, :] = plus(dv_carry[r], 2, rs, slice(None))

    def qkv_spec(which, shift):
        return pl.BlockSpec((rows, lw), lambda b, n: (jnp.clip(n - shift, 0, nb - 1), which * per + b))

    q_like = pl.BlockSpec((rows, lw), lambda b, n: (jnp.minimum(n, nb - 1), b))
    k_like = pl.BlockSpec((rows, lw), lambda b, n: (jnp.maximum(n - 1, 0), b))
    return _pallas(
        body, name=name, grid=(per, nb + 1), out_shape=[jax.ShapeDtypeStruct((t_len, w), F32)] * 3,
        in_specs=[qkv_spec(0, 0), qkv_spec(1, 1), qkv_spec(1, 0), qkv_spec(2, 1), qkv_spec(2, 0),
                  q_like, q_like, q_like] + [q_like, k_like, k_like][:n_acc],
        out_specs=[q_like, k_like, k_like], operands=[proj] * 5 + [out, lse, dout, *(acc or ())],
        scratch_shapes=[pltpu.VMEM((d, ATTN_BLOCK, lw), F32), pltpu.VMEM((d, ATTN_BLOCK, lw), F32)],
        semantics=("parallel", "arbitrary"), rides=rides)


def _shift_down(u, s):
    if s == 0:
        return u
    row = lax.broadcasted_iota(jnp.int32, u.shape, 0)
    return jnp.where(row >= s, pltpu.roll(u, s, 0), 0.0)


def _shift_up(u, s):
    if s == 0:
        return u
    n = u.shape[0]
    row = lax.broadcasted_iota(jnp.int32, u.shape, 0)
    return jnp.where(row < n - s, pltpu.roll(u, n - s, 0), 0.0)


def _conv_fwd(u, col0, w, b, name):
    t_len, ch = u.shape[0], w.shape[1]
    blk0 = col0 // LANES

    def body(u_ref, w_ref, b_ref, o_ref):
        uv = u_ref[...]
        pre = b_ref[...] + jnp.zeros_like(uv)
        for k in range(SSD_CONV):
            pre = pre + w_ref[k:k + 1, :] * _shift_down(uv, SSD_CONV - 1 - k)
        o_ref[...] = pre * jax.nn.sigmoid(pre)

    return pl.pallas_call(
        body, name=name, grid=(ch // LANES,),
        out_shape=jax.ShapeDtypeStruct((t_len, ch), F32),
        in_specs=[pl.BlockSpec((t_len, LANES), lambda j: (0, blk0 + j)),
                  pl.BlockSpec((SSD_CONV, LANES), lambda j: (0, j)), pl.BlockSpec((1, LANES), lambda j: (0, j))],
        out_specs=pl.BlockSpec((t_len, LANES), lambda j: (0, j)),
        compiler_params=_params("parallel"),
    )(u, w, b)


def _conv_bwd(u, col0, w, b, dact, name):
    t_len, ch = u.shape[0], w.shape[1]
    blk0 = col0 // LANES

    def body(u_ref, w_ref, b_ref, da_ref, du_ref, dw_ref, db_ref):
        uv = u_ref[...]
        shifted = [_shift_down(uv, SSD_CONV - 1 - k) for k in range(SSD_CONV)]
        pre = b_ref[...] + jnp.zeros_like(uv)
        for k in range(SSD_CONV):
            pre = pre + w_ref[k:k + 1, :] * shifted[k]
        sig = jax.nn.sigmoid(pre)
        dpre = da_ref[...] * (sig * (1.0 + pre * (1.0 - sig)))
        du = jnp.zeros_like(uv)
        for k in range(SSD_CONV):
            du = du + w_ref[k:k + 1, :] * _shift_up(dpre, SSD_CONV - 1 - k)
            dw_ref[k:k + 1, :] = jnp.sum(dpre * shifted[k], axis=0, keepdims=True)
        du_ref[...] = du
        db_ref[...] = jnp.sum(dpre, axis=0, keepdims=True)

    col = pl.BlockSpec((t_len, LANES), lambda j: (0, j))
    w_spec = pl.BlockSpec((SSD_CONV, LANES), lambda j: (0, j))
    b_spec = pl.BlockSpec((1, LANES), lambda j: (0, j))
    return pl.pallas_call(
        body, name=name, grid=(ch // LANES,),
        out_shape=[jax.ShapeDtypeStruct((t_len, ch), F32), jax.ShapeDtypeStruct((SSD_CONV, ch), F32),
                   jax.ShapeDtypeStruct((1, ch), F32)],
        in_specs=[pl.BlockSpec((t_len, LANES), lambda j: (0, blk0 + j)), w_spec, b_spec, col],
        out_specs=[col, w_spec, b_spec],
        compiler_params=_params("parallel"),
    )(u, w, b, dact)


def _cumsum_rows(v):
    n = v.shape[0]
    row = lax.broadcasted_iota(jnp.int32, v.shape, 0)
    s = 1
    while s < n:
        v = v + jnp.where(row >= s, pltpu.roll(v, s, 0), 0.0)
        s *= 2
    return v


def _rev_cumsum_rows(v):
    n = v.shape[0]
    row = lax.broadcasted_iota(jnp.int32, v.shape, 0)
    s = 1
    while s < n:
        v = v + jnp.where(row < n - s, pltpu.roll(v, n - s, 0), 0.0)
        s *= 2
    return v


def _head_selector(heads, width):
    j = lax.broadcasted_iota(jnp.int32, (LANES, width), 0)
    lane = lax.broadcasted_iota(jnp.int32, (LANES, width), 1)
    return jnp.where(jnp.logical_and(lane // HEAD_DIM == j, j < heads), 1.0, 0.0).astype(BF16)


class _SsdChunk:
    def __init__(self, dtraw_ref, bias_ref, alog_ref, xs_ref, b_ref, c_ref, heads):
        q = SSD_CHUNK
        width = heads * HEAD_DIM
        lane = lax.broadcasted_iota(jnp.int32, (q, LANES), 1)
        self.head_lanes = lane < heads
        lane1 = lax.broadcasted_iota(jnp.int32, (1, LANES), 1)
        self.a = jnp.where(lane1 < heads, -jnp.exp(alog_ref[...]), 0.0)
        self.dt_arg = dtraw_ref[...] + bias_ref[...]
        self.dt = jnp.where(self.head_lanes, jax.nn.softplus(self.dt_arg), 0.0)
        self.cum = _cumsum_rows(self.dt * self.a)
        self.cum_t = self.cum.T
        last = self.cum[q - 1:q, :]
        self.sel = _head_selector(heads, width)
        self.expand = lambda v: _dot_exact(v, self.sel, 1, 0)
        self.segsum = lambda v: _dot_exact(v, self.sel, 1, 1)
        self.e_exp = self.expand(jnp.exp(self.cum))
        self.d_exp = self.expand(jnp.exp(last - self.cum))
        self.elast_exp = self.e_exp[q - 1:q, :]
        self.dt_exp = self.expand(self.dt)
        self.xs = xs_ref[...]
        self.x = self.xs * self.dt_exp
        self.xb = self.x.astype(BF16)
        self.bb = b_ref[...].astype(BF16)
        self.cb = c_ref[...].astype(BF16)
        self.cbm = _dot(self.cb, self.bb, 1, 1)
        li = lax.broadcasted_iota(jnp.int32, (q, q), 0)
        si = lax.broadcasted_iota(jnp.int32, (q, q), 1)
        self.tri = li >= si
        hl = lax.broadcasted_iota(jnp.int32, (q, LANES), 1)
        self.pair_masks = [hl < HEAD_DIM, hl >= HEAD_DIM]

    def decay(self, j):
        diff = self.cum[:, j:j + 1] - self.cum_t[j:j + 1, :]
        return jnp.exp(jnp.where(self.tri, diff, NEG_INF))


def _ssd_specs(t_len, heads, n_chunks, xbc_cols, rev):
    q, gw = SSD_CHUNK, heads * HEAD_DIM
    ssd_w = SSD_GROUPS * gw
    b_blk = ssd_w // SSD_STATE
    ch = (lambda c: n_chunks - 1 - c) if rev else (lambda c: c)
    return dict(
        dtraw=pl.BlockSpec((None, q, LANES), lambda g, c: (g, ch(c), 0)),
        small=pl.BlockSpec((None, 1, LANES), lambda g, c: (g, 0, 0)),
        dsk=pl.BlockSpec((None, 1, gw), lambda g, c: (g, 0, 0)),
        xs=pl.BlockSpec((q, gw), lambda g, c: (ch(c), g)),
        b=pl.BlockSpec((q, SSD_STATE), lambda g, c: (ch(c), b_blk + g)),
        c=pl.BlockSpec((q, SSD_STATE), lambda g, c: (ch(c), b_blk + SSD_GROUPS + g)),
        z=pl.BlockSpec((q, gw), lambda g, c: (ch(c), 3 * SSD_GROUPS + g)),
        tok=pl.BlockSpec((q, gw), lambda g, c: (ch(c), g)),
        state=pl.BlockSpec((None, SSD_STATE, gw), lambda g, c: (ch(c), 0, g)),
        bc=pl.BlockSpec((q, SSD_STATE), lambda g, c: (ch(c), g)),
    )


def _ssd_fwd(xbc, qkvz, dtraw_g, bias_g, alog_g, dsk_exp, heads, name):
    t_len = xbc.shape[0]
    q, gw = SSD_CHUNK, heads * HEAD_DIM
    n_chunks = t_len // q
    ssd_w = SSD_GROUPS * gw
    sp = _ssd_specs(t_len, heads, n_chunks, xbc.shape[1], rev=False)

    def body(dtraw_ref, bias_ref, alog_ref, dsk_ref, xs_ref, b_ref, c_ref, z_ref,
             yg_ref, ypre_ref, st_ref, s_scr):
        @pl.when(pl.program_id(1) == 0)
        def _():
            s_scr[...] = jnp.zeros_like(s_scr)

        k = _SsdChunk(dtraw_ref, bias_ref, alog_ref, xs_ref, b_ref, c_ref, heads)
        s_prev = s_scr[...]
        st_ref[...] = s_prev
        y_off = k.e_exp * _dot(k.cb, s_prev.astype(BF16), 1, 0)
        parts = []
        for p in range(heads // 2):
            xp = k.xb[:, p * LANES:(p + 1) * LANES]
            acc = jnp.zeros((q, LANES), F32)
            for hh in range(2):
                m = (k.cbm * k.decay(2 * p + hh)).astype(BF16)
                acc = acc + _dot(m, jnp.where(k.pair_masks[hh], xp, jnp.zeros_like(xp)), 1, 0)
            parts.append(acc)
        y = jnp.concatenate(parts, axis=1) + y_off
        xd = (k.x * k.d_exp).astype(BF16)
        s_scr[...] = k.elast_exp * s_prev + _dot(k.bb, xd, 0, 0)
        y_pre = y + dsk_ref[...] * k.xs
        zv = z_ref[...]
        ypre_ref[...] = y_pre
        yg_ref[...] = y_pre * (zv * jax.nn.sigmoid(zv))

    return pl.pallas_call(
        body, name=name, grid=(SSD_GROUPS, n_chunks),
        out_shape=[jax.ShapeDtypeStruct((t_len, ssd_w), F32), jax.ShapeDtypeStruct((t_len, ssd_w), F32),
                   jax.ShapeDtypeStruct((n_chunks, SSD_STATE, ssd_w), F32)],
        in_specs=[sp["dtraw"], sp["small"], sp["small"], sp["dsk"], sp["xs"], sp["b"], sp["c"], sp["z"]],
        out_specs=[sp["tok"], sp["tok"], sp["state"]],
        scratch_shapes=[pltpu.VMEM((SSD_STATE, gw), F32)],
        compiler_params=_params("parallel", "arbitrary"),
    )(dtraw_g, bias_g, alog_g, dsk_exp, xbc, xbc, xbc, qkvz)


def _ssd_bwd(xbc, qkvz, dtraw_g, bias_g, alog_g, dsk_exp, ypre, states, dyg, heads, name):
    t_len = xbc.shape[0]
    q, gw = SSD_CHUNK, heads * HEAD_DIM
    n_chunks = t_len // q
    ssd_w = SSD_GROUPS * gw
    sp = _ssd_specs(t_len, heads, n_chunks, xbc.shape[1], rev=True)

    def body(dtraw_ref, bias_ref, alog_ref, dsk_ref, xs_ref, b_ref, c_ref, z_ref, ypre_ref, st_ref, dyg_ref,
             dxs_ref, db_ref, dc_ref, dz_ref, ddt_ref, small_ref, g_scr):
        first = pl.program_id(1) == 0

        @pl.when(first)
        def _():
            g_scr[...] = jnp.zeros_like(g_scr)

        k = _SsdChunk(dtraw_ref, bias_ref, alog_ref, xs_ref, b_ref, c_ref, heads)
        zv = z_ref[...]
        sig = jax.nn.sigmoid(zv)
        dyg = dyg_ref[...]
        y_pre = ypre_ref[...]
        dy = dyg * (zv * sig)
        dz_ref[...] = dyg * y_pre * (sig * (1.0 + zv * (1.0 - sig)))
        dsk = dsk_ref[...]
        g_next = g_scr[...]
        s_prev = st_ref[...]
        sb = s_prev.astype(BF16)
        xd = k.x * k.d_exp
        xdb = xd.astype(BF16)
        gb = g_next.astype(BF16)
        dx_off = k.d_exp * _dot(k.bb, gb, 1, 0)
        dyb = dy.astype(BF16)
        dcb = jnp.zeros((q, q), F32)
        lane = lax.broadcasted_iota(jnp.int32, (q, LANES), 1)
        row_t = lax.broadcasted_iota(jnp.int32, (LANES, q), 0)
        w_rows = jnp.zeros((q, LANES), F32)
        w_cols_t = jnp.zeros((LANES, q), F32)
        parts = []
        for p in range(heads // 2):
            cols = slice(p * LANES, (p + 1) * LANES)
            dyp, xp = dyb[:, cols], k.xb[:, cols]
            acc = jnp.zeros((q, LANES), F32)
            for hh in range(2):
                j = 2 * p + hh
                lm = k.decay(j)
                m32 = k.cbm * lm
                dym = jnp.where(k.pair_masks[hh], dyp, jnp.zeros_like(dyp))
                acc = acc + _dot(m32.astype(BF16), dym, 0, 0)
                dm = _dot(dym, xp, 1, 1)
                dcb = dcb + dm * lm
                wmat = dm * m32
                w_rows = w_rows + jnp.where(lane == j, jnp.sum(wmat, axis=1, keepdims=True), 0.0)
                w_cols_t = w_cols_t + jnp.where(row_t == j, jnp.sum(wmat, axis=0, keepdims=True), 0.0)
            parts.append(acc)
        dx = jnp.concatenate(parts, axis=1) + dx_off
        dcbb = dcb.astype(BF16)
        edy = (k.e_exp * dy).astype(BF16)
        dc_ref[...] = _dot(dcbb, k.bb, 1, 0) + _dot(edy, sb, 1, 1)
        db_ref[...] = _dot(dcbb, k.cb, 0, 0) + _dot(xdb, gb, 1, 1)
        g_scr[...] = k.elast_exp * g_next + _dot(k.cb, edy, 0, 0)

        y_off = k.e_exp * _dot(k.cb, sb, 1, 0)
        dcum = w_rows - w_cols_t.T + k.segsum(dy * y_off)
        t_term = k.segsum(k.x * dx_off)
        gs = jnp.broadcast_to(jnp.sum(g_next * s_prev, axis=0, keepdims=True), (8, gw))
        carried = k.segsum(gs)[0:1, :] * jnp.exp(k.cum[q - 1:q, :])
        dda = _rev_cumsum_rows(dcum) + (_cumsum_rows(t_term) - t_term) + carried
        ddt = jnp.where(k.head_lanes, dda * k.a + k.segsum(dx * k.xs), 0.0)
        ddtraw = ddt * jax.nn.sigmoid(k.dt_arg)
        ddt_ref[...] = ddtraw
        dxs_ref[...] = dx * k.dt_exp + dsk * dy
        ds = jnp.broadcast_to(jnp.sum(dy * k.xs, axis=0, keepdims=True), (8, gw))
        d_alog = jnp.sum(jnp.where(k.head_lanes, dda * k.dt, 0.0), axis=0, keepdims=True) * k.a
        rows8 = lax.broadcasted_iota(jnp.int32, (8, LANES), 0)
        small = jnp.where(rows8 == 0, d_alog, 0.0)
        small = small + jnp.where(rows8 == 1, jnp.sum(ddtraw, axis=0, keepdims=True), 0.0)
        small = small + jnp.where(rows8 == 2, k.segsum(ds)[0:1, :], 0.0)

        @pl.when(first)
        def _():
            small_ref[...] = small

        @pl.when(jnp.logical_not(first))
        def _():
            small_ref[...] += small

    bc_out = sp["bc"]
    return pl.pallas_call(
        body, name=name, grid=(SSD_GROUPS, n_chunks),
        out_shape=[jax.ShapeDtypeStruct((t_len, ssd_w), F32),
                   jax.ShapeDtypeStruct((t_len, SSD_GROUPS * SSD_STATE), F32),
                   jax.ShapeDtypeStruct((t_len, SSD_GROUPS * SSD_STATE), F32),
                   jax.ShapeDtypeStruct((t_len, ssd_w), F32),
                   jax.ShapeDtypeStruct((SSD_GROUPS, t_len, LANES), F32),
                   jax.ShapeDtypeStruct((SSD_GROUPS, 8, LANES), F32)],
        in_specs=[sp["dtraw"], sp["small"], sp["small"], sp["dsk"], sp["xs"], sp["b"], sp["c"], sp["z"],
                  sp["tok"], sp["state"], sp["tok"]],
        out_specs=[sp["tok"], bc_out, bc_out, sp["tok"], sp["dtraw"],
                   pl.BlockSpec((None, 8, LANES), lambda g, c: (g, 0, 0))],
        scratch_shapes=[pltpu.VMEM((SSD_STATE, gw), F32)],
        compiler_params=_params("parallel", "arbitrary"),
    )(dtraw_g, bias_g, alog_g, dsk_exp, xbc, xbc, xbc, qkvz, ypre, states, dyg)


def _adamw(w, g, m, v, name, rides=None):
    n_lead, rows, lanes = w.shape
    tr = _row_tile(rows, lanes, 4, 14)
    c1 = 1.0 / (1.0 - ADAM_B1 ** ADAM_STEP)
    c2 = 1.0 / (1.0 - ADAM_B2 ** ADAM_STEP)

    def body(w_ref, g_ref, m_ref, v_ref, d_ref, nm_ref, nv_ref):
        gv = g_ref[...]
        nm = ADAM_B1 * m_ref[...] + (1.0 - ADAM_B1) * gv
        nv = ADAM_B2 * v_ref[...] + (1.0 - ADAM_B2) * (gv * gv)
        nm_ref[...] = nm
        nv_ref[...] = nv
        d_ref[...] = -ADAM_LR * ((nm * c1) / (jnp.sqrt(nv * c2) + ADAM_EPS) + ADAM_WD * w_ref[...])

    spec = pl.BlockSpec((None, tr, lanes), lambda l, i: (l, i, 0))
    return _pallas(body, name=name, grid=(n_lead, rows // tr), out_shape=[jax.ShapeDtypeStruct(w.shape, F32)] * 3,
                   in_specs=[spec] * 4, out_specs=[spec] * 3, operands=[w, g, m, v],
                   semantics=("parallel", "parallel"), rides=rides)


def _pad_lanes(a, width=LANES):
    return jnp.pad(a, ((0, 0), (0, width - a.shape[1])))


def _group_pad(v, heads):
    return _pad_lanes(v.reshape(SSD_GROUPS, heads))[:, None, :]


def _layer_fwd(x0, p, wt, dims, tag, rides):
    w_attn, heads_g, n_heads, conv_ch = dims["w_attn"], dims["heads_g"], dims["n_heads"], dims["conv_ch"]
    h1 = _rmsnorm_fwd([x0], [[x0.shape[1]]], p["ln1_g"], f"ln1_fwd{tag}")
    proj = _mm(h1, wt("w_in"), name=f"in_proj{tag}", tn=1152, rides=rides)

    outs, lses = [], []
    for d in BRANCH_DILATIONS:
        o, l = _attn_branch_fwd(proj, w_attn, d, n_heads, f"attn_fwd_d{d}{tag}", rides)
        outs.append(o)
        lses.append(l)
    attn, lse = _attn_combine(outs, lses, f"attn_combine{tag}")

    xbc = _conv_fwd(proj, 4 * w_attn, p["conv_w"], p["conv_b"], f"conv_fwd{tag}")
    dt_col = 4 * w_attn + conv_ch
    dtraw_g = jnp.stack([_pad_lanes(proj[:, dt_col + g * heads_g:dt_col + (g + 1) * heads_g])
                         for g in range(SSD_GROUPS)])
    bias_g, alog_g = _group_pad(p["dt_bias"], heads_g), _group_pad(p["a_log"], heads_g)
    dsk_exp = jnp.repeat(p["d_skip"], HEAD_DIM).reshape(SSD_GROUPS, 1, heads_g * HEAD_DIM)
    yg, ypre, states = _ssd_fwd(xbc, proj, dtraw_g, bias_g, alog_g, dsk_exp, heads_g, f"ssd_fwd{tag}")

    gw = heads_g * HEAD_DIM
    mix_g = jnp.concatenate([p["attn_norm_g"], p["ssd_norm_g"]])[None, :]
    mix = _rmsnorm_fwd([attn, yg], [[w_attn], [gw] * SSD_GROUPS], mix_g, f"mix_norm_fwd{tag}")
    x1 = _mm(mix, wt("w_out"), name=f"out_proj{tag}", residual=x0, rides=rides)
    h2 = _rmsnorm_fwd([x1], [[x1.shape[1]]], p["ln2_g"], f"ln2_fwd{tag}")
    u = _mm(h2, wt("w_mlp_in"), name=f"mlp_in{tag}", out_dtype=BF16, tn=1024, rides=rides)
    x2 = _mm(u, wt("w_mlp_out"), name=f"mlp_out{tag}", a_act="relu2", residual=x1, tn=256, rides=rides)
    saved = dict(x0=x0, h1=h1, proj=proj, attn=attn, lse=lse, xbc=xbc, dtraw_g=dtraw_g,
                 bias_g=bias_g, alog_g=alog_g, dsk_exp=dsk_exp, yg=yg, ypre=ypre, states=states, mix=mix,
                 mix_g=mix_g, x1=x1, h2=h2, u=u)
    return x2, saved


def _pair_sums(ex, host, items):
    swapped = ex["rides"].done[("swap", host)]
    core = lax.axis_index("c").astype(jnp.int32).reshape(1)
    for i, (n, l) in enumerate(items):
        ex["pair"][(n, l)] = _pair_sum(ex["bufs"][(n, l)], swapped[i], core, f"pair_sum_{n}_l{l}")


def _layer_bwd(dx2, dx2_b, p, wt, s, dims, l, ex, copy_dx0):
    w_attn, heads_g, n_heads, conv_ch = dims["w_attn"], dims["heads_g"], dims["n_heads"], dims["conv_ch"]
    t_len, d_model = dx2.shape
    gw = heads_g * HEAD_DIM
    h_ssd = heads_g * SSD_GROUPS
    tag, rides, bufs = f"_l{l}", ex["rides"], ex["bufs"]
    du = _mm(dx2_b, wt("w_mlp_out"), name=f"mlp_out_dx{tag}", tb=True, gate=s["u"], out_dtype=BF16, tn=1024,
             rides=rides)
    d_wmo = _mm(s["u"], dx2_b, name=f"mlp_out_dw{tag}", ta=True, a_act="relu2", tm=512, tn=1024, out_dtype=BF16)
    bufs[("w_mlp_out", l)] = d_wmo.reshape(N_DEV, -1, d_model)
    bufs[("w_mlp_in", l)] = _mm(s["h2"], du, name=f"mlp_in_dw{tag}", ta=True, tm=512, tn=1024, out_dtype=BF16,
                                out_chunk=du.shape[1] // N_DEV)
    dh2 = _mm(du, wt("w_mlp_in"), name=f"mlp_in_dx{tag}", tb=True, tn=256, out_dtype=BF16, rides=rides)
    _pair_sums(ex, f"mlp_in_dx{tag}", [("w_mlp_out", l), ("w_mlp_in", l)])
    (dx1,), d_ln2, (dx1_b,) = _rmsnorm_bwd([s["x1"]], [[d_model]], p["ln2_g"], dh2, [dx2], f"ln2_bwd{tag}",
                                           bf16_copy=True)
    dmix = _mm(dx1_b, wt("w_out"), name=f"out_proj_dx{tag}", tb=True, out_dtype=BF16)
    d_wo = _mm(s["mix"], dx1_b, name=f"out_proj_dw{tag}", ta=True, tm=512, tn=1024, out_dtype=BF16)
    bufs[("w_out", l)] = d_wo.reshape(N_DEV, -1, d_model)
    after_branch = {BRANCH_DILATIONS[0]: [("w_out", l)]}
    (dattn, dyg), d_mix_g, _ = _rmsnorm_bwd([s["attn"], s["yg"]], [[w_attn], [gw] * SSD_GROUPS], s["mix_g"], dmix,
                                           [None, None], f"mix_norm_bwd{tag}")
    dxs, db, dc, dz, ddtraw_g, ssd_small = _ssd_bwd(
        s["xbc"], s["proj"], s["dtraw_g"], s["bias_g"], s["alog_g"], s["dsk_exp"], s["ypre"], s["states"], dyg,
        heads_g, f"ssd_bwd{tag}")
    dxbc = jnp.concatenate([dxs, db, dc], axis=1)
    dxbc_raw, d_conv_w, d_conv_b = _conv_bwd(s["proj"], 4 * w_attn, p["conv_w"], p["conv_b"], dxbc, f"conv_bwd{tag}")
    acc = None
    for d in BRANCH_DILATIONS:
        acc = _attn_branch_bwd(s["proj"], w_attn, s["attn"], s["lse"], dattn, d, n_heads, f"attn_bwd_d{d}{tag}", acc,
                               rides)
        if d in after_branch:
            _pair_sums(ex, f"attn_bwd_d{d}{tag}", after_branch[d])
    w_in = wt("w_in")
    in_proj = 4 * w_attn + conv_ch + h_ssd
    pad = jnp.zeros((t_len, w_in.shape[1] - in_proj), F32)
    dproj = jnp.concatenate([*acc, dz, dxbc_raw] + [ddtraw_g[g, :, :heads_g] for g in range(SSD_GROUPS)] + [pad],
                            axis=1).astype(BF16)
    d_win = _mm(s["h1"], dproj, name=f"in_proj_dw{tag}", ta=True, tm=512, tn=1152, out_dtype=BF16, rides=rides)
    bufs[("w_in", l)] = d_win[:, :in_proj].reshape(d_model, N_DEV, -1).transpose(1, 0, 2)
    dh1 = _mm(dproj, w_in, name=f"in_proj_dx{tag}", tb=True, out_dtype=BF16, rides=rides)
    _pair_sums(ex, f"in_proj_dx{tag}", [("w_in", l)])
    (dx0,), d_ln1, dx0_b = _rmsnorm_bwd([s["x0"]], [[d_model]], p["ln1_g"], dh1, [dx1], f"ln1_bwd{tag}",
                                        bf16_copy=copy_dx0)

    small = ssd_small[:, :, :heads_g]
    grads = dict(
        ln1_g=d_ln1[0], conv_w=d_conv_w, conv_b=d_conv_b[0],
        a_log=small[:, 0].reshape(h_ssd), dt_bias=small[:, 1].reshape(h_ssd), d_skip=small[:, 2].reshape(h_ssd),
        attn_norm_g=d_mix_g[0, :w_attn], ssd_norm_g=d_mix_g[0, w_attn:], ln2_g=d_ln2[0])
    return dx0, (dx0_b[0] if copy_dx0 else None), grads


_SMALL = ["ln1_g", "conv_w", "conv_b", "dt_bias", "a_log", "d_skip", "attn_norm_g", "ssd_norm_g", "ln2_g"]
_WEIGHTS = ["ln1_g", "w_in", "conv_w", "conv_b", "dt_bias", "a_log", "d_skip", "attn_norm_g", "ssd_norm_g",
            "w_out", "ln2_g", "w_mlp_in", "w_mlp_out", "final_norm_g"]


def _to_rows(a):
    flat = a.reshape(-1)
    rows = -(-flat.shape[0] // LANES)
    rows = -(-rows // 8) * 8
    return jnp.pad(flat, (0, rows * LANES - flat.shape[0])).reshape(rows, LANES)


def kernel(x, ln1_g, w_in, conv_w, conv_b, dt_bias, a_log, d_skip, attn_norm_g, ssd_norm_g, w_out, ln2_g, w_mlp_in, w_mlp_out, final_norm_g, loss_target, m_ln1_g, m_w_in, m_conv_w, m_conv_b, m_dt_bias, m_a_log, m_d_skip, m_attn_norm_g, m_ssd_norm_g, m_w_out, m_ln2_g, m_w_mlp_in, m_w_mlp_out, m_final_norm_g, v_ln1_g, v_w_in, v_conv_w, v_conv_b, v_dt_bias, v_a_log, v_d_skip, v_attn_norm_g, v_ssd_norm_g, v_w_out, v_ln2_g, v_w_mlp_in, v_w_mlp_out, v_final_norm_g):
    w = dict(ln1_g=ln1_g, w_in=w_in, conv_w=conv_w, conv_b=conv_b, dt_bias=dt_bias, a_log=a_log, d_skip=d_skip,
             attn_norm_g=attn_norm_g, ssd_norm_g=ssd_norm_g, w_out=w_out, ln2_g=ln2_g, w_mlp_in=w_mlp_in,
             w_mlp_out=w_mlp_out, final_norm_g=final_norm_g)
    mom = dict(ln1_g=m_ln1_g, w_in=m_w_in, conv_w=m_conv_w, conv_b=m_conv_b, dt_bias=m_dt_bias, a_log=m_a_log,
               d_skip=m_d_skip, attn_norm_g=m_attn_norm_g, ssd_norm_g=m_ssd_norm_g, w_out=m_w_out, ln2_g=m_ln2_g,
               w_mlp_in=m_w_mlp_in, w_mlp_out=m_w_mlp_out, final_norm_g=m_final_norm_g)
    var = dict(ln1_g=v_ln1_g, w_in=v_w_in, conv_w=v_conv_w, conv_b=v_conv_b, dt_bias=v_dt_bias, a_log=v_a_log,
               d_skip=v_d_skip, attn_norm_g=v_attn_norm_g, ssd_norm_g=v_ssd_norm_g, w_out=v_w_out, ln2_g=v_ln2_g,
               w_mlp_in=v_w_mlp_in, w_mlp_out=v_w_mlp_out, final_norm_g=v_final_norm_g)

    depth, d_model = ln1_g.shape
    t_len = x.shape[1]
    w_attn = attn_norm_g.shape[1]
    h_ssd = dt_bias.shape[1]
    conv_ch = conv_b.shape[1]
    in_proj = w_in.shape[2] * N_DEV
    assert ssd_norm_g.shape[1] == w_attn and in_proj == 4 * w_attn + conv_ch + h_ssd
    assert t_len % (BRANCH_DILATIONS[-1] * ATTN_BLOCK) == 0 and h_ssd % (2 * SSD_GROUPS) == 0
    dims = dict(w_attn=w_attn, heads_g=h_ssd // SSD_GROUPS, n_heads=w_attn // HEAD_DIM, conv_ch=conv_ch)
    names = ["w_in", "w_out", "w_mlp_in", "w_mlp_out"]

    rides = _Rides()
    ex = dict(rides=rides, bufs={}, pair={})
    latest, sent = {}, {}

    def shard(n, l):
        return w[n][l].astype(BF16)

    def half(rows, part):
        return None if part is None else (part * (rows // 2), rows // 2)

    def plan_spread(host, n, l, part=None):
        key, prev = ("spread", host, n, l, part), latest.get((n, l))
        rides.put(host, key, lambda: _GatherSpread([shard(n, l)], rows=half(w[n].shape[1], part),
                                                   into=[rides.done[prev[0]][prev[1]]] if prev else None))
        latest[(n, l)] = (key, 0)

    def plan_pass(host, items):
        key, srcs = ("pass", host), [latest[it] for it in items]
        rides.put(host, key, lambda: _GatherPass([rides.done[k][i] for k, i in srcs]))
        for i, it in enumerate(items):
            latest[it] = (key, i)

    def plan_swap(host, items):
        rides.put(host, ("swap", host), lambda: _SiblingSwap([ex["bufs"][it] for it in items]))

    def plan_send(host, n, l, part=None):
        key, prev = ("send", host, n, l, part), sent.get((n, l))
        rides.put(host, key, lambda: _ChipSend([ex["pair"][(n, l)]], rows=half(ex["pair"][(n, l)].shape[1], part),
                                               into=[rides.done[prev[0]][prev[1]]] if prev else None))
        sent[(n, l)] = (key, 0)

    d_first, d_mid, d_last = (f"d{d}" for d in BRANCH_DILATIONS)
    for l in range(depth):
        t = f"_l{l}"
        if l == 0:
            plan_spread(f"in_proj{t}", "w_out", 0)
            plan_spread(f"in_proj{t}", "w_mlp_in", 0, 0)
            plan_spread(f"attn_fwd_{d_first}{t}", "w_mlp_in", 0, 1)
            plan_spread(f"attn_fwd_{d_mid}{t}", "w_mlp_out", 0, 0)
            plan_pass(f"attn_fwd_{d_mid}{t}", [("w_out", 0), ("w_mlp_in", 0)])
            plan_spread(f"attn_fwd_{d_last}{t}", "w_mlp_out", 0, 1)
            plan_pass(f"out_proj{t}", [("w_mlp_out", 0)])
        else:
            plan_spread(f"in_proj{t}", "w_mlp_out", l, 0)
            plan_spread(f"attn_fwd_{d_first}{t}", "w_mlp_out", l, 1)
            plan_pass(f"attn_fwd_{d_mid}{t}", [("w_mlp_out", l)])
        if l + 1 < depth:
            plan_spread(f"out_proj{t}", "w_out", l + 1)
            plan_spread(f"mlp_in{t}", "w_in", l + 1)
            plan_spread(f"mlp_out{t}", "w_mlp_in", l + 1)
            plan_pass(f"pass_weights_l{l + 1}", [("w_out", l + 1), ("w_in", l + 1), ("w_mlp_in", l + 1)])
        plan_swap(f"mlp_in_dx{t}", [("w_mlp_out", l), ("w_mlp_in", l)])
        plan_send(f"attn_bwd_{d_first}{t}", "w_mlp_out", l, 0)
        plan_swap(f"attn_bwd_{d_first}{t}", [("w_out", l)])
        plan_send(f"attn_bwd_{d_mid}{t}", "w_mlp_out", l, 1)
        plan_send(f"attn_bwd_{d_last}{t}", "w_mlp_in", l)
        plan_send(f"in_proj_dw{t}", "w_out", l)
        plan_swap(f"in_proj_dx{t}", [("w_in", l)])
        if l > 0:
            plan_send(f"mlp_out_dx_l{l - 1}", "w_in", l)
        else:
            plan_send("adamw_w_mlp_in", "w_in", l, 0)
            plan_send("adamw_w_mlp_out", "w_in", l, 1)

    g_in0, g_cw = _all_gather([shard("w_in", 0), conv_w], "gather_first")
    full_cw = g_cw.transpose(1, 2, 0, 3).reshape(depth, SSD_CONV, conv_ch)
    proj_cols = -(-in_proj // LANES) * LANES
    full = {}

    def weight(n, l):
        if (n, l) not in full:
            if (n, l) == ("w_in", 0):
                g = g_in0
            else:
                key, i = latest[(n, l)]
                g = rides.done[key][i]
            if n == "w_in":
                g = _pad_lanes(g.transpose(1, 0, 2).reshape(d_model, in_proj), proj_cols)
            elif n == "w_mlp_in":
                g = g.transpose(1, 0, 2).reshape(d_model, -1)
            else:
                g = g.reshape(-1, d_model)
            full[(n, l)] = g
        return full[(n, l)]

    layers = [dict(ln1_g=ln1_g[l][None, :], ln2_g=ln2_g[l][None, :], conv_w=full_cw[l], conv_b=conv_b[l][None, :],
                   dt_bias=dt_bias[l], a_log=a_log[l], d_skip=d_skip[l], attn_norm_g=attn_norm_g[l],
                   ssd_norm_g=ssd_norm_g[l]) for l in range(depth)]

    h = x[0]
    saved = []
    for l in range(depth):
        h, s = _layer_fwd(h, layers[l], functools.partial(lambda n, l: weight(n, l), l=l), dims, f"_l{l}", rides)
        saved.append(s)
        if l + 1 < depth:
            _alone(rides, f"pass_weights_l{l + 1}")
    dh, d_final_g, loss_part, dh_b = _loss_head(h, final_norm_g[None, :], loss_target[0], "loss_head")

    grads = [None] * depth
    for l in reversed(range(depth)):
        dh, dh_b, grads[l] = _layer_bwd(dh, dh_b, layers[l], functools.partial(lambda n, l: weight(n, l), l=l),
                                        saved[l], dims, l, ex, copy_dx0=l > 0)
    grad_x = dh[None]

    my_chip = (2 * lax.axis_index("x") + lax.axis_index("y")).astype(jnp.int32).reshape(1)
    gsum, delta, new_m, new_v = {}, {}, {}, {}
    for n in names[1:] + names[:1]:
        per_layer = []
        for l in range(depth):
            key, i = sent[(n, l)]
            per_layer.append(_sum_with_own(rides.done[key][i], ex["pair"][(n, l)], my_chip, f"sum_{n}_l{l}"))
        gsum[n] = jnp.stack(per_layer)
        delta[n], new_m[n], new_v[n] = _adamw(w[n], gsum[n], mom[n], var[n], f"adamw_{n}", rides)

    small_parts = [jnp.stack([grads[l][n] for l in range(depth)]).reshape(-1) for n in _SMALL]
    small_parts += [d_final_g.reshape(-1), loss_part[0, :1]]
    sizes = [int(a.shape[0]) for a in small_parts]
    packed = _to_rows(jnp.concatenate(small_parts))
    (gathered,) = _all_gather([packed], "gather_small_grads")
    total = _sum_leading(gathered, "sum_small_grads").reshape(-1)
    offs = np.cumsum([0] + sizes)
    pieces = [total[offs[i]:offs[i + 1]] for i in range(len(sizes))]
    for n, piece in zip(_SMALL, pieces):
        shape = (depth, SSD_CONV, conv_ch) if n == "conv_w" else w[n].shape
        gsum[n] = piece.reshape(shape)
    gsum["final_norm_g"] = pieces[len(_SMALL)]
    loss = pieces[len(_SMALL) + 1][0]
    my_id = 4 * lax.axis_index("x") + 2 * lax.axis_index("y") + lax.axis_index("c")
    cw = conv_w.shape[2]
    gsum["conv_w"] = lax.dynamic_slice_in_dim(gsum["conv_w"], my_id * cw, cw, axis=2)

    small_names = [n for n in _WEIGHTS if n not in names]
    sm_sizes = [int(np.prod(w[n].shape)) for n in small_names]
    pack = lambda d: _to_rows(jnp.concatenate([d[n].reshape(-1) for n in small_names]))[None]
    outs = _adamw(pack(w), pack(gsum), pack(mom), pack(var), "adamw_small")
    sm_offs = np.cumsum([0] + sm_sizes)
    for res, o in zip((delta, new_m, new_v), outs):
        flat = o.reshape(-1)
        for i, n in enumerate(small_names):
            res[n] = flat[sm_offs[i]:sm_offs[i + 1]].reshape(w[n].shape)

    return (loss, grad_x, *[gsum[n] for n in _WEIGHTS], *[delta[n] for n in _WEIGHTS],
            *[new_m[n] for n in _WEIGHTS], *[new_v[n] for n in _WEIGHTS])
```

```python
import functools
import math

import numpy as np
import jax
import jax.numpy as jnp
from jax import lax
from jax.experimental import pallas as pl
from jax.experimental.pallas import tpu as pltpu

F32 = jnp.float32
BF16 = jnp.bfloat16

N_DEV = 8
LANES = 128
HEAD_DIM = 64
ATTN_BLOCK = 128
BRANCH_DILATIONS = (1, 4, 16)
SSD_GROUPS = 2
SSD_STATE = 128
SSD_CHUNK = 128
SSD_CONV = 4
NORM_EPS = 1e-5
ADAM_LR, ADAM_B1, ADAM_B2, ADAM_EPS, ADAM_WD, ADAM_STEP = 0.001, 0.9, 0.999, 1e-08, 0.01, 10
VMEM_LIMIT_BYTES = 56 * 1024 * 1024
MESH = pl.DeviceIdType.MESH
NEG_INF = float("-inf")


def _params(*sem):
    return pltpu.CompilerParams(dimension_semantics=tuple(sem), vmem_limit_bytes=VMEM_LIMIT_BYTES)


def _pick(n, target, mult):
    best = None
    for t in range(mult, min(n, target) + 1, mult):
        if n % t == 0:
            best = t
    assert best is not None, (n, target, mult)
    return best


def _dot(a, b, ca, cb):
    return lax.dot_general(a, b, (((ca,), (cb,)), ((), ())), preferred_element_type=F32)


def _split3(v):
    hi = v.astype(BF16)
    r = v - hi.astype(F32)
    mid = r.astype(BF16)
    lo = (r - mid.astype(F32)).astype(BF16)
    return hi, mid, lo


def _dot_exact(v, sel, ca, cb):
    hi, mid, lo = _split3(v)
    return _dot(hi, sel, ca, cb) + _dot(mid, sel, ca, cb) + _dot(lo, sel, ca, cb)


_HBM = pl.BlockSpec(memory_space=pltpu.HBM)


def _all_gather(xs, name):
    n = len(xs)

    def body(*refs):
        x_refs, o_refs = refs[:n], refs[n:2 * n]
        send_sems, recv_sems = refs[2 * n:]
        x, y, c = lax.axis_index("x"), lax.axis_index("y"), lax.axis_index("c")
        me, sibling = (x, y, c), (x, y, 1 - c)
        chips = [(1 - x, y), (x, 1 - y), (1 - x, 1 - y)]

        def copy(t, k, block, to, src=None):
            bx, by, bc = block
            dst = o_refs[t].at[4 * bx + 2 * by + bc]
            return pltpu.make_async_remote_copy(
                src_ref=dst if src is None else src, dst_ref=dst,
                send_sem=send_sems.at[t, k], recv_sem=recv_sems.at[t, k],
                device_id=to, device_id_type=MESH)

        first, passed = [], []
        for t in range(n):
            cps = [copy(t, 0, me, sibling, src=x_refs[t])]
            cps += [copy(t, 1 + j, me, (*chip, c), src=x_refs[t]) for j, chip in enumerate(chips)]
            for cp in cps:
                cp.start()
            first += cps
        for t in range(n):
            for j, chip in enumerate(chips):
                copy(t, 1 + j, (*chip, c), me).wait_recv()
                fwd = copy(t, 4 + j, (*chip, c), sibling)
                fwd.start()
                passed.append(fwd)
        for t in range(n):
            copy(t, 0, sibling, me).wait_recv()
            back = copy(t, 7, sibling, sibling)
            back.start()
            passed.append(back)
        for t in range(n):
            copy(t, 7, me, me).wait_recv()
            for j, chip in enumerate(chips):
                copy(t, 4 + j, (*chip, 1 - c), me).wait_recv()
        for cp in first + passed:
            cp.wait_send()

    return pl.pallas_call(
        body, name=name,
        out_shape=[jax.ShapeDtypeStruct((N_DEV,) + a.shape, a.dtype) for a in xs],
        in_specs=[_HBM] * n, out_specs=[_HBM] * n,
        scratch_shapes=[pltpu.SemaphoreType.DMA((n, 8)), pltpu.SemaphoreType.DMA((n, 8))],
    )(*xs)


def _place():
    x, y, c = lax.axis_index("x"), lax.axis_index("y"), lax.axis_index("c")
    return x, y, c, 4 * x + 2 * y + c, (x, y, 1 - c), [(1 - x, y), (x, 1 - y), (1 - x, 1 - y)]


def _remote(src, dst, send_sem, recv_sem, to):
    return pltpu.make_async_remote_copy(src_ref=src, dst_ref=dst, send_sem=send_sem, recv_sem=recv_sem,
                                        device_id=to, device_id_type=MESH)


class _Riding:
    aliases = {}

    def copies(self, ins, outs, sems):
        raise NotImplementedError

    def start(self, ins, outs, sems):
        local, out, _ = self.copies(ins, outs, sems)
        for cp in local + out:
            cp.start()

    def wait(self, ins, outs, sems):
        local, out, landing = self.copies(ins, outs, sems)
        for cp in landing:
            cp.wait_recv()
        for cp in out:
            cp.wait_send()
        for cp in local:
            cp.wait()


def _rows_of(ref, rows):
    return ref if rows is None else ref.at[pl.ds(rows[0], rows[1])]


class _GatherSpread(_Riding):
    def __init__(self, xs, rows=None, into=None):
        n = len(xs)
        self.rows = rows
        self.ins = list(xs) + list(into or [])
        self.out_shapes = [jax.ShapeDtypeStruct((N_DEV,) + a.shape, a.dtype) for a in xs]
        self.aliases = {n + t: t for t in range(n)} if into else {}
        self.sem_shapes = [pltpu.SemaphoreType.DMA((n, 4)), pltpu.SemaphoreType.DMA((n, 4))]

    def copies(self, ins, outs, sems):
        send, recv = sems
        _, _, c, me, sibling, chips = _place()
        targets = [sibling] + [(*chip, c) for chip in chips]
        out, landing = [], []
        for t in range(len(outs)):
            src = _rows_of(ins[t], self.rows)
            for k, to in enumerate(targets):
                out.append(_remote(src, _rows_of(outs[t].at[me], self.rows), send.at[t, k], recv.at[t, k], to))
                theirs = _rows_of(outs[t].at[4 * to[0] + 2 * to[1] + to[2]], self.rows)
                landing.append(_remote(src, theirs, send.at[t, k], recv.at[t, k], to))
        return [], out, landing


class _GatherPass(_Riding):
    def __init__(self, bufs):
        n = len(bufs)
        self.ins = list(bufs)
        self.out_shapes = [jax.ShapeDtypeStruct(b.shape, b.dtype) for b in bufs]
        self.aliases = {t: t for t in range(n)}
        self.sem_shapes = [pltpu.SemaphoreType.DMA((n, 4)), pltpu.SemaphoreType.DMA((n, 4))]

    def copies(self, ins, outs, sems):
        send, recv = sems
        x, y, c, me, sibling, chips = _place()
        out, landing = [], []
        for t in range(len(outs)):
            for j, (px, py) in enumerate(chips + [(x, y)]):
                held = outs[t].at[4 * px + 2 * py + c] if j < 3 else outs[t].at[4 * x + 2 * y + 1 - c]
                lands = outs[t].at[4 * px + 2 * py + 1 - c] if j < 3 else outs[t].at[me]
                out.append(_remote(held, held, send.at[t, j], recv.at[t, j], sibling))
                landing.append(_remote(held, lands, send.at[t, j], recv.at[t, j], sibling))
        return [], out, landing


class _SiblingSwap(_Riding):
    def __init__(self, xs):
        n = len(xs)
        self.ins = list(xs)
        self.out_shapes = [jax.ShapeDtypeStruct((N_DEV // 2,) + a.shape[1:], a.dtype) for a in xs]
        self.sem_shapes = [pltpu.SemaphoreType.DMA((n, 4)), pltpu.SemaphoreType.DMA((n, 4))]

    def copies(self, ins, outs, sems):
        send, recv = sems
        _, _, c, _, sibling, _ = _place()
        out = [_remote(ins[t].at[2 * q + 1 - c], outs[t].at[q], send.at[t, q], recv.at[t, q], sibling)
               for t in range(len(ins)) for q in range(N_DEV // 2)]
        return [], out, out


class _ChipSend(_Riding):
    def __init__(self, ps, rows=None, into=None):
        n = len(ps)
        self.rows = rows
        self.ins = list(ps) + list(into or [])
        self.out_shapes = [jax.ShapeDtypeStruct((3,) + a.shape[1:], a.dtype) for a in ps]
        self.aliases = {n + t: t for t in range(n)} if into else {}
        self.sem_shapes = [pltpu.SemaphoreType.DMA((n, 3)), pltpu.SemaphoreType.DMA((n, 3))]

    def copies(self, ins, outs, sems):
        send, recv = sems
        _, _, c, _, _, chips = _place()
        out = [_remote(_rows_of(ins[t].at[2 * px + py], self.rows), _rows_of(outs[t].at[j], self.rows),
                       send.at[t, j], recv.at[t, j], (px, py, c))
               for t in range(len(outs)) for j, (px, py) in enumerate(chips)]
        return [], out, out


class _Bundle(_Riding):
    def __init__(self, comms):
        self.comms = comms
        self.ins = [a for cm in comms for a in cm.ins]
        self.out_shapes = [s for cm in comms for s in cm.out_shapes]
        self.sem_shapes = [s for cm in comms for s in cm.sem_shapes]
        self.aliases = {}
        i0 = o0 = 0
        for cm in comms:
            self.aliases.update({i0 + i: o0 + j for i, j in cm.aliases.items()})
            i0, o0 = i0 + len(cm.ins), o0 + len(cm.out_shapes)

    def copies(self, ins, outs, sems):
        local, out, landing = [], [], []
        i0 = o0 = s0 = 0
        for cm in self.comms:
            i1, o1, s1 = i0 + len(cm.ins), o0 + len(cm.out_shapes), s0 + len(cm.sem_shapes)
            a, b, c = cm.copies(ins[i0:i1], outs[o0:o1], sems[s0:s1])
            local, out, landing = local + a, out + b, landing + c
            i0, o0, s0 = i1, o1, s1
        return local, out, landing


class _Rides:
    def __init__(self):
        self.plan, self.done, self.aboard = {}, {}, {}

    def put(self, host, key, make):
        self.plan.setdefault(host, []).append((key, make))

    def board(self, host):
        if host not in self.plan:
            return None
        self.aboard[host] = [make() for _, make in self.plan[host]]
        return _Bundle(self.aboard[host])

    def land(self, host, results):
        o0 = 0
        for (key, _), cm in zip(self.plan[host], self.aboard[host]):
            self.done[key] = list(results[o0:o0 + len(cm.out_shapes)])
            o0 += len(cm.out_shapes)


def _pallas(body, *, name, grid, out_shape, in_specs, out_specs, operands, semantics, scratch_shapes=(), rides=None):
    comm = rides.board(name) if rides is not None else None
    if comm is None:
        return pl.pallas_call(
            body, name=name, grid=grid, out_shape=list(out_shape), in_specs=list(in_specs),
            out_specs=list(out_specs), scratch_shapes=list(scratch_shapes), compiler_params=_params(*semantics),
        )(*operands)
    n_in, n_out, n_scr = len(in_specs), len(out_shape), len(scratch_shapes)
    n_ci, n_co = len(comm.ins), len(comm.out_shapes)

    def hosted(*refs):
        cuts = np.cumsum([0, n_in, n_ci, n_out, n_co, n_scr])
        ins, c_ins, outs, c_outs, scr = (refs[cuts[i]:cuts[i + 1]] for i in range(5))
        sems = refs[cuts[5]:]
        ids = [pl.program_id(a) for a in range(len(grid))]
        first = functools.reduce(jnp.logical_and, [i == 0 for i in ids])
        last = functools.reduce(jnp.logical_and, [i == g - 1 for i, g in zip(ids, grid)])

        @pl.when(first)
        def _():
            comm.start(c_ins, c_outs, sems)

        body(*ins, *outs, *scr)

        @pl.when(last)
        def _():
            comm.wait(c_ins, c_outs, sems)

    results = pl.pallas_call(
        hosted, name=name, grid=grid, out_shape=list(out_shape) + comm.out_shapes,
        in_specs=list(in_specs) + [_HBM] * n_ci, out_specs=list(out_specs) + [_HBM] * n_co,
        scratch_shapes=list(scratch_shapes) + comm.sem_shapes,
        input_output_aliases={n_in + i: n_out + j for i, j in comm.aliases.items()},
        compiler_params=_params(*["arbitrary"] * len(grid)),
    )(*operands, *comm.ins)
    rides.land(name, results[n_out:])
    return results[:n_out]


def _alone(rides, name):
    comm = rides.board(name)

    def body(*refs):
        n_ci, n_co = len(comm.ins), len(comm.out_shapes)
        ins, outs, sems = refs[:n_ci], refs[n_ci:n_ci + n_co], refs[n_ci + n_co:]
        comm.start(ins, outs, sems)
        comm.wait(ins, outs, sems)

    results = pl.pallas_call(
        body, name=name, out_shape=comm.out_shapes, in_specs=[_HBM] * len(comm.ins),
        out_specs=[_HBM] * len(comm.out_shapes), scratch_shapes=comm.sem_shapes,
        input_output_aliases=dict(comm.aliases),
    )(*comm.ins)
    rides.land(name, results)


def _row_tile(rows, cols, itemsize, copies, budget=24 * 1024 * 1024):
    padded = -(-cols // LANES) * LANES
    mult = 8 * (4 // itemsize)
    if rows % mult:
        return rows
    return _pick(rows, max(mult, budget // (copies * padded * itemsize)), mult)


def _sum_leading(x, name):
    n_src, rows, cols = x.shape
    tr = _row_tile(rows, cols, 4, 2 * (n_src + 2))

    def body(x_ref, o_ref):
        acc = x_ref[0].astype(F32)
        for s in range(1, n_src):
            acc = acc + x_ref[s].astype(F32)
        o_ref[...] = acc

    return pl.pallas_call(
        body, name=name, grid=(rows // tr,), out_shape=jax.ShapeDtypeStruct((rows, cols), F32),
        in_specs=[pl.BlockSpec((n_src, tr, cols), lambda i: (0, i, 0))],
        out_specs=pl.BlockSpec((tr, cols), lambda i: (i, 0)), compiler_params=_params("parallel"),
    )(x)


def _pair_sum(buf, theirs, core, name):
    n_q, rows, cols = theirs.shape
    tr = _row_tile(rows, cols, 4, 6)

    def body(core_ref, mine_ref, theirs_ref, o_ref):
        o_ref[...] = (mine_ref[...].astype(F32) + theirs_ref[...].astype(F32)).astype(BF16)

    spec = pl.BlockSpec((None, tr, cols), lambda q, i, core_ref: (q, i, 0))
    return pl.pallas_call(
        body, name=name, out_shape=jax.ShapeDtypeStruct(theirs.shape, BF16),
        grid_spec=pltpu.PrefetchScalarGridSpec(
            num_scalar_prefetch=1, grid=(n_q, rows // tr),
            in_specs=[pl.BlockSpec((None, tr, cols), lambda q, i, core_ref: (2 * q + core_ref[0], i, 0)), spec],
            out_specs=spec),
        compiler_params=_params("parallel", "parallel"),
    )(core, buf, theirs)


def _sum_with_own(recv, pair, chip, name):
    n_src, rows, cols = recv.shape
    tr = _row_tile(rows, cols, 4, 2 * (n_src + 3))

    def body(chip_ref, own_ref, recv_ref, o_ref):
        acc = own_ref[...].astype(F32)
        for s in range(n_src):
            acc = acc + recv_ref[s].astype(F32)
        o_ref[...] = acc

    return pl.pallas_call(
        body, name=name, out_shape=jax.ShapeDtypeStruct((rows, cols), F32),
        grid_spec=pltpu.PrefetchScalarGridSpec(
            num_scalar_prefetch=1, grid=(rows // tr,),
            in_specs=[pl.BlockSpec((None, tr, cols), lambda i, chip_ref: (chip_ref[0], i, 0)),
                      pl.BlockSpec((n_src, tr, cols), lambda i, chip_ref: (0, i, 0))],
            out_specs=pl.BlockSpec((tr, cols), lambda i, chip_ref: (i, 0))),
        compiler_params=_params("parallel"),
    )(chip, pair, recv)


def _mm(a, b, *, name, ta=False, tb=False, tm=1024, tn=512, out_dtype=F32, a_act=None,
        residual=None, gate=None, out_chunk=None, rides=None):
    k_dim, m = (a.shape if ta else a.shape[::-1])
    n, kb = (b.shape if tb else b.shape[::-1])
    assert kb == k_dim, (a.shape, b.shape, ta, tb)
    tm, tn = _pick(m, tm, 128), _pick(out_chunk or n, tn, 128)
    ca, cb = (0 if ta else 1), (1 if tb else 0)
    a_spec = pl.BlockSpec((k_dim, tm), lambda i, j: (0, i)) if ta else pl.BlockSpec((tm, k_dim), lambda i, j: (i, 0))
    b_spec = pl.BlockSpec((tn, k_dim), lambda i, j: (j, 0)) if tb else pl.BlockSpec((k_dim, tn), lambda i, j: (0, j))
    mn_spec = pl.BlockSpec((tm, tn), lambda i, j: (i, j))
    if out_chunk:
        per = out_chunk // tn
        o_spec = pl.BlockSpec((None, tm, tn), lambda i, j: (j // per, i, j % per))
        out_shape = jax.ShapeDtypeStruct((n // out_chunk, m, out_chunk), out_dtype)
    else:
        o_spec = mn_spec
        out_shape = jax.ShapeDtypeStruct((m, n), out_dtype)
    operands, in_specs = [a, b], [a_spec, b_spec]
    for extra in (gate, residual):
        if extra is not None:
            operands.append(extra)
            in_specs.append(mn_spec)

    def body(*refs):
        a_ref, b_ref, o_ref = refs[0], refs[1], refs[-1]
        extras = list(refs[2:-1])
        gate_ref = extras.pop(0) if gate is not None else None
        res_ref = extras.pop(0) if residual is not None else None
        av = a_ref[...].astype(BF16)
        if a_act == "relu2":
            av = jnp.square(jnp.maximum(av, jnp.zeros_like(av)))
        r = _dot(av, b_ref[...].astype(BF16), ca, cb)
        if gate_ref is not None:
            r = r * (2.0 * jnp.maximum(gate_ref[...].astype(F32), 0.0))
        if res_ref is not None:
            r = r + res_ref[...].astype(F32)
        o_ref[...] = r.astype(out_dtype)

    return _pallas(body, name=name, grid=(m // tm, n // tn), out_shape=[out_shape], in_specs=in_specs,
                   out_specs=[o_spec], operands=operands, semantics=("parallel", "arbitrary"), rides=rides)[0]


def _rmsnorm_fwd(xs, seg_widths, g, name, tm=256):
    t_len = xs[0].shape[0]
    width = sum(x.shape[1] for x in xs)
    tm = _pick(t_len, tm, 16)
    n = len(xs)

    def body(*refs):
        x_refs, g_ref, o_ref = refs[:n], refs[n], refs[n + 1]
        col = 0
        for x_ref, widths in zip(x_refs, seg_widths):
            off = 0
            for w in widths:
                xv = x_ref[:, off:off + w].astype(F32)
                r = lax.rsqrt(jnp.mean(xv * xv, axis=1, keepdims=True) + NORM_EPS)
                o_ref[:, col:col + w] = (xv * r * g_ref[:, col:col + w]).astype(BF16)
                off += w
                col += w

    return pl.pallas_call(
        body, name=name, grid=(t_len // tm,),
        out_shape=jax.ShapeDtypeStruct((t_len, width), BF16),
        in_specs=[pl.BlockSpec((tm, x.shape[1]), lambda i: (i, 0)) for x in xs]
        + [pl.BlockSpec((1, width), lambda i: (0, 0))],
        out_specs=pl.BlockSpec((tm, width), lambda i: (i, 0)),
        compiler_params=_params("parallel"),
    )(*xs, g)


def _rmsnorm_bwd(xs, seg_widths, g, dh, residuals, name, tm=256, bf16_copy=False):
    t_len = xs[0].shape[0]
    width = sum(x.shape[1] for x in xs)
    tm = _pick(t_len, tm, 8)
    n = len(xs)
    has_res = [r is not None for r in residuals]
    res_ops = [r for r in residuals if r is not None]

    def body(*refs):
        x_refs, g_ref, dh_ref = refs[:n], refs[n], refs[n + 1]
        res_refs = list(refs[n + 2:n + 2 + len(res_ops)])
        dx_refs = refs[n + 2 + len(res_ops):n + 2 + len(res_ops) + n]
        dg_ref = refs[n + 2 + len(res_ops) + n]
        copy_refs = refs[n + 3 + len(res_ops) + n:]
        first = pl.program_id(0) == 0
        col = 0
        for idx, (x_ref, widths) in enumerate(zip(x_refs, seg_widths)):
            res_ref = res_refs.pop(0) if has_res[idx] else None
            off = 0
            for w in widths:
                xv = x_ref[:, off:off + w].astype(F32)
                r = lax.rsqrt(jnp.mean(xv * xv, axis=1, keepdims=True) + NORM_EPS)
                xh = xv * r
                dhv = dh_ref[:, col:col + w].astype(F32)
                gd = dhv * g_ref[:, col:col + w]
                dx = r * (gd - xh * jnp.mean(gd * xh, axis=1, keepdims=True))
                if res_ref is not None:
                    dx = dx + res_ref[:, off:off + w]
                dx_refs[idx][:, off:off + w] = dx
                if bf16_copy:
                    copy_refs[idx][:, off:off + w] = dx.astype(BF16)
                part = jnp.sum(dhv * xh, axis=0, keepdims=True)

                @pl.when(first)
                def _(part=part, col=col, w=w):
                    dg_ref[:, col:col + w] = part

                @pl.when(jnp.logical_not(first))
                def _(part=part, col=col, w=w):
                    dg_ref[:, col:col + w] += part
                off += w
                col += w

    outs = pl.pallas_call(
        body, name=name, grid=(t_len // tm,),
        out_shape=[jax.ShapeDtypeStruct(x.shape, F32) for x in xs] + [jax.ShapeDtypeStruct((1, width), F32)]
        + ([jax.ShapeDtypeStruct(x.shape, BF16) for x in xs] if bf16_copy else []),
        in_specs=[pl.BlockSpec((tm, x.shape[1]), lambda i: (i, 0)) for x in xs]
        + [pl.BlockSpec((1, width), lambda i: (0, 0)), pl.BlockSpec((tm, width), lambda i: (i, 0))]
        + [pl.BlockSpec((tm, r.shape[1]), lambda i: (i, 0)) for r in res_ops],
        out_specs=[pl.BlockSpec((tm, x.shape[1]), lambda i: (i, 0)) for x in xs]
        + [pl.BlockSpec((1, width), lambda i: (0, 0))]
        + ([pl.BlockSpec((tm, x.shape[1]), lambda i: (i, 0)) for x in xs] if bf16_copy else []),
        compiler_params=_params("arbitrary"),
    )(*xs, g, dh, *res_ops)
    return outs[:n], outs[n], outs[n + 1:]


def _loss_head(x, g, target, name, tm=256):
    t_len, d = x.shape
    tm = _pick(t_len, tm, 8)

    def body(x_ref, g_ref, t_ref, dx_ref, dg_ref, loss_ref, dxb_ref):
        first = pl.program_id(0) == 0
        xv = x_ref[...]
        r = lax.rsqrt(jnp.mean(xv * xv, axis=1, keepdims=True) + NORM_EPS)
        xh = xv * r
        gv = g_ref[...]
        err = xh * gv - t_ref[...]
        part_loss = 0.5 * jnp.sum(jnp.mean(err * err, axis=1, keepdims=True), axis=0, keepdims=True)
        dy = err * (1.0 / d)
        gd = dy * gv
        dx = r * (gd - xh * jnp.mean(gd * xh, axis=1, keepdims=True))
        dx_ref[...] = dx
        dxb_ref[...] = dx.astype(BF16)
        part_g = jnp.sum(dy * xh, axis=0, keepdims=True)
        part_loss = jnp.broadcast_to(part_loss, (1, LANES))

        @pl.when(first)
        def _():
            dg_ref[...] = part_g
            loss_ref[...] = part_loss

        @pl.when(jnp.logical_not(first))
        def _():
            dg_ref[...] += part_g
            loss_ref[...] += part_loss

    return pl.pallas_call(
        body, name=name, grid=(t_len // tm,),
        out_shape=[jax.ShapeDtypeStruct((t_len, d), F32), jax.ShapeDtypeStruct((1, d), F32),
                   jax.ShapeDtypeStruct((1, LANES), F32), jax.ShapeDtypeStruct((t_len, d), BF16)],
        in_specs=[pl.BlockSpec((tm, d), lambda i: (i, 0)), pl.BlockSpec((1, d), lambda i: (0, 0)),
                  pl.BlockSpec((tm, d), lambda i: (i, 0))],
        out_specs=[pl.BlockSpec((tm, d), lambda i: (i, 0)), pl.BlockSpec((1, d), lambda i: (0, 0)),
                   pl.BlockSpec((1, LANES), lambda i: (0, 0)), pl.BlockSpec((tm, d), lambda i: (i, 0))],
        compiler_params=_params("arbitrary"),
    )(x, g, target)


def _alibi_slope(h, n_heads):
    return jnp.exp(jnp.full((1, 1), -8.0 * math.log(2.0) / n_heads, F32) * (h + 1).astype(F32))


def _attn_tiles(d, w):
    return ATTN_BLOCK * d, (w if d == 1 else LANES)


def _residue_rows(r, d):
    return pl.ds(r, ATTN_BLOCK, stride=d) if d > 1 else pl.ds(0, ATTN_BLOCK)


def _attn_masks(first_block):
    i = lax.broadcasted_iota(jnp.int32, (2 * ATTN_BLOCK, 2 * ATTN_BLOCK), 0) % ATTN_BLOCK
    j = lax.broadcasted_iota(jnp.int32, (2 * ATTN_BLOCK, 2 * ATTN_BLOCK), 1)
    delta = i - j + ATTN_BLOCK
    valid = jnp.logical_and(delta >= 0, delta <= ATTN_BLOCK)
    valid = jnp.logical_and(valid, jnp.logical_or(j >= ATTN_BLOCK, jnp.logical_not(first_block)))
    return valid, delta.astype(F32)


def _stack_heads(x, masks):
    zero = jnp.zeros_like(x)
    return jnp.concatenate([jnp.where(masks[0], x, zero), jnp.where(masks[1], x, zero)], axis=0)


def _unstack_heads(x2, masks):
    return jnp.where(masks[0], x2[:ATTN_BLOCK], x2[ATTN_BLOCK:])


def _pair_slopes(first_head, p, n_heads, d):
    row = lax.broadcasted_iota(jnp.int32, (2 * ATTN_BLOCK, 1), 0)
    sa, sb = (_alibi_slope(first_head + 2 * p + hh, n_heads) * d for hh in range(2))
    return jnp.where(row < ATTN_BLOCK, sa, sb)


def _head_lane_masks():
    lane = lax.broadcasted_iota(jnp.int32, (ATTN_BLOCK, LANES), 1)
    return [lane < HEAD_DIM, lane >= HEAD_DIM]


def _attn_branch_fwd(proj, w, dilation, n_heads, name, rides=None):
    t_len = proj.shape[0]
    d = dilation
    rows, lw = _attn_tiles(d, w)
    nb = t_len // rows
    n_pairs = lw // LANES
    per = w // lw
    scale = HEAD_DIM ** -0.5

    def body(q_ref, kp_ref, kc_ref, vp_ref, vc_ref, o_ref, lse_ref):
        first_head = pl.program_id(0) * (2 * n_pairs)
        first_block = pl.program_id(1) == 0
        valid, delta = _attn_masks(first_block)
        masks = _head_lane_masks()
        ones = jnp.ones((2 * ATTN_BLOCK, LANES), BF16)
        for p in range(n_pairs):
            cols = pl.ds(p * LANES, LANES)
            bias = _pair_slopes(first_head, p, n_heads, d) * delta
            for r in range(d):
                rs = _residue_rows(r, d)
                q2 = _stack_heads((q_ref[rs, cols] * scale).astype(BF16), masks)
                k2 = jnp.concatenate([kp_ref[rs, cols], kc_ref[rs, cols]], axis=0).astype(BF16)
                v2 = jnp.concatenate([vp_ref[rs, cols], vc_ref[rs, cols]], axis=0).astype(BF16)
                s = jnp.where(valid, _dot(q2, k2, 1, 1) - bias, NEG_INF)
                m = jnp.max(s, axis=1, keepdims=True)
                pr = jnp.exp(s - m).astype(BF16)
                den = _dot(pr, ones, 1, 0)
                o_ref[rs, cols] = _unstack_heads(_dot(pr, v2, 1, 0) / den, masks)
                lse_ref[rs, cols] = _unstack_heads(m + jnp.log(den), masks)

    def spec(which, prev):
        if prev:
            return pl.BlockSpec((rows, lw), lambda b, n: (jnp.maximum(n - 1, 0), which * per + b))
        return pl.BlockSpec((rows, lw), lambda b, n: (n, which * per + b))

    o_spec = pl.BlockSpec((rows, lw), lambda b, n: (n, b))
    return _pallas(
        body, name=name, grid=(per, nb), out_shape=[jax.ShapeDtypeStruct((t_len, w), F32)] * 2,
        in_specs=[spec(0, False), spec(1, True), spec(1, False), spec(2, True), spec(2, False)],
        out_specs=[o_spec, o_spec], operands=[proj] * 5, semantics=("parallel", "parallel"), rides=rides)


def _attn_combine(outs, lses, name, tm=512):
    t_len, w = outs[0].shape
    tm = _pick(t_len, tm, 8)
    nbr = len(outs)

    def body(*refs):
        o_refs, l_refs = refs[:nbr], refs[nbr:2 * nbr]
        out_ref, lse_ref = refs[2 * nbr:]
        ls = [r[...] for r in l_refs]
        m = functools.reduce(jnp.maximum, ls)
        es = [jnp.exp(l - m) for l in ls]
        den = functools.reduce(lambda a, b: a + b, es)
        num = functools.reduce(lambda a, b: a + b, [e * r[...] for e, r in zip(es, o_refs)])
        out_ref[...] = num / den
        lse_ref[...] = m + jnp.log(den)

    spec = pl.BlockSpec((tm, w), lambda i: (i, 0))
    return pl.pallas_call(
        body, name=name, grid=(t_len // tm,),
        out_shape=[jax.ShapeDtypeStruct((t_len, w), F32)] * 2,
        in_specs=[spec] * (2 * nbr), out_specs=[spec, spec],
        compiler_params=_params("parallel"),
    )(*outs, *lses)


def _attn_branch_bwd(proj, w, out, lse, dout, dilation, n_heads, name, acc=None, rides=None):
    t_len = proj.shape[0]
    d = dilation
    rows, lw = _attn_tiles(d, w)
    nb = t_len // rows
    n_pairs = lw // LANES
    per = w // lw
    scale = HEAD_DIM ** -0.5
    n_acc = 0 if acc is None else 3

    def body(*refs):
        q_ref, kp_ref, kc_ref, vp_ref, vc_ref, out_ref, lse_ref, do_ref = refs[:8]
        acc_refs = refs[8:8 + n_acc]
        dq_ref, dk_ref, dv_ref, dk_carry, dv_carry = refs[8 + n_acc:]
        first_head = pl.program_id(0) * (2 * n_pairs)
        n = pl.program_id(1)
        first_block = n == 0
        valid, dist = _attn_masks(first_block)
        masks = _head_lane_masks()

        def plus(value, idx, *where):
            return value + acc_refs[idx][where] if n_acc else value

        @pl.when(first_block)
        def _():
            dk_carry[...] = jnp.zeros_like(dk_carry)
            dv_carry[...] = jnp.zeros_like(dv_carry)

        @pl.when(n < nb)
        def _():
            for p in range(n_pairs):
                cols = pl.ds(p * LANES, LANES)
                bias = _pair_slopes(first_head, p, n_heads, d) * dist
                for r in range(d):
                    rs = _residue_rows(r, d)
                    q2 = _stack_heads((q_ref[rs, cols] * scale).astype(BF16), masks)
                    k2 = jnp.concatenate([kp_ref[rs, cols], kc_ref[rs, cols]], axis=0).astype(BF16)
                    v2 = jnp.concatenate([vp_ref[rs, cols], vc_ref[rs, cols]], axis=0).astype(BF16)
                    do = do_ref[rs, cols]
                    do2 = _stack_heads(do.astype(BF16), masks)
                    do_out = do * out_ref[rs, cols]
                    lse_all = lse_ref[rs, cols]
                    delta = jnp.concatenate([jnp.sum(jnp.where(masks[hh], do_out, 0.0), axis=1, keepdims=True)
                                             for hh in range(2)], axis=0)
                    lse2 = jnp.concatenate([jnp.max(jnp.where(masks[hh], lse_all, NEG_INF), axis=1, keepdims=True)
                                            for hh in range(2)], axis=0)
                    s = jnp.where(valid, _dot(q2, k2, 1, 1) - bias, NEG_INF)
                    pr = jnp.exp(s - lse2)
                    ds = (pr * (_dot(do2, v2, 1, 1) - delta)).astype(BF16)
                    dq = _unstack_heads(_dot(ds, k2, 1, 0), masks)
                    dk2 = _dot(ds, q2, 0, 0)
                    dv2 = _dot(pr.astype(BF16), do2, 0, 0)
                    dq_ref[rs, cols] = plus(dq * scale, 0, rs, cols)
                    dk_ref[rs, cols] = plus(dk_carry[r, :, cols] + dk2[:ATTN_BLOCK], 1, rs, cols)
                    dv_ref[rs, cols] = plus(dv_carry[r, :, cols] + dv2[:ATTN_BLOCK], 2, rs, cols)
                    dk_carry[r, :, cols] = dk2[ATTN_BLOCK:]
                    dv_carry[r, :, cols] = dv2[ATTN_BLOCK:]

        @pl.when(n == nb)
        def _():
            for r in range(d):
                rs = _residue_rows(r, d)
                dk_ref[rs, :] = plus(dk_carry[r], 1, rs, slice(None))
                dv_ref[rs, :] = plus(dv_carry[r], 2, rs, slice(None))

    def qkv_spec(which, shift):
        return pl.BlockSpec((rows, lw), lambda b, n: (jnp.clip(n - shift, 0, nb - 1), which * per + b))

    q_like = pl.BlockSpec((rows, lw), lambda b, n: (jnp.minimum(n, nb - 1), b))
    k_like = pl.BlockSpec((rows, lw), lambda b, n: (jnp.maximum(n - 1, 0), b))
    return _pallas(
        body, name=name, grid=(per, nb + 1), out_shape=[jax.ShapeDtypeStruct((t_len, w), F32)] * 3,
        in_specs=[qkv_spec(0, 0), qkv_spec(1, 1), qkv_spec(1, 0), qkv_spec(2, 1), qkv_spec(2, 0),
                  q_like, q_like, q_like] + [q_like, k_like, k_like][:n_acc],
        out_specs=[q_like, k_like, k_like], operands=[proj] * 5 + [out, lse, dout, *(acc or ())],
        scratch_shapes=[pltpu.VMEM((d, ATTN_BLOCK, lw), F32), pltpu.VMEM((d, ATTN_BLOCK, lw), F32)],
        semantics=("parallel", "arbitrary"), rides=rides)


def _shift_down(u, s):
    if s == 0:
        return u
    row = lax.broadcasted_iota(jnp.int32, u.shape, 0)
    return jnp.where(row >= s, pltpu.roll(u, s, 0), 0.0)


def _shift_up(u, s):
    if s == 0:
        return u
    n = u.shape[0]
    row = lax.broadcasted_iota(jnp.int32, u.shape, 0)
    return jnp.where(row < n - s, pltpu.roll(u, n - s, 0), 0.0)


def _conv_fwd(u, col0, w, b, name):
    t_len, ch = u.shape[0], w.shape[1]
    blk0 = col0 // LANES

    def body(u_ref, w_ref, b_ref, o_ref):
        uv = u_ref[...]
        pre = b_ref[...] + jnp.zeros_like(uv)
        for k in range(SSD_CONV):
            pre = pre + w_ref[k:k + 1, :] * _shift_down(uv, SSD_CONV - 1 - k)
        o_ref[...] = pre * jax.nn.sigmoid(pre)

    return pl.pallas_call(
        body, name=name, grid=(ch // LANES,),
        out_shape=jax.ShapeDtypeStruct((t_len, ch), F32),
        in_specs=[pl.BlockSpec((t_len, LANES), lambda j: (0, blk0 + j)),
                  pl.BlockSpec((SSD_CONV, LANES), lambda j: (0, j)), pl.BlockSpec((1, LANES), lambda j: (0, j))],
        out_specs=pl.BlockSpec((t_len, LANES), lambda j: (0, j)),
        compiler_params=_params("parallel"),
    )(u, w, b)


def _conv_bwd(u, col0, w, b, dact, name):
    t_len, ch = u.shape[0], w.shape[1]
    blk0 = col0 // LANES

    def body(u_ref, w_ref, b_ref, da_ref, du_ref, dw_ref, db_ref):
        uv = u_ref[...]
        shifted = [_shift_down(uv, SSD_CONV - 1 - k) for k in range(SSD_CONV)]
        pre = b_ref[...] + jnp.zeros_like(uv)
        for k in range(SSD_CONV):
            pre = pre + w_ref[k:k + 1, :] * shifted[k]
        sig = jax.nn.sigmoid(pre)
        dpre = da_ref[...] * (sig * (1.0 + pre * (1.0 - sig)))
        du = jnp.zeros_like(uv)
        for k in range(SSD_CONV):
            du = du + w_ref[k:k + 1, :] * _shift_up(dpre, SSD_CONV - 1 - k)
            dw_ref[k:k + 1, :] = jnp.sum(dpre * shifted[k], axis=0, keepdims=True)
        du_ref[...] = du
        db_ref[...] = jnp.sum(dpre, axis=0, keepdims=True)

    col = pl.BlockSpec((t_len, LANES), lambda j: (0, j))
    w_spec = pl.BlockSpec((SSD_CONV, LANES), lambda j: (0, j))
    b_spec = pl.BlockSpec((1, LANES), lambda j: (0, j))
    return pl.pallas_call(
        body, name=name, grid=(ch // LANES,),
        out_shape=[jax.ShapeDtypeStruct((t_len, ch), F32), jax.ShapeDtypeStruct((SSD_CONV, ch), F32),
                   jax.ShapeDtypeStruct((1, ch), F32)],
        in_specs=[pl.BlockSpec((t_len, LANES), lambda j: (0, blk0 + j)), w_spec, b_spec, col],
        out_specs=[col, w_spec, b_spec],
        compiler_params=_params("parallel"),
    )(u, w, b, dact)


def _cumsum_rows(v):
    n = v.shape[0]
    row = lax.broadcasted_iota(jnp.int32, v.shape, 0)
    s = 1
    while s < n:
        v = v + jnp.where(row >= s, pltpu.roll(v, s, 0), 0.0)
        s *= 2
    return v


def _rev_cumsum_rows(v):
    n = v.shape[0]
    row = lax.broadcasted_iota(jnp.int32, v.shape, 0)
    s = 1
    while s < n:
        v = v + jnp.where(row < n - s, pltpu.roll(v, n - s, 0), 0.0)
        s *= 2
    return v


def _head_selector(heads, width):
    j = lax.broadcasted_iota(jnp.int32, (LANES, width), 0)
    lane = lax.broadcasted_iota(jnp.int32, (LANES, width), 1)
    return jnp.where(jnp.logical_and(lane // HEAD_DIM == j, j < heads), 1.0, 0.0).astype(BF16)


class _SsdChunk:
    def __init__(self, dtraw_ref, bias_ref, alog_ref, xs_ref, b_ref, c_ref, heads):
        q = SSD_CHUNK
        width = heads * HEAD_DIM
        lane = lax.broadcasted_iota(jnp.int32, (q, LANES), 1)
        self.head_lanes = lane < heads
        lane1 = lax.broadcasted_iota(jnp.int32, (1, LANES), 1)
        self.a = jnp.where(lane1 < heads, -jnp.exp(alog_ref[...]), 0.0)
        self.dt_arg = dtraw_ref[...] + bias_ref[...]
        self.dt = jnp.where(self.head_lanes, jax.nn.softplus(self.dt_arg), 0.0)
        self.cum = _cumsum_rows(self.dt * self.a)
        self.cum_t = self.cum.T
        last = self.cum[q - 1:q, :]
        self.sel = _head_selector(heads, width)
        self.expand = lambda v: _dot_exact(v, self.sel, 1, 0)
        self.segsum = lambda v: _dot_exact(v, self.sel, 1, 1)
        self.e_exp = self.expand(jnp.exp(self.cum))
        self.d_exp = self.expand(jnp.exp(last - self.cum))
        self.elast_exp = self.e_exp[q - 1:q, :]
        self.dt_exp = self.expand(self.dt)
        self.xs = xs_ref[...]
        self.x = self.xs * self.dt_exp
        self.xb = self.x.astype(BF16)
        self.bb = b_ref[...].astype(BF16)
        self.cb = c_ref[...].astype(BF16)
        self.cbm = _dot(self.cb, self.bb, 1, 1)
        li = lax.broadcasted_iota(jnp.int32, (q, q), 0)
        si = lax.broadcasted_iota(jnp.int32, (q, q), 1)
        self.tri = li >= si
        hl = lax.broadcasted_iota(jnp.int32, (q, LANES), 1)
        self.pair_masks = [hl < HEAD_DIM, hl >= HEAD_DIM]

    def decay(self, j):
        diff = self.cum[:, j:j + 1] - self.cum_t[j:j + 1, :]
        return jnp.exp(jnp.where(self.tri, diff, NEG_INF))


def _ssd_specs(t_len, heads, n_chunks, xbc_cols, rev):
    q, gw = SSD_CHUNK, heads * HEAD_DIM
    ssd_w = SSD_GROUPS * gw
    b_blk = ssd_w // SSD_STATE
    ch = (lambda c: n_chunks - 1 - c) if rev else (lambda c: c)
    return dict(
        dtraw=pl.BlockSpec((None, q, LANES), lambda g, c: (g, ch(c), 0)),
        small=pl.BlockSpec((None, 1, LANES), lambda g, c: (g, 0, 0)),
        dsk=pl.BlockSpec((None, 1, gw), lambda g, c: (g, 0, 0)),
        xs=pl.BlockSpec((q, gw), lambda g, c: (ch(c), g)),
        b=pl.BlockSpec((q, SSD_STATE), lambda g, c: (ch(c), b_blk + g)),
        c=pl.BlockSpec((q, SSD_STATE), lambda g, c: (ch(c), b_blk + SSD_GROUPS + g)),
        z=pl.BlockSpec((q, gw), lambda g, c: (ch(c), 3 * SSD_GROUPS + g)),
        tok=pl.BlockSpec((q, gw), lambda g, c: (ch(c), g)),
        state=pl.BlockSpec((None, SSD_STATE, gw), lambda g, c: (ch(c), 0, g)),
        bc=pl.BlockSpec((q, SSD_STATE), lambda g, c: (ch(c), g)),
    )


def _ssd_fwd(xbc, qkvz, dtraw_g, bias_g, alog_g, dsk_exp, heads, name):
    t_len = xbc.shape[0]
    q, gw = SSD_CHUNK, heads * HEAD_DIM
    n_chunks = t_len // q
    ssd_w = SSD_GROUPS * gw
    sp = _ssd_specs(t_len, heads, n_chunks, xbc.shape[1], rev=False)

    def body(dtraw_ref, bias_ref, alog_ref, dsk_ref, xs_ref, b_ref, c_ref, z_ref,
             yg_ref, ypre_ref, st_ref, s_scr):
        @pl.when(pl.program_id(1) == 0)
        def _():
            s_scr[...] = jnp.zeros_like(s_scr)

        k = _SsdChunk(dtraw_ref, bias_ref, alog_ref, xs_ref, b_ref, c_ref, heads)
        s_prev = s_scr[...]
        st_ref[...] = s_prev
        y_off = k.e_exp * _dot(k.cb, s_prev.astype(BF16), 1, 0)
        parts = []
        for p in range(heads // 2):
            xp = k.xb[:, p * LANES:(p + 1) * LANES]
            acc = jnp.zeros((q, LANES), F32)
            for hh in range(2):
                m = (k.cbm * k.decay(2 * p + hh)).astype(BF16)
                acc = acc + _dot(m, jnp.where(k.pair_masks[hh], xp, jnp.zeros_like(xp)), 1, 0)
            parts.append(acc)
        y = jnp.concatenate(parts, axis=1) + y_off
        xd = (k.x * k.d_exp).astype(BF16)
        s_scr[...] = k.elast_exp * s_prev + _dot(k.bb, xd, 0, 0)
        y_pre = y + dsk_ref[...] * k.xs
        zv = z_ref[...]
        ypre_ref[...] = y_pre
        yg_ref[...] = y_pre * (zv * jax.nn.sigmoid(zv))

    return pl.pallas_call(
        body, name=name, grid=(SSD_GROUPS, n_chunks),
        out_shape=[jax.ShapeDtypeStruct((t_len, ssd_w), F32), jax.ShapeDtypeStruct((t_len, ssd_w), F32),
                   jax.ShapeDtypeStruct((n_chunks, SSD_STATE, ssd_w), F32)],
        in_specs=[sp["dtraw"], sp["small"], sp["small"], sp["dsk"], sp["xs"], sp["b"], sp["c"], sp["z"]],
        out_specs=[sp["tok"], sp["tok"], sp["state"]],
        scratch_shapes=[pltpu.VMEM((SSD_STATE, gw), F32)],
        compiler_params=_params("parallel", "arbitrary"),
    )(dtraw_g, bias_g, alog_g, dsk_exp, xbc, xbc, xbc, qkvz)


def _ssd_bwd(xbc, qkvz, dtraw_g, bias_g, alog_g, dsk_exp, ypre, states, dyg, heads, name):
    t_len = xbc.shape[0]
    q, gw = SSD_CHUNK, heads * HEAD_DIM
    n_chunks = t_len // q
    ssd_w = SSD_GROUPS * gw
    sp = _ssd_specs(t_len, heads, n_chunks, xbc.shape[1], rev=True)

    def body(dtraw_ref, bias_ref, alog_ref, dsk_ref, xs_ref, b_ref, c_ref, z_ref, ypre_ref, st_ref, dyg_ref,
             dxs_ref, db_ref, dc_ref, dz_ref, ddt_ref, small_ref, g_scr):
        first = pl.program_id(1) == 0

        @pl.when(first)
        def _():
            g_scr[...] = jnp.zeros_like(g_scr)

        k = _SsdChunk(dtraw_ref, bias_ref, alog_ref, xs_ref, b_ref, c_ref, heads)
        zv = z_ref[...]
        sig = jax.nn.sigmoid(zv)
        dyg = dyg_ref[...]
        y_pre = ypre_ref[...]
        dy = dyg * (zv * sig)
        dz_ref[...] = dyg * y_pre * (sig * (1.0 + zv * (1.0 - sig)))
        dsk = dsk_ref[...]
        g_next = g_scr[...]
        s_prev = st_ref[...]
        sb = s_prev.astype(BF16)
        xd = k.x * k.d_exp
        xdb = xd.astype(BF16)
        gb = g_next.astype(BF16)
        dx_off = k.d_exp * _dot(k.bb, gb, 1, 0)
        dyb = dy.astype(BF16)
        dcb = jnp.zeros((q, q), F32)
        lane = lax.broadcasted_iota(jnp.int32, (q, LANES), 1)
        row_t = lax.broadcasted_iota(jnp.int32, (LANES, q), 0)
        w_rows = jnp.zeros((q, LANES), F32)
        w_cols_t = jnp.zeros((LANES, q), F32)
        parts = []
        for p in range(heads // 2):
            cols = slice(p * LANES, (p + 1) * LANES)
            dyp, xp = dyb[:, cols], k.xb[:, cols]
            acc = jnp.zeros((q, LANES), F32)
            for hh in range(2):
                j = 2 * p + hh
                lm = k.decay(j)
                m32 = k.cbm * lm
                dym = jnp.where(k.pair_masks[hh], dyp, jnp.zeros_like(dyp))
                acc = acc + _dot(m32.astype(BF16), dym, 0, 0)
                dm = _dot(dym, xp, 1, 1)
                dcb = dcb + dm * lm
                wmat = dm * m32
                w_rows = w_rows + jnp.where(lane == j, jnp.sum(wmat, axis=1, keepdims=True), 0.0)
                w_cols_t = w_cols_t + jnp.where(row_t == j, jnp.sum(wmat, axis=0, keepdims=True), 0.0)
            parts.append(acc)
        dx = jnp.concatenate(parts, axis=1) + dx_off
        dcbb = dcb.astype(BF16)
        edy = (k.e_exp * dy).astype(BF16)
        dc_ref[...] = _dot(dcbb, k.bb, 1, 0) + _dot(edy, sb, 1, 1)
        db_ref[...] = _dot(dcbb, k.cb, 0, 0) + _dot(xdb, gb, 1, 1)
        g_scr[...] = k.elast_exp * g_next + _dot(k.cb, edy, 0, 0)

        y_off = k.e_exp * _dot(k.cb, sb, 1, 0)
        dcum = w_rows - w_cols_t.T + k.segsum(dy * y_off)
        t_term = k.segsum(k.x * dx_off)
        gs = jnp.broadcast_to(jnp.sum(g_next * s_prev, axis=0, keepdims=True), (8, gw))
        carried = k.segsum(gs)[0:1, :] * jnp.exp(k.cum[q - 1:q, :])
        dda = _rev_cumsum_rows(dcum) + (_cumsum_rows(t_term) - t_term) + carried
        ddt = jnp.where(k.head_lanes, dda * k.a + k.segsum(dx * k.xs), 0.0)
        ddtraw = ddt * jax.nn.sigmoid(k.dt_arg)
        ddt_ref[...] = ddtraw
        dxs_ref[...] = dx * k.dt_exp + dsk * dy
        ds = jnp.broadcast_to(jnp.sum(dy * k.xs, axis=0, keepdims=True), (8, gw))
        d_alog = jnp.sum(jnp.where(k.head_lanes, dda * k.dt, 0.0), axis=0, keepdims=True) * k.a
        rows8 = lax.broadcasted_iota(jnp.int32, (8, LANES), 0)
        small = jnp.where(rows8 == 0, d_alog, 0.0)
        small = small + jnp.where(rows8 == 1, jnp.sum(ddtraw, axis=0, keepdims=True), 0.0)
        small = small + jnp.where(rows8 == 2, k.segsum(ds)[0:1, :], 0.0)

        @pl.when(first)
        def _():
            small_ref[...] = small

        @pl.when(jnp.logical_not(first))
        def _():
            small_ref[...] += small

    bc_out = sp["bc"]
    return pl.pallas_call(
        body, name=name, grid=(SSD_GROUPS, n_chunks),
        out_shape=[jax.ShapeDtypeStruct((t_len, ssd_w), F32),
                   jax.ShapeDtypeStruct((t_len, SSD_GROUPS * SSD_STATE), F32),
                   jax.ShapeDtypeStruct((t_len, SSD_GROUPS * SSD_STATE), F32),
                   jax.ShapeDtypeStruct((t_len, ssd_w), F32),
                   jax.ShapeDtypeStruct((SSD_GROUPS, t_len, LANES), F32),
                   jax.ShapeDtypeStruct((SSD_GROUPS, 8, LANES), F32)],
        in_specs=[sp["dtraw"], sp["small"], sp["small"], sp["dsk"], sp["xs"], sp["b"], sp["c"], sp["z"],
                  sp["tok"], sp["state"], sp["tok"]],
        out_specs=[sp["tok"], bc_out, bc_out, sp["tok"], sp["dtraw"],
                   pl.BlockSpec((None, 8, LANES), lambda g, c: (g, 0, 0))],
        scratch_shapes=[pltpu.VMEM((SSD_STATE, gw), F32)],
        compiler_params=_params("parallel", "arbitrary"),
    )(dtraw_g, bias_g, alog_g, dsk_exp, xbc, xbc, xbc, qkvz, ypre, states, dyg)


def _adamw(w, g, m, v, name, rides=None):
    n_lead, rows, lanes = w.shape
    tr = _row_tile(rows, lanes, 4, 14)
    c1 = 1.0 / (1.0 - ADAM_B1 ** ADAM_STEP)
    c2 = 1.0 / (1.0 - ADAM_B2 ** ADAM_STEP)

    def body(w_ref, g_ref, m_ref, v_ref, d_ref, nm_ref, nv_ref):
        gv = g_ref[...]
        nm = ADAM_B1 * m_ref[...] + (1.0 - ADAM_B1) * gv
        nv = ADAM_B2 * v_ref[...] + (1.0 - ADAM_B2) * (gv * gv)
        nm_ref[...] = nm
        nv_ref[...] = nv
        d_ref[...] = -ADAM_LR * ((nm * c1) / (jnp.sqrt(nv * c2) + ADAM_EPS) + ADAM_WD * w_ref[...])

    spec = pl.BlockSpec((None, tr, lanes), lambda l, i: (l, i, 0))
    return _pallas(body, name=name, grid=(n_lead, rows // tr), out_shape=[jax.ShapeDtypeStruct(w.shape, F32)] * 3,
                   in_specs=[spec] * 4, out_specs=[spec] * 3, operands=[w, g, m, v],
                   semantics=("parallel", "parallel"), rides=rides)


def _pad_lanes(a, width=LANES):
    return jnp.pad(a, ((0, 0), (0, width - a.shape[1])))


def _group_pad(v, heads):
    return _pad_lanes(v.reshape(SSD_GROUPS, heads))[:, None, :]


def _layer_fwd(x0, p, wt, dims, tag, rides):
    w_attn, heads_g, n_heads, conv_ch = dims["w_attn"], dims["heads_g"], dims["n_heads"], dims["conv_ch"]
    h1 = _rmsnorm_fwd([x0], [[x0.shape[1]]], p["ln1_g"], f"ln1_fwd{tag}")
    proj = _mm(h1, wt("w_in"), name=f"in_proj{tag}", tn=1152, rides=rides)

    outs, lses = [], []
    for d in BRANCH_DILATIONS:
        o, l = _attn_branch_fwd(proj, w_attn, d, n_heads, f"attn_fwd_d{d}{tag}", rides)
        outs.append(o)
        lses.append(l)
    attn, lse = _attn_combine(outs, lses, f"attn_combine{tag}")

    xbc = _conv_fwd(proj, 4 * w_attn, p["conv_w"], p["conv_b"], f"conv_fwd{tag}")
    dt_col = 4 * w_attn + conv_ch
    dtraw_g = jnp.stack([_pad_lanes(proj[:, dt_col + g * heads_g:dt_col + (g + 1) * heads_g])
                         for g in range(SSD_GROUPS)])
    bias_g, alog_g = _group_pad(p["dt_bias"], heads_g), _group_pad(p["a_log"], heads_g)
    dsk_exp = jnp.repeat(p["d_skip"], HEAD_DIM).reshape(SSD_GROUPS, 1, heads_g * HEAD_DIM)
    yg, ypre, states = _ssd_fwd(xbc, proj, dtraw_g, bias_g, alog_g, dsk_exp, heads_g, f"ssd_fwd{tag}")

    gw = heads_g * HEAD_DIM
    mix_g = jnp.concatenate([p["attn_norm_g"], p["ssd_norm_g"]])[None, :]
    mix = _rmsnorm_fwd([attn, yg], [[w_attn], [gw] * SSD_GROUPS], mix_g, f"mix_norm_fwd{tag}")
    x1 = _mm(mix, wt("w_out"), name=f"out_proj{tag}", residual=x0, rides=rides)
    h2 = _rmsnorm_fwd([x1], [[x1.shape[1]]], p["ln2_g"], f"ln2_fwd{tag}")
    u = _mm(h2, wt("w_mlp_in"), name=f"mlp_in{tag}", out_dtype=BF16, tn=1024, rides=rides)
    x2 = _mm(u, wt("w_mlp_out"), name=f"mlp_out{tag}", a_act="relu2", residual=x1, tn=256, rides=rides)
    saved = dict(x0=x0, h1=h1, proj=proj, attn=attn, lse=lse, xbc=xbc, dtraw_g=dtraw_g,
                 bias_g=bias_g, alog_g=alog_g, dsk_exp=dsk_exp, yg=yg, ypre=ypre, states=states, mix=mix,
                 mix_g=mix_g, x1=x1, h2=h2, u=u)
    return x2, saved


def _pair_sums(ex, host, items):
    swapped = ex["rides"].done[("swap", host)]
    core = lax.axis_index("c").astype(jnp.int32).reshape(1)
    for i, (n, l) in enumerate(items):
        ex["pair"][(n, l)] = _pair_sum(ex["bufs"][(n, l)], swapped[i], core, f"pair_sum_{n}_l{l}")


def _layer_bwd(dx2, dx2_b, p, wt, s, dims, l, ex, copy_dx0):
    w_attn, heads_g, n_heads, conv_ch = dims["w_attn"], dims["heads_g"], dims["n_heads"], dims["conv_ch"]
    t_len, d_model = dx2.shape
    gw = heads_g * HEAD_DIM
    h_ssd = heads_g * SSD_GROUPS
    tag, rides, bufs = f"_l{l}", ex["rides"], ex["bufs"]
    du = _mm(dx2_b, wt("w_mlp_out"), name=f"mlp_out_dx{tag}", tb=True, gate=s["u"], out_dtype=BF16, tn=1024,
             rides=rides)
    d_wmo = _mm(s["u"], dx2_b, name=f"mlp_out_dw{tag}", ta=True, a_act="relu2", tm=512, tn=1024, out_dtype=BF16)
    bufs[("w_mlp_out", l)] = d_wmo.reshape(N_DEV, -1, d_model)
    bufs[("w_mlp_in", l)] = _mm(s["h2"], du, name=f"mlp_in_dw{tag}", ta=True, tm=512, tn=1024, out_dtype=BF16,
                                out_chunk=du.shape[1] // N_DEV)
    dh2 = _mm(du, wt("w_mlp_in"), name=f"mlp_in_dx{tag}", tb=True, tn=256, rides=rides)
    _pair_sums(ex, f"mlp_in_dx{tag}", [("w_mlp_out", l), ("w_mlp_in", l)])
    (dx1,), d_ln2, (dx1_b,) = _rmsnorm_bwd([s["x1"]], [[d_model]], p["ln2_g"], dh2, [dx2], f"ln2_bwd{tag}",
                                           bf16_copy=True)
    dmix = _mm(dx1_b, wt("w_out"), name=f"out_proj_dx{tag}", tb=True)
    d_wo = _mm(s["mix"], dx1_b, name=f"out_proj_dw{tag}", ta=True, tm=512, tn=1024, out_dtype=BF16)
    bufs[("w_out", l)] = d_wo.reshape(N_DEV, -1, d_model)
    after_branch = {BRANCH_DILATIONS[0]: [("w_out", l)]}
    (dattn, dyg), d_mix_g, _ = _rmsnorm_bwd([s["attn"], s["yg"]], [[w_attn], [gw] * SSD_GROUPS], s["mix_g"], dmix,
                                           [None, None], f"mix_norm_bwd{tag}")
    dxs, db, dc, dz, ddtraw_g, ssd_small = _ssd_bwd(
        s["xbc"], s["proj"], s["dtraw_g"], s["bias_g"], s["alog_g"], s["dsk_exp"], s["ypre"], s["states"], dyg,
        heads_g, f"ssd_bwd{tag}")
    dxbc = jnp.concatenate([dxs, db, dc], axis=1)
    dxbc_raw, d_conv_w, d_conv_b = _conv_bwd(s["proj"], 4 * w_attn, p["conv_w"], p["conv_b"], dxbc, f"conv_bwd{tag}")
    per_branch = []
    for d in BRANCH_DILATIONS:
        per_branch.append(_attn_branch_bwd(s["proj"], w_attn, s["attn"], s["lse"], dattn, d, n_heads,
                                           f"attn_bwd_d{d}{tag}", None, rides))
        if d in after_branch:
            _pair_sums(ex, f"attn_bwd_d{d}{tag}", after_branch[d])
    acc = [functools.reduce(lambda a, b: a + b, parts) for parts in zip(*per_branch)]
    w_in = wt("w_in")
    in_proj = 4 * w_attn + conv_ch + h_ssd
    pad = jnp.zeros((t_len, w_in.shape[1] - in_proj), F32)
    dproj = jnp.concatenate([*acc, dz, dxbc_raw] + [ddtraw_g[g, :, :heads_g] for g in range(SSD_GROUPS)] + [pad],
                            axis=1).astype(BF16)
    d_win = _mm(s["h1"], dproj, name=f"in_proj_dw{tag}", ta=True, tm=512, tn=1152, out_dtype=BF16, rides=rides)
    bufs[("w_in", l)] = d_win[:, :in_proj].reshape(d_model, N_DEV, -1).transpose(1, 0, 2)
    dh1 = _mm(dproj, w_in, name=f"in_proj_dx{tag}", tb=True, rides=rides)
    _pair_sums(ex, f"in_proj_dx{tag}", [("w_in", l)])
    (dx0,), d_ln1, dx0_b = _rmsnorm_bwd([s["x0"]], [[d_model]], p["ln1_g"], dh1, [dx1], f"ln1_bwd{tag}",
                                        bf16_copy=copy_dx0)

    small = ssd_small[:, :, :heads_g]
    grads = dict(
        ln1_g=d_ln1[0], conv_w=d_conv_w, conv_b=d_conv_b[0],
        a_log=small[:, 0].reshape(h_ssd), dt_bias=small[:, 1].reshape(h_ssd), d_skip=small[:, 2].reshape(h_ssd),
        attn_norm_g=d_mix_g[0, :w_attn], ssd_norm_g=d_mix_g[0, w_attn:], ln2_g=d_ln2[0])
    return dx0, (dx0_b[0] if copy_dx0 else None), grads


_SMALL = ["ln1_g", "conv_w", "conv_b", "dt_bias", "a_log", "d_skip", "attn_norm_g", "ssd_norm_g", "ln2_g"]
_WEIGHTS = ["ln1_g", "w_in", "conv_w", "conv_b", "dt_bias", "a_log", "d_skip", "attn_norm_g", "ssd_norm_g",
            "w_out", "ln2_g", "w_mlp_in", "w_mlp_out", "final_norm_g"]


def _to_rows(a):
    flat = a.reshape(-1)
    rows = -(-flat.shape[0] // LANES)
    rows = -(-rows // 8) * 8
    return jnp.pad(flat, (0, rows * LANES - flat.shape[0])).reshape(rows, LANES)


def kernel(x, ln1_g, w_in, conv_w, conv_b, dt_bias, a_log, d_skip, attn_norm_g, ssd_norm_g, w_out, ln2_g, w_mlp_in, w_mlp_out, final_norm_g, loss_target, m_ln1_g, m_w_in, m_conv_w, m_conv_b, m_dt_bias, m_a_log, m_d_skip, m_attn_norm_g, m_ssd_norm_g, m_w_out, m_ln2_g, m_w_mlp_in, m_w_mlp_out, m_final_norm_g, v_ln1_g, v_w_in, v_conv_w, v_conv_b, v_dt_bias, v_a_log, v_d_skip, v_attn_norm_g, v_ssd_norm_g, v_w_out, v_ln2_g, v_w_mlp_in, v_w_mlp_out, v_final_norm_g):
    w = dict(ln1_g=ln1_g, w_in=w_in, conv_w=conv_w, conv_b=conv_b, dt_bias=dt_bias, a_log=a_log, d_skip=d_skip,
             attn_norm_g=attn_norm_g, ssd_norm_g=ssd_norm_g, w_out=w_out, ln2_g=ln2_g, w_mlp_in=w_mlp_in,
             w_mlp_out=w_mlp_out, final_norm_g=final_norm_g)
    mom = dict(ln1_g=m_ln1_g, w_in=m_w_in, conv_w=m_conv_w, conv_b=m_conv_b, dt_bias=m_dt_bias, a_log=m_a_log,
               d_skip=m_d_skip, attn_norm_g=m_attn_norm_g, ssd_norm_g=m_ssd_norm_g, w_out=m_w_out, ln2_g=m_ln2_g,
               w_mlp_in=m_w_mlp_in, w_mlp_out=m_w_mlp_out, final_norm_g=m_final_norm_g)
    var = dict(ln1_g=v_ln1_g, w_in=v_w_in, conv_w=v_conv_w, conv_b=v_conv_b, dt_bias=v_dt_bias, a_log=v_a_log,
               d_skip=v_d_skip, attn_norm_g=v_attn_norm_g, ssd_norm_g=v_ssd_norm_g, w_out=v_w_out, ln2_g=v_ln2_g,
               w_mlp_in=v_w_mlp_in, w_mlp_out=v_w_mlp_out, final_norm_g=v_final_norm_g)

    depth, d_model = ln1_g.shape
    t_len = x.shape[1]
    w_attn = attn_norm_g.shape[1]
    h_ssd = dt_bias.shape[1]
    conv_ch = conv_b.shape[1]
    in_proj = w_in.shape[2] * N_DEV
    assert ssd_norm_g.shape[1] == w_attn and in_proj == 4 * w_attn + conv_ch + h_ssd
    assert t_len % (BRANCH_DILATIONS[-1] * ATTN_BLOCK) == 0 and h_ssd % (2 * SSD_GROUPS) == 0
    dims = dict(w_attn=w_attn, heads_g=h_ssd // SSD_GROUPS, n_heads=w_attn // HEAD_DIM, conv_ch=conv_ch)
    names = ["w_in", "w_out", "w_mlp_in", "w_mlp_out"]

    rides = _Rides()
    ex = dict(rides=rides, bufs={}, pair={})
    latest, sent = {}, {}

    def shard(n, l):
        return w[n][l].astype(BF16)

    def half(rows, part):
        return None if part is None else (part * (rows // 2), rows // 2)

    def plan_spread(host, n, l, part=None):
        key, prev = ("spread", host, n, l, part), latest.get((n, l))
        rides.put(host, key, lambda: _GatherSpread([shard(n, l)], rows=half(w[n].shape[1], part),
                                                   into=[rides.done[prev[0]][prev[1]]] if prev else None))
        latest[(n, l)] = (key, 0)

    def plan_pass(host, items):
        key, srcs = ("pass", host), [latest[it] for it in items]
        rides.put(host, key, lambda: _GatherPass([rides.done[k][i] for k, i in srcs]))
        for i, it in enumerate(items):
            latest[it] = (key, i)

    def plan_swap(host, items):
        rides.put(host, ("swap", host), lambda: _SiblingSwap([ex["bufs"][it] for it in items]))

    def plan_send(host, n, l, part=None):
        key, prev = ("send", host, n, l, part), sent.get((n, l))
        rides.put(host, key, lambda: _ChipSend([ex["pair"][(n, l)]], rows=half(ex["pair"][(n, l)].shape[1], part),
                                               into=[rides.done[prev[0]][prev[1]]] if prev else None))
        sent[(n, l)] = (key, 0)

    d_first, d_mid, d_last = (f"d{d}" for d in BRANCH_DILATIONS)
    for l in range(depth):
        t = f"_l{l}"
        if l == 0:
            plan_spread(f"in_proj{t}", "w_out", 0)
            plan_spread(f"in_proj{t}", "w_mlp_in", 0, 0)
            plan_spread(f"attn_fwd_{d_first}{t}", "w_mlp_in", 0, 1)
            plan_spread(f"attn_fwd_{d_mid}{t}", "w_mlp_out", 0, 0)
            plan_pass(f"attn_fwd_{d_mid}{t}", [("w_out", 0), ("w_mlp_in", 0)])
            plan_spread(f"attn_fwd_{d_last}{t}", "w_mlp_out", 0, 1)
            plan_pass(f"out_proj{t}", [("w_mlp_out", 0)])
        else:
            plan_spread(f"in_proj{t}", "w_mlp_out", l, 0)
            plan_spread(f"attn_fwd_{d_first}{t}", "w_mlp_out", l, 1)
            plan_pass(f"attn_fwd_{d_mid}{t}", [("w_mlp_out", l)])
        if l + 1 < depth:
            plan_spread(f"out_proj{t}", "w_out", l + 1)
            plan_spread(f"mlp_in{t}", "w_in", l + 1)
            plan_spread(f"mlp_out{t}", "w_mlp_in", l + 1)
            plan_pass(f"pass_weights_l{l + 1}", [("w_out", l + 1), ("w_in", l + 1), ("w_mlp_in", l + 1)])
        plan_swap(f"mlp_in_dx{t}", [("w_mlp_out", l), ("w_mlp_in", l)])
        plan_send(f"attn_bwd_{d_first}{t}", "w_mlp_out", l, 0)
        plan_swap(f"attn_bwd_{d_first}{t}", [("w_out", l)])
        plan_send(f"attn_bwd_{d_mid}{t}", "w_mlp_out", l, 1)
        plan_send(f"attn_bwd_{d_last}{t}", "w_mlp_in", l)
        plan_send(f"in_proj_dw{t}", "w_out", l)
        plan_swap(f"in_proj_dx{t}", [("w_in", l)])
        if l > 0:
            plan_send(f"mlp_out_dx_l{l - 1}", "w_in", l)
        else:
            plan_send("adamw_w_mlp_in", "w_in", l, 0)
            plan_send("adamw_w_mlp_out", "w_in", l, 1)

    g_in0, g_cw = _all_gather([shard("w_in", 0), conv_w], "gather_first")
    full_cw = g_cw.transpose(1, 2, 0, 3).reshape(depth, SSD_CONV, conv_ch)
    proj_cols = -(-in_proj // LANES) * LANES
    full = {}

    def weight(n, l):
        if (n, l) not in full:
            if (n, l) == ("w_in", 0):
                g = g_in0
            else:
                key, i = latest[(n, l)]
                g = rides.done[key][i]
            if n == "w_in":
                g = _pad_lanes(g.transpose(1, 0, 2).reshape(d_model, in_proj), proj_cols)
            elif n == "w_mlp_in":
                g = g.transpose(1, 0, 2).reshape(d_model, -1)
            else:
                g = g.reshape(-1, d_model)
            full[(n, l)] = g
        return full[(n, l)]

    layers = [dict(ln1_g=ln1_g[l][None, :], ln2_g=ln2_g[l][None, :], conv_w=full_cw[l], conv_b=conv_b[l][None, :],
                   dt_bias=dt_bias[l], a_log=a_log[l], d_skip=d_skip[l], attn_norm_g=attn_norm_g[l],
                   ssd_norm_g=ssd_norm_g[l]) for l in range(depth)]

    h = x[0]
    saved = []
    for l in range(depth):
        h, s = _layer_fwd(h, layers[l], functools.partial(lambda n, l: weight(n, l), l=l), dims, f"_l{l}", rides)
        saved.append(s)
        if l + 1 < depth:
            _alone(rides, f"pass_weights_l{l + 1}")
    dh, d_final_g, loss_part, dh_b = _loss_head(h, final_norm_g[None, :], loss_target[0], "loss_head")

    grads = [None] * depth
    for l in reversed(range(depth)):
        dh, dh_b, grads[l] = _layer_bwd(dh, dh_b, layers[l], functools.partial(lambda n, l: weight(n, l), l=l),
                                        saved[l], dims, l, ex, copy_dx0=l > 0)
    grad_x = dh[None]

    my_chip = (2 * lax.axis_index("x") + lax.axis_index("y")).astype(jnp.int32).reshape(1)
    gsum, delta, new_m, new_v = {}, {}, {}, {}
    for n in names[1:] + names[:1]:
        per_layer = []
        for l in range(depth):
            key, i = sent[(n, l)]
            per_layer.append(_sum_with_own(rides.done[key][i], ex["pair"][(n, l)], my_chip, f"sum_{n}_l{l}"))
        gsum[n] = jnp.stack(per_layer)
        delta[n], new_m[n], new_v[n] = _adamw(w[n], gsum[n], mom[n], var[n], f"adamw_{n}", rides)

    small_parts = [jnp.stack([grads[l][n] for l in range(depth)]).reshape(-1) for n in _SMALL]
    small_parts += [d_final_g.reshape(-1), loss_part[0, :1]]
    sizes = [int(a.shape[0]) for a in small_parts]
    packed = _to_rows(jnp.concatenate(small_parts))
    (gathered,) = _all_gather([packed], "gather_small_grads")
    total = _sum_leading(gathered, "sum_small_grads").reshape(-1)
    offs = np.cumsum([0] + sizes)
    pieces = [total[offs[i]:offs[i + 1]] for i in range(len(sizes))]
    for n, piece in zip(_SMALL, pieces):
        shape = (depth, SSD_CONV, conv_ch) if n == "conv_w" else w[n].shape
        gsum[n] = piece.reshape(shape)
    gsum["final_norm_g"] = pieces[len(_SMALL)]
    loss = pieces[len(_SMALL) + 1][0]
    my_id = 4 * lax.axis_index("x") + 2 * lax.axis_index("y") + lax.axis_index("c")
    cw = conv_w.shape[2]
    gsum["conv_w"] = lax.dynamic_slice_in_dim(gsum["conv_w"], my_id * cw, cw, axis=2)

    small_names = [n for n in _WEIGHTS if n not in names]
    sm_sizes = [int(np.prod(w[n].shape)) for n in small_names]
    pack = lambda d: _to_rows(jnp.concatenate([d[n].reshape(-1) for n in small_names]))[None]
    outs = _adamw(pack(w), pack(gsum), pack(mom), pack(var), "adamw_small")
    sm_offs = np.cumsum([0] + sm_sizes)
    for res, o in zip((delta, new_m, new_v), outs):
        flat = o.reshape(-1)
        for i, n in enumerate(small_names):
            res[n] = flat[sm_offs[i]:sm_offs[i + 1]].reshape(w[n].shape)

    return (loss, grad_x, *[gsum[n] for n in _WEIGHTS], *[delta[n] for n in _WEIGHTS],
            *[new_m[n] for n in _WEIGHTS], *[new_v[n] for n in _WEIGHTS])
```
